```python
import math
import jax, jax.numpy as jnp
from jax import lax
import numpy as np

D_MODEL = 2048
BATCH = 8
SEQ = 4096
DEPTH = 2

SSD_HEADS = 32
SSD_HEAD_DIM = 64
SSD_WIDTH = SSD_HEADS * SSD_HEAD_DIM
SSD_STATE = 128
SSD_GROUPS = 4
SSD_CONV = 4
SSD_CHUNK = 128
SSD_CONV_DIM = SSD_WIDTH + 2 * SSD_GROUPS * SSD_STATE
DT_MIN = 0.001
DT_MAX = 0.1

SB_HEADS = 16
SB_HEAD_DIM = 128
SB_WIDTH = SB_HEADS * SB_HEAD_DIM
SB_BLOCK = 128

IN_DIM = SSD_WIDTH + SSD_CONV_DIM + SSD_HEADS + 3 * SB_WIDTH
MIX_WIDTH = SSD_WIDTH + SB_WIDTH

POOL_WINDOWS = (2, 4, 8, 16)
POOL_GROUP = D_MODEL // len(POOL_WINDOWS)

D_FF = 4 * D_MODEL
EPS = 1e-6

kernel_name = 'hybrid_ssd_stickbreak_pool_trunk'


def rms_norm(x, g):
    xf = x.astype(jnp.float32)
    y = xf * lax.rsqrt(jnp.mean(xf * xf, axis=-1, keepdims=True) + EPS)
    return (y * g.astype(jnp.float32)).astype(x.dtype)


def causal_depthwise_conv(x, w, b):
    k, c = w.shape
    y = lax.conv_general_dilated(
        x, w[:, None, :].astype(x.dtype), window_strides=(1,),
        padding=[(k - 1, 0)], dimension_numbers=('NWC', 'WIO', 'NWC'),
        feature_group_count=c)
    return y + b.astype(x.dtype)


def ssd_scan(x, dt, a, bmat, cmat):
    f32 = jnp.float32
    bsz, t, h, p = x.shape
    g, n = bmat.shape[2], bmat.shape[3]
    r, l = h // g, SSD_CHUNK
    nc = t // l
    xdt = (x.astype(f32) * dt[..., None]).reshape(bsz, nc, l, g, r, p)
    da = (dt * a).reshape(bsz, nc, l, g, r).transpose(0, 3, 4, 1, 2)
    bc = bmat.astype(f32).reshape(bsz, nc, l, g, n)
    cc = cmat.astype(f32).reshape(bsz, nc, l, g, n)
    a_cs = jnp.cumsum(da, axis=-1)
    causal = jnp.tril(jnp.ones((l, l), dtype=bool))
    seg = a_cs[..., :, None] - a_cs[..., None, :]
    decay = jnp.exp(jnp.where(causal, seg, -jnp.inf))
    cb = jnp.einsum('bclgn,bcsgn->bgcls', cc, bc)
    y_diag = jnp.einsum('bgrcls,bcsgrp->bclgrp', decay * cb[:, :, None], xdt)
    decay_to_end = jnp.exp(a_cs[..., -1:] - a_cs)
    chunk_states = jnp.einsum('bcsgn,bgrcs,bcsgrp->bcgrpn', bc, decay_to_end, xdt)
    chunk_decay = jnp.exp(a_cs[..., -1])

    def step(state, inp):
        s_c, d_c = inp
        return state * d_c[..., None, None] + s_c, state

    init = jnp.zeros((bsz, g, r, p, n), f32)
    _, prev = lax.scan(step, init, (jnp.moveaxis(chunk_states, 1, 0),
                                    jnp.moveaxis(chunk_decay, -1, 0)))
    y_off = jnp.einsum('bclgn,cbgrpn->bclgrp', cc, prev) * \
        jnp.exp(a_cs).transpose(0, 3, 4, 1, 2)[..., None]
    return (y_diag + y_off).reshape(bsz, t, h, p)


def stick_breaking_attention(q, k, v):
    bsz, h, t, d = q.shape
    scale = d ** -0.5
    outs = []
    for i in range(t // SB_BLOCK):
        q0 = i * SB_BLOCK
        end = q0 + SB_BLOCK
        z = jnp.einsum('bhqd,bhkd->bhqk', q[:, :, q0:end], k[:, :, :end]).astype(jnp.float32) * scale
        qpos = q0 + jnp.arange(SB_BLOCK)[:, None]
        kpos = jnp.arange(end)[None, :]
        mask = kpos < qpos
        log_beta = jax.nn.log_sigmoid(z)
        log_keep = jnp.where(mask, jax.nn.log_sigmoid(-z), 0.0)
        between = lax.cumsum(log_keep, axis=3, reverse=True) - log_keep
        w = jnp.where(mask, jnp.exp(log_beta + between), 0.0)
        outs.append(jnp.einsum('bhqk,bhkd->bhqd', w.astype(v.dtype), v[:, :, :end]))
    return jnp.concatenate(outs, axis=2)


def hybrid_mixer(h, w_in, conv_w, conv_b, dt_bias, a_log, d_skip, out_norm, q_norm, k_norm, w_out):
    bsz, t, _ = h.shape
    proj = h @ w_in
    cuts = [SSD_WIDTH, SSD_WIDTH + SSD_CONV_DIM, SSD_WIDTH + SSD_CONV_DIM + SSD_HEADS,
            SSD_WIDTH + SSD_CONV_DIM + SSD_HEADS + SB_WIDTH,
            SSD_WIDTH + SSD_CONV_DIM + SSD_HEADS + 2 * SB_WIDTH]
    z, xbc, dt_raw, q, k, v = jnp.split(proj, cuts, axis=-1)

    xbc = jax.nn.silu(causal_depthwise_conv(xbc, conv_w, conv_b))
    xs, bm, cm = jnp.split(xbc, [SSD_WIDTH, SSD_WIDTH + SSD_GROUPS * SSD_STATE], axis=-1)
    xs = xs.reshape(bsz, t, SSD_HEADS, SSD_HEAD_DIM)
    bm = bm.reshape(bsz, t, SSD_GROUPS, SSD_STATE)
    cm = cm.reshape(bsz, t, SSD_GROUPS, SSD_STATE)
    dt = jax.nn.softplus(dt_raw.astype(jnp.float32) + dt_bias.astype(jnp.float32))
    a = -jnp.exp(a_log.astype(jnp.float32))
    y = ssd_scan(xs, dt, a, bm, cm) + d_skip.astype(jnp.float32)[:, None] * xs.astype(jnp.float32)
    gated = y.reshape(bsz, t, SSD_WIDTH) * jax.nn.silu(z.astype(jnp.float32))
    gsz = SSD_WIDTH // SSD_GROUPS
    y_ssd = rms_norm(gated.reshape(bsz, t, SSD_GROUPS, gsz),
                     out_norm.reshape(SSD_GROUPS, gsz)).reshape(bsz, t, SSD_WIDTH)

    def heads(u):
        return u.reshape(bsz, t, SB_HEADS, SB_HEAD_DIM).transpose(0, 2, 1, 3)
    qh = rms_norm(heads(q), q_norm)
    kh = rms_norm(heads(k), k_norm)
    y_sb = stick_breaking_attention(qh, kh, heads(v))
    y_sb = y_sb.transpose(0, 2, 1, 3).reshape(bsz, t, SB_WIDTH)

    merged = jnp.concatenate([y_ssd.astype(h.dtype), y_sb.astype(h.dtype)], axis=-1)
    return merged @ w_out


def multiscale_pool(h, w, b, scale):
    bsz, t, _ = h.shape
    hf = h.astype(jnp.float32)
    cs = jnp.cumsum(hf, axis=1)
    count = jnp.arange(1, t + 1, dtype=jnp.float32)[:, None]
    diffs = []
    for gi, win in enumerate(POOL_WINDOWS):
        sl = slice(gi * POOL_GROUP, (gi + 1) * POOL_GROUP)
        c = cs[..., sl]
        lagged = jnp.pad(c, ((0, 0), (win, 0), (0, 0)))[:, :t]
        mean = (c - lagged) / jnp.minimum(count, float(win))
        diffs.append(mean - hf[..., sl])
    d = jnp.stack(diffs, axis=2).astype(h.dtype)
    y = jnp.einsum('btgc,gcd->btgd', d, w).reshape(bsz, t, D_MODEL) + b
    return y * scale


def sq_relu_mlp(h, w_up, w_down):
    u = jax.nn.relu(h @ w_up)
    return (u * u) @ w_down


def _fwd_setup_inputs(seed: int = 0) -> dict:
    key = jax.random.key(seed)
    ks = jax.random.split(key, 20)
    ne = (DEPTH + 1) // 2
    no = DEPTH // 2
    f32 = jnp.float32

    def normal(k, shape, s):
        return jax.random.normal(k, shape, f32) * s

    def gain(k, shape):
        return 1.0 + 0.02 * jax.random.normal(k, shape, f32)

    dt0 = jnp.exp(jax.random.uniform(ks[5], (ne, SSD_HEADS), f32, math.log(DT_MIN), math.log(DT_MAX)))
    return {
        'x': normal(ks[0], (BATCH, SEQ, D_MODEL), 1.0),
        'hyb_norm': gain(ks[1], (ne, D_MODEL)),
        'hyb_w_in': normal(ks[2], (ne, D_MODEL, IN_DIM), D_MODEL ** -0.5),
        'ssd_conv_w': normal(ks[3], (ne, SSD_CONV, SSD_CONV_DIM), SSD_CONV ** -0.5),
        'ssd_conv_b': normal(ks[4], (ne, SSD_CONV_DIM), 0.02),
        'ssd_dt_bias': dt0 + jnp.log(-jnp.expm1(-dt0)),
        'ssd_a_log': jnp.log(jax.random.uniform(ks[6], (ne, SSD_HEADS), f32, 1.0, 16.0)),
        'ssd_d': 1.0 + 0.1 * jax.random.normal(ks[7], (ne, SSD_HEADS), f32),
        'ssd_out_norm': gain(ks[8], (ne, SSD_WIDTH)),
        'sb_q_norm': gain(ks[9], (ne, SB_HEAD_DIM)),
        'sb_k_norm': gain(ks[10], (ne, SB_HEAD_DIM)),
        'hyb_w_out': normal(ks[11], (ne, MIX_WIDTH, D_MODEL), MIX_WIDTH ** -0.5),
        'pool_norm': gain(ks[12], (no, D_MODEL)),
        'pool_w': normal(ks[13], (no, len(POOL_WINDOWS), POOL_GROUP, POOL_GROUP), POOL_GROUP ** -0.5),
        'pool_b': normal(ks[14], (no, D_MODEL), 0.02),
        'pool_scale': gain(ks[15], (no, D_MODEL)),
        'mlp_norm': gain(ks[16], (DEPTH, D_MODEL)),
        'mlp_w_up': normal(ks[17], (DEPTH, D_MODEL, D_FF), D_MODEL ** -0.5),
        'mlp_w_down': normal(ks[18], (DEPTH, D_FF, D_MODEL), D_FF ** -0.5),
    }


def _fwd_reference(x, hyb_norm, hyb_w_in, ssd_conv_w, ssd_conv_b, ssd_dt_bias, ssd_a_log, ssd_d,
              ssd_out_norm, sb_q_norm, sb_k_norm, hyb_w_out, pool_norm, pool_w, pool_b,
              pool_scale, mlp_norm, mlp_w_up, mlp_w_down):
    for layer in range(DEPTH):
        i = layer // 2
        if layer % 2 == 0:
            mix = hybrid_mixer(rms_norm(x, hyb_norm[i]), hyb_w_in[i], ssd_conv_w[i], ssd_conv_b[i],
                               ssd_dt_bias[i], ssd_a_log[i], ssd_d[i], ssd_out_norm[i],
                               sb_q_norm[i], sb_k_norm[i], hyb_w_out[i])
        else:
            mix = multiscale_pool(rms_norm(x, pool_norm[i]), pool_w[i], pool_b[i], pool_scale[i])
        x = x + mix.astype(x.dtype)
        x = x + sq_relu_mlp(rms_norm(x, mlp_norm[layer]), mlp_w_up[layer], mlp_w_down[layer]).astype(x.dtype)
    return x


import jax as _jax
import jax.numpy as _jnp

TWIN_FORMAT = 'train_step'
FWD_PARAMS = ['x', 'hyb_norm', 'hyb_w_in', 'ssd_conv_w', 'ssd_conv_b', 'ssd_dt_bias', 'ssd_a_log', 'ssd_d', 'ssd_out_norm', 'sb_q_norm', 'sb_k_norm', 'hyb_w_out', 'pool_norm', 'pool_w', 'pool_b', 'pool_scale', 'mlp_norm', 'mlp_w_up', 'mlp_w_down']
TWIN_WEIGHTS = ['hyb_norm', 'hyb_w_in', 'ssd_conv_w', 'ssd_conv_b', 'ssd_dt_bias', 'ssd_a_log', 'ssd_d', 'ssd_out_norm', 'sb_q_norm', 'sb_k_norm', 'hyb_w_out', 'pool_norm', 'pool_w', 'pool_b', 'pool_scale', 'mlp_norm', 'mlp_w_up', 'mlp_w_down']
TWIN_DIFF_INPUT = 'x'
TWIN_INPUTS = ['x', 'hyb_norm', 'hyb_w_in', 'ssd_conv_w', 'ssd_conv_b', 'ssd_dt_bias', 'ssd_a_log', 'ssd_d', 'ssd_out_norm', 'sb_q_norm', 'sb_k_norm', 'hyb_w_out', 'pool_norm', 'pool_w', 'pool_b', 'pool_scale', 'mlp_norm', 'mlp_w_up', 'mlp_w_down', 'loss_target', 'm_hyb_norm', 'm_hyb_w_in', 'm_ssd_conv_w', 'm_ssd_conv_b', 'm_ssd_dt_bias', 'm_ssd_a_log', 'm_ssd_d', 'm_ssd_out_norm', 'm_sb_q_norm', 'm_sb_k_norm', 'm_hyb_w_out', 'm_pool_norm', 'm_pool_w', 'm_pool_b', 'm_pool_scale', 'm_mlp_norm', 'm_mlp_w_up', 'm_mlp_w_down', 'v_hyb_norm', 'v_hyb_w_in', 'v_ssd_conv_w', 'v_ssd_conv_b', 'v_ssd_dt_bias', 'v_ssd_a_log', 'v_ssd_d', 'v_ssd_out_norm', 'v_sb_q_norm', 'v_sb_k_norm', 'v_hyb_w_out', 'v_pool_norm', 'v_pool_w', 'v_pool_b', 'v_pool_scale', 'v_mlp_norm', 'v_mlp_w_up', 'v_mlp_w_down']
TWIN_OUTPUTS = ['loss', 'grad_x', 'grad_hyb_norm', 'grad_hyb_w_in', 'grad_ssd_conv_w', 'grad_ssd_conv_b', 'grad_ssd_dt_bias', 'grad_ssd_a_log', 'grad_ssd_d', 'grad_ssd_out_norm', 'grad_sb_q_norm', 'grad_sb_k_norm', 'grad_hyb_w_out', 'grad_pool_norm', 'grad_pool_w', 'grad_pool_b', 'grad_pool_scale', 'grad_mlp_norm', 'grad_mlp_w_up', 'grad_mlp_w_down', 'delta_hyb_norm', 'delta_hyb_w_in', 'delta_ssd_conv_w', 'delta_ssd_conv_b', 'delta_ssd_dt_bias', 'delta_ssd_a_log', 'delta_ssd_d', 'delta_ssd_out_norm', 'delta_sb_q_norm', 'delta_sb_k_norm', 'delta_hyb_w_out', 'delta_pool_norm', 'delta_pool_w', 'delta_pool_b', 'delta_pool_scale', 'delta_mlp_norm', 'delta_mlp_w_up', 'delta_mlp_w_down', 'new_m_hyb_norm', 'new_m_hyb_w_in', 'new_m_ssd_conv_w', 'new_m_ssd_conv_b', 'new_m_ssd_dt_bias', 'new_m_ssd_a_log', 'new_m_ssd_d', 'new_m_ssd_out_norm', 'new_m_sb_q_norm', 'new_m_sb_k_norm', 'new_m_hyb_w_out', 'new_m_pool_norm', 'new_m_pool_w', 'new_m_pool_b', 'new_m_pool_scale', 'new_m_mlp_norm', 'new_m_mlp_w_up', 'new_m_mlp_w_down', 'new_v_hyb_norm', 'new_v_hyb_w_in', 'new_v_ssd_conv_w', 'new_v_ssd_conv_b', 'new_v_ssd_dt_bias', 'new_v_ssd_a_log', 'new_v_ssd_d', 'new_v_ssd_out_norm', 'new_v_sb_q_norm', 'new_v_sb_k_norm', 'new_v_hyb_w_out', 'new_v_pool_norm', 'new_v_pool_w', 'new_v_pool_b', 'new_v_pool_scale', 'new_v_mlp_norm', 'new_v_mlp_w_up', 'new_v_mlp_w_down']
TWIN_LEAF_KINDS = {'loss': 'loss', 'grad_x': 'grad_x', 'grad_hyb_norm': 'grad_w', 'grad_hyb_w_in': 'grad_w', 'grad_ssd_conv_w': 'grad_w', 'grad_ssd_conv_b': 'grad_w', 'grad_ssd_dt_bias': 'grad_w', 'grad_ssd_a_log': 'grad_w', 'grad_ssd_d': 'grad_w', 'grad_ssd_out_norm': 'grad_w', 'grad_sb_q_norm': 'grad_w', 'grad_sb_k_norm': 'grad_w', 'grad_hyb_w_out': 'grad_w', 'grad_pool_norm': 'grad_w', 'grad_pool_w': 'grad_w', 'grad_pool_b': 'grad_w', 'grad_pool_scale': 'grad_w', 'grad_mlp_norm': 'grad_w', 'grad_mlp_w_up': 'grad_w', 'grad_mlp_w_down': 'grad_w', 'delta_hyb_norm': 'delta_w', 'delta_hyb_w_in': 'delta_w', 'delta_ssd_conv_w': 'delta_w', 'delta_ssd_conv_b': 'delta_w', 'delta_ssd_dt_bias': 'delta_w', 'delta_ssd_a_log': 'delta_w', 'delta_ssd_d': 'delta_w', 'delta_ssd_out_norm': 'delta_w', 'delta_sb_q_norm': 'delta_w', 'delta_sb_k_norm': 'delta_w', 'delta_hyb_w_out': 'delta_w', 'delta_pool_norm': 'delta_w', 'delta_pool_w': 'delta_w', 'delta_pool_b': 'delta_w', 'delta_pool_scale': 'delta_w', 'delta_mlp_norm': 'delta_w', 'delta_mlp_w_up': 'delta_w', 'delta_mlp_w_down': 'delta_w', 'new_m_hyb_norm': 'new_m', 'new_m_hyb_w_in': 'new_m', 'new_m_ssd_conv_w': 'new_m', 'new_m_ssd_conv_b': 'new_m', 'new_m_ssd_dt_bias': 'new_m', 'new_m_ssd_a_log': 'new_m', 'new_m_ssd_d': 'new_m', 'new_m_ssd_out_norm': 'new_m', 'new_m_sb_q_norm': 'new_m', 'new_m_sb_k_norm': 'new_m', 'new_m_hyb_w_out': 'new_m', 'new_m_pool_norm': 'new_m', 'new_m_pool_w': 'new_m', 'new_m_pool_b': 'new_m', 'new_m_pool_scale': 'new_m', 'new_m_mlp_norm': 'new_m', 'new_m_mlp_w_up': 'new_m', 'new_m_mlp_w_down': 'new_m', 'new_v_hyb_norm': 'new_v', 'new_v_hyb_w_in': 'new_v', 'new_v_ssd_conv_w': 'new_v', 'new_v_ssd_conv_b': 'new_v', 'new_v_ssd_dt_bias': 'new_v', 'new_v_ssd_a_log': 'new_v', 'new_v_ssd_d': 'new_v', 'new_v_ssd_out_norm': 'new_v', 'new_v_sb_q_norm': 'new_v', 'new_v_sb_k_norm': 'new_v', 'new_v_hyb_w_out': 'new_v', 'new_v_pool_norm': 'new_v', 'new_v_pool_w': 'new_v', 'new_v_pool_b': 'new_v', 'new_v_pool_scale': 'new_v', 'new_v_mlp_norm': 'new_v', 'new_v_mlp_w_up': 'new_v', 'new_v_mlp_w_down': 'new_v'}


def _forward(args):
    return _fwd_reference(*[args[k] for k in FWD_PARAMS])


def _output_shape():
    def fwd():
        inp = _fwd_setup_inputs(0)
        return _fwd_reference(*[inp[k] for k in FWD_PARAMS])
    out = _jax.eval_shape(fwd)
    return out.shape, out.dtype

N_MICROBATCH = 1
ADAM_LR = 0.001
ADAM_B1 = 0.9
ADAM_B2 = 0.999
ADAM_EPS = 1e-08
ADAM_WD = 0.01
ADAM_STEP = 10
PER_EXAMPLE_BATCH_AXIS = {'x': 0, 'loss_target': 0}
SHARED_INPUTS = []
_WEIGHT_DTYPES = {'hyb_norm': _jnp.float32, 'hyb_w_in': _jnp.float32, 'ssd_conv_w': _jnp.float32, 'ssd_conv_b': _jnp.float32, 'ssd_dt_bias': _jnp.float32, 'ssd_a_log': _jnp.float32, 'ssd_d': _jnp.float32, 'ssd_out_norm': _jnp.float32, 'sb_q_norm': _jnp.float32, 'sb_k_norm': _jnp.float32, 'hyb_w_out': _jnp.float32, 'pool_norm': _jnp.float32, 'pool_w': _jnp.float32, 'pool_b': _jnp.float32, 'pool_scale': _jnp.float32, 'mlp_norm': _jnp.float32, 'mlp_w_up': _jnp.float32, 'mlp_w_down': _jnp.float32}
MOMENT_SCALE = {'hyb_norm': 2.806498e+00, 'hyb_w_in': 2.905702e-01, 'ssd_conv_w': 1.411325e+00, 'ssd_conv_b': 4.799554e+00, 'ssd_dt_bias': 1.682187e+00, 'ssd_a_log': 6.100573e+00, 'ssd_d': 7.978319e+00, 'ssd_out_norm': 1.116224e+01, 'sb_q_norm': 8.212319e+00, 'sb_k_norm': 8.223153e+00, 'hyb_w_out': 2.558844e+00, 'pool_norm': 1.134300e+01, 'pool_w': 1.081434e+00, 'pool_b': 1.493185e+01, 'pool_scale': 1.136874e+01, 'mlp_norm': 4.746387e+01, 'mlp_w_up': 1.743580e+00, 'mlp_w_down': 6.813704e+00}


def _to_microbatches(a, axis):
    t = _jnp.moveaxis(a, axis, 0)
    t = t.reshape((N_MICROBATCH, t.shape[0] // N_MICROBATCH) + t.shape[1:])
    return _jnp.moveaxis(t, 1, axis + 1)


def setup_inputs(seed: int = 0) -> dict:
    inp = _fwd_setup_inputs(seed)
    key = _jax.random.fold_in(_jax.random.key(seed), 7919)
    shape, _ = _output_shape()
    out = dict(inp)
    out["loss_target"] = _jax.random.normal(_jax.random.fold_in(key, 0), shape, _jnp.float32)
    for i, name in enumerate(TWIN_WEIGHTS):
        w = inp[name].astype(_jnp.float32)
        if MOMENT_SCALE is None:
            s = _jnp.sqrt(_jnp.mean(_jnp.square(w)) + 1e-30)
        else:
            s = MOMENT_SCALE[name]
        km, kv = _jax.random.split(_jax.random.fold_in(key, i + 1))
        out[name] = w
        out["m_" + name] = s * _jax.random.normal(km, w.shape, _jnp.float32)
        out["v_" + name] = (s * s) * _jax.random.uniform(kv, w.shape, _jnp.float32, 0.5, 1.5)
    if N_MICROBATCH > 1:
        for name, axis in PER_EXAMPLE_BATCH_AXIS.items():
            out[name] = _to_microbatches(out[name], axis)
    return {'x': out['x'], 'hyb_norm': out['hyb_norm'], 'hyb_w_in': out['hyb_w_in'], 'ssd_conv_w': out['ssd_conv_w'], 'ssd_conv_b': out['ssd_conv_b'], 'ssd_dt_bias': out['ssd_dt_bias'], 'ssd_a_log': out['ssd_a_log'], 'ssd_d': out['ssd_d'], 'ssd_out_norm': out['ssd_out_norm'], 'sb_q_norm': out['sb_q_norm'], 'sb_k_norm': out['sb_k_norm'], 'hyb_w_out': out['hyb_w_out'], 'pool_norm': out['pool_norm'], 'pool_w': out['pool_w'], 'pool_b': out['pool_b'], 'pool_scale': out['pool_scale'], 'mlp_norm': out['mlp_norm'], 'mlp_w_up': out['mlp_w_up'], 'mlp_w_down': out['mlp_w_down'], 'loss_target': out['loss_target'], 'm_hyb_norm': out['m_hyb_norm'], 'm_hyb_w_in': out['m_hyb_w_in'], 'm_ssd_conv_w': out['m_ssd_conv_w'], 'm_ssd_conv_b': out['m_ssd_conv_b'], 'm_ssd_dt_bias': out['m_ssd_dt_bias'], 'm_ssd_a_log': out['m_ssd_a_log'], 'm_ssd_d': out['m_ssd_d'], 'm_ssd_out_norm': out['m_ssd_out_norm'], 'm_sb_q_norm': out['m_sb_q_norm'], 'm_sb_k_norm': out['m_sb_k_norm'], 'm_hyb_w_out': out['m_hyb_w_out'], 'm_pool_norm': out['m_pool_norm'], 'm_pool_w': out['m_pool_w'], 'm_pool_b': out['m_pool_b'], 'm_pool_scale': out['m_pool_scale'], 'm_mlp_norm': out['m_mlp_norm'], 'm_mlp_w_up': out['m_mlp_w_up'], 'm_mlp_w_down': out['m_mlp_w_down'], 'v_hyb_norm': out['v_hyb_norm'], 'v_hyb_w_in': out['v_hyb_w_in'], 'v_ssd_conv_w': out['v_ssd_conv_w'], 'v_ssd_conv_b': out['v_ssd_conv_b'], 'v_ssd_dt_bias': out['v_ssd_dt_bias'], 'v_ssd_a_log': out['v_ssd_a_log'], 'v_ssd_d': out['v_ssd_d'], 'v_ssd_out_norm': out['v_ssd_out_norm'], 'v_sb_q_norm': out['v_sb_q_norm'], 'v_sb_k_norm': out['v_sb_k_norm'], 'v_hyb_w_out': out['v_hyb_w_out'], 'v_pool_norm': out['v_pool_norm'], 'v_pool_w': out['v_pool_w'], 'v_pool_b': out['v_pool_b'], 'v_pool_scale': out['v_pool_scale'], 'v_mlp_norm': out['v_mlp_norm'], 'v_mlp_w_up': out['v_mlp_w_up'], 'v_mlp_w_down': out['v_mlp_w_down']}


def _loss(weights, diff, rest, loss_target):
    with _jax.named_scope("forward"):
        args = {**rest, TWIN_DIFF_INPUT: diff, **{k: w.astype(_WEIGHT_DTYPES[k]) for k, w in weights.items()}}
        y = _forward(args)
    with _jax.named_scope("loss_head"):
        err = _jnp.square(y.astype(_jnp.float32) - loss_target)
        return 0.5 * _jnp.sum(_jnp.mean(err, axis=-1)) if err.ndim else 0.5 * err


def _adamw(w, g, m, v):
    m = ADAM_B1 * m + (1.0 - ADAM_B1) * g
    v = ADAM_B2 * v + (1.0 - ADAM_B2) * _jnp.square(g)
    m_hat = m / (1.0 - ADAM_B1 ** ADAM_STEP)
    v_hat = v / (1.0 - ADAM_B2 ** ADAM_STEP)
    delta = -ADAM_LR * (m_hat / (_jnp.sqrt(v_hat) + ADAM_EPS) + ADAM_WD * w)
    return delta, m, v


def reference(x, hyb_norm, hyb_w_in, ssd_conv_w, ssd_conv_b, ssd_dt_bias, ssd_a_log, ssd_d, ssd_out_norm, sb_q_norm, sb_k_norm, hyb_w_out, pool_norm, pool_w, pool_b, pool_scale, mlp_norm, mlp_w_up, mlp_w_down, loss_target, m_hyb_norm, m_hyb_w_in, m_ssd_conv_w, m_ssd_conv_b, m_ssd_dt_bias, m_ssd_a_log, m_ssd_d, m_ssd_out_norm, m_sb_q_norm, m_sb_k_norm, m_hyb_w_out, m_pool_norm, m_pool_w, m_pool_b, m_pool_scale, m_mlp_norm, m_mlp_w_up, m_mlp_w_down, v_hyb_norm, v_hyb_w_in, v_ssd_conv_w, v_ssd_conv_b, v_ssd_dt_bias, v_ssd_a_log, v_ssd_d, v_ssd_out_norm, v_sb_q_norm, v_sb_k_norm, v_hyb_w_out, v_pool_norm, v_pool_w, v_pool_b, v_pool_scale, v_mlp_norm, v_mlp_w_up, v_mlp_w_down):
    given = dict(x=x, hyb_norm=hyb_norm, hyb_w_in=hyb_w_in, ssd_conv_w=ssd_conv_w, ssd_conv_b=ssd_conv_b, ssd_dt_bias=ssd_dt_bias, ssd_a_log=ssd_a_log, ssd_d=ssd_d, ssd_out_norm=ssd_out_norm, sb_q_norm=sb_q_norm, sb_k_norm=sb_k_norm, hyb_w_out=hyb_w_out, pool_norm=pool_norm, pool_w=pool_w, pool_b=pool_b, pool_scale=pool_scale, mlp_norm=mlp_norm, mlp_w_up=mlp_w_up, mlp_w_down=mlp_w_down, loss_target=loss_target, m_hyb_norm=m_hyb_norm, m_hyb_w_in=m_hyb_w_in, m_ssd_conv_w=m_ssd_conv_w, m_ssd_conv_b=m_ssd_conv_b, m_ssd_dt_bias=m_ssd_dt_bias, m_ssd_a_log=m_ssd_a_log, m_ssd_d=m_ssd_d, m_ssd_out_norm=m_ssd_out_norm, m_sb_q_norm=m_sb_q_norm, m_sb_k_norm=m_sb_k_norm, m_hyb_w_out=m_hyb_w_out, m_pool_norm=m_pool_norm, m_pool_w=m_pool_w, m_pool_b=m_pool_b, m_pool_scale=m_pool_scale, m_mlp_norm=m_mlp_norm, m_mlp_w_up=m_mlp_w_up, m_mlp_w_down=m_mlp_w_down, v_hyb_norm=v_hyb_norm, v_hyb_w_in=v_hyb_w_in, v_ssd_conv_w=v_ssd_conv_w, v_ssd_conv_b=v_ssd_conv_b, v_ssd_dt_bias=v_ssd_dt_bias, v_ssd_a_log=v_ssd_a_log, v_ssd_d=v_ssd_d, v_ssd_out_norm=v_ssd_out_norm, v_sb_q_norm=v_sb_q_norm, v_sb_k_norm=v_sb_k_norm, v_hyb_w_out=v_hyb_w_out, v_pool_norm=v_pool_norm, v_pool_w=v_pool_w, v_pool_b=v_pool_b, v_pool_scale=v_pool_scale, v_mlp_norm=v_mlp_norm, v_mlp_w_up=v_mlp_w_up, v_mlp_w_down=v_mlp_w_down)
    weights = {n: given[n] for n in TWIN_WEIGHTS}
    shared = {n: given[n] for n in SHARED_INPUTS}
    per_example = {n: given[n] for n in ['x']}
    grad_fn = _jax.value_and_grad(_loss, argnums=(0, 1))

    def one_microbatch(ex, loss_target):
        ex = dict(ex)
        diff = ex.pop(TWIN_DIFF_INPUT)
        return grad_fn(weights, diff, {**shared, **ex}, loss_target)

    if N_MICROBATCH == 1:
        loss, (grad_w, grad_x) = one_microbatch(per_example, given["loss_target"])
    else:
        def body(carry, xs):
            loss_sum, grad_sum = carry
            l_k, (gw_k, gx_k) = one_microbatch(xs[0], xs[1])
            with _jax.named_scope("update"):
                return (loss_sum + l_k, _jax.tree.map(_jnp.add, grad_sum, gw_k)), gx_k

        init = (_jnp.zeros((), _jnp.float32), _jax.tree.map(_jnp.zeros_like, weights))
        (loss, grad_w), grad_x = _jax.lax.scan(body, init, (per_example, given["loss_target"]))
    with _jax.named_scope("update"):
        delta_w, new_m, new_v = {}, {}, {}
        for n in TWIN_WEIGHTS:
            delta_w[n], new_m[n], new_v[n] = _adamw(weights[n], grad_w[n], given["m_" + n], given["v_" + n])
    return (loss, grad_x, *[grad_w[n] for n in TWIN_WEIGHTS], *[delta_w[n] for n in TWIN_WEIGHTS],
            *[new_m[n] for n in TWIN_WEIGHTS], *[new_v[n] for n in TWIN_WEIGHTS])
```

```python
import functools
import math

import jax
import jax.numpy as jnp
from jax import lax
from jax.experimental import pallas as pl
from jax.experimental.pallas import tpu as pltpu

f32 = jnp.float32
bf16 = jnp.bfloat16

EPS = 1e-6
SSD_HEAD_DIM = 64
SSD_STATE = 128
SSD_GROUPS = 4
SSD_CHUNK = 128
LANES = 128
SB_HEAD_DIM = 128
POOL_WINDOWS = (2, 4, 8, 16)
POOL_HALO = 16
CONV_HALO = 8
ADAM_LR, ADAM_B1, ADAM_B2, ADAM_EPS, ADAM_WD, ADAM_STEP = 0.001, 0.9, 0.999, 1e-08, 0.01, 10
VMEM_LIMIT = 56 * 1024 * 1024
N_CHIPS = 4
MESH = pl.DeviceIdType.MESH

_DIMS = {"nn": (((1,), (0,)), ((), ())), "nt": (((1,), (1,)), ((), ())), "tn": (((0,), (0,)), ((), ()))}


def _fit(n, t):
    if n <= t:
        return n
    return max(d for d in range(LANES, t + 1, LANES) if n % d == 0)


def _cp(*sem):
    return pltpu.CompilerParams(dimension_semantics=sem, vmem_limit_bytes=VMEM_LIMIT)


def _sigmoid(v):
    return 1.0 / (1.0 + jnp.exp(-v))


def _softplus(v):
    return jnp.maximum(v, 0.0) + jnp.log(1.0 + jnp.exp(-jnp.abs(v)))


def _split(v, parts):
    out, rem = [], v
    for _ in range(parts):
        p = rem.astype(bf16)
        out.append(p)
        rem = rem - p.astype(f32)
    return out


def _dot(a, b, mode="nn"):
    return lax.dot_general(a, b, _DIMS[mode], preferred_element_type=f32)


def _mask_dot(mask_b, v, parts, mode="nn"):
    acc = None
    for p in _split(v, parts):
        t = _dot(mask_b, p, mode)
        acc = t if acc is None else acc + t
    return acc


def _dot_mask(v, mask_b, parts):
    acc = None
    for p in _split(v, parts):
        t = _dot(p, mask_b)
        acc = t if acc is None else acc + t
    return acc


def _mm(a, b, mode, M, N, K, outs, name, epilogue=None, extras=(), a_off=(0, 0), b_off=(0, 0),
        tm=1024, tn=1024, tk=1024):
    tm, tn, tk = _fit(M, tm), _fit(N, tn), _fit(K, tk)
    nk = K // tk
    if mode == "tn":
        a_blk, ad = (tk, tm), (tk, tm)
    else:
        a_blk, ad = (tm, tk), (tm, tk)
    b_blk = (tn, tk) if mode == "nt" else (tk, tn)
    assert a_off[0] % ad[0] == 0 and a_off[1] % ad[1] == 0 and b_off[0] % b_blk[0] == 0 and b_off[1] % b_blk[1] == 0
    ao = (a_off[0] // ad[0], a_off[1] // ad[1])
    bo = (b_off[0] // b_blk[0], b_off[1] // b_blk[1])
    if mode == "tn":
        a_map = lambda i, j, k: (k + ao[0], i + ao[1])
    else:
        a_map = lambda i, j, k: (i + ao[0], k + ao[1])
    if mode == "nt":
        b_map = lambda i, j, k: (j + bo[0], k + bo[1])
    else:
        b_map = lambda i, j, k: (k + bo[0], j + bo[1])
    in_specs = [pl.BlockSpec(a_blk, a_map), pl.BlockSpec(b_blk, b_map)]
    for arr, kind in extras:
        if kind == "tile":
            in_specs.append(pl.BlockSpec((tm, tn), lambda i, j, k: (i, j)))
        else:
            in_specs.append(pl.BlockSpec((1, tn), lambda i, j, k: (0, j)))
    ne, no = len(extras), len(outs)
    if epilogue is None:
        epilogue = lambda acc: (acc,)

    def body(a_ref, b_ref, *rest):
        ex, out_refs = rest[:ne], rest[ne:ne + no]

        def finish(acc):
            res = epilogue(acc, *[e[...] for e in ex])
            for o, r in zip(out_refs, res):
                o[...] = r.astype(o.dtype)

        prod = lax.dot_general(a_ref[...].astype(bf16), b_ref[...].astype(bf16), _DIMS[mode],
                               preferred_element_type=f32)
        if nk == 1:
            finish(prod)
        else:
            acc_ref = rest[-1]
            k = pl.program_id(2)

            @pl.when(k == 0)
            def _():
                acc_ref[...] = prod

            @pl.when(k > 0)
            def _():
                acc_ref[...] += prod

            @pl.when(k == nk - 1)
            def _():
                finish(acc_ref[...])

    res = pl.pallas_call(
        body, name=name, grid=(M // tm, N // tn, nk), in_specs=in_specs,
        out_specs=[pl.BlockSpec((tm, tn), lambda i, j, k: (i, j)) for _ in outs],
        out_shape=[jax.ShapeDtypeStruct((M, N), dt) for dt in outs],
        scratch_shapes=[pltpu.VMEM((tm, tn), f32)] if nk > 1 else [],
        compiler_params=_cp("parallel", "parallel", "arbitrary"),
    )(a, b, *[e[0] for e in extras])
    return res[0] if no == 1 else res


def _rms_fwd(x, g, out_dtype, name, tr=256):
    T, D = x.shape
    tr = min(tr, T)

    def body(x_ref, g_ref, o_ref):
        xv = x_ref[...]
        r = lax.rsqrt(jnp.mean(xv * xv, axis=-1, keepdims=True) + EPS)
        o_ref[...] = (xv * r * g_ref[...]).astype(o_ref.dtype)

    return pl.pallas_call(
        body, name=name, grid=(T // tr,),
        in_specs=[pl.BlockSpec((tr, D), lambda r: (r, 0)), pl.BlockSpec((1, D), lambda r: (0, 0))],
        out_specs=pl.BlockSpec((tr, D), lambda r: (r, 0)),
        out_shape=jax.ShapeDtypeStruct((T, D), out_dtype), compiler_params=_cp("parallel"),
    )(x, g)


def _rms_bwd(x, g, dh, dres, name, tr=256):
    T, D = x.shape
    tr = min(tr, T)

    def body(x_ref, g_ref, dh_ref, dres_ref, dx_ref, dg_ref):
        xv = x_ref[...]
        r = lax.rsqrt(jnp.mean(xv * xv, axis=-1, keepdims=True) + EPS)
        xh = xv * r
        dhv = dh_ref[...]
        dhg = dhv * g_ref[...]
        dx_ref[...] = dres_ref[...] + r * (dhg - xh * jnp.mean(dhg * xh, axis=-1, keepdims=True))

        @pl.when(pl.program_id(0) == 0)
        def _():
            dg_ref[...] = jnp.zeros_like(dg_ref)

        dg_ref[...] += jnp.sum(dhv * xh, axis=0, keepdims=True)

    row = pl.BlockSpec((tr, D), lambda r: (r, 0))
    vec = pl.BlockSpec((1, D), lambda r: (0, 0))
    return pl.pallas_call(
        body, name=name, grid=(T // tr,), in_specs=[row, vec, row, row], out_specs=[row, vec],
        out_shape=[jax.ShapeDtypeStruct((T, D), f32), jax.ShapeDtypeStruct((1, D), f32)],
        compiler_params=_cp("arbitrary"),
    )(x, g, dh, dres)


def _loss_grad(y, tgt, name, tr=256):
    T, D = y.shape
    tr = min(tr, T)

    def body(y_ref, t_ref, dy_ref, s_ref):
        e = y_ref[...] - t_ref[...]
        dy_ref[...] = e * (1.0 / D)

        @pl.when(pl.program_id(0) == 0)
        def _():
            s_ref[...] = jnp.zeros_like(s_ref)

        s_ref[...] += jnp.sum(e * e)

    row = pl.BlockSpec((tr, D), lambda r: (r, 0))
    return pl.pallas_call(
        body, name=name, grid=(T // tr,), in_specs=[row, row],
        out_specs=[row, pl.BlockSpec((8, LANES), lambda r: (0, 0))],
        out_shape=[jax.ShapeDtypeStruct((T, D), f32), jax.ShapeDtypeStruct((8, LANES), f32)],
        compiler_params=_cp("arbitrary"),
    )(y, tgt)


def _shift_down(cur, prev, s):
    rolled = pltpu.roll(cur, s, 0)
    top = pltpu.roll(prev, s, 0)
    row = lax.broadcasted_iota(jnp.int32, top.shape, 0)
    head = jnp.where(row < s, top, rolled[0:CONV_HALO])
    return jnp.concatenate([head, rolled[CONV_HALO:]], axis=0)


def _shift_up(cur, nxt, s):
    n = cur.shape[0]
    rolled = pltpu.roll(cur, n - s, 0)
    bot = pltpu.roll(nxt, CONV_HALO - s, 0)
    row = lax.broadcasted_iota(jnp.int32, bot.shape, 0)
    tail = jnp.where(row >= CONV_HALO - s, bot, rolled[n - CONV_HALO:])
    return jnp.concatenate([rolled[:n - CONV_HALO], tail], axis=0)


def _conv_pre(cur, prev, w_ref, b_ref):
    taps = [cur] + [_shift_down(cur, prev, s) for s in (1, 2, 3)]
    pre = b_ref[...] + w_ref[3:4, :] * taps[0]
    for s in (1, 2, 3):
        pre = pre + w_ref[3 - s:4 - s, :] * taps[s]
    return pre, taps


def _conv_specs(T, C, rc, cb):
    cur = pl.BlockSpec((rc, cb), lambda j, r: (r, j))
    prev = pl.BlockSpec((CONV_HALO, cb), lambda j, r: (jnp.maximum(r * (rc // CONV_HALO) - 1, 0), j))
    nxt = pl.BlockSpec((CONV_HALO, cb), lambda j, r: (jnp.minimum((r + 1) * (rc // CONV_HALO), T // CONV_HALO - 1), j))
    w = pl.BlockSpec((4, cb), lambda j, r: (0, j))
    b = pl.BlockSpec((1, cb), lambda j, r: (0, j))
    return cur, prev, nxt, w, b


def _conv_fwd(xraw, w, b, name):
    T, C = xraw.shape
    rc, cb = min(512, T), min(512, C)
    cur, prev, _, ws, bs = _conv_specs(T, C, rc, cb)

    def body(x_ref, p_ref, w_ref, b_ref, o_ref):
        pv = jnp.where(pl.program_id(1) > 0, p_ref[...], 0.0)
        pre, _ = _conv_pre(x_ref[...], pv, w_ref, b_ref)
        o_ref[...] = pre * _sigmoid(pre)

    return pl.pallas_call(
        body, name=name, grid=(C // cb, T // rc), in_specs=[cur, prev, ws, bs], out_specs=cur,
        out_shape=jax.ShapeDtypeStruct((T, C), f32), compiler_params=_cp("parallel", "parallel"),
    )(xraw, xraw, w, b)


def _conv_bwd_pre(xraw, dxbc, w, b, name):
    T, C = xraw.shape
    rc, cb = min(512, T), min(512, C)
    cur, prev, _, ws, bs = _conv_specs(T, C, rc, cb)

    def body(x_ref, p_ref, d_ref, w_ref, b_ref, dpre_ref, dw_ref, db_ref):
        pv = jnp.where(pl.program_id(1) > 0, p_ref[...], 0.0)
        pre, taps = _conv_pre(x_ref[...], pv, w_ref, b_ref)
        sg = _sigmoid(pre)
        dpre = d_ref[...] * (sg * (1.0 + pre * (1.0 - sg)))
        dpre_ref[...] = dpre

        @pl.when(pl.program_id(1) == 0)
        def _():
            dw_ref[...] = jnp.zeros_like(dw_ref)
            db_ref[...] = jnp.zeros_like(db_ref)

        row = lax.broadcasted_iota(jnp.int32, dw_ref.shape, 0)
        upd = jnp.zeros(dw_ref.shape, f32)
        for s in range(4):
            upd = upd + jnp.where(row == 3 - s, jnp.sum(dpre * taps[s], axis=0, keepdims=True), 0.0)
        dw_ref[...] += upd
        db_ref[...] += jnp.sum(dpre, axis=0, keepdims=True)

    return pl.pallas_call(
        body, name=name, grid=(C // cb, T // rc), in_specs=[cur, prev, cur, ws, bs], out_specs=[cur, ws, bs],
        out_shape=[jax.ShapeDtypeStruct((T, C), f32), jax.ShapeDtypeStruct((4, C), f32), jax.ShapeDtypeStruct((1, C), f32)],
        compiler_params=_cp("parallel", "arbitrary"),
    )(xraw, xraw, dxbc, w, b)


def _conv_bwd_in(dpre, w, name):
    T, C = dpre.shape
    rc, cb = min(512, T), min(512, C)
    cur, _, nxt, ws, _ = _conv_specs(T, C, rc, cb)
    nr = T // rc

    def body(d_ref, n_ref, w_ref, o_ref):
        nv = jnp.where(pl.program_id(1) < nr - 1, n_ref[...], 0.0)
        cv = d_ref[...]
        out = w_ref[3:4, :] * cv
        for s in (1, 2, 3):
            out = out + w_ref[3 - s:4 - s, :] * _shift_up(cv, nv, s)
        o_ref[...] = out.astype(o_ref.dtype)

    return pl.pallas_call(
        body, name=name, grid=(C // cb, nr), in_specs=[cur, nxt, ws], out_specs=cur,
        out_shape=jax.ShapeDtypeStruct((T, C), bf16), compiler_params=_cp("parallel", "parallel"),
    )(dpre, dpre, w)


def _ssd_common(dtr_ref, bias_ref, alog_ref):
    L = SSD_CHUNK
    xs = dtr_ref[...] + bias_ref[...]
    dt = _softplus(xs)
    a = -jnp.exp(alog_ref[...])
    row = lax.broadcasted_iota(jnp.int32, (L, L), 0)
    col = lax.broadcasted_iota(jnp.int32, (L, L), 1)
    causal = row >= col
    cs = _mask_dot(causal.astype(bf16), dt * a, 3)
    cs_last = cs[L - 1:L, :]
    return xs, dt, a, causal, cs, cs.T, jnp.exp(cs), jnp.exp(cs_last - cs), jnp.exp(cs_last)


def _ssd_fwd(xbc, dtraw, z, dt_bias, a_log, d_skip, out_norm, HS, name):
    T = xbc.shape[0]
    L, P, NS, G = SSD_CHUNK, SSD_HEAD_DIM, SSD_STATE, SSD_GROUPS
    SW, HPG, nc = HS * P, HS // SSD_GROUPS, T // SSD_CHUNK
    gsz = SW // G

    def body(xbc_ref, dtr_ref, z_ref, bias_ref, alog_ref, dsk_ref, on_ref, y_ref, yn_ref, sp_ref, st_ref):
        @pl.when(pl.program_id(0) == 0)
        def _():
            st_ref[...] = jnp.zeros_like(st_ref)

        sp_ref[0] = st_ref[...]
        _, dt, _, causal, cs, csT, ecs, dte, cdec = _ssd_common(dtr_ref, bias_ref, alog_ref)
        for g in range(G):
            Bb = xbc_ref[:, SW + g * NS:SW + (g + 1) * NS].astype(bf16)
            Cb = xbc_ref[:, SW + (G + g) * NS:SW + (G + g + 1) * NS].astype(bf16)
            Gm = _dot(Cb, Bb, "nt")
            for r in range(HPG):
                h = g * HPG + r
                hs = slice(h * P, (h + 1) * P)
                Xh = xbc_ref[:, hs]
                Xd = Xh * dt[:, h:h + 1]
                seg = cs[:, h:h + 1] - csT[h:h + 1, :]
                Lm = jnp.where(causal, jnp.exp(jnp.minimum(seg, 0.0)), 0.0)
                yd = _dot((Lm * Gm).astype(bf16), Xd.astype(bf16))
                Sp = st_ref[:, hs]
                yo = _dot(Cb, Sp.astype(bf16)) * ecs[:, h:h + 1]
                y_ref[:, hs] = yd + yo + dsk_ref[:, h:h + 1] * Xh
                st_ref[:, hs] = cdec[:, h:h + 1] * Sp + _dot(Bb, (Xd * dte[:, h:h + 1]).astype(bf16), "tn")
        zz = z_ref[...]
        gated = y_ref[...] * (zz * _sigmoid(zz))
        for g in range(G):
            gs = slice(g * gsz, (g + 1) * gsz)
            sg = gated[:, gs]
            rr = lax.rsqrt(jnp.mean(sg * sg, axis=-1, keepdims=True) + EPS)
            yn_ref[:, gs] = (sg * rr * on_ref[:, gs]).astype(yn_ref.dtype)

    vec = pl.BlockSpec((1, LANES), lambda c: (0, 0))
    return pl.pallas_call(
        body, name=name, grid=(nc,),
        in_specs=[pl.BlockSpec((L, xbc.shape[1]), lambda c: (c, 0)), pl.BlockSpec((L, LANES), lambda c: (c, 0)),
                  pl.BlockSpec((L, SW), lambda c: (c, 0)), vec, vec, vec, pl.BlockSpec((1, SW), lambda c: (0, 0))],
        out_specs=[pl.BlockSpec((L, SW), lambda c: (c, 0)), pl.BlockSpec((L, SW), lambda c: (c, 0)),
                   pl.BlockSpec((1, NS, SW), lambda c: (c, 0, 0))],
        out_shape=[jax.ShapeDtypeStruct((T, SW), f32), jax.ShapeDtypeStruct((T, SW), bf16),
                   jax.ShapeDtypeStruct((nc, NS, SW), f32)],
        scratch_shapes=[pltpu.VMEM((NS, SW), f32)], compiler_params=_cp("arbitrary"),
    )(xbc, dtraw, z, dt_bias, a_log, d_skip, out_norm)


def _ssd_bwd(xbc, dtraw, sprev, dy, dt_bias, a_log, d_skip, HS, name):
    T = xbc.shape[0]
    L, P, NS, G = SSD_CHUNK, SSD_HEAD_DIM, SSD_STATE, SSD_GROUPS
    SW, HPG, nc = HS * P, HS // SSD_GROUPS, T // SSD_CHUNK

    def body(xbc_ref, dtr_ref, sp_ref, dy_ref, bias_ref, alog_ref, dsk_ref,
             dxbc_ref, ddtr_ref, dalog_ref, dbias_ref, dd_ref, ds_ref):
        @pl.when(pl.program_id(0) == 0)
        def _():
            ds_ref[...] = jnp.zeros_like(ds_ref)
            dalog_ref[...] = jnp.zeros_like(dalog_ref)
            dbias_ref[...] = jnp.zeros_like(dbias_ref)
            dd_ref[...] = jnp.zeros_like(dd_ref)

        xs, dt, a, causal, cs, csT, ecs, dte, cdec = _ssd_common(dtr_ref, bias_ref, alog_ref)
        lane = lax.broadcasted_iota(jnp.int32, (L, LANES), 1)
        sub = lax.broadcasted_iota(jnp.int32, (LANES, L), 0)
        lane1 = lax.broadcasted_iota(jnp.int32, (1, LANES), 1)
        dcs = jnp.zeros((L, LANES), f32)
        dcs_t = jnp.zeros((LANES, L), f32)
        xds = jnp.zeros((L, LANES), f32)
        dlast = jnp.zeros((1, LANES), f32)
        dD = jnp.zeros((1, LANES), f32)
        for g in range(G):
            bsl = slice(SW + g * NS, SW + (g + 1) * NS)
            csl = slice(SW + (G + g) * NS, SW + (G + g + 1) * NS)
            Bb = xbc_ref[:, bsl].astype(bf16)
            Cb = xbc_ref[:, csl].astype(bf16)
            Gm = _dot(Cb, Bb, "nt")
            dG = jnp.zeros((L, L), f32)
            dB = jnp.zeros((L, NS), f32)
            dC = jnp.zeros((L, NS), f32)
            for r in range(HPG):
                h = g * HPG + r
                hs = slice(h * P, (h + 1) * P)
                Xh = xbc_ref[:, hs]
                dth = dt[:, h:h + 1]
                Xd = Xh * dth
                Xdb = Xd.astype(bf16)
                seg = cs[:, h:h + 1] - csT[h:h + 1, :]
                Lm = jnp.where(causal, jnp.exp(jnp.minimum(seg, 0.0)), 0.0)
                Mf = Lm * Gm
                dYh = dy_ref[:, hs]
                dYb = dYh.astype(bf16)
                Sp = sp_ref[0, :, hs]
                Spb = Sp.astype(bf16)
                dSh = ds_ref[:, hs]
                dSb = dSh.astype(bf16)
                ecs_h, dte_h, cdec_h = ecs[:, h:h + 1], dte[:, h:h + 1], cdec[:, h:h + 1]
                dM = _dot(dYb, Xdb, "nt")
                dXd = _dot(Mf.astype(bf16), dYb, "tn")
                Q = dM * Mf
                dcs = dcs + jnp.where(lane == h, jnp.sum(Q, axis=1, keepdims=True), 0.0)
                dcs_t = dcs_t - jnp.where(sub == h, jnp.sum(Q, axis=0, keepdims=True), 0.0)
                dG = dG + dM * Lm
                CS = _dot(Cb, Spb)
                dcs = dcs + jnp.where(lane == h, jnp.sum(dYh * CS, axis=1, keepdims=True) * ecs_h, 0.0)
                Wb = (dYh * ecs_h).astype(bf16)
                dC = dC + _dot(Wb, Spb, "nt")
                Zb = _dot(Bb, dSb)
                dXd = dXd + dte_h * Zb
                Tt = dte_h * jnp.sum(Xd * Zb, axis=1, keepdims=True)
                dcs = dcs - jnp.where(lane == h, Tt, 0.0)
                dlast = dlast + jnp.where(lane1 == h, jnp.sum(Tt) + cdec_h * jnp.sum(Sp * dSh), 0.0)
                dB = dB + _dot((Xd * dte_h).astype(bf16), dSb, "nt")
                dxbc_ref[:, hs] = dXd * dth + dsk_ref[:, h:h + 1] * dYh
                xds = xds + jnp.where(lane == h, jnp.sum(dXd * Xh, axis=1, keepdims=True), 0.0)
                dD = dD + jnp.where(lane1 == h, jnp.sum(dYh * Xh), 0.0)
                ds_ref[:, hs] = cdec_h * dSh + _dot(Cb, Wb, "tn")
            dGb = dG.astype(bf16)
            dxbc_ref[:, bsl] = dB + _dot(dGb, Cb, "tn")
            dxbc_ref[:, csl] = dC + _dot(dGb, Bb)
        rowi = lax.broadcasted_iota(jnp.int32, (L, LANES), 0)
        dcs = dcs + dcs_t.T + jnp.where(rowi == L - 1, dlast, 0.0)
        anti = (lax.broadcasted_iota(jnp.int32, (L, L), 1) >= lax.broadcasted_iota(jnp.int32, (L, L), 0)).astype(bf16)
        dda = _mask_dot(anti, dcs, 3)
        ddt = dda * a + xds
        dalog_ref[...] += jnp.sum(dda * dt, axis=0, keepdims=True) * a
        ddtr = ddt * _sigmoid(xs)
        ddtr_ref[...] = ddtr
        dbias_ref[...] += jnp.sum(ddtr, axis=0, keepdims=True)
        dd_ref[...] += dD

    rev = lambda c: (nc - 1 - c, 0)
    vec = pl.BlockSpec((1, LANES), lambda c: (0, 0))
    return pl.pallas_call(
        body, name=name, grid=(nc,),
        in_specs=[pl.BlockSpec((L, xbc.shape[1]), rev), pl.BlockSpec((L, LANES), rev),
                  pl.BlockSpec((1, NS, SW), lambda c: (nc - 1 - c, 0, 0)), pl.BlockSpec((L, SW), rev), vec, vec, vec],
        out_specs=[pl.BlockSpec((L, xbc.shape[1]), rev), pl.BlockSpec((L, LANES), rev), vec, vec, vec],
        out_shape=[jax.ShapeDtypeStruct(xbc.shape, f32), jax.ShapeDtypeStruct((T, LANES), f32)]
        + [jax.ShapeDtypeStruct((1, LANES), f32)] * 3,
        scratch_shapes=[pltpu.VMEM((NS, SW), f32)], compiler_params=_cp("arbitrary"),
    )(xbc, dtraw, sprev, dy, dt_bias, a_log, d_skip)


def _gate_bwd(y, z, dyn, out_norm, name, tr=256):
    T, SW = y.shape
    tr = min(tr, T)
    gsz = SW // SSD_GROUPS

    def body(y_ref, z_ref, d_ref, on_ref, dy_ref, dz_ref, don_ref):
        @pl.when(pl.program_id(0) == 0)
        def _():
            don_ref[...] = jnp.zeros_like(don_ref)

        for g in range(SSD_GROUPS):
            gs = slice(g * gsz, (g + 1) * gsz)
            yv, zv, dv = y_ref[:, gs], z_ref[:, gs], d_ref[:, gs]
            sg = _sigmoid(zv)
            sl = zv * sg
            gated = yv * sl
            rr = lax.rsqrt(jnp.mean(gated * gated, axis=-1, keepdims=True) + EPS)
            gh = gated * rr
            dgn = dv * on_ref[:, gs]
            dgated = rr * (dgn - gh * jnp.mean(dgn * gh, axis=-1, keepdims=True))
            dy_ref[:, gs] = dgated * sl
            dz_ref[:, gs] = (dgated * yv * (sg * (1.0 + zv * (1.0 - sg)))).astype(dz_ref.dtype)
            don_ref[:, gs] += jnp.sum(dv * gh, axis=0, keepdims=True)

    row = pl.BlockSpec((tr, SW), lambda r: (r, 0))
    vec = pl.BlockSpec((1, SW), lambda r: (0, 0))
    return pl.pallas_call(
        body, name=name, grid=(T // tr,), in_specs=[row, row, row, vec], out_specs=[row, row, vec],
        out_shape=[jax.ShapeDtypeStruct((T, SW), f32), jax.ShapeDtypeStruct((T, SW), bf16),
                   jax.ShapeDtypeStruct((1, SW), f32)],
        compiler_params=_cp("arbitrary"),
    )(y, z, dyn, out_norm)


def _qk_norm_fwd(qkv, qn_w, kn_w, SBW, name, tr=256):
    T = qkv.shape[0]
    tr = min(tr, T)
    nh = SBW // SB_HEAD_DIM

    def body(q_ref, k_ref, v_ref, qw_ref, kw_ref, qo_ref, ko_ref, vo_ref):
        for src, w_ref, dst in ((q_ref, qw_ref, qo_ref), (k_ref, kw_ref, ko_ref)):
            for h in range(nh):
                hs = slice(h * SB_HEAD_DIM, (h + 1) * SB_HEAD_DIM)
                sv = src[:, hs]
                rr = lax.rsqrt(jnp.mean(sv * sv, axis=-1, keepdims=True) + EPS)
                dst[:, hs] = (sv * rr * w_ref[...]).astype(dst.dtype)
        vo_ref[...] = v_ref[...].astype(vo_ref.dtype)

    blk = lambda j: pl.BlockSpec((tr, SBW), lambda r: (r, j))
    vec = pl.BlockSpec((1, SB_HEAD_DIM), lambda r: (0, 0))
    out = pl.BlockSpec((tr, SBW), lambda r: (r, 0))
    return pl.pallas_call(
        body, name=name, grid=(T // tr,), in_specs=[blk(0), blk(1), blk(2), vec, vec], out_specs=[out, out, out],
        out_shape=[jax.ShapeDtypeStruct((T, SBW), bf16)] * 3, compiler_params=_cp("parallel"),
    )(qkv, qkv, qkv, qn_w, kn_w)


def _qk_norm_bwd(qkv, dqn, dkn, dv, qn_w, kn_w, SBW, name, tr=256):
    T = qkv.shape[0]
    tr = min(tr, T)
    nh = SBW // SB_HEAD_DIM

    def body(q_ref, k_ref, dq_ref, dk_ref, dv_ref, qw_ref, kw_ref, o_ref, dqw_ref, dkw_ref):
        @pl.when(pl.program_id(0) == 0)
        def _():
            dqw_ref[...] = jnp.zeros_like(dqw_ref)
            dkw_ref[...] = jnp.zeros_like(dkw_ref)

        for part, (src, d_ref, w_ref, dw_ref) in enumerate(((q_ref, dq_ref, qw_ref, dqw_ref), (k_ref, dk_ref, kw_ref, dkw_ref))):
            dw = jnp.zeros((1, SB_HEAD_DIM), f32)
            for h in range(nh):
                hs = slice(h * SB_HEAD_DIM, (h + 1) * SB_HEAD_DIM)
                os_ = slice(part * SBW + h * SB_HEAD_DIM, part * SBW + (h + 1) * SB_HEAD_DIM)
                sv, dn = src[:, hs], d_ref[:, hs]
                rr = lax.rsqrt(jnp.mean(sv * sv, axis=-1, keepdims=True) + EPS)
                xh = sv * rr
                dg = dn * w_ref[...]
                o_ref[:, os_] = (rr * (dg - xh * jnp.mean(dg * xh, axis=-1, keepdims=True))).astype(o_ref.dtype)
                dw = dw + jnp.sum(dn * xh, axis=0, keepdims=True)
            dw_ref[...] += dw
        o_ref[:, 2 * SBW:] = dv_ref[...].astype(o_ref.dtype)

    blk = lambda j: pl.BlockSpec((tr, SBW), lambda r: (r, j))
    vec = pl.BlockSpec((1, SB_HEAD_DIM), lambda r: (0, 0))
    row = pl.BlockSpec((tr, SBW), lambda r: (r, 0))
    return pl.pallas_call(
        body, name=name, grid=(T // tr,), in_specs=[blk(0), blk(1), row, row, row, vec, vec],
        out_specs=[pl.BlockSpec((tr, 3 * SBW), lambda r: (r, 0)), vec, vec],
        out_shape=[jax.ShapeDtypeStruct((T, 3 * SBW), bf16)] + [jax.ShapeDtypeStruct((1, SB_HEAD_DIM), f32)] * 2,
        compiler_params=_cp("arbitrary"),
    )(qkv, qkv, dqn, dkn, dv, qn_w, kn_w)


def _sb_logits(q, kb, scale):
    zl = _dot(q, kb, "nt") * scale
    lb = jnp.minimum(zl, 0.0) - jnp.log(1.0 + jnp.exp(-jnp.abs(zl)))
    return zl, lb, lb - zl


def _sb_fwd(qn, kn, vb, name, tb=256):
    T, W = qn.shape
    tb = min(tb, T)
    nh, nq, dh = W // SB_HEAD_DIM, T // tb, SB_HEAD_DIM
    scale = dh ** -0.5

    def body(q_ref, k_ref, v_ref, o_ref, c_ref):
        qi = pl.program_id(1)
        q = q_ref[...]
        row = lax.broadcasted_iota(jnp.int32, (tb, tb), 0)
        col = lax.broadcasted_iota(jnp.int32, (tb, tb), 1)
        later = (row > col).astype(bf16)

        def step(j, carry, diag):
            acc, run = carry
            ks = pl.multiple_of(j * tb, tb)
            kb, vv = k_ref[pl.ds(ks, tb), :], v_ref[pl.ds(ks, tb), :]
            _, lb, lk = _sb_logits(q, kb, scale)
            if diag:
                lk = jnp.where(col < row, lk, 0.0)
            between = _dot_mask(lk, later, 2)
            w = jnp.exp(lb + between + run)
            if diag:
                w = jnp.where(col < row, w, 0.0)
            acc = acc + _dot(w.astype(bf16), vv)
            return acc, run + between[:, 0:1] + lk[:, 0:1]

        carry = step(qi, (jnp.zeros((tb, dh), f32), jnp.zeros((tb, 1), f32)), True)
        acc, run = lax.fori_loop(0, qi, lambda t, c: step(qi - 1 - t, c, False), carry)
        o_ref[...] = acc.astype(o_ref.dtype)
        c_ref[...] = jnp.broadcast_to(run, (tb, dh))

    qblk = pl.BlockSpec((tb, dh), lambda h, i: (i, h))
    full = pl.BlockSpec((T, dh), lambda h, i: (0, h))
    return pl.pallas_call(
        body, name=name, grid=(nh, nq), in_specs=[qblk, full, full], out_specs=[qblk, qblk],
        out_shape=[jax.ShapeDtypeStruct((T, W), bf16), jax.ShapeDtypeStruct((T, W), f32)],
        compiler_params=_cp("parallel", "parallel"),
    )(qn, kn, vb)


def _sb_bwd(qn, kn, vb, do, ctot, do_off, name, tb=256):
    T, W = qn.shape
    tb = min(tb, T)
    nh, nq, dh = W // SB_HEAD_DIM, T // tb, SB_HEAD_DIM
    scale = dh ** -0.5
    ob = do_off // dh

    def body(q_ref, k_ref, v_ref, do_ref, c_ref, dq_ref, dk_ref, dv_ref):
        qi = pl.program_id(1)

        @pl.when(qi == 0)
        def _():
            dk_ref[...] = jnp.zeros_like(dk_ref)
            dv_ref[...] = jnp.zeros_like(dv_ref)

        q = q_ref[...]
        dob = do_ref[...].astype(bf16)
        total = c_ref[:, 0:1]
        row = lax.broadcasted_iota(jnp.int32, (tb, tb), 0)
        col = lax.broadcasted_iota(jnp.int32, (tb, tb), 1)
        upto = (row <= col).astype(bf16)
        before = (row < col).astype(bf16)

        def step(j, carry, diag):
            dq, pre, gpre = carry
            ks = pl.multiple_of(j * tb, tb)
            kb, vv = k_ref[pl.ds(ks, tb), :], v_ref[pl.ds(ks, tb), :]
            _, lb, lk = _sb_logits(q, kb, scale)
            if diag:
                lk = jnp.where(col < row, lk, 0.0)
            pin = _dot_mask(lk, upto, 2)
            w = jnp.exp(lb + (total - pre - pin))
            if diag:
                w = jnp.where(col < row, w, 0.0)
            dw = _dot(dob, vv, "nt")
            dv_ref[pl.ds(ks, tb), :] += _dot(w.astype(bf16), dob, "tn")
            gg = dw * w
            gex = _dot_mask(gg, before, 2)
            beta = jnp.exp(lb)
            dz = (gg * (1.0 - beta) - (gpre + gex) * beta) * scale
            if diag:
                dz = jnp.where(col < row, dz, 0.0)
            dzb = dz.astype(bf16)
            dk_ref[pl.ds(ks, tb), :] += _dot(dzb, q, "tn")
            return (dq + _dot(dzb, kb), pre + pin[:, tb - 1:tb], gpre + gex[:, tb - 1:tb] + gg[:, tb - 1:tb])

        init = (jnp.zeros((tb, dh), f32), jnp.zeros((tb, 1), f32), jnp.zeros((tb, 1), f32))
        carry = lax.fori_loop(0, qi, lambda t, c: step(t, c, False), init)
        dq_ref[...] = step(qi, carry, True)[0]

    qblk = pl.BlockSpec((tb, dh), lambda h, i: (i, h))
    full = pl.BlockSpec((T, dh), lambda h, i: (0, h))
    return pl.pallas_call(
        body, name=name, grid=(nh, nq),
        in_specs=[qblk, full, full, pl.BlockSpec((tb, dh), lambda h, i: (i, h + ob)), qblk],
        out_specs=[qblk, full, full], out_shape=[jax.ShapeDtypeStruct((T, W), f32)] * 3,
        compiler_params=_cp("parallel", "arbitrary"),
    )(qn, kn, vb, do, ctot)


def _pool_select(sums, g):
    return jnp.where(g == 0, sums[0], jnp.where(g == 1, sums[1], jnp.where(g == 2, sums[2], sums[3])))


def _pool_count(g, r, rc, n, cols, off=0):
    t = (r * rc + off + lax.broadcasted_iota(jnp.int32, (n, cols), 0)).astype(f32)
    win = jnp.left_shift(2, g).astype(f32)
    return jnp.minimum(t + 1.0, win)


def _pool_fwd(hp, xres, w, b, scale, name, rc=512):
    T, D = hp.shape
    rc = min(rc, T)
    pg = D // len(POOL_WINDOWS)

    def body(h_ref, p_ref, x_ref, w_ref, b_ref, s_ref, o_ref, yp_ref, d_ref):
        g, r = pl.program_id(0), pl.program_id(1)
        cur = h_ref[...]
        halo = jnp.where(r > 0, p_ref[...], 0.0)
        ext = jnp.concatenate([halo, cur], axis=0)
        sums, s = [], ext
        for sh in (1, 2, 4, 8):
            s = s + pltpu.roll(s, sh, 0)
            sums.append(s)
        d = _pool_select(sums, g)[POOL_HALO:] / _pool_count(g, r, rc, rc, pg) - cur
        yp = _dot(d.astype(bf16), w_ref[0]) + b_ref[...]
        yp_ref[...] = yp
        d_ref[...] = d.astype(d_ref.dtype)
        o_ref[...] = x_ref[...] + yp * s_ref[...]

    cur = pl.BlockSpec((rc, pg), lambda g, r: (r, g))
    prev = pl.BlockSpec((POOL_HALO, pg), lambda g, r: (jnp.maximum(r * (rc // POOL_HALO) - 1, 0), g))
    vec = pl.BlockSpec((1, pg), lambda g, r: (0, g))
    return pl.pallas_call(
        body, name=name, grid=(len(POOL_WINDOWS), T // rc),
        in_specs=[cur, prev, cur, pl.BlockSpec((1, pg, pg), lambda g, r: (g, 0, 0)), vec, vec],
        out_specs=[cur, cur, cur],
        out_shape=[jax.ShapeDtypeStruct((T, D), f32), jax.ShapeDtypeStruct((T, D), f32), jax.ShapeDtypeStruct((T, D), bf16)],
        compiler_params=_cp("parallel", "parallel"),
    )(hp, hp, xres, w, b, scale)


def _pool_bwd(dx, yp, d, w, scale, name, rc=512):
    T, D = dx.shape
    rc = min(rc, T)
    pg = D // len(POOL_WINDOWS)
    nr = T // rc

    def body(dx_ref, dn_ref, yp_ref, d_ref, w_ref, s_ref, dh_ref, dw_ref, db_ref, dsc_ref):
        g, r = pl.program_id(0), pl.program_id(1)

        @pl.when(r == 0)
        def _():
            dw_ref[...] = jnp.zeros_like(dw_ref)
            db_ref[...] = jnp.zeros_like(db_ref)
            dsc_ref[...] = jnp.zeros_like(dsc_ref)

        dxv = dx_ref[...]
        dyp = dxv * s_ref[...]
        dsc_ref[...] += jnp.sum(dxv * yp_ref[...], axis=0, keepdims=True)
        db_ref[...] += jnp.sum(dyp, axis=0, keepdims=True)
        dypb = dyp.astype(bf16)
        dw_ref[0] += _dot(d_ref[...], dypb, "tn")
        dd = _dot(dypb, w_ref[0], "nt")
        ddn = _dot((dn_ref[...] * s_ref[...]).astype(bf16), w_ref[0], "nt")
        e = dd / _pool_count(g, r, rc, rc, pg)
        en = jnp.where(r < nr - 1, ddn / _pool_count(g, r, rc, POOL_HALO, pg, off=rc), 0.0)
        ext = jnp.concatenate([e, en], axis=0)
        sums, s = [], ext
        for sh in (1, 2, 4, 8):
            s = s + pltpu.roll(s, rc + POOL_HALO - sh, 0)
            sums.append(s)
        dh_ref[...] = _pool_select(sums, g)[:rc] - dd

    cur = pl.BlockSpec((rc, pg), lambda g, r: (r, g))
    nxt = pl.BlockSpec((POOL_HALO, pg), lambda g, r: (jnp.minimum((r + 1) * (rc // POOL_HALO), T // POOL_HALO - 1), g))
    vec = pl.BlockSpec((1, pg), lambda g, r: (0, g))
    wsp = pl.BlockSpec((1, pg, pg), lambda g, r: (g, 0, 0))
    return pl.pallas_call(
        body, name=name, grid=(len(POOL_WINDOWS), nr), in_specs=[cur, nxt, cur, cur, wsp, vec],
        out_specs=[cur, wsp, vec, vec],
        out_shape=[jax.ShapeDtypeStruct((T, D), f32), jax.ShapeDtypeStruct(w.shape, f32),
                   jax.ShapeDtypeStruct((1, D), f32), jax.ShapeDtypeStruct((1, D), f32)],
        compiler_params=_cp("parallel", "arbitrary"),
    )(dx, dx, yp, d, w, scale)


def _adamw(w, g, m, v, name, tr=256):
    R, C = w.shape
    tr = min(tr, R)
    assert R % tr == 0

    def body(w_ref, g_ref, m_ref, v_ref, d_ref, mo_ref, vo_ref):
        gv = g_ref[...]
        mn = ADAM_B1 * m_ref[...] + (1.0 - ADAM_B1) * gv
        vn = ADAM_B2 * v_ref[...] + (1.0 - ADAM_B2) * (gv * gv)
        m_hat = mn / (1.0 - ADAM_B1 ** ADAM_STEP)
        v_hat = vn / (1.0 - ADAM_B2 ** ADAM_STEP)
        d_ref[...] = -ADAM_LR * (m_hat / (jnp.sqrt(v_hat) + ADAM_EPS) + ADAM_WD * w_ref[...])
        mo_ref[...] = mn
        vo_ref[...] = vn

    blk = pl.BlockSpec((tr, C), lambda r: (r, 0))
    return pl.pallas_call(
        body, name=name, grid=(R // tr,), in_specs=[blk] * 4, out_specs=[blk] * 3,
        out_shape=[jax.ShapeDtypeStruct((R, C), f32)] * 3, compiler_params=_cp("parallel"),
    )(w, g, m, v)


def _pair_sum(g4, recv, core, name, br=256):
    _, R, C = g4.shape
    hr = R // 2
    br = min(br, hr)
    nb = hr // br

    def body(c_ref, a_ref, b_ref, o_ref):
        o_ref[...] = a_ref[...] + b_ref[...]

    return pl.pallas_call(
        body, name=name,
        grid_spec=pltpu.PrefetchScalarGridSpec(
            num_scalar_prefetch=1, grid=(N_CHIPS, nb),
            in_specs=[pl.BlockSpec((1, br, C), lambda s, i, c: (s, c[0] * nb + i, 0)),
                      pl.BlockSpec((1, br, C), lambda s, i, c: (s, i, 0))],
            out_specs=pl.BlockSpec((1, br, C), lambda s, i, c: (s, i, 0))),
        out_shape=jax.ShapeDtypeStruct((N_CHIPS, hr, C), f32), compiler_params=_cp("parallel", "parallel"),
    )(core, g4, recv)


def _chip_sum(own4, recv4, chip, name, br=256):
    _, hr, C = own4.shape
    br = min(br, hr)

    def body(c_ref, a_ref, b1_ref, b2_ref, b3_ref, o_ref):
        o_ref[...] = ((a_ref[0] + b1_ref[0]) + b2_ref[0]) + b3_ref[0]

    other = lambda k: pl.BlockSpec((1, br, C), lambda i, c: ((c[0] + k) % N_CHIPS, i, 0))
    return pl.pallas_call(
        body, name=name,
        grid_spec=pltpu.PrefetchScalarGridSpec(
            num_scalar_prefetch=1, grid=(hr // br,),
            in_specs=[pl.BlockSpec((1, br, C), lambda i, c: (c[0], i, 0)), other(1), other(2), other(3)],
            out_specs=pl.BlockSpec((br, C), lambda i, c: (i, 0))),
        out_shape=jax.ShapeDtypeStruct((hr, C), f32), compiler_params=_cp("parallel"),
    )(chip, own4, recv4, recv4, recv4)


ANY = pl.BlockSpec(memory_space=pl.ANY)


def _mesh_pos():
    x, y, c = lax.axis_index("x"), lax.axis_index("y"), lax.axis_index("c")
    others = [(1 - x, y), (x, 1 - y), (1 - x, 1 - y)]
    return x, y, c, 2 * x + y, others


def _gather_small(blk, name):
    m, n = blk.shape

    def body(x_ref, out_ref, sum_ref, send_sems, recv_sems, local_sem):
        x, y, c, _, others = _mesh_pos()
        me, sibling = (x, y, c), (x, y, 1 - c)

        def rows(px, py, pc):
            return out_ref.at[pl.ds((4 * px + 2 * py + pc) * m, m), :]

        def copy(k, block, to, src=None):
            return pltpu.make_async_remote_copy(
                src_ref=rows(*block) if src is None else src, dst_ref=rows(*block),
                send_sem=send_sems.at[k], recv_sem=recv_sems.at[k], device_id=to, device_id_type=MESH)

        mine = pltpu.make_async_copy(x_ref, rows(*me), local_sem)
        mine.start()
        first = [copy(0, me, sibling, src=x_ref)]
        first += [copy(1 + j, me, (*chip, c), src=x_ref) for j, chip in enumerate(others)]
        for cp in first:
            cp.start()
        passed = [copy(4 + j, (*chip, c), sibling) for j, chip in enumerate(others)]
        for j, chip in enumerate(others):
            copy(1 + j, (*chip, c), me).wait_recv()
            passed[j].start()
        copy(0, sibling, me).wait_recv()
        for j, chip in enumerate(others):
            copy(4 + j, (*chip, 1 - c), me).wait_recv()
        for cp in first + passed:
            cp.wait_send()
        mine.wait()
        acc = out_ref[0:m, :]
        for d in range(1, 8):
            acc = acc + out_ref[d * m:(d + 1) * m, :]
        sum_ref[...] = acc

    vm = pl.BlockSpec(memory_space=pltpu.VMEM)
    return pl.pallas_call(
        body, name=name, in_specs=[vm], out_specs=[vm, vm],
        out_shape=[jax.ShapeDtypeStruct((8 * m, n), f32), jax.ShapeDtypeStruct((m, n), f32)],
        scratch_shapes=[pltpu.SemaphoreType.DMA((7,)), pltpu.SemaphoreType.DMA((7,)), pltpu.SemaphoreType.DMA],
    )(blk)


def _gather_weights(shards, name):
    nt = len(shards)

    def body(*refs):
        ins, outs = refs[:nt], refs[nt:2 * nt]
        send_sems, recv_sems, local_sems = refs[2 * nt:]
        x, y, c, chip, others = _mesh_pos()
        sibling = (x, y, 1 - c)
        locals_, sends = [], []
        for t in range(nt):
            hr = ins[t].shape[0] // 2
            cp = pltpu.make_async_copy(ins[t], outs[t].at[chip], local_sems.at[t])
            cp.start()
            locals_.append(cp)
            for j, (px, py) in enumerate(others):
                cp = pltpu.make_async_remote_copy(
                    src_ref=ins[t].at[pl.ds(c * hr, hr)], dst_ref=outs[t].at[chip, pl.ds(c * hr, hr)],
                    send_sem=send_sems.at[t, j], recv_sem=recv_sems.at[t, j], device_id=(px, py, c), device_id_type=MESH)
                cp.start()
                sends.append(cp)
        for t in range(nt):
            hr = ins[t].shape[0] // 2
            for j, (px, py) in enumerate(others):
                piece = outs[t].at[2 * px + py, pl.ds(c * hr, hr)]
                pltpu.make_async_remote_copy(
                    src_ref=piece, dst_ref=piece, send_sem=send_sems.at[t, j], recv_sem=recv_sems.at[t, j],
                    device_id=(px, py, c), device_id_type=MESH).wait_recv()
                cp = pltpu.make_async_remote_copy(
                    src_ref=piece, dst_ref=piece, send_sem=send_sems.at[t, 3 + j], recv_sem=recv_sems.at[t, 3 + j],
                    device_id=sibling, device_id_type=MESH)
                cp.start()
                sends.append(cp)
        for t in range(nt):
            hr = ins[t].shape[0] // 2
            for j, (px, py) in enumerate(others):
                piece = outs[t].at[2 * px + py, pl.ds((1 - c) * hr, hr)]
                pltpu.make_async_remote_copy(
                    src_ref=piece, dst_ref=piece, send_sem=send_sems.at[t, 3 + j], recv_sem=recv_sems.at[t, 3 + j],
                    device_id=sibling, device_id_type=MESH).wait_recv()
        for cp in sends:
            cp.wait_send()
        for cp in locals_:
            cp.wait()

    return pl.pallas_call(
        body, name=name, in_specs=[ANY] * nt, out_specs=[ANY] * nt,
        out_shape=[jax.ShapeDtypeStruct((N_CHIPS,) + s.shape, s.dtype) for s in shards],
        scratch_shapes=[pltpu.SemaphoreType.DMA((nt, 6)), pltpu.SemaphoreType.DMA((nt, 6)), pltpu.SemaphoreType.DMA((nt,))],
    )(*shards)


def _swap_halves(g4s, name):
    nt = len(g4s)

    def body(*refs):
        ins, outs = refs[:nt], refs[nt:2 * nt]
        send_sems, recv_sems = refs[2 * nt:]
        x, y, c, _, _ = _mesh_pos()
        cps = []
        for t in range(nt):
            hr = ins[t].shape[1] // 2
            cp = pltpu.make_async_remote_copy(
                src_ref=ins[t].at[:, pl.ds((1 - c) * hr, hr)], dst_ref=outs[t], send_sem=send_sems.at[t],
                recv_sem=recv_sems.at[t], device_id=(x, y, 1 - c), device_id_type=MESH)
            cp.start()
            cps.append(cp)
        for cp in cps:
            cp.wait()

    return pl.pallas_call(
        body, name=name, in_specs=[ANY] * nt, out_specs=[ANY] * nt,
        out_shape=[jax.ShapeDtypeStruct((N_CHIPS, g.shape[1] // 2, g.shape[2]), g.dtype) for g in g4s],
        scratch_shapes=[pltpu.SemaphoreType.DMA((nt,)), pltpu.SemaphoreType.DMA((nt,))],
    )(*g4s)


def _exchange_chips(h4s, name):
    nt = len(h4s)

    def body(*refs):
        ins, outs = refs[:nt], refs[nt:2 * nt]
        send_sems, recv_sems = refs[2 * nt:]
        x, y, c, chip, others = _mesh_pos()
        cps = []
        for t in range(nt):
            for j, (px, py) in enumerate(others):
                cp = pltpu.make_async_remote_copy(
                    src_ref=ins[t].at[2 * px + py], dst_ref=outs[t].at[chip], send_sem=send_sems.at[t, j],
                    recv_sem=recv_sems.at[t, j], device_id=(px, py, c), device_id_type=MESH)
                cp.start()
                cps.append(cp)
        for t in range(nt):
            for j, (px, py) in enumerate(others):
                landed = outs[t].at[2 * px + py]
                pltpu.make_async_remote_copy(
                    src_ref=landed, dst_ref=landed, send_sem=send_sems.at[t, j], recv_sem=recv_sems.at[t, j],
                    device_id=(px, py, c), device_id_type=MESH).wait_recv()
        for cp in cps:
            cp.wait_send()

    return pl.pallas_call(
        body, name=name, in_specs=[ANY] * nt, out_specs=[ANY] * nt,
        out_shape=[jax.ShapeDtypeStruct(h.shape, h.dtype) for h in h4s],
        scratch_shapes=[pltpu.SemaphoreType.DMA((nt, 3)), pltpu.SemaphoreType.DMA((nt, 3))],
    )(*h4s)


def _join_halves(fs, name):
    nt = len(fs)

    def body(*refs):
        ins, outs = refs[:nt], refs[nt:2 * nt]
        send_sems, recv_sems, local_sems = refs[2 * nt:]
        x, y, c, _, _ = _mesh_pos()
        cps, locals_ = [], []
        for t in range(nt):
            hr = ins[t].shape[0]
            lc = pltpu.make_async_copy(ins[t], outs[t].at[pl.ds(c * hr, hr)], local_sems.at[t])
            lc.start()
            locals_.append(lc)
            cp = pltpu.make_async_remote_copy(
                src_ref=ins[t], dst_ref=outs[t].at[pl.ds(c * hr, hr)], send_sem=send_sems.at[t],
                recv_sem=recv_sems.at[t], device_id=(x, y, 1 - c), device_id_type=MESH)
            cp.start()
            cps.append(cp)
        for t in range(nt):
            hr = ins[t].shape[0]
            got = outs[t].at[pl.ds((1 - c) * hr, hr)]
            pltpu.make_async_remote_copy(
                src_ref=got, dst_ref=got, send_sem=send_sems.at[t], recv_sem=recv_sems.at[t],
                device_id=(x, y, 1 - c), device_id_type=MESH).wait_recv()
        for cp in cps:
            cp.wait_send()
        for lc in locals_:
            lc.wait()

    return pl.pallas_call(
        body, name=name, in_specs=[ANY] * nt, out_specs=[ANY] * nt,
        out_shape=[jax.ShapeDtypeStruct((2 * f.shape[0], f.shape[1]), f.dtype) for f in fs],
        scratch_shapes=[pltpu.SemaphoreType.DMA((nt,)), pltpu.SemaphoreType.DMA((nt,)), pltpu.SemaphoreType.DMA((nt,))],
    )(*fs)


def _pad_lanes(v, n=LANES):
    return jnp.pad(v, ((0, 0), (0, n - v.shape[-1])))


def _mlp_fwd(xin, norm_g, w_up, w_down, tag):
    T, D = xin.shape
    F = w_up.shape[1]
    h = _rms_fwd(xin, norm_g, bf16, f"{tag}_norm")
    def relu_sq(acc):
        r = jnp.maximum(acc, 0.0)
        return r, r * r

    u, a = _mm(h, w_up, "nn", T, F, D, (bf16, bf16), f"{tag}_up", epilogue=relu_sq)
    out = _mm(a, w_down, "nn", T, D, F, (f32,), f"{tag}_down", epilogue=lambda acc, res: (res + acc,),
              extras=((xin, "tile"),))
    return out, (xin, h, u, a)


def _mlp_bwd(dy, saved, norm_g, w_up, w_down, tag):
    xin, h, u, a = saved
    T, D = xin.shape
    F = w_up.shape[1]
    du = _mm(dy, w_down, "nt", T, F, D, (bf16,), f"{tag}_dact", epilogue=lambda acc, uu: (acc * (2.0 * uu.astype(f32)),),
             extras=((u, "tile"),))
    dw_down = _mm(a, dy, "tn", F, D, T, (f32,), f"{tag}_dwdown")
    dh = _mm(du, w_up, "nt", T, D, F, (f32,), f"{tag}_dh")
    dw_up = _mm(h, du, "tn", D, F, T, (f32,), f"{tag}_dwup")
    dx, dg = _rms_bwd(xin, norm_g, dh, dy, f"{tag}_dnorm")
    return dx, dg, dw_up, dw_down


def _local_step(xc, tgt, W, HS, SBW):
    T, D = xc.shape
    SW = HS * SSD_HEAD_DIM
    CD = W["conv_b"].shape[-1]
    w_up, w_down, mlp_norm = W["w_up"], W["w_down"], W["mlp_norm"]

    h0 = _rms_fwd(xc, W["hyb_norm"], bf16, "hyb_norm")
    z = _mm(h0, W["w_z"], "nn", T, SW, D, (f32,), "proj_z")
    xraw = _mm(h0, W["w_xbc"], "nn", T, CD, D, (f32,), "proj_xbc")
    dtraw = _mm(h0, W["w_dt"], "nn", T, LANES, D, (f32,), "proj_dt")
    qkv = _mm(h0, W["w_qkv"], "nn", T, 3 * SBW, D, (f32,), "proj_qkv")
    xbc = _conv_fwd(xraw, W["conv_w"], W["conv_b"], "conv")
    y_ssd, yn_ssd, sprev = _ssd_fwd(xbc, dtraw, z, W["dt_bias"], W["a_log"], W["d_skip"], W["out_norm"], HS, "ssd")
    qn, kn, vb = _qk_norm_fwd(qkv, W["q_norm"], W["k_norm"], SBW, "qk_norm")
    y_sb, ctot = _sb_fwd(qn, kn, vb, "sb_attn")
    add = lambda acc, prev: (prev + acc,)
    mix = _mm(yn_ssd, W["w_out"], "nn", T, D, SW, (f32,), "out_ssd", epilogue=add, extras=((xc, "tile"),))
    x1 = _mm(y_sb, W["w_out"], "nn", T, D, SBW, (f32,), "out_sb", epilogue=add, extras=((mix, "tile"),), b_off=(SW, 0))
    x2, mlp0 = _mlp_fwd(x1, mlp_norm[0:1], w_up[0], w_down[0], "mlp0")

    hp = _rms_fwd(x2, W["pool_norm"], f32, "pool_norm")
    x3, yp, dpool = _pool_fwd(hp, x2, W["w_pool"], W["pool_b"], W["pool_scale"], "pool")
    x4, mlp1 = _mlp_fwd(x3, mlp_norm[1:2], w_up[1], w_down[1], "mlp1")

    dy, sq = _loss_grad(x4, tgt, "loss")

    dx3, dg_mlp1, dw_up1, dw_down1 = _mlp_bwd(dy, mlp1, mlp_norm[1:2], w_up[1], w_down[1], "mlp1")
    dhp, dw_pool, db_pool, dsc_pool = _pool_bwd(dx3, yp, dpool, W["w_pool"], W["pool_scale"], "pool_bwd")
    dx2, dg_pool = _rms_bwd(x2, W["pool_norm"], dhp, dx3, "pool_dnorm")
    dx1, dg_mlp0, dw_up0, dw_down0 = _mlp_bwd(dx2, mlp0, mlp_norm[0:1], w_up[0], w_down[0], "mlp0")

    dmerged = _mm(dx1, W["w_out"], "nt", T, SW + SBW, D, (f32,), "dmerged")
    dw_out = jnp.concatenate([_mm(yn_ssd, dx1, "tn", SW, D, T, (f32,), "dwout_ssd"),
                              _mm(y_sb, dx1, "tn", SBW, D, T, (f32,), "dwout_sb")], axis=0)
    dqn, dkn, dvv = _sb_bwd(qn, kn, vb, dmerged, ctot, SW, "sb_attn_bwd")
    dqkv, dg_q, dg_k = _qk_norm_bwd(qkv, dqn, dkn, dvv, W["q_norm"], W["k_norm"], SBW, "qk_norm_bwd")
    dy_ssd, dz, dg_on = _gate_bwd(y_ssd, z, dmerged, W["out_norm"], "gate_bwd")
    dxbc, ddtraw, dalog, dbias, ddskip = _ssd_bwd(xbc, dtraw, sprev, dy_ssd, W["dt_bias"], W["a_log"], W["d_skip"], HS, "ssd_bwd")
    dpre, dconv_w, dconv_b = _conv_bwd_pre(xraw, dxbc, W["conv_w"], W["conv_b"], "conv_bwd_pre")
    dxraw = _conv_bwd_in(dpre, W["conv_w"], "conv_bwd_in")
    dh0 = _mm(dz, W["w_z"], "nt", T, D, SW, (f32,), "dh0_z")
    dh0 = _mm(dxraw, W["w_xbc"], "nt", T, D, CD, (f32,), "dh0_xbc", epilogue=add, extras=((dh0, "tile"),))
    dh0 = _mm(ddtraw, W["w_dt"], "nt", T, D, LANES, (f32,), "dh0_dt", epilogue=add, extras=((dh0, "tile"),))
    dh0 = _mm(dqkv, W["w_qkv"], "nt", T, D, 3 * SBW, (f32,), "dh0_qkv", epilogue=add, extras=((dh0, "tile"),))
    dw_in = jnp.concatenate([
        _mm(h0, dz, "tn", D, SW, T, (f32,), "dwin_z"), _mm(h0, dxraw, "tn", D, CD, T, (f32,), "dwin_xbc"),
        _mm(h0, ddtraw, "tn", D, LANES, T, (f32,), "dwin_dt")[:, :HS], _mm(h0, dqkv, "tn", D, 3 * SBW, T, (f32,), "dwin_qkv")], axis=1)
    grad_x, dg_hyb = _rms_bwd(xc, W["hyb_norm"], dh0, dx1, "hyb_dnorm")
    grads = dict(w_in=dw_in, w_out=dw_out, w_pool=dw_pool, w_up=(dw_up0, dw_up1), w_down=(dw_down0, dw_down1),
                 hyb_norm=dg_hyb, conv_w=dconv_w, conv_b=dconv_b, dt_bias=dbias, a_log=dalog, d_skip=ddskip, out_norm=dg_on,
                 q_norm=dg_q, k_norm=dg_k, mlp_norm=(dg_mlp0, dg_mlp1), pool_norm=dg_pool, pool_b=db_pool, pool_scale=dsc_pool)
    return sq, grad_x, grads


def kernel(x, hyb_norm, hyb_w_in, ssd_conv_w, ssd_conv_b, ssd_dt_bias, ssd_a_log, ssd_d, ssd_out_norm, sb_q_norm, sb_k_norm, hyb_w_out, pool_norm, pool_w, pool_b, pool_scale, mlp_norm, mlp_w_up, mlp_w_down, loss_target, m_hyb_norm, m_hyb_w_in, m_ssd_conv_w, m_ssd_conv_b, m_ssd_dt_bias, m_ssd_a_log, m_ssd_d, m_ssd_out_norm, m_sb_q_norm, m_sb_k_norm, m_hyb_w_out, m_pool_norm, m_pool_w, m_pool_b, m_pool_scale, m_mlp_norm, m_mlp_w_up, m_mlp_w_down, v_hyb_norm, v_hyb_w_in, v_ssd_conv_w, v_ssd_conv_b, v_ssd_dt_bias, v_ssd_a_log, v_ssd_d, v_ssd_out_norm, v_sb_q_norm, v_sb_k_norm, v_hyb_w_out, v_pool_norm, v_pool_w, v_pool_b, v_pool_scale, v_mlp_norm, v_mlp_w_up, v_mlp_w_down):
    T, D = x.shape[1], x.shape[2]
    HS = ssd_dt_bias.shape[-1]
    SW = HS * SSD_HEAD_DIM
    CD = ssd_conv_b.shape[-1]
    IN = N_CHIPS * hyb_w_in.shape[-1]
    SBW = (IN - SW - CD - HS) // 3
    F = N_CHIPS * mlp_w_up.shape[-1]
    NL = mlp_norm.shape[0]
    PG = D // len(POOL_WINDOWS)
    xc, tgt = x[0], loss_target[0]
    ix, iy, ic = lax.axis_index("x"), lax.axis_index("y"), lax.axis_index("c")
    chip = (2 * ix + iy).astype(jnp.int32)
    chip_a, core_a = chip.reshape(1), ic.astype(jnp.int32).reshape(1)

    small = jnp.concatenate([ssd_conv_w.reshape(-1), pool_norm.reshape(-1), pool_b.reshape(-1), pool_scale.reshape(-1)])
    ns = small.shape[0]
    ns8 = -(-ns // (8 * LANES)) * LANES
    gathered, _ = _gather_small(jnp.pad(small, (0, 8 * ns8 - ns)).reshape(8, ns8), "gather_small")
    per_chip = gathered.reshape(N_CHIPS, 2, 8 * ns8)[:, 0, :ns]
    cw = CD // N_CHIPS
    conv_w = per_chip[:, :4 * cw].reshape(N_CHIPS, 4, cw).transpose(1, 0, 2).reshape(4, CD)
    pvec = per_chip[:, 4 * cw:].reshape(N_CHIPS, 3, PG)
    pool_norm_f, pool_b_f, pool_scale_f = (pvec[:, i].reshape(1, D) for i in range(3))

    big = [hyb_w_in[0], hyb_w_out[0], pool_w[0].reshape(-1, PG), mlp_w_up.reshape(-1, mlp_w_up.shape[-1]),
           mlp_w_down.reshape(-1, D)]
    g_in, g_out, g_pool, g_up, g_down = _gather_weights([b.astype(bf16) for b in big], "gather_weights")
    w_in = g_in.transpose(1, 0, 2).reshape(D, IN)
    c1, c2, c3 = SW, SW + CD, SW + CD + HS
    w_z, w_xbc, w_dt, w_qkv = w_in[:, :c1], w_in[:, c1:c2], _pad_lanes(w_in[:, c2:c3]), w_in[:, c3:]
    w_out = g_out.reshape(SW + SBW, D)
    w_pool = g_pool.reshape(N_CHIPS, len(POOL_WINDOWS), PG // N_CHIPS, PG).transpose(1, 0, 2, 3).reshape(len(POOL_WINDOWS), PG, PG)
    fs = F // N_CHIPS
    w_up = [g_up.reshape(N_CHIPS, NL, D, fs)[:, l].transpose(1, 0, 2).reshape(D, F) for l in range(NL)]
    w_down = [g_down.reshape(N_CHIPS, NL, fs, D)[:, l].reshape(F, D) for l in range(NL)]

    dt_bias_p, a_log_p, d_skip_p = _pad_lanes(ssd_dt_bias), _pad_lanes(ssd_a_log), _pad_lanes(ssd_d)

    full = dict(hyb_norm=hyb_norm, w_z=w_z, w_xbc=w_xbc, w_dt=w_dt, w_qkv=w_qkv, conv_w=conv_w, conv_b=ssd_conv_b,
                dt_bias=dt_bias_p, a_log=a_log_p, d_skip=d_skip_p, out_norm=ssd_out_norm, q_norm=sb_q_norm, k_norm=sb_k_norm,
                w_out=w_out, pool_norm=pool_norm_f, w_pool=w_pool, pool_b=pool_b_f, pool_scale=pool_scale_f,
                mlp_norm=mlp_norm, w_up=w_up, w_down=w_down)
    sq, grad_x, gr = _local_step(xc, tgt, full, HS, SBW)
    loss = lax.psum(sq[0, 0] * (0.5 / D), ("x", "y", "c"))
    dw_in, dw_out, dw_pool = gr["w_in"], gr["w_out"], gr["w_pool"]
    (dw_up0, dw_up1), (dw_down0, dw_down1) = gr["w_up"], gr["w_down"]
    dg_hyb, dconv_b, dbias, dalog, ddskip, dg_on, dg_q, dg_k = (gr[k] for k in (
        "hyb_norm", "conv_b", "dt_bias", "a_log", "d_skip", "out_norm", "q_norm", "k_norm"))
    (dg_mlp0, dg_mlp1), dconv_w, dg_pool, db_pool, dsc_pool = gr["mlp_norm"], gr["conv_w"], gr["pool_norm"], gr["pool_b"], gr["pool_scale"]


    g4 = [dw_in.reshape(D, N_CHIPS, IN // N_CHIPS).transpose(1, 0, 2),
          dw_out.reshape(N_CHIPS, (SW + SBW) // N_CHIPS, D),
          dw_pool.reshape(len(POOL_WINDOWS), N_CHIPS, PG // N_CHIPS, PG).transpose(1, 0, 2, 3).reshape(N_CHIPS, PG, PG),
          jnp.stack([dw_up0, dw_up1]).reshape(NL, D, N_CHIPS, fs).transpose(2, 0, 1, 3).reshape(N_CHIPS, NL * D, fs),
          jnp.stack([dw_down0, dw_down1]).reshape(NL, N_CHIPS, fs, D).transpose(1, 0, 2, 3).reshape(N_CHIPS, NL * fs, D)]
    from_sibling = _swap_halves(g4, "grads_pair_swap")
    h4 = [_pair_sum(g, r, core_a, f"grads_pair_sum{i}") for i, (g, r) in enumerate(zip(g4, from_sibling))]
    from_chips = _exchange_chips(h4, "grads_chip_exchange")
    halves = [_chip_sum(h, r, chip_a, f"grads_chip_sum{i}") for i, (h, r) in enumerate(zip(h4, from_chips))]
    gb_in, gb_out, gb_pool, gb_up, gb_down = _join_halves(halves, "grads_join")

    full_small = [dg_hyb, dconv_b, dbias[:, :HS], dalog[:, :HS], ddskip[:, :HS], dg_on, dg_q, dg_k,
                  jnp.concatenate([dg_mlp0, dg_mlp1], axis=0).reshape(1, -1),
                  dconv_w.reshape(1, -1), dg_pool, db_pool, dsc_pool]
    sizes = [v.shape[-1] for v in full_small]
    packed = jnp.concatenate([v.reshape(-1) for v in full_small])
    npk = packed.shape[0]
    npk8 = -(-npk // (8 * LANES)) * LANES
    _, summed = _gather_small(jnp.pad(packed, (0, 8 * npk8 - npk)).reshape(8, npk8), "grads_small")
    summed = summed.reshape(-1)[:npk]
    offs = [0]
    for s in sizes:
        offs.append(offs[-1] + s)
    (g_hyb_norm, g_conv_b, g_dt_bias, g_a_log, g_d, g_out_norm, g_q_norm, g_k_norm, g_mlp_norm, g_conv_w_full,
     g_pool_norm_full, g_pool_b_full, g_pool_scale_full) = (summed[offs[i]:offs[i + 1]] for i in range(len(sizes)))
    take = lambda full, n: lax.dynamic_slice_in_dim(full.reshape(-1, N_CHIPS, n), chip, 1, axis=1)
    small_grads = {
        "hyb_norm": g_hyb_norm.reshape(hyb_norm.shape), "ssd_conv_w": take(g_conv_w_full, cw).reshape(ssd_conv_w.shape),
        "ssd_conv_b": g_conv_b.reshape(ssd_conv_b.shape), "ssd_dt_bias": g_dt_bias.reshape(ssd_dt_bias.shape),
        "ssd_a_log": g_a_log.reshape(ssd_a_log.shape), "ssd_d": g_d.reshape(ssd_d.shape),
        "ssd_out_norm": g_out_norm.reshape(ssd_out_norm.shape), "sb_q_norm": g_q_norm.reshape(sb_q_norm.shape),
        "sb_k_norm": g_k_norm.reshape(sb_k_norm.shape), "pool_norm": take(g_pool_norm_full, PG).reshape(pool_norm.shape),
        "pool_b": take(g_pool_b_full, PG).reshape(pool_b.shape), "pool_scale": take(g_pool_scale_full, PG).reshape(pool_scale.shape),
        "mlp_norm": g_mlp_norm.reshape(mlp_norm.shape),
    }

    weights = dict(hyb_norm=hyb_norm, hyb_w_in=hyb_w_in, ssd_conv_w=ssd_conv_w, ssd_conv_b=ssd_conv_b, ssd_dt_bias=ssd_dt_bias,
                   ssd_a_log=ssd_a_log, ssd_d=ssd_d, ssd_out_norm=ssd_out_norm, sb_q_norm=sb_q_norm, sb_k_norm=sb_k_norm,
                   hyb_w_out=hyb_w_out, pool_norm=pool_norm, pool_w=pool_w, pool_b=pool_b, pool_scale=pool_scale,
                   mlp_norm=mlp_norm, mlp_w_up=mlp_w_up, mlp_w_down=mlp_w_down)
    moms = dict(hyb_norm=m_hyb_norm, hyb_w_in=m_hyb_w_in, ssd_conv_w=m_ssd_conv_w, ssd_conv_b=m_ssd_conv_b, ssd_dt_bias=m_ssd_dt_bias,
                ssd_a_log=m_ssd_a_log, ssd_d=m_ssd_d, ssd_out_norm=m_ssd_out_norm, sb_q_norm=m_sb_q_norm, sb_k_norm=m_sb_k_norm,
                hyb_w_out=m_hyb_w_out, pool_norm=m_pool_norm, pool_w=m_pool_w, pool_b=m_pool_b, pool_scale=m_pool_scale,
                mlp_norm=m_mlp_norm, mlp_w_up=m_mlp_w_up, mlp_w_down=m_mlp_w_down)
    vels = dict(hyb_norm=v_hyb_norm, hyb_w_in=v_hyb_w_in, ssd_conv_w=v_ssd_conv_w, ssd_conv_b=v_ssd_conv_b, ssd_dt_bias=v_ssd_dt_bias,
                ssd_a_log=v_ssd_a_log, ssd_d=v_ssd_d, ssd_out_norm=v_ssd_out_norm, sb_q_norm=v_sb_q_norm, sb_k_norm=v_sb_k_norm,
                hyb_w_out=v_hyb_w_out, pool_norm=v_pool_norm, pool_w=v_pool_w, pool_b=v_pool_b, pool_scale=v_pool_scale,
                mlp_norm=v_mlp_norm, mlp_w_up=v_mlp_w_up, mlp_w_down=v_mlp_w_down)
    order = list(weights)
    grads, delta, new_m, new_v = {}, {}, {}, {}
    for name, g2 in (("hyb_w_in", gb_in), ("hyb_w_out", gb_out), ("pool_w", gb_pool), ("mlp_w_up", gb_up), ("mlp_w_down", gb_down)):
        shp = weights[name].shape
        d_, m_, v_ = _adamw(weights[name].reshape(g2.shape), g2, moms[name].reshape(g2.shape), vels[name].reshape(g2.shape),
                            f"adamw_{name}")
        grads[name], delta[name], new_m[name], new_v[name] = (t.reshape(shp) for t in (g2, d_, m_, v_))
    snames = list(small_grads)
    pack = lambda d: jnp.concatenate([d[n].reshape(-1) for n in snames])
    nsm = sum(small_grads[n].size for n in snames)
    cols = -(-nsm // (8 * LANES)) * LANES
    as_blk = lambda v: jnp.pad(v, (0, 8 * cols - nsm)).reshape(8, cols)
    padded_v = jnp.pad(pack(vels), (0, 8 * cols - nsm), constant_values=1.0).reshape(8, cols)
    d_, m_, v_ = _adamw(as_blk(pack(weights)), as_blk(pack(small_grads)), as_blk(pack(moms)), padded_v, "adamw_small")
    off = 0
    for n in snames:
        sz, shp = small_grads[n].size, weights[n].shape
        grads[n] = small_grads[n]
        delta[n], new_m[n], new_v[n] = (t.reshape(-1)[off:off + sz].reshape(shp) for t in (d_, m_, v_))
        off += sz

    return (loss, grad_x.reshape(x.shape), *[grads[n] for n in order], *[delta[n] for n in order],
            *[new_m[n] for n in order], *[new_v[n] for n in order])
```

```python
import functools
import math

import jax
import jax.numpy as jnp
from jax import lax
from jax.experimental import pallas as pl
from jax.experimental.pallas import tpu as pltpu

f32 = jnp.float32
bf16 = jnp.bfloat16

EPS = 1e-6
SSD_HEAD_DIM = 64
SSD_STATE = 128
SSD_GROUPS = 4
SSD_CHUNK = 128
LANES = 128
SB_HEAD_DIM = 128
POOL_WINDOWS = (2, 4, 8, 16)
POOL_HALO = 16
CONV_HALO = 8
ADAM_LR, ADAM_B1, ADAM_B2, ADAM_EPS, ADAM_WD, ADAM_STEP = 0.001, 0.9, 0.999, 1e-08, 0.01, 10
VMEM_LIMIT = 56 * 1024 * 1024
N_CHIPS = 4
MESH = pl.DeviceIdType.MESH

_DIMS = {"nn": (((1,), (0,)), ((), ())), "nt": (((1,), (1,)), ((), ())), "tn": (((0,), (0,)), ((), ()))}


def _fit(n, t):
    if n <= t:
        return n
    return max(d for d in range(LANES, t + 1, LANES) if n % d == 0)


def _cp(*sem):
    return pltpu.CompilerParams(dimension_semantics=sem, vmem_limit_bytes=VMEM_LIMIT)


def _sigmoid(v):
    return 1.0 / (1.0 + jnp.exp(-v))


def _softplus(v):
    return jnp.maximum(v, 0.0) + jnp.log(1.0 + jnp.exp(-jnp.abs(v)))


def _split(v, parts):
    out, rem = [], v
    for _ in range(parts):
        p = rem.astype(bf16)
        out.append(p)
        rem = rem - p.astype(f32)
    return out


def _dot(a, b, mode="nn"):
    return lax.dot_general(a, b, _DIMS[mode], preferred_element_type=f32)


def _mask_dot(mask_b, v, parts, mode="nn"):
    acc = None
    for p in _split(v, parts):
        t = _dot(mask_b, p, mode)
        acc = t if acc is None else acc + t
    return acc


def _dot_mask(v, mask_b, parts):
    acc = None
    for p in _split(v, parts):
        t = _dot(p, mask_b)
        acc = t if acc is None else acc + t
    return acc


def _stacked(view, br, bc, rmap, cmap):
    kind, per, layer, rows_per_layer = view
    if kind == "cols":
        npc = per // bc
        return pl.BlockSpec((None, br, bc), lambda i, j, k: (cmap(i, j, k) // npc, layer * (rows_per_layer // br) + rmap(i, j, k),
                                                              cmap(i, j, k) % npc))
    npc = per // br
    return pl.BlockSpec((None, br, bc), lambda i, j, k: (rmap(i, j, k) // npc, layer * npc + rmap(i, j, k) % npc, cmap(i, j, k)))


def _mm(a, b, mode, M, N, K, outs, name, epilogue=None, extras=(), a_off=(0, 0), b_off=(0, 0),
        tm=1024, tn=1024, tk=1024, b_view=None, out_view=None, out_stack=None, alias=None, comm=None):
    if b_view is not None:
        if (b_view[0] == "cols") == (mode != "nt"):
            tn = min(tn, b_view[1])
        else:
            tk = min(tk, b_view[1])
    if out_view is not None:
        tm, tn = (tm, min(tn, out_view[1])) if out_view[0] == "cols" else (min(tm, out_view[1]), tn)
    tm, tn, tk = _fit(M, tm), _fit(N, tn), _fit(K, tk)
    nk = K // tk
    if mode == "tn":
        a_blk, ad = (tk, tm), (tk, tm)
    else:
        a_blk, ad = (tm, tk), (tm, tk)
    b_blk = (tn, tk) if mode == "nt" else (tk, tn)
    assert a_off[0] % ad[0] == 0 and a_off[1] % ad[1] == 0 and b_off[0] % b_blk[0] == 0 and b_off[1] % b_blk[1] == 0
    ao = (a_off[0] // ad[0], a_off[1] // ad[1])
    bo = (b_off[0] // b_blk[0], b_off[1] // b_blk[1])
    if mode == "tn":
        a_map = lambda i, j, k: (k + ao[0], i + ao[1])
    else:
        a_map = lambda i, j, k: (i + ao[0], k + ao[1])
    if mode == "nt":
        b_map = lambda i, j, k: (j + bo[0], k + bo[1])
    else:
        b_map = lambda i, j, k: (k + bo[0], j + bo[1])
    if b_view is not None:
        if mode == "nt":
            b_spec = _stacked(b_view, tn, tk, lambda i, j, k: j, lambda i, j, k: k)
        else:
            b_spec = _stacked(b_view, tk, tn, lambda i, j, k: k, lambda i, j, k: j)
    else:
        b_spec = pl.BlockSpec(b_blk, b_map)
    in_specs = [pl.BlockSpec(a_blk, a_map), b_spec]
    for arr, kind in extras:
        if kind == "tile":
            in_specs.append(pl.BlockSpec((tm, tn), lambda i, j, k: (i, j)))
        else:
            in_specs.append(pl.BlockSpec((1, tn), lambda i, j, k: (0, j)))
    ne, no = len(extras), len(outs)
    if epilogue is None:
        epilogue = lambda acc: (acc,)
    operands = [a, b, *[e[0] for e in extras]]
    aliases = {}
    if alias is not None:
        in_specs.append(ANY)
        aliases[len(operands)] = 0
        operands.append(alias)
    n_in = len(operands)
    if out_view is not None:
        out_specs = [_stacked(out_view, tm, tn, lambda i, j, k: i, lambda i, j, k: j)]
        out_shape = [jax.ShapeDtypeStruct(out_stack, outs[0])]
    else:
        out_specs = [pl.BlockSpec((tm, tn), lambda i, j, k: (i, j)) for _ in outs]
        out_shape = [jax.ShapeDtypeStruct((M, N), dt) for dt in outs]
    scratch = [pltpu.VMEM((tm, tn), f32)] if nk > 1 else []
    grid = (M // tm, N // tn, nk)
    if comm is not None:
        in_specs += [ANY] * len(comm["ins"])
        operands += comm["ins"]
        out_specs += [ANY] * len(comm["outs"])
        out_shape += comm["outs"]
        scratch += comm["sems"]
    nci, nco, ncs = (len(comm["ins"]), len(comm["outs"]), len(comm["sems"])) if comm is not None else (0, 0, 0)

    def body(*refs):
        a_ref, b_ref = refs[0], refs[1]
        ex, out_refs = refs[2:2 + ne], refs[n_in + nci:n_in + nci + no]
        rest = refs[n_in + nci + no + nco:]
        if comm is not None:
            cargs = (refs[n_in:n_in + nci], refs[n_in + nci + no:n_in + nci + no + nco], refs[len(refs) - ncs:])
            pid = [pl.program_id(d) for d in range(3)]

            @pl.when((pid[0] == 0) & (pid[1] == 0) & (pid[2] == 0))
            def _():
                comm["start"](*cargs)

        def finish(acc):
            res = epilogue(acc, *[e[...] for e in ex])
            for o, r in zip(out_refs, res):
                o[...] = r.astype(o.dtype)

        prod = lax.dot_general(a_ref[...].astype(bf16), b_ref[...].astype(bf16), _DIMS[mode],
                               preferred_element_type=f32)
        if nk == 1:
            finish(prod)
        else:
            acc_ref = rest[0]
            k = pl.program_id(2)

            @pl.when(k == 0)
            def _():
                acc_ref[...] = prod

            @pl.when(k > 0)
            def _():
                acc_ref[...] += prod

            @pl.when(k == nk - 1)
            def _():
                finish(acc_ref[...])

        if comm is not None:
            @pl.when((pid[0] == grid[0] - 1) & (pid[1] == grid[1] - 1) & (pid[2] == grid[2] - 1))
            def _():
                comm["finish"](*cargs)

    sem = ("arbitrary",) * 3 if comm is not None else ("parallel", "parallel", "arbitrary")
    res = pl.pallas_call(
        body, name=name, grid=grid, in_specs=in_specs, out_specs=out_specs, out_shape=out_shape,
        scratch_shapes=scratch, input_output_aliases=aliases, compiler_params=_cp(*sem),
    )(*operands)
    if comm is not None:
        return res[:no], res[no:]
    return res[0] if no == 1 else res


def _rms_fwd(x, g, out_dtype, name, tr=256):
    T, D = x.shape
    tr = min(tr, T)

    def body(x_ref, g_ref, o_ref):
        xv = x_ref[...]
        r = lax.rsqrt(jnp.mean(xv * xv, axis=-1, keepdims=True) + EPS)
        o_ref[...] = (xv * r * g_ref[...]).astype(o_ref.dtype)

    return pl.pallas_call(
        body, name=name, grid=(T // tr,),
        in_specs=[pl.BlockSpec((tr, D), lambda r: (r, 0)), pl.BlockSpec((1, D), lambda r: (0, 0))],
        out_specs=pl.BlockSpec((tr, D), lambda r: (r, 0)),
        out_shape=jax.ShapeDtypeStruct((T, D), out_dtype), compiler_params=_cp("parallel"),
    )(x, g)


def _rms_bwd(x, g, dh, dres, name, tr=256):
    T, D = x.shape
    tr = min(tr, T)

    def body(x_ref, g_ref, dh_ref, dres_ref, dx_ref, dg_ref):
        xv = x_ref[...]
        r = lax.rsqrt(jnp.mean(xv * xv, axis=-1, keepdims=True) + EPS)
        xh = xv * r
        dhv = dh_ref[...]
        dhg = dhv * g_ref[...]
        dx_ref[...] = dres_ref[...] + r * (dhg - xh * jnp.mean(dhg * xh, axis=-1, keepdims=True))

        @pl.when(pl.program_id(0) == 0)
        def _():
            dg_ref[...] = jnp.zeros_like(dg_ref)

        dg_ref[...] += jnp.sum(dhv * xh, axis=0, keepdims=True)

    row = pl.BlockSpec((tr, D), lambda r: (r, 0))
    vec = pl.BlockSpec((1, D), lambda r: (0, 0))
    return pl.pallas_call(
        body, name=name, grid=(T // tr,), in_specs=[row, vec, row, row], out_specs=[row, vec],
        out_shape=[jax.ShapeDtypeStruct((T, D), f32), jax.ShapeDtypeStruct((1, D), f32)],
        compiler_params=_cp("arbitrary"),
    )(x, g, dh, dres)


def _loss_grad(y, tgt, name, tr=256):
    T, D = y.shape
    tr = min(tr, T)

    def body(y_ref, t_ref, dy_ref, s_ref):
        e = y_ref[...] - t_ref[...]
        dy_ref[...] = e * (1.0 / D)

        @pl.when(pl.program_id(0) == 0)
        def _():
            s_ref[...] = jnp.zeros_like(s_ref)

        s_ref[...] += jnp.sum(e * e)

    row = pl.BlockSpec((tr, D), lambda r: (r, 0))
    return pl.pallas_call(
        body, name=name, grid=(T // tr,), in_specs=[row, row],
        out_specs=[row, pl.BlockSpec((8, LANES), lambda r: (0, 0))],
        out_shape=[jax.ShapeDtypeStruct((T, D), f32), jax.ShapeDtypeStruct((8, LANES), f32)],
        compiler_params=_cp("arbitrary"),
    )(y, tgt)


def _shift_down(cur, prev, s):
    rolled = pltpu.roll(cur, s, 0)
    top = pltpu.roll(prev, s, 0)
    row = lax.broadcasted_iota(jnp.int32, top.shape, 0)
    head = jnp.where(row < s, top, rolled[0:CONV_HALO])
    return jnp.concatenate([head, rolled[CONV_HALO:]], axis=0)


def _shift_up(cur, nxt, s):
    n = cur.shape[0]
    rolled = pltpu.roll(cur, n - s, 0)
    bot = pltpu.roll(nxt, CONV_HALO - s, 0)
    row = lax.broadcasted_iota(jnp.int32, bot.shape, 0)
    tail = jnp.where(row >= CONV_HALO - s, bot, rolled[n - CONV_HALO:])
    return jnp.concatenate([rolled[:n - CONV_HALO], tail], axis=0)


def _conv_pre(cur, prev, w_ref, b_ref):
    taps = [cur] + [_shift_down(cur, prev, s) for s in (1, 2, 3)]
    pre = b_ref[...] + w_ref[3:4, :] * taps[0]
    for s in (1, 2, 3):
        pre = pre + w_ref[3 - s:4 - s, :] * taps[s]
    return pre, taps


def _conv_specs(T, C, rc, cb):
    cur = pl.BlockSpec((rc, cb), lambda j, r: (r, j))
    prev = pl.BlockSpec((CONV_HALO, cb), lambda j, r: (jnp.maximum(r * (rc // CONV_HALO) - 1, 0), j))
    nxt = pl.BlockSpec((CONV_HALO, cb), lambda j, r: (jnp.minimum((r + 1) * (rc // CONV_HALO), T // CONV_HALO - 1), j))
    w = pl.BlockSpec((4, cb), lambda j, r: (0, j))
    b = pl.BlockSpec((1, cb), lambda j, r: (0, j))
    return cur, prev, nxt, w, b


def _conv_fwd(xraw, w, b, name):
    T, C = xraw.shape
    rc, cb = min(512, T), min(512, C)
    cur, prev, _, ws, bs = _conv_specs(T, C, rc, cb)

    def body(x_ref, p_ref, w_ref, b_ref, o_ref):
        pv = jnp.where(pl.program_id(1) > 0, p_ref[...], 0.0)
        pre, _ = _conv_pre(x_ref[...], pv, w_ref, b_ref)
        o_ref[...] = pre * _sigmoid(pre)

    return pl.pallas_call(
        body, name=name, grid=(C // cb, T // rc), in_specs=[cur, prev, ws, bs], out_specs=cur,
        out_shape=jax.ShapeDtypeStruct((T, C), f32), compiler_params=_cp("parallel", "parallel"),
    )(xraw, xraw, w, b)


def _conv_bwd_pre(xraw, dxbc, w, b, name):
    T, C = xraw.shape
    rc, cb = min(512, T), min(512, C)
    cur, prev, _, ws, bs = _conv_specs(T, C, rc, cb)

    def body(x_ref, p_ref, d_ref, w_ref, b_ref, dpre_ref, dw_ref, db_ref):
        pv = jnp.where(pl.program_id(1) > 0, p_ref[...], 0.0)
        pre, taps = _conv_pre(x_ref[...], pv, w_ref, b_ref)
        sg = _sigmoid(pre)
        dpre = d_ref[...] * (sg * (1.0 + pre * (1.0 - sg)))
        dpre_ref[...] = dpre

        @pl.when(pl.program_id(1) == 0)
        def _():
            dw_ref[...] = jnp.zeros_like(dw_ref)
            db_ref[...] = jnp.zeros_like(db_ref)

        row = lax.broadcasted_iota(jnp.int32, dw_ref.shape, 0)
        upd = jnp.zeros(dw_ref.shape, f32)
        for s in range(4):
            upd = upd + jnp.where(row == 3 - s, jnp.sum(dpre * taps[s], axis=0, keepdims=True), 0.0)
        dw_ref[...] += upd
        db_ref[...] += jnp.sum(dpre, axis=0, keepdims=True)

    return pl.pallas_call(
        body, name=name, grid=(C // cb, T // rc), in_specs=[cur, prev, cur, ws, bs], out_specs=[cur, ws, bs],
        out_shape=[jax.ShapeDtypeStruct((T, C), f32), jax.ShapeDtypeStruct((4, C), f32), jax.ShapeDtypeStruct((1, C), f32)],
        compiler_params=_cp("parallel", "arbitrary"),
    )(xraw, xraw, dxbc, w, b)


def _conv_bwd_in(dpre, w, name):
    T, C = dpre.shape
    rc, cb = min(512, T), min(512, C)
    cur, _, nxt, ws, _ = _conv_specs(T, C, rc, cb)
    nr = T // rc

    def body(d_ref, n_ref, w_ref, o_ref):
        nv = jnp.where(pl.program_id(1) < nr - 1, n_ref[...], 0.0)
        cv = d_ref[...]
        out = w_ref[3:4, :] * cv
        for s in (1, 2, 3):
            out = out + w_ref[3 - s:4 - s, :] * _shift_up(cv, nv, s)
        o_ref[...] = out.astype(o_ref.dtype)

    return pl.pallas_call(
        body, name=name, grid=(C // cb, nr), in_specs=[cur, nxt, ws], out_specs=cur,
        out_shape=jax.ShapeDtypeStruct((T, C), bf16), compiler_params=_cp("parallel", "parallel"),
    )(dpre, dpre, w)


def _ssd_common(dtr_ref, bias_ref, alog_ref):
    L = SSD_CHUNK
    xs = dtr_ref[...] + bias_ref[...]
    dt = _softplus(xs)
    a = -jnp.exp(alog_ref[...])
    row = lax.broadcasted_iota(jnp.int32, (L, L), 0)
    col = lax.broadcasted_iota(jnp.int32, (L, L), 1)
    causal = row >= col
    cs = _mask_dot(causal.astype(bf16), dt * a, 3)
    cs_last = cs[L - 1:L, :]
    return xs, dt, a, causal, cs, cs.T, jnp.exp(cs), jnp.exp(cs_last - cs), jnp.exp(cs_last)


def _ssd_fwd(xbc, dtraw, z, dt_bias, a_log, d_skip, out_norm, HS, name, comm=None):
    T = xbc.shape[0]
    L, P, NS, G = SSD_CHUNK, SSD_HEAD_DIM, SSD_STATE, SSD_GROUPS
    SW, HPG, nc = HS * P, HS // SSD_GROUPS, T // SSD_CHUNK
    gsz = SW // G

    def body(*refs):
        (xbc_ref, dtr_ref, z_ref, bias_ref, alog_ref, dsk_ref, on_ref), (y_ref, yn_ref, sp_ref), (st_ref,), cargs = _hosted(comm, 7, 3, refs)
        first, last = _first_last((nc,))
        if comm is not None:
            @pl.when(first)
            def _():
                comm["start"](*cargs)

            @pl.when(last)
            def _():
                comm["finish"](*cargs)

        @pl.when(pl.program_id(0) == 0)
        def _():
            st_ref[...] = jnp.zeros_like(st_ref)

        sp_ref[0] = st_ref[...]
        _, dt, _, causal, cs, csT, ecs, dte, cdec = _ssd_common(dtr_ref, bias_ref, alog_ref)
        for g in range(G):
            Bb = xbc_ref[:, SW + g * NS:SW + (g + 1) * NS].astype(bf16)
            Cb = xbc_ref[:, SW + (G + g) * NS:SW + (G + g + 1) * NS].astype(bf16)
            Gm = _dot(Cb, Bb, "nt")
            for r in range(HPG):
                h = g * HPG + r
                hs = slice(h * P, (h + 1) * P)
                Xh = xbc_ref[:, hs]
                Xd = Xh * dt[:, h:h + 1]
                seg = cs[:, h:h + 1] - csT[h:h + 1, :]
                Lm = jnp.where(causal, jnp.exp(jnp.minimum(seg, 0.0)), 0.0)
                yd = _dot((Lm * Gm).astype(bf16), Xd.astype(bf16))
                Sp = st_ref[:, hs]
                yo = _dot(Cb, Sp.astype(bf16)) * ecs[:, h:h + 1]
                y_ref[:, hs] = yd + yo + dsk_ref[:, h:h + 1] * Xh
                st_ref[:, hs] = cdec[:, h:h + 1] * Sp + _dot(Bb, (Xd * dte[:, h:h + 1]).astype(bf16), "tn")
        zz = z_ref[...]
        gated = y_ref[...] * (zz * _sigmoid(zz))
        for g in range(G):
            gs = slice(g * gsz, (g + 1) * gsz)
            sg = gated[:, gs]
            rr = lax.rsqrt(jnp.mean(sg * sg, axis=-1, keepdims=True) + EPS)
            yn_ref[:, gs] = (sg * rr * on_ref[:, gs]).astype(yn_ref.dtype)

    vec = pl.BlockSpec((1, LANES), lambda c: (0, 0))
    (y, yn, sp), got = _host_call(
        body, name, (nc,),
        [pl.BlockSpec((L, xbc.shape[1]), lambda c: (c, 0)), pl.BlockSpec((L, LANES), lambda c: (c, 0)),
         pl.BlockSpec((L, SW), lambda c: (c, 0)), vec, vec, vec, pl.BlockSpec((1, SW), lambda c: (0, 0))],
        [pl.BlockSpec((L, SW), lambda c: (c, 0)), pl.BlockSpec((L, SW), lambda c: (c, 0)),
         pl.BlockSpec((1, NS, SW), lambda c: (c, 0, 0))],
        [jax.ShapeDtypeStruct((T, SW), f32), jax.ShapeDtypeStruct((T, SW), bf16), jax.ShapeDtypeStruct((nc, NS, SW), f32)],
        [pltpu.VMEM((NS, SW), f32)], [xbc, dtraw, z, dt_bias, a_log, d_skip, out_norm], comm)
    return y, yn, sp, got


def _ssd_bwd(xbc, dtraw, sprev, dy, dt_bias, a_log, d_skip, HS, name):
    T = xbc.shape[0]
    L, P, NS, G = SSD_CHUNK, SSD_HEAD_DIM, SSD_STATE, SSD_GROUPS
    SW, HPG, nc = HS * P, HS // SSD_GROUPS, T // SSD_CHUNK

    def body(xbc_ref, dtr_ref, sp_ref, dy_ref, bias_ref, alog_ref, dsk_ref,
             dxbc_ref, ddtr_ref, dalog_ref, dbias_ref, dd_ref, ds_ref):
        @pl.when(pl.program_id(0) == 0)
        def _():
            ds_ref[...] = jnp.zeros_like(ds_ref)
            dalog_ref[...] = jnp.zeros_like(dalog_ref)
            dbias_ref[...] = jnp.zeros_like(dbias_ref)
            dd_ref[...] = jnp.zeros_like(dd_ref)

        xs, dt, a, causal, cs, csT, ecs, dte, cdec = _ssd_common(dtr_ref, bias_ref, alog_ref)
        lane = lax.broadcasted_iota(jnp.int32, (L, LANES), 1)
        sub = lax.broadcasted_iota(jnp.int32, (LANES, L), 0)
        lane1 = lax.broadcasted_iota(jnp.int32, (1, LANES), 1)
        dcs = jnp.zeros((L, LANES), f32)
        dcs_t = jnp.zeros((LANES, L), f32)
        xds = jnp.zeros((L, LANES), f32)
        dlast = jnp.zeros((1, LANES), f32)
        dD = jnp.zeros((1, LANES), f32)
        for g in range(G):
            bsl = slice(SW + g * NS, SW + (g + 1) * NS)
            csl = slice(SW + (G + g) * NS, SW + (G + g + 1) * NS)
            Bb = xbc_ref[:, bsl].astype(bf16)
            Cb = xbc_ref[:, csl].astype(bf16)
            Gm = _dot(Cb, Bb, "nt")
            dG = jnp.zeros((L, L), f32)
            dB = jnp.zeros((L, NS), f32)
            dC = jnp.zeros((L, NS), f32)
            for r in range(HPG):
                h = g * HPG + r
                hs = slice(h * P, (h + 1) * P)
                Xh = xbc_ref[:, hs]
                dth = dt[:, h:h + 1]
                Xd = Xh * dth
                Xdb = Xd.astype(bf16)
                seg = cs[:, h:h + 1] - csT[h:h + 1, :]
                Lm = jnp.where(causal, jnp.exp(jnp.minimum(seg, 0.0)), 0.0)
                Mf = Lm * Gm
                dYh = dy_ref[:, hs]
                dYb = dYh.astype(bf16)
                Sp = sp_ref[0, :, hs]
                Spb = Sp.astype(bf16)
                dSh = ds_ref[:, hs]
                dSb = dSh.astype(bf16)
                ecs_h, dte_h, cdec_h = ecs[:, h:h + 1], dte[:, h:h + 1], cdec[:, h:h + 1]
                dM = _dot(dYb, Xdb, "nt")
                dXd = _dot(Mf.astype(bf16), dYb, "tn")
                Q = dM * Mf
                dcs = dcs + jnp.where(lane == h, jnp.sum(Q, axis=1, keepdims=True), 0.0)
                dcs_t = dcs_t - jnp.where(sub == h, jnp.sum(Q, axis=0, keepdims=True), 0.0)
                dG = dG + dM * Lm
                CS = _dot(Cb, Spb)
                dcs = dcs + jnp.where(lane == h, jnp.sum(dYh * CS, axis=1, keepdims=True) * ecs_h, 0.0)
                Wb = (dYh * ecs_h).astype(bf16)
                dC = dC + _dot(Wb, Spb, "nt")
                Zb = _dot(Bb, dSb)
                dXd = dXd + dte_h * Zb
                Tt = dte_h * jnp.sum(Xd * Zb, axis=1, keepdims=True)
                dcs = dcs - jnp.where(lane == h, Tt, 0.0)
                dlast = dlast + jnp.where(lane1 == h, jnp.sum(Tt) + cdec_h * jnp.sum(Sp * dSh), 0.0)
                dB = dB + _dot((Xd * dte_h).astype(bf16), dSb, "nt")
                dxbc_ref[:, hs] = dXd * dth + dsk_ref[:, h:h + 1] * dYh
                xds = xds + jnp.where(lane == h, jnp.sum(dXd * Xh, axis=1, keepdims=True), 0.0)
                dD = dD + jnp.where(lane1 == h, jnp.sum(dYh * Xh), 0.0)
                ds_ref[:, hs] = cdec_h * dSh + _dot(Cb, Wb, "tn")
            dGb = dG.astype(bf16)
            dxbc_ref[:, bsl] = dB + _dot(dGb, Cb, "tn")
            dxbc_ref[:, csl] = dC + _dot(dGb, Bb)
        rowi = lax.broadcasted_iota(jnp.int32, (L, LANES), 0)
        dcs = dcs + dcs_t.T + jnp.where(rowi == L - 1, dlast, 0.0)
        anti = (lax.broadcasted_iota(jnp.int32, (L, L), 1) >= lax.broadcasted_iota(jnp.int32, (L, L), 0)).astype(bf16)
        dda = _mask_dot(anti, dcs, 3)
        ddt = dda * a + xds
        dalog_ref[...] += jnp.sum(dda * dt, axis=0, keepdims=True) * a
        ddtr = ddt * _sigmoid(xs)
        ddtr_ref[...] = ddtr
        dbias_ref[...] += jnp.sum(ddtr, axis=0, keepdims=True)
        dd_ref[...] += dD

    rev = lambda c: (nc - 1 - c, 0)
    vec = pl.BlockSpec((1, LANES), lambda c: (0, 0))
    return pl.pallas_call(
        body, name=name, grid=(nc,),
        in_specs=[pl.BlockSpec((L, xbc.shape[1]), rev), pl.BlockSpec((L, LANES), rev),
                  pl.BlockSpec((1, NS, SW), lambda c: (nc - 1 - c, 0, 0)), pl.BlockSpec((L, SW), rev), vec, vec, vec],
        out_specs=[pl.BlockSpec((L, xbc.shape[1]), rev), pl.BlockSpec((L, LANES), rev), vec, vec, vec],
        out_shape=[jax.ShapeDtypeStruct(xbc.shape, f32), jax.ShapeDtypeStruct((T, LANES), f32)]
        + [jax.ShapeDtypeStruct((1, LANES), f32)] * 3,
        scratch_shapes=[pltpu.VMEM((NS, SW), f32)], compiler_params=_cp("arbitrary"),
    )(xbc, dtraw, sprev, dy, dt_bias, a_log, d_skip)


def _gate_bwd(y, z, dyn, out_norm, name, tr=256):
    T, SW = y.shape
    tr = min(tr, T)
    gsz = SW // SSD_GROUPS

    def body(y_ref, z_ref, d_ref, on_ref, dy_ref, dz_ref, don_ref):
        @pl.when(pl.program_id(0) == 0)
        def _():
            don_ref[...] = jnp.zeros_like(don_ref)

        for g in range(SSD_GROUPS):
            gs = slice(g * gsz, (g + 1) * gsz)
            yv, zv, dv = y_ref[:, gs], z_ref[:, gs], d_ref[:, gs]
            sg = _sigmoid(zv)
            sl = zv * sg
            gated = yv * sl
            rr = lax.rsqrt(jnp.mean(gated * gated, axis=-1, keepdims=True) + EPS)
            gh = gated * rr
            dgn = dv * on_ref[:, gs]
            dgated = rr * (dgn - gh * jnp.mean(dgn * gh, axis=-1, keepdims=True))
            dy_ref[:, gs] = dgated * sl
            dz_ref[:, gs] = (dgated * yv * (sg * (1.0 + zv * (1.0 - sg)))).astype(dz_ref.dtype)
            don_ref[:, gs] += jnp.sum(dv * gh, axis=0, keepdims=True)

    row = pl.BlockSpec((tr, SW), lambda r: (r, 0))
    vec = pl.BlockSpec((1, SW), lambda r: (0, 0))
    return pl.pallas_call(
        body, name=name, grid=(T // tr,), in_specs=[row, row, row, vec], out_specs=[row, row, vec],
        out_shape=[jax.ShapeDtypeStruct((T, SW), f32), jax.ShapeDtypeStruct((T, SW), bf16),
                   jax.ShapeDtypeStruct((1, SW), f32)],
        compiler_params=_cp("arbitrary"),
    )(y, z, dyn, out_norm)


def _qk_norm_fwd(qkv, qn_w, kn_w, SBW, name, tr=256):
    T = qkv.shape[0]
    tr = min(tr, T)
    nh = SBW // SB_HEAD_DIM

    def body(q_ref, k_ref, v_ref, qw_ref, kw_ref, qo_ref, ko_ref, vo_ref):
        for src, w_ref, dst in ((q_ref, qw_ref, qo_ref), (k_ref, kw_ref, ko_ref)):
            for h in range(nh):
                hs = slice(h * SB_HEAD_DIM, (h + 1) * SB_HEAD_DIM)
                sv = src[:, hs]
                rr = lax.rsqrt(jnp.mean(sv * sv, axis=-1, keepdims=True) + EPS)
                dst[:, hs] = (sv * rr * w_ref[...]).astype(dst.dtype)
        vo_ref[...] = v_ref[...].astype(vo_ref.dtype)

    blk = lambda j: pl.BlockSpec((tr, SBW), lambda r: (r, j))
    vec = pl.BlockSpec((1, SB_HEAD_DIM), lambda r: (0, 0))
    out = pl.BlockSpec((tr, SBW), lambda r: (r, 0))
    return pl.pallas_call(
        body, name=name, grid=(T // tr,), in_specs=[blk(0), blk(1), blk(2), vec, vec], out_specs=[out, out, out],
        out_shape=[jax.ShapeDtypeStruct((T, SBW), bf16)] * 3, compiler_params=_cp("parallel"),
    )(qkv, qkv, qkv, qn_w, kn_w)


def _qk_norm_bwd(qkv, dqn, dkn, dv, qn_w, kn_w, SBW, name, tr=256):
    T = qkv.shape[0]
    tr = min(tr, T)
    nh = SBW // SB_HEAD_DIM

    def body(q_ref, k_ref, dq_ref, dk_ref, dv_ref, qw_ref, kw_ref, o_ref, dqw_ref, dkw_ref):
        @pl.when(pl.program_id(0) == 0)
        def _():
            dqw_ref[...] = jnp.zeros_like(dqw_ref)
            dkw_ref[...] = jnp.zeros_like(dkw_ref)

        for part, (src, d_ref, w_ref, dw_ref) in enumerate(((q_ref, dq_ref, qw_ref, dqw_ref), (k_ref, dk_ref, kw_ref, dkw_ref))):
            dw = jnp.zeros((1, SB_HEAD_DIM), f32)
            for h in range(nh):
                hs = slice(h * SB_HEAD_DIM, (h + 1) * SB_HEAD_DIM)
                os_ = slice(part * SBW + h * SB_HEAD_DIM, part * SBW + (h + 1) * SB_HEAD_DIM)
                sv, dn = src[:, hs], d_ref[:, hs]
                rr = lax.rsqrt(jnp.mean(sv * sv, axis=-1, keepdims=True) + EPS)
                xh = sv * rr
                dg = dn * w_ref[...]
                o_ref[:, os_] = (rr * (dg - xh * jnp.mean(dg * xh, axis=-1, keepdims=True))).astype(o_ref.dtype)
                dw = dw + jnp.sum(dn * xh, axis=0, keepdims=True)
            dw_ref[...] += dw
        o_ref[:, 2 * SBW:] = dv_ref[...].astype(o_ref.dtype)

    blk = lambda j: pl.BlockSpec((tr, SBW), lambda r: (r, j))
    vec = pl.BlockSpec((1, SB_HEAD_DIM), lambda r: (0, 0))
    row = pl.BlockSpec((tr, SBW), lambda r: (r, 0))
    return pl.pallas_call(
        body, name=name, grid=(T // tr,), in_specs=[blk(0), blk(1), row, row, row, vec, vec],
        out_specs=[pl.BlockSpec((tr, 3 * SBW), lambda r: (r, 0)), vec, vec],
        out_shape=[jax.ShapeDtypeStruct((T, 3 * SBW), bf16)] + [jax.ShapeDtypeStruct((1, SB_HEAD_DIM), f32)] * 2,
        compiler_params=_cp("arbitrary"),
    )(qkv, qkv, dqn, dkn, dv, qn_w, kn_w)


def _sb_logits(q, kb, scale):
    zl = _dot(q, kb, "nt") * scale
    lb = jnp.minimum(zl, 0.0) - jnp.log(1.0 + jnp.exp(-jnp.abs(zl)))
    return zl, lb, lb - zl


def _tail_update(old, r0, new_tail):
    return new_tail if r0 == 0 else jnp.concatenate([old[:r0], new_tail], axis=0)


def _hosted(comm, n_in, n_out, refs):
    nci, nco, ncs = (len(comm["ins"]), len(comm["outs"]), len(comm["sems"])) if comm is not None else (0, 0, 0)
    ins, outs = refs[:n_in], refs[n_in + nci:n_in + nci + n_out]
    scratch = refs[n_in + nci + n_out + nco:len(refs) - ncs]
    cargs = (refs[n_in:n_in + nci], refs[n_in + nci + n_out:n_in + nci + n_out + nco], refs[len(refs) - ncs:])
    return ins, outs, scratch, cargs


def _host_call(body, name, grid, in_specs, out_specs, out_shape, scratch, operands, comm):
    n_out = len(out_shape)
    in_specs, out_specs, out_shape, scratch, operands = list(in_specs), list(out_specs), list(out_shape), list(scratch), list(operands)
    io = {}
    if comm is not None:
        for src, dst in comm.get("aliases", {}).items():
            io[len(operands) + src] = n_out + dst
        in_specs += [ANY] * len(comm["ins"])
        operands += comm["ins"]
        out_specs += [ANY] * len(comm["outs"])
        out_shape += comm["outs"]
        scratch += comm["sems"]
    res = pl.pallas_call(body, name=name, grid=grid, in_specs=in_specs, out_specs=out_specs, out_shape=out_shape,
                         scratch_shapes=scratch, input_output_aliases=io,
                         compiler_params=_cp(*(("arbitrary",) * len(grid))))(*operands)
    return res[:n_out], res[n_out:]


def _first_last(grid):
    pid = [pl.program_id(d) for d in range(len(grid))]
    first, last = pid[0] == 0, pid[0] == grid[0] - 1
    for d in range(1, len(grid)):
        first, last = first & (pid[d] == 0), last & (pid[d] == grid[d] - 1)
    return first, last


def _sb_fwd(qn, kn, vb, name, comm=None, tq=1024, tk=256):
    T, W = qn.shape
    tq = min(tq, T)
    tk = min(tk, tq)
    nh, nq, dh, nd = W // SB_HEAD_DIM, T // tq, SB_HEAD_DIM, tq // tk
    scale = dh ** -0.5
    grid = (nh, nq)

    def body(*refs):
        (q_ref, k_ref, v_ref), (o_ref, c_ref), _, cargs = _hosted(comm, 3, 2, refs)
        first, last = _first_last(grid)
        if comm is not None:
            @pl.when(first)
            def _():
                comm["start"](*cargs)

        qi = pl.program_id(1)
        q = q_ref[...]
        later = (lax.broadcasted_iota(jnp.int32, (tk, tk), 0) > lax.broadcasted_iota(jnp.int32, (tk, tk), 1)).astype(bf16)

        def step(j, carry, d):
            acc, run = carry
            r0 = 0 if d is None else d * tk
            ks = pl.multiple_of(j * tk, tk)
            kb, vv = k_ref[pl.ds(ks, tk), :], v_ref[pl.ds(ks, tk), :]
            _, lb, lk = _sb_logits(q[r0:], kb, scale)
            if d is not None:
                mask = lax.broadcasted_iota(jnp.int32, lk.shape, 1) < lax.broadcasted_iota(jnp.int32, lk.shape, 0)
                lk = jnp.where(mask, lk, 0.0)
            between = _dot_mask(lk, later, 2)
            w = jnp.exp(lb + between + run[r0:])
            if d is not None:
                w = jnp.where(mask, w, 0.0)
            return (_tail_update(acc, r0, acc[r0:] + _dot(w.astype(bf16), vv)),
                    _tail_update(run, r0, run[r0:] + between[:, 0:1] + lk[:, 0:1]))

        carry = (jnp.zeros((tq, dh), f32), jnp.zeros((tq, 1), f32))
        for d in range(nd - 1, -1, -1):
            carry = step(qi * nd + d, carry, d)
        n_before = qi * nd
        acc, run = lax.fori_loop(0, n_before, lambda t, c: step(n_before - 1 - t, c, None), carry)
        o_ref[...] = acc.astype(o_ref.dtype)
        c_ref[...] = jnp.broadcast_to(run, (tq, dh))
        if comm is not None:
            @pl.when(last)
            def _():
                comm["finish"](*cargs)

    qblk = pl.BlockSpec((tq, dh), lambda h, i: (i, h))
    full = pl.BlockSpec((T, dh), lambda h, i: (0, h))
    (o, c), got = _host_call(body, name, grid, [qblk, full, full], [qblk, qblk],
                             [jax.ShapeDtypeStruct((T, W), bf16), jax.ShapeDtypeStruct((T, W), f32)], [], [qn, kn, vb], comm)
    return o, c, got


def _sb_bwd(qn, kn, vb, do, ctot, do_off, name, comm=None, tq=1024, tk=256):
    T, W = qn.shape
    tq = min(tq, T)
    tk = min(tk, tq)
    nh, nq, dh, nd = W // SB_HEAD_DIM, T // tq, SB_HEAD_DIM, tq // tk
    scale = dh ** -0.5
    ob = do_off // dh
    grid = (nh, nq)

    def body(*refs):
        (q_ref, k_ref, v_ref, do_ref, c_ref), (dq_ref, dk_ref, dv_ref), _, cargs = _hosted(comm, 5, 3, refs)
        first, last = _first_last(grid)
        if comm is not None:
            @pl.when(first)
            def _():
                comm["start"](*cargs)

        qi = pl.program_id(1)

        @pl.when(qi == 0)
        def _():
            dk_ref[...] = jnp.zeros_like(dk_ref)
            dv_ref[...] = jnp.zeros_like(dv_ref)

        q = q_ref[...]
        dob = do_ref[...].astype(bf16)
        total = c_ref[:, 0:1]
        r2 = lax.broadcasted_iota(jnp.int32, (tk, tk), 0)
        c2 = lax.broadcasted_iota(jnp.int32, (tk, tk), 1)
        upto = (r2 <= c2).astype(bf16)
        before = (r2 < c2).astype(bf16)

        def step(j, carry, d):
            dq, pre, gpre = carry
            r0 = 0 if d is None else d * tk
            ks = pl.multiple_of(j * tk, tk)
            kb, vv = k_ref[pl.ds(ks, tk), :], v_ref[pl.ds(ks, tk), :]
            qs, dos = q[r0:], dob[r0:]
            _, lb, lk = _sb_logits(qs, kb, scale)
            if d is not None:
                mask = lax.broadcasted_iota(jnp.int32, lk.shape, 1) < lax.broadcasted_iota(jnp.int32, lk.shape, 0)
                lk = jnp.where(mask, lk, 0.0)
            pin = _dot_mask(lk, upto, 2)
            w = jnp.exp(lb + (total[r0:] - pre[r0:] - pin))
            if d is not None:
                w = jnp.where(mask, w, 0.0)
            dw = _dot(dos, vv, "nt")
            dv_ref[pl.ds(ks, tk), :] += _dot(w.astype(bf16), dos, "tn")
            gg = dw * w
            gex = _dot(gg.astype(bf16), before)
            beta = jnp.exp(lb)
            dz = (gg * (1.0 - beta) - (gpre[r0:] + gex) * beta) * scale
            if d is not None:
                dz = jnp.where(mask, dz, 0.0)
            dzb = dz.astype(bf16)
            dk_ref[pl.ds(ks, tk), :] += _dot(dzb, qs, "tn")
            return (_tail_update(dq, r0, dq[r0:] + _dot(dzb, kb)),
                    _tail_update(pre, r0, pre[r0:] + pin[:, tk - 1:tk]),
                    _tail_update(gpre, r0, gpre[r0:] + gex[:, tk - 1:tk] + gg[:, tk - 1:tk]))

        init = (jnp.zeros((tq, dh), f32), jnp.zeros((tq, 1), f32), jnp.zeros((tq, 1), f32))
        carry = lax.fori_loop(0, qi * nd, lambda t, c: step(t, c, None), init)
        for d in range(nd):
            carry = step(qi * nd + d, carry, d)
        dq_ref[...] = carry[0]
        if comm is not None:
            @pl.when(last)
            def _():
                comm["finish"](*cargs)

    qblk = pl.BlockSpec((tq, dh), lambda h, i: (i, h))
    full = pl.BlockSpec((T, dh), lambda h, i: (0, h))
    (dq, dk, dv), got = _host_call(
        body, name, grid, [qblk, full, full, pl.BlockSpec((tq, dh), lambda h, i: (i, h + ob)), qblk], [qblk, full, full],
        [jax.ShapeDtypeStruct((T, W), f32)] * 3, [], [qn, kn, vb, do, ctot], comm)
    return dq, dk, dv, got


def _pool_select(sums, g):
    return jnp.where(g == 0, sums[0], jnp.where(g == 1, sums[1], jnp.where(g == 2, sums[2], sums[3])))


def _pool_count(g, r, rc, n, cols, off=0):
    t = (r * rc + off + lax.broadcasted_iota(jnp.int32, (n, cols), 0)).astype(f32)
    win = jnp.left_shift(2, g).astype(f32)
    return jnp.minimum(t + 1.0, win)


def _pool_fwd(hp, xres, w, b, scale, name, rc=512):
    T, D = hp.shape
    rc = min(rc, T)
    pg = D // len(POOL_WINDOWS)

    def body(h_ref, p_ref, x_ref, w_ref, b_ref, s_ref, o_ref, yp_ref, d_ref):
        g, r = pl.program_id(0), pl.program_id(1)
        cur = h_ref[...]
        halo = jnp.where(r > 0, p_ref[...], 0.0)
        ext = jnp.concatenate([halo, cur], axis=0)
        sums, s = [], ext
        for sh in (1, 2, 4, 8):
            s = s + pltpu.roll(s, sh, 0)
            sums.append(s)
        d = _pool_select(sums, g)[POOL_HALO:] / _pool_count(g, r, rc, rc, pg) - cur
        yp = _dot(d.astype(bf16), w_ref[0]) + b_ref[...]
        yp_ref[...] = yp
        d_ref[...] = d.astype(d_ref.dtype)
        o_ref[...] = x_ref[...] + yp * s_ref[...]

    cur = pl.BlockSpec((rc, pg), lambda g, r: (r, g))
    prev = pl.BlockSpec((POOL_HALO, pg), lambda g, r: (jnp.maximum(r * (rc // POOL_HALO) - 1, 0), g))
    vec = pl.BlockSpec((1, pg), lambda g, r: (0, g))
    return pl.pallas_call(
        body, name=name, grid=(len(POOL_WINDOWS), T // rc),
        in_specs=[cur, prev, cur, pl.BlockSpec((1, pg, pg), lambda g, r: (g, 0, 0)), vec, vec],
        out_specs=[cur, cur, cur],
        out_shape=[jax.ShapeDtypeStruct((T, D), f32), jax.ShapeDtypeStruct((T, D), f32), jax.ShapeDtypeStruct((T, D), bf16)],
        compiler_params=_cp("parallel", "parallel"),
    )(hp, hp, xres, w, b, scale)


def _pool_bwd(dx, yp, d, w, scale, name, rc=512):
    T, D = dx.shape
    rc = min(rc, T)
    pg = D // len(POOL_WINDOWS)
    nr = T // rc

    def body(dx_ref, dn_ref, yp_ref, d_ref, w_ref, s_ref, dh_ref, dw_ref, db_ref, dsc_ref):
        g, r = pl.program_id(0), pl.program_id(1)

        @pl.when(r == 0)
        def _():
            dw_ref[...] = jnp.zeros_like(dw_ref)
            db_ref[...] = jnp.zeros_like(db_ref)
            dsc_ref[...] = jnp.zeros_like(dsc_ref)

        dxv = dx_ref[...]
        dyp = dxv * s_ref[...]
        dsc_ref[...] += jnp.sum(dxv * yp_ref[...], axis=0, keepdims=True)
        db_ref[...] += jnp.sum(dyp, axis=0, keepdims=True)
        dypb = dyp.astype(bf16)
        dw_ref[0] += _dot(d_ref[...], dypb, "tn")
        dd = _dot(dypb, w_ref[0], "nt")
        ddn = _dot((dn_ref[...] * s_ref[...]).astype(bf16), w_ref[0], "nt")
        e = dd / _pool_count(g, r, rc, rc, pg)
        en = jnp.where(r < nr - 1, ddn / _pool_count(g, r, rc, POOL_HALO, pg, off=rc), 0.0)
        ext = jnp.concatenate([e, en], axis=0)
        sums, s = [], ext
        for sh in (1, 2, 4, 8):
            s = s + pltpu.roll(s, rc + POOL_HALO - sh, 0)
            sums.append(s)
        dh_ref[...] = _pool_select(sums, g)[:rc] - dd

    cur = pl.BlockSpec((rc, pg), lambda g, r: (r, g))
    nxt = pl.BlockSpec((POOL_HALO, pg), lambda g, r: (jnp.minimum((r + 1) * (rc // POOL_HALO), T // POOL_HALO - 1), g))
    vec = pl.BlockSpec((1, pg), lambda g, r: (0, g))
    wsp = pl.BlockSpec((1, pg, pg), lambda g, r: (g, 0, 0))
    return pl.pallas_call(
        body, name=name, grid=(len(POOL_WINDOWS), nr), in_specs=[cur, nxt, cur, cur, wsp, vec],
        out_specs=[cur, wsp, vec, vec],
        out_shape=[jax.ShapeDtypeStruct((T, D), f32), jax.ShapeDtypeStruct(w.shape, f32),
                   jax.ShapeDtypeStruct((1, D), f32), jax.ShapeDtypeStruct((1, D), f32)],
        compiler_params=_cp("parallel", "arbitrary"),
    )(dx, dx, yp, d, w, scale)


def _adamw(w, g, m, v, name, tr=256):
    R, C = w.shape
    tr = min(tr, R)
    assert R % tr == 0

    def body(w_ref, g_ref, m_ref, v_ref, d_ref, mo_ref, vo_ref):
        gv = g_ref[...]
        mn = ADAM_B1 * m_ref[...] + (1.0 - ADAM_B1) * gv
        vn = ADAM_B2 * v_ref[...] + (1.0 - ADAM_B2) * (gv * gv)
        m_hat = mn / (1.0 - ADAM_B1 ** ADAM_STEP)
        v_hat = vn / (1.0 - ADAM_B2 ** ADAM_STEP)
        d_ref[...] = -ADAM_LR * (m_hat / (jnp.sqrt(v_hat) + ADAM_EPS) + ADAM_WD * w_ref[...])
        mo_ref[...] = mn
        vo_ref[...] = vn

    blk = pl.BlockSpec((tr, C), lambda r: (r, 0))
    return pl.pallas_call(
        body, name=name, grid=(R // tr,), in_specs=[blk] * 4, out_specs=[blk] * 3,
        out_shape=[jax.ShapeDtypeStruct((R, C), f32)] * 3, compiler_params=_cp("parallel"),
    )(w, g, m, v)


def _pair_sum(g4, recv, name, br=256):
    _, R, C = g4.shape
    hr = R // 2
    br = min(br, hr)
    nb = hr // br

    def body(a_ref, b_ref, o_ref, ob_ref):
        s = a_ref[...] + b_ref[...]
        o_ref[...] = s
        ob_ref[...] = s.astype(ob_ref.dtype)

    out = pl.BlockSpec((1, br, C), lambda s, i: (s, i, 0))
    return pl.pallas_call(
        body, name=name, grid=(N_CHIPS, nb),
        in_specs=[pl.BlockSpec((1, br, C), lambda s, i: (s, lax.axis_index("c") * nb + i, 0)), out],
        out_specs=[out, out],
        out_shape=[jax.ShapeDtypeStruct((N_CHIPS, hr, C), f32), jax.ShapeDtypeStruct((N_CHIPS, hr, C), bf16)],
        compiler_params=_cp("parallel", "parallel"),
    )(g4, recv)


def _chip_sum(own4, recv4, name, br=256):
    _, hr, C = own4.shape
    br = min(br, hr)
    nb = hr // br
    chip = lambda: 2 * lax.axis_index("x") + lax.axis_index("y")

    def body(a_ref, b1_ref, b2_ref, b3_ref, o_ref):
        o_ref[...] = ((a_ref[0] + b1_ref[0].astype(f32)) + b2_ref[0].astype(f32)) + b3_ref[0].astype(f32)

    other = lambda k: pl.BlockSpec((1, br, C), lambda i: ((chip() + k) % N_CHIPS, i, 0))
    return pl.pallas_call(
        body, name=name, grid=(nb,),
        in_specs=[pl.BlockSpec((1, br, C), lambda i: (chip(), i, 0)), other(1), other(2), other(3)],
        out_specs=pl.BlockSpec((br, C), lambda i: (lax.axis_index("c") * nb + i, 0)),
        out_shape=jax.ShapeDtypeStruct((2 * hr, C), f32), compiler_params=_cp("parallel"),
    )(own4, recv4, recv4, recv4)


ANY = pl.BlockSpec(memory_space=pl.ANY)


def _mesh_pos():
    x, y, c = lax.axis_index("x"), lax.axis_index("y"), lax.axis_index("c")
    others = [(1 - x, y), (x, 1 - y), (1 - x, 1 - y)]
    return x, y, c, 2 * x + y, others


def _gather_small(blk, name):
    m, n = blk.shape

    def body(x_ref, out_ref, sum_ref, send_sems, recv_sems, local_sem):
        x, y, c, _, others = _mesh_pos()
        me, sibling = (x, y, c), (x, y, 1 - c)

        def rows(px, py, pc):
            return out_ref.at[pl.ds((4 * px + 2 * py + pc) * m, m), :]

        def copy(k, block, to, src=None):
            return pltpu.make_async_remote_copy(
                src_ref=rows(*block) if src is None else src, dst_ref=rows(*block),
                send_sem=send_sems.at[k], recv_sem=recv_sems.at[k], device_id=to, device_id_type=MESH)

        mine = pltpu.make_async_copy(x_ref, rows(*me), local_sem)
        mine.start()
        first = [copy(0, me, sibling, src=x_ref)]
        first += [copy(1 + j, me, (*chip, c), src=x_ref) for j, chip in enumerate(others)]
        for cp in first:
            cp.start()
        passed = [copy(4 + j, (*chip, c), sibling) for j, chip in enumerate(others)]
        for j, chip in enumerate(others):
            copy(1 + j, (*chip, c), me).wait_recv()
            passed[j].start()
        copy(0, sibling, me).wait_recv()
        for j, chip in enumerate(others):
            copy(4 + j, (*chip, 1 - c), me).wait_recv()
        for cp in first + passed:
            cp.wait_send()
        mine.wait()
        acc = out_ref[0:m, :]
        for d in range(1, 8):
            acc = acc + out_ref[d * m:(d + 1) * m, :]
        sum_ref[...] = acc

    vm = pl.BlockSpec(memory_space=pltpu.VMEM)
    return pl.pallas_call(
        body, name=name, in_specs=[vm], out_specs=[vm, vm],
        out_shape=[jax.ShapeDtypeStruct((8 * m, n), f32), jax.ShapeDtypeStruct((m, n), f32)],
        scratch_shapes=[pltpu.SemaphoreType.DMA((7,)), pltpu.SemaphoreType.DMA((7,)), pltpu.SemaphoreType.DMA],
    )(blk)


def _copy(src, dst, sems, idx, to):
    return pltpu.make_async_remote_copy(src_ref=src, dst_ref=dst, send_sem=sems[0].at[idx], recv_sem=sems[1].at[idx],
                                        device_id=to, device_id_type=MESH)


def _gather_ici(shards):
    nt = len(shards)

    def copies(ins, outs, sems):
        x, y, c, chip, others = _mesh_pos()
        send, land = [], []
        for t in range(nt):
            hr = ins[t].shape[0] // 2
            for j, (px, py) in enumerate(others):
                send.append((ins[t].at[pl.ds(c * hr, hr)], outs[t].at[chip, pl.ds(c * hr, hr)], sems, (t, j), (px, py, c)))
                piece = outs[t].at[2 * px + py, pl.ds(c * hr, hr)]
                land.append((piece, piece, sems, (t, j), (px, py, c)))
        return send, land

    def start(ins, outs, sems):
        for args in copies(ins, outs, sems)[0]:
            _copy(*args).start()

    def finish(ins, outs, sems):
        send, land = copies(ins, outs, sems)
        for args in land:
            _copy(*args).wait_recv()
        for args in send:
            _copy(*args).wait_send()

    return dict(ins=list(shards), outs=[jax.ShapeDtypeStruct((N_CHIPS,) + s.shape, s.dtype) for s in shards],
                sems=[pltpu.SemaphoreType.DMA((nt, 3)), pltpu.SemaphoreType.DMA((nt, 3))], start=start, finish=finish)


def _gather_d2d(stacks):
    nt = len(stacks)

    def copies(ins, outs, sems):
        x, y, c, _, others = _mesh_pos()
        send, land = [], []
        for t in range(nt):
            hr = outs[t].shape[1] // 2
            for j, (px, py) in enumerate(others):
                mine = outs[t].at[2 * px + py, pl.ds(c * hr, hr)]
                theirs = outs[t].at[2 * px + py, pl.ds((1 - c) * hr, hr)]
                send.append((mine, mine, sems, (t, j), (x, y, 1 - c)))
                land.append((theirs, theirs, sems, (t, j), (x, y, 1 - c)))
        return send, land

    def start(ins, outs, sems):
        for args in copies(ins, outs, sems)[0]:
            _copy(*args).start()

    def finish(ins, outs, sems):
        send, land = copies(ins, outs, sems)
        for args in land:
            _copy(*args).wait_recv()
        for args in send:
            _copy(*args).wait_send()

    return dict(ins=list(stacks), outs=[jax.ShapeDtypeStruct(s.shape, s.dtype) for s in stacks],
                sems=[pltpu.SemaphoreType.DMA((nt, 3)), pltpu.SemaphoreType.DMA((nt, 3))], start=start, finish=finish,
                aliases={t: t for t in range(nt)})


def _run_exchange(comm, name):
    ni, no = len(comm["ins"]), len(comm["outs"])

    def body(*refs):
        args = (refs[:ni], refs[ni:ni + no], refs[ni + no:])
        comm["start"](*args)
        comm["finish"](*args)

    return pl.pallas_call(
        body, name=name, in_specs=[ANY] * ni, out_specs=[ANY] * no, out_shape=comm["outs"], scratch_shapes=comm["sems"],
        input_output_aliases=dict(comm.get("aliases", {})))(*comm["ins"])


def _swap_halves(g4s, name):
    nt = len(g4s)

    def body(*refs):
        ins, outs = refs[:nt], refs[nt:2 * nt]
        send_sems, recv_sems = refs[2 * nt:]
        x, y, c, _, _ = _mesh_pos()
        cps = []
        for t in range(nt):
            hr = ins[t].shape[1] // 2
            cp = pltpu.make_async_remote_copy(
                src_ref=ins[t].at[:, pl.ds((1 - c) * hr, hr)], dst_ref=outs[t], send_sem=send_sems.at[t],
                recv_sem=recv_sems.at[t], device_id=(x, y, 1 - c), device_id_type=MESH)
            cp.start()
            cps.append(cp)
        for cp in cps:
            cp.wait()

    return pl.pallas_call(
        body, name=name, in_specs=[ANY] * nt, out_specs=[ANY] * nt,
        out_shape=[jax.ShapeDtypeStruct((N_CHIPS, g.shape[1] // 2, g.shape[2]), g.dtype) for g in g4s],
        scratch_shapes=[pltpu.SemaphoreType.DMA((nt,)), pltpu.SemaphoreType.DMA((nt,))],
    )(*g4s)


def _exchange_chips(h4s):
    nt = len(h4s)

    def copies(ins, outs, sems):
        x, y, c, chip, others = _mesh_pos()
        send, land = [], []
        for t in range(nt):
            for j, (px, py) in enumerate(others):
                send.append((ins[t].at[2 * px + py], outs[t].at[chip], sems, (t, j), (px, py, c)))
                landed = outs[t].at[2 * px + py]
                land.append((landed, landed, sems, (t, j), (px, py, c)))
        return send, land

    def start(ins, outs, sems):
        for args in copies(ins, outs, sems)[0]:
            _copy(*args).start()

    def finish(ins, outs, sems):
        send, land = copies(ins, outs, sems)
        for args in land:
            _copy(*args).wait_recv()
        for args in send:
            _copy(*args).wait_send()

    return dict(ins=list(h4s), outs=[jax.ShapeDtypeStruct(h.shape, h.dtype) for h in h4s],
                sems=[pltpu.SemaphoreType.DMA((nt, 3)), pltpu.SemaphoreType.DMA((nt, 3))], start=start, finish=finish)


def _join_halves(fs):
    nt = len(fs)

    def copies(ins, outs, sems):
        x, y, c, _, _ = _mesh_pos()
        send, land = [], []
        for t in range(nt):
            hr = outs[t].shape[0] // 2
            mine, theirs = outs[t].at[pl.ds(c * hr, hr)], outs[t].at[pl.ds((1 - c) * hr, hr)]
            send.append((mine, mine, sems, t, (x, y, 1 - c)))
            land.append((theirs, theirs, sems, t, (x, y, 1 - c)))
        return send, land

    def start(ins, outs, sems):
        for args in copies(ins, outs, sems)[0]:
            _copy(*args).start()

    def finish(ins, outs, sems):
        send, land = copies(ins, outs, sems)
        for args in land:
            _copy(*args).wait_recv()
        for args in send:
            _copy(*args).wait_send()

    return dict(ins=list(fs), outs=[jax.ShapeDtypeStruct(f.shape, f.dtype) for f in fs],
                sems=[pltpu.SemaphoreType.DMA((nt,)), pltpu.SemaphoreType.DMA((nt,))], start=start, finish=finish,
                aliases={t: t for t in range(nt)})


def _pad_lanes(v, n=LANES):
    return jnp.pad(v, ((0, 0), (0, n - v.shape[-1])))


def _mlp_fwd(xin, norm_g, w_up, w_down, F, tag):
    T, D = xin.shape
    h = _rms_fwd(xin, norm_g, bf16, f"{tag}_norm")

    def relu_sq(acc):
        r = jnp.maximum(acc, 0.0)
        return r, r * r

    u, a = _mm(h, w_up[0], "nn", T, F, D, (bf16, bf16), f"{tag}_up", epilogue=relu_sq, b_view=w_up[1])
    out = _mm(a, w_down[0], "nn", T, D, F, (f32,), f"{tag}_down", epilogue=lambda acc, res: (res + acc,),
              extras=((xin, "tile"),), b_view=w_down[1])
    return out, (xin, h, u, a)


def _mlp_bwd(dy, saved, norm_g, w_up, w_down, F, tag, up_to=None, down_to=None):
    xin, h, u, a = saved
    T, D = xin.shape
    to = lambda t: {} if t is None else dict(out_view=t[0], out_stack=t[1], alias=t[2])
    du = _mm(dy, w_down[0], "nt", T, F, D, (bf16,), f"{tag}_dact", epilogue=lambda acc, uu: (acc * (2.0 * uu.astype(f32)),),
             extras=((u, "tile"),), b_view=w_down[1])
    dw_down = _mm(a, dy, "tn", F, D, T, (f32,), f"{tag}_dwdown", **to(down_to))
    dh = _mm(du, w_up[0], "nt", T, D, F, (f32,), f"{tag}_dh", b_view=w_up[1])
    dw_up = _mm(h, du, "tn", D, F, T, (f32,), f"{tag}_dwup", **to(up_to))
    dx, dg = _rms_bwd(xin, norm_g, dh, dy, f"{tag}_dnorm")
    return dx, dg, dw_up, dw_down


def _local_step(xc, tgt, W, HS, SBW, net=None):
    T, D = xc.shape
    SW = HS * SSD_HEAD_DIM
    CD = W["conv_b"].shape[-1]
    mlp_norm = W["mlp_norm"]
    add = lambda acc, prev: (prev + acc,)

    h0 = _rms_fwd(xc, W["hyb_norm"], bf16, "hyb_norm")
    z = _mm(h0, W["w_z"], "nn", T, SW, D, (f32,), "proj_z")
    xraw = _mm(h0, W["w_xbc"], "nn", T, CD, D, (f32,), "proj_xbc")
    dtraw = _mm(h0, W["w_dt"], "nn", T, LANES, D, (f32,), "proj_dt")
    qkv = _mm(h0, W["w_qkv"], "nn", T, 3 * SBW, D, (f32,), "proj_qkv")
    qn, kn, vb = _qk_norm_fwd(qkv, W["q_norm"], W["k_norm"], SBW, "qk_norm")
    y_sb, ctot, got = _sb_fwd(qn, kn, vb, "sb_attn", comm=net.rest_ici() if net else None)
    xbc = _conv_fwd(xraw, W["conv_w"], W["conv_b"], "conv")
    y_ssd, yn_ssd, sprev, got = _ssd_fwd(xbc, dtraw, z, W["dt_bias"], W["a_log"], W["d_skip"], W["out_norm"], HS, "ssd",
                                         comm=net.rest_d2d(got) if net else None)
    if net:
        W = {**W, **net.rest_weights(got)}
    w_up, w_down, F = W["w_up"], W["w_down"], W["F"]
    mix = _mm(yn_ssd, W["w_out"], "nn", T, D, SW, (f32,), "out_ssd", epilogue=add, extras=((xc, "tile"),))
    x1 = _mm(y_sb, W["w_out"], "nn", T, D, SBW, (f32,), "out_sb", epilogue=add, extras=((mix, "tile"),), b_off=(SW, 0))
    x2, mlp0 = _mlp_fwd(x1, mlp_norm[0:1], w_up[0], w_down[0], F, "mlp0")

    hp = _rms_fwd(x2, W["pool_norm"], f32, "pool_norm")
    x3, yp, dpool = _pool_fwd(hp, x2, W["w_pool"], W["pool_b"], W["pool_scale"], "pool")
    x4, mlp1 = _mlp_fwd(x3, mlp_norm[1:2], w_up[1], w_down[1], F, "mlp1")

    dy, sq = _loss_grad(x4, tgt, "loss")

    up_to, down_to = (net.mlp_to("up", 1, None), net.mlp_to("down", 1, None)) if net else (None, None)
    dx3, dg_mlp1, dw_up1, dw_down1 = _mlp_bwd(dy, mlp1, mlp_norm[1:2], w_up[1], w_down[1], F, "mlp1", up_to, down_to)
    dhp, dw_pool, db_pool, dsc_pool = _pool_bwd(dx3, yp, dpool, W["w_pool"], W["pool_scale"], "pool_bwd")
    dx2, dg_pool = _rms_bwd(x2, W["pool_norm"], dhp, dx3, "pool_dnorm")
    up_to, down_to = (net.mlp_to("up", 0, dw_up1), net.mlp_to("down", 0, dw_down1)) if net else (None, None)
    dx1, dg_mlp0, dw_up0, dw_down0 = _mlp_bwd(dx2, mlp0, mlp_norm[0:1], w_up[0], w_down[0], F, "mlp0", up_to, down_to)

    dmerged = _mm(dx1, W["w_out"], "nt", T, SW + SBW, D, (f32,), "dmerged")
    dw_out = jnp.concatenate([_mm(yn_ssd, dx1, "tn", SW, D, T, (f32,), "dwout_ssd"),
                              _mm(y_sb, dx1, "tn", SBW, D, T, (f32,), "dwout_sb")], axis=0)
    dqn, dkn, dvv, got = _sb_bwd(qn, kn, vb, dmerged, ctot, SW, "sb_attn_bwd",
                                 comm=net.reduce_early(dw_out, dw_pool, dw_up0, dw_down0) if net else None)
    if net:
        net.reduce_early_done(got)
    dqkv, dg_q, dg_k = _qk_norm_bwd(qkv, dqn, dkn, dvv, W["q_norm"], W["k_norm"], SBW, "qk_norm_bwd")
    dy_ssd, dz, dg_on = _gate_bwd(y_ssd, z, dmerged, W["out_norm"], "gate_bwd")
    dxbc, ddtraw, dalog, dbias, ddskip = _ssd_bwd(xbc, dtraw, sprev, dy_ssd, W["dt_bias"], W["a_log"], W["d_skip"], HS, "ssd_bwd")
    dpre, dconv_w, dconv_b = _conv_bwd_pre(xraw, dxbc, W["conv_w"], W["conv_b"], "conv_bwd_pre")
    dxraw = _conv_bwd_in(dpre, W["conv_w"], "conv_bwd_in")
    dw_in = jnp.concatenate([
        _mm(h0, dz, "tn", D, SW, T, (f32,), "dwin_z"), _mm(h0, dxraw, "tn", D, CD, T, (f32,), "dwin_xbc"),
        _mm(h0, ddtraw, "tn", D, LANES, T, (f32,), "dwin_dt")[:, :HS], _mm(h0, dqkv, "tn", D, 3 * SBW, T, (f32,), "dwin_qkv")], axis=1)
    dh0 = _mm(dz, W["w_z"], "nt", T, D, SW, (f32,), "dh0_z")
    dh0 = _mm(dxraw, W["w_xbc"], "nt", T, D, CD, (f32,), "dh0_xbc", epilogue=add, extras=((dh0, "tile"),))
    dh0 = _mm(ddtraw, W["w_dt"], "nt", T, D, LANES, (f32,), "dh0_dt", epilogue=add, extras=((dh0, "tile"),))
    if net:
        (dh0,), got = _mm(dqkv, W["w_qkv"], "nt", T, D, 3 * SBW, (f32,), "dh0_qkv", epilogue=add, extras=((dh0, "tile"),),
                          comm=net.reduce_late(dw_in))
        net.reduce_late_done(got)
    else:
        dh0 = _mm(dqkv, W["w_qkv"], "nt", T, D, 3 * SBW, (f32,), "dh0_qkv", epilogue=add, extras=((dh0, "tile"),))
    grad_x, dg_hyb = _rms_bwd(xc, W["hyb_norm"], dh0, dx1, "hyb_dnorm")
    grads = dict(w_in=dw_in, w_out=dw_out, w_pool=dw_pool, w_up=(dw_up0, dw_up1), w_down=(dw_down0, dw_down1),
                 hyb_norm=dg_hyb, conv_w=dconv_w, conv_b=dconv_b, dt_bias=dbias, a_log=dalog, d_skip=ddskip, out_norm=dg_on,
                 q_norm=dg_q, k_norm=dg_k, mlp_norm=(dg_mlp0, dg_mlp1), pool_norm=dg_pool, pool_b=db_pool, pool_scale=dsc_pool)
    return sq, grad_x, grads


class _Net:
    def __init__(self, own_rest, chip, dims):
        self.own, self.chip, self.dims = own_rest, chip, dims

    def rest_ici(self):
        return _gather_ici(self.own)

    def rest_d2d(self, got):
        return _gather_d2d(list(got))

    def rest_weights(self, got):
        d, nw = self.dims, len(POOL_WINDOWS)
        D, F, NL, PG = d["D"], d["F"], d["NL"], d["PG"]
        fs = F // N_CHIPS
        g_out, g_pool, g_up, g_down = (lax.dynamic_update_index_in_dim(g, o, self.chip, 0) for g, o in zip(got, self.own))
        w_pool = g_pool.reshape(N_CHIPS, nw, PG // N_CHIPS, PG).transpose(1, 0, 2, 3).reshape(nw, PG, PG)
        return dict(w_out=g_out.reshape(d["MIX"], D), w_pool=w_pool, F=F,
                    w_up=[(g_up, ("cols", fs, l, D)) for l in range(NL)],
                    w_down=[(g_down, ("rows", fs, l, None)) for l in range(NL)])

    def mlp_to(self, which, layer, earlier):
        d = self.dims
        fs = d["F"] // N_CHIPS
        if which == "up":
            return ("cols", fs, layer, d["D"]), (N_CHIPS, d["NL"] * d["D"], fs), earlier
        return ("rows", fs, layer, None), (N_CHIPS, d["NL"] * fs, d["D"]), earlier

    def _pairs(self, g4s, tag):
        recv = _swap_halves(g4s, f"{tag}_pair_swap")
        sums = [_pair_sum(g, r, f"{tag}_pair_sum{i}") for i, (g, r) in enumerate(zip(g4s, recv))]
        return [s[0] for s in sums], [s[1] for s in sums]

    def reduce_early(self, dw_out, dw_pool, g_up, g_down):
        d, nw = self.dims, len(POOL_WINDOWS)
        PG = d["PG"]
        g4 = [dw_out.reshape(N_CHIPS, d["MIX"] // N_CHIPS, d["D"]),
              dw_pool.reshape(nw, N_CHIPS, PG // N_CHIPS, PG).transpose(1, 0, 2, 3).reshape(N_CHIPS, PG, PG), g_up, g_down]
        self.early_own, sent = self._pairs(g4, "grads_early")
        return _exchange_chips(sent)

    def reduce_early_done(self, got):
        self.early_got = list(got)

    def reduce_late(self, dw_in):
        d = self.dims
        g4 = [dw_in.reshape(d["D"], N_CHIPS, d["IN"] // N_CHIPS).transpose(1, 0, 2)]
        self.late_own, sent = self._pairs(g4, "grads_late")
        return _exchange_chips(sent)

    def reduce_late_done(self, got):
        self.late_got = list(got)

    def reduced(self):
        halves = [_chip_sum(h, r, f"grads_chip_sum{i}")
                  for i, (h, r) in enumerate(zip(self.late_own + self.early_own, self.late_got + self.early_got))]
        return _run_exchange(_join_halves(halves), "grads_join")


def kernel(x, hyb_norm, hyb_w_in, ssd_conv_w, ssd_conv_b, ssd_dt_bias, ssd_a_log, ssd_d, ssd_out_norm, sb_q_norm, sb_k_norm, hyb_w_out, pool_norm, pool_w, pool_b, pool_scale, mlp_norm, mlp_w_up, mlp_w_down, loss_target, m_hyb_norm, m_hyb_w_in, m_ssd_conv_w, m_ssd_conv_b, m_ssd_dt_bias, m_ssd_a_log, m_ssd_d, m_ssd_out_norm, m_sb_q_norm, m_sb_k_norm, m_hyb_w_out, m_pool_norm, m_pool_w, m_pool_b, m_pool_scale, m_mlp_norm, m_mlp_w_up, m_mlp_w_down, v_hyb_norm, v_hyb_w_in, v_ssd_conv_w, v_ssd_conv_b, v_ssd_dt_bias, v_ssd_a_log, v_ssd_d, v_ssd_out_norm, v_sb_q_norm, v_sb_k_norm, v_hyb_w_out, v_pool_norm, v_pool_w, v_pool_b, v_pool_scale, v_mlp_norm, v_mlp_w_up, v_mlp_w_down):
    T, D = x.shape[1], x.shape[2]
    HS = ssd_dt_bias.shape[-1]
    SW = HS * SSD_HEAD_DIM
    CD = ssd_conv_b.shape[-1]
    IN = N_CHIPS * hyb_w_in.shape[-1]
    SBW = (IN - SW - CD - HS) // 3
    F = N_CHIPS * mlp_w_up.shape[-1]
    NL = mlp_norm.shape[0]
    PG = D // len(POOL_WINDOWS)
    xc, tgt = x[0], loss_target[0]
    ix, iy, ic = lax.axis_index("x"), lax.axis_index("y"), lax.axis_index("c")
    chip = (2 * ix + iy).astype(jnp.int32)

    small = jnp.concatenate([ssd_conv_w.reshape(-1), pool_norm.reshape(-1), pool_b.reshape(-1), pool_scale.reshape(-1)])
    ns = small.shape[0]
    ns8 = -(-ns // (8 * LANES)) * LANES
    gathered, _ = _gather_small(jnp.pad(small, (0, 8 * ns8 - ns)).reshape(8, ns8), "gather_small")
    per_chip = gathered.reshape(N_CHIPS, 2, 8 * ns8)[:, 0, :ns]
    cw = CD // N_CHIPS
    conv_w = per_chip[:, :4 * cw].reshape(N_CHIPS, 4, cw).transpose(1, 0, 2).reshape(4, CD)
    pvec = per_chip[:, 4 * cw:].reshape(N_CHIPS, 3, PG)
    pool_norm_f, pool_b_f, pool_scale_f = (pvec[:, i].reshape(1, D) for i in range(3))

    fs = F // N_CHIPS
    own_in = hyb_w_in[0].astype(bf16)
    g_in = _run_exchange(_gather_d2d(_run_exchange(_gather_ici([own_in]), "gather_in_ici")), "gather_in_d2d")[0]
    w_in = lax.dynamic_update_index_in_dim(g_in, own_in, chip, 0).transpose(1, 0, 2).reshape(D, IN)
    c1, c2, c3 = SW, SW + CD, SW + CD + HS
    w_z, w_xbc, w_dt, w_qkv = w_in[:, :c1], w_in[:, c1:c2], _pad_lanes(w_in[:, c2:c3]), w_in[:, c3:]
    dt_bias_p, a_log_p, d_skip_p = _pad_lanes(ssd_dt_bias), _pad_lanes(ssd_a_log), _pad_lanes(ssd_d)

    own_rest = [hyb_w_out[0].astype(bf16), pool_w[0].reshape(-1, PG).astype(bf16),
                mlp_w_up.reshape(-1, fs).astype(bf16), mlp_w_down.reshape(-1, D).astype(bf16)]
    net = _Net(own_rest, chip, dict(D=D, F=F, NL=NL, PG=PG, IN=IN, MIX=SW + SBW))
    first = dict(hyb_norm=hyb_norm, w_z=w_z, w_xbc=w_xbc, w_dt=w_dt, w_qkv=w_qkv, conv_w=conv_w, conv_b=ssd_conv_b,
                 dt_bias=dt_bias_p, a_log=a_log_p, d_skip=d_skip_p, out_norm=ssd_out_norm, q_norm=sb_q_norm, k_norm=sb_k_norm,
                 pool_norm=pool_norm_f, pool_b=pool_b_f, pool_scale=pool_scale_f, mlp_norm=mlp_norm)
    sq, grad_x, gr = _local_step(xc, tgt, first, HS, SBW, net)
    loss = lax.psum(sq[0, 0] * (0.5 / D), ("x", "y", "c"))
    dg_hyb, dconv_b, dbias, dalog, ddskip, dg_on, dg_q, dg_k = (gr[k] for k in (
        "hyb_norm", "conv_b", "dt_bias", "a_log", "d_skip", "out_norm", "q_norm", "k_norm"))
    (dg_mlp0, dg_mlp1), dconv_w, dg_pool, db_pool, dsc_pool = gr["mlp_norm"], gr["conv_w"], gr["pool_norm"], gr["pool_b"], gr["pool_scale"]
    gb_in, gb_out, gb_pool, gb_up, gb_down = net.reduced()

    full_small = [dg_hyb, dconv_b, dbias[:, :HS], dalog[:, :HS], ddskip[:, :HS], dg_on, dg_q, dg_k,
                  jnp.concatenate([dg_mlp0, dg_mlp1], axis=0).reshape(1, -1),
                  dconv_w.reshape(1, -1), dg_pool, db_pool, dsc_pool]
    sizes = [v.shape[-1] for v in full_small]
    packed = jnp.concatenate([v.reshape(-1) for v in full_small])
    npk = packed.shape[0]
    npk8 = -(-npk // (8 * LANES)) * LANES
    _, summed = _gather_small(jnp.pad(packed, (0, 8 * npk8 - npk)).reshape(8, npk8), "grads_small")
    summed = summed.reshape(-1)[:npk]
    offs = [0]
    for s in sizes:
        offs.append(offs[-1] + s)
    (g_hyb_norm, g_conv_b, g_dt_bias, g_a_log, g_d, g_out_norm, g_q_norm, g_k_norm, g_mlp_norm, g_conv_w_full,
     g_pool_norm_full, g_pool_b_full, g_pool_scale_full) = (summed[offs[i]:offs[i + 1]] for i in range(len(sizes)))
    take = lambda full, n: lax.dynamic_slice_in_dim(full.reshape(-1, N_CHIPS, n), chip, 1, axis=1)
    small_grads = {
        "hyb_norm": g_hyb_norm.reshape(hyb_norm.shape), "ssd_conv_w": take(g_conv_w_full, cw).reshape(ssd_conv_w.shape),
        "ssd_conv_b": g_conv_b.reshape(ssd_conv_b.shape), "ssd_dt_bias": g_dt_bias.reshape(ssd_dt_bias.shape),
        "ssd_a_log": g_a_log.reshape(ssd_a_log.shape), "ssd_d": g_d.reshape(ssd_d.shape),
        "ssd_out_norm": g_out_norm.reshape(ssd_out_norm.shape), "sb_q_norm": g_q_norm.reshape(sb_q_norm.shape),
        "sb_k_norm": g_k_norm.reshape(sb_k_norm.shape), "pool_norm": take(g_pool_norm_full, PG).reshape(pool_norm.shape),
        "pool_b": take(g_pool_b_full, PG).reshape(pool_b.shape), "pool_scale": take(g_pool_scale_full, PG).reshape(pool_scale.shape),
        "mlp_norm": g_mlp_norm.reshape(mlp_norm.shape),
    }

    weights = dict(hyb_norm=hyb_norm, hyb_w_in=hyb_w_in, ssd_conv_w=ssd_conv_w, ssd_conv_b=ssd_conv_b, ssd_dt_bias=ssd_dt_bias,
                   ssd_a_log=ssd_a_log, ssd_d=ssd_d, ssd_out_norm=ssd_out_norm, sb_q_norm=sb_q_norm, sb_k_norm=sb_k_norm,
                   hyb_w_out=hyb_w_out, pool_norm=pool_norm, pool_w=pool_w, pool_b=pool_b, pool_scale=pool_scale,
                   mlp_norm=mlp_norm, mlp_w_up=mlp_w_up, mlp_w_down=mlp_w_down)
    moms = dict(hyb_norm=m_hyb_norm, hyb_w_in=m_hyb_w_in, ssd_conv_w=m_ssd_conv_w, ssd_conv_b=m_ssd_conv_b, ssd_dt_bias=m_ssd_dt_bias,
                ssd_a_log=m_ssd_a_log, ssd_d=m_ssd_d, ssd_out_norm=m_ssd_out_norm, sb_q_norm=m_sb_q_norm, sb_k_norm=m_sb_k_norm,
                hyb_w_out=m_hyb_w_out, pool_norm=m_pool_norm, pool_w=m_pool_w, pool_b=m_pool_b, pool_scale=m_pool_scale,
                mlp_norm=m_mlp_norm, mlp_w_up=m_mlp_w_up, mlp_w_down=m_mlp_w_down)
    vels = dict(hyb_norm=v_hyb_norm, hyb_w_in=v_hyb_w_in, ssd_conv_w=v_ssd_conv_w, ssd_conv_b=v_ssd_conv_b, ssd_dt_bias=v_ssd_dt_bias,
                ssd_a_log=v_ssd_a_log, ssd_d=v_ssd_d, ssd_out_norm=v_ssd_out_norm, sb_q_norm=v_sb_q_norm, sb_k_norm=v_sb_k_norm,
                hyb_w_out=v_hyb_w_out, pool_norm=v_pool_norm, pool_w=v_pool_w, pool_b=v_pool_b, pool_scale=v_pool_scale,
                mlp_norm=v_mlp_norm, mlp_w_up=v_mlp_w_up, mlp_w_down=v_mlp_w_down)
    order = list(weights)
    grads, delta, new_m, new_v = {}, {}, {}, {}
    for name, g2 in (("hyb_w_in", gb_in), ("hyb_w_out", gb_out), ("pool_w", gb_pool), ("mlp_w_up", gb_up), ("mlp_w_down", gb_down)):
        shp = weights[name].shape
        d_, m_, v_ = _adamw(weights[name].reshape(g2.shape), g2, moms[name].reshape(g2.shape), vels[name].reshape(g2.shape),
                            f"adamw_{name}")
        grads[name], delta[name], new_m[name], new_v[name] = (t.reshape(shp) for t in (g2, d_, m_, v_))
    snames = list(small_grads)
    pack = lambda d: jnp.concatenate([d[n].reshape(-1) for n in snames])
    nsm = sum(small_grads[n].size for n in snames)
    cols = -(-nsm // (8 * LANES)) * LANES
    as_blk = lambda v: jnp.pad(v, (0, 8 * cols - nsm)).reshape(8, cols)
    padded_v = jnp.pad(pack(vels), (0, 8 * cols - nsm), constant_values=1.0).reshape(8, cols)
    d_, m_, v_ = _adamw(as_blk(pack(weights)), as_blk(pack(small_grads)), as_blk(pack(moms)), padded_v, "adamw_small")
    off = 0
    for n in snames:
        sz, shp = small_grads[n].size, weights[n].shape
        grads[n] = small_grads[n]
        delta[n], new_m[n], new_v[n] = (t.reshape(-1)[off:off + sz].reshape(shp) for t in (d_, m_, v_))
        off += sz

    return (loss, grad_x.reshape(x.shape), *[grads[n] for n in order], *[delta[n] for n in order],
            *[new_m[n] for n in order], *[new_v[n] for n in order])
```

```python
import functools
import math

import jax
import jax.numpy as jnp
from jax import lax
from jax.experimental import pallas as pl
from jax.experimental.pallas import tpu as pltpu

f32 = jnp.float32
bf16 = jnp.bfloat16

EPS = 1e-6
SSD_HEAD_DIM = 64
SSD_STATE = 128
SSD_GROUPS = 4
SSD_CHUNK = 128
LANES = 128
SB_HEAD_DIM = 128
POOL_WINDOWS = (2, 4, 8, 16)
POOL_HALO = 16
CONV_HALO = 8
ADAM_LR, ADAM_B1, ADAM_B2, ADAM_EPS, ADAM_WD, ADAM_STEP = 0.001, 0.9, 0.999, 1e-08, 0.01, 10
VMEM_LIMIT = 56 * 1024 * 1024
MM_TILE_BUDGET = 40 * 1024 * 1024
N_CHIPS = 4
MESH = pl.DeviceIdType.MESH

_DIMS = {"nn": (((1,), (0,)), ((), ())), "nt": (((1,), (1,)), ((), ())), "tn": (((0,), (0,)), ((), ()))}


def _fit(n, t):
    if n <= t:
        return n
    return max(d for d in range(LANES, t + 1, LANES) if n % d == 0)


def _cp(*sem):
    return pltpu.CompilerParams(dimension_semantics=sem, vmem_limit_bytes=VMEM_LIMIT)


def _sigmoid(v):
    return 1.0 / (1.0 + jnp.exp(-v))


def _softplus(v):
    return jnp.maximum(v, 0.0) + jnp.log(1.0 + jnp.exp(-jnp.abs(v)))


def _split(v, parts):
    out, rem = [], v
    for _ in range(parts):
        p = rem.astype(bf16)
        out.append(p)
        rem = rem - p.astype(f32)
    return out


def _dot(a, b, mode="nn"):
    return lax.dot_general(a, b, _DIMS[mode], preferred_element_type=f32)


def _mask_dot(mask_b, v, parts, mode="nn"):
    acc = None
    for p in _split(v, parts):
        t = _dot(mask_b, p, mode)
        acc = t if acc is None else acc + t
    return acc


def _dot_mask(v, mask_b, parts):
    acc = None
    for p in _split(v, parts):
        t = _dot(p, mask_b)
        acc = t if acc is None else acc + t
    return acc


def _stacked(view, br, bc, rmap, cmap):
    kind, per, layer, rows_per_layer = view
    if kind == "cols":
        npc = per // bc
        return pl.BlockSpec((None, br, bc), lambda i, j, k: (cmap(i, j, k) // npc, layer * (rows_per_layer // br) + rmap(i, j, k),
                                                              cmap(i, j, k) % npc))
    npc = per // br
    return pl.BlockSpec((None, br, bc), lambda i, j, k: (rmap(i, j, k) // npc, layer * npc + rmap(i, j, k) % npc, cmap(i, j, k)))


def _pick_tiles(M, N, K, caps, a_bytes, b_bytes, io_bytes):
    def cands(n, cap, sizes):
        got = [s for s in sizes if s <= min(n, cap) and n % s == 0]
        return got or [_fit(n, min(n, cap))]

    best = None
    for tk in cands(K, caps[2], (8192, 4096, 2048, 1024, 512, 256, 128)):
        for tm in cands(M, caps[0], (1024, 512, 256, 128)):
            for tn in cands(N, caps[1], (1024, 512, 256, 128)):
                need = 2 * (tm * tk * a_bytes + tk * tn * b_bytes) + tm * tn * (2 * io_bytes + (4 if tk < K else 0))
                key = (need <= MM_TILE_BUDGET, tk, tm * tn, tm)
                if best is None or key > best[0]:
                    best = (key, (tm, tn, tk))
    return best[1]


def _mm(a, b, mode, M, N, K, outs, name, epilogue=None, extras=(), a_off=(0, 0), b_off=(0, 0),
        b_view=None, out_view=None, out_stack=None, alias=None, comm=None):
    caps = [M, N, K]
    if b_view is not None:
        caps[1 if (b_view[0] == "cols") == (mode != "nt") else 2] = b_view[1]
    if out_view is not None:
        d = 1 if out_view[0] == "cols" else 0
        caps[d] = min(caps[d], out_view[1])
    for off, dims in ((a_off, (2, 0) if mode == "tn" else (0, 2)), (b_off, (1, 2) if mode == "nt" else (2, 1))):
        for o, d in zip(off, dims):
            if o:
                caps[d] = min(caps[d], math.gcd(o, caps[d]))
    io_bytes = sum(jnp.dtype(dt).itemsize for dt in outs) + sum(e[0].dtype.itemsize for e in extras if e[1] == "tile")
    tm, tn, tk = _pick_tiles(M, N, K, caps, a.dtype.itemsize, b.dtype.itemsize, io_bytes)
    nk = K // tk
    if mode == "tn":
        a_blk, ad = (tk, tm), (tk, tm)
    else:
        a_blk, ad = (tm, tk), (tm, tk)
    b_blk = (tn, tk) if mode == "nt" else (tk, tn)
    assert a_off[0] % ad[0] == 0 and a_off[1] % ad[1] == 0 and b_off[0] % b_blk[0] == 0 and b_off[1] % b_blk[1] == 0
    ao = (a_off[0] // ad[0], a_off[1] // ad[1])
    bo = (b_off[0] // b_blk[0], b_off[1] // b_blk[1])
    if mode == "tn":
        a_map = lambda i, j, k: (k + ao[0], i + ao[1])
    else:
        a_map = lambda i, j, k: (i + ao[0], k + ao[1])
    if mode == "nt":
        b_map = lambda i, j, k: (j + bo[0], k + bo[1])
    else:
        b_map = lambda i, j, k: (k + bo[0], j + bo[1])
    if b_view is not None:
        if mode == "nt":
            b_spec = _stacked(b_view, tn, tk, lambda i, j, k: j, lambda i, j, k: k)
        else:
            b_spec = _stacked(b_view, tk, tn, lambda i, j, k: k, lambda i, j, k: j)
    else:
        b_spec = pl.BlockSpec(b_blk, b_map)
    in_specs = [pl.BlockSpec(a_blk, a_map), b_spec]
    for arr, kind in extras:
        if kind == "tile":
            in_specs.append(pl.BlockSpec((tm, tn), lambda i, j, k: (i, j)))
        else:
            in_specs.append(pl.BlockSpec((1, tn), lambda i, j, k: (0, j)))
    ne, no = len(extras), len(outs)
    if epilogue is None:
        epilogue = lambda acc: (acc,)
    operands = [a, b, *[e[0] for e in extras]]
    aliases = {}
    if alias is not None:
        in_specs.append(ANY)
        aliases[len(operands)] = 0
        operands.append(alias)
    n_in = len(operands)
    if out_view is not None:
        out_specs = [_stacked(out_view, tm, tn, lambda i, j, k: i, lambda i, j, k: j)]
        out_shape = [jax.ShapeDtypeStruct(out_stack, outs[0])]
    else:
        out_specs = [pl.BlockSpec((tm, tn), lambda i, j, k: (i, j)) for _ in outs]
        out_shape = [jax.ShapeDtypeStruct((M, N), dt) for dt in outs]
    scratch = [pltpu.VMEM((tm, tn), f32)] if nk > 1 else []
    grid = (M // tm, N // tn, nk)
    if comm is not None:
        in_specs += [ANY] * len(comm["ins"])
        operands += comm["ins"]
        out_specs += [ANY] * len(comm["outs"])
        out_shape += comm["outs"]
        scratch += comm["sems"]
    nci, nco, ncs = (len(comm["ins"]), len(comm["outs"]), len(comm["sems"])) if comm is not None else (0, 0, 0)

    def body(*refs):
        a_ref, b_ref = refs[0], refs[1]
        ex, out_refs = refs[2:2 + ne], refs[n_in + nci:n_in + nci + no]
        rest = refs[n_in + nci + no + nco:]
        if comm is not None:
            cargs = (refs[n_in:n_in + nci], refs[n_in + nci + no:n_in + nci + no + nco], refs[len(refs) - ncs:])
            pid = [pl.program_id(d) for d in range(3)]

            @pl.when((pid[0] == 0) & (pid[1] == 0) & (pid[2] == 0))
            def _():
                comm["start"](*cargs)

        def finish(acc):
            res = epilogue(acc, *[e[...] for e in ex])
            for o, r in zip(out_refs, res):
                o[...] = r.astype(o.dtype)

        prod = lax.dot_general(a_ref[...].astype(bf16), b_ref[...].astype(bf16), _DIMS[mode],
                               preferred_element_type=f32)
        if nk == 1:
            finish(prod)
        else:
            acc_ref = rest[0]
            k = pl.program_id(2)

            @pl.when(k == 0)
            def _():
                acc_ref[...] = prod

            @pl.when(k > 0)
            def _():
                acc_ref[...] += prod

            @pl.when(k == nk - 1)
            def _():
                finish(acc_ref[...])

        if comm is not None:
            @pl.when((pid[0] == grid[0] - 1) & (pid[1] == grid[1] - 1) & (pid[2] == grid[2] - 1))
            def _():
                comm["finish"](*cargs)

    sem = ("arbitrary",) * 3 if comm is not None else ("parallel", "parallel", "arbitrary")
    res = pl.pallas_call(
        body, name=name, grid=grid, in_specs=in_specs, out_specs=out_specs, out_shape=out_shape,
        scratch_shapes=scratch, input_output_aliases=aliases, compiler_params=_cp(*sem),
    )(*operands)
    if comm is not None:
        return res[:no], res[no:]
    return res[0] if no == 1 else res


def _rms_fwd(x, g, out_dtype, name, tr=256):
    T, D = x.shape
    tr = min(tr, T)

    def body(x_ref, g_ref, o_ref):
        xv = x_ref[...]
        r = lax.rsqrt(jnp.mean(xv * xv, axis=-1, keepdims=True) + EPS)
        o_ref[...] = (xv * r * g_ref[...]).astype(o_ref.dtype)

    return pl.pallas_call(
        body, name=name, grid=(T // tr,),
        in_specs=[pl.BlockSpec((tr, D), lambda r: (r, 0)), pl.BlockSpec((1, D), lambda r: (0, 0))],
        out_specs=pl.BlockSpec((tr, D), lambda r: (r, 0)),
        out_shape=jax.ShapeDtypeStruct((T, D), out_dtype), compiler_params=_cp("parallel"),
    )(x, g)


def _rms_bwd(x, g, dh, dres, name, tr=256):
    T, D = x.shape
    tr = min(tr, T)

    def body(x_ref, g_ref, dh_ref, dres_ref, dx_ref, dg_ref):
        xv = x_ref[...]
        r = lax.rsqrt(jnp.mean(xv * xv, axis=-1, keepdims=True) + EPS)
        xh = xv * r
        dhv = dh_ref[...]
        dhg = dhv * g_ref[...]
        dx_ref[...] = dres_ref[...] + r * (dhg - xh * jnp.mean(dhg * xh, axis=-1, keepdims=True))

        @pl.when(pl.program_id(0) == 0)
        def _():
            dg_ref[...] = jnp.zeros_like(dg_ref)

        dg_ref[...] += jnp.sum(dhv * xh, axis=0, keepdims=True)

    row = pl.BlockSpec((tr, D), lambda r: (r, 0))
    vec = pl.BlockSpec((1, D), lambda r: (0, 0))
    return pl.pallas_call(
        body, name=name, grid=(T // tr,), in_specs=[row, vec, row, row], out_specs=[row, vec],
        out_shape=[jax.ShapeDtypeStruct((T, D), f32), jax.ShapeDtypeStruct((1, D), f32)],
        compiler_params=_cp("arbitrary"),
    )(x, g, dh, dres)


def _loss_grad(y, tgt, name, tr=256):
    T, D = y.shape
    tr = min(tr, T)

    def body(y_ref, t_ref, dy_ref, s_ref):
        e = y_ref[...] - t_ref[...]
        dy_ref[...] = e * (1.0 / D)

        @pl.when(pl.program_id(0) == 0)
        def _():
            s_ref[...] = jnp.zeros_like(s_ref)

        s_ref[...] += jnp.sum(e * e)

    row = pl.BlockSpec((tr, D), lambda r: (r, 0))
    return pl.pallas_call(
        body, name=name, grid=(T // tr,), in_specs=[row, row],
        out_specs=[row, pl.BlockSpec((8, LANES), lambda r: (0, 0))],
        out_shape=[jax.ShapeDtypeStruct((T, D), f32), jax.ShapeDtypeStruct((8, LANES), f32)],
        compiler_params=_cp("arbitrary"),
    )(y, tgt)


def _shift_down(cur, prev, s):
    rolled = pltpu.roll(cur, s, 0)
    top = pltpu.roll(prev, s, 0)
    row = lax.broadcasted_iota(jnp.int32, top.shape, 0)
    head = jnp.where(row < s, top, rolled[0:CONV_HALO])
    return jnp.concatenate([head, rolled[CONV_HALO:]], axis=0)


def _shift_up(cur, nxt, s):
    n = cur.shape[0]
    rolled = pltpu.roll(cur, n - s, 0)
    bot = pltpu.roll(nxt, CONV_HALO - s, 0)
    row = lax.broadcasted_iota(jnp.int32, bot.shape, 0)
    tail = jnp.where(row >= CONV_HALO - s, bot, rolled[n - CONV_HALO:])
    return jnp.concatenate([rolled[:n - CONV_HALO], tail], axis=0)


def _conv_pre(cur, prev, w_ref, b_ref):
    taps = [cur] + [_shift_down(cur, prev, s) for s in (1, 2, 3)]
    pre = b_ref[...] + w_ref[3:4, :] * taps[0]
    for s in (1, 2, 3):
        pre = pre + w_ref[3 - s:4 - s, :] * taps[s]
    return pre, taps


def _conv_specs(T, C, rc, cb):
    cur = pl.BlockSpec((rc, cb), lambda j, r: (r, j))
    prev = pl.BlockSpec((CONV_HALO, cb), lambda j, r: (jnp.maximum(r * (rc // CONV_HALO) - 1, 0), j))
    nxt = pl.BlockSpec((CONV_HALO, cb), lambda j, r: (jnp.minimum((r + 1) * (rc // CONV_HALO), T // CONV_HALO - 1), j))
    w = pl.BlockSpec((4, cb), lambda j, r: (0, j))
    b = pl.BlockSpec((1, cb), lambda j, r: (0, j))
    return cur, prev, nxt, w, b


def _conv_fwd(xraw, w, b, name):
    T, C = xraw.shape
    rc, cb = min(512, T), min(512, C)
    cur, prev, _, ws, bs = _conv_specs(T, C, rc, cb)

    def body(x_ref, p_ref, w_ref, b_ref, o_ref):
        pv = jnp.where(pl.program_id(1) > 0, p_ref[...], 0.0)
        pre, _ = _conv_pre(x_ref[...], pv, w_ref, b_ref)
        o_ref[...] = pre * _sigmoid(pre)

    return pl.pallas_call(
        body, name=name, grid=(C // cb, T // rc), in_specs=[cur, prev, ws, bs], out_specs=cur,
        out_shape=jax.ShapeDtypeStruct((T, C), f32), compiler_params=_cp("parallel", "parallel"),
    )(xraw, xraw, w, b)


def _conv_bwd_pre(xraw, dxbc, w, b, name):
    T, C = xraw.shape
    rc, cb = min(512, T), min(512, C)
    cur, prev, _, ws, bs = _conv_specs(T, C, rc, cb)

    def body(x_ref, p_ref, d_ref, w_ref, b_ref, dpre_ref, dw_ref, db_ref):
        pv = jnp.where(pl.program_id(1) > 0, p_ref[...], 0.0)
        pre, taps = _conv_pre(x_ref[...], pv, w_ref, b_ref)
        sg = _sigmoid(pre)
        dpre = d_ref[...] * (sg * (1.0 + pre * (1.0 - sg)))
        dpre_ref[...] = dpre

        @pl.when(pl.program_id(1) == 0)
        def _():
            dw_ref[...] = jnp.zeros_like(dw_ref)
            db_ref[...] = jnp.zeros_like(db_ref)

        row = lax.broadcasted_iota(jnp.int32, dw_ref.shape, 0)
        upd = jnp.zeros(dw_ref.shape, f32)
        for s in range(4):
            upd = upd + jnp.where(row == 3 - s, jnp.sum(dpre * taps[s], axis=0, keepdims=True), 0.0)
        dw_ref[...] += upd
        db_ref[...] += jnp.sum(dpre, axis=0, keepdims=True)

    return pl.pallas_call(
        body, name=name, grid=(C // cb, T // rc), in_specs=[cur, prev, cur, ws, bs], out_specs=[cur, ws, bs],
        out_shape=[jax.ShapeDtypeStruct((T, C), f32), jax.ShapeDtypeStruct((4, C), f32), jax.ShapeDtypeStruct((1, C), f32)],
        compiler_params=_cp("parallel", "arbitrary"),
    )(xraw, xraw, dxbc, w, b)


def _conv_bwd_in(dpre, w, name):
    T, C = dpre.shape
    rc, cb = min(512, T), min(512, C)
    cur, _, nxt, ws, _ = _conv_specs(T, C, rc, cb)
    nr = T // rc

    def body(d_ref, n_ref, w_ref, o_ref):
        nv = jnp.where(pl.program_id(1) < nr - 1, n_ref[...], 0.0)
        cv = d_ref[...]
        out = w_ref[3:4, :] * cv
        for s in (1, 2, 3):
            out = out + w_ref[3 - s:4 - s, :] * _shift_up(cv, nv, s)
        o_ref[...] = out.astype(o_ref.dtype)

    return pl.pallas_call(
        body, name=name, grid=(C // cb, nr), in_specs=[cur, nxt, ws], out_specs=cur,
        out_shape=jax.ShapeDtypeStruct((T, C), bf16), compiler_params=_cp("parallel", "parallel"),
    )(dpre, dpre, w)


HEAD_SHIFT = SSD_HEAD_DIM.bit_length() - 1


def _ssd_prep(dtr_ref, bias_ref, alog_ref, SW):
    L = SSD_CHUNK
    xs = dtr_ref[...] + bias_ref[...]
    dt = _softplus(xs)
    a = -jnp.exp(alog_ref[...])
    causal = lax.broadcasted_iota(jnp.int32, (L, L), 0) >= lax.broadcasted_iota(jnp.int32, (L, L), 1)
    cs = _mask_dot(causal.astype(bf16), dt * a, 3)
    spread = (lax.broadcasted_iota(jnp.int32, (LANES, SW), 0)
              == lax.shift_right_logical(lax.broadcasted_iota(jnp.int32, (LANES, SW), 1), HEAD_SHIFT)).astype(bf16)
    dt_x = _dot_mask(dt, spread, 3)
    cs_x = _dot_mask(cs, spread, 3)
    last_x = cs_x[L - 1:L, :]
    return xs, dt, a, causal, cs, cs.T, dt_x, jnp.exp(cs_x), jnp.exp(last_x - cs_x), jnp.exp(last_x)


def _head_decay(cs, csT, causal, h):
    seg = cs[:, h:h + 1] - csT[h:h + 1, :]
    return jnp.where(causal, jnp.exp(jnp.minimum(seg, 0.0)), 0.0)


def _ssd_fwd(xbc, dtraw, z, dt_bias, a_log, d_skip_x, out_norm, HS, name, comm=None):
    T = xbc.shape[0]
    L, P, NS, G = SSD_CHUNK, SSD_HEAD_DIM, SSD_STATE, SSD_GROUPS
    SW, HPG, nc = HS * P, HS // SSD_GROUPS, T // SSD_CHUNK
    gsz = SW // G
    gw = HPG * P
    assert HPG % 2 == 0 and 2 * P == LANES

    def body(*refs):
        (xbc_ref, dtr_ref, z_ref, bias_ref, alog_ref, dsk_ref, on_ref), (y_ref, yn_ref, sp_ref), (st_ref,), cargs = _hosted(comm, 7, 3, refs)
        first, last = _first_last((nc,))
        if comm is not None:
            @pl.when(first)
            def _():
                comm["start"](*cargs)

            @pl.when(last)
            def _():
                comm["finish"](*cargs)

        @pl.when(pl.program_id(0) == 0)
        def _():
            st_ref[...] = jnp.zeros_like(st_ref)

        sp_ref[0] = st_ref[...]
        _, _, _, causal, cs, csT, dt_x, ecs_x, dte_x, cdec_x = _ssd_prep(dtr_ref, bias_ref, alog_ref, SW)
        X = xbc_ref[:, 0:SW]
        Xd = X * dt_x
        Xdb = Xd.astype(bf16)
        XEb = (Xd * dte_x).astype(bf16)
        left = lax.broadcasted_iota(jnp.int32, (L, LANES), 1) < P
        for g in range(G):
            gs = slice(g * gw, (g + 1) * gw)
            Bb = xbc_ref[:, SW + g * NS:SW + (g + 1) * NS].astype(bf16)
            Cb = xbc_ref[:, SW + (G + g) * NS:SW + (G + g + 1) * NS].astype(bf16)
            Gm = _dot(Cb, Bb, "nt")
            Sp = st_ref[:, gs]
            yo = _dot(Cb, Sp.astype(bf16)) * ecs_x[:, gs]
            st_ref[:, gs] = cdec_x[:, gs] * Sp + _dot(Bb, XEb[:, gs], "tn")
            for pr in range(HPG // 2):
                h0 = g * HPG + 2 * pr
                ps = slice(h0 * P, (h0 + 2) * P)
                xp = Xdb[:, ps]
                yd = jnp.where(left, _dot((_head_decay(cs, csT, causal, h0) * Gm).astype(bf16), xp),
                               _dot((_head_decay(cs, csT, causal, h0 + 1) * Gm).astype(bf16), xp))
                y_ref[:, ps] = yd + yo[:, pr * LANES:(pr + 1) * LANES] + dsk_ref[:, ps] * X[:, ps]
        zz = z_ref[...]
        gated = y_ref[...] * (zz * _sigmoid(zz))
        for g in range(G):
            gs = slice(g * gsz, (g + 1) * gsz)
            sg = gated[:, gs]
            rr = lax.rsqrt(jnp.mean(sg * sg, axis=-1, keepdims=True) + EPS)
            yn_ref[:, gs] = (sg * rr * on_ref[:, gs]).astype(yn_ref.dtype)

    vec = pl.BlockSpec((1, LANES), lambda c: (0, 0))
    wide = pl.BlockSpec((1, SW), lambda c: (0, 0))
    (y, yn, sp), got = _host_call(
        body, name, (nc,),
        [pl.BlockSpec((L, xbc.shape[1]), lambda c: (c, 0)), pl.BlockSpec((L, LANES), lambda c: (c, 0)),
         pl.BlockSpec((L, SW), lambda c: (c, 0)), vec, vec, wide, wide],
        [pl.BlockSpec((L, SW), lambda c: (c, 0)), pl.BlockSpec((L, SW), lambda c: (c, 0)),
         pl.BlockSpec((1, NS, SW), lambda c: (c, 0, 0))],
        [jax.ShapeDtypeStruct((T, SW), f32), jax.ShapeDtypeStruct((T, SW), bf16), jax.ShapeDtypeStruct((nc, NS, SW), f32)],
        [pltpu.VMEM((NS, SW), f32)], [xbc, dtraw, z, dt_bias, a_log, d_skip_x, out_norm], comm)
    return y, yn, sp, got


def _ssd_bwd(xbc, dtraw, sprev, dy, dt_bias, a_log, d_skip_x, HS, name):
    T = xbc.shape[0]
    L, P, NS, G = SSD_CHUNK, SSD_HEAD_DIM, SSD_STATE, SSD_GROUPS
    SW, HPG, nc = HS * P, HS // SSD_GROUPS, T // SSD_CHUNK
    gw = HPG * P

    def body(xbc_ref, dtr_ref, sp_ref, dy_ref, bias_ref, alog_ref, dsk_ref,
             dxbc_ref, ddtr_ref, dalog_ref, dbias_ref, dd_ref, ds_ref):
        @pl.when(pl.program_id(0) == 0)
        def _():
            ds_ref[...] = jnp.zeros_like(ds_ref)
            dalog_ref[...] = jnp.zeros_like(dalog_ref)
            dbias_ref[...] = jnp.zeros_like(dbias_ref)
            dd_ref[...] = jnp.zeros_like(dd_ref)

        xs, dt, a, causal, cs, csT, dt_x, ecs_x, dte_x, cdec_x = _ssd_prep(dtr_ref, bias_ref, alog_ref, SW)
        lane = lax.broadcasted_iota(jnp.int32, (L, LANES), 1)
        sub = lax.broadcasted_iota(jnp.int32, (LANES, L), 0)
        left = lane < P
        dcs = jnp.zeros((L, LANES), f32)
        dcs_t = jnp.zeros((LANES, L), f32)
        xds = jnp.zeros((L, LANES), f32)
        dlast = jnp.zeros((1, LANES), f32)
        dD = jnp.zeros((1, LANES), f32)
        for g in range(G):
            gs = slice(g * gw, (g + 1) * gw)
            bsl = slice(SW + g * NS, SW + (g + 1) * NS)
            csl = slice(SW + (G + g) * NS, SW + (G + g + 1) * NS)
            Bb = xbc_ref[:, bsl].astype(bf16)
            Cb = xbc_ref[:, csl].astype(bf16)
            Gm = _dot(Cb, Bb, "nt")
            X = xbc_ref[:, gs]
            Xd = X * dt_x[:, gs]
            Xdb = Xd.astype(bf16)
            XE = Xd * dte_x[:, gs]
            dY = dy_ref[:, gs]
            dYb = dY.astype(bf16)
            Wb = (dY * ecs_x[:, gs]).astype(bf16)
            Sp = sp_ref[0, :, gs]
            Spb = Sp.astype(bf16)
            dS = ds_ref[:, gs]
            dSb = dS.astype(bf16)
            CS = _dot(Cb, Spb)
            Zb = _dot(Bb, dSb)
            dC = _dot(Wb, Spb, "nt")
            dB = _dot(XE.astype(bf16), dSb, "nt")
            ds_ref[:, gs] = cdec_x[:, gs] * dS + _dot(Cb, Wb, "tn")
            R1 = dY * CS * ecs_x[:, gs]
            R2 = XE * Zb
            to_head = (lax.shift_right_logical(lax.broadcasted_iota(jnp.int32, (gw, LANES), 0), HEAD_SHIFT) + g * HPG
                       == lax.broadcasted_iota(jnp.int32, (gw, LANES), 1)).astype(bf16)
            dcs = dcs + _dot_mask(R1 - R2, to_head, 3)
            dlast = (dlast + jnp.sum(_dot_mask(R2, to_head, 3), axis=0, keepdims=True)
                     + jnp.sum(_dot_mask(Sp * dS * cdec_x[:, gs], to_head, 3), axis=0, keepdims=True))
            dG = jnp.zeros((L, L), f32)
            pieces = []
            for pr in range(HPG // 2):
                h0 = g * HPG + 2 * pr
                pw = slice(pr * LANES, (pr + 1) * LANES)
                xp, dyp = Xdb[:, pw], dYb[:, pw]
                halves = []
                for k, h in enumerate((h0, h0 + 1)):
                    Lm = _head_decay(cs, csT, causal, h)
                    Mf = Lm * Gm
                    keep = left if k == 0 else jnp.logical_not(left)
                    dM = _dot(jnp.where(keep, dyp, jnp.zeros_like(dyp)), xp, "nt")
                    Q = dM * Mf
                    dcs = dcs + jnp.where(lane == h, jnp.sum(Q, axis=1, keepdims=True), 0.0)
                    dcs_t = dcs_t - jnp.where(sub == h, jnp.sum(Q, axis=0, keepdims=True), 0.0)
                    dG = dG + dM * Lm
                    halves.append(_dot(Mf.astype(bf16), dyp, "tn"))
                pieces.append(jnp.where(left, halves[0], halves[1]))
            dXd = jnp.concatenate(pieces, axis=1) + dte_x[:, gs] * Zb
            dxbc_ref[:, gs] = dXd * dt_x[:, gs] + dsk_ref[:, gs] * dY
            xds = xds + _dot_mask(dXd * X, to_head, 3)
            dD = dD + jnp.sum(_dot_mask(dY * X, to_head, 3), axis=0, keepdims=True)
            dGb = dG.astype(bf16)
            dxbc_ref[:, bsl] = dB + _dot(dGb, Cb, "tn")
            dxbc_ref[:, csl] = dC + _dot(dGb, Bb)
        rowi = lax.broadcasted_iota(jnp.int32, (L, LANES), 0)
        dcs = dcs + dcs_t.T + jnp.where(rowi == L - 1, dlast, 0.0)
        anti = (lax.broadcasted_iota(jnp.int32, (L, L), 1) >= lax.broadcasted_iota(jnp.int32, (L, L), 0)).astype(bf16)
        dda = _mask_dot(anti, dcs, 3)
        ddt = dda * a + xds
        dalog_ref[...] += jnp.sum(dda * dt, axis=0, keepdims=True) * a
        ddtr = ddt * _sigmoid(xs)
        ddtr_ref[...] = ddtr
        dbias_ref[...] += jnp.sum(ddtr, axis=0, keepdims=True)
        dd_ref[...] += dD

    rev = lambda c: (nc - 1 - c, 0)
    vec = pl.BlockSpec((1, LANES), lambda c: (0, 0))
    return pl.pallas_call(
        body, name=name, grid=(nc,),
        in_specs=[pl.BlockSpec((L, xbc.shape[1]), rev), pl.BlockSpec((L, LANES), rev),
                  pl.BlockSpec((1, NS, SW), lambda c: (nc - 1 - c, 0, 0)), pl.BlockSpec((L, SW), rev), vec, vec,
                  pl.BlockSpec((1, SW), lambda c: (0, 0))],
        out_specs=[pl.BlockSpec((L, xbc.shape[1]), rev), pl.BlockSpec((L, LANES), rev), vec, vec, vec],
        out_shape=[jax.ShapeDtypeStruct(xbc.shape, f32), jax.ShapeDtypeStruct((T, LANES), f32)]
        + [jax.ShapeDtypeStruct((1, LANES), f32)] * 3,
        scratch_shapes=[pltpu.VMEM((NS, SW), f32)], compiler_params=_cp("arbitrary"),
    )(xbc, dtraw, sprev, dy, dt_bias, a_log, d_skip_x)


def _gate_bwd(y, z, dyn, out_norm, name, tr=256):
    T, SW = y.shape
    tr = min(tr, T)
    gsz = SW // SSD_GROUPS

    def body(y_ref, z_ref, d_ref, on_ref, dy_ref, dz_ref, don_ref):
        @pl.when(pl.program_id(0) == 0)
        def _():
            don_ref[...] = jnp.zeros_like(don_ref)

        for g in range(SSD_GROUPS):
            gs = slice(g * gsz, (g + 1) * gsz)
            yv, zv, dv = y_ref[:, gs], z_ref[:, gs], d_ref[:, gs]
            sg = _sigmoid(zv)
            sl = zv * sg
            gated = yv * sl
            rr = lax.rsqrt(jnp.mean(gated * gated, axis=-1, keepdims=True) + EPS)
            gh = gated * rr
            dgn = dv * on_ref[:, gs]
            dgated = rr * (dgn - gh * jnp.mean(dgn * gh, axis=-1, keepdims=True))
            dy_ref[:, gs] = dgated * sl
            dz_ref[:, gs] = (dgated * yv * (sg * (1.0 + zv * (1.0 - sg)))).astype(dz_ref.dtype)
            don_ref[:, gs] += jnp.sum(dv * gh, axis=0, keepdims=True)

    row = pl.BlockSpec((tr, SW), lambda r: (r, 0))
    vec = pl.BlockSpec((1, SW), lambda r: (0, 0))
    return pl.pallas_call(
        body, name=name, grid=(T // tr,), in_specs=[row, row, row, vec], out_specs=[row, row, vec],
        out_shape=[jax.ShapeDtypeStruct((T, SW), f32), jax.ShapeDtypeStruct((T, SW), bf16),
                   jax.ShapeDtypeStruct((1, SW), f32)],
        compiler_params=_cp("arbitrary"),
    )(y, z, dyn, out_norm)


def _qk_norm_fwd(qkv, qn_w, kn_w, SBW, name, tr=256):
    T = qkv.shape[0]
    tr = min(tr, T)
    nh = SBW // SB_HEAD_DIM

    def body(q_ref, k_ref, v_ref, qw_ref, kw_ref, qo_ref, ko_ref, vo_ref):
        for src, w_ref, dst in ((q_ref, qw_ref, qo_ref), (k_ref, kw_ref, ko_ref)):
            for h in range(nh):
                hs = slice(h * SB_HEAD_DIM, (h + 1) * SB_HEAD_DIM)
                sv = src[:, hs]
                rr = lax.rsqrt(jnp.mean(sv * sv, axis=-1, keepdims=True) + EPS)
                dst[:, hs] = (sv * rr * w_ref[...]).astype(dst.dtype)
        vo_ref[...] = v_ref[...].astype(vo_ref.dtype)

    blk = lambda j: pl.BlockSpec((tr, SBW), lambda r: (r, j))
    vec = pl.BlockSpec((1, SB_HEAD_DIM), lambda r: (0, 0))
    out = pl.BlockSpec((tr, SBW), lambda r: (r, 0))
    return pl.pallas_call(
        body, name=name, grid=(T // tr,), in_specs=[blk(0), blk(1), blk(2), vec, vec], out_specs=[out, out, out],
        out_shape=[jax.ShapeDtypeStruct((T, SBW), bf16)] * 3, compiler_params=_cp("parallel"),
    )(qkv, qkv, qkv, qn_w, kn_w)


def _qk_norm_bwd(qkv, dqn, dkn, dv, qn_w, kn_w, SBW, name, tr=256):
    T = qkv.shape[0]
    tr = min(tr, T)
    nh = SBW // SB_HEAD_DIM

    def body(q_ref, k_ref, dq_ref, dk_ref, dv_ref, qw_ref, kw_ref, o_ref, dqw_ref, dkw_ref):
        @pl.when(pl.program_id(0) == 0)
        def _():
            dqw_ref[...] = jnp.zeros_like(dqw_ref)
            dkw_ref[...] = jnp.zeros_like(dkw_ref)

        for part, (src, d_ref, w_ref, dw_ref) in enumerate(((q_ref, dq_ref, qw_ref, dqw_ref), (k_ref, dk_ref, kw_ref, dkw_ref))):
            dw = jnp.zeros((1, SB_HEAD_DIM), f32)
            for h in range(nh):
                hs = slice(h * SB_HEAD_DIM, (h + 1) * SB_HEAD_DIM)
                os_ = slice(part * SBW + h * SB_HEAD_DIM, part * SBW + (h + 1) * SB_HEAD_DIM)
                sv, dn = src[:, hs], d_ref[:, hs]
                rr = lax.rsqrt(jnp.mean(sv * sv, axis=-1, keepdims=True) + EPS)
                xh = sv * rr
                dg = dn * w_ref[...]
                o_ref[:, os_] = (rr * (dg - xh * jnp.mean(dg * xh, axis=-1, keepdims=True))).astype(o_ref.dtype)
                dw = dw + jnp.sum(dn * xh, axis=0, keepdims=True)
            dw_ref[...] += dw
        o_ref[:, 2 * SBW:] = dv_ref[...].astype(o_ref.dtype)

    blk = lambda j: pl.BlockSpec((tr, SBW), lambda r: (r, j))
    vec = pl.BlockSpec((1, SB_HEAD_DIM), lambda r: (0, 0))
    row = pl.BlockSpec((tr, SBW), lambda r: (r, 0))
    return pl.pallas_call(
        body, name=name, grid=(T // tr,), in_specs=[blk(0), blk(1), row, row, row, vec, vec],
        out_specs=[pl.BlockSpec((tr, 3 * SBW), lambda r: (r, 0)), vec, vec],
        out_shape=[jax.ShapeDtypeStruct((T, 3 * SBW), bf16)] + [jax.ShapeDtypeStruct((1, SB_HEAD_DIM), f32)] * 2,
        compiler_params=_cp("arbitrary"),
    )(qkv, qkv, dqn, dkn, dv, qn_w, kn_w)


def _sb_logits(q, kb, scale):
    zl = _dot(q, kb, "nt") * scale
    lb = jnp.minimum(zl, 0.0) - jnp.log(1.0 + jnp.exp(-jnp.abs(zl)))
    return zl, lb, lb - zl


def _tail_update(old, r0, new_tail):
    return new_tail if r0 == 0 else jnp.concatenate([old[:r0], new_tail], axis=0)


def _hosted(comm, n_in, n_out, refs):
    nci, nco, ncs = (len(comm["ins"]), len(comm["outs"]), len(comm["sems"])) if comm is not None else (0, 0, 0)
    ins, outs = refs[:n_in], refs[n_in + nci:n_in + nci + n_out]
    scratch = refs[n_in + nci + n_out + nco:len(refs) - ncs]
    cargs = (refs[n_in:n_in + nci], refs[n_in + nci + n_out:n_in + nci + n_out + nco], refs[len(refs) - ncs:])
    return ins, outs, scratch, cargs


def _host_call(body, name, grid, in_specs, out_specs, out_shape, scratch, operands, comm):
    n_out = len(out_shape)
    in_specs, out_specs, out_shape, scratch, operands = list(in_specs), list(out_specs), list(out_shape), list(scratch), list(operands)
    io = {}
    if comm is not None:
        for src, dst in comm.get("aliases", {}).items():
            io[len(operands) + src] = n_out + dst
        in_specs += [ANY] * len(comm["ins"])
        operands += comm["ins"]
        out_specs += [ANY] * len(comm["outs"])
        out_shape += comm["outs"]
        scratch += comm["sems"]
    res = pl.pallas_call(body, name=name, grid=grid, in_specs=in_specs, out_specs=out_specs, out_shape=out_shape,
                         scratch_shapes=scratch, input_output_aliases=io,
                         compiler_params=_cp(*(("arbitrary",) * len(grid))))(*operands)
    return res[:n_out], res[n_out:]


def _first_last(grid):
    pid = [pl.program_id(d) for d in range(len(grid))]
    first, last = pid[0] == 0, pid[0] == grid[0] - 1
    for d in range(1, len(grid)):
        first, last = first & (pid[d] == 0), last & (pid[d] == grid[d] - 1)
    return first, last


def _sb_fwd(qn, kn, vb, name, comm=None, tq=1024, tk=256):
    T, W = qn.shape
    tq = min(tq, T)
    tk = min(tk, tq)
    nh, nq, dh, nd = W // SB_HEAD_DIM, T // tq, SB_HEAD_DIM, tq // tk
    scale = dh ** -0.5
    grid = (nh, nq)

    def body(*refs):
        (q_ref, k_ref, v_ref), (o_ref, c_ref), _, cargs = _hosted(comm, 3, 2, refs)
        first, last = _first_last(grid)
        if comm is not None:
            @pl.when(first)
            def _():
                comm["start"](*cargs)

        qi = pl.program_id(1)
        q = q_ref[...]
        later = (lax.broadcasted_iota(jnp.int32, (tk, tk), 0) > lax.broadcasted_iota(jnp.int32, (tk, tk), 1)).astype(bf16)

        def step(j, carry, d):
            acc, run = carry
            r0 = 0 if d is None else d * tk
            ks = pl.multiple_of(j * tk, tk)
            kb, vv = k_ref[pl.ds(ks, tk), :], v_ref[pl.ds(ks, tk), :]
            _, lb, lk = _sb_logits(q[r0:], kb, scale)
            if d is not None:
                mask = lax.broadcasted_iota(jnp.int32, lk.shape, 1) < lax.broadcasted_iota(jnp.int32, lk.shape, 0)
                lk = jnp.where(mask, lk, 0.0)
            between = _dot_mask(lk, later, 2)
            w = jnp.exp(lb + between + run[r0:])
            if d is not None:
                w = jnp.where(mask, w, 0.0)
            return (_tail_update(acc, r0, acc[r0:] + _dot(w.astype(bf16), vv)),
                    _tail_update(run, r0, run[r0:] + between[:, 0:1] + lk[:, 0:1]))

        carry = (jnp.zeros((tq, dh), f32), jnp.zeros((tq, 1), f32))
        for d in range(nd - 1, -1, -1):
            carry = step(qi * nd + d, carry, d)
        n_before = qi * nd
        acc, run = lax.fori_loop(0, n_before, lambda t, c: step(n_before - 1 - t, c, None), carry)
        o_ref[...] = acc.astype(o_ref.dtype)
        c_ref[...] = jnp.broadcast_to(run, (tq, dh))
        if comm is not None:
            @pl.when(last)
            def _():
                comm["finish"](*cargs)

    qblk = pl.BlockSpec((tq, dh), lambda h, i: (i, h))
    full = pl.BlockSpec((T, dh), lambda h, i: (0, h))
    (o, c), got = _host_call(body, name, grid, [qblk, full, full], [qblk, qblk],
                             [jax.ShapeDtypeStruct((T, W), bf16), jax.ShapeDtypeStruct((T, W), f32)], [], [qn, kn, vb], comm)
    return o, c, got


def _sb_bwd(qn, kn, vb, do, ctot, do_off, name, comm=None, tq=1024, tk=256):
    T, W = qn.shape
    tq = min(tq, T)
    tk = min(tk, tq)
    nh, nq, dh, nd = W // SB_HEAD_DIM, T // tq, SB_HEAD_DIM, tq // tk
    scale = dh ** -0.5
    ob = do_off // dh
    grid = (nh, nq)

    def body(*refs):
        (q_ref, k_ref, v_ref, do_ref, c_ref), (dq_ref, dk_ref, dv_ref), _, cargs = _hosted(comm, 5, 3, refs)
        first, last = _first_last(grid)
        if comm is not None:
            @pl.when(first)
            def _():
                comm["start"](*cargs)

        qi = pl.program_id(1)

        @pl.when(qi == 0)
        def _():
            dk_ref[...] = jnp.zeros_like(dk_ref)
            dv_ref[...] = jnp.zeros_like(dv_ref)

        q = q_ref[...]
        dob = do_ref[...].astype(bf16)
        total = c_ref[:, 0:1]
        r2 = lax.broadcasted_iota(jnp.int32, (tk, tk), 0)
        c2 = lax.broadcasted_iota(jnp.int32, (tk, tk), 1)
        upto = (r2 <= c2).astype(bf16)
        before = (r2 < c2).astype(bf16)

        def step(j, carry, d):
            dq, pre, gpre = carry
            r0 = 0 if d is None else d * tk
            ks = pl.multiple_of(j * tk, tk)
            kb, vv = k_ref[pl.ds(ks, tk), :], v_ref[pl.ds(ks, tk), :]
            qs, dos = q[r0:], dob[r0:]
            _, lb, lk = _sb_logits(qs, kb, scale)
            if d is not None:
                mask = lax.broadcasted_iota(jnp.int32, lk.shape, 1) < lax.broadcasted_iota(jnp.int32, lk.shape, 0)
                lk = jnp.where(mask, lk, 0.0)
            pin = _dot_mask(lk, upto, 2)
            w = jnp.exp(lb + (total[r0:] - pre[r0:] - pin))
            if d is not None:
                w = jnp.where(mask, w, 0.0)
            dw = _dot(dos, vv, "nt")
            dv_ref[pl.ds(ks, tk), :] += _dot(w.astype(bf16), dos, "tn")
            gg = dw * w
            gex = _dot(gg.astype(bf16), before)
            beta = jnp.exp(lb)
            dz = (gg * (1.0 - beta) - (gpre[r0:] + gex) * beta) * scale
            if d is not None:
                dz = jnp.where(mask, dz, 0.0)
            dzb = dz.astype(bf16)
            dk_ref[pl.ds(ks, tk), :] += _dot(dzb, qs, "tn")
            return (_tail_update(dq, r0, dq[r0:] + _dot(dzb, kb)),
                    _tail_update(pre, r0, pre[r0:] + pin[:, tk - 1:tk]),
                    _tail_update(gpre, r0, gpre[r0:] + gex[:, tk - 1:tk] + gg[:, tk - 1:tk]))

        init = (jnp.zeros((tq, dh), f32), jnp.zeros((tq, 1), f32), jnp.zeros((tq, 1), f32))
        carry = lax.fori_loop(0, qi * nd, lambda t, c: step(t, c, None), init)
        for d in range(nd):
            carry = step(qi * nd + d, carry, d)
        dq_ref[...] = carry[0]
        if comm is not None:
            @pl.when(last)
            def _():
                comm["finish"](*cargs)

    qblk = pl.BlockSpec((tq, dh), lambda h, i: (i, h))
    full = pl.BlockSpec((T, dh), lambda h, i: (0, h))
    (dq, dk, dv), got = _host_call(
        body, name, grid, [qblk, full, full, pl.BlockSpec((tq, dh), lambda h, i: (i, h + ob)), qblk], [qblk, full, full],
        [jax.ShapeDtypeStruct((T, W), f32)] * 3, [], [qn, kn, vb, do, ctot], comm)
    return dq, dk, dv, got


def _pool_select(sums, g):
    return jnp.where(g == 0, sums[0], jnp.where(g == 1, sums[1], jnp.where(g == 2, sums[2], sums[3])))


def _pool_count(g, r, rc, n, cols, off=0):
    t = (r * rc + off + lax.broadcasted_iota(jnp.int32, (n, cols), 0)).astype(f32)
    win = jnp.left_shift(2, g).astype(f32)
    return jnp.minimum(t + 1.0, win)


def _pool_fwd(hp, xres, w, b, scale, name, rc=512):
    T, D = hp.shape
    rc = min(rc, T)
    pg = D // len(POOL_WINDOWS)

    def body(h_ref, p_ref, x_ref, w_ref, b_ref, s_ref, o_ref, yp_ref, d_ref):
        g, r = pl.program_id(0), pl.program_id(1)
        cur = h_ref[...]
        halo = jnp.where(r > 0, p_ref[...], 0.0)
        ext = jnp.concatenate([halo, cur], axis=0)
        sums, s = [], ext
        for sh in (1, 2, 4, 8):
            s = s + pltpu.roll(s, sh, 0)
            sums.append(s)
        d = _pool_select(sums, g)[POOL_HALO:] / _pool_count(g, r, rc, rc, pg) - cur
        yp = _dot(d.astype(bf16), w_ref[0]) + b_ref[...]
        yp_ref[...] = yp
        d_ref[...] = d.astype(d_ref.dtype)
        o_ref[...] = x_ref[...] + yp * s_ref[...]

    cur = pl.BlockSpec((rc, pg), lambda g, r: (r, g))
    prev = pl.BlockSpec((POOL_HALO, pg), lambda g, r: (jnp.maximum(r * (rc // POOL_HALO) - 1, 0), g))
    vec = pl.BlockSpec((1, pg), lambda g, r: (0, g))
    return pl.pallas_call(
        body, name=name, grid=(len(POOL_WINDOWS), T // rc),
        in_specs=[cur, prev, cur, pl.BlockSpec((1, pg, pg), lambda g, r: (g, 0, 0)), vec, vec],
        out_specs=[cur, cur, cur],
        out_shape=[jax.ShapeDtypeStruct((T, D), f32), jax.ShapeDtypeStruct((T, D), f32), jax.ShapeDtypeStruct((T, D), bf16)],
        compiler_params=_cp("parallel", "parallel"),
    )(hp, hp, xres, w, b, scale)


def _pool_bwd(dx, yp, d, w, scale, name, rc=512):
    T, D = dx.shape
    rc = min(rc, T)
    pg = D // len(POOL_WINDOWS)
    nr = T // rc

    def body(dx_ref, dn_ref, yp_ref, d_ref, w_ref, s_ref, dh_ref, dw_ref, db_ref, dsc_ref):
        g, r = pl.program_id(0), pl.program_id(1)

        @pl.when(r == 0)
        def _():
            dw_ref[...] = jnp.zeros_like(dw_ref)
            db_ref[...] = jnp.zeros_like(db_ref)
            dsc_ref[...] = jnp.zeros_like(dsc_ref)

        dxv = dx_ref[...]
        dyp = dxv * s_ref[...]
        dsc_ref[...] += jnp.sum(dxv * yp_ref[...], axis=0, keepdims=True)
        db_ref[...] += jnp.sum(dyp, axis=0, keepdims=True)
        dypb = dyp.astype(bf16)
        dw_ref[0] += _dot(d_ref[...], dypb, "tn")
        dd = _dot(dypb, w_ref[0], "nt")
        ddn = _dot((dn_ref[...] * s_ref[...]).astype(bf16), w_ref[0], "nt")
        e = dd / _pool_count(g, r, rc, rc, pg)
        en = jnp.where(r < nr - 1, ddn / _pool_count(g, r, rc, POOL_HALO, pg, off=rc), 0.0)
        ext = jnp.concatenate([e, en], axis=0)
        sums, s = [], ext
        for sh in (1, 2, 4, 8):
            s = s + pltpu.roll(s, rc + POOL_HALO - sh, 0)
            sums.append(s)
        dh_ref[...] = _pool_select(sums, g)[:rc] - dd

    cur = pl.BlockSpec((rc, pg), lambda g, r: (r, g))
    nxt = pl.BlockSpec((POOL_HALO, pg), lambda g, r: (jnp.minimum((r + 1) * (rc // POOL_HALO), T // POOL_HALO - 1), g))
    vec = pl.BlockSpec((1, pg), lambda g, r: (0, g))
    wsp = pl.BlockSpec((1, pg, pg), lambda g, r: (g, 0, 0))
    return pl.pallas_call(
        body, name=name, grid=(len(POOL_WINDOWS), nr), in_specs=[cur, nxt, cur, cur, wsp, vec],
        out_specs=[cur, wsp, vec, vec],
        out_shape=[jax.ShapeDtypeStruct((T, D), f32), jax.ShapeDtypeStruct(w.shape, f32),
                   jax.ShapeDtypeStruct((1, D), f32), jax.ShapeDtypeStruct((1, D), f32)],
        compiler_params=_cp("parallel", "arbitrary"),
    )(dx, dx, yp, d, w, scale)


def _adamw(w, g, m, v, name, tr=256):
    R, C = w.shape
    tr = min(tr, R)
    assert R % tr == 0

    def body(w_ref, g_ref, m_ref, v_ref, d_ref, mo_ref, vo_ref):
        gv = g_ref[...]
        mn = ADAM_B1 * m_ref[...] + (1.0 - ADAM_B1) * gv
        vn = ADAM_B2 * v_ref[...] + (1.0 - ADAM_B2) * (gv * gv)
        m_hat = mn / (1.0 - ADAM_B1 ** ADAM_STEP)
        v_hat = vn / (1.0 - ADAM_B2 ** ADAM_STEP)
        d_ref[...] = -ADAM_LR * (m_hat / (jnp.sqrt(v_hat) + ADAM_EPS) + ADAM_WD * w_ref[...])
        mo_ref[...] = mn
        vo_ref[...] = vn

    blk = pl.BlockSpec((tr, C), lambda r: (r, 0))
    return pl.pallas_call(
        body, name=name, grid=(R // tr,), in_specs=[blk] * 4, out_specs=[blk] * 3,
        out_shape=[jax.ShapeDtypeStruct((R, C), f32)] * 3, compiler_params=_cp("parallel"),
    )(w, g, m, v)


def _pair_sum(g4, recv, name, br=256):
    _, R, C = g4.shape
    hr = R // 2
    br = min(br, hr)
    nb = hr // br

    def body(a_ref, b_ref, o_ref, ob_ref):
        s = a_ref[...] + b_ref[...]
        o_ref[...] = s
        ob_ref[...] = s.astype(ob_ref.dtype)

    out = pl.BlockSpec((1, br, C), lambda s, i: (s, i, 0))
    return pl.pallas_call(
        body, name=name, grid=(N_CHIPS, nb),
        in_specs=[pl.BlockSpec((1, br, C), lambda s, i: (s, lax.axis_index("c") * nb + i, 0)), out],
        out_specs=[out, out],
        out_shape=[jax.ShapeDtypeStruct((N_CHIPS, hr, C), f32), jax.ShapeDtypeStruct((N_CHIPS, hr, C), bf16)],
        compiler_params=_cp("parallel", "parallel"),
    )(g4, recv)


def _chip_sum(own4, recv4, name, br=256):
    _, hr, C = own4.shape
    br = min(br, hr)
    nb = hr // br
    chip = lambda: 2 * lax.axis_index("x") + lax.axis_index("y")

    def body(a_ref, b1_ref, b2_ref, b3_ref, o_ref):
        o_ref[...] = ((a_ref[0] + b1_ref[0].astype(f32)) + b2_ref[0].astype(f32)) + b3_ref[0].astype(f32)

    other = lambda k: pl.BlockSpec((1, br, C), lambda i: ((chip() + k) % N_CHIPS, i, 0))
    return pl.pallas_call(
        body, name=name, grid=(nb,),
        in_specs=[pl.BlockSpec((1, br, C), lambda i: (chip(), i, 0)), other(1), other(2), other(3)],
        out_specs=pl.BlockSpec((br, C), lambda i: (lax.axis_index("c") * nb + i, 0)),
        out_shape=jax.ShapeDtypeStruct((2 * hr, C), f32), compiler_params=_cp("parallel"),
    )(own4, recv4, recv4, recv4)


ANY = pl.BlockSpec(memory_space=pl.ANY)


def _mesh_pos():
    x, y, c = lax.axis_index("x"), lax.axis_index("y"), lax.axis_index("c")
    others = [(1 - x, y), (x, 1 - y), (1 - x, 1 - y)]
    return x, y, c, 2 * x + y, others


def _gather_small(blk, name):
    m, n = blk.shape

    def body(x_ref, out_ref, sum_ref, send_sems, recv_sems, local_sem):
        x, y, c, _, others = _mesh_pos()
        me, sibling = (x, y, c), (x, y, 1 - c)

        def rows(px, py, pc):
            return out_ref.at[pl.ds((4 * px + 2 * py + pc) * m, m), :]

        def copy(k, block, to, src=None):
            return pltpu.make_async_remote_copy(
                src_ref=rows(*block) if src is None else src, dst_ref=rows(*block),
                send_sem=send_sems.at[k], recv_sem=recv_sems.at[k], device_id=to, device_id_type=MESH)

        mine = pltpu.make_async_copy(x_ref, rows(*me), local_sem)
        mine.start()
        first = [copy(0, me, sibling, src=x_ref)]
        first += [copy(1 + j, me, (*chip, c), src=x_ref) for j, chip in enumerate(others)]
        for cp in first:
            cp.start()
        passed = [copy(4 + j, (*chip, c), sibling) for j, chip in enumerate(others)]
        for j, chip in enumerate(others):
            copy(1 + j, (*chip, c), me).wait_recv()
            passed[j].start()
        copy(0, sibling, me).wait_recv()
        for j, chip in enumerate(others):
            copy(4 + j, (*chip, 1 - c), me).wait_recv()
        for cp in first + passed:
            cp.wait_send()
        mine.wait()
        acc = out_ref[0:m, :]
        for d in range(1, 8):
            acc = acc + out_ref[d * m:(d + 1) * m, :]
        sum_ref[...] = acc

    vm = pl.BlockSpec(memory_space=pltpu.VMEM)
    return pl.pallas_call(
        body, name=name, in_specs=[vm], out_specs=[vm, vm],
        out_shape=[jax.ShapeDtypeStruct((8 * m, n), f32), jax.ShapeDtypeStruct((m, n), f32)],
        scratch_shapes=[pltpu.SemaphoreType.DMA((7,)), pltpu.SemaphoreType.DMA((7,)), pltpu.SemaphoreType.DMA],
    )(blk)


def _copy(src, dst, sems, idx, to):
    return pltpu.make_async_remote_copy(src_ref=src, dst_ref=dst, send_sem=sems[0].at[idx], recv_sem=sems[1].at[idx],
                                        device_id=to, device_id_type=MESH)


def _gather_ici(shards):
    nt = len(shards)

    def copies(ins, outs, sems):
        x, y, c, chip, others = _mesh_pos()
        send, land = [], []
        for t in range(nt):
            hr = ins[t].shape[0] // 2
            for j, (px, py) in enumerate(others):
                send.append((ins[t].at[pl.ds(c * hr, hr)], outs[t].at[chip, pl.ds(c * hr, hr)], sems, (t, j), (px, py, c)))
                piece = outs[t].at[2 * px + py, pl.ds(c * hr, hr)]
                land.append((piece, piece, sems, (t, j), (px, py, c)))
        return send, land

    def start(ins, outs, sems):
        for args in copies(ins, outs, sems)[0]:
            _copy(*args).start()

    def finish(ins, outs, sems):
        send, land = copies(ins, outs, sems)
        for args in land:
            _copy(*args).wait_recv()
        for args in send:
            _copy(*args).wait_send()

    return dict(ins=list(shards), outs=[jax.ShapeDtypeStruct((N_CHIPS,) + s.shape, s.dtype) for s in shards],
                sems=[pltpu.SemaphoreType.DMA((nt, 3)), pltpu.SemaphoreType.DMA((nt, 3))], start=start, finish=finish)


def _gather_d2d(stacks):
    nt = len(stacks)

    def copies(ins, outs, sems):
        x, y, c, _, others = _mesh_pos()
        send, land = [], []
        for t in range(nt):
            hr = outs[t].shape[1] // 2
            for j, (px, py) in enumerate(others):
                mine = outs[t].at[2 * px + py, pl.ds(c * hr, hr)]
                theirs = outs[t].at[2 * px + py, pl.ds((1 - c) * hr, hr)]
                send.append((mine, mine, sems, (t, j), (x, y, 1 - c)))
                land.append((theirs, theirs, sems, (t, j), (x, y, 1 - c)))
        return send, land

    def start(ins, outs, sems):
        for args in copies(ins, outs, sems)[0]:
            _copy(*args).start()

    def finish(ins, outs, sems):
        send, land = copies(ins, outs, sems)
        for args in land:
            _copy(*args).wait_recv()
        for args in send:
            _copy(*args).wait_send()

    return dict(ins=list(stacks), outs=[jax.ShapeDtypeStruct(s.shape, s.dtype) for s in stacks],
                sems=[pltpu.SemaphoreType.DMA((nt, 3)), pltpu.SemaphoreType.DMA((nt, 3))], start=start, finish=finish,
                aliases={t: t for t in range(nt)})


def _run_exchange(comm, name):
    ni, no = len(comm["ins"]), len(comm["outs"])

    def body(*refs):
        args = (refs[:ni], refs[ni:ni + no], refs[ni + no:])
        comm["start"](*args)
        comm["finish"](*args)

    return pl.pallas_call(
        body, name=name, in_specs=[ANY] * ni, out_specs=[ANY] * no, out_shape=comm["outs"], scratch_shapes=comm["sems"],
        input_output_aliases=dict(comm.get("aliases", {})))(*comm["ins"])


def _swap_halves(g4s, name):
    nt = len(g4s)

    def body(*refs):
        ins, outs = refs[:nt], refs[nt:2 * nt]
        send_sems, recv_sems = refs[2 * nt:]
        x, y, c, _, _ = _mesh_pos()
        cps = []
        for t in range(nt):
            hr = ins[t].shape[1] // 2
            cp = pltpu.make_async_remote_copy(
                src_ref=ins[t].at[:, pl.ds((1 - c) * hr, hr)], dst_ref=outs[t], send_sem=send_sems.at[t],
                recv_sem=recv_sems.at[t], device_id=(x, y, 1 - c), device_id_type=MESH)
            cp.start()
            cps.append(cp)
        for cp in cps:
            cp.wait()

    return pl.pallas_call(
        body, name=name, in_specs=[ANY] * nt, out_specs=[ANY] * nt,
        out_shape=[jax.ShapeDtypeStruct((N_CHIPS, g.shape[1] // 2, g.shape[2]), g.dtype) for g in g4s],
        scratch_shapes=[pltpu.SemaphoreType.DMA((nt,)), pltpu.SemaphoreType.DMA((nt,))],
    )(*g4s)


def _exchange_chips(h4s):
    nt = len(h4s)

    def copies(ins, outs, sems):
        x, y, c, chip, others = _mesh_pos()
        send, land = [], []
        for t in range(nt):
            for j, (px, py) in enumerate(others):
                send.append((ins[t].at[2 * px + py], outs[t].at[chip], sems, (t, j), (px, py, c)))
                landed = outs[t].at[2 * px + py]
                land.append((landed, landed, sems, (t, j), (px, py, c)))
        return send, land

    def start(ins, outs, sems):
        for args in copies(ins, outs, sems)[0]:
            _copy(*args).start()

    def finish(ins, outs, sems):
        send, land = copies(ins, outs, sems)
        for args in land:
            _copy(*args).wait_recv()
        for args in send:
            _copy(*args).wait_send()

    return dict(ins=list(h4s), outs=[jax.ShapeDtypeStruct(h.shape, h.dtype) for h in h4s],
                sems=[pltpu.SemaphoreType.DMA((nt, 3)), pltpu.SemaphoreType.DMA((nt, 3))], start=start, finish=finish)


def _join_halves(fs):
    nt = len(fs)

    def copies(ins, outs, sems):
        x, y, c, _, _ = _mesh_pos()
        send, land = [], []
        for t in range(nt):
            hr = outs[t].shape[0] // 2
            mine, theirs = outs[t].at[pl.ds(c * hr, hr)], outs[t].at[pl.ds((1 - c) * hr, hr)]
            send.append((mine, mine, sems, t, (x, y, 1 - c)))
            land.append((theirs, theirs, sems, t, (x, y, 1 - c)))
        return send, land

    def start(ins, outs, sems):
        for args in copies(ins, outs, sems)[0]:
            _copy(*args).start()

    def finish(ins, outs, sems):
        send, land = copies(ins, outs, sems)
        for args in land:
            _copy(*args).wait_recv()
        for args in send:
            _copy(*args).wait_send()

    return dict(ins=list(fs), outs=[jax.ShapeDtypeStruct(f.shape, f.dtype) for f in fs],
                sems=[pltpu.SemaphoreType.DMA((nt,)), pltpu.SemaphoreType.DMA((nt,))], start=start, finish=finish,
                aliases={t: t for t in range(nt)})


def _pad_lanes(v, n=LANES):
    return jnp.pad(v, ((0, 0), (0, n - v.shape[-1])))


def _mlp_fwd(xin, norm_g, w_up, w_down, F, tag):
    T, D = xin.shape
    h = _rms_fwd(xin, norm_g, bf16, f"{tag}_norm")

    def relu_sq(acc):
        r = jnp.maximum(acc, 0.0)
        return r, r * r

    u, a = _mm(h, w_up[0], "nn", T, F, D, (bf16, bf16), f"{tag}_up", epilogue=relu_sq, b_view=w_up[1])
    out = _mm(a, w_down[0], "nn", T, D, F, (f32,), f"{tag}_down", epilogue=lambda acc, res: (res + acc,),
              extras=((xin, "tile"),), b_view=w_down[1])
    return out, (xin, h, u, a)


def _mlp_bwd(dy, saved, norm_g, w_up, w_down, F, tag, up_to=None, down_to=None):
    xin, h, u, a = saved
    T, D = xin.shape
    to = lambda t: {} if t is None else dict(out_view=t[0], out_stack=t[1], alias=t[2])
    du = _mm(dy, w_down[0], "nt", T, F, D, (bf16,), f"{tag}_dact", epilogue=lambda acc, uu: (acc * (2.0 * uu.astype(f32)),),
             extras=((u, "tile"),), b_view=w_down[1])
    dw_down = _mm(a, dy, "tn", F, D, T, (f32,), f"{tag}_dwdown", **to(down_to))
    dh = _mm(du, w_up[0], "nt", T, D, F, (f32,), f"{tag}_dh", b_view=w_up[1])
    dw_up = _mm(h, du, "tn", D, F, T, (f32,), f"{tag}_dwup", **to(up_to))
    dx, dg = _rms_bwd(xin, norm_g, dh, dy, f"{tag}_dnorm")
    return dx, dg, dw_up, dw_down


def _local_step(xc, tgt, W, HS, SBW, net=None):
    T, D = xc.shape
    SW = HS * SSD_HEAD_DIM
    CD = W["conv_b"].shape[-1]
    mlp_norm = W["mlp_norm"]
    add = lambda acc, prev: (prev + acc,)

    h0 = _rms_fwd(xc, W["hyb_norm"], bf16, "hyb_norm")
    z = _mm(h0, W["w_z"], "nn", T, SW, D, (f32,), "proj_z")
    xraw = _mm(h0, W["w_xbc"], "nn", T, CD, D, (f32,), "proj_xbc")
    dtraw = _mm(h0, W["w_dt"], "nn", T, LANES, D, (f32,), "proj_dt")
    qkv = _mm(h0, W["w_qkv"], "nn", T, 3 * SBW, D, (f32,), "proj_qkv")
    qn, kn, vb = _qk_norm_fwd(qkv, W["q_norm"], W["k_norm"], SBW, "qk_norm")
    y_sb, ctot, got = _sb_fwd(qn, kn, vb, "sb_attn", comm=net.rest_ici() if net else None)
    xbc = _conv_fwd(xraw, W["conv_w"], W["conv_b"], "conv")
    y_ssd, yn_ssd, sprev, got = _ssd_fwd(xbc, dtraw, z, W["dt_bias"], W["a_log"], W["d_skip"], W["out_norm"], HS, "ssd",
                                         comm=net.rest_d2d(got) if net else None)
    if net:
        W = {**W, **net.rest_weights(got)}
    w_up, w_down, F = W["w_up"], W["w_down"], W["F"]
    mix = _mm(yn_ssd, W["w_out"], "nn", T, D, SW, (f32,), "out_ssd", epilogue=add, extras=((xc, "tile"),))
    x1 = _mm(y_sb, W["w_out"], "nn", T, D, SBW, (f32,), "out_sb", epilogue=add, extras=((mix, "tile"),), b_off=(SW, 0))
    x2, mlp0 = _mlp_fwd(x1, mlp_norm[0:1], w_up[0], w_down[0], F, "mlp0")

    hp = _rms_fwd(x2, W["pool_norm"], f32, "pool_norm")
    x3, yp, dpool = _pool_fwd(hp, x2, W["w_pool"], W["pool_b"], W["pool_scale"], "pool")
    x4, mlp1 = _mlp_fwd(x3, mlp_norm[1:2], w_up[1], w_down[1], F, "mlp1")

    dy, sq = _loss_grad(x4, tgt, "loss")

    up_to, down_to = (net.mlp_to("up", 1, None), net.mlp_to("down", 1, None)) if net else (None, None)
    dx3, dg_mlp1, dw_up1, dw_down1 = _mlp_bwd(dy, mlp1, mlp_norm[1:2], w_up[1], w_down[1], F, "mlp1", up_to, down_to)
    dhp, dw_pool, db_pool, dsc_pool = _pool_bwd(dx3, yp, dpool, W["w_pool"], W["pool_scale"], "pool_bwd")
    dx2, dg_pool = _rms_bwd(x2, W["pool_norm"], dhp, dx3, "pool_dnorm")
    up_to, down_to = (net.mlp_to("up", 0, dw_up1), net.mlp_to("down", 0, dw_down1)) if net else (None, None)
    dx1, dg_mlp0, dw_up0, dw_down0 = _mlp_bwd(dx2, mlp0, mlp_norm[0:1], w_up[0], w_down[0], F, "mlp0", up_to, down_to)

    dmerged = _mm(dx1, W["w_out"], "nt", T, SW + SBW, D, (f32,), "dmerged")
    dw_out = jnp.concatenate([_mm(yn_ssd, dx1, "tn", SW, D, T, (f32,), "dwout_ssd"),
                              _mm(y_sb, dx1, "tn", SBW, D, T, (f32,), "dwout_sb")], axis=0)
    dqn, dkn, dvv, got = _sb_bwd(qn, kn, vb, dmerged, ctot, SW, "sb_attn_bwd",
                                 comm=net.reduce_early(dw_out, dw_pool, dw_up0, dw_down0) if net else None)
    if net:
        net.reduce_early_done(got)
    dqkv, dg_q, dg_k = _qk_norm_bwd(qkv, dqn, dkn, dvv, W["q_norm"], W["k_norm"], SBW, "qk_norm_bwd")
    dy_ssd, dz, dg_on = _gate_bwd(y_ssd, z, dmerged, W["out_norm"], "gate_bwd")
    dxbc, ddtraw, dalog, dbias, ddskip = _ssd_bwd(xbc, dtraw, sprev, dy_ssd, W["dt_bias"], W["a_log"], W["d_skip"], HS, "ssd_bwd")
    dpre, dconv_w, dconv_b = _conv_bwd_pre(xraw, dxbc, W["conv_w"], W["conv_b"], "conv_bwd_pre")
    dxraw = _conv_bwd_in(dpre, W["conv_w"], "conv_bwd_in")
    dw_in = jnp.concatenate([
        _mm(h0, dz, "tn", D, SW, T, (f32,), "dwin_z"), _mm(h0, dxraw, "tn", D, CD, T, (f32,), "dwin_xbc"),
        _mm(h0, ddtraw, "tn", D, LANES, T, (f32,), "dwin_dt")[:, :HS], _mm(h0, dqkv, "tn", D, 3 * SBW, T, (f32,), "dwin_qkv")], axis=1)
    dh0 = _mm(dz, W["w_z"], "nt", T, D, SW, (f32,), "dh0_z")
    dh0 = _mm(dxraw, W["w_xbc"], "nt", T, D, CD, (f32,), "dh0_xbc", epilogue=add, extras=((dh0, "tile"),))
    dh0 = _mm(ddtraw, W["w_dt"], "nt", T, D, LANES, (f32,), "dh0_dt", epilogue=add, extras=((dh0, "tile"),))
    if net:
        (dh0,), got = _mm(dqkv, W["w_qkv"], "nt", T, D, 3 * SBW, (f32,), "dh0_qkv", epilogue=add, extras=((dh0, "tile"),),
                          comm=net.reduce_late(dw_in))
        net.reduce_late_done(got)
    else:
        dh0 = _mm(dqkv, W["w_qkv"], "nt", T, D, 3 * SBW, (f32,), "dh0_qkv", epilogue=add, extras=((dh0, "tile"),))
    grad_x, dg_hyb = _rms_bwd(xc, W["hyb_norm"], dh0, dx1, "hyb_dnorm")
    grads = dict(w_in=dw_in, w_out=dw_out, w_pool=dw_pool, w_up=(dw_up0, dw_up1), w_down=(dw_down0, dw_down1),
                 hyb_norm=dg_hyb, conv_w=dconv_w, conv_b=dconv_b, dt_bias=dbias, a_log=dalog, d_skip=ddskip, out_norm=dg_on,
                 q_norm=dg_q, k_norm=dg_k, mlp_norm=(dg_mlp0, dg_mlp1), pool_norm=dg_pool, pool_b=db_pool, pool_scale=dsc_pool)
    return sq, grad_x, grads


class _Net:
    def __init__(self, own_rest, chip, dims):
        self.own, self.chip, self.dims = own_rest, chip, dims

    def rest_ici(self):
        return _gather_ici(self.own)

    def rest_d2d(self, got):
        return _gather_d2d(list(got))

    def rest_weights(self, got):
        d, nw = self.dims, len(POOL_WINDOWS)
        D, F, NL, PG = d["D"], d["F"], d["NL"], d["PG"]
        fs = F // N_CHIPS
        g_out, g_pool, g_up, g_down = (lax.dynamic_update_index_in_dim(g, o, self.chip, 0) for g, o in zip(got, self.own))
        w_pool = g_pool.reshape(N_CHIPS, nw, PG // N_CHIPS, PG).transpose(1, 0, 2, 3).reshape(nw, PG, PG)
        return dict(w_out=g_out.reshape(d["MIX"], D), w_pool=w_pool, F=F,
                    w_up=[(g_up, ("cols", fs, l, D)) for l in range(NL)],
                    w_down=[(g_down, ("rows", fs, l, None)) for l in range(NL)])

    def mlp_to(self, which, layer, earlier):
        d = self.dims
        fs = d["F"] // N_CHIPS
        if which == "up":
            return ("cols", fs, layer, d["D"]), (N_CHIPS, d["NL"] * d["D"], fs), earlier
        return ("rows", fs, layer, None), (N_CHIPS, d["NL"] * fs, d["D"]), earlier

    def _pairs(self, g4s, tag):
        recv = _swap_halves(g4s, f"{tag}_pair_swap")
        sums = [_pair_sum(g, r, f"{tag}_pair_sum{i}") for i, (g, r) in enumerate(zip(g4s, recv))]
        return [s[0] for s in sums], [s[1] for s in sums]

    def reduce_early(self, dw_out, dw_pool, g_up, g_down):
        d, nw = self.dims, len(POOL_WINDOWS)
        PG = d["PG"]
        g4 = [dw_out.reshape(N_CHIPS, d["MIX"] // N_CHIPS, d["D"]),
              dw_pool.reshape(nw, N_CHIPS, PG // N_CHIPS, PG).transpose(1, 0, 2, 3).reshape(N_CHIPS, PG, PG), g_up, g_down]
        self.early_own, sent = self._pairs(g4, "grads_early")
        return _exchange_chips(sent)

    def reduce_early_done(self, got):
        self.early_got = list(got)

    def reduce_late(self, dw_in):
        d = self.dims
        g4 = [dw_in.reshape(d["D"], N_CHIPS, d["IN"] // N_CHIPS).transpose(1, 0, 2)]
        self.late_own, sent = self._pairs(g4, "grads_late")
        return _exchange_chips(sent)

    def reduce_late_done(self, got):
        self.late_got = list(got)

    def reduced(self):
        halves = [_chip_sum(h, r, f"grads_chip_sum{i}")
                  for i, (h, r) in enumerate(zip(self.late_own + self.early_own, self.late_got + self.early_got))]
        return _run_exchange(_join_halves(halves), "grads_join")


def kernel(x, hyb_norm, hyb_w_in, ssd_conv_w, ssd_conv_b, ssd_dt_bias, ssd_a_log, ssd_d, ssd_out_norm, sb_q_norm, sb_k_norm, hyb_w_out, pool_norm, pool_w, pool_b, pool_scale, mlp_norm, mlp_w_up, mlp_w_down, loss_target, m_hyb_norm, m_hyb_w_in, m_ssd_conv_w, m_ssd_conv_b, m_ssd_dt_bias, m_ssd_a_log, m_ssd_d, m_ssd_out_norm, m_sb_q_norm, m_sb_k_norm, m_hyb_w_out, m_pool_norm, m_pool_w, m_pool_b, m_pool_scale, m_mlp_norm, m_mlp_w_up, m_mlp_w_down, v_hyb_norm, v_hyb_w_in, v_ssd_conv_w, v_ssd_conv_b, v_ssd_dt_bias, v_ssd_a_log, v_ssd_d, v_ssd_out_norm, v_sb_q_norm, v_sb_k_norm, v_hyb_w_out, v_pool_norm, v_pool_w, v_pool_b, v_pool_scale, v_mlp_norm, v_mlp_w_up, v_mlp_w_down):
    T, D = x.shape[1], x.shape[2]
    HS = ssd_dt_bias.shape[-1]
    SW = HS * SSD_HEAD_DIM
    CD = ssd_conv_b.shape[-1]
    IN = N_CHIPS * hyb_w_in.shape[-1]
    SBW = (IN - SW - CD - HS) // 3
    F = N_CHIPS * mlp_w_up.shape[-1]
    NL = mlp_norm.shape[0]
    PG = D // len(POOL_WINDOWS)
    xc, tgt = x[0], loss_target[0]
    ix, iy, ic = lax.axis_index("x"), lax.axis_index("y"), lax.axis_index("c")
    chip = (2 * ix + iy).astype(jnp.int32)

    small = jnp.concatenate([ssd_conv_w.reshape(-1), pool_norm.reshape(-1), pool_b.reshape(-1), pool_scale.reshape(-1)])
    ns = small.shape[0]
    ns8 = -(-ns // (8 * LANES)) * LANES
    gathered, _ = _gather_small(jnp.pad(small, (0, 8 * ns8 - ns)).reshape(8, ns8), "gather_small")
    per_chip = gathered.reshape(N_CHIPS, 2, 8 * ns8)[:, 0, :ns]
    cw = CD // N_CHIPS
    conv_w = per_chip[:, :4 * cw].reshape(N_CHIPS, 4, cw).transpose(1, 0, 2).reshape(4, CD)
    pvec = per_chip[:, 4 * cw:].reshape(N_CHIPS, 3, PG)
    pool_norm_f, pool_b_f, pool_scale_f = (pvec[:, i].reshape(1, D) for i in range(3))

    fs = F // N_CHIPS
    own_in = hyb_w_in[0].astype(bf16)
    g_in = _run_exchange(_gather_d2d(_run_exchange(_gather_ici([own_in]), "gather_in_ici")), "gather_in_d2d")[0]
    w_in = lax.dynamic_update_index_in_dim(g_in, own_in, chip, 0).transpose(1, 0, 2).reshape(D, IN)
    c1, c2, c3 = SW, SW + CD, SW + CD + HS
    w_z, w_xbc, w_dt, w_qkv = w_in[:, :c1], w_in[:, c1:c2], _pad_lanes(w_in[:, c2:c3]), w_in[:, c3:]
    dt_bias_p, a_log_p, d_skip_p = _pad_lanes(ssd_dt_bias), _pad_lanes(ssd_a_log), jnp.repeat(ssd_d, SSD_HEAD_DIM, axis=-1)

    own_rest = [hyb_w_out[0].astype(bf16), pool_w[0].reshape(-1, PG).astype(bf16),
                mlp_w_up.reshape(-1, fs).astype(bf16), mlp_w_down.reshape(-1, D).astype(bf16)]
    net = _Net(own_rest, chip, dict(D=D, F=F, NL=NL, PG=PG, IN=IN, MIX=SW + SBW))
    first = dict(hyb_norm=hyb_norm, w_z=w_z, w_xbc=w_xbc, w_dt=w_dt, w_qkv=w_qkv, conv_w=conv_w, conv_b=ssd_conv_b,
                 dt_bias=dt_bias_p, a_log=a_log_p, d_skip=d_skip_p, out_norm=ssd_out_norm, q_norm=sb_q_norm, k_norm=sb_k_norm,
                 pool_norm=pool_norm_f, pool_b=pool_b_f, pool_scale=pool_scale_f, mlp_norm=mlp_norm)
    sq, grad_x, gr = _local_step(xc, tgt, first, HS, SBW, net)
    loss = lax.psum(sq[0, 0] * (0.5 / D), ("x", "y", "c"))
    dg_hyb, dconv_b, dbias, dalog, ddskip, dg_on, dg_q, dg_k = (gr[k] for k in (
        "hyb_norm", "conv_b", "dt_bias", "a_log", "d_skip", "out_norm", "q_norm", "k_norm"))
    (dg_mlp0, dg_mlp1), dconv_w, dg_pool, db_pool, dsc_pool = gr["mlp_norm"], gr["conv_w"], gr["pool_norm"], gr["pool_b"], gr["pool_scale"]
    gb_in, gb_out, gb_pool, gb_up, gb_down = net.reduced()

    full_small = [dg_hyb, dconv_b, dbias[:, :HS], dalog[:, :HS], ddskip[:, :HS], dg_on, dg_q, dg_k,
                  jnp.concatenate([dg_mlp0, dg_mlp1], axis=0).reshape(1, -1),
                  dconv_w.reshape(1, -1), dg_pool, db_pool, dsc_pool]
    sizes = [v.shape[-1] for v in full_small]
    packed = jnp.concatenate([v.reshape(-1) for v in full_small])
    npk = packed.shape[0]
    npk8 = -(-npk // (8 * LANES)) * LANES
    _, summed = _gather_small(jnp.pad(packed, (0, 8 * npk8 - npk)).reshape(8, npk8), "grads_small")
    summed = summed.reshape(-1)[:npk]
    offs = [0]
    for s in sizes:
        offs.append(offs[-1] + s)
    (g_hyb_norm, g_conv_b, g_dt_bias, g_a_log, g_d, g_out_norm, g_q_norm, g_k_norm, g_mlp_norm, g_conv_w_full,
     g_pool_norm_full, g_pool_b_full, g_pool_scale_full) = (summed[offs[i]:offs[i + 1]] for i in range(len(sizes)))
    take = lambda full, n: lax.dynamic_slice_in_dim(full.reshape(-1, N_CHIPS, n), chip, 1, axis=1)
    small_grads = {
        "hyb_norm": g_hyb_norm.reshape(hyb_norm.shape), "ssd_conv_w": take(g_conv_w_full, cw).reshape(ssd_conv_w.shape),
        "ssd_conv_b": g_conv_b.reshape(ssd_conv_b.shape), "ssd_dt_bias": g_dt_bias.reshape(ssd_dt_bias.shape),
        "ssd_a_log": g_a_log.reshape(ssd_a_log.shape), "ssd_d": g_d.reshape(ssd_d.shape),
        "ssd_out_norm": g_out_norm.reshape(ssd_out_norm.shape), "sb_q_norm": g_q_norm.reshape(sb_q_norm.shape),
        "sb_k_norm": g_k_norm.reshape(sb_k_norm.shape), "pool_norm": take(g_pool_norm_full, PG).reshape(pool_norm.shape),
        "pool_b": take(g_pool_b_full, PG).reshape(pool_b.shape), "pool_scale": take(g_pool_scale_full, PG).reshape(pool_scale.shape),
        "mlp_norm": g_mlp_norm.reshape(mlp_norm.shape),
    }

    weights = dict(hyb_norm=hyb_norm, hyb_w_in=hyb_w_in, ssd_conv_w=ssd_conv_w, ssd_conv_b=ssd_conv_b, ssd_dt_bias=ssd_dt_bias,
                   ssd_a_log=ssd_a_log, ssd_d=ssd_d, ssd_out_norm=ssd_out_norm, sb_q_norm=sb_q_norm, sb_k_norm=sb_k_norm,
                   hyb_w_out=hyb_w_out, pool_norm=pool_norm, pool_w=pool_w, pool_b=pool_b, pool_scale=pool_scale,
                   mlp_norm=mlp_norm, mlp_w_up=mlp_w_up, mlp_w_down=mlp_w_down)
    moms = dict(hyb_norm=m_hyb_norm, hyb_w_in=m_hyb_w_in, ssd_conv_w=m_ssd_conv_w, ssd_conv_b=m_ssd_conv_b, ssd_dt_bias=m_ssd_dt_bias,
                ssd_a_log=m_ssd_a_log, ssd_d=m_ssd_d, ssd_out_norm=m_ssd_out_norm, sb_q_norm=m_sb_q_norm, sb_k_norm=m_sb_k_norm,
                hyb_w_out=m_hyb_w_out, pool_norm=m_pool_norm, pool_w=m_pool_w, pool_b=m_pool_b, pool_scale=m_pool_scale,
                mlp_norm=m_mlp_norm, mlp_w_up=m_mlp_w_up, mlp_w_down=m_mlp_w_down)
    vels = dict(hyb_norm=v_hyb_norm, hyb_w_in=v_hyb_w_in, ssd_conv_w=v_ssd_conv_w, ssd_conv_b=v_ssd_conv_b, ssd_dt_bias=v_ssd_dt_bias,
                ssd_a_log=v_ssd_a_log, ssd_d=v_ssd_d, ssd_out_norm=v_ssd_out_norm, sb_q_norm=v_sb_q_norm, sb_k_norm=v_sb_k_norm,
                hyb_w_out=v_hyb_w_out, pool_norm=v_pool_norm, pool_w=v_pool_w, pool_b=v_pool_b, pool_scale=v_pool_scale,
                mlp_norm=v_mlp_norm, mlp_w_up=v_mlp_w_up, mlp_w_down=v_mlp_w_down)
    order = list(weights)
    grads, delta, new_m, new_v = {}, {}, {}, {}
    for name, g2 in (("hyb_w_in", gb_in), ("hyb_w_out", gb_out), ("pool_w", gb_pool), ("mlp_w_up", gb_up), ("mlp_w_down", gb_down)):
        shp = weights[name].shape
        d_, m_, v_ = _adamw(weights[name].reshape(g2.shape), g2, moms[name].reshape(g2.shape), vels[name].reshape(g2.shape),
                            f"adamw_{name}")
        grads[name], delta[name], new_m[name], new_v[name] = (t.reshape(shp) for t in (g2, d_, m_, v_))
    snames = list(small_grads)
    pack = lambda d: jnp.concatenate([d[n].reshape(-1) for n in snames])
    nsm = sum(small_grads[n].size for n in snames)
    cols = -(-nsm // (8 * LANES)) * LANES
    as_blk = lambda v: jnp.pad(v, (0, 8 * cols - nsm)).reshape(8, cols)
    padded_v = jnp.pad(pack(vels), (0, 8 * cols - nsm), constant_values=1.0).reshape(8, cols)
    d_, m_, v_ = _adamw(as_blk(pack(weights)), as_blk(pack(small_grads)), as_blk(pack(moms)), padded_v, "adamw_small")
    off = 0
    for n in snames:
        sz, shp = small_grads[n].size, weights[n].shape
        grads[n] = small_grads[n]
        delta[n], new_m[n], new_v[n] = (t.reshape(-1)[off:off + sz].reshape(shp) for t in (d_, m_, v_))
        off += sz

    return (loss, grad_x.reshape(x.shape), *[grads[n] for n in order], *[delta[n] for n in order],
            *[new_m[n] for n in order], *[new_v[n] for n in order])
```

```python
import functools
import math

import jax
import jax.numpy as jnp
from jax import lax
from jax.experimental import pallas as pl
from jax.experimental.pallas import tpu as pltpu

f32 = jnp.float32
bf16 = jnp.bfloat16

EPS = 1e-6
SSD_HEAD_DIM = 64
SSD_STATE = 128
SSD_GROUPS = 4
SSD_CHUNK = 128
LANES = 128
SB_HEAD_DIM = 128
POOL_WINDOWS = (2, 4, 8, 16)
POOL_HALO = 16
CONV_HALO = 8
ADAM_LR, ADAM_B1, ADAM_B2, ADAM_EPS, ADAM_WD, ADAM_STEP = 0.001, 0.9, 0.999, 1e-08, 0.01, 10
VMEM_LIMIT = 56 * 1024 * 1024
MM_TILE_BUDGET = 40 * 1024 * 1024
N_CHIPS = 4
MESH = pl.DeviceIdType.MESH

_DIMS = {"nn": (((1,), (0,)), ((), ())), "nt": (((1,), (1,)), ((), ())), "tn": (((0,), (0,)), ((), ()))}


def _fit(n, t):
    if n <= t:
        return n
    return max(d for d in range(LANES, t + 1, LANES) if n % d == 0)


def _cp(*sem):
    return pltpu.CompilerParams(dimension_semantics=sem, vmem_limit_bytes=VMEM_LIMIT)


def _sigmoid(v):
    return 1.0 / (1.0 + jnp.exp(-v))


def _softplus(v):
    return jnp.maximum(v, 0.0) + jnp.log(1.0 + jnp.exp(-jnp.abs(v)))


def _split(v, parts):
    out, rem = [], v
    for _ in range(parts):
        p = rem.astype(bf16)
        out.append(p)
        rem = rem - p.astype(f32)
    return out


def _dot(a, b, mode="nn"):
    return lax.dot_general(a, b, _DIMS[mode], preferred_element_type=f32)


def _mask_dot(mask_b, v, parts):
    return _dot(jnp.concatenate([mask_b] * parts, axis=1), jnp.concatenate(_split(v, parts), axis=0))


def _dot_mask(v, mask_b, parts):
    return _dot(jnp.concatenate(_split(v, parts), axis=1), jnp.concatenate([mask_b] * parts, axis=0))


def _stacked(view, br, bc, rmap, cmap):
    kind, per, layer, rows_per_layer = view
    if kind == "cols":
        npc = per // bc
        return pl.BlockSpec((None, br, bc), lambda i, j, k: (cmap(i, j, k) // npc, layer * (rows_per_layer // br) + rmap(i, j, k),
                                                              cmap(i, j, k) % npc))
    npc = per // br
    return pl.BlockSpec((None, br, bc), lambda i, j, k: (rmap(i, j, k) // npc, layer * npc + rmap(i, j, k) % npc, cmap(i, j, k)))


def _pick_tiles(M, N, K, caps, a_bytes, b_bytes, io_bytes):
    def cands(n, cap, sizes):
        got = [s for s in sizes if s <= min(n, cap) and n % s == 0]
        return got or [_fit(n, min(n, cap))]

    best = None
    for tk in cands(K, caps[2], (8192, 4096, 2048, 1024, 512, 256, 128)):
        for tm in cands(M, caps[0], (1024, 512, 256, 128)):
            for tn in cands(N, caps[1], (1024, 512, 256, 128)):
                need = 2 * (tm * tk * a_bytes + tk * tn * b_bytes) + tm * tn * (2 * io_bytes + (4 if tk < K else 0))
                key = (need <= MM_TILE_BUDGET, tk, tm * tn, tm)
                if best is None or key > best[0]:
                    best = (key, (tm, tn, tk))
    return best[1]


def _mm(a, b, mode, M, N, K, outs, name, epilogue=None, extras=(), a_off=(0, 0), b_off=(0, 0),
        b_view=None, out_view=None, out_stack=None, alias=None, comm=None):
    caps = [M, N, K]
    if b_view is not None:
        caps[1 if (b_view[0] == "cols") == (mode != "nt") else 2] = b_view[1]
    if out_view is not None:
        d = 1 if out_view[0] == "cols" else 0
        caps[d] = min(caps[d], out_view[1])
    for off, dims in ((a_off, (2, 0) if mode == "tn" else (0, 2)), (b_off, (1, 2) if mode == "nt" else (2, 1))):
        for o, d in zip(off, dims):
            if o:
                caps[d] = min(caps[d], math.gcd(o, caps[d]))
    io_bytes = sum(jnp.dtype(dt).itemsize for dt in outs) + sum(e[0].dtype.itemsize for e in extras if e[1] == "tile")
    tm, tn, tk = _pick_tiles(M, N, K, caps, a.dtype.itemsize, b.dtype.itemsize, io_bytes)
    nk = K // tk
    if mode == "tn":
        a_blk, ad = (tk, tm), (tk, tm)
    else:
        a_blk, ad = (tm, tk), (tm, tk)
    b_blk = (tn, tk) if mode == "nt" else (tk, tn)
    assert a_off[0] % ad[0] == 0 and a_off[1] % ad[1] == 0 and b_off[0] % b_blk[0] == 0 and b_off[1] % b_blk[1] == 0
    ao = (a_off[0] // ad[0], a_off[1] // ad[1])
    bo = (b_off[0] // b_blk[0], b_off[1] // b_blk[1])
    if mode == "tn":
        a_map = lambda i, j, k: (k + ao[0], i + ao[1])
    else:
        a_map = lambda i, j, k: (i + ao[0], k + ao[1])
    if mode == "nt":
        b_map = lambda i, j, k: (j + bo[0], k + bo[1])
    else:
        b_map = lambda i, j, k: (k + bo[0], j + bo[1])
    if b_view is not None:
        if mode == "nt":
            b_spec = _stacked(b_view, tn, tk, lambda i, j, k: j, lambda i, j, k: k)
        else:
            b_spec = _stacked(b_view, tk, tn, lambda i, j, k: k, lambda i, j, k: j)
    else:
        b_spec = pl.BlockSpec(b_blk, b_map)
    in_specs = [pl.BlockSpec(a_blk, a_map), b_spec]
    for arr, kind in extras:
        if kind == "tile":
            in_specs.append(pl.BlockSpec((tm, tn), lambda i, j, k: (i, j)))
        else:
            in_specs.append(pl.BlockSpec((1, tn), lambda i, j, k: (0, j)))
    ne, no = len(extras), len(outs)
    if epilogue is None:
        epilogue = lambda acc: (acc,)
    operands = [a, b, *[e[0] for e in extras]]
    aliases = {}
    if alias is not None:
        in_specs.append(ANY)
        aliases[len(operands)] = 0
        operands.append(alias)
    n_in = len(operands)
    if out_view is not None:
        out_specs = [_stacked(out_view, tm, tn, lambda i, j, k: i, lambda i, j, k: j)]
        out_shape = [jax.ShapeDtypeStruct(out_stack, outs[0])]
    else:
        out_specs = [pl.BlockSpec((tm, tn), lambda i, j, k: (i, j)) for _ in outs]
        out_shape = [jax.ShapeDtypeStruct((M, N), dt) for dt in outs]
    scratch = [pltpu.VMEM((tm, tn), f32)] if nk > 1 else []
    grid = (M // tm, N // tn, nk)
    if comm is not None:
        in_specs += [ANY] * len(comm["ins"])
        operands += comm["ins"]
        out_specs += [ANY] * len(comm["outs"])
        out_shape += comm["outs"]
        scratch += comm["sems"]
    nci, nco, ncs = (len(comm["ins"]), len(comm["outs"]), len(comm["sems"])) if comm is not None else (0, 0, 0)

    def body(*refs):
        a_ref, b_ref = refs[0], refs[1]
        ex, out_refs = refs[2:2 + ne], refs[n_in + nci:n_in + nci + no]
        rest = refs[n_in + nci + no + nco:]
        if comm is not None:
            cargs = (refs[n_in:n_in + nci], refs[n_in + nci + no:n_in + nci + no + nco], refs[len(refs) - ncs:])
            pid = [pl.program_id(d) for d in range(3)]

            @pl.when((pid[0] == 0) & (pid[1] == 0) & (pid[2] == 0))
            def _():
                comm["start"](*cargs)

        def finish(acc):
            res = epilogue(acc, *[e[...] for e in ex])
            for o, r in zip(out_refs, res):
                o[...] = r.astype(o.dtype)

        prod = lax.dot_general(a_ref[...].astype(bf16), b_ref[...].astype(bf16), _DIMS[mode],
                               preferred_element_type=f32)
        if nk == 1:
            finish(prod)
        else:
            acc_ref = rest[0]
            k = pl.program_id(2)

            @pl.when(k == 0)
            def _():
                acc_ref[...] = prod

            @pl.when(k > 0)
            def _():
                acc_ref[...] += prod

            @pl.when(k == nk - 1)
            def _():
                finish(acc_ref[...])

        if comm is not None:
            @pl.when((pid[0] == grid[0] - 1) & (pid[1] == grid[1] - 1) & (pid[2] == grid[2] - 1))
            def _():
                comm["finish"](*cargs)

    sem = ("arbitrary",) * 3 if comm is not None else ("parallel", "parallel", "arbitrary")
    res = pl.pallas_call(
        body, name=name, grid=grid, in_specs=in_specs, out_specs=out_specs, out_shape=out_shape,
        scratch_shapes=scratch, input_output_aliases=aliases, compiler_params=_cp(*sem),
    )(*operands)
    if comm is not None:
        return res[:no], res[no:]
    return res[0] if no == 1 else res


def _rms_fwd(x, g, out_dtype, name, tr=256):
    T, D = x.shape
    tr = min(tr, T)

    def body(x_ref, g_ref, o_ref):
        xv = x_ref[...]
        r = lax.rsqrt(jnp.mean(xv * xv, axis=-1, keepdims=True) + EPS)
        o_ref[...] = (xv * r * g_ref[...]).astype(o_ref.dtype)

    return pl.pallas_call(
        body, name=name, grid=(T // tr,),
        in_specs=[pl.BlockSpec((tr, D), lambda r: (r, 0)), pl.BlockSpec((1, D), lambda r: (0, 0))],
        out_specs=pl.BlockSpec((tr, D), lambda r: (r, 0)),
        out_shape=jax.ShapeDtypeStruct((T, D), out_dtype), compiler_params=_cp("parallel"),
    )(x, g)


def _rms_bwd(x, g, dh, dres, name, tr=256):
    T, D = x.shape
    tr = min(tr, T)

    def body(x_ref, g_ref, dh_ref, dres_ref, dx_ref, dg_ref):
        xv = x_ref[...]
        r = lax.rsqrt(jnp.mean(xv * xv, axis=-1, keepdims=True) + EPS)
        xh = xv * r
        dhv = dh_ref[...]
        dhg = dhv * g_ref[...]
        dx_ref[...] = dres_ref[...] + r * (dhg - xh * jnp.mean(dhg * xh, axis=-1, keepdims=True))

        @pl.when(pl.program_id(0) == 0)
        def _():
            dg_ref[...] = jnp.zeros_like(dg_ref)

        dg_ref[...] += jnp.sum(dhv * xh, axis=0, keepdims=True)

    row = pl.BlockSpec((tr, D), lambda r: (r, 0))
    vec = pl.BlockSpec((1, D), lambda r: (0, 0))
    return pl.pallas_call(
        body, name=name, grid=(T // tr,), in_specs=[row, vec, row, row], out_specs=[row, vec],
        out_shape=[jax.ShapeDtypeStruct((T, D), f32), jax.ShapeDtypeStruct((1, D), f32)],
        compiler_params=_cp("arbitrary"),
    )(x, g, dh, dres)


def _loss_grad(y, tgt, name, tr=256):
    T, D = y.shape
    tr = min(tr, T)

    def body(y_ref, t_ref, dy_ref, s_ref):
        e = y_ref[...] - t_ref[...]
        dy_ref[...] = e * (1.0 / D)

        @pl.when(pl.program_id(0) == 0)
        def _():
            s_ref[...] = jnp.zeros_like(s_ref)

        s_ref[...] += jnp.sum(e * e)

    row = pl.BlockSpec((tr, D), lambda r: (r, 0))
    return pl.pallas_call(
        body, name=name, grid=(T // tr,), in_specs=[row, row],
        out_specs=[row, pl.BlockSpec((8, LANES), lambda r: (0, 0))],
        out_shape=[jax.ShapeDtypeStruct((T, D), f32), jax.ShapeDtypeStruct((8, LANES), f32)],
        compiler_params=_cp("arbitrary"),
    )(y, tgt)


def _shift_down(cur, prev, s):
    rolled = pltpu.roll(cur, s, 0)
    top = pltpu.roll(prev, s, 0)
    row = lax.broadcasted_iota(jnp.int32, top.shape, 0)
    head = jnp.where(row < s, top, rolled[0:CONV_HALO])
    return jnp.concatenate([head, rolled[CONV_HALO:]], axis=0)


def _shift_up(cur, nxt, s):
    n = cur.shape[0]
    rolled = pltpu.roll(cur, n - s, 0)
    bot = pltpu.roll(nxt, CONV_HALO - s, 0)
    row = lax.broadcasted_iota(jnp.int32, bot.shape, 0)
    tail = jnp.where(row >= CONV_HALO - s, bot, rolled[n - CONV_HALO:])
    return jnp.concatenate([rolled[:n - CONV_HALO], tail], axis=0)


def _conv_pre(cur, prev, w_ref, b_ref):
    taps = [cur] + [_shift_down(cur, prev, s) for s in (1, 2, 3)]
    pre = b_ref[...] + w_ref[3:4, :] * taps[0]
    for s in (1, 2, 3):
        pre = pre + w_ref[3 - s:4 - s, :] * taps[s]
    return pre, taps


def _conv_specs(T, C, rc, cb):
    cur = pl.BlockSpec((rc, cb), lambda j, r: (r, j))
    prev = pl.BlockSpec((CONV_HALO, cb), lambda j, r: (jnp.maximum(r * (rc // CONV_HALO) - 1, 0), j))
    nxt = pl.BlockSpec((CONV_HALO, cb), lambda j, r: (jnp.minimum((r + 1) * (rc // CONV_HALO), T // CONV_HALO - 1), j))
    w = pl.BlockSpec((4, cb), lambda j, r: (0, j))
    b = pl.BlockSpec((1, cb), lambda j, r: (0, j))
    return cur, prev, nxt, w, b


def _conv_fwd(xraw, w, b, name):
    T, C = xraw.shape
    rc, cb = min(512, T), min(512, C)
    cur, prev, _, ws, bs = _conv_specs(T, C, rc, cb)

    def body(x_ref, p_ref, w_ref, b_ref, o_ref):
        pv = jnp.where(pl.program_id(1) > 0, p_ref[...], 0.0)
        pre, _ = _conv_pre(x_ref[...], pv, w_ref, b_ref)
        o_ref[...] = pre * _sigmoid(pre)

    return pl.pallas_call(
        body, name=name, grid=(C // cb, T // rc), in_specs=[cur, prev, ws, bs], out_specs=cur,
        out_shape=jax.ShapeDtypeStruct((T, C), f32), compiler_params=_cp("parallel", "parallel"),
    )(xraw, xraw, w, b)


def _conv_bwd_pre(xraw, dxbc, w, b, name):
    T, C = xraw.shape
    rc, cb = min(512, T), min(512, C)
    cur, prev, _, ws, bs = _conv_specs(T, C, rc, cb)

    def body(x_ref, p_ref, d_ref, w_ref, b_ref, dpre_ref, dw_ref, db_ref):
        pv = jnp.where(pl.program_id(1) > 0, p_ref[...], 0.0)
        pre, taps = _conv_pre(x_ref[...], pv, w_ref, b_ref)
        sg = _sigmoid(pre)
        dpre = d_ref[...] * (sg * (1.0 + pre * (1.0 - sg)))
        dpre_ref[...] = dpre

        @pl.when(pl.program_id(1) == 0)
        def _():
            dw_ref[...] = jnp.zeros_like(dw_ref)
            db_ref[...] = jnp.zeros_like(db_ref)

        row = lax.broadcasted_iota(jnp.int32, dw_ref.shape, 0)
        upd = jnp.zeros(dw_ref.shape, f32)
        for s in range(4):
            upd = upd + jnp.where(row == 3 - s, jnp.sum(dpre * taps[s], axis=0, keepdims=True), 0.0)
        dw_ref[...] += upd
        db_ref[...] += jnp.sum(dpre, axis=0, keepdims=True)

    return pl.pallas_call(
        body, name=name, grid=(C // cb, T // rc), in_specs=[cur, prev, cur, ws, bs], out_specs=[cur, ws, bs],
        out_shape=[jax.ShapeDtypeStruct((T, C), f32), jax.ShapeDtypeStruct((4, C), f32), jax.ShapeDtypeStruct((1, C), f32)],
        compiler_params=_cp("parallel", "arbitrary"),
    )(xraw, xraw, dxbc, w, b)


def _conv_bwd_in(dpre, w, name):
    T, C = dpre.shape
    rc, cb = min(512, T), min(512, C)
    cur, _, nxt, ws, _ = _conv_specs(T, C, rc, cb)
    nr = T // rc

    def body(d_ref, n_ref, w_ref, o_ref):
        nv = jnp.where(pl.program_id(1) < nr - 1, n_ref[...], 0.0)
        cv = d_ref[...]
        out = w_ref[3:4, :] * cv
        for s in (1, 2, 3):
            out = out + w_ref[3 - s:4 - s, :] * _shift_up(cv, nv, s)
        o_ref[...] = out.astype(o_ref.dtype)

    return pl.pallas_call(
        body, name=name, grid=(C // cb, nr), in_specs=[cur, nxt, ws], out_specs=cur,
        out_shape=jax.ShapeDtypeStruct((T, C), bf16), compiler_params=_cp("parallel", "parallel"),
    )(dpre, dpre, w)


HEAD_SHIFT = SSD_HEAD_DIM.bit_length() - 1


def _ssd_prep(dtr_ref, bias_ref, alog_ref, SW):
    L = SSD_CHUNK
    xs = dtr_ref[...] + bias_ref[...]
    dt = _softplus(xs)
    a = -jnp.exp(alog_ref[...])
    causal = lax.broadcasted_iota(jnp.int32, (L, L), 0) >= lax.broadcasted_iota(jnp.int32, (L, L), 1)
    cs = _mask_dot(causal.astype(bf16), dt * a, 3)
    spread = (lax.broadcasted_iota(jnp.int32, (LANES, SW), 0)
              == lax.shift_right_logical(lax.broadcasted_iota(jnp.int32, (LANES, SW), 1), HEAD_SHIFT)).astype(bf16)
    dt_x = _dot_mask(dt, spread, 3)
    cs_x = _dot_mask(cs, spread, 3)
    last_x = cs_x[L - 1:L, :]
    return xs, dt, a, causal, cs, cs.T, dt_x, jnp.exp(cs_x), jnp.exp(last_x - cs_x), jnp.exp(last_x)


def _head_decay(cs, csT, causal, h):
    seg = cs[:, h:h + 1] - csT[h:h + 1, :]
    return jnp.where(causal, jnp.exp(jnp.minimum(seg, 0.0)), 0.0)


def _ssd_fwd(xbc, dtraw, z, dt_bias, a_log, d_skip_x, out_norm, HS, name, comm=None):
    T = xbc.shape[0]
    L, P, NS, G = SSD_CHUNK, SSD_HEAD_DIM, SSD_STATE, SSD_GROUPS
    SW, HPG, nc = HS * P, HS // SSD_GROUPS, T // SSD_CHUNK
    gsz = SW // G
    gw = HPG * P
    assert HPG % 2 == 0 and 2 * P == LANES

    def body(*refs):
        (xbc_ref, dtr_ref, z_ref, bias_ref, alog_ref, dsk_ref, on_ref), (y_ref, yn_ref, sp_ref), (st_ref,), cargs = _hosted(comm, 7, 3, refs)
        first, last = _first_last((nc,))
        if comm is not None:
            @pl.when(first)
            def _():
                comm["start"](*cargs)

            @pl.when(last)
            def _():
                comm["finish"](*cargs)

        @pl.when(pl.program_id(0) == 0)
        def _():
            st_ref[...] = jnp.zeros_like(st_ref)

        sp_ref[0] = st_ref[...]
        _, _, _, causal, cs, csT, dt_x, ecs_x, dte_x, cdec_x = _ssd_prep(dtr_ref, bias_ref, alog_ref, SW)
        X = xbc_ref[:, 0:SW]
        Xd = X * dt_x
        Xdb = Xd.astype(bf16)
        XEb = (Xd * dte_x).astype(bf16)
        left = lax.broadcasted_iota(jnp.int32, (L, LANES), 1) < P
        for g in range(G):
            gs = slice(g * gw, (g + 1) * gw)
            Bb = xbc_ref[:, SW + g * NS:SW + (g + 1) * NS].astype(bf16)
            Cb = xbc_ref[:, SW + (G + g) * NS:SW + (G + g + 1) * NS].astype(bf16)
            Gm = _dot(Cb, Bb, "nt")
            Sp = st_ref[:, gs]
            yo = _dot(Cb, Sp.astype(bf16)) * ecs_x[:, gs]
            st_ref[:, gs] = cdec_x[:, gs] * Sp + _dot(Bb, XEb[:, gs], "tn")
            for pr in range(HPG // 2):
                h0 = g * HPG + 2 * pr
                ps = slice(h0 * P, (h0 + 2) * P)
                xp = Xdb[:, ps]
                yd = jnp.where(left, _dot((_head_decay(cs, csT, causal, h0) * Gm).astype(bf16), xp),
                               _dot((_head_decay(cs, csT, causal, h0 + 1) * Gm).astype(bf16), xp))
                y_ref[:, ps] = yd + yo[:, pr * LANES:(pr + 1) * LANES] + dsk_ref[:, ps] * X[:, ps]
        zz = z_ref[...]
        gated = y_ref[...] * (zz * _sigmoid(zz))
        for g in range(G):
            gs = slice(g * gsz, (g + 1) * gsz)
            sg = gated[:, gs]
            rr = lax.rsqrt(jnp.mean(sg * sg, axis=-1, keepdims=True) + EPS)
            yn_ref[:, gs] = (sg * rr * on_ref[:, gs]).astype(yn_ref.dtype)

    vec = pl.BlockSpec((1, LANES), lambda c: (0, 0))
    wide = pl.BlockSpec((1, SW), lambda c: (0, 0))
    (y, yn, sp), got = _host_call(
        body, name, (nc,),
        [pl.BlockSpec((L, xbc.shape[1]), lambda c: (c, 0)), pl.BlockSpec((L, LANES), lambda c: (c, 0)),
         pl.BlockSpec((L, SW), lambda c: (c, 0)), vec, vec, wide, wide],
        [pl.BlockSpec((L, SW), lambda c: (c, 0)), pl.BlockSpec((L, SW), lambda c: (c, 0)),
         pl.BlockSpec((1, NS, SW), lambda c: (c, 0, 0))],
        [jax.ShapeDtypeStruct((T, SW), f32), jax.ShapeDtypeStruct((T, SW), bf16), jax.ShapeDtypeStruct((nc, NS, SW), f32)],
        [pltpu.VMEM((NS, SW), f32)], [xbc, dtraw, z, dt_bias, a_log, d_skip_x, out_norm], comm)
    return y, yn, sp, got


def _ssd_bwd(xbc, dtraw, sprev, dy, dt_bias, a_log, d_skip_x, HS, name):
    T = xbc.shape[0]
    L, P, NS, G = SSD_CHUNK, SSD_HEAD_DIM, SSD_STATE, SSD_GROUPS
    SW, HPG, nc = HS * P, HS // SSD_GROUPS, T // SSD_CHUNK
    gw = HPG * P

    def body(xbc_ref, dtr_ref, sp_ref, dy_ref, bias_ref, alog_ref, dsk_ref,
             dxbc_ref, ddtr_ref, dalog_ref, dbias_ref, dd_ref, ds_ref):
        @pl.when(pl.program_id(0) == 0)
        def _():
            ds_ref[...] = jnp.zeros_like(ds_ref)
            dalog_ref[...] = jnp.zeros_like(dalog_ref)
            dbias_ref[...] = jnp.zeros_like(dbias_ref)
            dd_ref[...] = jnp.zeros_like(dd_ref)

        xs, dt, a, causal, cs, csT, dt_x, ecs_x, dte_x, cdec_x = _ssd_prep(dtr_ref, bias_ref, alog_ref, SW)
        lane = lax.broadcasted_iota(jnp.int32, (L, LANES), 1)
        sub = lax.broadcasted_iota(jnp.int32, (LANES, L), 0)
        left = lane < P
        dcs = jnp.zeros((L, LANES), f32)
        dcs_t = jnp.zeros((LANES, L), f32)
        xds = jnp.zeros((L, LANES), f32)
        dlast = jnp.zeros((1, LANES), f32)
        dD = jnp.zeros((1, LANES), f32)
        for g in range(G):
            gs = slice(g * gw, (g + 1) * gw)
            bsl = slice(SW + g * NS, SW + (g + 1) * NS)
            csl = slice(SW + (G + g) * NS, SW + (G + g + 1) * NS)
            Bb = xbc_ref[:, bsl].astype(bf16)
            Cb = xbc_ref[:, csl].astype(bf16)
            Gm = _dot(Cb, Bb, "nt")
            X = xbc_ref[:, gs]
            Xd = X * dt_x[:, gs]
            Xdb = Xd.astype(bf16)
            XE = Xd * dte_x[:, gs]
            dY = dy_ref[:, gs]
            dYb = dY.astype(bf16)
            Wb = (dY * ecs_x[:, gs]).astype(bf16)
            Sp = sp_ref[0, :, gs]
            Spb = Sp.astype(bf16)
            dS = ds_ref[:, gs]
            dSb = dS.astype(bf16)
            CS = _dot(Cb, Spb)
            Zb = _dot(Bb, dSb)
            dC = _dot(Wb, Spb, "nt")
            dB = _dot(XE.astype(bf16), dSb, "nt")
            ds_ref[:, gs] = cdec_x[:, gs] * dS + _dot(Cb, Wb, "tn")
            R1 = dY * CS * ecs_x[:, gs]
            R2 = XE * Zb
            to_head = (lax.shift_right_logical(lax.broadcasted_iota(jnp.int32, (gw, LANES), 0), HEAD_SHIFT) + g * HPG
                       == lax.broadcasted_iota(jnp.int32, (gw, LANES), 1)).astype(bf16)
            dcs = dcs + _dot_mask(R1 - R2, to_head, 3)
            dlast = (dlast + jnp.sum(_dot_mask(R2, to_head, 3), axis=0, keepdims=True)
                     + jnp.sum(_dot_mask(Sp * dS * cdec_x[:, gs], to_head, 3), axis=0, keepdims=True))
            dG = jnp.zeros((L, L), f32)
            pieces = []
            for pr in range(HPG // 2):
                h0 = g * HPG + 2 * pr
                pw = slice(pr * LANES, (pr + 1) * LANES)
                xp, dyp = Xdb[:, pw], dYb[:, pw]
                halves = []
                for k, h in enumerate((h0, h0 + 1)):
                    Lm = _head_decay(cs, csT, causal, h)
                    Mf = Lm * Gm
                    keep = left if k == 0 else jnp.logical_not(left)
                    dM = _dot(jnp.where(keep, dyp, jnp.zeros_like(dyp)), xp, "nt")
                    Q = dM * Mf
                    dcs = dcs + jnp.where(lane == h, jnp.sum(Q, axis=1, keepdims=True), 0.0)
                    dcs_t = dcs_t - jnp.where(sub == h, jnp.sum(Q, axis=0, keepdims=True), 0.0)
                    dG = dG + dM * Lm
                    halves.append(_dot(Mf.astype(bf16), dyp, "tn"))
                pieces.append(jnp.where(left, halves[0], halves[1]))
            dXd = jnp.concatenate(pieces, axis=1) + dte_x[:, gs] * Zb
            dxbc_ref[:, gs] = dXd * dt_x[:, gs] + dsk_ref[:, gs] * dY
            xds = xds + _dot_mask(dXd * X, to_head, 3)
            dD = dD + jnp.sum(_dot_mask(dY * X, to_head, 3), axis=0, keepdims=True)
            dGb = dG.astype(bf16)
            dxbc_ref[:, bsl] = dB + _dot(dGb, Cb, "tn")
            dxbc_ref[:, csl] = dC + _dot(dGb, Bb)
        rowi = lax.broadcasted_iota(jnp.int32, (L, LANES), 0)
        dcs = dcs + dcs_t.T + jnp.where(rowi == L - 1, dlast, 0.0)
        anti = (lax.broadcasted_iota(jnp.int32, (L, L), 1) >= lax.broadcasted_iota(jnp.int32, (L, L), 0)).astype(bf16)
        dda = _mask_dot(anti, dcs, 3)
        ddt = dda * a + xds
        dalog_ref[...] += jnp.sum(dda * dt, axis=0, keepdims=True) * a
        ddtr = ddt * _sigmoid(xs)
        ddtr_ref[...] = ddtr
        dbias_ref[...] += jnp.sum(ddtr, axis=0, keepdims=True)
        dd_ref[...] += dD

    rev = lambda c: (nc - 1 - c, 0)
    vec = pl.BlockSpec((1, LANES), lambda c: (0, 0))
    return pl.pallas_call(
        body, name=name, grid=(nc,),
        in_specs=[pl.BlockSpec((L, xbc.shape[1]), rev), pl.BlockSpec((L, LANES), rev),
                  pl.BlockSpec((1, NS, SW), lambda c: (nc - 1 - c, 0, 0)), pl.BlockSpec((L, SW), rev), vec, vec,
                  pl.BlockSpec((1, SW), lambda c: (0, 0))],
        out_specs=[pl.BlockSpec((L, xbc.shape[1]), rev), pl.BlockSpec((L, LANES), rev), vec, vec, vec],
        out_shape=[jax.ShapeDtypeStruct(xbc.shape, f32), jax.ShapeDtypeStruct((T, LANES), f32)]
        + [jax.ShapeDtypeStruct((1, LANES), f32)] * 3,
        scratch_shapes=[pltpu.VMEM((NS, SW), f32)], compiler_params=_cp("arbitrary"),
    )(xbc, dtraw, sprev, dy, dt_bias, a_log, d_skip_x)


def _gate_bwd(y, z, dyn, out_norm, name, tr=256):
    T, SW = y.shape
    tr = min(tr, T)
    gsz = SW // SSD_GROUPS

    def body(y_ref, z_ref, d_ref, on_ref, dy_ref, dz_ref, don_ref):
        @pl.when(pl.program_id(0) == 0)
        def _():
            don_ref[...] = jnp.zeros_like(don_ref)

        for g in range(SSD_GROUPS):
            gs = slice(g * gsz, (g + 1) * gsz)
            yv, zv, dv = y_ref[:, gs], z_ref[:, gs], d_ref[:, gs]
            sg = _sigmoid(zv)
            sl = zv * sg
            gated = yv * sl
            rr = lax.rsqrt(jnp.mean(gated * gated, axis=-1, keepdims=True) + EPS)
            gh = gated * rr
            dgn = dv * on_ref[:, gs]
            dgated = rr * (dgn - gh * jnp.mean(dgn * gh, axis=-1, keepdims=True))
            dy_ref[:, gs] = dgated * sl
            dz_ref[:, gs] = (dgated * yv * (sg * (1.0 + zv * (1.0 - sg)))).astype(dz_ref.dtype)
            don_ref[:, gs] += jnp.sum(dv * gh, axis=0, keepdims=True)

    row = pl.BlockSpec((tr, SW), lambda r: (r, 0))
    vec = pl.BlockSpec((1, SW), lambda r: (0, 0))
    return pl.pallas_call(
        body, name=name, grid=(T // tr,), in_specs=[row, row, row, vec], out_specs=[row, row, vec],
        out_shape=[jax.ShapeDtypeStruct((T, SW), f32), jax.ShapeDtypeStruct((T, SW), bf16),
                   jax.ShapeDtypeStruct((1, SW), f32)],
        compiler_params=_cp("arbitrary"),
    )(y, z, dyn, out_norm)


def _qk_norm_fwd(qkv, qn_w, kn_w, SBW, name, tr=256):
    T = qkv.shape[0]
    tr = min(tr, T)
    nh = SBW // SB_HEAD_DIM

    def body(q_ref, k_ref, v_ref, qw_ref, kw_ref, qo_ref, ko_ref, vo_ref):
        for src, w_ref, dst in ((q_ref, qw_ref, qo_ref), (k_ref, kw_ref, ko_ref)):
            for h in range(nh):
                hs = slice(h * SB_HEAD_DIM, (h + 1) * SB_HEAD_DIM)
                sv = src[:, hs]
                rr = lax.rsqrt(jnp.mean(sv * sv, axis=-1, keepdims=True) + EPS)
                dst[:, hs] = (sv * rr * w_ref[...]).astype(dst.dtype)
        vo_ref[...] = v_ref[...].astype(vo_ref.dtype)

    blk = lambda j: pl.BlockSpec((tr, SBW), lambda r: (r, j))
    vec = pl.BlockSpec((1, SB_HEAD_DIM), lambda r: (0, 0))
    out = pl.BlockSpec((tr, SBW), lambda r: (r, 0))
    return pl.pallas_call(
        body, name=name, grid=(T // tr,), in_specs=[blk(0), blk(1), blk(2), vec, vec], out_specs=[out, out, out],
        out_shape=[jax.ShapeDtypeStruct((T, SBW), bf16)] * 3, compiler_params=_cp("parallel"),
    )(qkv, qkv, qkv, qn_w, kn_w)


def _qk_norm_bwd(qkv, dqn, dkn, dv, qn_w, kn_w, SBW, name, tr=256):
    T = qkv.shape[0]
    tr = min(tr, T)
    nh = SBW // SB_HEAD_DIM

    def body(q_ref, k_ref, dq_ref, dk_ref, dv_ref, qw_ref, kw_ref, o_ref, dqw_ref, dkw_ref):
        @pl.when(pl.program_id(0) == 0)
        def _():
            dqw_ref[...] = jnp.zeros_like(dqw_ref)
            dkw_ref[...] = jnp.zeros_like(dkw_ref)

        for part, (src, d_ref, w_ref, dw_ref) in enumerate(((q_ref, dq_ref, qw_ref, dqw_ref), (k_ref, dk_ref, kw_ref, dkw_ref))):
            dw = jnp.zeros((1, SB_HEAD_DIM), f32)
            for h in range(nh):
                hs = slice(h * SB_HEAD_DIM, (h + 1) * SB_HEAD_DIM)
                os_ = slice(part * SBW + h * SB_HEAD_DIM, part * SBW + (h + 1) * SB_HEAD_DIM)
                sv, dn = src[:, hs], d_ref[:, hs]
                rr = lax.rsqrt(jnp.mean(sv * sv, axis=-1, keepdims=True) + EPS)
                xh = sv * rr
                dg = dn * w_ref[...]
                o_ref[:, os_] = (rr * (dg - xh * jnp.mean(dg * xh, axis=-1, keepdims=True))).astype(o_ref.dtype)
                dw = dw + jnp.sum(dn * xh, axis=0, keepdims=True)
            dw_ref[...] += dw
        o_ref[:, 2 * SBW:] = dv_ref[...].astype(o_ref.dtype)

    blk = lambda j: pl.BlockSpec((tr, SBW), lambda r: (r, j))
    vec = pl.BlockSpec((1, SB_HEAD_DIM), lambda r: (0, 0))
    row = pl.BlockSpec((tr, SBW), lambda r: (r, 0))
    return pl.pallas_call(
        body, name=name, grid=(T // tr,), in_specs=[blk(0), blk(1), row, row, row, vec, vec],
        out_specs=[pl.BlockSpec((tr, 3 * SBW), lambda r: (r, 0)), vec, vec],
        out_shape=[jax.ShapeDtypeStruct((T, 3 * SBW), bf16)] + [jax.ShapeDtypeStruct((1, SB_HEAD_DIM), f32)] * 2,
        compiler_params=_cp("arbitrary"),
    )(qkv, qkv, dqn, dkn, dv, qn_w, kn_w)


def _sb_logits(q, kb, scale):
    zl = _dot(q, kb, "nt") * scale
    lb = jnp.minimum(zl, 0.0) - jnp.log(1.0 + jnp.exp(-jnp.abs(zl)))
    return zl, lb, lb - zl


def _tail_update(old, r0, new_tail):
    return new_tail if r0 == 0 else jnp.concatenate([old[:r0], new_tail], axis=0)


def _hosted(comm, n_in, n_out, refs):
    nci, nco, ncs = (len(comm["ins"]), len(comm["outs"]), len(comm["sems"])) if comm is not None else (0, 0, 0)
    ins, outs = refs[:n_in], refs[n_in + nci:n_in + nci + n_out]
    scratch = refs[n_in + nci + n_out + nco:len(refs) - ncs]
    cargs = (refs[n_in:n_in + nci], refs[n_in + nci + n_out:n_in + nci + n_out + nco], refs[len(refs) - ncs:])
    return ins, outs, scratch, cargs


def _host_call(body, name, grid, in_specs, out_specs, out_shape, scratch, operands, comm):
    n_out = len(out_shape)
    in_specs, out_specs, out_shape, scratch, operands = list(in_specs), list(out_specs), list(out_shape), list(scratch), list(operands)
    io = {}
    if comm is not None:
        for src, dst in comm.get("aliases", {}).items():
            io[len(operands) + src] = n_out + dst
        in_specs += [ANY] * len(comm["ins"])
        operands += comm["ins"]
        out_specs += [ANY] * len(comm["outs"])
        out_shape += comm["outs"]
        scratch += comm["sems"]
    res = pl.pallas_call(body, name=name, grid=grid, in_specs=in_specs, out_specs=out_specs, out_shape=out_shape,
                         scratch_shapes=scratch, input_output_aliases=io,
                         compiler_params=_cp(*(("arbitrary",) * len(grid))))(*operands)
    return res[:n_out], res[n_out:]


def _first_last(grid):
    pid = [pl.program_id(d) for d in range(len(grid))]
    first, last = pid[0] == 0, pid[0] == grid[0] - 1
    for d in range(1, len(grid)):
        first, last = first & (pid[d] == 0), last & (pid[d] == grid[d] - 1)
    return first, last


def _sb_fwd(qn, kn, vb, name, comm=None, tq=2048, tk=256):
    T, W = qn.shape
    tq = min(tq, T)
    tk = min(tk, tq)
    nh, nq, dh, nd = W // SB_HEAD_DIM, T // tq, SB_HEAD_DIM, tq // tk
    scale = dh ** -0.5
    grid = (nh, nq)

    def body(*refs):
        (q_ref, k_ref, v_ref), (o_ref, c_ref), _, cargs = _hosted(comm, 3, 2, refs)
        first, last = _first_last(grid)
        if comm is not None:
            @pl.when(first)
            def _():
                comm["start"](*cargs)

        qi = pl.program_id(1)
        q = q_ref[...]
        later = (lax.broadcasted_iota(jnp.int32, (tk, tk), 0) > lax.broadcasted_iota(jnp.int32, (tk, tk), 1)).astype(bf16)

        def step(j, carry, d):
            acc, run = carry
            r0 = 0 if d is None else d * tk
            ks = pl.multiple_of(j * tk, tk)
            kb, vv = k_ref[pl.ds(ks, tk), :], v_ref[pl.ds(ks, tk), :]
            _, lb, lk = _sb_logits(q[r0:], kb, scale)
            if d is not None:
                mask = lax.broadcasted_iota(jnp.int32, lk.shape, 1) < lax.broadcasted_iota(jnp.int32, lk.shape, 0)
                lk = jnp.where(mask, lk, 0.0)
            between = _dot_mask(lk, later, 2)
            w = jnp.exp(lb + between + run[r0:])
            if d is not None:
                w = jnp.where(mask, w, 0.0)
            return (_tail_update(acc, r0, acc[r0:] + _dot(w.astype(bf16), vv)),
                    _tail_update(run, r0, run[r0:] + between[:, 0:1] + lk[:, 0:1]))

        carry = (jnp.zeros((tq, dh), f32), jnp.zeros((tq, 1), f32))
        for d in range(nd - 1, -1, -1):
            carry = step(qi * nd + d, carry, d)
        n_before = qi * nd
        acc, run = lax.fori_loop(0, n_before, lambda t, c: step(n_before - 1 - t, c, None), carry)
        o_ref[...] = acc.astype(o_ref.dtype)
        c_ref[...] = jnp.broadcast_to(run, (tq, dh))
        if comm is not None:
            @pl.when(last)
            def _():
                comm["finish"](*cargs)

    qblk = pl.BlockSpec((tq, dh), lambda h, i: (i, h))
    full = pl.BlockSpec((T, dh), lambda h, i: (0, h))
    (o, c), got = _host_call(body, name, grid, [qblk, full, full], [qblk, qblk],
                             [jax.ShapeDtypeStruct((T, W), bf16), jax.ShapeDtypeStruct((T, W), f32)], [], [qn, kn, vb], comm)
    return o, c, got


def _sb_bwd(qn, kn, vb, do, ctot, do_off, name, comm=None, tq=2048, tk=256):
    T, W = qn.shape
    tq = min(tq, T)
    tk = min(tk, tq)
    nh, nq, dh, nd = W // SB_HEAD_DIM, T // tq, SB_HEAD_DIM, tq // tk
    scale = dh ** -0.5
    ob = do_off // dh
    grid = (nh, nq)

    def body(*refs):
        (q_ref, k_ref, v_ref, do_ref, c_ref), (dq_ref, dk_ref, dv_ref), _, cargs = _hosted(comm, 5, 3, refs)
        first, last = _first_last(grid)
        if comm is not None:
            @pl.when(first)
            def _():
                comm["start"](*cargs)

        qi = pl.program_id(1)

        @pl.when(qi == 0)
        def _():
            dk_ref[...] = jnp.zeros_like(dk_ref)
            dv_ref[...] = jnp.zeros_like(dv_ref)

        q = q_ref[...]
        dob = do_ref[...].astype(bf16)
        total = c_ref[:, 0:1]
        r2 = lax.broadcasted_iota(jnp.int32, (tk, tk), 0)
        c2 = lax.broadcasted_iota(jnp.int32, (tk, tk), 1)
        upto = (r2 <= c2).astype(bf16)
        before = (r2 < c2).astype(bf16)

        def step(j, carry, d):
            dq, pre, gpre = carry
            r0 = 0 if d is None else d * tk
            ks = pl.multiple_of(j * tk, tk)
            kb, vv = k_ref[pl.ds(ks, tk), :], v_ref[pl.ds(ks, tk), :]
            qs, dos = q[r0:], dob[r0:]
            _, lb, lk = _sb_logits(qs, kb, scale)
            if d is not None:
                mask = lax.broadcasted_iota(jnp.int32, lk.shape, 1) < lax.broadcasted_iota(jnp.int32, lk.shape, 0)
                lk = jnp.where(mask, lk, 0.0)
            pin = _dot_mask(lk, upto, 2)
            w = jnp.exp(lb + (total[r0:] - pre[r0:] - pin))
            if d is not None:
                w = jnp.where(mask, w, 0.0)
            dw = _dot(dos, vv, "nt")
            dv_ref[pl.ds(ks, tk), :] += _dot(w.astype(bf16), dos, "tn")
            gg = dw * w
            gex = _dot(gg.astype(bf16), before)
            beta = jnp.exp(lb)
            dz = (gg * (1.0 - beta) - (gpre[r0:] + gex) * beta) * scale
            if d is not None:
                dz = jnp.where(mask, dz, 0.0)
            dzb = dz.astype(bf16)
            dk_ref[pl.ds(ks, tk), :] += _dot(dzb, qs, "tn")
            return (_tail_update(dq, r0, dq[r0:] + _dot(dzb, kb)),
                    _tail_update(pre, r0, pre[r0:] + pin[:, tk - 1:tk]),
                    _tail_update(gpre, r0, gpre[r0:] + gex[:, tk - 1:tk] + gg[:, tk - 1:tk]))

        init = (jnp.zeros((tq, dh), f32), jnp.zeros((tq, 1), f32), jnp.zeros((tq, 1), f32))
        carry = lax.fori_loop(0, qi * nd, lambda t, c: step(t, c, None), init)
        for d in range(nd):
            carry = step(qi * nd + d, carry, d)
        dq_ref[...] = carry[0]
        if comm is not None:
            @pl.when(last)
            def _():
                comm["finish"](*cargs)

    qblk = pl.BlockSpec((tq, dh), lambda h, i: (i, h))
    full = pl.BlockSpec((T, dh), lambda h, i: (0, h))
    (dq, dk, dv), got = _host_call(
        body, name, grid, [qblk, full, full, pl.BlockSpec((tq, dh), lambda h, i: (i, h + ob)), qblk], [qblk, full, full],
        [jax.ShapeDtypeStruct((T, W), f32)] * 3, [], [qn, kn, vb, do, ctot], comm)
    return dq, dk, dv, got


def _pool_select(sums, g):
    return jnp.where(g == 0, sums[0], jnp.where(g == 1, sums[1], jnp.where(g == 2, sums[2], sums[3])))


def _pool_count(g, r, rc, n, cols, off=0):
    t = (r * rc + off + lax.broadcasted_iota(jnp.int32, (n, cols), 0)).astype(f32)
    win = jnp.left_shift(2, g).astype(f32)
    return jnp.minimum(t + 1.0, win)


def _pool_fwd(hp, xres, w, b, scale, name, rc=512):
    T, D = hp.shape
    rc = min(rc, T)
    pg = D // len(POOL_WINDOWS)

    def body(h_ref, p_ref, x_ref, w_ref, b_ref, s_ref, o_ref, yp_ref, d_ref):
        g, r = pl.program_id(0), pl.program_id(1)
        cur = h_ref[...]
        halo = jnp.where(r > 0, p_ref[...], 0.0)
        ext = jnp.concatenate([halo, cur], axis=0)
        sums, s = [], ext
        for sh in (1, 2, 4, 8):
            s = s + pltpu.roll(s, sh, 0)
            sums.append(s)
        d = _pool_select(sums, g)[POOL_HALO:] / _pool_count(g, r, rc, rc, pg) - cur
        yp = _dot(d.astype(bf16), w_ref[0]) + b_ref[...]
        yp_ref[...] = yp
        d_ref[...] = d.astype(d_ref.dtype)
        o_ref[...] = x_ref[...] + yp * s_ref[...]

    cur = pl.BlockSpec((rc, pg), lambda g, r: (r, g))
    prev = pl.BlockSpec((POOL_HALO, pg), lambda g, r: (jnp.maximum(r * (rc // POOL_HALO) - 1, 0), g))
    vec = pl.BlockSpec((1, pg), lambda g, r: (0, g))
    return pl.pallas_call(
        body, name=name, grid=(len(POOL_WINDOWS), T // rc),
        in_specs=[cur, prev, cur, pl.BlockSpec((1, pg, pg), lambda g, r: (g, 0, 0)), vec, vec],
        out_specs=[cur, cur, cur],
        out_shape=[jax.ShapeDtypeStruct((T, D), f32), jax.ShapeDtypeStruct((T, D), f32), jax.ShapeDtypeStruct((T, D), bf16)],
        compiler_params=_cp("parallel", "parallel"),
    )(hp, hp, xres, w, b, scale)


def _pool_bwd(dx, yp, d, w, scale, name, rc=512):
    T, D = dx.shape
    rc = min(rc, T)
    pg = D // len(POOL_WINDOWS)
    nr = T // rc

    def body(dx_ref, dn_ref, yp_ref, d_ref, w_ref, s_ref, dh_ref, dw_ref, db_ref, dsc_ref):
        g, r = pl.program_id(0), pl.program_id(1)

        @pl.when(r == 0)
        def _():
            dw_ref[...] = jnp.zeros_like(dw_ref)
            db_ref[...] = jnp.zeros_like(db_ref)
            dsc_ref[...] = jnp.zeros_like(dsc_ref)

        dxv = dx_ref[...]
        dyp = dxv * s_ref[...]
        dsc_ref[...] += jnp.sum(dxv * yp_ref[...], axis=0, keepdims=True)
        db_ref[...] += jnp.sum(dyp, axis=0, keepdims=True)
        dypb = dyp.astype(bf16)
        dw_ref[0] += _dot(d_ref[...], dypb, "tn")
        dd = _dot(dypb, w_ref[0], "nt")
        ddn = _dot((dn_ref[...] * s_ref[...]).astype(bf16), w_ref[0], "nt")
        e = dd / _pool_count(g, r, rc, rc, pg)
        en = jnp.where(r < nr - 1, ddn / _pool_count(g, r, rc, POOL_HALO, pg, off=rc), 0.0)
        ext = jnp.concatenate([e, en], axis=0)
        sums, s = [], ext
        for sh in (1, 2, 4, 8):
            s = s + pltpu.roll(s, rc + POOL_HALO - sh, 0)
            sums.append(s)
        dh_ref[...] = _pool_select(sums, g)[:rc] - dd

    cur = pl.BlockSpec((rc, pg), lambda g, r: (r, g))
    nxt = pl.BlockSpec((POOL_HALO, pg), lambda g, r: (jnp.minimum((r + 1) * (rc // POOL_HALO), T // POOL_HALO - 1), g))
    vec = pl.BlockSpec((1, pg), lambda g, r: (0, g))
    wsp = pl.BlockSpec((1, pg, pg), lambda g, r: (g, 0, 0))
    return pl.pallas_call(
        body, name=name, grid=(len(POOL_WINDOWS), nr), in_specs=[cur, nxt, cur, cur, wsp, vec],
        out_specs=[cur, wsp, vec, vec],
        out_shape=[jax.ShapeDtypeStruct((T, D), f32), jax.ShapeDtypeStruct(w.shape, f32),
                   jax.ShapeDtypeStruct((1, D), f32), jax.ShapeDtypeStruct((1, D), f32)],
        compiler_params=_cp("parallel", "arbitrary"),
    )(dx, dx, yp, d, w, scale)


def _adamw(w, g, m, v, name, tr=256):
    R, C = w.shape
    tr = min(tr, R)
    assert R % tr == 0

    def body(w_ref, g_ref, m_ref, v_ref, d_ref, mo_ref, vo_ref):
        gv = g_ref[...]
        mn = ADAM_B1 * m_ref[...] + (1.0 - ADAM_B1) * gv
        vn = ADAM_B2 * v_ref[...] + (1.0 - ADAM_B2) * (gv * gv)
        m_hat = mn / (1.0 - ADAM_B1 ** ADAM_STEP)
        v_hat = vn / (1.0 - ADAM_B2 ** ADAM_STEP)
        d_ref[...] = -ADAM_LR * (m_hat / (jnp.sqrt(v_hat) + ADAM_EPS) + ADAM_WD * w_ref[...])
        mo_ref[...] = mn
        vo_ref[...] = vn

    blk = pl.BlockSpec((tr, C), lambda r: (r, 0))
    return pl.pallas_call(
        body, name=name, grid=(R // tr,), in_specs=[blk] * 4, out_specs=[blk] * 3,
        out_shape=[jax.ShapeDtypeStruct((R, C), f32)] * 3, compiler_params=_cp("parallel"),
    )(w, g, m, v)


def _pair_sum(g4, recv, name, br=256):
    _, R, C = g4.shape
    hr = R // 2
    br = min(br, hr)
    nb = hr // br

    def body(a_ref, b_ref, o_ref, ob_ref):
        s = a_ref[...] + b_ref[...]
        o_ref[...] = s
        ob_ref[...] = s.astype(ob_ref.dtype)

    out = pl.BlockSpec((1, br, C), lambda s, i: (s, i, 0))
    return pl.pallas_call(
        body, name=name, grid=(N_CHIPS, nb),
        in_specs=[pl.BlockSpec((1, br, C), lambda s, i: (s, lax.axis_index("c") * nb + i, 0)), out],
        out_specs=[out, out],
        out_shape=[jax.ShapeDtypeStruct((N_CHIPS, hr, C), f32), jax.ShapeDtypeStruct((N_CHIPS, hr, C), bf16)],
        compiler_params=_cp("parallel", "parallel"),
    )(g4, recv)


def _chip_sum(own4, recv4, name, br=256):
    _, hr, C = own4.shape
    br = min(br, hr)
    nb = hr // br
    chip = lambda: 2 * lax.axis_index("x") + lax.axis_index("y")

    def body(a_ref, b1_ref, b2_ref, b3_ref, o_ref):
        o_ref[...] = ((a_ref[0] + b1_ref[0].astype(f32)) + b2_ref[0].astype(f32)) + b3_ref[0].astype(f32)

    other = lambda k: pl.BlockSpec((1, br, C), lambda i: ((chip() + k) % N_CHIPS, i, 0))
    return pl.pallas_call(
        body, name=name, grid=(nb,),
        in_specs=[pl.BlockSpec((1, br, C), lambda i: (chip(), i, 0)), other(1), other(2), other(3)],
        out_specs=pl.BlockSpec((br, C), lambda i: (lax.axis_index("c") * nb + i, 0)),
        out_shape=jax.ShapeDtypeStruct((2 * hr, C), f32), compiler_params=_cp("parallel"),
    )(own4, recv4, recv4, recv4)


ANY = pl.BlockSpec(memory_space=pl.ANY)


def _mesh_pos():
    x, y, c = lax.axis_index("x"), lax.axis_index("y"), lax.axis_index("c")
    others = [(1 - x, y), (x, 1 - y), (1 - x, 1 - y)]
    return x, y, c, 2 * x + y, others


def _gather_small(blk, name):
    m, n = blk.shape

    def body(x_ref, out_ref, sum_ref, send_sems, recv_sems, local_sem):
        x, y, c, _, others = _mesh_pos()
        me, sibling = (x, y, c), (x, y, 1 - c)

        def rows(px, py, pc):
            return out_ref.at[pl.ds((4 * px + 2 * py + pc) * m, m), :]

        def copy(k, block, to, src=None):
            return pltpu.make_async_remote_copy(
                src_ref=rows(*block) if src is None else src, dst_ref=rows(*block),
                send_sem=send_sems.at[k], recv_sem=recv_sems.at[k], device_id=to, device_id_type=MESH)

        mine = pltpu.make_async_copy(x_ref, rows(*me), local_sem)
        mine.start()
        first = [copy(0, me, sibling, src=x_ref)]
        first += [copy(1 + j, me, (*chip, c), src=x_ref) for j, chip in enumerate(others)]
        for cp in first:
            cp.start()
        passed = [copy(4 + j, (*chip, c), sibling) for j, chip in enumerate(others)]
        for j, chip in enumerate(others):
            copy(1 + j, (*chip, c), me).wait_recv()
            passed[j].start()
        copy(0, sibling, me).wait_recv()
        for j, chip in enumerate(others):
            copy(4 + j, (*chip, 1 - c), me).wait_recv()
        for cp in first + passed:
            cp.wait_send()
        mine.wait()
        acc = out_ref[0:m, :]
        for d in range(1, 8):
            acc = acc + out_ref[d * m:(d + 1) * m, :]
        sum_ref[...] = acc

    vm = pl.BlockSpec(memory_space=pltpu.VMEM)
    return pl.pallas_call(
        body, name=name, in_specs=[vm], out_specs=[vm, vm],
        out_shape=[jax.ShapeDtypeStruct((8 * m, n), f32), jax.ShapeDtypeStruct((m, n), f32)],
        scratch_shapes=[pltpu.SemaphoreType.DMA((7,)), pltpu.SemaphoreType.DMA((7,)), pltpu.SemaphoreType.DMA],
    )(blk)


def _copy(src, dst, sems, idx, to):
    return pltpu.make_async_remote_copy(src_ref=src, dst_ref=dst, send_sem=sems[0].at[idx], recv_sem=sems[1].at[idx],
                                        device_id=to, device_id_type=MESH)


def _gather_ici(shards):
    nt = len(shards)

    def copies(ins, outs, sems):
        x, y, c, chip, others = _mesh_pos()
        send, land = [], []
        for t in range(nt):
            hr = ins[t].shape[0] // 2
            for j, (px, py) in enumerate(others):
                send.append((ins[t].at[pl.ds(c * hr, hr)], outs[t].at[chip, pl.ds(c * hr, hr)], sems, (t, j), (px, py, c)))
                piece = outs[t].at[2 * px + py, pl.ds(c * hr, hr)]
                land.append((piece, piece, sems, (t, j), (px, py, c)))
        return send, land

    def start(ins, outs, sems):
        for args in copies(ins, outs, sems)[0]:
            _copy(*args).start()

    def finish(ins, outs, sems):
        send, land = copies(ins, outs, sems)
        for args in land:
            _copy(*args).wait_recv()
        for args in send:
            _copy(*args).wait_send()

    return dict(ins=list(shards), outs=[jax.ShapeDtypeStruct((N_CHIPS,) + s.shape, s.dtype) for s in shards],
                sems=[pltpu.SemaphoreType.DMA((nt, 3)), pltpu.SemaphoreType.DMA((nt, 3))], start=start, finish=finish)


def _gather_d2d(stacks):
    nt = len(stacks)

    def copies(ins, outs, sems):
        x, y, c, _, others = _mesh_pos()
        send, land = [], []
        for t in range(nt):
            hr = outs[t].shape[1] // 2
            for j, (px, py) in enumerate(others):
                mine = outs[t].at[2 * px + py, pl.ds(c * hr, hr)]
                theirs = outs[t].at[2 * px + py, pl.ds((1 - c) * hr, hr)]
                send.append((mine, mine, sems, (t, j), (x, y, 1 - c)))
                land.append((theirs, theirs, sems, (t, j), (x, y, 1 - c)))
        return send, land

    def start(ins, outs, sems):
        for args in copies(ins, outs, sems)[0]:
            _copy(*args).start()

    def finish(ins, outs, sems):
        send, land = copies(ins, outs, sems)
        for args in land:
            _copy(*args).wait_recv()
        for args in send:
            _copy(*args).wait_send()

    return dict(ins=list(stacks), outs=[jax.ShapeDtypeStruct(s.shape, s.dtype) for s in stacks],
                sems=[pltpu.SemaphoreType.DMA((nt, 3)), pltpu.SemaphoreType.DMA((nt, 3))], start=start, finish=finish,
                aliases={t: t for t in range(nt)})


def _run_exchange(comm, name):
    ni, no = len(comm["ins"]), len(comm["outs"])

    def body(*refs):
        args = (refs[:ni], refs[ni:ni + no], refs[ni + no:])
        comm["start"](*args)
        comm["finish"](*args)

    return pl.pallas_call(
        body, name=name, in_specs=[ANY] * ni, out_specs=[ANY] * no, out_shape=comm["outs"], scratch_shapes=comm["sems"],
        input_output_aliases=dict(comm.get("aliases", {})))(*comm["ins"])


def _swap_halves(g4s, name):
    nt = len(g4s)

    def body(*refs):
        ins, outs = refs[:nt], refs[nt:2 * nt]
        send_sems, recv_sems = refs[2 * nt:]
        x, y, c, _, _ = _mesh_pos()
        cps = []
        for t in range(nt):
            hr = ins[t].shape[1] // 2
            cp = pltpu.make_async_remote_copy(
                src_ref=ins[t].at[:, pl.ds((1 - c) * hr, hr)], dst_ref=outs[t], send_sem=send_sems.at[t],
                recv_sem=recv_sems.at[t], device_id=(x, y, 1 - c), device_id_type=MESH)
            cp.start()
            cps.append(cp)
        for cp in cps:
            cp.wait()

    return pl.pallas_call(
        body, name=name, in_specs=[ANY] * nt, out_specs=[ANY] * nt,
        out_shape=[jax.ShapeDtypeStruct((N_CHIPS, g.shape[1] // 2, g.shape[2]), g.dtype) for g in g4s],
        scratch_shapes=[pltpu.SemaphoreType.DMA((nt,)), pltpu.SemaphoreType.DMA((nt,))],
    )(*g4s)


def _exchange_chips(h4s):
    nt = len(h4s)

    def copies(ins, outs, sems):
        x, y, c, chip, others = _mesh_pos()
        send, land = [], []
        for t in range(nt):
            for j, (px, py) in enumerate(others):
                send.append((ins[t].at[2 * px + py], outs[t].at[chip], sems, (t, j), (px, py, c)))
                landed = outs[t].at[2 * px + py]
                land.append((landed, landed, sems, (t, j), (px, py, c)))
        return send, land

    def start(ins, outs, sems):
        for args in copies(ins, outs, sems)[0]:
            _copy(*args).start()

    def finish(ins, outs, sems):
        send, land = copies(ins, outs, sems)
        for args in land:
            _copy(*args).wait_recv()
        for args in send:
            _copy(*args).wait_send()

    return dict(ins=list(h4s), outs=[jax.ShapeDtypeStruct(h.shape, h.dtype) for h in h4s],
                sems=[pltpu.SemaphoreType.DMA((nt, 3)), pltpu.SemaphoreType.DMA((nt, 3))], start=start, finish=finish)


def _join_halves(fs):
    nt = len(fs)

    def copies(ins, outs, sems):
        x, y, c, _, _ = _mesh_pos()
        send, land = [], []
        for t in range(nt):
            hr = outs[t].shape[0] // 2
            mine, theirs = outs[t].at[pl.ds(c * hr, hr)], outs[t].at[pl.ds((1 - c) * hr, hr)]
            send.append((mine, mine, sems, t, (x, y, 1 - c)))
            land.append((theirs, theirs, sems, t, (x, y, 1 - c)))
        return send, land

    def start(ins, outs, sems):
        for args in copies(ins, outs, sems)[0]:
            _copy(*args).start()

    def finish(ins, outs, sems):
        send, land = copies(ins, outs, sems)
        for args in land:
            _copy(*args).wait_recv()
        for args in send:
            _copy(*args).wait_send()

    return dict(ins=list(fs), outs=[jax.ShapeDtypeStruct(f.shape, f.dtype) for f in fs],
                sems=[pltpu.SemaphoreType.DMA((nt,)), pltpu.SemaphoreType.DMA((nt,))], start=start, finish=finish,
                aliases={t: t for t in range(nt)})


def _pad_lanes(v, n=LANES):
    return jnp.pad(v, ((0, 0), (0, n - v.shape[-1])))


def _mlp_fwd(xin, norm_g, w_up, w_down, F, tag, comms=(None, None)):
    T, D = xin.shape
    h = _rms_fwd(xin, norm_g, bf16, f"{tag}_norm")

    def relu_sq(acc):
        r = jnp.maximum(acc, 0.0)
        return r, r * r

    got = [None, None]
    ua = _mm(h, w_up[0], "nn", T, F, D, (bf16, bf16), f"{tag}_up", epilogue=relu_sq, b_view=w_up[1], comm=comms[0])
    (u, a), got[0] = ua if comms[0] is not None else (ua, None)
    out = _mm(a, w_down[0], "nn", T, D, F, (f32,), f"{tag}_down", epilogue=lambda acc, res: (res + acc,),
              extras=((xin, "tile"),), b_view=w_down[1], comm=comms[1])
    (out,), got[1] = out if comms[1] is not None else ((out,), None)
    return out, (xin, h, u, a), got


def _mlp_bwd(dy, saved, norm_g, w_up, w_down, F, tag, up_to=None, down_to=None):
    xin, h, u, a = saved
    T, D = xin.shape
    to = lambda t: {} if t is None else dict(out_view=t[0], out_stack=t[1], alias=t[2])
    du = _mm(dy, w_down[0], "nt", T, F, D, (bf16,), f"{tag}_dact", epilogue=lambda acc, uu: (acc * (2.0 * uu.astype(f32)),),
             extras=((u, "tile"),), b_view=w_down[1])
    dw_down = _mm(a, dy, "tn", F, D, T, (f32,), f"{tag}_dwdown", **to(down_to))
    dh = _mm(du, w_up[0], "nt", T, D, F, (f32,), f"{tag}_dh", b_view=w_up[1])
    dw_up = _mm(h, du, "tn", D, F, T, (f32,), f"{tag}_dwup", **to(up_to))
    dx, dg = _rms_bwd(xin, norm_g, dh, dy, f"{tag}_dnorm")
    return dx, dg, dw_up, dw_down


def _local_step(xc, tgt, W, HS, SBW, net=None):
    T, D = xc.shape
    SW = HS * SSD_HEAD_DIM
    CD = W["conv_b"].shape[-1]
    mlp_norm = W["mlp_norm"]
    add = lambda acc, prev: (prev + acc,)

    h0 = _rms_fwd(xc, W["hyb_norm"], bf16, "hyb_norm")
    z = _mm(h0, W["w_z"], "nn", T, SW, D, (f32,), "proj_z")
    xraw = _mm(h0, W["w_xbc"], "nn", T, CD, D, (f32,), "proj_xbc")
    dtraw = _mm(h0, W["w_dt"], "nn", T, LANES, D, (f32,), "proj_dt")
    qkv = _mm(h0, W["w_qkv"], "nn", T, 3 * SBW, D, (f32,), "proj_qkv")
    qn, kn, vb = _qk_norm_fwd(qkv, W["q_norm"], W["k_norm"], SBW, "qk_norm")
    y_sb, ctot, got = _sb_fwd(qn, kn, vb, "sb_attn", comm=net.rest_ici() if net else None)
    xbc = _conv_fwd(xraw, W["conv_w"], W["conv_b"], "conv")
    y_ssd, yn_ssd, sprev, got = _ssd_fwd(xbc, dtraw, z, W["dt_bias"], W["a_log"], W["d_skip"], W["out_norm"], HS, "ssd",
                                         comm=net.rest_d2d(got) if net else None)
    if net:
        W = {**W, **net.rest_weights(got)}
    w_up, w_down, F = W["w_up"], W["w_down"], W["F"]
    mix = _mm(yn_ssd, W["w_out"], "nn", T, D, SW, (f32,), "out_ssd", epilogue=add, extras=((xc, "tile"),))
    x1 = _mm(y_sb, W["w_out"], "nn", T, D, SBW, (f32,), "out_sb", epilogue=add, extras=((mix, "tile"),), b_off=(SW, 0))
    x2, mlp0, got = _mlp_fwd(x1, mlp_norm[0:1], w_up[0], w_down[0], F, "mlp0",
                             comms=(net.last_ici(0), net.last_ici(1)) if net else (None, None))
    if net:
        w_up, w_down = net.last_weights(got, w_up, w_down)

    hp = _rms_fwd(x2, W["pool_norm"], f32, "pool_norm")
    x3, yp, dpool = _pool_fwd(hp, x2, W["w_pool"], W["pool_b"], W["pool_scale"], "pool")
    x4, mlp1, _ = _mlp_fwd(x3, mlp_norm[1:2], w_up[1], w_down[1], F, "mlp1")

    dy, sq = _loss_grad(x4, tgt, "loss")

    up_to, down_to = (net.mlp_to("up", 1, None), net.mlp_to("down", 1, None)) if net else (None, None)
    dx3, dg_mlp1, dw_up1, dw_down1 = _mlp_bwd(dy, mlp1, mlp_norm[1:2], w_up[1], w_down[1], F, "mlp1", up_to, down_to)
    dhp, dw_pool, db_pool, dsc_pool = _pool_bwd(dx3, yp, dpool, W["w_pool"], W["pool_scale"], "pool_bwd")
    dx2, dg_pool = _rms_bwd(x2, W["pool_norm"], dhp, dx3, "pool_dnorm")
    up_to, down_to = (net.mlp_to("up", 0, dw_up1), net.mlp_to("down", 0, dw_down1)) if net else (None, None)
    dx1, dg_mlp0, dw_up0, dw_down0 = _mlp_bwd(dx2, mlp0, mlp_norm[0:1], w_up[0], w_down[0], F, "mlp0", up_to, down_to)

    dmerged = _mm(dx1, W["w_out"], "nt", T, SW + SBW, D, (f32,), "dmerged")
    dw_out = jnp.concatenate([_mm(yn_ssd, dx1, "tn", SW, D, T, (f32,), "dwout_ssd"),
                              _mm(y_sb, dx1, "tn", SBW, D, T, (f32,), "dwout_sb")], axis=0)
    dqn, dkn, dvv, got = _sb_bwd(qn, kn, vb, dmerged, ctot, SW, "sb_attn_bwd",
                                 comm=net.reduce_early(dw_out, dw_pool, dw_up0, dw_down0) if net else None)
    if net:
        net.reduce_early_done(got)
    dqkv, dg_q, dg_k = _qk_norm_bwd(qkv, dqn, dkn, dvv, W["q_norm"], W["k_norm"], SBW, "qk_norm_bwd")
    dy_ssd, dz, dg_on = _gate_bwd(y_ssd, z, dmerged, W["out_norm"], "gate_bwd")
    dxbc, ddtraw, dalog, dbias, ddskip = _ssd_bwd(xbc, dtraw, sprev, dy_ssd, W["dt_bias"], W["a_log"], W["d_skip"], HS, "ssd_bwd")
    dpre, dconv_w, dconv_b = _conv_bwd_pre(xraw, dxbc, W["conv_w"], W["conv_b"], "conv_bwd_pre")
    dxraw = _conv_bwd_in(dpre, W["conv_w"], "conv_bwd_in")
    dw_in = jnp.concatenate([
        _mm(h0, dz, "tn", D, SW, T, (f32,), "dwin_z"), _mm(h0, dxraw, "tn", D, CD, T, (f32,), "dwin_xbc"),
        _mm(h0, ddtraw, "tn", D, LANES, T, (f32,), "dwin_dt")[:, :HS], _mm(h0, dqkv, "tn", D, 3 * SBW, T, (f32,), "dwin_qkv")], axis=1)
    dh0 = _mm(dz, W["w_z"], "nt", T, D, SW, (f32,), "dh0_z")
    dh0 = _mm(dxraw, W["w_xbc"], "nt", T, D, CD, (f32,), "dh0_xbc", epilogue=add, extras=((dh0, "tile"),))
    dh0 = _mm(ddtraw, W["w_dt"], "nt", T, D, LANES, (f32,), "dh0_dt", epilogue=add, extras=((dh0, "tile"),))
    if net:
        (dh0,), got = _mm(dqkv, W["w_qkv"], "nt", T, D, 3 * SBW, (f32,), "dh0_qkv", epilogue=add, extras=((dh0, "tile"),),
                          comm=net.reduce_late(dw_in))
        net.reduce_late_done(got)
    else:
        dh0 = _mm(dqkv, W["w_qkv"], "nt", T, D, 3 * SBW, (f32,), "dh0_qkv", epilogue=add, extras=((dh0, "tile"),))
    grad_x, dg_hyb = _rms_bwd(xc, W["hyb_norm"], dh0, dx1, "hyb_dnorm")
    grads = dict(w_in=dw_in, w_out=dw_out, w_pool=dw_pool, w_up=(dw_up0, dw_up1), w_down=(dw_down0, dw_down1),
                 hyb_norm=dg_hyb, conv_w=dconv_w, conv_b=dconv_b, dt_bias=dbias, a_log=dalog, d_skip=ddskip, out_norm=dg_on,
                 q_norm=dg_q, k_norm=dg_k, mlp_norm=(dg_mlp0, dg_mlp1), pool_norm=dg_pool, pool_b=db_pool, pool_scale=dsc_pool)
    return sq, grad_x, grads


class _Net:
    def __init__(self, own_first, own_last, chip, dims):
        self.own, self.own_last, self.chip, self.dims = own_first, own_last, chip, dims

    def _place_own(self, stacks, own):
        return [lax.dynamic_update_index_in_dim(g, o, self.chip, 0) for g, o in zip(stacks, own)]

    def rest_ici(self):
        return _gather_ici(self.own)

    def rest_d2d(self, got):
        return _gather_d2d(list(got))

    def rest_weights(self, got):
        d, nw = self.dims, len(POOL_WINDOWS)
        D, F, PG = d["D"], d["F"], d["PG"]
        fs = F // N_CHIPS
        g_out, g_pool, g_up, g_down = self._place_own(got, self.own)
        w_pool = g_pool.reshape(N_CHIPS, nw, PG // N_CHIPS, PG).transpose(1, 0, 2, 3).reshape(nw, PG, PG)
        return dict(w_out=g_out.reshape(d["MIX"], D), w_pool=w_pool, F=F,
                    w_up=[(g_up, ("cols", fs, 0, D))], w_down=[(g_down, ("rows", fs, 0, None))])

    def last_ici(self, which):
        return _gather_ici([self.own_last[which]])

    def last_weights(self, got, w_up, w_down):
        d = self.dims
        fs = d["F"] // N_CHIPS
        stacks = _run_exchange(_gather_d2d([got[0][0], got[1][0]]), "gather_last_d2d")
        g_up, g_down = self._place_own(stacks, self.own_last)
        return w_up + [(g_up, ("cols", fs, 0, d["D"]))], w_down + [(g_down, ("rows", fs, 0, None))]

    def mlp_to(self, which, layer, earlier):
        d = self.dims
        fs = d["F"] // N_CHIPS
        if which == "up":
            return ("cols", fs, layer, d["D"]), (N_CHIPS, d["NL"] * d["D"], fs), earlier
        return ("rows", fs, layer, None), (N_CHIPS, d["NL"] * fs, d["D"]), earlier

    def _pairs(self, g4s, tag):
        recv = _swap_halves(g4s, f"{tag}_pair_swap")
        sums = [_pair_sum(g, r, f"{tag}_pair_sum{i}") for i, (g, r) in enumerate(zip(g4s, recv))]
        return [s[0] for s in sums], [s[1] for s in sums]

    def reduce_early(self, dw_out, dw_pool, g_up, g_down):
        d, nw = self.dims, len(POOL_WINDOWS)
        PG = d["PG"]
        g4 = [dw_out.reshape(N_CHIPS, d["MIX"] // N_CHIPS, d["D"]),
              dw_pool.reshape(nw, N_CHIPS, PG // N_CHIPS, PG).transpose(1, 0, 2, 3).reshape(N_CHIPS, PG, PG), g_up, g_down]
        self.early_own, sent = self._pairs(g4, "grads_early")
        return _exchange_chips(sent)

    def reduce_early_done(self, got):
        self.early_got = list(got)

    def reduce_late(self, dw_in):
        d = self.dims
        g4 = [dw_in.reshape(d["D"], N_CHIPS, d["IN"] // N_CHIPS).transpose(1, 0, 2)]
        self.late_own, sent = self._pairs(g4, "grads_late")
        return _exchange_chips(sent)

    def reduce_late_done(self, got):
        self.late_got = list(got)

    def reduced(self):
        halves = [_chip_sum(h, r, f"grads_chip_sum{i}")
                  for i, (h, r) in enumerate(zip(self.late_own + self.early_own, self.late_got + self.early_got))]
        return _run_exchange(_join_halves(halves), "grads_join")


def kernel(x, hyb_norm, hyb_w_in, ssd_conv_w, ssd_conv_b, ssd_dt_bias, ssd_a_log, ssd_d, ssd_out_norm, sb_q_norm, sb_k_norm, hyb_w_out, pool_norm, pool_w, pool_b, pool_scale, mlp_norm, mlp_w_up, mlp_w_down, loss_target, m_hyb_norm, m_hyb_w_in, m_ssd_conv_w, m_ssd_conv_b, m_ssd_dt_bias, m_ssd_a_log, m_ssd_d, m_ssd_out_norm, m_sb_q_norm, m_sb_k_norm, m_hyb_w_out, m_pool_norm, m_pool_w, m_pool_b, m_pool_scale, m_mlp_norm, m_mlp_w_up, m_mlp_w_down, v_hyb_norm, v_hyb_w_in, v_ssd_conv_w, v_ssd_conv_b, v_ssd_dt_bias, v_ssd_a_log, v_ssd_d, v_ssd_out_norm, v_sb_q_norm, v_sb_k_norm, v_hyb_w_out, v_pool_norm, v_pool_w, v_pool_b, v_pool_scale, v_mlp_norm, v_mlp_w_up, v_mlp_w_down):
    T, D = x.shape[1], x.shape[2]
    HS = ssd_dt_bias.shape[-1]
    SW = HS * SSD_HEAD_DIM
    CD = ssd_conv_b.shape[-1]
    IN = N_CHIPS * hyb_w_in.shape[-1]
    SBW = (IN - SW - CD - HS) // 3
    F = N_CHIPS * mlp_w_up.shape[-1]
    NL = mlp_norm.shape[0]
    PG = D // len(POOL_WINDOWS)
    xc, tgt = x[0], loss_target[0]
    ix, iy, ic = lax.axis_index("x"), lax.axis_index("y"), lax.axis_index("c")
    chip = (2 * ix + iy).astype(jnp.int32)

    small = jnp.concatenate([ssd_conv_w.reshape(-1), pool_norm.reshape(-1), pool_b.reshape(-1), pool_scale.reshape(-1)])
    ns = small.shape[0]
    ns8 = -(-ns // (8 * LANES)) * LANES
    gathered, _ = _gather_small(jnp.pad(small, (0, 8 * ns8 - ns)).reshape(8, ns8), "gather_small")
    per_chip = gathered.reshape(N_CHIPS, 2, 8 * ns8)[:, 0, :ns]
    cw = CD // N_CHIPS
    conv_w = per_chip[:, :4 * cw].reshape(N_CHIPS, 4, cw).transpose(1, 0, 2).reshape(4, CD)
    pvec = per_chip[:, 4 * cw:].reshape(N_CHIPS, 3, PG)
    pool_norm_f, pool_b_f, pool_scale_f = (pvec[:, i].reshape(1, D) for i in range(3))

    fs = F // N_CHIPS
    own_in = hyb_w_in[0].astype(bf16)
    g_in = _run_exchange(_gather_d2d(_run_exchange(_gather_ici([own_in]), "gather_in_ici")), "gather_in_d2d")[0]
    w_in = lax.dynamic_update_index_in_dim(g_in, own_in, chip, 0).transpose(1, 0, 2).reshape(D, IN)
    c1, c2, c3 = SW, SW + CD, SW + CD + HS
    w_z, w_xbc, w_dt, w_qkv = w_in[:, :c1], w_in[:, c1:c2], _pad_lanes(w_in[:, c2:c3]), w_in[:, c3:]
    dt_bias_p, a_log_p, d_skip_p = _pad_lanes(ssd_dt_bias), _pad_lanes(ssd_a_log), jnp.repeat(ssd_d, SSD_HEAD_DIM, axis=-1)

    assert NL == 2
    own_first = [hyb_w_out[0].astype(bf16), pool_w[0].reshape(-1, PG).astype(bf16),
                 mlp_w_up[0].astype(bf16), mlp_w_down[0].astype(bf16)]
    own_last = [mlp_w_up[1].astype(bf16), mlp_w_down[1].astype(bf16)]
    net = _Net(own_first, own_last, chip, dict(D=D, F=F, NL=NL, PG=PG, IN=IN, MIX=SW + SBW))
    first = dict(hyb_norm=hyb_norm, w_z=w_z, w_xbc=w_xbc, w_dt=w_dt, w_qkv=w_qkv, conv_w=conv_w, conv_b=ssd_conv_b,
                 dt_bias=dt_bias_p, a_log=a_log_p, d_skip=d_skip_p, out_norm=ssd_out_norm, q_norm=sb_q_norm, k_norm=sb_k_norm,
                 pool_norm=pool_norm_f, pool_b=pool_b_f, pool_scale=pool_scale_f, mlp_norm=mlp_norm)
    sq, grad_x, gr = _local_step(xc, tgt, first, HS, SBW, net)
    loss = lax.psum(sq[0, 0] * (0.5 / D), ("x", "y", "c"))
    dg_hyb, dconv_b, dbias, dalog, ddskip, dg_on, dg_q, dg_k = (gr[k] for k in (
        "hyb_norm", "conv_b", "dt_bias", "a_log", "d_skip", "out_norm", "q_norm", "k_norm"))
    (dg_mlp0, dg_mlp1), dconv_w, dg_pool, db_pool, dsc_pool = gr["mlp_norm"], gr["conv_w"], gr["pool_norm"], gr["pool_b"], gr["pool_scale"]
    gb_in, gb_out, gb_pool, gb_up, gb_down = net.reduced()

    full_small = [dg_hyb, dconv_b, dbias[:, :HS], dalog[:, :HS], ddskip[:, :HS], dg_on, dg_q, dg_k,
                  jnp.concatenate([dg_mlp0, dg_mlp1], axis=0).reshape(1, -1),
                  dconv_w.reshape(1, -1), dg_pool, db_pool, dsc_pool]
    sizes = [v.shape[-1] for v in full_small]
    packed = jnp.concatenate([v.reshape(-1) for v in full_small])
    npk = packed.shape[0]
    npk8 = -(-npk // (8 * LANES)) * LANES
    _, summed = _gather_small(jnp.pad(packed, (0, 8 * npk8 - npk)).reshape(8, npk8), "grads_small")
    summed = summed.reshape(-1)[:npk]
    offs = [0]
    for s in sizes:
        offs.append(offs[-1] + s)
    (g_hyb_norm, g_conv_b, g_dt_bias, g_a_log, g_d, g_out_norm, g_q_norm, g_k_norm, g_mlp_norm, g_conv_w_full,
     g_pool_norm_full, g_pool_b_full, g_pool_scale_full) = (summed[offs[i]:offs[i + 1]] for i in range(len(sizes)))
    take = lambda full, n: lax.dynamic_slice_in_dim(full.reshape(-1, N_CHIPS, n), chip, 1, axis=1)
    small_grads = {
        "hyb_norm": g_hyb_norm.reshape(hyb_norm.shape), "ssd_conv_w": take(g_conv_w_full, cw).reshape(ssd_conv_w.shape),
        "ssd_conv_b": g_conv_b.reshape(ssd_conv_b.shape), "ssd_dt_bias": g_dt_bias.reshape(ssd_dt_bias.shape),
        "ssd_a_log": g_a_log.reshape(ssd_a_log.shape), "ssd_d": g_d.reshape(ssd_d.shape),
        "ssd_out_norm": g_out_norm.reshape(ssd_out_norm.shape), "sb_q_norm": g_q_norm.reshape(sb_q_norm.shape),
        "sb_k_norm": g_k_norm.reshape(sb_k_norm.shape), "pool_norm": take(g_pool_norm_full, PG).reshape(pool_norm.shape),
        "pool_b": take(g_pool_b_full, PG).reshape(pool_b.shape), "pool_scale": take(g_pool_scale_full, PG).reshape(pool_scale.shape),
        "mlp_norm": g_mlp_norm.reshape(mlp_norm.shape),
    }

    weights = dict(hyb_norm=hyb_norm, hyb_w_in=hyb_w_in, ssd_conv_w=ssd_conv_w, ssd_conv_b=ssd_conv_b, ssd_dt_bias=ssd_dt_bias,
                   ssd_a_log=ssd_a_log, ssd_d=ssd_d, ssd_out_norm=ssd_out_norm, sb_q_norm=sb_q_norm, sb_k_norm=sb_k_norm,
                   hyb_w_out=hyb_w_out, pool_norm=pool_norm, pool_w=pool_w, pool_b=pool_b, pool_scale=pool_scale,
                   mlp_norm=mlp_norm, mlp_w_up=mlp_w_up, mlp_w_down=mlp_w_down)
    moms = dict(hyb_norm=m_hyb_norm, hyb_w_in=m_hyb_w_in, ssd_conv_w=m_ssd_conv_w, ssd_conv_b=m_ssd_conv_b, ssd_dt_bias=m_ssd_dt_bias,
                ssd_a_log=m_ssd_a_log, ssd_d=m_ssd_d, ssd_out_norm=m_ssd_out_norm, sb_q_norm=m_sb_q_norm, sb_k_norm=m_sb_k_norm,
                hyb_w_out=m_hyb_w_out, pool_norm=m_pool_norm, pool_w=m_pool_w, pool_b=m_pool_b, pool_scale=m_pool_scale,
                mlp_norm=m_mlp_norm, mlp_w_up=m_mlp_w_up, mlp_w_down=m_mlp_w_down)
    vels = dict(hyb_norm=v_hyb_norm, hyb_w_in=v_hyb_w_in, ssd_conv_w=v_ssd_conv_w, ssd_conv_b=v_ssd_conv_b, ssd_dt_bias=v_ssd_dt_bias,
                ssd_a_log=v_ssd_a_log, ssd_d=v_ssd_d, ssd_out_norm=v_ssd_out_norm, sb_q_norm=v_sb_q_norm, sb_k_norm=v_sb_k_norm,
                hyb_w_out=v_hyb_w_out, pool_norm=v_pool_norm, pool_w=v_pool_w, pool_b=v_pool_b, pool_scale=v_pool_scale,
                mlp_norm=v_mlp_norm, mlp_w_up=v_mlp_w_up, mlp_w_down=v_mlp_w_down)
    order = list(weights)
    grads, delta, new_m, new_v = {}, {}, {}, {}
    for name, g2 in (("hyb_w_in", gb_in), ("hyb_w_out", gb_out), ("pool_w", gb_pool), ("mlp_w_up", gb_up), ("mlp_w_down", gb_down)):
        shp = weights[name].shape
        d_, m_, v_ = _adamw(weights[name].reshape(g2.shape), g2, moms[name].reshape(g2.shape), vels[name].reshape(g2.shape),
                            f"adamw_{name}")
        grads[name], delta[name], new_m[name], new_v[name] = (t.reshape(shp) for t in (g2, d_, m_, v_))
    snames = list(small_grads)
    pack = lambda d: jnp.concatenate([d[n].reshape(-1) for n in snames])
    nsm = sum(small_grads[n].size for n in snames)
    cols = -(-nsm // (8 * LANES)) * LANES
    as_blk = lambda v: jnp.pad(v, (0, 8 * cols - nsm)).reshape(8, cols)
    padded_v = jnp.pad(pack(vels), (0, 8 * cols - nsm), constant_values=1.0).reshape(8, cols)
    d_, m_, v_ = _adamw(as_blk(pack(weights)), as_blk(pack(small_grads)), as_blk(pack(moms)), padded_v, "adamw_small")
    off = 0
    for n in snames:
        sz, shp = small_grads[n].size, weights[n].shape
        grads[n] = small_grads[n]
        delta[n], new_m[n], new_v[n] = (t.reshape(-1)[off:off + sz].reshape(shp) for t in (d_, m_, v_))
        off += sz

    return (loss, grad_x.reshape(x.shape), *[grads[n] for n in order], *[delta[n] for n in order],
            *[new_m[n] for n in order], *[new_v[n] for n in order])
```

```python
import functools
import math

import jax
import jax.numpy as jnp
from jax import lax
from jax.experimental import pallas as pl
from jax.experimental.pallas import tpu as pltpu

f32 = jnp.float32
bf16 = jnp.bfloat16

EPS = 1e-6
SSD_HEAD_DIM = 64
SSD_STATE = 128
SSD_GROUPS = 4
SSD_CHUNK = 128
LANES = 128
SB_HEAD_DIM = 128
POOL_WINDOWS = (2, 4, 8, 16)
POOL_HALO = 16
CONV_HALO = 8
ADAM_LR, ADAM_B1, ADAM_B2, ADAM_EPS, ADAM_WD, ADAM_STEP = 0.001, 0.9, 0.999, 1e-08, 0.01, 10
VMEM_LIMIT = 56 * 1024 * 1024
MM_TILE_BUDGET = 40 * 1024 * 1024
N_CHIPS = 4
MESH = pl.DeviceIdType.MESH

_DIMS = {"nn": (((1,), (0,)), ((), ())), "nt": (((1,), (1,)), ((), ())), "tn": (((0,), (0,)), ((), ()))}


def _fit(n, t):
    if n <= t:
        return n
    return max(d for d in range(LANES, t + 1, LANES) if n % d == 0)


def _cp(*sem):
    return pltpu.CompilerParams(dimension_semantics=sem, vmem_limit_bytes=VMEM_LIMIT)


def _sigmoid(v):
    return 1.0 / (1.0 + jnp.exp(-v))


def _softplus(v):
    return jnp.maximum(v, 0.0) + jnp.log(1.0 + jnp.exp(-jnp.abs(v)))


def _split(v, parts):
    out, rem = [], v
    for _ in range(parts):
        p = rem.astype(bf16)
        out.append(p)
        rem = rem - p.astype(f32)
    return out


def _dot(a, b, mode="nn"):
    return lax.dot_general(a, b, _DIMS[mode], preferred_element_type=f32)


def _mask_dot(mask_b, v, parts):
    return _dot(jnp.concatenate([mask_b] * parts, axis=1), jnp.concatenate(_split(v, parts), axis=0))


def _dot_mask(v, mask_b, parts):
    return _dot(jnp.concatenate(_split(v, parts), axis=1), jnp.concatenate([mask_b] * parts, axis=0))


def _stacked(view, br, bc, rmap, cmap):
    kind, per, layer, rows_per_layer = view
    if kind == "cols":
        npc = per // bc
        return pl.BlockSpec((None, br, bc), lambda i, j, k: (cmap(i, j, k) // npc, layer * (rows_per_layer // br) + rmap(i, j, k),
                                                              cmap(i, j, k) % npc))
    npc = per // br
    return pl.BlockSpec((None, br, bc), lambda i, j, k: (rmap(i, j, k) // npc, layer * npc + rmap(i, j, k) % npc, cmap(i, j, k)))


def _pick_tiles(M, N, K, caps, a_bytes, b_bytes, io_bytes):
    def cands(n, cap, sizes):
        got = [s for s in sizes if s <= min(n, cap) and n % s == 0]
        return got or [_fit(n, min(n, cap))]

    best = None
    for tk in cands(K, caps[2], (8192, 4096, 2048, 1024, 512, 256, 128)):
        for tm in cands(M, caps[0], (1024, 512, 256, 128)):
            for tn in cands(N, caps[1], (1024, 512, 256, 128)):
                need = 2 * (tm * tk * a_bytes + tk * tn * b_bytes) + tm * tn * (2 * io_bytes + (4 if tk < K else 0))
                key = (need <= MM_TILE_BUDGET, tk, tm * tn, tm)
                if best is None or key > best[0]:
                    best = (key, (tm, tn, tk))
    return best[1]


def _mm(a, b, mode, M, N, K, outs, name, epilogue=None, extras=(), a_off=(0, 0), b_off=(0, 0),
        b_view=None, out_view=None, out_stack=None, alias=None, comm=None):
    caps = [M, N, K]
    if b_view is not None:
        caps[1 if (b_view[0] == "cols") == (mode != "nt") else 2] = b_view[1]
    if out_view is not None:
        d = 1 if out_view[0] == "cols" else 0
        caps[d] = min(caps[d], out_view[1])
    for off, dims in ((a_off, (2, 0) if mode == "tn" else (0, 2)), (b_off, (1, 2) if mode == "nt" else (2, 1))):
        for o, d in zip(off, dims):
            if o:
                caps[d] = min(caps[d], math.gcd(o, caps[d]))
    io_bytes = sum(jnp.dtype(dt).itemsize for dt in outs) + sum(e[0].dtype.itemsize for e in extras if e[1] == "tile")
    tm, tn, tk = _pick_tiles(M, N, K, caps, a.dtype.itemsize, b.dtype.itemsize, io_bytes)
    nk = K // tk
    if mode == "tn":
        a_blk, ad = (tk, tm), (tk, tm)
    else:
        a_blk, ad = (tm, tk), (tm, tk)
    b_blk = (tn, tk) if mode == "nt" else (tk, tn)
    assert a_off[0] % ad[0] == 0 and a_off[1] % ad[1] == 0 and b_off[0] % b_blk[0] == 0 and b_off[1] % b_blk[1] == 0
    ao = (a_off[0] // ad[0], a_off[1] // ad[1])
    bo = (b_off[0] // b_blk[0], b_off[1] // b_blk[1])
    if mode == "tn":
        a_map = lambda i, j, k: (k + ao[0], i + ao[1])
    else:
        a_map = lambda i, j, k: (i + ao[0], k + ao[1])
    if mode == "nt":
        b_map = lambda i, j, k: (j + bo[0], k + bo[1])
    else:
        b_map = lambda i, j, k: (k + bo[0], j + bo[1])
    if b_view is not None:
        if mode == "nt":
            b_spec = _stacked(b_view, tn, tk, lambda i, j, k: j, lambda i, j, k: k)
        else:
            b_spec = _stacked(b_view, tk, tn, lambda i, j, k: k, lambda i, j, k: j)
    else:
        b_spec = pl.BlockSpec(b_blk, b_map)
    in_specs = [pl.BlockSpec(a_blk, a_map), b_spec]
    for arr, kind in extras:
        if kind == "tile":
            in_specs.append(pl.BlockSpec((tm, tn), lambda i, j, k: (i, j)))
        else:
            in_specs.append(pl.BlockSpec((1, tn), lambda i, j, k: (0, j)))
    ne, no = len(extras), len(outs)
    if epilogue is None:
        epilogue = lambda acc: (acc,)
    operands = [a, b, *[e[0] for e in extras]]
    aliases = {}
    if alias is not None:
        in_specs.append(ANY)
        aliases[len(operands)] = 0
        operands.append(alias)
    n_in = len(operands)
    if out_view is not None:
        out_specs = [_stacked(out_view, tm, tn, lambda i, j, k: i, lambda i, j, k: j)]
        out_shape = [jax.ShapeDtypeStruct(out_stack, outs[0])]
    else:
        out_specs = [pl.BlockSpec((tm, tn), lambda i, j, k: (i, j)) for _ in outs]
        out_shape = [jax.ShapeDtypeStruct((M, N), dt) for dt in outs]
    scratch = [pltpu.VMEM((tm, tn), f32)] if nk > 1 else []
    grid = (M // tm, N // tn, nk)
    if comm is not None:
        in_specs += [ANY] * len(comm["ins"])
        operands += comm["ins"]
        out_specs += [ANY] * len(comm["outs"])
        out_shape += comm["outs"]
        scratch += comm["sems"]
    nci, nco, ncs = (len(comm["ins"]), len(comm["outs"]), len(comm["sems"])) if comm is not None else (0, 0, 0)

    def body(*refs):
        a_ref, b_ref = refs[0], refs[1]
        ex, out_refs = refs[2:2 + ne], refs[n_in + nci:n_in + nci + no]
        rest = refs[n_in + nci + no + nco:]
        if comm is not None:
            cargs = (refs[n_in:n_in + nci], refs[n_in + nci + no:n_in + nci + no + nco], refs[len(refs) - ncs:])
            pid = [pl.program_id(d) for d in range(3)]

            @pl.when((pid[0] == 0) & (pid[1] == 0) & (pid[2] == 0))
            def _():
                comm["start"](*cargs)

        def finish(acc):
            res = epilogue(acc, *[e[...] for e in ex])
            for o, r in zip(out_refs, res):
                o[...] = r.astype(o.dtype)

        prod = lax.dot_general(a_ref[...].astype(bf16), b_ref[...].astype(bf16), _DIMS[mode],
                               preferred_element_type=f32)
        if nk == 1:
            finish(prod)
        else:
            acc_ref = rest[0]
            k = pl.program_id(2)

            @pl.when(k == 0)
            def _():
                acc_ref[...] = prod

            @pl.when(k > 0)
            def _():
                acc_ref[...] += prod

            @pl.when(k == nk - 1)
            def _():
                finish(acc_ref[...])

        if comm is not None:
            @pl.when((pid[0] == grid[0] - 1) & (pid[1] == grid[1] - 1) & (pid[2] == grid[2] - 1))
            def _():
                comm["finish"](*cargs)

    sem = ("arbitrary",) * 3 if comm is not None else ("parallel", "parallel", "arbitrary")
    res = pl.pallas_call(
        body, name=name, grid=grid, in_specs=in_specs, out_specs=out_specs, out_shape=out_shape,
        scratch_shapes=scratch, input_output_aliases=aliases, compiler_params=_cp(*sem),
    )(*operands)
    if comm is not None:
        return res[:no], res[no:]
    return res[0] if no == 1 else res


def _rms_fwd(x, g, out_dtype, name, tr=256):
    T, D = x.shape
    tr = min(tr, T)

    def body(x_ref, g_ref, o_ref):
        xv = x_ref[...]
        r = lax.rsqrt(jnp.mean(xv * xv, axis=-1, keepdims=True) + EPS)
        o_ref[...] = (xv * r * g_ref[...]).astype(o_ref.dtype)

    return pl.pallas_call(
        body, name=name, grid=(T // tr,),
        in_specs=[pl.BlockSpec((tr, D), lambda r: (r, 0)), pl.BlockSpec((1, D), lambda r: (0, 0))],
        out_specs=pl.BlockSpec((tr, D), lambda r: (r, 0)),
        out_shape=jax.ShapeDtypeStruct((T, D), out_dtype), compiler_params=_cp("parallel"),
    )(x, g)


def _rms_bwd(x, g, dh, dres, name, tr=256):
    T, D = x.shape
    tr = min(tr, T)

    def body(x_ref, g_ref, dh_ref, dres_ref, dx_ref, dg_ref):
        xv = x_ref[...]
        r = lax.rsqrt(jnp.mean(xv * xv, axis=-1, keepdims=True) + EPS)
        xh = xv * r
        dhv = dh_ref[...]
        dhg = dhv * g_ref[...]
        dx_ref[...] = dres_ref[...] + r * (dhg - xh * jnp.mean(dhg * xh, axis=-1, keepdims=True))

        @pl.when(pl.program_id(0) == 0)
        def _():
            dg_ref[...] = jnp.zeros_like(dg_ref)

        dg_ref[...] += jnp.sum(dhv * xh, axis=0, keepdims=True)

    row = pl.BlockSpec((tr, D), lambda r: (r, 0))
    vec = pl.BlockSpec((1, D), lambda r: (0, 0))
    return pl.pallas_call(
        body, name=name, grid=(T // tr,), in_specs=[row, vec, row, row], out_specs=[row, vec],
        out_shape=[jax.ShapeDtypeStruct((T, D), f32), jax.ShapeDtypeStruct((1, D), f32)],
        compiler_params=_cp("arbitrary"),
    )(x, g, dh, dres)


def _loss_grad(y, tgt, name, tr=256):
    T, D = y.shape
    tr = min(tr, T)

    def body(y_ref, t_ref, dy_ref, s_ref):
        e = y_ref[...] - t_ref[...]
        dy_ref[...] = e * (1.0 / D)

        @pl.when(pl.program_id(0) == 0)
        def _():
            s_ref[...] = jnp.zeros_like(s_ref)

        s_ref[...] += jnp.sum(e * e)

    row = pl.BlockSpec((tr, D), lambda r: (r, 0))
    return pl.pallas_call(
        body, name=name, grid=(T // tr,), in_specs=[row, row],
        out_specs=[row, pl.BlockSpec((8, LANES), lambda r: (0, 0))],
        out_shape=[jax.ShapeDtypeStruct((T, D), f32), jax.ShapeDtypeStruct((8, LANES), f32)],
        compiler_params=_cp("arbitrary"),
    )(y, tgt)


def _shift_down(cur, prev, s):
    rolled = pltpu.roll(cur, s, 0)
    top = pltpu.roll(prev, s, 0)
    row = lax.broadcasted_iota(jnp.int32, top.shape, 0)
    head = jnp.where(row < s, top, rolled[0:CONV_HALO])
    return jnp.concatenate([head, rolled[CONV_HALO:]], axis=0)


def _shift_up(cur, nxt, s):
    n = cur.shape[0]
    rolled = pltpu.roll(cur, n - s, 0)
    bot = pltpu.roll(nxt, CONV_HALO - s, 0)
    row = lax.broadcasted_iota(jnp.int32, bot.shape, 0)
    tail = jnp.where(row >= CONV_HALO - s, bot, rolled[n - CONV_HALO:])
    return jnp.concatenate([rolled[:n - CONV_HALO], tail], axis=0)


def _conv_pre(cur, prev, w_ref, b_ref):
    taps = [cur] + [_shift_down(cur, prev, s) for s in (1, 2, 3)]
    pre = b_ref[...] + w_ref[3:4, :] * taps[0]
    for s in (1, 2, 3):
        pre = pre + w_ref[3 - s:4 - s, :] * taps[s]
    return pre, taps


def _conv_specs(T, C, rc, cb):
    cur = pl.BlockSpec((rc, cb), lambda j, r: (r, j))
    prev = pl.BlockSpec((CONV_HALO, cb), lambda j, r: (jnp.maximum(r * (rc // CONV_HALO) - 1, 0), j))
    nxt = pl.BlockSpec((CONV_HALO, cb), lambda j, r: (jnp.minimum((r + 1) * (rc // CONV_HALO), T // CONV_HALO - 1), j))
    w = pl.BlockSpec((4, cb), lambda j, r: (0, j))
    b = pl.BlockSpec((1, cb), lambda j, r: (0, j))
    return cur, prev, nxt, w, b


def _conv_fwd(xraw, w, b, name):
    T, C = xraw.shape
    rc, cb = min(512, T), min(512, C)
    cur, prev, _, ws, bs = _conv_specs(T, C, rc, cb)

    def body(x_ref, p_ref, w_ref, b_ref, o_ref):
        pv = jnp.where(pl.program_id(1) > 0, p_ref[...], 0.0)
        pre, _ = _conv_pre(x_ref[...], pv, w_ref, b_ref)
        o_ref[...] = pre * _sigmoid(pre)

    return pl.pallas_call(
        body, name=name, grid=(C // cb, T // rc), in_specs=[cur, prev, ws, bs], out_specs=cur,
        out_shape=jax.ShapeDtypeStruct((T, C), f32), compiler_params=_cp("parallel", "parallel"),
    )(xraw, xraw, w, b)


def _conv_bwd_pre(xraw, dxbc, w, b, name):
    T, C = xraw.shape
    rc, cb = min(512, T), min(512, C)
    cur, prev, _, ws, bs = _conv_specs(T, C, rc, cb)

    def body(x_ref, p_ref, d_ref, w_ref, b_ref, dpre_ref, dw_ref, db_ref):
        pv = jnp.where(pl.program_id(1) > 0, p_ref[...], 0.0)
        pre, taps = _conv_pre(x_ref[...], pv, w_ref, b_ref)
        sg = _sigmoid(pre)
        dpre = d_ref[...] * (sg * (1.0 + pre * (1.0 - sg)))
        dpre_ref[...] = dpre

        @pl.when(pl.program_id(1) == 0)
        def _():
            dw_ref[...] = jnp.zeros_like(dw_ref)
            db_ref[...] = jnp.zeros_like(db_ref)

        row = lax.broadcasted_iota(jnp.int32, dw_ref.shape, 0)
        upd = jnp.zeros(dw_ref.shape, f32)
        for s in range(4):
            upd = upd + jnp.where(row == 3 - s, jnp.sum(dpre * taps[s], axis=0, keepdims=True), 0.0)
        dw_ref[...] += upd
        db_ref[...] += jnp.sum(dpre, axis=0, keepdims=True)

    return pl.pallas_call(
        body, name=name, grid=(C // cb, T // rc), in_specs=[cur, prev, cur, ws, bs], out_specs=[cur, ws, bs],
        out_shape=[jax.ShapeDtypeStruct((T, C), f32), jax.ShapeDtypeStruct((4, C), f32), jax.ShapeDtypeStruct((1, C), f32)],
        compiler_params=_cp("parallel", "arbitrary"),
    )(xraw, xraw, dxbc, w, b)


def _conv_bwd_in(dpre, w, name):
    T, C = dpre.shape
    rc, cb = min(512, T), min(512, C)
    cur, _, nxt, ws, _ = _conv_specs(T, C, rc, cb)
    nr = T // rc

    def body(d_ref, n_ref, w_ref, o_ref):
        nv = jnp.where(pl.program_id(1) < nr - 1, n_ref[...], 0.0)
        cv = d_ref[...]
        out = w_ref[3:4, :] * cv
        for s in (1, 2, 3):
            out = out + w_ref[3 - s:4 - s, :] * _shift_up(cv, nv, s)
        o_ref[...] = out.astype(o_ref.dtype)

    return pl.pallas_call(
        body, name=name, grid=(C // cb, nr), in_specs=[cur, nxt, ws], out_specs=cur,
        out_shape=jax.ShapeDtypeStruct((T, C), bf16), compiler_params=_cp("parallel", "parallel"),
    )(dpre, dpre, w)


HEAD_SHIFT = SSD_HEAD_DIM.bit_length() - 1


def _ssd_prep(dtr_ref, bias_ref, alog_ref, SW):
    L = SSD_CHUNK
    xs = dtr_ref[...] + bias_ref[...]
    dt = _softplus(xs)
    a = -jnp.exp(alog_ref[...])
    causal = lax.broadcasted_iota(jnp.int32, (L, L), 0) >= lax.broadcasted_iota(jnp.int32, (L, L), 1)
    cs = _mask_dot(causal.astype(bf16), dt * a, 3)
    spread = (lax.broadcasted_iota(jnp.int32, (LANES, SW), 0)
              == lax.shift_right_logical(lax.broadcasted_iota(jnp.int32, (LANES, SW), 1), HEAD_SHIFT)).astype(bf16)
    dt_x = _dot_mask(dt, spread, 3)
    cs_x = _dot_mask(cs, spread, 3)
    last_x = cs_x[L - 1:L, :]
    return xs, dt, a, causal, cs, cs.T, dt_x, jnp.exp(cs_x), jnp.exp(last_x - cs_x), jnp.exp(last_x)


def _head_decay(cs, csT, causal, h):
    seg = cs[:, h:h + 1] - csT[h:h + 1, :]
    return jnp.where(causal, jnp.exp(jnp.minimum(seg, 0.0)), 0.0)


def _ssd_fwd(xbc, dtraw, z, dt_bias, a_log, d_skip_x, out_norm, HS, name, comm=None):
    T = xbc.shape[0]
    L, P, NS, G = SSD_CHUNK, SSD_HEAD_DIM, SSD_STATE, SSD_GROUPS
    SW, HPG, nc = HS * P, HS // SSD_GROUPS, T // SSD_CHUNK
    gsz = SW // G
    gw = HPG * P
    assert HPG % 2 == 0 and 2 * P == LANES

    def body(*refs):
        (xbc_ref, dtr_ref, z_ref, bias_ref, alog_ref, dsk_ref, on_ref), (y_ref, yn_ref, sp_ref), (st_ref,), cargs = _hosted(comm, 7, 3, refs)
        first, last = _first_last((nc,))
        if comm is not None:
            @pl.when(first)
            def _():
                comm["start"](*cargs)

            @pl.when(last)
            def _():
                comm["finish"](*cargs)

        @pl.when(pl.program_id(0) == 0)
        def _():
            st_ref[...] = jnp.zeros_like(st_ref)

        sp_ref[0] = st_ref[...]
        _, _, _, causal, cs, csT, dt_x, ecs_x, dte_x, cdec_x = _ssd_prep(dtr_ref, bias_ref, alog_ref, SW)
        X = xbc_ref[:, 0:SW]
        Xd = X * dt_x
        Xdb = Xd.astype(bf16)
        XEb = (Xd * dte_x).astype(bf16)
        left = lax.broadcasted_iota(jnp.int32, (L, LANES), 1) < P
        for g in range(G):
            gs = slice(g * gw, (g + 1) * gw)
            Bb = xbc_ref[:, SW + g * NS:SW + (g + 1) * NS].astype(bf16)
            Cb = xbc_ref[:, SW + (G + g) * NS:SW + (G + g + 1) * NS].astype(bf16)
            Gm = _dot(Cb, Bb, "nt")
            Sp = st_ref[:, gs]
            yo = _dot(Cb, Sp.astype(bf16)) * ecs_x[:, gs]
            st_ref[:, gs] = cdec_x[:, gs] * Sp + _dot(Bb, XEb[:, gs], "tn")
            for pr in range(HPG // 2):
                h0 = g * HPG + 2 * pr
                ps = slice(h0 * P, (h0 + 2) * P)
                xp = Xdb[:, ps]
                yd = jnp.where(left, _dot((_head_decay(cs, csT, causal, h0) * Gm).astype(bf16), xp),
                               _dot((_head_decay(cs, csT, causal, h0 + 1) * Gm).astype(bf16), xp))
                y_ref[:, ps] = yd + yo[:, pr * LANES:(pr + 1) * LANES] + dsk_ref[:, ps] * X[:, ps]
        zz = z_ref[...]
        gated = y_ref[...] * (zz * _sigmoid(zz))
        for g in range(G):
            gs = slice(g * gsz, (g + 1) * gsz)
            sg = gated[:, gs]
            rr = lax.rsqrt(jnp.mean(sg * sg, axis=-1, keepdims=True) + EPS)
            yn_ref[:, gs] = (sg * rr * on_ref[:, gs]).astype(yn_ref.dtype)

    vec = pl.BlockSpec((1, LANES), lambda c: (0, 0))
    wide = pl.BlockSpec((1, SW), lambda c: (0, 0))
    (y, yn, sp), got = _host_call(
        body, name, (nc,),
        [pl.BlockSpec((L, xbc.shape[1]), lambda c: (c, 0)), pl.BlockSpec((L, LANES), lambda c: (c, 0)),
         pl.BlockSpec((L, SW), lambda c: (c, 0)), vec, vec, wide, wide],
        [pl.BlockSpec((L, SW), lambda c: (c, 0)), pl.BlockSpec((L, SW), lambda c: (c, 0)),
         pl.BlockSpec((1, NS, SW), lambda c: (c, 0, 0))],
        [jax.ShapeDtypeStruct((T, SW), f32), jax.ShapeDtypeStruct((T, SW), bf16), jax.ShapeDtypeStruct((nc, NS, SW), f32)],
        [pltpu.VMEM((NS, SW), f32)], [xbc, dtraw, z, dt_bias, a_log, d_skip_x, out_norm], comm)
    return y, yn, sp, got


def _ssd_bwd(xbc, dtraw, sprev, dy, dt_bias, a_log, d_skip_x, HS, name):
    T = xbc.shape[0]
    L, P, NS, G = SSD_CHUNK, SSD_HEAD_DIM, SSD_STATE, SSD_GROUPS
    SW, HPG, nc = HS * P, HS // SSD_GROUPS, T // SSD_CHUNK
    gw = HPG * P

    def body(xbc_ref, dtr_ref, sp_ref, dy_ref, bias_ref, alog_ref, dsk_ref,
             dxbc_ref, ddtr_ref, dalog_ref, dbias_ref, dd_ref, ds_ref):
        @pl.when(pl.program_id(0) == 0)
        def _():
            ds_ref[...] = jnp.zeros_like(ds_ref)
            dalog_ref[...] = jnp.zeros_like(dalog_ref)
            dbias_ref[...] = jnp.zeros_like(dbias_ref)
            dd_ref[...] = jnp.zeros_like(dd_ref)

        xs, dt, a, causal, cs, csT, dt_x, ecs_x, dte_x, cdec_x = _ssd_prep(dtr_ref, bias_ref, alog_ref, SW)
        lane = lax.broadcasted_iota(jnp.int32, (L, LANES), 1)
        sub = lax.broadcasted_iota(jnp.int32, (LANES, L), 0)
        left = lane < P
        dcs = jnp.zeros((L, LANES), f32)
        dcs_t = jnp.zeros((LANES, L), f32)
        xds = jnp.zeros((L, LANES), f32)
        dlast = jnp.zeros((1, LANES), f32)
        dD = jnp.zeros((1, LANES), f32)
        for g in range(G):
            gs = slice(g * gw, (g + 1) * gw)
            bsl = slice(SW + g * NS, SW + (g + 1) * NS)
            csl = slice(SW + (G + g) * NS, SW + (G + g + 1) * NS)
            Bb = xbc_ref[:, bsl].astype(bf16)
            Cb = xbc_ref[:, csl].astype(bf16)
            Gm = _dot(Cb, Bb, "nt")
            X = xbc_ref[:, gs]
            Xd = X * dt_x[:, gs]
            Xdb = Xd.astype(bf16)
            XE = Xd * dte_x[:, gs]
            dY = dy_ref[:, gs]
            dYb = dY.astype(bf16)
            Wb = (dY * ecs_x[:, gs]).astype(bf16)
            Sp = sp_ref[0, :, gs]
            Spb = Sp.astype(bf16)
            dS = ds_ref[:, gs]
            dSb = dS.astype(bf16)
            CS = _dot(Cb, Spb)
            Zb = _dot(Bb, dSb)
            dC = _dot(Wb, Spb, "nt")
            dB = _dot(XE.astype(bf16), dSb, "nt")
            ds_ref[:, gs] = cdec_x[:, gs] * dS + _dot(Cb, Wb, "tn")
            R1 = dY * CS * ecs_x[:, gs]
            R2 = XE * Zb
            to_head = (lax.shift_right_logical(lax.broadcasted_iota(jnp.int32, (gw, LANES), 0), HEAD_SHIFT) + g * HPG
                       == lax.broadcasted_iota(jnp.int32, (gw, LANES), 1)).astype(bf16)
            dcs = dcs + _dot_mask(R1 - R2, to_head, 3)
            dlast = (dlast + jnp.sum(_dot_mask(R2, to_head, 3), axis=0, keepdims=True)
                     + jnp.sum(_dot_mask(Sp * dS * cdec_x[:, gs], to_head, 3), axis=0, keepdims=True))
            dG = jnp.zeros((L, L), f32)
            pieces = []
            for pr in range(HPG // 2):
                h0 = g * HPG + 2 * pr
                pw = slice(pr * LANES, (pr + 1) * LANES)
                xp, dyp = Xdb[:, pw], dYb[:, pw]
                halves = []
                for k, h in enumerate((h0, h0 + 1)):
                    Lm = _head_decay(cs, csT, causal, h)
                    Mf = Lm * Gm
                    keep = left if k == 0 else jnp.logical_not(left)
                    dM = _dot(jnp.where(keep, dyp, jnp.zeros_like(dyp)), xp, "nt")
                    Q = dM * Mf
                    dcs = dcs + jnp.where(lane == h, jnp.sum(Q, axis=1, keepdims=True), 0.0)
                    dcs_t = dcs_t - jnp.where(sub == h, jnp.sum(Q, axis=0, keepdims=True), 0.0)
                    dG = dG + dM * Lm
                    halves.append(_dot(Mf.astype(bf16), dyp, "tn"))
                pieces.append(jnp.where(left, halves[0], halves[1]))
            dXd = jnp.concatenate(pieces, axis=1) + dte_x[:, gs] * Zb
            dxbc_ref[:, gs] = dXd * dt_x[:, gs] + dsk_ref[:, gs] * dY
            xds = xds + _dot_mask(dXd * X, to_head, 3)
            dD = dD + jnp.sum(_dot_mask(dY * X, to_head, 3), axis=0, keepdims=True)
            dGb = dG.astype(bf16)
            dxbc_ref[:, bsl] = dB + _dot(dGb, Cb, "tn")
            dxbc_ref[:, csl] = dC + _dot(dGb, Bb)
        rowi = lax.broadcasted_iota(jnp.int32, (L, LANES), 0)
        dcs = dcs + dcs_t.T + jnp.where(rowi == L - 1, dlast, 0.0)
        anti = (lax.broadcasted_iota(jnp.int32, (L, L), 1) >= lax.broadcasted_iota(jnp.int32, (L, L), 0)).astype(bf16)
        dda = _mask_dot(anti, dcs, 3)
        ddt = dda * a + xds
        dalog_ref[...] += jnp.sum(dda * dt, axis=0, keepdims=True) * a
        ddtr = ddt * _sigmoid(xs)
        ddtr_ref[...] = ddtr
        dbias_ref[...] += jnp.sum(ddtr, axis=0, keepdims=True)
        dd_ref[...] += dD

    rev = lambda c: (nc - 1 - c, 0)
    vec = pl.BlockSpec((1, LANES), lambda c: (0, 0))
    return pl.pallas_call(
        body, name=name, grid=(nc,),
        in_specs=[pl.BlockSpec((L, xbc.shape[1]), rev), pl.BlockSpec((L, LANES), rev),
                  pl.BlockSpec((1, NS, SW), lambda c: (nc - 1 - c, 0, 0)), pl.BlockSpec((L, SW), rev), vec, vec,
                  pl.BlockSpec((1, SW), lambda c: (0, 0))],
        out_specs=[pl.BlockSpec((L, xbc.shape[1]), rev), pl.BlockSpec((L, LANES), rev), vec, vec, vec],
        out_shape=[jax.ShapeDtypeStruct(xbc.shape, f32), jax.ShapeDtypeStruct((T, LANES), f32)]
        + [jax.ShapeDtypeStruct((1, LANES), f32)] * 3,
        scratch_shapes=[pltpu.VMEM((NS, SW), f32)], compiler_params=_cp("arbitrary"),
    )(xbc, dtraw, sprev, dy, dt_bias, a_log, d_skip_x)


def _gate_bwd(y, z, dyn, out_norm, name, tr=256):
    T, SW = y.shape
    tr = min(tr, T)
    gsz = SW // SSD_GROUPS

    def body(y_ref, z_ref, d_ref, on_ref, dy_ref, dz_ref, don_ref):
        @pl.when(pl.program_id(0) == 0)
        def _():
            don_ref[...] = jnp.zeros_like(don_ref)

        for g in range(SSD_GROUPS):
            gs = slice(g * gsz, (g + 1) * gsz)
            yv, zv, dv = y_ref[:, gs], z_ref[:, gs], d_ref[:, gs]
            sg = _sigmoid(zv)
            sl = zv * sg
            gated = yv * sl
            rr = lax.rsqrt(jnp.mean(gated * gated, axis=-1, keepdims=True) + EPS)
            gh = gated * rr
            dgn = dv * on_ref[:, gs]
            dgated = rr * (dgn - gh * jnp.mean(dgn * gh, axis=-1, keepdims=True))
            dy_ref[:, gs] = dgated * sl
            dz_ref[:, gs] = (dgated * yv * (sg * (1.0 + zv * (1.0 - sg)))).astype(dz_ref.dtype)
            don_ref[:, gs] += jnp.sum(dv * gh, axis=0, keepdims=True)

    row = pl.BlockSpec((tr, SW), lambda r: (r, 0))
    vec = pl.BlockSpec((1, SW), lambda r: (0, 0))
    return pl.pallas_call(
        body, name=name, grid=(T // tr,), in_specs=[row, row, row, vec], out_specs=[row, row, vec],
        out_shape=[jax.ShapeDtypeStruct((T, SW), f32), jax.ShapeDtypeStruct((T, SW), bf16),
                   jax.ShapeDtypeStruct((1, SW), f32)],
        compiler_params=_cp("arbitrary"),
    )(y, z, dyn, out_norm)


def _qk_norm_fwd(qkv, qn_w, kn_w, SBW, name, tr=256):
    T = qkv.shape[0]
    tr = min(tr, T)
    nh = SBW // SB_HEAD_DIM

    def body(q_ref, k_ref, v_ref, qw_ref, kw_ref, qo_ref, ko_ref, vo_ref):
        for src, w_ref, dst in ((q_ref, qw_ref, qo_ref), (k_ref, kw_ref, ko_ref)):
            for h in range(nh):
                hs = slice(h * SB_HEAD_DIM, (h + 1) * SB_HEAD_DIM)
                sv = src[:, hs]
                rr = lax.rsqrt(jnp.mean(sv * sv, axis=-1, keepdims=True) + EPS)
                dst[:, hs] = (sv * rr * w_ref[...]).astype(dst.dtype)
        vo_ref[...] = v_ref[...].astype(vo_ref.dtype)

    blk = lambda j: pl.BlockSpec((tr, SBW), lambda r: (r, j))
    vec = pl.BlockSpec((1, SB_HEAD_DIM), lambda r: (0, 0))
    out = pl.BlockSpec((tr, SBW), lambda r: (r, 0))
    return pl.pallas_call(
        body, name=name, grid=(T // tr,), in_specs=[blk(0), blk(1), blk(2), vec, vec], out_specs=[out, out, out],
        out_shape=[jax.ShapeDtypeStruct((T, SBW), bf16)] * 3, compiler_params=_cp("parallel"),
    )(qkv, qkv, qkv, qn_w, kn_w)


def _qk_norm_bwd(qkv, dqn, dkn, dv, qn_w, kn_w, SBW, name, tr=256):
    T = qkv.shape[0]
    tr = min(tr, T)
    nh = SBW // SB_HEAD_DIM

    def body(q_ref, k_ref, dq_ref, dk_ref, dv_ref, qw_ref, kw_ref, o_ref, dqw_ref, dkw_ref):
        @pl.when(pl.program_id(0) == 0)
        def _():
            dqw_ref[...] = jnp.zeros_like(dqw_ref)
            dkw_ref[...] = jnp.zeros_like(dkw_ref)

        for part, (src, d_ref, w_ref, dw_ref) in enumerate(((q_ref, dq_ref, qw_ref, dqw_ref), (k_ref, dk_ref, kw_ref, dkw_ref))):
            dw = jnp.zeros((1, SB_HEAD_DIM), f32)
            for h in range(nh):
                hs = slice(h * SB_HEAD_DIM, (h + 1) * SB_HEAD_DIM)
                os_ = slice(part * SBW + h * SB_HEAD_DIM, part * SBW + (h + 1) * SB_HEAD_DIM)
                sv, dn = src[:, hs], d_ref[:, hs]
                rr = lax.rsqrt(jnp.mean(sv * sv, axis=-1, keepdims=True) + EPS)
                xh = sv * rr
                dg = dn * w_ref[...]
                o_ref[:, os_] = (rr * (dg - xh * jnp.mean(dg * xh, axis=-1, keepdims=True))).astype(o_ref.dtype)
                dw = dw + jnp.sum(dn * xh, axis=0, keepdims=True)
            dw_ref[...] += dw
        o_ref[:, 2 * SBW:] = dv_ref[...].astype(o_ref.dtype)

    blk = lambda j: pl.BlockSpec((tr, SBW), lambda r: (r, j))
    vec = pl.BlockSpec((1, SB_HEAD_DIM), lambda r: (0, 0))
    row = pl.BlockSpec((tr, SBW), lambda r: (r, 0))
    return pl.pallas_call(
        body, name=name, grid=(T // tr,), in_specs=[blk(0), blk(1), row, row, row, vec, vec],
        out_specs=[pl.BlockSpec((tr, 3 * SBW), lambda r: (r, 0)), vec, vec],
        out_shape=[jax.ShapeDtypeStruct((T, 3 * SBW), bf16)] + [jax.ShapeDtypeStruct((1, SB_HEAD_DIM), f32)] * 2,
        compiler_params=_cp("arbitrary"),
    )(qkv, qkv, dqn, dkn, dv, qn_w, kn_w)


def _sb_logits(q, kb, scale):
    zl = _dot(q, kb, "nt") * scale
    lb = jnp.minimum(zl, 0.0) - jnp.log(1.0 + jnp.exp(-jnp.abs(zl)))
    return zl, lb, lb - zl


def _tail_update(old, r0, new_tail):
    return new_tail if r0 == 0 else jnp.concatenate([old[:r0], new_tail], axis=0)


def _hosted(comm, n_in, n_out, refs):
    nci, nco, ncs = (len(comm["ins"]), len(comm["outs"]), len(comm["sems"])) if comm is not None else (0, 0, 0)
    ins, outs = refs[:n_in], refs[n_in + nci:n_in + nci + n_out]
    scratch = refs[n_in + nci + n_out + nco:len(refs) - ncs]
    cargs = (refs[n_in:n_in + nci], refs[n_in + nci + n_out:n_in + nci + n_out + nco], refs[len(refs) - ncs:])
    return ins, outs, scratch, cargs


def _host_call(body, name, grid, in_specs, out_specs, out_shape, scratch, operands, comm):
    n_out = len(out_shape)
    in_specs, out_specs, out_shape, scratch, operands = list(in_specs), list(out_specs), list(out_shape), list(scratch), list(operands)
    io = {}
    if comm is not None:
        for src, dst in comm.get("aliases", {}).items():
            io[len(operands) + src] = n_out + dst
        in_specs += [ANY] * len(comm["ins"])
        operands += comm["ins"]
        out_specs += [ANY] * len(comm["outs"])
        out_shape += comm["outs"]
        scratch += comm["sems"]
    res = pl.pallas_call(body, name=name, grid=grid, in_specs=in_specs, out_specs=out_specs, out_shape=out_shape,
                         scratch_shapes=scratch, input_output_aliases=io,
                         compiler_params=_cp(*(("arbitrary",) * len(grid))))(*operands)
    return res[:n_out], res[n_out:]


def _first_last(grid):
    pid = [pl.program_id(d) for d in range(len(grid))]
    first, last = pid[0] == 0, pid[0] == grid[0] - 1
    for d in range(1, len(grid)):
        first, last = first & (pid[d] == 0), last & (pid[d] == grid[d] - 1)
    return first, last


def _sb_fwd(qn, kn, vb, name, comm=None, tq=2048, tk=256):
    T, W = qn.shape
    tq = min(tq, T)
    tk = min(tk, tq)
    nh, nq, dh, nd = W // SB_HEAD_DIM, T // tq, SB_HEAD_DIM, tq // tk
    scale = dh ** -0.5
    grid = (nh, nq)

    def body(*refs):
        (q_ref, k_ref, v_ref), (o_ref, c_ref), _, cargs = _hosted(comm, 3, 2, refs)
        first, last = _first_last(grid)
        if comm is not None:
            @pl.when(first)
            def _():
                comm["start"](*cargs)

        qi = pl.program_id(1)
        q = q_ref[...]
        later = (lax.broadcasted_iota(jnp.int32, (tk, tk), 0) > lax.broadcasted_iota(jnp.int32, (tk, tk), 1)).astype(bf16)

        def step(j, carry, d):
            acc, run = carry
            r0 = 0 if d is None else d * tk
            ks = pl.multiple_of(j * tk, tk)
            kb, vv = k_ref[pl.ds(ks, tk), :], v_ref[pl.ds(ks, tk), :]
            _, lb, lk = _sb_logits(q[r0:], kb, scale)
            if d is not None:
                mask = lax.broadcasted_iota(jnp.int32, lk.shape, 1) < lax.broadcasted_iota(jnp.int32, lk.shape, 0)
                lk = jnp.where(mask, lk, 0.0)
            between = _dot_mask(lk, later, 2)
            w = jnp.exp(lb + between + run[r0:])
            if d is not None:
                w = jnp.where(mask, w, 0.0)
            return (_tail_update(acc, r0, acc[r0:] + _dot(w.astype(bf16), vv)),
                    _tail_update(run, r0, run[r0:] + between[:, 0:1] + lk[:, 0:1]))

        carry = (jnp.zeros((tq, dh), f32), jnp.zeros((tq, 1), f32))
        for d in range(nd - 1, -1, -1):
            carry = step(qi * nd + d, carry, d)
        n_before = qi * nd
        acc, run = lax.fori_loop(0, n_before, lambda t, c: step(n_before - 1 - t, c, None), carry)
        o_ref[...] = acc.astype(o_ref.dtype)
        c_ref[...] = jnp.broadcast_to(run, (tq, dh))
        if comm is not None:
            @pl.when(last)
            def _():
                comm["finish"](*cargs)

    qblk = pl.BlockSpec((tq, dh), lambda h, i: (i, h))
    full = pl.BlockSpec((T, dh), lambda h, i: (0, h))
    (o, c), got = _host_call(body, name, grid, [qblk, full, full], [qblk, qblk],
                             [jax.ShapeDtypeStruct((T, W), bf16), jax.ShapeDtypeStruct((T, W), f32)], [], [qn, kn, vb], comm)
    return o, c, got


def _sb_bwd(qn, kn, vb, do, ctot, do_off, name, comm=None, tq=2048, tk=256):
    T, W = qn.shape
    tq = min(tq, T)
    tk = min(tk, tq)
    nh, nq, dh, nd = W // SB_HEAD_DIM, T // tq, SB_HEAD_DIM, tq // tk
    scale = dh ** -0.5
    ob = do_off // dh
    grid = (nh, nq)

    def body(*refs):
        (q_ref, k_ref, v_ref, do_ref, c_ref), (dq_ref, dk_ref, dv_ref), _, cargs = _hosted(comm, 5, 3, refs)
        first, last = _first_last(grid)
        if comm is not None:
            @pl.when(first)
            def _():
                comm["start"](*cargs)

        qi = pl.program_id(1)

        @pl.when(qi == 0)
        def _():
            dk_ref[...] = jnp.zeros_like(dk_ref)
            dv_ref[...] = jnp.zeros_like(dv_ref)

        q = q_ref[...]
        dob = do_ref[...].astype(bf16)
        total = c_ref[:, 0:1]
        r2 = lax.broadcasted_iota(jnp.int32, (tk, tk), 0)
        c2 = lax.broadcasted_iota(jnp.int32, (tk, tk), 1)
        upto = (r2 <= c2).astype(bf16)
        before = (r2 < c2).astype(bf16)

        def step(j, carry, d):
            dq, pre, gpre = carry
            r0 = 0 if d is None else d * tk
            ks = pl.multiple_of(j * tk, tk)
            kb, vv = k_ref[pl.ds(ks, tk), :], v_ref[pl.ds(ks, tk), :]
            qs, dos = q[r0:], dob[r0:]
            _, lb, lk = _sb_logits(qs, kb, scale)
            if d is not None:
                mask = lax.broadcasted_iota(jnp.int32, lk.shape, 1) < lax.broadcasted_iota(jnp.int32, lk.shape, 0)
                lk = jnp.where(mask, lk, 0.0)
            pin = _dot_mask(lk, upto, 2)
            w = jnp.exp(lb + (total[r0:] - pre[r0:] - pin))
            if d is not None:
                w = jnp.where(mask, w, 0.0)
            dw = _dot(dos, vv, "nt")
            dv_ref[pl.ds(ks, tk), :] += _dot(w.astype(bf16), dos, "tn")
            gg = dw * w
            gex = _dot(gg.astype(bf16), before)
            beta = jnp.exp(lb)
            dz = (gg * (1.0 - beta) - (gpre[r0:] + gex) * beta) * scale
            if d is not None:
                dz = jnp.where(mask, dz, 0.0)
            dzb = dz.astype(bf16)
            dk_ref[pl.ds(ks, tk), :] += _dot(dzb, qs, "tn")
            return (_tail_update(dq, r0, dq[r0:] + _dot(dzb, kb)),
                    _tail_update(pre, r0, pre[r0:] + pin[:, tk - 1:tk]),
                    _tail_update(gpre, r0, gpre[r0:] + gex[:, tk - 1:tk] + gg[:, tk - 1:tk]))

        init = (jnp.zeros((tq, dh), f32), jnp.zeros((tq, 1), f32), jnp.zeros((tq, 1), f32))
        carry = lax.fori_loop(0, qi * nd, lambda t, c: step(t, c, None), init)
        for d in range(nd):
            carry = step(qi * nd + d, carry, d)
        dq_ref[...] = carry[0]
        if comm is not None:
            @pl.when(last)
            def _():
                comm["finish"](*cargs)

    qblk = pl.BlockSpec((tq, dh), lambda h, i: (i, h))
    full = pl.BlockSpec((T, dh), lambda h, i: (0, h))
    (dq, dk, dv), got = _host_call(
        body, name, grid, [qblk, full, full, pl.BlockSpec((tq, dh), lambda h, i: (i, h + ob)), qblk], [qblk, full, full],
        [jax.ShapeDtypeStruct((T, W), f32)] * 3, [], [qn, kn, vb, do, ctot], comm)
    return dq, dk, dv, got


def _pool_select(sums, g):
    return jnp.where(g == 0, sums[0], jnp.where(g == 1, sums[1], jnp.where(g == 2, sums[2], sums[3])))


def _pool_count(g, r, rc, n, cols, off=0):
    t = (r * rc + off + lax.broadcasted_iota(jnp.int32, (n, cols), 0)).astype(f32)
    win = jnp.left_shift(2, g).astype(f32)
    return jnp.minimum(t + 1.0, win)


def _pool_fwd(hp, xres, w, b, scale, name, rc=512):
    T, D = hp.shape
    rc = min(rc, T)
    pg = D // len(POOL_WINDOWS)

    def body(h_ref, p_ref, x_ref, w_ref, b_ref, s_ref, o_ref, yp_ref, d_ref):
        g, r = pl.program_id(0), pl.program_id(1)
        cur = h_ref[...]
        halo = jnp.where(r > 0, p_ref[...], 0.0)
        ext = jnp.concatenate([halo, cur], axis=0)
        sums, s = [], ext
        for sh in (1, 2, 4, 8):
            s = s + pltpu.roll(s, sh, 0)
            sums.append(s)
        d = _pool_select(sums, g)[POOL_HALO:] / _pool_count(g, r, rc, rc, pg) - cur
        yp = _dot(d.astype(bf16), w_ref[0]) + b_ref[...]
        yp_ref[...] = yp
        d_ref[...] = d.astype(d_ref.dtype)
        o_ref[...] = x_ref[...] + yp * s_ref[...]

    cur = pl.BlockSpec((rc, pg), lambda g, r: (r, g))
    prev = pl.BlockSpec((POOL_HALO, pg), lambda g, r: (jnp.maximum(r * (rc // POOL_HALO) - 1, 0), g))
    vec = pl.BlockSpec((1, pg), lambda g, r: (0, g))
    return pl.pallas_call(
        body, name=name, grid=(len(POOL_WINDOWS), T // rc),
        in_specs=[cur, prev, cur, pl.BlockSpec((1, pg, pg), lambda g, r: (g, 0, 0)), vec, vec],
        out_specs=[cur, cur, cur],
        out_shape=[jax.ShapeDtypeStruct((T, D), f32), jax.ShapeDtypeStruct((T, D), f32), jax.ShapeDtypeStruct((T, D), bf16)],
        compiler_params=_cp("parallel", "parallel"),
    )(hp, hp, xres, w, b, scale)


def _pool_bwd(dx, yp, d, w, scale, name, rc=512):
    T, D = dx.shape
    rc = min(rc, T)
    pg = D // len(POOL_WINDOWS)
    nr = T // rc

    def body(dx_ref, dn_ref, yp_ref, d_ref, w_ref, s_ref, dh_ref, dw_ref, db_ref, dsc_ref):
        g, r = pl.program_id(0), pl.program_id(1)

        @pl.when(r == 0)
        def _():
            dw_ref[...] = jnp.zeros_like(dw_ref)
            db_ref[...] = jnp.zeros_like(db_ref)
            dsc_ref[...] = jnp.zeros_like(dsc_ref)

        dxv = dx_ref[...]
        dyp = dxv * s_ref[...]
        dsc_ref[...] += jnp.sum(dxv * yp_ref[...], axis=0, keepdims=True)
        db_ref[...] += jnp.sum(dyp, axis=0, keepdims=True)
        dypb = dyp.astype(bf16)
        dw_ref[0] += _dot(d_ref[...], dypb, "tn")
        dd = _dot(dypb, w_ref[0], "nt")
        ddn = _dot((dn_ref[...] * s_ref[...]).astype(bf16), w_ref[0], "nt")
        e = dd / _pool_count(g, r, rc, rc, pg)
        en = jnp.where(r < nr - 1, ddn / _pool_count(g, r, rc, POOL_HALO, pg, off=rc), 0.0)
        ext = jnp.concatenate([e, en], axis=0)
        sums, s = [], ext
        for sh in (1, 2, 4, 8):
            s = s + pltpu.roll(s, rc + POOL_HALO - sh, 0)
            sums.append(s)
        dh_ref[...] = _pool_select(sums, g)[:rc] - dd

    cur = pl.BlockSpec((rc, pg), lambda g, r: (r, g))
    nxt = pl.BlockSpec((POOL_HALO, pg), lambda g, r: (jnp.minimum((r + 1) * (rc // POOL_HALO), T // POOL_HALO - 1), g))
    vec = pl.BlockSpec((1, pg), lambda g, r: (0, g))
    wsp = pl.BlockSpec((1, pg, pg), lambda g, r: (g, 0, 0))
    return pl.pallas_call(
        body, name=name, grid=(len(POOL_WINDOWS), nr), in_specs=[cur, nxt, cur, cur, wsp, vec],
        out_specs=[cur, wsp, vec, vec],
        out_shape=[jax.ShapeDtypeStruct((T, D), f32), jax.ShapeDtypeStruct(w.shape, f32),
                   jax.ShapeDtypeStruct((1, D), f32), jax.ShapeDtypeStruct((1, D), f32)],
        compiler_params=_cp("parallel", "arbitrary"),
    )(dx, dx, yp, d, w, scale)


def _adamw(w, g, m, v, name, tr=256):
    R, C = w.shape
    tr = min(tr, R)
    lanes = -(-C // LANES) * LANES
    while tr > 8 and 2 * 8 * tr * lanes * 4 > MM_TILE_BUDGET:
        tr //= 2
    assert R % tr == 0

    def body(w_ref, g_ref, m_ref, v_ref, d_ref, mo_ref, vo_ref, go_ref):
        gv = g_ref[...]
        mn = ADAM_B1 * m_ref[...] + (1.0 - ADAM_B1) * gv
        vn = ADAM_B2 * v_ref[...] + (1.0 - ADAM_B2) * (gv * gv)
        m_hat = mn / (1.0 - ADAM_B1 ** ADAM_STEP)
        v_hat = vn / (1.0 - ADAM_B2 ** ADAM_STEP)
        d_ref[...] = -ADAM_LR * (m_hat / (jnp.sqrt(v_hat) + ADAM_EPS) + ADAM_WD * w_ref[...])
        mo_ref[...] = mn
        vo_ref[...] = vn
        go_ref[...] = gv

    blk = pl.BlockSpec((tr, C), lambda r: (r, 0))
    return pl.pallas_call(
        body, name=name, grid=(R // tr,), in_specs=[blk] * 4, out_specs=[blk] * 4,
        out_shape=[jax.ShapeDtypeStruct((R, C), f32)] * 4, compiler_params=_cp("parallel"),
    )(w, g, m, v)


def _pair_sum(g4, recv, name, br=256):
    _, R, C = g4.shape
    hr = R // 2
    br = min(br, hr)
    nb = hr // br

    def body(a_ref, b_ref, o_ref):
        o_ref[...] = (a_ref[...] + b_ref[...]).astype(o_ref.dtype)

    out = pl.BlockSpec((1, br, C), lambda s, i: (s, i, 0))
    return pl.pallas_call(
        body, name=name, grid=(N_CHIPS, nb),
        in_specs=[pl.BlockSpec((1, br, C), lambda s, i: (s, lax.axis_index("c") * nb + i, 0)), out], out_specs=out,
        out_shape=jax.ShapeDtypeStruct((N_CHIPS, hr, C), bf16), compiler_params=_cp("parallel", "parallel"),
    )(g4, recv)


def _chip_sum(g4, recv, pieces, name, br=256):
    _, hr, C = recv.shape
    br = min(br, hr)
    nb = hr // br
    chip = lambda: 2 * lax.axis_index("x") + lax.axis_index("y")

    def body(a_ref, r_ref, b1_ref, b2_ref, b3_ref, o_ref):
        o_ref[...] = (((a_ref[0] + r_ref[0]) + b1_ref[0].astype(f32)) + b2_ref[0].astype(f32)) + b3_ref[0].astype(f32)

    other = lambda k: pl.BlockSpec((1, br, C), lambda i: ((chip() + k) % N_CHIPS, i, 0))
    return pl.pallas_call(
        body, name=name, grid=(nb,),
        in_specs=[pl.BlockSpec((1, br, C), lambda i: (chip(), lax.axis_index("c") * nb + i, 0)),
                  pl.BlockSpec((1, br, C), lambda i: (chip(), i, 0)), other(1), other(2), other(3)],
        out_specs=pl.BlockSpec((br, C), lambda i: (lax.axis_index("c") * nb + i, 0)),
        out_shape=jax.ShapeDtypeStruct((2 * hr, C), f32), compiler_params=_cp("parallel"),
    )(g4, recv, pieces, pieces, pieces)


ANY = pl.BlockSpec(memory_space=pl.ANY)


def _mesh_pos():
    x, y, c = lax.axis_index("x"), lax.axis_index("y"), lax.axis_index("c")
    others = [(1 - x, y), (x, 1 - y), (1 - x, 1 - y)]
    return x, y, c, 2 * x + y, others


def _gather_small(blk, name):
    m, n = blk.shape

    def body(x_ref, out_ref, sum_ref, send_sems, recv_sems, local_sem):
        x, y, c, _, others = _mesh_pos()
        me, sibling = (x, y, c), (x, y, 1 - c)

        def rows(px, py, pc):
            return out_ref.at[pl.ds((4 * px + 2 * py + pc) * m, m), :]

        def copy(k, block, to, src=None):
            return pltpu.make_async_remote_copy(
                src_ref=rows(*block) if src is None else src, dst_ref=rows(*block),
                send_sem=send_sems.at[k], recv_sem=recv_sems.at[k], device_id=to, device_id_type=MESH)

        mine = pltpu.make_async_copy(x_ref, rows(*me), local_sem)
        mine.start()
        first = [copy(0, me, sibling, src=x_ref)]
        first += [copy(1 + j, me, (*chip, c), src=x_ref) for j, chip in enumerate(others)]
        for cp in first:
            cp.start()
        passed = [copy(4 + j, (*chip, c), sibling) for j, chip in enumerate(others)]
        for j, chip in enumerate(others):
            copy(1 + j, (*chip, c), me).wait_recv()
            passed[j].start()
        copy(0, sibling, me).wait_recv()
        for j, chip in enumerate(others):
            copy(4 + j, (*chip, 1 - c), me).wait_recv()
        for cp in first + passed:
            cp.wait_send()
        mine.wait()
        acc = out_ref[0:m, :]
        for d in range(1, 8):
            acc = acc + out_ref[d * m:(d + 1) * m, :]
        sum_ref[...] = acc

    vm = pl.BlockSpec(memory_space=pltpu.VMEM)
    return pl.pallas_call(
        body, name=name, in_specs=[vm], out_specs=[vm, vm],
        out_shape=[jax.ShapeDtypeStruct((8 * m, n), f32), jax.ShapeDtypeStruct((m, n), f32)],
        scratch_shapes=[pltpu.SemaphoreType.DMA((7,)), pltpu.SemaphoreType.DMA((7,)), pltpu.SemaphoreType.DMA],
    )(blk)


def _copy(src, dst, sems, idx, to):
    return pltpu.make_async_remote_copy(src_ref=src, dst_ref=dst, send_sem=sems[0].at[idx], recv_sem=sems[1].at[idx],
                                        device_id=to, device_id_type=MESH)


def _gather_ici(shards):
    nt = len(shards)

    def copies(ins, outs, sems):
        x, y, c, chip, others = _mesh_pos()
        send, land = [], []
        for t in range(nt):
            hr = ins[t].shape[0] // 2
            for j, (px, py) in enumerate(others):
                send.append((ins[t].at[pl.ds(c * hr, hr)], outs[t].at[chip, pl.ds(c * hr, hr)], sems, (t, j), (px, py, c)))
                piece = outs[t].at[2 * px + py, pl.ds(c * hr, hr)]
                land.append((piece, piece, sems, (t, j), (px, py, c)))
        return send, land

    def start(ins, outs, sems):
        for args in copies(ins, outs, sems)[0]:
            _copy(*args).start()

    def finish(ins, outs, sems):
        send, land = copies(ins, outs, sems)
        for args in land:
            _copy(*args).wait_recv()
        for args in send:
            _copy(*args).wait_send()

    return dict(ins=list(shards), outs=[jax.ShapeDtypeStruct((N_CHIPS,) + s.shape, s.dtype) for s in shards],
                sems=[pltpu.SemaphoreType.DMA((nt, 3)), pltpu.SemaphoreType.DMA((nt, 3))], start=start, finish=finish)


def _gather_d2d(stacks):
    nt = len(stacks)

    def copies(ins, outs, sems):
        x, y, c, _, others = _mesh_pos()
        send, land = [], []
        for t in range(nt):
            hr = outs[t].shape[1] // 2
            for j, (px, py) in enumerate(others):
                mine = outs[t].at[2 * px + py, pl.ds(c * hr, hr)]
                theirs = outs[t].at[2 * px + py, pl.ds((1 - c) * hr, hr)]
                send.append((mine, mine, sems, (t, j), (x, y, 1 - c)))
                land.append((theirs, theirs, sems, (t, j), (x, y, 1 - c)))
        return send, land

    def start(ins, outs, sems):
        for args in copies(ins, outs, sems)[0]:
            _copy(*args).start()

    def finish(ins, outs, sems):
        send, land = copies(ins, outs, sems)
        for args in land:
            _copy(*args).wait_recv()
        for args in send:
            _copy(*args).wait_send()

    return dict(ins=list(stacks), outs=[jax.ShapeDtypeStruct(s.shape, s.dtype) for s in stacks],
                sems=[pltpu.SemaphoreType.DMA((nt, 3)), pltpu.SemaphoreType.DMA((nt, 3))], start=start, finish=finish,
                aliases={t: t for t in range(nt)})


def _run_exchange(comm, name):
    ni, no = len(comm["ins"]), len(comm["outs"])

    def body(*refs):
        args = (refs[:ni], refs[ni:ni + no], refs[ni + no:])
        comm["start"](*args)
        comm["finish"](*args)

    return pl.pallas_call(
        body, name=name, in_specs=[ANY] * ni, out_specs=[ANY] * no, out_shape=comm["outs"], scratch_shapes=comm["sems"],
        input_output_aliases=dict(comm.get("aliases", {})))(*comm["ins"])


def _swap_halves(g4s):
    nt = len(g4s)

    def copies(ins, outs, sems):
        x, y, c, _, _ = _mesh_pos()
        both = []
        for t in range(nt):
            hr = ins[t].shape[1] // 2
            both.append((ins[t].at[:, pl.ds((1 - c) * hr, hr)], outs[t], sems, t, (x, y, 1 - c)))
        return both, both

    def start(ins, outs, sems):
        for args in copies(ins, outs, sems)[0]:
            _copy(*args).start()

    def finish(ins, outs, sems):
        send, land = copies(ins, outs, sems)
        for args in land:
            _copy(*args).wait_recv()
        for args in send:
            _copy(*args).wait_send()

    return dict(ins=list(g4s), outs=[jax.ShapeDtypeStruct((N_CHIPS, g.shape[1] // 2, g.shape[2]), g.dtype) for g in g4s],
                sems=[pltpu.SemaphoreType.DMA((nt,)), pltpu.SemaphoreType.DMA((nt,))], start=start, finish=finish)


def _exchange_chips(h4s):
    nt = len(h4s)

    def copies(ins, outs, sems):
        x, y, c, chip, others = _mesh_pos()
        send, land = [], []
        for t in range(nt):
            for j, (px, py) in enumerate(others):
                send.append((ins[t].at[2 * px + py], outs[t].at[chip], sems, (t, j), (px, py, c)))
                landed = outs[t].at[2 * px + py]
                land.append((landed, landed, sems, (t, j), (px, py, c)))
        return send, land

    def start(ins, outs, sems):
        for args in copies(ins, outs, sems)[0]:
            _copy(*args).start()

    def finish(ins, outs, sems):
        send, land = copies(ins, outs, sems)
        for args in land:
            _copy(*args).wait_recv()
        for args in send:
            _copy(*args).wait_send()

    return dict(ins=list(h4s), outs=[jax.ShapeDtypeStruct(h.shape, h.dtype) for h in h4s],
                sems=[pltpu.SemaphoreType.DMA((nt, 3)), pltpu.SemaphoreType.DMA((nt, 3))], start=start, finish=finish)


def _join_halves(fs):
    nt = len(fs)

    def copies(ins, outs, sems):
        x, y, c, _, _ = _mesh_pos()
        send, land = [], []
        for t in range(nt):
            hr = outs[t].shape[0] // 2
            mine, theirs = outs[t].at[pl.ds(c * hr, hr)], outs[t].at[pl.ds((1 - c) * hr, hr)]
            send.append((mine, mine, sems, t, (x, y, 1 - c)))
            land.append((theirs, theirs, sems, t, (x, y, 1 - c)))
        return send, land

    def start(ins, outs, sems):
        for args in copies(ins, outs, sems)[0]:
            _copy(*args).start()

    def finish(ins, outs, sems):
        send, land = copies(ins, outs, sems)
        for args in land:
            _copy(*args).wait_recv()
        for args in send:
            _copy(*args).wait_send()

    return dict(ins=list(fs), outs=[jax.ShapeDtypeStruct(f.shape, f.dtype) for f in fs],
                sems=[pltpu.SemaphoreType.DMA((nt,)), pltpu.SemaphoreType.DMA((nt,))], start=start, finish=finish,
                aliases={t: t for t in range(nt)})


def _pad_lanes(v, n=LANES):
    return jnp.pad(v, ((0, 0), (0, n - v.shape[-1])))


def _mlp_fwd(xin, norm_g, w_up, w_down, F, tag, comms=(None, None)):
    T, D = xin.shape
    h = _rms_fwd(xin, norm_g, bf16, f"{tag}_norm")

    def relu_sq(acc):
        r = jnp.maximum(acc, 0.0)
        return r, r * r

    got = [None, None]
    ua = _mm(h, w_up[0], "nn", T, F, D, (bf16, bf16), f"{tag}_up", epilogue=relu_sq, b_view=w_up[1], comm=comms[0])
    (u, a), got[0] = ua if comms[0] is not None else (ua, None)
    out = _mm(a, w_down[0], "nn", T, D, F, (f32,), f"{tag}_down", epilogue=lambda acc, res: (res + acc,),
              extras=((xin, "tile"),), b_view=w_down[1], comm=comms[1])
    (out,), got[1] = out if comms[1] is not None else ((out,), None)
    return out, (xin, h, u, a), got


def _mlp_bwd(dy, saved, norm_g, w_up, w_down, F, tag, up_to=None, down_to=None):
    xin, h, u, a = saved
    T, D = xin.shape
    to = lambda t: {} if t is None else dict(out_view=t[0], out_stack=t[1], alias=t[2])
    du = _mm(dy, w_down[0], "nt", T, F, D, (bf16,), f"{tag}_dact", epilogue=lambda acc, uu: (acc * (2.0 * uu.astype(f32)),),
             extras=((u, "tile"),), b_view=w_down[1])
    dw_down = _mm(a, dy, "tn", F, D, T, (f32,), f"{tag}_dwdown", **to(down_to))
    dh = _mm(du, w_up[0], "nt", T, D, F, (f32,), f"{tag}_dh", b_view=w_up[1])
    dw_up = _mm(h, du, "tn", D, F, T, (f32,), f"{tag}_dwup", **to(up_to))
    dx, dg = _rms_bwd(xin, norm_g, dh, dy, f"{tag}_dnorm")
    return dx, dg, dw_up, dw_down


def _local_step(xc, tgt, W, HS, SBW, net=None):
    T, D = xc.shape
    SW = HS * SSD_HEAD_DIM
    CD = W["conv_b"].shape[-1]
    mlp_norm = W["mlp_norm"]
    add = lambda acc, prev: (prev + acc,)

    h0 = _rms_fwd(xc, W["hyb_norm"], bf16, "hyb_norm")
    z = _mm(h0, W["w_z"], "nn", T, SW, D, (f32,), "proj_z")
    xraw = _mm(h0, W["w_xbc"], "nn", T, CD, D, (f32,), "proj_xbc")
    dtraw = _mm(h0, W["w_dt"], "nn", T, LANES, D, (f32,), "proj_dt")
    qkv = _mm(h0, W["w_qkv"], "nn", T, 3 * SBW, D, (f32,), "proj_qkv")
    qn, kn, vb = _qk_norm_fwd(qkv, W["q_norm"], W["k_norm"], SBW, "qk_norm")
    y_sb, ctot, got = _sb_fwd(qn, kn, vb, "sb_attn", comm=net.rest_ici() if net else None)
    xbc = _conv_fwd(xraw, W["conv_w"], W["conv_b"], "conv")
    y_ssd, yn_ssd, sprev, got = _ssd_fwd(xbc, dtraw, z, W["dt_bias"], W["a_log"], W["d_skip"], W["out_norm"], HS, "ssd",
                                         comm=net.rest_d2d(got) if net else None)
    if net:
        W = {**W, **net.rest_weights(got)}
    w_up, w_down, F = W["w_up"], W["w_down"], W["F"]
    mix = _mm(yn_ssd, W["w_out"], "nn", T, D, SW, (f32,), "out_ssd", epilogue=add, extras=((xc, "tile"),))
    x1 = _mm(y_sb, W["w_out"], "nn", T, D, SBW, (f32,), "out_sb", epilogue=add, extras=((mix, "tile"),), b_off=(SW, 0))
    x2, mlp0, got = _mlp_fwd(x1, mlp_norm[0:1], w_up[0], w_down[0], F, "mlp0",
                             comms=(net.last_ici(0), net.last_ici(1)) if net else (None, None))
    if net:
        w_up, w_down = net.last_weights(got, w_up, w_down)

    hp = _rms_fwd(x2, W["pool_norm"], f32, "pool_norm")
    x3, yp, dpool = _pool_fwd(hp, x2, W["w_pool"], W["pool_b"], W["pool_scale"], "pool")
    x4, mlp1, _ = _mlp_fwd(x3, mlp_norm[1:2], w_up[1], w_down[1], F, "mlp1")

    dy, sq = _loss_grad(x4, tgt, "loss")

    up_to, down_to = (net.mlp_to("up", 1, None), net.mlp_to("down", 1, None)) if net else (None, None)
    dx3, dg_mlp1, dw_up1, dw_down1 = _mlp_bwd(dy, mlp1, mlp_norm[1:2], w_up[1], w_down[1], F, "mlp1", up_to, down_to)
    dhp, dw_pool, db_pool, dsc_pool = _pool_bwd(dx3, yp, dpool, W["w_pool"], W["pool_scale"], "pool_bwd")
    dx2, dg_pool = _rms_bwd(x2, W["pool_norm"], dhp, dx3, "pool_dnorm")
    up_to, down_to = (net.mlp_to("up", 0, dw_up1), net.mlp_to("down", 0, dw_down1)) if net else (None, None)
    dx1, dg_mlp0, dw_up0, dw_down0 = _mlp_bwd(dx2, mlp0, mlp_norm[0:1], w_up[0], w_down[0], F, "mlp0", up_to, down_to)

    dw_out = jnp.concatenate([_mm(yn_ssd, dx1, "tn", SW, D, T, (f32,), "dwout_ssd"),
                              _mm(y_sb, dx1, "tn", SBW, D, T, (f32,), "dwout_sb")], axis=0)
    if net:
        (dmerged,), got = _mm(dx1, W["w_out"], "nt", T, SW + SBW, D, (f32,), "dmerged",
                              comm=net.early_swap(dw_out, dw_pool, dw_up0, dw_down0))
        dqn, dkn, dvv, got = _sb_bwd(qn, kn, vb, dmerged, ctot, SW, "sb_attn_bwd", comm=net.reduce_early(got))
        net.reduce_early_done(got)
    else:
        dmerged = _mm(dx1, W["w_out"], "nt", T, SW + SBW, D, (f32,), "dmerged")
        dqn, dkn, dvv, _ = _sb_bwd(qn, kn, vb, dmerged, ctot, SW, "sb_attn_bwd")
    dqkv, dg_q, dg_k = _qk_norm_bwd(qkv, dqn, dkn, dvv, W["q_norm"], W["k_norm"], SBW, "qk_norm_bwd")
    dy_ssd, dz, dg_on = _gate_bwd(y_ssd, z, dmerged, W["out_norm"], "gate_bwd")
    dxbc, ddtraw, dalog, dbias, ddskip = _ssd_bwd(xbc, dtraw, sprev, dy_ssd, W["dt_bias"], W["a_log"], W["d_skip"], HS, "ssd_bwd")
    dpre, dconv_w, dconv_b = _conv_bwd_pre(xraw, dxbc, W["conv_w"], W["conv_b"], "conv_bwd_pre")
    dxraw = _conv_bwd_in(dpre, W["conv_w"], "conv_bwd_in")
    dw_in = [_mm(h0, dz, "tn", D, SW, T, (f32,), "dwin_z"), _mm(h0, dxraw, "tn", D, CD, T, (f32,), "dwin_xbc"),
             _mm(h0, ddtraw, "tn", D, LANES, T, (f32,), "dwin_dt")[:, :HS], _mm(h0, dqkv, "tn", D, 3 * SBW, T, (f32,), "dwin_qkv")]
    dh0 = _mm(dz, W["w_z"], "nt", T, D, SW, (f32,), "dh0_z")
    dh0 = _mm(dxraw, W["w_xbc"], "nt", T, D, CD, (f32,), "dh0_xbc", epilogue=add, extras=((dh0, "tile"),))
    dh0 = _mm(ddtraw, W["w_dt"], "nt", T, D, LANES, (f32,), "dh0_dt", epilogue=add, extras=((dh0, "tile"),))
    if net:
        (dh0,), got = _mm(dqkv, W["w_qkv"], "nt", T, D, 3 * SBW, (f32,), "dh0_qkv", epilogue=add, extras=((dh0, "tile"),),
                          comm=net.reduce_late(dw_in))
        net.reduce_late_done(got)
    else:
        dh0 = _mm(dqkv, W["w_qkv"], "nt", T, D, 3 * SBW, (f32,), "dh0_qkv", epilogue=add, extras=((dh0, "tile"),))
    grad_x, dg_hyb = _rms_bwd(xc, W["hyb_norm"], dh0, dx1, "hyb_dnorm")
    grads = dict(w_in=dw_in, w_out=dw_out, w_pool=dw_pool, w_up=(dw_up0, dw_up1), w_down=(dw_down0, dw_down1),
                 hyb_norm=dg_hyb, conv_w=dconv_w, conv_b=dconv_b, dt_bias=dbias, a_log=dalog, d_skip=ddskip, out_norm=dg_on,
                 q_norm=dg_q, k_norm=dg_k, mlp_norm=(dg_mlp0, dg_mlp1), pool_norm=dg_pool, pool_b=db_pool, pool_scale=dsc_pool)
    return sq, grad_x, grads


def _stack_columns(pieces, n):
    cs = sum(p.shape[1] for p in pieces) // n
    slots = []
    for j in range(n):
        parts, off = [], 0
        for p in pieces:
            lo, hi = max(j * cs, off), min((j + 1) * cs, off + p.shape[1])
            if lo < hi:
                parts.append(p[:, lo - off:hi - off])
            off += p.shape[1]
        slots.append(parts[0] if len(parts) == 1 else jnp.concatenate(parts, axis=1))
    return jnp.stack(slots)


class _Net:
    def __init__(self, own_first, own_last, chip, dims):
        self.own, self.own_last, self.chip, self.dims = own_first, own_last, chip, dims

    def _place_own(self, stacks, own):
        return [lax.dynamic_update_index_in_dim(g, o, self.chip, 0) for g, o in zip(stacks, own)]

    def rest_ici(self):
        return _gather_ici(self.own)

    def rest_d2d(self, got):
        return _gather_d2d(list(got))

    def rest_weights(self, got):
        d, nw = self.dims, len(POOL_WINDOWS)
        D, F, PG = d["D"], d["F"], d["PG"]
        fs = F // N_CHIPS
        g_out, g_pool, g_up, g_down = self._place_own(got, self.own)
        w_pool = g_pool.reshape(N_CHIPS, nw, PG // N_CHIPS, PG).transpose(1, 0, 2, 3).reshape(nw, PG, PG)
        return dict(w_out=g_out.reshape(d["MIX"], D), w_pool=w_pool, F=F,
                    w_up=[(g_up, ("cols", fs, 0, D))], w_down=[(g_down, ("rows", fs, 0, None))])

    def last_ici(self, which):
        return _gather_ici([self.own_last[which]])

    def last_weights(self, got, w_up, w_down):
        d = self.dims
        fs = d["F"] // N_CHIPS
        stacks = _run_exchange(_gather_d2d([got[0][0], got[1][0]]), "gather_last_d2d")
        g_up, g_down = self._place_own(stacks, self.own_last)
        return w_up + [(g_up, ("cols", fs, 0, d["D"]))], w_down + [(g_down, ("rows", fs, 0, None))]

    def mlp_to(self, which, layer, earlier):
        d = self.dims
        fs = d["F"] // N_CHIPS
        if which == "up":
            return ("cols", fs, layer, d["D"]), (N_CHIPS, d["NL"] * d["D"], fs), earlier
        return ("rows", fs, layer, None), (N_CHIPS, d["NL"] * fs, d["D"]), earlier

    def early_swap(self, dw_out, dw_pool, g_up, g_down):
        d, nw = self.dims, len(POOL_WINDOWS)
        PG = d["PG"]
        self.early_g4 = [dw_out.reshape(N_CHIPS, d["MIX"] // N_CHIPS, d["D"]),
                         dw_pool.reshape(nw, N_CHIPS, PG // N_CHIPS, PG).transpose(1, 0, 2, 3).reshape(N_CHIPS, PG, PG),
                         g_up, g_down]
        return _swap_halves(self.early_g4)

    def reduce_early(self, recv):
        self.early_recv = list(recv)
        sent = [_pair_sum(g, r, f"grads_early_pair_sum{i}") for i, (g, r) in enumerate(zip(self.early_g4, self.early_recv))]
        return _exchange_chips(sent)

    def reduce_early_done(self, got):
        self.early_got = list(got)

    def reduce_late(self, dw_in):
        self.late_g4 = [_stack_columns(dw_in, N_CHIPS)]
        self.late_recv = list(_run_exchange(_swap_halves(self.late_g4), "grads_late_pair_swap"))
        return _exchange_chips([_pair_sum(self.late_g4[0], self.late_recv[0], "grads_late_pair_sum")])

    def reduce_late_done(self, got):
        self.late_got = list(got)

    def reduced(self):
        halves = [_chip_sum(g, r, p, f"grads_chip_sum{i}") for i, (g, r, p) in enumerate(zip(
            self.late_g4 + self.early_g4, self.late_recv + self.early_recv, self.late_got + self.early_got))]
        return _run_exchange(_join_halves(halves), "grads_join")


def kernel(x, hyb_norm, hyb_w_in, ssd_conv_w, ssd_conv_b, ssd_dt_bias, ssd_a_log, ssd_d, ssd_out_norm, sb_q_norm, sb_k_norm, hyb_w_out, pool_norm, pool_w, pool_b, pool_scale, mlp_norm, mlp_w_up, mlp_w_down, loss_target, m_hyb_norm, m_hyb_w_in, m_ssd_conv_w, m_ssd_conv_b, m_ssd_dt_bias, m_ssd_a_log, m_ssd_d, m_ssd_out_norm, m_sb_q_norm, m_sb_k_norm, m_hyb_w_out, m_pool_norm, m_pool_w, m_pool_b, m_pool_scale, m_mlp_norm, m_mlp_w_up, m_mlp_w_down, v_hyb_norm, v_hyb_w_in, v_ssd_conv_w, v_ssd_conv_b, v_ssd_dt_bias, v_ssd_a_log, v_ssd_d, v_ssd_out_norm, v_sb_q_norm, v_sb_k_norm, v_hyb_w_out, v_pool_norm, v_pool_w, v_pool_b, v_pool_scale, v_mlp_norm, v_mlp_w_up, v_mlp_w_down):
    T, D = x.shape[1], x.shape[2]
    HS = ssd_dt_bias.shape[-1]
    SW = HS * SSD_HEAD_DIM
    CD = ssd_conv_b.shape[-1]
    IN = N_CHIPS * hyb_w_in.shape[-1]
    SBW = (IN - SW - CD - HS) // 3
    F = N_CHIPS * mlp_w_up.shape[-1]
    NL = mlp_norm.shape[0]
    PG = D // len(POOL_WINDOWS)
    xc, tgt = x[0], loss_target[0]
    ix, iy, ic = lax.axis_index("x"), lax.axis_index("y"), lax.axis_index("c")
    chip = (2 * ix + iy).astype(jnp.int32)

    small = jnp.concatenate([ssd_conv_w.reshape(-1), pool_norm.reshape(-1), pool_b.reshape(-1), pool_scale.reshape(-1)])
    ns = small.shape[0]
    ns8 = -(-ns // (8 * LANES)) * LANES
    gathered, _ = _gather_small(jnp.pad(small, (0, 8 * ns8 - ns)).reshape(8, ns8), "gather_small")
    per_chip = gathered.reshape(N_CHIPS, 2, 8 * ns8)[:, 0, :ns]
    cw = CD // N_CHIPS
    conv_w = per_chip[:, :4 * cw].reshape(N_CHIPS, 4, cw).transpose(1, 0, 2).reshape(4, CD)
    pvec = per_chip[:, 4 * cw:].reshape(N_CHIPS, 3, PG)
    pool_norm_f, pool_b_f, pool_scale_f = (pvec[:, i].reshape(1, D) for i in range(3))

    fs = F // N_CHIPS
    own_in = hyb_w_in[0].astype(bf16)
    g_in = _run_exchange(_gather_d2d(_run_exchange(_gather_ici([own_in]), "gather_in_ici")), "gather_in_d2d")[0]
    w_in = lax.dynamic_update_index_in_dim(g_in, own_in, chip, 0).transpose(1, 0, 2).reshape(D, IN)
    c1, c2, c3 = SW, SW + CD, SW + CD + HS
    w_z, w_xbc, w_dt, w_qkv = w_in[:, :c1], w_in[:, c1:c2], _pad_lanes(w_in[:, c2:c3]), w_in[:, c3:]
    dt_bias_p, a_log_p, d_skip_p = _pad_lanes(ssd_dt_bias), _pad_lanes(ssd_a_log), jnp.repeat(ssd_d, SSD_HEAD_DIM, axis=-1)

    assert NL == 2
    own_first = [hyb_w_out[0].astype(bf16), pool_w[0].reshape(-1, PG).astype(bf16),
                 mlp_w_up[0].astype(bf16), mlp_w_down[0].astype(bf16)]
    own_last = [mlp_w_up[1].astype(bf16), mlp_w_down[1].astype(bf16)]
    net = _Net(own_first, own_last, chip, dict(D=D, F=F, NL=NL, PG=PG, IN=IN, MIX=SW + SBW))
    first = dict(hyb_norm=hyb_norm, w_z=w_z, w_xbc=w_xbc, w_dt=w_dt, w_qkv=w_qkv, conv_w=conv_w, conv_b=ssd_conv_b,
                 dt_bias=dt_bias_p, a_log=a_log_p, d_skip=d_skip_p, out_norm=ssd_out_norm, q_norm=sb_q_norm, k_norm=sb_k_norm,
                 pool_norm=pool_norm_f, pool_b=pool_b_f, pool_scale=pool_scale_f, mlp_norm=mlp_norm)
    sq, grad_x, gr = _local_step(xc, tgt, first, HS, SBW, net)
    loss = lax.psum(sq[0, 0] * (0.5 / D), ("x", "y", "c"))
    dg_hyb, dconv_b, dbias, dalog, ddskip, dg_on, dg_q, dg_k = (gr[k] for k in (
        "hyb_norm", "conv_b", "dt_bias", "a_log", "d_skip", "out_norm", "q_norm", "k_norm"))
    (dg_mlp0, dg_mlp1), dconv_w, dg_pool, db_pool, dsc_pool = gr["mlp_norm"], gr["conv_w"], gr["pool_norm"], gr["pool_b"], gr["pool_scale"]
    gb_in, gb_out, gb_pool, gb_up, gb_down = net.reduced()

    full_small = [dg_hyb, dconv_b, dbias[:, :HS], dalog[:, :HS], ddskip[:, :HS], dg_on, dg_q, dg_k,
                  jnp.concatenate([dg_mlp0, dg_mlp1], axis=0).reshape(1, -1),
                  dconv_w.reshape(1, -1), dg_pool, db_pool, dsc_pool]
    sizes = [v.shape[-1] for v in full_small]
    packed = jnp.concatenate([v.reshape(-1) for v in full_small])
    npk = packed.shape[0]
    npk8 = -(-npk // (8 * LANES)) * LANES
    _, summed = _gather_small(jnp.pad(packed, (0, 8 * npk8 - npk)).reshape(8, npk8), "grads_small")
    summed = summed.reshape(-1)[:npk]
    offs = [0]
    for s in sizes:
        offs.append(offs[-1] + s)
    (g_hyb_norm, g_conv_b, g_dt_bias, g_a_log, g_d, g_out_norm, g_q_norm, g_k_norm, g_mlp_norm, g_conv_w_full,
     g_pool_norm_full, g_pool_b_full, g_pool_scale_full) = (summed[offs[i]:offs[i + 1]] for i in range(len(sizes)))
    take = lambda full, n: lax.dynamic_slice_in_dim(full.reshape(-1, N_CHIPS, n), chip, 1, axis=1)
    small_grads = {
        "hyb_norm": g_hyb_norm.reshape(hyb_norm.shape), "ssd_conv_w": take(g_conv_w_full, cw).reshape(ssd_conv_w.shape),
        "ssd_conv_b": g_conv_b.reshape(ssd_conv_b.shape), "ssd_dt_bias": g_dt_bias.reshape(ssd_dt_bias.shape),
        "ssd_a_log": g_a_log.reshape(ssd_a_log.shape), "ssd_d": g_d.reshape(ssd_d.shape),
        "ssd_out_norm": g_out_norm.reshape(ssd_out_norm.shape), "sb_q_norm": g_q_norm.reshape(sb_q_norm.shape),
        "sb_k_norm": g_k_norm.reshape(sb_k_norm.shape), "pool_norm": take(g_pool_norm_full, PG).reshape(pool_norm.shape),
        "pool_b": take(g_pool_b_full, PG).reshape(pool_b.shape), "pool_scale": take(g_pool_scale_full, PG).reshape(pool_scale.shape),
        "mlp_norm": g_mlp_norm.reshape(mlp_norm.shape),
    }

    weights = dict(hyb_norm=hyb_norm, hyb_w_in=hyb_w_in, ssd_conv_w=ssd_conv_w, ssd_conv_b=ssd_conv_b, ssd_dt_bias=ssd_dt_bias,
                   ssd_a_log=ssd_a_log, ssd_d=ssd_d, ssd_out_norm=ssd_out_norm, sb_q_norm=sb_q_norm, sb_k_norm=sb_k_norm,
                   hyb_w_out=hyb_w_out, pool_norm=pool_norm, pool_w=pool_w, pool_b=pool_b, pool_scale=pool_scale,
                   mlp_norm=mlp_norm, mlp_w_up=mlp_w_up, mlp_w_down=mlp_w_down)
    moms = dict(hyb_norm=m_hyb_norm, hyb_w_in=m_hyb_w_in, ssd_conv_w=m_ssd_conv_w, ssd_conv_b=m_ssd_conv_b, ssd_dt_bias=m_ssd_dt_bias,
                ssd_a_log=m_ssd_a_log, ssd_d=m_ssd_d, ssd_out_norm=m_ssd_out_norm, sb_q_norm=m_sb_q_norm, sb_k_norm=m_sb_k_norm,
                hyb_w_out=m_hyb_w_out, pool_norm=m_pool_norm, pool_w=m_pool_w, pool_b=m_pool_b, pool_scale=m_pool_scale,
                mlp_norm=m_mlp_norm, mlp_w_up=m_mlp_w_up, mlp_w_down=m_mlp_w_down)
    vels = dict(hyb_norm=v_hyb_norm, hyb_w_in=v_hyb_w_in, ssd_conv_w=v_ssd_conv_w, ssd_conv_b=v_ssd_conv_b, ssd_dt_bias=v_ssd_dt_bias,
                ssd_a_log=v_ssd_a_log, ssd_d=v_ssd_d, ssd_out_norm=v_ssd_out_norm, sb_q_norm=v_sb_q_norm, sb_k_norm=v_sb_k_norm,
                hyb_w_out=v_hyb_w_out, pool_norm=v_pool_norm, pool_w=v_pool_w, pool_b=v_pool_b, pool_scale=v_pool_scale,
                mlp_norm=v_mlp_norm, mlp_w_up=v_mlp_w_up, mlp_w_down=v_mlp_w_down)
    order = list(weights)
    grads, delta, new_m, new_v = {}, {}, {}, {}
    for name, g2 in (("hyb_w_in", gb_in), ("hyb_w_out", gb_out), ("pool_w", gb_pool), ("mlp_w_up", gb_up), ("mlp_w_down", gb_down)):
        shp = weights[name].shape
        d_, m_, v_, g_ = _adamw(weights[name].reshape(g2.shape), g2, moms[name].reshape(g2.shape), vels[name].reshape(g2.shape),
                                f"adamw_{name}")
        grads[name], delta[name], new_m[name], new_v[name] = (t.reshape(shp) for t in (g_, d_, m_, v_))
    snames = list(small_grads)
    pack = lambda d: jnp.concatenate([d[n].reshape(-1) for n in snames])
    nsm = sum(small_grads[n].size for n in snames)
    cols = -(-nsm // (8 * LANES)) * LANES
    as_blk = lambda v: jnp.pad(v, (0, 8 * cols - nsm)).reshape(8, cols)
    padded_v = jnp.pad(pack(vels), (0, 8 * cols - nsm), constant_values=1.0).reshape(8, cols)
    d_, m_, v_, _ = _adamw(as_blk(pack(weights)), as_blk(pack(small_grads)), as_blk(pack(moms)), padded_v, "adamw_small")
    off = 0
    for n in snames:
        sz, shp = small_grads[n].size, weights[n].shape
        grads[n] = small_grads[n]
        delta[n], new_m[n], new_v[n] = (t.reshape(-1)[off:off + sz].reshape(shp) for t in (d_, m_, v_))
        off += sz

    return (loss, grad_x.reshape(x.shape), *[grads[n] for n in order], *[delta[n] for n in order],
            *[new_m[n] for n in order], *[new_v[n] for n in order])
```

```python
import functools
import math

import jax
import jax.numpy as jnp
from jax import lax
from jax.experimental import pallas as pl
from jax.experimental.pallas import tpu as pltpu

f32 = jnp.float32
bf16 = jnp.bfloat16

EPS = 1e-6
SSD_HEAD_DIM = 64
SSD_STATE = 128
SSD_GROUPS = 4
SSD_CHUNK = 128
LANES = 128
SB_HEAD_DIM = 128
POOL_WINDOWS = (2, 4, 8, 16)
POOL_HALO = 16
CONV_HALO = 8
ADAM_LR, ADAM_B1, ADAM_B2, ADAM_EPS, ADAM_WD, ADAM_STEP = 0.001, 0.9, 0.999, 1e-08, 0.01, 10
VMEM_LIMIT = 56 * 1024 * 1024
MM_TILE_BUDGET = 40 * 1024 * 1024
N_CHIPS = 4
MESH = pl.DeviceIdType.MESH

_DIMS = {"nn": (((1,), (0,)), ((), ())), "nt": (((1,), (1,)), ((), ())), "tn": (((0,), (0,)), ((), ()))}


def _fit(n, t):
    if n <= t:
        return n
    return max(d for d in range(LANES, t + 1, LANES) if n % d == 0)


def _cp(*sem):
    return pltpu.CompilerParams(dimension_semantics=sem, vmem_limit_bytes=VMEM_LIMIT)


def _sigmoid(v):
    return 1.0 / (1.0 + jnp.exp(-v))


def _softplus(v):
    return jnp.maximum(v, 0.0) + jnp.log(1.0 + jnp.exp(-jnp.abs(v)))


def _split(v, parts):
    out, rem = [], v
    for _ in range(parts):
        p = rem.astype(bf16)
        out.append(p)
        rem = rem - p.astype(f32)
    return out


def _dot(a, b, mode="nn"):
    return lax.dot_general(a, b, _DIMS[mode], preferred_element_type=f32)


def _mask_dot(mask_b, v, parts):
    return _dot(jnp.concatenate([mask_b] * parts, axis=1), jnp.concatenate(_split(v, parts), axis=0))


def _dot_mask(v, mask_b, parts):
    return _dot(jnp.concatenate(_split(v, parts), axis=1), jnp.concatenate([mask_b] * parts, axis=0))


def _stacked(view, br, bc, rmap, cmap):
    kind, per, layer, rows_per_layer = view
    if kind == "cols":
        npc = per // bc
        return pl.BlockSpec((None, br, bc), lambda i, j, k: (cmap(i, j, k) // npc, layer * (rows_per_layer // br) + rmap(i, j, k),
                                                              cmap(i, j, k) % npc))
    npc = per // br
    return pl.BlockSpec((None, br, bc), lambda i, j, k: (rmap(i, j, k) // npc, layer * npc + rmap(i, j, k) % npc, cmap(i, j, k)))


def _pick_tiles(M, N, K, caps, a_bytes, b_bytes, io_bytes):
    def cands(n, cap, sizes):
        got = [s for s in sizes if s <= min(n, cap) and n % s == 0]
        return got or [_fit(n, min(n, cap))]

    best = None
    for tk in cands(K, caps[2], (8192, 4096, 2048, 1024, 512, 256, 128)):
        for tm in cands(M, caps[0], (1024, 512, 256, 128)):
            for tn in cands(N, caps[1], (1024, 512, 256, 128)):
                need = 2 * (tm * tk * a_bytes + tk * tn * b_bytes) + tm * tn * (2 * io_bytes + (4 if tk < K else 0))
                key = (need <= MM_TILE_BUDGET, tk, tm * tn, tm)
                if best is None or key > best[0]:
                    best = (key, (tm, tn, tk))
    return best[1]


def _mm(a, b, mode, M, N, K, outs, name, epilogue=None, extras=(), a_off=(0, 0), b_off=(0, 0),
        b_view=None, out_view=None, out_stack=None, alias=None, comm=None):
    caps = [M, N, K]
    if b_view is not None:
        caps[1 if (b_view[0] == "cols") == (mode != "nt") else 2] = b_view[1]
    if out_view is not None:
        d = 1 if out_view[0] == "cols" else 0
        caps[d] = min(caps[d], out_view[1])
    for off, dims in ((a_off, (2, 0) if mode == "tn" else (0, 2)), (b_off, (1, 2) if mode == "nt" else (2, 1))):
        for o, d in zip(off, dims):
            if o:
                caps[d] = min(caps[d], math.gcd(o, caps[d]))
    io_bytes = sum(jnp.dtype(dt).itemsize for dt in outs) + sum(e[0].dtype.itemsize for e in extras if e[1] == "tile")
    tm, tn, tk = _pick_tiles(M, N, K, caps, a.dtype.itemsize, b.dtype.itemsize, io_bytes)
    nk = K // tk
    if mode == "tn":
        a_blk, ad = (tk, tm), (tk, tm)
    else:
        a_blk, ad = (tm, tk), (tm, tk)
    b_blk = (tn, tk) if mode == "nt" else (tk, tn)
    assert a_off[0] % ad[0] == 0 and a_off[1] % ad[1] == 0 and b_off[0] % b_blk[0] == 0 and b_off[1] % b_blk[1] == 0
    ao = (a_off[0] // ad[0], a_off[1] // ad[1])
    bo = (b_off[0] // b_blk[0], b_off[1] // b_blk[1])
    if mode == "tn":
        a_map = lambda i, j, k: (k + ao[0], i + ao[1])
    else:
        a_map = lambda i, j, k: (i + ao[0], k + ao[1])
    if mode == "nt":
        b_map = lambda i, j, k: (j + bo[0], k + bo[1])
    else:
        b_map = lambda i, j, k: (k + bo[0], j + bo[1])
    if b_view is not None:
        if mode == "nt":
            b_spec = _stacked(b_view, tn, tk, lambda i, j, k: j, lambda i, j, k: k)
        else:
            b_spec = _stacked(b_view, tk, tn, lambda i, j, k: k, lambda i, j, k: j)
    else:
        b_spec = pl.BlockSpec(b_blk, b_map)
    in_specs = [pl.BlockSpec(a_blk, a_map), b_spec]
    for arr, kind in extras:
        if kind == "tile":
            in_specs.append(pl.BlockSpec((tm, tn), lambda i, j, k: (i, j)))
        else:
            in_specs.append(pl.BlockSpec((1, tn), lambda i, j, k: (0, j)))
    ne, no = len(extras), len(outs)
    if epilogue is None:
        epilogue = lambda acc: (acc,)
    operands = [a, b, *[e[0] for e in extras]]
    aliases = {}
    if alias is not None:
        in_specs.append(ANY)
        aliases[len(operands)] = 0
        operands.append(alias)
    n_in = len(operands)
    if out_view is not None:
        out_specs = [_stacked(out_view, tm, tn, lambda i, j, k: i, lambda i, j, k: j)]
        out_shape = [jax.ShapeDtypeStruct(out_stack, outs[0])]
    else:
        out_specs = [pl.BlockSpec((tm, tn), lambda i, j, k: (i, j)) for _ in outs]
        out_shape = [jax.ShapeDtypeStruct((M, N), dt) for dt in outs]
    scratch = [pltpu.VMEM((tm, tn), f32)] if nk > 1 else []
    grid = (M // tm, N // tn, nk)
    if comm is not None:
        in_specs += [ANY] * len(comm["ins"])
        operands += comm["ins"]
        out_specs += [ANY] * len(comm["outs"])
        out_shape += comm["outs"]
        scratch += comm["sems"]
    nci, nco, ncs = (len(comm["ins"]), len(comm["outs"]), len(comm["sems"])) if comm is not None else (0, 0, 0)

    def body(*refs):
        a_ref, b_ref = refs[0], refs[1]
        ex, out_refs = refs[2:2 + ne], refs[n_in + nci:n_in + nci + no]
        rest = refs[n_in + nci + no + nco:]
        if comm is not None:
            cargs = (refs[n_in:n_in + nci], refs[n_in + nci + no:n_in + nci + no + nco], refs[len(refs) - ncs:])
            pid = [pl.program_id(d) for d in range(3)]

            @pl.when((pid[0] == 0) & (pid[1] == 0) & (pid[2] == 0))
            def _():
                comm["start"](*cargs)

        def finish(acc):
            res = epilogue(acc, *[e[...] for e in ex])
            for o, r in zip(out_refs, res):
                o[...] = r.astype(o.dtype)

        prod = lax.dot_general(a_ref[...].astype(bf16), b_ref[...].astype(bf16), _DIMS[mode],
                               preferred_element_type=f32)
        if nk == 1:
            finish(prod)
        else:
            acc_ref = rest[0]
            k = pl.program_id(2)

            @pl.when(k == 0)
            def _():
                acc_ref[...] = prod

            @pl.when(k > 0)
            def _():
                acc_ref[...] += prod

            @pl.when(k == nk - 1)
            def _():
                finish(acc_ref[...])

        if comm is not None:
            @pl.when((pid[0] == grid[0] - 1) & (pid[1] == grid[1] - 1) & (pid[2] == grid[2] - 1))
            def _():
                comm["finish"](*cargs)

    sem = ("arbitrary",) * 3 if comm is not None else ("parallel", "parallel", "arbitrary")
    res = pl.pallas_call(
        body, name=name, grid=grid, in_specs=in_specs, out_specs=out_specs, out_shape=out_shape,
        scratch_shapes=scratch, input_output_aliases=aliases, compiler_params=_cp(*sem),
    )(*operands)
    if comm is not None:
        return res[:no], res[no:]
    return res[0] if no == 1 else res


def _rms_fwd(x, g, out_dtype, name, tr=256):
    T, D = x.shape
    tr = min(tr, T)

    def body(x_ref, g_ref, o_ref):
        xv = x_ref[...]
        r = lax.rsqrt(jnp.mean(xv * xv, axis=-1, keepdims=True) + EPS)
        o_ref[...] = (xv * r * g_ref[...]).astype(o_ref.dtype)

    return pl.pallas_call(
        body, name=name, grid=(T // tr,),
        in_specs=[pl.BlockSpec((tr, D), lambda r: (r, 0)), pl.BlockSpec((1, D), lambda r: (0, 0))],
        out_specs=pl.BlockSpec((tr, D), lambda r: (r, 0)),
        out_shape=jax.ShapeDtypeStruct((T, D), out_dtype), compiler_params=_cp("parallel"),
    )(x, g)


def _rms_bwd(x, g, dh, dres, name, tr=256):
    T, D = x.shape
    tr = min(tr, T)

    def body(x_ref, g_ref, dh_ref, dres_ref, dx_ref, dg_ref):
        xv = x_ref[...]
        r = lax.rsqrt(jnp.mean(xv * xv, axis=-1, keepdims=True) + EPS)
        xh = xv * r
        dhv = dh_ref[...]
        dhg = dhv * g_ref[...]
        dx_ref[...] = dres_ref[...] + r * (dhg - xh * jnp.mean(dhg * xh, axis=-1, keepdims=True))

        @pl.when(pl.program_id(0) == 0)
        def _():
            dg_ref[...] = jnp.zeros_like(dg_ref)

        dg_ref[...] += jnp.sum(dhv * xh, axis=0, keepdims=True)

    row = pl.BlockSpec((tr, D), lambda r: (r, 0))
    vec = pl.BlockSpec((1, D), lambda r: (0, 0))
    return pl.pallas_call(
        body, name=name, grid=(T // tr,), in_specs=[row, vec, row, row], out_specs=[row, vec],
        out_shape=[jax.ShapeDtypeStruct((T, D), f32), jax.ShapeDtypeStruct((1, D), f32)],
        compiler_params=_cp("arbitrary"),
    )(x, g, dh, dres)


def _loss_grad(y, tgt, name, tr=256):
    T, D = y.shape
    tr = min(tr, T)

    def body(y_ref, t_ref, dy_ref, s_ref):
        e = y_ref[...] - t_ref[...]
        dy_ref[...] = e * (1.0 / D)

        @pl.when(pl.program_id(0) == 0)
        def _():
            s_ref[...] = jnp.zeros_like(s_ref)

        s_ref[...] += jnp.sum(e * e)

    row = pl.BlockSpec((tr, D), lambda r: (r, 0))
    return pl.pallas_call(
        body, name=name, grid=(T // tr,), in_specs=[row, row],
        out_specs=[row, pl.BlockSpec((8, LANES), lambda r: (0, 0))],
        out_shape=[jax.ShapeDtypeStruct((T, D), f32), jax.ShapeDtypeStruct((8, LANES), f32)],
        compiler_params=_cp("arbitrary"),
    )(y, tgt)


def _shift_down(cur, prev, s):
    rolled = pltpu.roll(cur, s, 0)
    top = pltpu.roll(prev, s, 0)
    row = lax.broadcasted_iota(jnp.int32, top.shape, 0)
    head = jnp.where(row < s, top, rolled[0:CONV_HALO])
    return jnp.concatenate([head, rolled[CONV_HALO:]], axis=0)


def _shift_up(cur, nxt, s):
    n = cur.shape[0]
    rolled = pltpu.roll(cur, n - s, 0)
    bot = pltpu.roll(nxt, CONV_HALO - s, 0)
    row = lax.broadcasted_iota(jnp.int32, bot.shape, 0)
    tail = jnp.where(row >= CONV_HALO - s, bot, rolled[n - CONV_HALO:])
    return jnp.concatenate([rolled[:n - CONV_HALO], tail], axis=0)


def _conv_pre(cur, prev, w_ref, b_ref):
    taps = [cur] + [_shift_down(cur, prev, s) for s in (1, 2, 3)]
    pre = b_ref[...] + w_ref[3:4, :] * taps[0]
    for s in (1, 2, 3):
        pre = pre + w_ref[3 - s:4 - s, :] * taps[s]
    return pre, taps


def _conv_specs(T, C, rc, cb):
    cur = pl.BlockSpec((rc, cb), lambda j, r: (r, j))
    prev = pl.BlockSpec((CONV_HALO, cb), lambda j, r: (jnp.maximum(r * (rc // CONV_HALO) - 1, 0), j))
    nxt = pl.BlockSpec((CONV_HALO, cb), lambda j, r: (jnp.minimum((r + 1) * (rc // CONV_HALO), T // CONV_HALO - 1), j))
    w = pl.BlockSpec((4, cb), lambda j, r: (0, j))
    b = pl.BlockSpec((1, cb), lambda j, r: (0, j))
    return cur, prev, nxt, w, b


def _conv_fwd(xraw, w, b, name):
    T, C = xraw.shape
    rc, cb = min(512, T), min(512, C)
    cur, prev, _, ws, bs = _conv_specs(T, C, rc, cb)

    def body(x_ref, p_ref, w_ref, b_ref, o_ref):
        pv = jnp.where(pl.program_id(1) > 0, p_ref[...], 0.0)
        pre, _ = _conv_pre(x_ref[...], pv, w_ref, b_ref)
        o_ref[...] = pre * _sigmoid(pre)

    return pl.pallas_call(
        body, name=name, grid=(C // cb, T // rc), in_specs=[cur, prev, ws, bs], out_specs=cur,
        out_shape=jax.ShapeDtypeStruct((T, C), f32), compiler_params=_cp("parallel", "parallel"),
    )(xraw, xraw, w, b)


def _conv_bwd_pre(xraw, dxbc, w, b, name):
    T, C = xraw.shape
    rc, cb = min(512, T), min(512, C)
    cur, prev, _, ws, bs = _conv_specs(T, C, rc, cb)

    def body(x_ref, p_ref, d_ref, w_ref, b_ref, dpre_ref, dw_ref, db_ref):
        pv = jnp.where(pl.program_id(1) > 0, p_ref[...], 0.0)
        pre, taps = _conv_pre(x_ref[...], pv, w_ref, b_ref)
        sg = _sigmoid(pre)
        dpre = d_ref[...] * (sg * (1.0 + pre * (1.0 - sg)))
        dpre_ref[...] = dpre

        @pl.when(pl.program_id(1) == 0)
        def _():
            dw_ref[...] = jnp.zeros_like(dw_ref)
            db_ref[...] = jnp.zeros_like(db_ref)

        row = lax.broadcasted_iota(jnp.int32, dw_ref.shape, 0)
        upd = jnp.zeros(dw_ref.shape, f32)
        for s in range(4):
            upd = upd + jnp.where(row == 3 - s, jnp.sum(dpre * taps[s], axis=0, keepdims=True), 0.0)
        dw_ref[...] += upd
        db_ref[...] += jnp.sum(dpre, axis=0, keepdims=True)

    return pl.pallas_call(
        body, name=name, grid=(C // cb, T // rc), in_specs=[cur, prev, cur, ws, bs], out_specs=[cur, ws, bs],
        out_shape=[jax.ShapeDtypeStruct((T, C), f32), jax.ShapeDtypeStruct((4, C), f32), jax.ShapeDtypeStruct((1, C), f32)],
        compiler_params=_cp("parallel", "arbitrary"),
    )(xraw, xraw, dxbc, w, b)


def _conv_bwd_in(dpre, w, name):
    T, C = dpre.shape
    rc, cb = min(512, T), min(512, C)
    cur, _, nxt, ws, _ = _conv_specs(T, C, rc, cb)
    nr = T // rc

    def body(d_ref, n_ref, w_ref, o_ref):
        nv = jnp.where(pl.program_id(1) < nr - 1, n_ref[...], 0.0)
        cv = d_ref[...]
        out = w_ref[3:4, :] * cv
        for s in (1, 2, 3):
            out = out + w_ref[3 - s:4 - s, :] * _shift_up(cv, nv, s)
        o_ref[...] = out.astype(o_ref.dtype)

    return pl.pallas_call(
        body, name=name, grid=(C // cb, nr), in_specs=[cur, nxt, ws], out_specs=cur,
        out_shape=jax.ShapeDtypeStruct((T, C), bf16), compiler_params=_cp("parallel", "parallel"),
    )(dpre, dpre, w)


HEAD_SHIFT = SSD_HEAD_DIM.bit_length() - 1


def _ssd_prep(dtr_ref, bias_ref, alog_ref, SW):
    L = SSD_CHUNK
    xs = dtr_ref[...] + bias_ref[...]
    dt = _softplus(xs)
    a = -jnp.exp(alog_ref[...])
    causal = lax.broadcasted_iota(jnp.int32, (L, L), 0) >= lax.broadcasted_iota(jnp.int32, (L, L), 1)
    cs = _mask_dot(causal.astype(bf16), dt * a, 3)
    spread = (lax.broadcasted_iota(jnp.int32, (LANES, SW), 0)
              == lax.shift_right_logical(lax.broadcasted_iota(jnp.int32, (LANES, SW), 1), HEAD_SHIFT)).astype(bf16)
    dt_x = _dot_mask(dt, spread, 3)
    cs_x = _dot_mask(cs, spread, 3)
    last_x = cs_x[L - 1:L, :]
    return xs, dt, a, causal, cs, cs.T, dt_x, jnp.exp(cs_x), jnp.exp(last_x - cs_x), jnp.exp(last_x)


def _head_decay(cs, csT, causal, h):
    seg = cs[:, h:h + 1] - csT[h:h + 1, :]
    return jnp.where(causal, jnp.exp(jnp.minimum(seg, 0.0)), 0.0)


def _ssd_fwd(xbc, dtraw, z, dt_bias, a_log, d_skip_x, out_norm, HS, name, comm=None):
    T = xbc.shape[0]
    L, P, NS, G = SSD_CHUNK, SSD_HEAD_DIM, SSD_STATE, SSD_GROUPS
    SW, HPG, nc = HS * P, HS // SSD_GROUPS, T // SSD_CHUNK
    gsz = SW // G
    gw = HPG * P
    assert HPG % 2 == 0 and 2 * P == LANES

    def body(*refs):
        (xbc_ref, dtr_ref, z_ref, bias_ref, alog_ref, dsk_ref, on_ref), (y_ref, yn_ref, sp_ref), (st_ref,), cargs = _hosted(comm, 7, 3, refs)
        first, last = _first_last((nc,))
        if comm is not None:
            @pl.when(first)
            def _():
                comm["start"](*cargs)

            @pl.when(last)
            def _():
                comm["finish"](*cargs)

        @pl.when(pl.program_id(0) == 0)
        def _():
            st_ref[...] = jnp.zeros_like(st_ref)

        sp_ref[0] = st_ref[...]
        _, _, _, causal, cs, csT, dt_x, ecs_x, dte_x, cdec_x = _ssd_prep(dtr_ref, bias_ref, alog_ref, SW)
        X = xbc_ref[:, 0:SW]
        Xd = X * dt_x
        Xdb = Xd.astype(bf16)
        XEb = (Xd * dte_x).astype(bf16)
        left = lax.broadcasted_iota(jnp.int32, (L, LANES), 1) < P
        for g in range(G):
            gs = slice(g * gw, (g + 1) * gw)
            Bb = xbc_ref[:, SW + g * NS:SW + (g + 1) * NS].astype(bf16)
            Cb = xbc_ref[:, SW + (G + g) * NS:SW + (G + g + 1) * NS].astype(bf16)
            Gm = _dot(Cb, Bb, "nt")
            Sp = st_ref[:, gs]
            yo = _dot(Cb, Sp.astype(bf16)) * ecs_x[:, gs]
            st_ref[:, gs] = cdec_x[:, gs] * Sp + _dot(Bb, XEb[:, gs], "tn")
            for pr in range(HPG // 2):
                h0 = g * HPG + 2 * pr
                ps = slice(h0 * P, (h0 + 2) * P)
                xp = Xdb[:, ps]
                yd = jnp.where(left, _dot((_head_decay(cs, csT, causal, h0) * Gm).astype(bf16), xp),
                               _dot((_head_decay(cs, csT, causal, h0 + 1) * Gm).astype(bf16), xp))
                y_ref[:, ps] = yd + yo[:, pr * LANES:(pr + 1) * LANES] + dsk_ref[:, ps] * X[:, ps]
        zz = z_ref[...]
        gated = y_ref[...] * (zz * _sigmoid(zz))
        for g in range(G):
            gs = slice(g * gsz, (g + 1) * gsz)
            sg = gated[:, gs]
            rr = lax.rsqrt(jnp.mean(sg * sg, axis=-1, keepdims=True) + EPS)
            yn_ref[:, gs] = (sg * rr * on_ref[:, gs]).astype(yn_ref.dtype)

    vec = pl.BlockSpec((1, LANES), lambda c: (0, 0))
    wide = pl.BlockSpec((1, SW), lambda c: (0, 0))
    (y, yn, sp), got = _host_call(
        body, name, (nc,),
        [pl.BlockSpec((L, xbc.shape[1]), lambda c: (c, 0)), pl.BlockSpec((L, LANES), lambda c: (c, 0)),
         pl.BlockSpec((L, SW), lambda c: (c, 0)), vec, vec, wide, wide],
        [pl.BlockSpec((L, SW), lambda c: (c, 0)), pl.BlockSpec((L, SW), lambda c: (c, 0)),
         pl.BlockSpec((1, NS, SW), lambda c: (c, 0, 0))],
        [jax.ShapeDtypeStruct((T, SW), f32), jax.ShapeDtypeStruct((T, SW), bf16), jax.ShapeDtypeStruct((nc, NS, SW), f32)],
        [pltpu.VMEM((NS, SW), f32)], [xbc, dtraw, z, dt_bias, a_log, d_skip_x, out_norm], comm)
    return y, yn, sp, got


def _ssd_bwd(xbc, dtraw, sprev, dy, dt_bias, a_log, d_skip_x, HS, name):
    T = xbc.shape[0]
    L, P, NS, G = SSD_CHUNK, SSD_HEAD_DIM, SSD_STATE, SSD_GROUPS
    SW, HPG, nc = HS * P, HS // SSD_GROUPS, T // SSD_CHUNK
    gw = HPG * P

    def body(xbc_ref, dtr_ref, sp_ref, dy_ref, bias_ref, alog_ref, dsk_ref,
             dxbc_ref, ddtr_ref, dalog_ref, dbias_ref, dd_ref, ds_ref):
        @pl.when(pl.program_id(0) == 0)
        def _():
            ds_ref[...] = jnp.zeros_like(ds_ref)
            dalog_ref[...] = jnp.zeros_like(dalog_ref)
            dbias_ref[...] = jnp.zeros_like(dbias_ref)
            dd_ref[...] = jnp.zeros_like(dd_ref)

        xs, dt, a, causal, cs, csT, dt_x, ecs_x, dte_x, cdec_x = _ssd_prep(dtr_ref, bias_ref, alog_ref, SW)
        lane = lax.broadcasted_iota(jnp.int32, (L, LANES), 1)
        sub = lax.broadcasted_iota(jnp.int32, (LANES, L), 0)
        left = lane < P
        dcs = jnp.zeros((L, LANES), f32)
        dcs_t = jnp.zeros((LANES, L), f32)
        xds = jnp.zeros((L, LANES), f32)
        dlast = jnp.zeros((1, LANES), f32)
        dD = jnp.zeros((1, LANES), f32)
        for g in range(G):
            gs = slice(g * gw, (g + 1) * gw)
            bsl = slice(SW + g * NS, SW + (g + 1) * NS)
            csl = slice(SW + (G + g) * NS, SW + (G + g + 1) * NS)
            Bb = xbc_ref[:, bsl].astype(bf16)
            Cb = xbc_ref[:, csl].astype(bf16)
            Gm = _dot(Cb, Bb, "nt")
            X = xbc_ref[:, gs]
            Xd = X * dt_x[:, gs]
            Xdb = Xd.astype(bf16)
            XE = Xd * dte_x[:, gs]
            dY = dy_ref[:, gs]
            dYb = dY.astype(bf16)
            Wb = (dY * ecs_x[:, gs]).astype(bf16)
            Sp = sp_ref[0, :, gs]
            Spb = Sp.astype(bf16)
            dS = ds_ref[:, gs]
            dSb = dS.astype(bf16)
            CS = _dot(Cb, Spb)
            Zb = _dot(Bb, dSb)
            dC = _dot(Wb, Spb, "nt")
            dB = _dot(XE.astype(bf16), dSb, "nt")
            ds_ref[:, gs] = cdec_x[:, gs] * dS + _dot(Cb, Wb, "tn")
            R1 = dY * CS * ecs_x[:, gs]
            R2 = XE * Zb
            to_head = (lax.shift_right_logical(lax.broadcasted_iota(jnp.int32, (gw, LANES), 0), HEAD_SHIFT) + g * HPG
                       == lax.broadcasted_iota(jnp.int32, (gw, LANES), 1)).astype(bf16)
            dcs = dcs + _dot_mask(R1 - R2, to_head, 3)
            dlast = (dlast + jnp.sum(_dot_mask(R2, to_head, 3), axis=0, keepdims=True)
                     + jnp.sum(_dot_mask(Sp * dS * cdec_x[:, gs], to_head, 3), axis=0, keepdims=True))
            dG = jnp.zeros((L, L), f32)
            pieces = []
            for pr in range(HPG // 2):
                h0 = g * HPG + 2 * pr
                pw = slice(pr * LANES, (pr + 1) * LANES)
                xp, dyp = Xdb[:, pw], dYb[:, pw]
                halves = []
                for k, h in enumerate((h0, h0 + 1)):
                    Lm = _head_decay(cs, csT, causal, h)
                    Mf = Lm * Gm
                    keep = left if k == 0 else jnp.logical_not(left)
                    dM = _dot(jnp.where(keep, dyp, jnp.zeros_like(dyp)), xp, "nt")
                    Q = dM * Mf
                    dcs = dcs + jnp.where(lane == h, jnp.sum(Q, axis=1, keepdims=True), 0.0)
                    dcs_t = dcs_t - jnp.where(sub == h, jnp.sum(Q, axis=0, keepdims=True), 0.0)
                    dG = dG + dM * Lm
                    halves.append(_dot(Mf.astype(bf16), dyp, "tn"))
                pieces.append(jnp.where(left, halves[0], halves[1]))
            dXd = jnp.concatenate(pieces, axis=1) + dte_x[:, gs] * Zb
            dxbc_ref[:, gs] = dXd * dt_x[:, gs] + dsk_ref[:, gs] * dY
            xds = xds + _dot_mask(dXd * X, to_head, 3)
            dD = dD + jnp.sum(_dot_mask(dY * X, to_head, 3), axis=0, keepdims=True)
            dGb = dG.astype(bf16)
            dxbc_ref[:, bsl] = dB + _dot(dGb, Cb, "tn")
            dxbc_ref[:, csl] = dC + _dot(dGb, Bb)
        rowi = lax.broadcasted_iota(jnp.int32, (L, LANES), 0)
        dcs = dcs + dcs_t.T + jnp.where(rowi == L - 1, dlast, 0.0)
        anti = (lax.broadcasted_iota(jnp.int32, (L, L), 1) >= lax.broadcasted_iota(jnp.int32, (L, L), 0)).astype(bf16)
        dda = _mask_dot(anti, dcs, 3)
        ddt = dda * a + xds
        dalog_ref[...] += jnp.sum(dda * dt, axis=0, keepdims=True) * a
        ddtr = ddt * _sigmoid(xs)
        ddtr_ref[...] = ddtr
        dbias_ref[...] += jnp.sum(ddtr, axis=0, keepdims=True)
        dd_ref[...] += dD

    rev = lambda c: (nc - 1 - c, 0)
    vec = pl.BlockSpec((1, LANES), lambda c: (0, 0))
    return pl.pallas_call(
        body, name=name, grid=(nc,),
        in_specs=[pl.BlockSpec((L, xbc.shape[1]), rev), pl.BlockSpec((L, LANES), rev),
                  pl.BlockSpec((1, NS, SW), lambda c: (nc - 1 - c, 0, 0)), pl.BlockSpec((L, SW), rev), vec, vec,
                  pl.BlockSpec((1, SW), lambda c: (0, 0))],
        out_specs=[pl.BlockSpec((L, xbc.shape[1]), rev), pl.BlockSpec((L, LANES), rev), vec, vec, vec],
        out_shape=[jax.ShapeDtypeStruct(xbc.shape, f32), jax.ShapeDtypeStruct((T, LANES), f32)]
        + [jax.ShapeDtypeStruct((1, LANES), f32)] * 3,
        scratch_shapes=[pltpu.VMEM((NS, SW), f32)], compiler_params=_cp("arbitrary"),
    )(xbc, dtraw, sprev, dy, dt_bias, a_log, d_skip_x)


def _gate_bwd(y, z, dyn, out_norm, name, tr=256):
    T, SW = y.shape
    tr = min(tr, T)
    gsz = SW // SSD_GROUPS

    def body(y_ref, z_ref, d_ref, on_ref, dy_ref, dz_ref, don_ref):
        @pl.when(pl.program_id(0) == 0)
        def _():
            don_ref[...] = jnp.zeros_like(don_ref)

        for g in range(SSD_GROUPS):
            gs = slice(g * gsz, (g + 1) * gsz)
            yv, zv, dv = y_ref[:, gs], z_ref[:, gs], d_ref[:, gs]
            sg = _sigmoid(zv)
            sl = zv * sg
            gated = yv * sl
            rr = lax.rsqrt(jnp.mean(gated * gated, axis=-1, keepdims=True) + EPS)
            gh = gated * rr
            dgn = dv * on_ref[:, gs]
            dgated = rr * (dgn - gh * jnp.mean(dgn * gh, axis=-1, keepdims=True))
            dy_ref[:, gs] = dgated * sl
            dz_ref[:, gs] = (dgated * yv * (sg * (1.0 + zv * (1.0 - sg)))).astype(dz_ref.dtype)
            don_ref[:, gs] += jnp.sum(dv * gh, axis=0, keepdims=True)

    row = pl.BlockSpec((tr, SW), lambda r: (r, 0))
    vec = pl.BlockSpec((1, SW), lambda r: (0, 0))
    return pl.pallas_call(
        body, name=name, grid=(T // tr,), in_specs=[row, row, row, vec], out_specs=[row, row, vec],
        out_shape=[jax.ShapeDtypeStruct((T, SW), f32), jax.ShapeDtypeStruct((T, SW), bf16),
                   jax.ShapeDtypeStruct((1, SW), f32)],
        compiler_params=_cp("arbitrary"),
    )(y, z, dyn, out_norm)


def _qk_norm_fwd(qkv, qn_w, kn_w, SBW, name, tr=256):
    T = qkv.shape[0]
    tr = min(tr, T)
    nh = SBW // SB_HEAD_DIM

    def body(q_ref, k_ref, v_ref, qw_ref, kw_ref, qo_ref, ko_ref, vo_ref):
        for src, w_ref, dst in ((q_ref, qw_ref, qo_ref), (k_ref, kw_ref, ko_ref)):
            for h in range(nh):
                hs = slice(h * SB_HEAD_DIM, (h + 1) * SB_HEAD_DIM)
                sv = src[:, hs]
                rr = lax.rsqrt(jnp.mean(sv * sv, axis=-1, keepdims=True) + EPS)
                dst[:, hs] = (sv * rr * w_ref[...]).astype(dst.dtype)
        vo_ref[...] = v_ref[...].astype(vo_ref.dtype)

    blk = lambda j: pl.BlockSpec((tr, SBW), lambda r: (r, j))
    vec = pl.BlockSpec((1, SB_HEAD_DIM), lambda r: (0, 0))
    out = pl.BlockSpec((tr, SBW), lambda r: (r, 0))
    return pl.pallas_call(
        body, name=name, grid=(T // tr,), in_specs=[blk(0), blk(1), blk(2), vec, vec], out_specs=[out, out, out],
        out_shape=[jax.ShapeDtypeStruct((T, SBW), bf16)] * 3, compiler_params=_cp("parallel"),
    )(qkv, qkv, qkv, qn_w, kn_w)


def _qk_norm_bwd(qkv, dqn, dkn, dv, qn_w, kn_w, SBW, name, tr=256):
    T = qkv.shape[0]
    tr = min(tr, T)
    nh = SBW // SB_HEAD_DIM

    def body(q_ref, k_ref, dq_ref, dk_ref, dv_ref, qw_ref, kw_ref, o_ref, dqw_ref, dkw_ref):
        @pl.when(pl.program_id(0) == 0)
        def _():
            dqw_ref[...] = jnp.zeros_like(dqw_ref)
            dkw_ref[...] = jnp.zeros_like(dkw_ref)

        for part, (src, d_ref, w_ref, dw_ref) in enumerate(((q_ref, dq_ref, qw_ref, dqw_ref), (k_ref, dk_ref, kw_ref, dkw_ref))):
            dw = jnp.zeros((1, SB_HEAD_DIM), f32)
            for h in range(nh):
                hs = slice(h * SB_HEAD_DIM, (h + 1) * SB_HEAD_DIM)
                os_ = slice(part * SBW + h * SB_HEAD_DIM, part * SBW + (h + 1) * SB_HEAD_DIM)
                sv, dn = src[:, hs], d_ref[:, hs]
                rr = lax.rsqrt(jnp.mean(sv * sv, axis=-1, keepdims=True) + EPS)
                xh = sv * rr
                dg = dn * w_ref[...]
                o_ref[:, os_] = (rr * (dg - xh * jnp.mean(dg * xh, axis=-1, keepdims=True))).astype(o_ref.dtype)
                dw = dw + jnp.sum(dn * xh, axis=0, keepdims=True)
            dw_ref[...] += dw
        o_ref[:, 2 * SBW:] = dv_ref[...].astype(o_ref.dtype)

    blk = lambda j: pl.BlockSpec((tr, SBW), lambda r: (r, j))
    vec = pl.BlockSpec((1, SB_HEAD_DIM), lambda r: (0, 0))
    row = pl.BlockSpec((tr, SBW), lambda r: (r, 0))
    return pl.pallas_call(
        body, name=name, grid=(T // tr,), in_specs=[blk(0), blk(1), row, row, row, vec, vec],
        out_specs=[pl.BlockSpec((tr, 3 * SBW), lambda r: (r, 0)), vec, vec],
        out_shape=[jax.ShapeDtypeStruct((T, 3 * SBW), bf16)] + [jax.ShapeDtypeStruct((1, SB_HEAD_DIM), f32)] * 2,
        compiler_params=_cp("arbitrary"),
    )(qkv, qkv, dqn, dkn, dv, qn_w, kn_w)


def _sb_logits(q, kb, scale):
    zl = _dot(q, kb, "nt") * scale
    lb = jnp.minimum(zl, 0.0) - jnp.log(1.0 + jnp.exp(-jnp.abs(zl)))
    return zl, lb, lb - zl


def _tail_update(old, r0, new_tail):
    return new_tail if r0 == 0 else jnp.concatenate([old[:r0], new_tail], axis=0)


def _hosted(comm, n_in, n_out, refs):
    nci, nco, ncs = (len(comm["ins"]), len(comm["outs"]), len(comm["sems"])) if comm is not None else (0, 0, 0)
    ins, outs = refs[:n_in], refs[n_in + nci:n_in + nci + n_out]
    scratch = refs[n_in + nci + n_out + nco:len(refs) - ncs]
    cargs = (refs[n_in:n_in + nci], refs[n_in + nci + n_out:n_in + nci + n_out + nco], refs[len(refs) - ncs:])
    return ins, outs, scratch, cargs


def _host_call(body, name, grid, in_specs, out_specs, out_shape, scratch, operands, comm):
    n_out = len(out_shape)
    in_specs, out_specs, out_shape, scratch, operands = list(in_specs), list(out_specs), list(out_shape), list(scratch), list(operands)
    io = {}
    if comm is not None:
        for src, dst in comm.get("aliases", {}).items():
            io[len(operands) + src] = n_out + dst
        in_specs += [ANY] * len(comm["ins"])
        operands += comm["ins"]
        out_specs += [ANY] * len(comm["outs"])
        out_shape += comm["outs"]
        scratch += comm["sems"]
    res = pl.pallas_call(body, name=name, grid=grid, in_specs=in_specs, out_specs=out_specs, out_shape=out_shape,
                         scratch_shapes=scratch, input_output_aliases=io,
                         compiler_params=_cp(*(("arbitrary",) * len(grid))))(*operands)
    return res[:n_out], res[n_out:]


def _first_last(grid):
    pid = [pl.program_id(d) for d in range(len(grid))]
    first, last = pid[0] == 0, pid[0] == grid[0] - 1
    for d in range(1, len(grid)):
        first, last = first & (pid[d] == 0), last & (pid[d] == grid[d] - 1)
    return first, last


def _sb_fwd(qn, kn, vb, name, comm=None, tq=2048, tk=256):
    T, W = qn.shape
    tq = min(tq, T)
    tk = min(tk, tq)
    nh, nq, dh, nd = W // SB_HEAD_DIM, T // tq, SB_HEAD_DIM, tq // tk
    scale = dh ** -0.5
    grid = (nh, nq)

    def body(*refs):
        (q_ref, k_ref, v_ref), (o_ref, c_ref), _, cargs = _hosted(comm, 3, 2, refs)
        first, last = _first_last(grid)
        if comm is not None:
            @pl.when(first)
            def _():
                comm["start"](*cargs)

        qi = pl.program_id(1)
        q = q_ref[...]
        later = (lax.broadcasted_iota(jnp.int32, (tk, tk), 0) > lax.broadcasted_iota(jnp.int32, (tk, tk), 1)).astype(bf16)

        def step(j, carry, d):
            acc, run = carry
            r0 = 0 if d is None else d * tk
            ks = pl.multiple_of(j * tk, tk)
            kb, vv = k_ref[pl.ds(ks, tk), :], v_ref[pl.ds(ks, tk), :]
            _, lb, lk = _sb_logits(q[r0:], kb, scale)
            if d is not None:
                mask = lax.broadcasted_iota(jnp.int32, lk.shape, 1) < lax.broadcasted_iota(jnp.int32, lk.shape, 0)
                lk = jnp.where(mask, lk, 0.0)
            between = _dot_mask(lk, later, 2)
            w = jnp.exp(lb + between + run[r0:])
            if d is not None:
                w = jnp.where(mask, w, 0.0)
            return (_tail_update(acc, r0, acc[r0:] + _dot(w.astype(bf16), vv)),
                    _tail_update(run, r0, run[r0:] + between[:, 0:1] + lk[:, 0:1]))

        carry = (jnp.zeros((tq, dh), f32), jnp.zeros((tq, 1), f32))
        for d in range(nd - 1, -1, -1):
            carry = step(qi * nd + d, carry, d)
        n_before = qi * nd
        acc, run = lax.fori_loop(0, n_before, lambda t, c: step(n_before - 1 - t, c, None), carry)
        o_ref[...] = acc.astype(o_ref.dtype)
        c_ref[...] = jnp.broadcast_to(run, (tq, dh))
        if comm is not None:
            @pl.when(last)
            def _():
                comm["finish"](*cargs)

    qblk = pl.BlockSpec((tq, dh), lambda h, i: (i, h))
    full = pl.BlockSpec((T, dh), lambda h, i: (0, h))
    (o, c), got = _host_call(body, name, grid, [qblk, full, full], [qblk, qblk],
                             [jax.ShapeDtypeStruct((T, W), bf16), jax.ShapeDtypeStruct((T, W), f32)], [], [qn, kn, vb], comm)
    return o, c, got


def _sb_bwd(qn, kn, vb, do, ctot, do_off, name, comm=None, tq=2048, tk=256):
    T, W = qn.shape
    tq = min(tq, T)
    tk = min(tk, tq)
    nh, nq, dh, nd = W // SB_HEAD_DIM, T // tq, SB_HEAD_DIM, tq // tk
    scale = dh ** -0.5
    ob = do_off // dh
    grid = (nh, nq)

    def body(*refs):
        (q_ref, k_ref, v_ref, do_ref, c_ref), (dq_ref, dk_ref, dv_ref), _, cargs = _hosted(comm, 5, 3, refs)
        first, last = _first_last(grid)
        if comm is not None:
            @pl.when(first)
            def _():
                comm["start"](*cargs)

        qi = pl.program_id(1)

        @pl.when(qi == 0)
        def _():
            dk_ref[...] = jnp.zeros_like(dk_ref)
            dv_ref[...] = jnp.zeros_like(dv_ref)

        q = q_ref[...]
        dob = do_ref[...].astype(bf16)
        total = c_ref[:, 0:1]
        r2 = lax.broadcasted_iota(jnp.int32, (tk, tk), 0)
        c2 = lax.broadcasted_iota(jnp.int32, (tk, tk), 1)
        upto = (r2 <= c2).astype(bf16)
        before = (r2 < c2).astype(bf16)

        def step(j, carry, d):
            dq, pre, gpre = carry
            r0 = 0 if d is None else d * tk
            ks = pl.multiple_of(j * tk, tk)
            kb, vv = k_ref[pl.ds(ks, tk), :], v_ref[pl.ds(ks, tk), :]
            qs, dos = q[r0:], dob[r0:]
            _, lb, lk = _sb_logits(qs, kb, scale)
            if d is not None:
                mask = lax.broadcasted_iota(jnp.int32, lk.shape, 1) < lax.broadcasted_iota(jnp.int32, lk.shape, 0)
                lk = jnp.where(mask, lk, 0.0)
            pin = _dot_mask(lk, upto, 2)
            w = jnp.exp(lb + (total[r0:] - pre[r0:] - pin))
            if d is not None:
                w = jnp.where(mask, w, 0.0)
            dw = _dot(dos, vv, "nt")
            dv_ref[pl.ds(ks, tk), :] += _dot(w.astype(bf16), dos, "tn")
            gg = dw * w
            gex = _dot(gg.astype(bf16), before)
            beta = jnp.exp(lb)
            dz = (gg * (1.0 - beta) - (gpre[r0:] + gex) * beta) * scale
            if d is not None:
                dz = jnp.where(mask, dz, 0.0)
            dzb = dz.astype(bf16)
            dk_ref[pl.ds(ks, tk), :] += _dot(dzb, qs, "tn")
            return (_tail_update(dq, r0, dq[r0:] + _dot(dzb, kb)),
                    _tail_update(pre, r0, pre[r0:] + pin[:, tk - 1:tk]),
                    _tail_update(gpre, r0, gpre[r0:] + gex[:, tk - 1:tk] + gg[:, tk - 1:tk]))

        init = (jnp.zeros((tq, dh), f32), jnp.zeros((tq, 1), f32), jnp.zeros((tq, 1), f32))
        carry = lax.fori_loop(0, qi * nd, lambda t, c: step(t, c, None), init)
        for d in range(nd):
            carry = step(qi * nd + d, carry, d)
        dq_ref[...] = carry[0]
        if comm is not None:
            @pl.when(last)
            def _():
                comm["finish"](*cargs)

    qblk = pl.BlockSpec((tq, dh), lambda h, i: (i, h))
    full = pl.BlockSpec((T, dh), lambda h, i: (0, h))
    (dq, dk, dv), got = _host_call(
        body, name, grid, [qblk, full, full, pl.BlockSpec((tq, dh), lambda h, i: (i, h + ob)), qblk], [qblk, full, full],
        [jax.ShapeDtypeStruct((T, W), f32)] * 3, [], [qn, kn, vb, do, ctot], comm)
    return dq, dk, dv, got


def _pool_select(sums, g):
    return jnp.where(g == 0, sums[0], jnp.where(g == 1, sums[1], jnp.where(g == 2, sums[2], sums[3])))


def _pool_count(g, r, rc, n, cols, off=0):
    t = (r * rc + off + lax.broadcasted_iota(jnp.int32, (n, cols), 0)).astype(f32)
    win = jnp.left_shift(2, g).astype(f32)
    return jnp.minimum(t + 1.0, win)


def _pool_fwd(hp, xres, w, b, scale, name, rc=512, comm=None):
    T, D = hp.shape
    rc = min(rc, T)
    pg = D // len(POOL_WINDOWS)
    grid = (len(POOL_WINDOWS), T // rc)

    def body(*refs):
        (h_ref, p_ref, x_ref, w_ref, b_ref, s_ref), (o_ref, yp_ref, d_ref), _, cargs = _hosted(comm, 6, 3, refs)
        if comm is not None:
            first, last = _first_last(grid)

            @pl.when(first)
            def _():
                comm["start"](*cargs)

            @pl.when(last)
            def _():
                comm["finish"](*cargs)

        g, r = pl.program_id(0), pl.program_id(1)
        cur = h_ref[...]
        halo = jnp.where(r > 0, p_ref[...], 0.0)
        ext = jnp.concatenate([halo, cur], axis=0)
        sums, s = [], ext
        for sh in (1, 2, 4, 8):
            s = s + pltpu.roll(s, sh, 0)
            sums.append(s)
        d = _pool_select(sums, g)[POOL_HALO:] / _pool_count(g, r, rc, rc, pg) - cur
        yp = _dot(d.astype(bf16), w_ref[0]) + b_ref[...]
        yp_ref[...] = yp
        d_ref[...] = d.astype(d_ref.dtype)
        o_ref[...] = x_ref[...] + yp * s_ref[...]

    cur = pl.BlockSpec((rc, pg), lambda g, r: (r, g))
    prev = pl.BlockSpec((POOL_HALO, pg), lambda g, r: (jnp.maximum(r * (rc // POOL_HALO) - 1, 0), g))
    vec = pl.BlockSpec((1, pg), lambda g, r: (0, g))
    (o, yp, d), got = _host_call(
        body, name, grid, [cur, prev, cur, pl.BlockSpec((1, pg, pg), lambda g, r: (g, 0, 0)), vec, vec], [cur, cur, cur],
        [jax.ShapeDtypeStruct((T, D), f32), jax.ShapeDtypeStruct((T, D), f32), jax.ShapeDtypeStruct((T, D), bf16)],
        [], [hp, hp, xres, w, b, scale], comm)
    return o, yp, d, got


def _pool_bwd(dx, yp, d, w, scale, name, rc=512):
    T, D = dx.shape
    rc = min(rc, T)
    pg = D // len(POOL_WINDOWS)
    nr = T // rc

    def body(dx_ref, dn_ref, yp_ref, d_ref, w_ref, s_ref, dh_ref, dw_ref, db_ref, dsc_ref):
        g, r = pl.program_id(0), pl.program_id(1)

        @pl.when(r == 0)
        def _():
            dw_ref[...] = jnp.zeros_like(dw_ref)
            db_ref[...] = jnp.zeros_like(db_ref)
            dsc_ref[...] = jnp.zeros_like(dsc_ref)

        dxv = dx_ref[...]
        dyp = dxv * s_ref[...]
        dsc_ref[...] += jnp.sum(dxv * yp_ref[...], axis=0, keepdims=True)
        db_ref[...] += jnp.sum(dyp, axis=0, keepdims=True)
        dypb = dyp.astype(bf16)
        dw_ref[0] += _dot(d_ref[...], dypb, "tn")
        dd = _dot(dypb, w_ref[0], "nt")
        ddn = _dot((dn_ref[...] * s_ref[...]).astype(bf16), w_ref[0], "nt")
        e = dd / _pool_count(g, r, rc, rc, pg)
        en = jnp.where(r < nr - 1, ddn / _pool_count(g, r, rc, POOL_HALO, pg, off=rc), 0.0)
        ext = jnp.concatenate([e, en], axis=0)
        sums, s = [], ext
        for sh in (1, 2, 4, 8):
            s = s + pltpu.roll(s, rc + POOL_HALO - sh, 0)
            sums.append(s)
        dh_ref[...] = _pool_select(sums, g)[:rc] - dd

    cur = pl.BlockSpec((rc, pg), lambda g, r: (r, g))
    nxt = pl.BlockSpec((POOL_HALO, pg), lambda g, r: (jnp.minimum((r + 1) * (rc // POOL_HALO), T // POOL_HALO - 1), g))
    vec = pl.BlockSpec((1, pg), lambda g, r: (0, g))
    wsp = pl.BlockSpec((1, pg, pg), lambda g, r: (g, 0, 0))
    return pl.pallas_call(
        body, name=name, grid=(len(POOL_WINDOWS), nr), in_specs=[cur, nxt, cur, cur, wsp, vec],
        out_specs=[cur, wsp, vec, vec],
        out_shape=[jax.ShapeDtypeStruct((T, D), f32), jax.ShapeDtypeStruct(w.shape, f32),
                   jax.ShapeDtypeStruct((1, D), f32), jax.ShapeDtypeStruct((1, D), f32)],
        compiler_params=_cp("parallel", "arbitrary"),
    )(dx, dx, yp, d, w, scale)


def _adamw(w, g, m, v, name, tr=256):
    R, C = w.shape
    tr = min(tr, R)
    lanes = -(-C // LANES) * LANES
    while tr > 8 and 2 * 8 * tr * lanes * 4 > MM_TILE_BUDGET:
        tr //= 2
    assert R % tr == 0

    def body(w_ref, g_ref, m_ref, v_ref, d_ref, mo_ref, vo_ref, go_ref):
        gv = g_ref[...]
        mn = ADAM_B1 * m_ref[...] + (1.0 - ADAM_B1) * gv
        vn = ADAM_B2 * v_ref[...] + (1.0 - ADAM_B2) * (gv * gv)
        m_hat = mn / (1.0 - ADAM_B1 ** ADAM_STEP)
        v_hat = vn / (1.0 - ADAM_B2 ** ADAM_STEP)
        d_ref[...] = -ADAM_LR * (m_hat / (jnp.sqrt(v_hat) + ADAM_EPS) + ADAM_WD * w_ref[...])
        mo_ref[...] = mn
        vo_ref[...] = vn
        go_ref[...] = gv

    blk = pl.BlockSpec((tr, C), lambda r: (r, 0))
    return pl.pallas_call(
        body, name=name, grid=(R // tr,), in_specs=[blk] * 4, out_specs=[blk] * 4,
        out_shape=[jax.ShapeDtypeStruct((R, C), f32)] * 4, compiler_params=_cp("parallel"),
    )(w, g, m, v)


def _pair_sum(g4, recv, name, br=256):
    _, R, C = g4.shape
    hr = R // 2
    br = min(br, hr)
    nb = hr // br

    def body(a_ref, b_ref, o_ref):
        o_ref[...] = (a_ref[...] + b_ref[...]).astype(o_ref.dtype)

    out = pl.BlockSpec((1, br, C), lambda s, i: (s, i, 0))
    return pl.pallas_call(
        body, name=name, grid=(N_CHIPS, nb),
        in_specs=[pl.BlockSpec((1, br, C), lambda s, i: (s, lax.axis_index("c") * nb + i, 0)), out], out_specs=out,
        out_shape=jax.ShapeDtypeStruct((N_CHIPS, hr, C), bf16), compiler_params=_cp("parallel", "parallel"),
    )(g4, recv)


def _chip_sum(g4, recv, pieces, name, br=256):
    _, hr, C = recv.shape
    br = min(br, hr)
    nb = hr // br
    chip = lambda: 2 * lax.axis_index("x") + lax.axis_index("y")

    def body(a_ref, r_ref, b1_ref, b2_ref, b3_ref, o_ref):
        o_ref[...] = (((a_ref[0] + r_ref[0]) + b1_ref[0].astype(f32)) + b2_ref[0].astype(f32)) + b3_ref[0].astype(f32)

    other = lambda k: pl.BlockSpec((1, br, C), lambda i: ((chip() + k) % N_CHIPS, i, 0))
    return pl.pallas_call(
        body, name=name, grid=(nb,),
        in_specs=[pl.BlockSpec((1, br, C), lambda i: (chip(), lax.axis_index("c") * nb + i, 0)),
                  pl.BlockSpec((1, br, C), lambda i: (chip(), i, 0)), other(1), other(2), other(3)],
        out_specs=pl.BlockSpec((br, C), lambda i: (lax.axis_index("c") * nb + i, 0)),
        out_shape=jax.ShapeDtypeStruct((2 * hr, C), f32), compiler_params=_cp("parallel"),
    )(g4, recv, pieces, pieces, pieces)


ANY = pl.BlockSpec(memory_space=pl.ANY)


def _mesh_pos():
    x, y, c = lax.axis_index("x"), lax.axis_index("y"), lax.axis_index("c")
    others = [(1 - x, y), (x, 1 - y), (1 - x, 1 - y)]
    return x, y, c, 2 * x + y, others


def _gather_small(blk, name):
    m, n = blk.shape

    def body(x_ref, out_ref, sum_ref, send_sems, recv_sems, local_sem):
        x, y, c, _, others = _mesh_pos()
        me, sibling = (x, y, c), (x, y, 1 - c)

        def rows(px, py, pc):
            return out_ref.at[pl.ds((4 * px + 2 * py + pc) * m, m), :]

        def copy(k, block, to, src=None):
            return pltpu.make_async_remote_copy(
                src_ref=rows(*block) if src is None else src, dst_ref=rows(*block),
                send_sem=send_sems.at[k], recv_sem=recv_sems.at[k], device_id=to, device_id_type=MESH)

        mine = pltpu.make_async_copy(x_ref, rows(*me), local_sem)
        mine.start()
        first = [copy(0, me, sibling, src=x_ref)]
        first += [copy(1 + j, me, (*chip, c), src=x_ref) for j, chip in enumerate(others)]
        for cp in first:
            cp.start()
        passed = [copy(4 + j, (*chip, c), sibling) for j, chip in enumerate(others)]
        for j, chip in enumerate(others):
            copy(1 + j, (*chip, c), me).wait_recv()
            passed[j].start()
        copy(0, sibling, me).wait_recv()
        for j, chip in enumerate(others):
            copy(4 + j, (*chip, 1 - c), me).wait_recv()
        for cp in first + passed:
            cp.wait_send()
        mine.wait()
        acc = out_ref[0:m, :]
        for d in range(1, 8):
            acc = acc + out_ref[d * m:(d + 1) * m, :]
        sum_ref[...] = acc

    vm = pl.BlockSpec(memory_space=pltpu.VMEM)
    return pl.pallas_call(
        body, name=name, in_specs=[vm], out_specs=[vm, vm],
        out_shape=[jax.ShapeDtypeStruct((8 * m, n), f32), jax.ShapeDtypeStruct((m, n), f32)],
        scratch_shapes=[pltpu.SemaphoreType.DMA((7,)), pltpu.SemaphoreType.DMA((7,)), pltpu.SemaphoreType.DMA],
    )(blk)


def _copy(src, dst, sems, idx, to):
    return pltpu.make_async_remote_copy(src_ref=src, dst_ref=dst, send_sem=sems[0].at[idx], recv_sem=sems[1].at[idx],
                                        device_id=to, device_id_type=MESH)


def _gather_ici(shards):
    nt = len(shards)

    def copies(ins, outs, sems):
        x, y, c, chip, others = _mesh_pos()
        send, land = [], []
        for t in range(nt):
            hr = ins[t].shape[0] // 2
            for j, (px, py) in enumerate(others):
                send.append((ins[t].at[pl.ds(c * hr, hr)], outs[t].at[chip, pl.ds(c * hr, hr)], sems, (t, j), (px, py, c)))
                piece = outs[t].at[2 * px + py, pl.ds(c * hr, hr)]
                land.append((piece, piece, sems, (t, j), (px, py, c)))
        return send, land

    def start(ins, outs, sems):
        for args in copies(ins, outs, sems)[0]:
            _copy(*args).start()

    def finish(ins, outs, sems):
        send, land = copies(ins, outs, sems)
        for args in land:
            _copy(*args).wait_recv()
        for args in send:
            _copy(*args).wait_send()

    return dict(ins=list(shards), outs=[jax.ShapeDtypeStruct((N_CHIPS,) + s.shape, s.dtype) for s in shards],
                sems=[pltpu.SemaphoreType.DMA((nt, 3)), pltpu.SemaphoreType.DMA((nt, 3))], start=start, finish=finish)


def _gather_d2d(stacks):
    nt = len(stacks)

    def copies(ins, outs, sems):
        x, y, c, _, others = _mesh_pos()
        send, land = [], []
        for t in range(nt):
            hr = outs[t].shape[1] // 2
            for j, (px, py) in enumerate(others):
                mine = outs[t].at[2 * px + py, pl.ds(c * hr, hr)]
                theirs = outs[t].at[2 * px + py, pl.ds((1 - c) * hr, hr)]
                send.append((mine, mine, sems, (t, j), (x, y, 1 - c)))
                land.append((theirs, theirs, sems, (t, j), (x, y, 1 - c)))
        return send, land

    def start(ins, outs, sems):
        for args in copies(ins, outs, sems)[0]:
            _copy(*args).start()

    def finish(ins, outs, sems):
        send, land = copies(ins, outs, sems)
        for args in land:
            _copy(*args).wait_recv()
        for args in send:
            _copy(*args).wait_send()

    return dict(ins=list(stacks), outs=[jax.ShapeDtypeStruct(s.shape, s.dtype) for s in stacks],
                sems=[pltpu.SemaphoreType.DMA((nt, 3)), pltpu.SemaphoreType.DMA((nt, 3))], start=start, finish=finish,
                aliases={t: t for t in range(nt)})


def _run_exchange(comm, name):
    ni, no = len(comm["ins"]), len(comm["outs"])

    def body(*refs):
        args = (refs[:ni], refs[ni:ni + no], refs[ni + no:])
        comm["start"](*args)
        comm["finish"](*args)

    return pl.pallas_call(
        body, name=name, in_specs=[ANY] * ni, out_specs=[ANY] * no, out_shape=comm["outs"], scratch_shapes=comm["sems"],
        input_output_aliases=dict(comm.get("aliases", {})))(*comm["ins"])


def _swap_halves(g4s):
    nt = len(g4s)

    def copies(ins, outs, sems):
        x, y, c, _, _ = _mesh_pos()
        both = []
        for t in range(nt):
            hr = ins[t].shape[1] // 2
            both.append((ins[t].at[:, pl.ds((1 - c) * hr, hr)], outs[t], sems, t, (x, y, 1 - c)))
        return both, both

    def start(ins, outs, sems):
        for args in copies(ins, outs, sems)[0]:
            _copy(*args).start()

    def finish(ins, outs, sems):
        send, land = copies(ins, outs, sems)
        for args in land:
            _copy(*args).wait_recv()
        for args in send:
            _copy(*args).wait_send()

    return dict(ins=list(g4s), outs=[jax.ShapeDtypeStruct((N_CHIPS, g.shape[1] // 2, g.shape[2]), g.dtype) for g in g4s],
                sems=[pltpu.SemaphoreType.DMA((nt,)), pltpu.SemaphoreType.DMA((nt,))], start=start, finish=finish)


def _exchange_chips(h4s):
    nt = len(h4s)

    def copies(ins, outs, sems):
        x, y, c, chip, others = _mesh_pos()
        send, land = [], []
        for t in range(nt):
            for j, (px, py) in enumerate(others):
                send.append((ins[t].at[2 * px + py], outs[t].at[chip], sems, (t, j), (px, py, c)))
                landed = outs[t].at[2 * px + py]
                land.append((landed, landed, sems, (t, j), (px, py, c)))
        return send, land

    def start(ins, outs, sems):
        for args in copies(ins, outs, sems)[0]:
            _copy(*args).start()

    def finish(ins, outs, sems):
        send, land = copies(ins, outs, sems)
        for args in land:
            _copy(*args).wait_recv()
        for args in send:
            _copy(*args).wait_send()

    return dict(ins=list(h4s), outs=[jax.ShapeDtypeStruct(h.shape, h.dtype) for h in h4s],
                sems=[pltpu.SemaphoreType.DMA((nt, 3)), pltpu.SemaphoreType.DMA((nt, 3))], start=start, finish=finish)


def _join_halves(fs):
    nt = len(fs)

    def copies(ins, outs, sems):
        x, y, c, _, _ = _mesh_pos()
        send, land = [], []
        for t in range(nt):
            hr = outs[t].shape[0] // 2
            mine, theirs = outs[t].at[pl.ds(c * hr, hr)], outs[t].at[pl.ds((1 - c) * hr, hr)]
            send.append((mine, mine, sems, t, (x, y, 1 - c)))
            land.append((theirs, theirs, sems, t, (x, y, 1 - c)))
        return send, land

    def start(ins, outs, sems):
        for args in copies(ins, outs, sems)[0]:
            _copy(*args).start()

    def finish(ins, outs, sems):
        send, land = copies(ins, outs, sems)
        for args in land:
            _copy(*args).wait_recv()
        for args in send:
            _copy(*args).wait_send()

    return dict(ins=list(fs), outs=[jax.ShapeDtypeStruct(f.shape, f.dtype) for f in fs],
                sems=[pltpu.SemaphoreType.DMA((nt,)), pltpu.SemaphoreType.DMA((nt,))], start=start, finish=finish,
                aliases={t: t for t in range(nt)})


def _pad_lanes(v, n=LANES):
    return jnp.pad(v, ((0, 0), (0, n - v.shape[-1])))


def _mlp_fwd(xin, norm_g, w_up, w_down, F, tag, comms=(None, None)):
    T, D = xin.shape
    h = _rms_fwd(xin, norm_g, bf16, f"{tag}_norm")

    def relu_sq(acc):
        r = jnp.maximum(acc, 0.0)
        return r, r * r

    got = [None, None]
    ua = _mm(h, w_up[0], "nn", T, F, D, (bf16, bf16), f"{tag}_up", epilogue=relu_sq, b_view=w_up[1], comm=comms[0])
    (u, a), got[0] = ua if comms[0] is not None else (ua, None)
    out = _mm(a, w_down[0], "nn", T, D, F, (f32,), f"{tag}_down", epilogue=lambda acc, res: (res + acc,),
              extras=((xin, "tile"),), b_view=w_down[1], comm=comms[1])
    (out,), got[1] = out if comms[1] is not None else ((out,), None)
    return out, (xin, h, u, a), got


def _mlp_bwd(dy, saved, norm_g, w_up, w_down, F, tag, up_to=None, down_to=None, host=None):
    xin, h, u, a = saved
    T, D = xin.shape
    to = lambda t: {} if t is None else dict(out_view=t[0], out_stack=t[1], alias=t[2])
    du = _mm(dy, w_down[0], "nt", T, F, D, (bf16,), f"{tag}_dact", epilogue=lambda acc, uu: (acc * (2.0 * uu.astype(f32)),),
             extras=((u, "tile"),), b_view=w_down[1])
    dw_down = _mm(a, dy, "tn", F, D, T, (f32,), f"{tag}_dwdown", **to(down_to))
    dh = _mm(du, w_up[0], "nt", T, D, F, (f32,), f"{tag}_dh", b_view=w_up[1], comm=host(dw_down) if host else None)
    (dh,), got = dh if host else ((dh,), None)
    dw_up = _mm(h, du, "tn", D, F, T, (f32,), f"{tag}_dwup", **to(up_to))
    dx, dg = _rms_bwd(xin, norm_g, dh, dy, f"{tag}_dnorm")
    return dx, dg, dw_up, dw_down, got


def _local_step(xc, tgt, W, HS, SBW, net=None):
    T, D = xc.shape
    SW = HS * SSD_HEAD_DIM
    CD = W["conv_b"].shape[-1]
    mlp_norm = W["mlp_norm"]
    add = lambda acc, prev: (prev + acc,)

    h0 = _rms_fwd(xc, W["hyb_norm"], bf16, "hyb_norm")
    z = _mm(h0, W["w_z"], "nn", T, SW, D, (f32,), "proj_z")
    xraw = _mm(h0, W["w_xbc"], "nn", T, CD, D, (f32,), "proj_xbc")
    dtraw = _mm(h0, W["w_dt"], "nn", T, LANES, D, (f32,), "proj_dt")
    qkv = _mm(h0, W["w_qkv"], "nn", T, 3 * SBW, D, (f32,), "proj_qkv")
    qn, kn, vb = _qk_norm_fwd(qkv, W["q_norm"], W["k_norm"], SBW, "qk_norm")
    y_sb, ctot, got = _sb_fwd(qn, kn, vb, "sb_attn", comm=net.rest_ici() if net else None)
    xbc = _conv_fwd(xraw, W["conv_w"], W["conv_b"], "conv")
    y_ssd, yn_ssd, sprev, got = _ssd_fwd(xbc, dtraw, z, W["dt_bias"], W["a_log"], W["d_skip"], W["out_norm"], HS, "ssd",
                                         comm=net.rest_d2d(got) if net else None)
    if net:
        W = {**W, **net.rest_weights(got)}
    w_up, w_down, F = W["w_up"], W["w_down"], W["F"]
    mix = _mm(yn_ssd, W["w_out"], "nn", T, D, SW, (f32,), "out_ssd", epilogue=add, extras=((xc, "tile"),))
    x1 = _mm(y_sb, W["w_out"], "nn", T, D, SBW, (f32,), "out_sb", epilogue=add, extras=((mix, "tile"),), b_off=(SW, 0))
    x2, mlp0, got = _mlp_fwd(x1, mlp_norm[0:1], w_up[0], w_down[0], F, "mlp0",
                             comms=(net.last_ici(0), net.last_ici(1)) if net else (None, None))
    hp = _rms_fwd(x2, W["pool_norm"], f32, "pool_norm")
    x3, yp, dpool, got = _pool_fwd(hp, x2, W["w_pool"], W["pool_b"], W["pool_scale"], "pool",
                                   comm=net.last_d2d(got) if net else None)
    if net:
        w_up, w_down = net.last_weights(got, w_up, w_down)
    x4, mlp1, _ = _mlp_fwd(x3, mlp_norm[1:2], w_up[1], w_down[1], F, "mlp1")

    dy, sq = _loss_grad(x4, tgt, "loss")

    up_to, down_to = (net.mlp_to("up", 1, None), net.mlp_to("down", 1, None)) if net else (None, None)
    dx3, dg_mlp1, dw_up1, dw_down1, _ = _mlp_bwd(dy, mlp1, mlp_norm[1:2], w_up[1], w_down[1], F, "mlp1", up_to, down_to)
    dhp, dw_pool, db_pool, dsc_pool = _pool_bwd(dx3, yp, dpool, W["w_pool"], W["pool_scale"], "pool_bwd")
    dx2, dg_pool = _rms_bwd(x2, W["pool_norm"], dhp, dx3, "pool_dnorm")
    up_to, down_to = (net.mlp_to("up", 0, dw_up1), net.mlp_to("down", 0, dw_down1)) if net else (None, None)
    dx1, dg_mlp0, dw_up0, dw_down0, got_down = _mlp_bwd(dx2, mlp0, mlp_norm[0:1], w_up[0], w_down[0], F, "mlp0", up_to, down_to,
                                                        host=net.swap_down if net else None)

    dw_out = jnp.concatenate([_mm(yn_ssd, dx1, "tn", SW, D, T, (f32,), "dwout_ssd"),
                              _mm(y_sb, dx1, "tn", SBW, D, T, (f32,), "dwout_sb")], axis=0)
    if net:
        (dmerged,), got = _mm(dx1, W["w_out"], "nt", T, SW + SBW, D, (f32,), "dmerged",
                              comm=net.swap_rest(dw_out, dw_pool, dw_up0))
        dqn, dkn, dvv, got = _sb_bwd(qn, kn, vb, dmerged, ctot, SW, "sb_attn_bwd", comm=net.reduce_early(list(got) + list(got_down)))
        net.reduce_early_done(got)
    else:
        dmerged = _mm(dx1, W["w_out"], "nt", T, SW + SBW, D, (f32,), "dmerged")
        dqn, dkn, dvv, _ = _sb_bwd(qn, kn, vb, dmerged, ctot, SW, "sb_attn_bwd")
    dqkv, dg_q, dg_k = _qk_norm_bwd(qkv, dqn, dkn, dvv, W["q_norm"], W["k_norm"], SBW, "qk_norm_bwd")
    dy_ssd, dz, dg_on = _gate_bwd(y_ssd, z, dmerged, W["out_norm"], "gate_bwd")
    dxbc, ddtraw, dalog, dbias, ddskip = _ssd_bwd(xbc, dtraw, sprev, dy_ssd, W["dt_bias"], W["a_log"], W["d_skip"], HS, "ssd_bwd")
    dpre, dconv_w, dconv_b = _conv_bwd_pre(xraw, dxbc, W["conv_w"], W["conv_b"], "conv_bwd_pre")
    dxraw = _conv_bwd_in(dpre, W["conv_w"], "conv_bwd_in")
    dw_in = [_mm(h0, dz, "tn", D, SW, T, (f32,), "dwin_z"), _mm(h0, dxraw, "tn", D, CD, T, (f32,), "dwin_xbc"),
             _mm(h0, ddtraw, "tn", D, LANES, T, (f32,), "dwin_dt")[:, :HS], _mm(h0, dqkv, "tn", D, 3 * SBW, T, (f32,), "dwin_qkv")]
    dh0 = _mm(dz, W["w_z"], "nt", T, D, SW, (f32,), "dh0_z")
    if net:
        (dh0,), got = _mm(dxraw, W["w_xbc"], "nt", T, D, CD, (f32,), "dh0_xbc", epilogue=add, extras=((dh0, "tile"),),
                          comm=net.late_swap(dw_in))
        dh0 = _mm(ddtraw, W["w_dt"], "nt", T, D, LANES, (f32,), "dh0_dt", epilogue=add, extras=((dh0, "tile"),))
        (dh0,), got = _mm(dqkv, W["w_qkv"], "nt", T, D, 3 * SBW, (f32,), "dh0_qkv", epilogue=add, extras=((dh0, "tile"),),
                          comm=net.reduce_late(got))
        net.reduce_late_done(got)
    else:
        dh0 = _mm(dxraw, W["w_xbc"], "nt", T, D, CD, (f32,), "dh0_xbc", epilogue=add, extras=((dh0, "tile"),))
        dh0 = _mm(ddtraw, W["w_dt"], "nt", T, D, LANES, (f32,), "dh0_dt", epilogue=add, extras=((dh0, "tile"),))
        dh0 = _mm(dqkv, W["w_qkv"], "nt", T, D, 3 * SBW, (f32,), "dh0_qkv", epilogue=add, extras=((dh0, "tile"),))
    grad_x, dg_hyb = _rms_bwd(xc, W["hyb_norm"], dh0, dx1, "hyb_dnorm")
    grads = dict(w_in=dw_in, w_out=dw_out, w_pool=dw_pool, w_up=(dw_up0, dw_up1), w_down=(dw_down0, dw_down1),
                 hyb_norm=dg_hyb, conv_w=dconv_w, conv_b=dconv_b, dt_bias=dbias, a_log=dalog, d_skip=ddskip, out_norm=dg_on,
                 q_norm=dg_q, k_norm=dg_k, mlp_norm=(dg_mlp0, dg_mlp1), pool_norm=dg_pool, pool_b=db_pool, pool_scale=dsc_pool)
    return sq, grad_x, grads


def _stack_columns(pieces, n):
    cs = sum(p.shape[1] for p in pieces) // n
    slots = []
    for j in range(n):
        parts, off = [], 0
        for p in pieces:
            lo, hi = max(j * cs, off), min((j + 1) * cs, off + p.shape[1])
            if lo < hi:
                parts.append(p[:, lo - off:hi - off])
            off += p.shape[1]
        slots.append(parts[0] if len(parts) == 1 else jnp.concatenate(parts, axis=1))
    return jnp.stack(slots)


class _Net:
    def __init__(self, own_first, own_last, chip, dims):
        self.own, self.own_last, self.chip, self.dims = own_first, own_last, chip, dims

    def _place_own(self, stacks, own):
        return [lax.dynamic_update_index_in_dim(g, o, self.chip, 0) for g, o in zip(stacks, own)]

    def rest_ici(self):
        return _gather_ici(self.own)

    def rest_d2d(self, got):
        return _gather_d2d(list(got))

    def rest_weights(self, got):
        d, nw = self.dims, len(POOL_WINDOWS)
        D, F, PG = d["D"], d["F"], d["PG"]
        fs = F // N_CHIPS
        g_out, g_pool, g_up, g_down = self._place_own(got, self.own)
        w_pool = g_pool.reshape(N_CHIPS, nw, PG // N_CHIPS, PG).transpose(1, 0, 2, 3).reshape(nw, PG, PG)
        return dict(w_out=g_out.reshape(d["MIX"], D), w_pool=w_pool, F=F,
                    w_up=[(g_up, ("cols", fs, 0, D))], w_down=[(g_down, ("rows", fs, 0, None))])

    def last_ici(self, which):
        return _gather_ici([self.own_last[which]])

    def last_d2d(self, got):
        return _gather_d2d([got[0][0], got[1][0]])

    def last_weights(self, stacks, w_up, w_down):
        d = self.dims
        fs = d["F"] // N_CHIPS
        g_up, g_down = self._place_own(stacks, self.own_last)
        return w_up + [(g_up, ("cols", fs, 0, d["D"]))], w_down + [(g_down, ("rows", fs, 0, None))]

    def mlp_to(self, which, layer, earlier):
        d = self.dims
        fs = d["F"] // N_CHIPS
        if which == "up":
            return ("cols", fs, layer, d["D"]), (N_CHIPS, d["NL"] * d["D"], fs), earlier
        return ("rows", fs, layer, None), (N_CHIPS, d["NL"] * fs, d["D"]), earlier

    def swap_down(self, g_down):
        self.g_down = g_down
        return _swap_halves([g_down])

    def swap_rest(self, dw_out, dw_pool, g_up):
        d, nw = self.dims, len(POOL_WINDOWS)
        PG = d["PG"]
        self.early_g4 = [dw_out.reshape(N_CHIPS, d["MIX"] // N_CHIPS, d["D"]),
                         dw_pool.reshape(nw, N_CHIPS, PG // N_CHIPS, PG).transpose(1, 0, 2, 3).reshape(N_CHIPS, PG, PG),
                         g_up, self.g_down]
        return _swap_halves(self.early_g4[:3])

    def reduce_early(self, recv):
        self.early_recv = list(recv)
        sent = [_pair_sum(g, r, f"grads_early_pair_sum{i}") for i, (g, r) in enumerate(zip(self.early_g4, self.early_recv))]
        return _exchange_chips(sent)

    def reduce_early_done(self, got):
        self.early_got = list(got)

    def late_swap(self, dw_in):
        self.late_g4 = [_stack_columns(dw_in, N_CHIPS)]
        return _swap_halves(self.late_g4)

    def reduce_late(self, recv):
        self.late_recv = list(recv)
        return _exchange_chips([_pair_sum(self.late_g4[0], self.late_recv[0], "grads_late_pair_sum")])

    def reduce_late_done(self, got):
        self.late_got = list(got)

    def reduced(self):
        halves = [_chip_sum(g, r, p, f"grads_chip_sum{i}") for i, (g, r, p) in enumerate(zip(
            self.late_g4 + self.early_g4, self.late_recv + self.early_recv, self.late_got + self.early_got))]
        return _run_exchange(_join_halves(halves), "grads_join")


def kernel(x, hyb_norm, hyb_w_in, ssd_conv_w, ssd_conv_b, ssd_dt_bias, ssd_a_log, ssd_d, ssd_out_norm, sb_q_norm, sb_k_norm, hyb_w_out, pool_norm, pool_w, pool_b, pool_scale, mlp_norm, mlp_w_up, mlp_w_down, loss_target, m_hyb_norm, m_hyb_w_in, m_ssd_conv_w, m_ssd_conv_b, m_ssd_dt_bias, m_ssd_a_log, m_ssd_d, m_ssd_out_norm, m_sb_q_norm, m_sb_k_norm, m_hyb_w_out, m_pool_norm, m_pool_w, m_pool_b, m_pool_scale, m_mlp_norm, m_mlp_w_up, m_mlp_w_down, v_hyb_norm, v_hyb_w_in, v_ssd_conv_w, v_ssd_conv_b, v_ssd_dt_bias, v_ssd_a_log, v_ssd_d, v_ssd_out_norm, v_sb_q_norm, v_sb_k_norm, v_hyb_w_out, v_pool_norm, v_pool_w, v_pool_b, v_pool_scale, v_mlp_norm, v_mlp_w_up, v_mlp_w_down):
    T, D = x.shape[1], x.shape[2]
    HS = ssd_dt_bias.shape[-1]
    SW = HS * SSD_HEAD_DIM
    CD = ssd_conv_b.shape[-1]
    IN = N_CHIPS * hyb_w_in.shape[-1]
    SBW = (IN - SW - CD - HS) // 3
    F = N_CHIPS * mlp_w_up.shape[-1]
    NL = mlp_norm.shape[0]
    PG = D // len(POOL_WINDOWS)
    xc, tgt = x[0], loss_target[0]
    ix, iy, ic = lax.axis_index("x"), lax.axis_index("y"), lax.axis_index("c")
    chip = (2 * ix + iy).astype(jnp.int32)

    small = jnp.concatenate([ssd_conv_w.reshape(-1), pool_norm.reshape(-1), pool_b.reshape(-1), pool_scale.reshape(-1)])
    ns = small.shape[0]
    ns8 = -(-ns // (8 * LANES)) * LANES
    gathered, _ = _gather_small(jnp.pad(small, (0, 8 * ns8 - ns)).reshape(8, ns8), "gather_small")
    per_chip = gathered.reshape(N_CHIPS, 2, 8 * ns8)[:, 0, :ns]
    cw = CD // N_CHIPS
    conv_w = per_chip[:, :4 * cw].reshape(N_CHIPS, 4, cw).transpose(1, 0, 2).reshape(4, CD)
    pvec = per_chip[:, 4 * cw:].reshape(N_CHIPS, 3, PG)
    pool_norm_f, pool_b_f, pool_scale_f = (pvec[:, i].reshape(1, D) for i in range(3))

    fs = F // N_CHIPS
    own_in = hyb_w_in[0].astype(bf16)
    g_in = _run_exchange(_gather_d2d(_run_exchange(_gather_ici([own_in]), "gather_in_ici")), "gather_in_d2d")[0]
    w_in = lax.dynamic_update_index_in_dim(g_in, own_in, chip, 0).transpose(1, 0, 2).reshape(D, IN)
    c1, c2, c3 = SW, SW + CD, SW + CD + HS
    w_z, w_xbc, w_dt, w_qkv = w_in[:, :c1], w_in[:, c1:c2], _pad_lanes(w_in[:, c2:c3]), w_in[:, c3:]
    dt_bias_p, a_log_p, d_skip_p = _pad_lanes(ssd_dt_bias), _pad_lanes(ssd_a_log), jnp.repeat(ssd_d, SSD_HEAD_DIM, axis=-1)

    assert NL == 2
    own_first = [hyb_w_out[0].astype(bf16), pool_w[0].reshape(-1, PG).astype(bf16),
                 mlp_w_up[0].astype(bf16), mlp_w_down[0].astype(bf16)]
    own_last = [mlp_w_up[1].astype(bf16), mlp_w_down[1].astype(bf16)]
    net = _Net(own_first, own_last, chip, dict(D=D, F=F, NL=NL, PG=PG, IN=IN, MIX=SW + SBW))
    first = dict(hyb_norm=hyb_norm, w_z=w_z, w_xbc=w_xbc, w_dt=w_dt, w_qkv=w_qkv, conv_w=conv_w, conv_b=ssd_conv_b,
                 dt_bias=dt_bias_p, a_log=a_log_p, d_skip=d_skip_p, out_norm=ssd_out_norm, q_norm=sb_q_norm, k_norm=sb_k_norm,
                 pool_norm=pool_norm_f, pool_b=pool_b_f, pool_scale=pool_scale_f, mlp_norm=mlp_norm)
    sq, grad_x, gr = _local_step(xc, tgt, first, HS, SBW, net)
    loss = lax.psum(sq[0, 0] * (0.5 / D), ("x", "y", "c"))
    dg_hyb, dconv_b, dbias, dalog, ddskip, dg_on, dg_q, dg_k = (gr[k] for k in (
        "hyb_norm", "conv_b", "dt_bias", "a_log", "d_skip", "out_norm", "q_norm", "k_norm"))
    (dg_mlp0, dg_mlp1), dconv_w, dg_pool, db_pool, dsc_pool = gr["mlp_norm"], gr["conv_w"], gr["pool_norm"], gr["pool_b"], gr["pool_scale"]
    gb_in, gb_out, gb_pool, gb_up, gb_down = net.reduced()

    full_small = [dg_hyb, dconv_b, dbias[:, :HS], dalog[:, :HS], ddskip[:, :HS], dg_on, dg_q, dg_k,
                  jnp.concatenate([dg_mlp0, dg_mlp1], axis=0).reshape(1, -1),
                  dconv_w.reshape(1, -1), dg_pool, db_pool, dsc_pool]
    sizes = [v.shape[-1] for v in full_small]
    packed = jnp.concatenate([v.reshape(-1) for v in full_small])
    npk = packed.shape[0]
    npk8 = -(-npk // (8 * LANES)) * LANES
    _, summed = _gather_small(jnp.pad(packed, (0, 8 * npk8 - npk)).reshape(8, npk8), "grads_small")
    summed = summed.reshape(-1)[:npk]
    offs = [0]
    for s in sizes:
        offs.append(offs[-1] + s)
    (g_hyb_norm, g_conv_b, g_dt_bias, g_a_log, g_d, g_out_norm, g_q_norm, g_k_norm, g_mlp_norm, g_conv_w_full,
     g_pool_norm_full, g_pool_b_full, g_pool_scale_full) = (summed[offs[i]:offs[i + 1]] for i in range(len(sizes)))
    take = lambda full, n: lax.dynamic_slice_in_dim(full.reshape(-1, N_CHIPS, n), chip, 1, axis=1)
    small_grads = {
        "hyb_norm": g_hyb_norm.reshape(hyb_norm.shape), "ssd_conv_w": take(g_conv_w_full, cw).reshape(ssd_conv_w.shape),
        "ssd_conv_b": g_conv_b.reshape(ssd_conv_b.shape), "ssd_dt_bias": g_dt_bias.reshape(ssd_dt_bias.shape),
        "ssd_a_log": g_a_log.reshape(ssd_a_log.shape), "ssd_d": g_d.reshape(ssd_d.shape),
        "ssd_out_norm": g_out_norm.reshape(ssd_out_norm.shape), "sb_q_norm": g_q_norm.reshape(sb_q_norm.shape),
        "sb_k_norm": g_k_norm.reshape(sb_k_norm.shape), "pool_norm": take(g_pool_norm_full, PG).reshape(pool_norm.shape),
        "pool_b": take(g_pool_b_full, PG).reshape(pool_b.shape), "pool_scale": take(g_pool_scale_full, PG).reshape(pool_scale.shape),
        "mlp_norm": g_mlp_norm.reshape(mlp_norm.shape),
    }

    weights = dict(hyb_norm=hyb_norm, hyb_w_in=hyb_w_in, ssd_conv_w=ssd_conv_w, ssd_conv_b=ssd_conv_b, ssd_dt_bias=ssd_dt_bias,
                   ssd_a_log=ssd_a_log, ssd_d=ssd_d, ssd_out_norm=ssd_out_norm, sb_q_norm=sb_q_norm, sb_k_norm=sb_k_norm,
                   hyb_w_out=hyb_w_out, pool_norm=pool_norm, pool_w=pool_w, pool_b=pool_b, pool_scale=pool_scale,
                   mlp_norm=mlp_norm, mlp_w_up=mlp_w_up, mlp_w_down=mlp_w_down)
    moms = dict(hyb_norm=m_hyb_norm, hyb_w_in=m_hyb_w_in, ssd_conv_w=m_ssd_conv_w, ssd_conv_b=m_ssd_conv_b, ssd_dt_bias=m_ssd_dt_bias,
                ssd_a_log=m_ssd_a_log, ssd_d=m_ssd_d, ssd_out_norm=m_ssd_out_norm, sb_q_norm=m_sb_q_norm, sb_k_norm=m_sb_k_norm,
                hyb_w_out=m_hyb_w_out, pool_norm=m_pool_norm, pool_w=m_pool_w, pool_b=m_pool_b, pool_scale=m_pool_scale,
                mlp_norm=m_mlp_norm, mlp_w_up=m_mlp_w_up, mlp_w_down=m_mlp_w_down)
    vels = dict(hyb_norm=v_hyb_norm, hyb_w_in=v_hyb_w_in, ssd_conv_w=v_ssd_conv_w, ssd_conv_b=v_ssd_conv_b, ssd_dt_bias=v_ssd_dt_bias,
                ssd_a_log=v_ssd_a_log, ssd_d=v_ssd_d, ssd_out_norm=v_ssd_out_norm, sb_q_norm=v_sb_q_norm, sb_k_norm=v_sb_k_norm,
                hyb_w_out=v_hyb_w_out, pool_norm=v_pool_norm, pool_w=v_pool_w, pool_b=v_pool_b, pool_scale=v_pool_scale,
                mlp_norm=v_mlp_norm, mlp_w_up=v_mlp_w_up, mlp_w_down=v_mlp_w_down)
    order = list(weights)
    grads, delta, new_m, new_v = {}, {}, {}, {}
    for name, g2 in (("hyb_w_in", gb_in), ("hyb_w_out", gb_out), ("pool_w", gb_pool), ("mlp_w_up", gb_up), ("mlp_w_down", gb_down)):
        shp = weights[name].shape
        d_, m_, v_, g_ = _adamw(weights[name].reshape(g2.shape), g2, moms[name].reshape(g2.shape), vels[name].reshape(g2.shape),
                                f"adamw_{name}")
        grads[name], delta[name], new_m[name], new_v[name] = (t.reshape(shp) for t in (g_, d_, m_, v_))
    snames = list(small_grads)
    pack = lambda d: jnp.concatenate([d[n].reshape(-1) for n in snames])
    nsm = sum(small_grads[n].size for n in snames)
    cols = -(-nsm // (8 * LANES)) * LANES
    as_blk = lambda v: jnp.pad(v, (0, 8 * cols - nsm)).reshape(8, cols)
    padded_v = jnp.pad(pack(vels), (0, 8 * cols - nsm), constant_values=1.0).reshape(8, cols)
    d_, m_, v_, _ = _adamw(as_blk(pack(weights)), as_blk(pack(small_grads)), as_blk(pack(moms)), padded_v, "adamw_small")
    off = 0
    for n in snames:
        sz, shp = small_grads[n].size, weights[n].shape
        grads[n] = small_grads[n]
        delta[n], new_m[n], new_v[n] = (t.reshape(-1)[off:off + sz].reshape(shp) for t in (d_, m_, v_))
        off += sz

    return (loss, grad_x.reshape(x.shape), *[grads[n] for n in order], *[delta[n] for n in order],
            *[new_m[n] for n in order], *[new_v[n] for n in order])
```

```python
import functools
import math

import jax
import jax.numpy as jnp
from jax import lax
from jax.experimental import pallas as pl
from jax.experimental.pallas import tpu as pltpu

f32 = jnp.float32
bf16 = jnp.bfloat16

EPS = 1e-6
SSD_HEAD_DIM = 64
SSD_STATE = 128
SSD_GROUPS = 4
SSD_CHUNK = 128
LANES = 128
SB_HEAD_DIM = 128
POOL_WINDOWS = (2, 4, 8, 16)
POOL_HALO = 16
CONV_HALO = 8
ADAM_LR, ADAM_B1, ADAM_B2, ADAM_EPS, ADAM_WD, ADAM_STEP = 0.001, 0.9, 0.999, 1e-08, 0.01, 10
VMEM_LIMIT = 56 * 1024 * 1024
MM_TILE_BUDGET = 40 * 1024 * 1024
N_CHIPS = 4
MESH = pl.DeviceIdType.MESH

_DIMS = {"nn": (((1,), (0,)), ((), ())), "nt": (((1,), (1,)), ((), ())), "tn": (((0,), (0,)), ((), ()))}


def _fit(n, t):
    if n <= t:
        return n
    return max(d for d in range(LANES, t + 1, LANES) if n % d == 0)


def _cp(*sem):
    return pltpu.CompilerParams(dimension_semantics=sem, vmem_limit_bytes=VMEM_LIMIT)


def _sigmoid(v):
    return 1.0 / (1.0 + jnp.exp(-v))


def _softplus(v):
    return jnp.maximum(v, 0.0) + jnp.log(1.0 + jnp.exp(-jnp.abs(v)))


def _split(v, parts):
    out, rem = [], v
    for _ in range(parts):
        p = rem.astype(bf16)
        out.append(p)
        rem = rem - p.astype(f32)
    return out


def _dot(a, b, mode="nn"):
    return lax.dot_general(a, b, _DIMS[mode], preferred_element_type=f32)


def _mask_dot(mask_b, v, parts):
    return _dot(jnp.concatenate([mask_b] * parts, axis=1), jnp.concatenate(_split(v, parts), axis=0))


def _dot_mask(v, mask_b, parts):
    return _dot(jnp.concatenate(_split(v, parts), axis=1), jnp.concatenate([mask_b] * parts, axis=0))


def _stacked(view, br, bc, rmap, cmap):
    kind, per, layer, rows_per_layer = view
    if kind == "cols":
        npc = per // bc
        return pl.BlockSpec((None, br, bc), lambda i, j, k: (cmap(i, j, k) // npc, layer * (rows_per_layer // br) + rmap(i, j, k),
                                                              cmap(i, j, k) % npc))
    npc = per // br
    return pl.BlockSpec((None, br, bc), lambda i, j, k: (rmap(i, j, k) // npc, layer * npc + rmap(i, j, k) % npc, cmap(i, j, k)))


def _pick_tiles(M, N, K, caps, a_bytes, b_bytes, io_bytes):
    def cands(n, cap, sizes):
        got = [s for s in sizes if s <= min(n, cap) and n % s == 0]
        return got or [_fit(n, min(n, cap))]

    best = None
    for tk in cands(K, caps[2], (8192, 4096, 2048, 1024, 512, 256, 128)):
        for tm in cands(M, caps[0], (1024, 512, 256, 128)):
            for tn in cands(N, caps[1], (1024, 512, 256, 128)):
                need = 2 * (tm * tk * a_bytes + tk * tn * b_bytes) + tm * tn * (2 * io_bytes + (4 if tk < K else 0))
                key = (need <= MM_TILE_BUDGET, tk, tm * tn, tm)
                if best is None or key > best[0]:
                    best = (key, (tm, tn, tk))
    return best[1]


def _mm(a, b, mode, M, N, K, outs, name, epilogue=None, extras=(), a_off=(0, 0), b_off=(0, 0),
        b_view=None, out_view=None, out_stack=None, alias=None, comm=None):
    caps = [M, N, K]
    if b_view is not None:
        caps[1 if (b_view[0] == "cols") == (mode != "nt") else 2] = b_view[1]
    if out_view is not None:
        d = 1 if out_view[0] == "cols" else 0
        caps[d] = min(caps[d], out_view[1])
    for off, dims in ((a_off, (2, 0) if mode == "tn" else (0, 2)), (b_off, (1, 2) if mode == "nt" else (2, 1))):
        for o, d in zip(off, dims):
            if o:
                caps[d] = min(caps[d], math.gcd(o, caps[d]))
    io_bytes = sum(jnp.dtype(dt).itemsize for dt in outs) + sum(e[0].dtype.itemsize for e in extras if e[1] == "tile")
    tm, tn, tk = _pick_tiles(M, N, K, caps, a.dtype.itemsize, b.dtype.itemsize, io_bytes)
    nk = K // tk
    if mode == "tn":
        a_blk, ad = (tk, tm), (tk, tm)
    else:
        a_blk, ad = (tm, tk), (tm, tk)
    b_blk = (tn, tk) if mode == "nt" else (tk, tn)
    assert a_off[0] % ad[0] == 0 and a_off[1] % ad[1] == 0 and b_off[0] % b_blk[0] == 0 and b_off[1] % b_blk[1] == 0
    ao = (a_off[0] // ad[0], a_off[1] // ad[1])
    bo = (b_off[0] // b_blk[0], b_off[1] // b_blk[1])
    if mode == "tn":
        a_map = lambda i, j, k: (k + ao[0], i + ao[1])
    else:
        a_map = lambda i, j, k: (i + ao[0], k + ao[1])
    if mode == "nt":
        b_map = lambda i, j, k: (j + bo[0], k + bo[1])
    else:
        b_map = lambda i, j, k: (k + bo[0], j + bo[1])
    if b_view is not None:
        if mode == "nt":
            b_spec = _stacked(b_view, tn, tk, lambda i, j, k: j, lambda i, j, k: k)
        else:
            b_spec = _stacked(b_view, tk, tn, lambda i, j, k: k, lambda i, j, k: j)
    else:
        b_spec = pl.BlockSpec(b_blk, b_map)
    in_specs = [pl.BlockSpec(a_blk, a_map), b_spec]
    for arr, kind in extras:
        if kind == "tile":
            in_specs.append(pl.BlockSpec((tm, tn), lambda i, j, k: (i, j)))
        else:
            in_specs.append(pl.BlockSpec((1, tn), lambda i, j, k: (0, j)))
    ne, no = len(extras), len(outs)
    if epilogue is None:
        epilogue = lambda acc: (acc,)
    operands = [a, b, *[e[0] for e in extras]]
    aliases = {}
    if alias is not None:
        in_specs.append(ANY)
        aliases[len(operands)] = 0
        operands.append(alias)
    n_in = len(operands)
    if out_view is not None:
        out_specs = [_stacked(out_view, tm, tn, lambda i, j, k: i, lambda i, j, k: j)]
        out_shape = [jax.ShapeDtypeStruct(out_stack, outs[0])]
    else:
        out_specs = [pl.BlockSpec((tm, tn), lambda i, j, k: (i, j)) for _ in outs]
        out_shape = [jax.ShapeDtypeStruct((M, N), dt) for dt in outs]
    scratch = [pltpu.VMEM((tm, tn), f32)] if nk > 1 else []
    grid = (M // tm, N // tn, nk)
    if comm is not None:
        in_specs += [ANY] * len(comm["ins"])
        operands += comm["ins"]
        out_specs += [ANY] * len(comm["outs"])
        out_shape += comm["outs"]
        scratch += comm["sems"]
    nci, nco, ncs = (len(comm["ins"]), len(comm["outs"]), len(comm["sems"])) if comm is not None else (0, 0, 0)

    def body(*refs):
        a_ref, b_ref = refs[0], refs[1]
        ex, out_refs = refs[2:2 + ne], refs[n_in + nci:n_in + nci + no]
        rest = refs[n_in + nci + no + nco:]
        if comm is not None:
            cargs = (refs[n_in:n_in + nci], refs[n_in + nci + no:n_in + nci + no + nco], refs[len(refs) - ncs:])
            pid = [pl.program_id(d) for d in range(3)]

            @pl.when((pid[0] == 0) & (pid[1] == 0) & (pid[2] == 0))
            def _():
                comm["start"](*cargs)

        def finish(acc):
            res = epilogue(acc, *[e[...] for e in ex])
            for o, r in zip(out_refs, res):
                o[...] = r.astype(o.dtype)

        prod = lax.dot_general(a_ref[...].astype(bf16), b_ref[...].astype(bf16), _DIMS[mode],
                               preferred_element_type=f32)
        if nk == 1:
            finish(prod)
        else:
            acc_ref = rest[0]
            k = pl.program_id(2)

            @pl.when(k == 0)
            def _():
                acc_ref[...] = prod

            @pl.when(k > 0)
            def _():
                acc_ref[...] += prod

            @pl.when(k == nk - 1)
            def _():
                finish(acc_ref[...])

        if comm is not None:
            @pl.when((pid[0] == grid[0] - 1) & (pid[1] == grid[1] - 1) & (pid[2] == grid[2] - 1))
            def _():
                comm["finish"](*cargs)

    sem = ("arbitrary",) * 3 if comm is not None else ("parallel", "parallel", "arbitrary")
    res = pl.pallas_call(
        body, name=name, grid=grid, in_specs=in_specs, out_specs=out_specs, out_shape=out_shape,
        scratch_shapes=scratch, input_output_aliases=aliases, compiler_params=_cp(*sem),
    )(*operands)
    if comm is not None:
        return res[:no], res[no:]
    return res[0] if no == 1 else res


def _rms_fwd(x, g, out_dtype, name, tr=256):
    T, D = x.shape
    tr = min(tr, T)

    def body(x_ref, g_ref, o_ref):
        xv = x_ref[...]
        r = lax.rsqrt(jnp.mean(xv * xv, axis=-1, keepdims=True) + EPS)
        o_ref[...] = (xv * r * g_ref[...]).astype(o_ref.dtype)

    return pl.pallas_call(
        body, name=name, grid=(T // tr,),
        in_specs=[pl.BlockSpec((tr, D), lambda r: (r, 0)), pl.BlockSpec((1, D), lambda r: (0, 0))],
        out_specs=pl.BlockSpec((tr, D), lambda r: (r, 0)),
        out_shape=jax.ShapeDtypeStruct((T, D), out_dtype), compiler_params=_cp("parallel"),
    )(x, g)


def _rms_bwd(x, g, dh, dres, name, tr=256):
    T, D = x.shape
    tr = min(tr, T)

    def body(x_ref, g_ref, dh_ref, dres_ref, dx_ref, dg_ref):
        xv = x_ref[...]
        r = lax.rsqrt(jnp.mean(xv * xv, axis=-1, keepdims=True) + EPS)
        xh = xv * r
        dhv = dh_ref[...]
        dhg = dhv * g_ref[...]
        dx_ref[...] = dres_ref[...] + r * (dhg - xh * jnp.mean(dhg * xh, axis=-1, keepdims=True))

        @pl.when(pl.program_id(0) == 0)
        def _():
            dg_ref[...] = jnp.zeros_like(dg_ref)

        dg_ref[...] += jnp.sum(dhv * xh, axis=0, keepdims=True)

    row = pl.BlockSpec((tr, D), lambda r: (r, 0))
    vec = pl.BlockSpec((1, D), lambda r: (0, 0))
    return pl.pallas_call(
        body, name=name, grid=(T // tr,), in_specs=[row, vec, row, row], out_specs=[row, vec],
        out_shape=[jax.ShapeDtypeStruct((T, D), f32), jax.ShapeDtypeStruct((1, D), f32)],
        compiler_params=_cp("arbitrary"),
    )(x, g, dh, dres)


def _loss_grad(y, tgt, name, tr=256):
    T, D = y.shape
    tr = min(tr, T)

    def body(y_ref, t_ref, dy_ref, s_ref):
        e = y_ref[...] - t_ref[...]
        dy_ref[...] = e * (1.0 / D)

        @pl.when(pl.program_id(0) == 0)
        def _():
            s_ref[...] = jnp.zeros_like(s_ref)

        s_ref[...] += jnp.sum(e * e)

    row = pl.BlockSpec((tr, D), lambda r: (r, 0))
    return pl.pallas_call(
        body, name=name, grid=(T // tr,), in_specs=[row, row],
        out_specs=[row, pl.BlockSpec((8, LANES), lambda r: (0, 0))],
        out_shape=[jax.ShapeDtypeStruct((T, D), f32), jax.ShapeDtypeStruct((8, LANES), f32)],
        compiler_params=_cp("arbitrary"),
    )(y, tgt)


def _shift_down(cur, prev, s):
    rolled = pltpu.roll(cur, s, 0)
    top = pltpu.roll(prev, s, 0)
    row = lax.broadcasted_iota(jnp.int32, top.shape, 0)
    head = jnp.where(row < s, top, rolled[0:CONV_HALO])
    return jnp.concatenate([head, rolled[CONV_HALO:]], axis=0)


def _shift_up(cur, nxt, s):
    n = cur.shape[0]
    rolled = pltpu.roll(cur, n - s, 0)
    bot = pltpu.roll(nxt, CONV_HALO - s, 0)
    row = lax.broadcasted_iota(jnp.int32, bot.shape, 0)
    tail = jnp.where(row >= CONV_HALO - s, bot, rolled[n - CONV_HALO:])
    return jnp.concatenate([rolled[:n - CONV_HALO], tail], axis=0)


def _conv_pre(cur, prev, w_ref, b_ref):
    taps = [cur] + [_shift_down(cur, prev, s) for s in (1, 2, 3)]
    pre = b_ref[...] + w_ref[3:4, :] * taps[0]
    for s in (1, 2, 3):
        pre = pre + w_ref[3 - s:4 - s, :] * taps[s]
    return pre, taps


def _conv_specs(T, C, rc, cb):
    cur = pl.BlockSpec((rc, cb), lambda j, r: (r, j))
    prev = pl.BlockSpec((CONV_HALO, cb), lambda j, r: (jnp.maximum(r * (rc // CONV_HALO) - 1, 0), j))
    nxt = pl.BlockSpec((CONV_HALO, cb), lambda j, r: (jnp.minimum((r + 1) * (rc // CONV_HALO), T // CONV_HALO - 1), j))
    w = pl.BlockSpec((4, cb), lambda j, r: (0, j))
    b = pl.BlockSpec((1, cb), lambda j, r: (0, j))
    return cur, prev, nxt, w, b


def _conv_fwd(xraw, w, b, name):
    T, C = xraw.shape
    rc, cb = min(512, T), min(512, C)
    cur, prev, _, ws, bs = _conv_specs(T, C, rc, cb)

    def body(x_ref, p_ref, w_ref, b_ref, o_ref):
        pv = jnp.where(pl.program_id(1) > 0, p_ref[...], 0.0)
        pre, _ = _conv_pre(x_ref[...], pv, w_ref, b_ref)
        o_ref[...] = pre * _sigmoid(pre)

    return pl.pallas_call(
        body, name=name, grid=(C // cb, T // rc), in_specs=[cur, prev, ws, bs], out_specs=cur,
        out_shape=jax.ShapeDtypeStruct((T, C), f32), compiler_params=_cp("parallel", "parallel"),
    )(xraw, xraw, w, b)


def _conv_bwd_pre(xraw, dxbc, w, b, name):
    T, C = xraw.shape
    rc, cb = min(512, T), min(512, C)
    cur, prev, _, ws, bs = _conv_specs(T, C, rc, cb)

    def body(x_ref, p_ref, d_ref, w_ref, b_ref, dpre_ref, dw_ref, db_ref):
        pv = jnp.where(pl.program_id(1) > 0, p_ref[...], 0.0)
        pre, taps = _conv_pre(x_ref[...], pv, w_ref, b_ref)
        sg = _sigmoid(pre)
        dpre = d_ref[...] * (sg * (1.0 + pre * (1.0 - sg)))
        dpre_ref[...] = dpre

        @pl.when(pl.program_id(1) == 0)
        def _():
            dw_ref[...] = jnp.zeros_like(dw_ref)
            db_ref[...] = jnp.zeros_like(db_ref)

        row = lax.broadcasted_iota(jnp.int32, dw_ref.shape, 0)
        upd = jnp.zeros(dw_ref.shape, f32)
        for s in range(4):
            upd = upd + jnp.where(row == 3 - s, jnp.sum(dpre * taps[s], axis=0, keepdims=True), 0.0)
        dw_ref[...] += upd
        db_ref[...] += jnp.sum(dpre, axis=0, keepdims=True)

    return pl.pallas_call(
        body, name=name, grid=(C // cb, T // rc), in_specs=[cur, prev, cur, ws, bs], out_specs=[cur, ws, bs],
        out_shape=[jax.ShapeDtypeStruct((T, C), f32), jax.ShapeDtypeStruct((4, C), f32), jax.ShapeDtypeStruct((1, C), f32)],
        compiler_params=_cp("parallel", "arbitrary"),
    )(xraw, xraw, dxbc, w, b)


def _conv_bwd_in(dpre, w, name):
    T, C = dpre.shape
    rc, cb = min(512, T), min(512, C)
    cur, _, nxt, ws, _ = _conv_specs(T, C, rc, cb)
    nr = T // rc

    def body(d_ref, n_ref, w_ref, o_ref):
        nv = jnp.where(pl.program_id(1) < nr - 1, n_ref[...], 0.0)
        cv = d_ref[...]
        out = w_ref[3:4, :] * cv
        for s in (1, 2, 3):
            out = out + w_ref[3 - s:4 - s, :] * _shift_up(cv, nv, s)
        o_ref[...] = out.astype(o_ref.dtype)

    return pl.pallas_call(
        body, name=name, grid=(C // cb, nr), in_specs=[cur, nxt, ws], out_specs=cur,
        out_shape=jax.ShapeDtypeStruct((T, C), bf16), compiler_params=_cp("parallel", "parallel"),
    )(dpre, dpre, w)


HEAD_SHIFT = SSD_HEAD_DIM.bit_length() - 1


def _ssd_prep(dtr_ref, bias_ref, alog_ref, SW):
    L = SSD_CHUNK
    xs = dtr_ref[...] + bias_ref[...]
    dt = _softplus(xs)
    a = -jnp.exp(alog_ref[...])
    causal = lax.broadcasted_iota(jnp.int32, (L, L), 0) >= lax.broadcasted_iota(jnp.int32, (L, L), 1)
    cs = _mask_dot(causal.astype(bf16), dt * a, 3)
    spread = (lax.broadcasted_iota(jnp.int32, (LANES, SW), 0)
              == lax.shift_right_logical(lax.broadcasted_iota(jnp.int32, (LANES, SW), 1), HEAD_SHIFT)).astype(bf16)
    dt_x = _dot_mask(dt, spread, 3)
    cs_x = _dot_mask(cs, spread, 3)
    last_x = cs_x[L - 1:L, :]
    return xs, dt, a, causal, cs, cs.T, dt_x, jnp.exp(cs_x), jnp.exp(last_x - cs_x), jnp.exp(last_x)


def _head_decay(cs, csT, causal, h):
    seg = cs[:, h:h + 1] - csT[h:h + 1, :]
    return jnp.where(causal, jnp.exp(jnp.minimum(seg, 0.0)), 0.0)


def _ssd_fwd(xbc, dtraw, z, dt_bias, a_log, d_skip_x, out_norm, HS, name, comm=None):
    T = xbc.shape[0]
    L, P, NS, G = SSD_CHUNK, SSD_HEAD_DIM, SSD_STATE, SSD_GROUPS
    SW, HPG, nc = HS * P, HS // SSD_GROUPS, T // SSD_CHUNK
    gsz = SW // G
    gw = HPG * P
    assert HPG % 2 == 0 and 2 * P == LANES

    def body(*refs):
        (xbc_ref, dtr_ref, z_ref, bias_ref, alog_ref, dsk_ref, on_ref), (y_ref, yn_ref, sp_ref), (st_ref,), cargs = _hosted(comm, 7, 3, refs)
        first, last = _first_last((nc,))
        if comm is not None:
            @pl.when(first)
            def _():
                comm["start"](*cargs)

            @pl.when(last)
            def _():
                comm["finish"](*cargs)

        @pl.when(pl.program_id(0) == 0)
        def _():
            st_ref[...] = jnp.zeros_like(st_ref)

        sp_ref[0] = st_ref[...]
        _, _, _, causal, cs, csT, dt_x, ecs_x, dte_x, cdec_x = _ssd_prep(dtr_ref, bias_ref, alog_ref, SW)
        X = xbc_ref[:, 0:SW]
        Xd = X * dt_x
        Xdb = Xd.astype(bf16)
        XEb = (Xd * dte_x).astype(bf16)
        left = lax.broadcasted_iota(jnp.int32, (L, LANES), 1) < P
        for g in range(G):
            gs = slice(g * gw, (g + 1) * gw)
            Bb = xbc_ref[:, SW + g * NS:SW + (g + 1) * NS].astype(bf16)
            Cb = xbc_ref[:, SW + (G + g) * NS:SW + (G + g + 1) * NS].astype(bf16)
            Gm = _dot(Cb, Bb, "nt")
            Sp = st_ref[:, gs]
            yo = _dot(Cb, Sp.astype(bf16)) * ecs_x[:, gs]
            st_ref[:, gs] = cdec_x[:, gs] * Sp + _dot(Bb, XEb[:, gs], "tn")
            for pr in range(HPG // 2):
                h0 = g * HPG + 2 * pr
                ps = slice(h0 * P, (h0 + 2) * P)
                xp = Xdb[:, ps]
                yd = jnp.where(left, _dot((_head_decay(cs, csT, causal, h0) * Gm).astype(bf16), xp),
                               _dot((_head_decay(cs, csT, causal, h0 + 1) * Gm).astype(bf16), xp))
                y_ref[:, ps] = yd + yo[:, pr * LANES:(pr + 1) * LANES] + dsk_ref[:, ps] * X[:, ps]
        zz = z_ref[...]
        gated = y_ref[...] * (zz * _sigmoid(zz))
        for g in range(G):
            gs = slice(g * gsz, (g + 1) * gsz)
            sg = gated[:, gs]
            rr = lax.rsqrt(jnp.mean(sg * sg, axis=-1, keepdims=True) + EPS)
            yn_ref[:, gs] = (sg * rr * on_ref[:, gs]).astype(yn_ref.dtype)

    vec = pl.BlockSpec((1, LANES), lambda c: (0, 0))
    wide = pl.BlockSpec((1, SW), lambda c: (0, 0))
    (y, yn, sp), got = _host_call(
        body, name, (nc,),
        [pl.BlockSpec((L, xbc.shape[1]), lambda c: (c, 0)), pl.BlockSpec((L, LANES), lambda c: (c, 0)),
         pl.BlockSpec((L, SW), lambda c: (c, 0)), vec, vec, wide, wide],
        [pl.BlockSpec((L, SW), lambda c: (c, 0)), pl.BlockSpec((L, SW), lambda c: (c, 0)),
         pl.BlockSpec((1, NS, SW), lambda c: (c, 0, 0))],
        [jax.ShapeDtypeStruct((T, SW), f32), jax.ShapeDtypeStruct((T, SW), bf16), jax.ShapeDtypeStruct((nc, NS, SW), f32)],
        [pltpu.VMEM((NS, SW), f32)], [xbc, dtraw, z, dt_bias, a_log, d_skip_x, out_norm], comm)
    return y, yn, sp, got


def _ssd_bwd(xbc, dtraw, sprev, dy, dt_bias, a_log, d_skip_x, HS, name):
    T = xbc.shape[0]
    L, P, NS, G = SSD_CHUNK, SSD_HEAD_DIM, SSD_STATE, SSD_GROUPS
    SW, HPG, nc = HS * P, HS // SSD_GROUPS, T // SSD_CHUNK
    gw = HPG * P

    def body(xbc_ref, dtr_ref, sp_ref, dy_ref, bias_ref, alog_ref, dsk_ref,
             dxbc_ref, ddtr_ref, dalog_ref, dbias_ref, dd_ref, ds_ref):
        @pl.when(pl.program_id(0) == 0)
        def _():
            ds_ref[...] = jnp.zeros_like(ds_ref)
            dalog_ref[...] = jnp.zeros_like(dalog_ref)
            dbias_ref[...] = jnp.zeros_like(dbias_ref)
            dd_ref[...] = jnp.zeros_like(dd_ref)

        xs, dt, a, causal, cs, csT, dt_x, ecs_x, dte_x, cdec_x = _ssd_prep(dtr_ref, bias_ref, alog_ref, SW)
        lane = lax.broadcasted_iota(jnp.int32, (L, LANES), 1)
        sub = lax.broadcasted_iota(jnp.int32, (LANES, L), 0)
        left = lane < P
        dcs = jnp.zeros((L, LANES), f32)
        dcs_t = jnp.zeros((LANES, L), f32)
        xds = jnp.zeros((L, LANES), f32)
        dlast = jnp.zeros((1, LANES), f32)
        dD = jnp.zeros((1, LANES), f32)
        for g in range(G):
            gs = slice(g * gw, (g + 1) * gw)
            bsl = slice(SW + g * NS, SW + (g + 1) * NS)
            csl = slice(SW + (G + g) * NS, SW + (G + g + 1) * NS)
            Bb = xbc_ref[:, bsl].astype(bf16)
            Cb = xbc_ref[:, csl].astype(bf16)
            Gm = _dot(Cb, Bb, "nt")
            X = xbc_ref[:, gs]
            Xd = X * dt_x[:, gs]
            Xdb = Xd.astype(bf16)
            XE = Xd * dte_x[:, gs]
            dY = dy_ref[:, gs]
            dYb = dY.astype(bf16)
            Wb = (dY * ecs_x[:, gs]).astype(bf16)
            Sp = sp_ref[0, :, gs]
            Spb = Sp.astype(bf16)
            dS = ds_ref[:, gs]
            dSb = dS.astype(bf16)
            CS = _dot(Cb, Spb)
            Zb = _dot(Bb, dSb)
            dC = _dot(Wb, Spb, "nt")
            dB = _dot(XE.astype(bf16), dSb, "nt")
            ds_ref[:, gs] = cdec_x[:, gs] * dS + _dot(Cb, Wb, "tn")
            R1 = dY * CS * ecs_x[:, gs]
            R2 = XE * Zb
            to_head = (lax.shift_right_logical(lax.broadcasted_iota(jnp.int32, (gw, LANES), 0), HEAD_SHIFT) + g * HPG
                       == lax.broadcasted_iota(jnp.int32, (gw, LANES), 1)).astype(bf16)
            dcs = dcs + _dot_mask(R1 - R2, to_head, 3)
            dlast = (dlast + jnp.sum(_dot_mask(R2, to_head, 3), axis=0, keepdims=True)
                     + jnp.sum(_dot_mask(Sp * dS * cdec_x[:, gs], to_head, 3), axis=0, keepdims=True))
            dG = jnp.zeros((L, L), f32)
            pieces = []
            for pr in range(HPG // 2):
                h0 = g * HPG + 2 * pr
                pw = slice(pr * LANES, (pr + 1) * LANES)
                xp, dyp = Xdb[:, pw], dYb[:, pw]
                halves = []
                for k, h in enumerate((h0, h0 + 1)):
                    Lm = _head_decay(cs, csT, causal, h)
                    Mf = Lm * Gm
                    keep = left if k == 0 else jnp.logical_not(left)
                    dM = _dot(jnp.where(keep, dyp, jnp.zeros_like(dyp)), xp, "nt")
                    Q = dM * Mf
                    dcs = dcs + jnp.where(lane == h, jnp.sum(Q, axis=1, keepdims=True), 0.0)
                    dcs_t = dcs_t - jnp.where(sub == h, jnp.sum(Q, axis=0, keepdims=True), 0.0)
                    dG = dG + dM * Lm
                    halves.append(_dot(Mf.astype(bf16), dyp, "tn"))
                pieces.append(jnp.where(left, halves[0], halves[1]))
            dXd = jnp.concatenate(pieces, axis=1) + dte_x[:, gs] * Zb
            dxbc_ref[:, gs] = dXd * dt_x[:, gs] + dsk_ref[:, gs] * dY
            xds = xds + _dot_mask(dXd * X, to_head, 3)
            dD = dD + jnp.sum(_dot_mask(dY * X, to_head, 3), axis=0, keepdims=True)
            dGb = dG.astype(bf16)
            dxbc_ref[:, bsl] = dB + _dot(dGb, Cb, "tn")
            dxbc_ref[:, csl] = dC + _dot(dGb, Bb)
        rowi = lax.broadcasted_iota(jnp.int32, (L, LANES), 0)
        dcs = dcs + dcs_t.T + jnp.where(rowi == L - 1, dlast, 0.0)
        anti = (lax.broadcasted_iota(jnp.int32, (L, L), 1) >= lax.broadcasted_iota(jnp.int32, (L, L), 0)).astype(bf16)
        dda = _mask_dot(anti, dcs, 3)
        ddt = dda * a + xds
        dalog_ref[...] += jnp.sum(dda * dt, axis=0, keepdims=True) * a
        ddtr = ddt * _sigmoid(xs)
        ddtr_ref[...] = ddtr
        dbias_ref[...] += jnp.sum(ddtr, axis=0, keepdims=True)
        dd_ref[...] += dD

    rev = lambda c: (nc - 1 - c, 0)
    vec = pl.BlockSpec((1, LANES), lambda c: (0, 0))
    return pl.pallas_call(
        body, name=name, grid=(nc,),
        in_specs=[pl.BlockSpec((L, xbc.shape[1]), rev), pl.BlockSpec((L, LANES), rev),
                  pl.BlockSpec((1, NS, SW), lambda c: (nc - 1 - c, 0, 0)), pl.BlockSpec((L, SW), rev), vec, vec,
                  pl.BlockSpec((1, SW), lambda c: (0, 0))],
        out_specs=[pl.BlockSpec((L, xbc.shape[1]), rev), pl.BlockSpec((L, LANES), rev), vec, vec, vec],
        out_shape=[jax.ShapeDtypeStruct(xbc.shape, f32), jax.ShapeDtypeStruct((T, LANES), f32)]
        + [jax.ShapeDtypeStruct((1, LANES), f32)] * 3,
        scratch_shapes=[pltpu.VMEM((NS, SW), f32)], compiler_params=_cp("arbitrary"),
    )(xbc, dtraw, sprev, dy, dt_bias, a_log, d_skip_x)


def _gate_bwd(y, z, dyn, out_norm, name, tr=256):
    T, SW = y.shape
    tr = min(tr, T)
    gsz = SW // SSD_GROUPS

    def body(y_ref, z_ref, d_ref, on_ref, dy_ref, dz_ref, don_ref):
        @pl.when(pl.program_id(0) == 0)
        def _():
            don_ref[...] = jnp.zeros_like(don_ref)

        for g in range(SSD_GROUPS):
            gs = slice(g * gsz, (g + 1) * gsz)
            yv, zv, dv = y_ref[:, gs], z_ref[:, gs], d_ref[:, gs]
            sg = _sigmoid(zv)
            sl = zv * sg
            gated = yv * sl
            rr = lax.rsqrt(jnp.mean(gated * gated, axis=-1, keepdims=True) + EPS)
            gh = gated * rr
            dgn = dv * on_ref[:, gs]
            dgated = rr * (dgn - gh * jnp.mean(dgn * gh, axis=-1, keepdims=True))
            dy_ref[:, gs] = dgated * sl
            dz_ref[:, gs] = (dgated * yv * (sg * (1.0 + zv * (1.0 - sg)))).astype(dz_ref.dtype)
            don_ref[:, gs] += jnp.sum(dv * gh, axis=0, keepdims=True)

    row = pl.BlockSpec((tr, SW), lambda r: (r, 0))
    vec = pl.BlockSpec((1, SW), lambda r: (0, 0))
    return pl.pallas_call(
        body, name=name, grid=(T // tr,), in_specs=[row, row, row, vec], out_specs=[row, row, vec],
        out_shape=[jax.ShapeDtypeStruct((T, SW), f32), jax.ShapeDtypeStruct((T, SW), bf16),
                   jax.ShapeDtypeStruct((1, SW), f32)],
        compiler_params=_cp("arbitrary"),
    )(y, z, dyn, out_norm)


def _qk_norm_fwd(qkv, qn_w, kn_w, SBW, name, tr=256):
    T = qkv.shape[0]
    tr = min(tr, T)
    nh = SBW // SB_HEAD_DIM

    def body(q_ref, k_ref, v_ref, qw_ref, kw_ref, qo_ref, ko_ref, vo_ref):
        for src, w_ref, dst in ((q_ref, qw_ref, qo_ref), (k_ref, kw_ref, ko_ref)):
            for h in range(nh):
                hs = slice(h * SB_HEAD_DIM, (h + 1) * SB_HEAD_DIM)
                sv = src[:, hs]
                rr = lax.rsqrt(jnp.mean(sv * sv, axis=-1, keepdims=True) + EPS)
                dst[:, hs] = (sv * rr * w_ref[...]).astype(dst.dtype)
        vo_ref[...] = v_ref[...].astype(vo_ref.dtype)

    blk = lambda j: pl.BlockSpec((tr, SBW), lambda r: (r, j))
    vec = pl.BlockSpec((1, SB_HEAD_DIM), lambda r: (0, 0))
    out = pl.BlockSpec((tr, SBW), lambda r: (r, 0))
    return pl.pallas_call(
        body, name=name, grid=(T // tr,), in_specs=[blk(0), blk(1), blk(2), vec, vec], out_specs=[out, out, out],
        out_shape=[jax.ShapeDtypeStruct((T, SBW), bf16)] * 3, compiler_params=_cp("parallel"),
    )(qkv, qkv, qkv, qn_w, kn_w)


def _qk_norm_bwd(qkv, dqn, dkn, dv, qn_w, kn_w, SBW, name, tr=256):
    T = qkv.shape[0]
    tr = min(tr, T)
    nh = SBW // SB_HEAD_DIM

    def body(q_ref, k_ref, dq_ref, dk_ref, dv_ref, qw_ref, kw_ref, o_ref, dqw_ref, dkw_ref):
        @pl.when(pl.program_id(0) == 0)
        def _():
            dqw_ref[...] = jnp.zeros_like(dqw_ref)
            dkw_ref[...] = jnp.zeros_like(dkw_ref)

        for part, (src, d_ref, w_ref, dw_ref) in enumerate(((q_ref, dq_ref, qw_ref, dqw_ref), (k_ref, dk_ref, kw_ref, dkw_ref))):
            dw = jnp.zeros((1, SB_HEAD_DIM), f32)
            for h in range(nh):
                hs = slice(h * SB_HEAD_DIM, (h + 1) * SB_HEAD_DIM)
                os_ = slice(part * SBW + h * SB_HEAD_DIM, part * SBW + (h + 1) * SB_HEAD_DIM)
                sv, dn = src[:, hs], d_ref[:, hs]
                rr = lax.rsqrt(jnp.mean(sv * sv, axis=-1, keepdims=True) + EPS)
                xh = sv * rr
                dg = dn * w_ref[...]
                o_ref[:, os_] = (rr * (dg - xh * jnp.mean(dg * xh, axis=-1, keepdims=True))).astype(o_ref.dtype)
                dw = dw + jnp.sum(dn * xh, axis=0, keepdims=True)
            dw_ref[...] += dw
        o_ref[:, 2 * SBW:] = dv_ref[...].astype(o_ref.dtype)

    blk = lambda j: pl.BlockSpec((tr, SBW), lambda r: (r, j))
    vec = pl.BlockSpec((1, SB_HEAD_DIM), lambda r: (0, 0))
    row = pl.BlockSpec((tr, SBW), lambda r: (r, 0))
    return pl.pallas_call(
        body, name=name, grid=(T // tr,), in_specs=[blk(0), blk(1), row, row, row, vec, vec],
        out_specs=[pl.BlockSpec((tr, 3 * SBW), lambda r: (r, 0)), vec, vec],
        out_shape=[jax.ShapeDtypeStruct((T, 3 * SBW), bf16)] + [jax.ShapeDtypeStruct((1, SB_HEAD_DIM), f32)] * 2,
        compiler_params=_cp("arbitrary"),
    )(qkv, qkv, dqn, dkn, dv, qn_w, kn_w)


def _sb_logits(q, kb, scale):
    zl = _dot(q, kb, "nt") * scale
    lb = jnp.minimum(zl, 0.0) - jnp.log(1.0 + jnp.exp(-jnp.abs(zl)))
    return zl, lb, lb - zl


def _tail_update(old, r0, new_tail):
    return new_tail if r0 == 0 else jnp.concatenate([old[:r0], new_tail], axis=0)


def _hosted(comm, n_in, n_out, refs):
    nci, nco, ncs = (len(comm["ins"]), len(comm["outs"]), len(comm["sems"])) if comm is not None else (0, 0, 0)
    ins, outs = refs[:n_in], refs[n_in + nci:n_in + nci + n_out]
    scratch = refs[n_in + nci + n_out + nco:len(refs) - ncs]
    cargs = (refs[n_in:n_in + nci], refs[n_in + nci + n_out:n_in + nci + n_out + nco], refs[len(refs) - ncs:])
    return ins, outs, scratch, cargs


def _host_call(body, name, grid, in_specs, out_specs, out_shape, scratch, operands, comm):
    n_out = len(out_shape)
    in_specs, out_specs, out_shape, scratch, operands = list(in_specs), list(out_specs), list(out_shape), list(scratch), list(operands)
    io = {}
    if comm is not None:
        for src, dst in comm.get("aliases", {}).items():
            io[len(operands) + src] = n_out + dst
        in_specs += [ANY] * len(comm["ins"])
        operands += comm["ins"]
        out_specs += [ANY] * len(comm["outs"])
        out_shape += comm["outs"]
        scratch += comm["sems"]
    res = pl.pallas_call(body, name=name, grid=grid, in_specs=in_specs, out_specs=out_specs, out_shape=out_shape,
                         scratch_shapes=scratch, input_output_aliases=io,
                         compiler_params=_cp(*(("arbitrary",) * len(grid))))(*operands)
    return res[:n_out], res[n_out:]


def _first_last(grid):
    pid = [pl.program_id(d) for d in range(len(grid))]
    first, last = pid[0] == 0, pid[0] == grid[0] - 1
    for d in range(1, len(grid)):
        first, last = first & (pid[d] == 0), last & (pid[d] == grid[d] - 1)
    return first, last


def _sb_fwd(qn, kn, vb, name, comm=None, tq=2048, tk=256):
    T, W = qn.shape
    tq = min(tq, T)
    tk = min(tk, tq)
    nh, nq, dh, nd = W // SB_HEAD_DIM, T // tq, SB_HEAD_DIM, tq // tk
    scale = dh ** -0.5
    grid = (nh, nq)

    def body(*refs):
        (q_ref, k_ref, v_ref), (o_ref, c_ref), _, cargs = _hosted(comm, 3, 2, refs)
        first, last = _first_last(grid)
        if comm is not None:
            @pl.when(first)
            def _():
                comm["start"](*cargs)

        qi = pl.program_id(1)
        q = q_ref[...]
        later = (lax.broadcasted_iota(jnp.int32, (tk, tk), 0) > lax.broadcasted_iota(jnp.int32, (tk, tk), 1)).astype(bf16)

        def step(j, carry, d):
            acc, run = carry
            r0 = 0 if d is None else d * tk
            ks = pl.multiple_of(j * tk, tk)
            kb, vv = k_ref[pl.ds(ks, tk), :], v_ref[pl.ds(ks, tk), :]
            _, lb, lk = _sb_logits(q[r0:], kb, scale)
            if d is not None:
                mask = lax.broadcasted_iota(jnp.int32, lk.shape, 1) < lax.broadcasted_iota(jnp.int32, lk.shape, 0)
                lk = jnp.where(mask, lk, 0.0)
            between = _dot_mask(lk, later, 2)
            w = jnp.exp(lb + between + run[r0:])
            if d is not None:
                w = jnp.where(mask, w, 0.0)
            return (_tail_update(acc, r0, acc[r0:] + _dot(w.astype(bf16), vv)),
                    _tail_update(run, r0, run[r0:] + between[:, 0:1] + lk[:, 0:1]))

        carry = (jnp.zeros((tq, dh), f32), jnp.zeros((tq, 1), f32))
        for d in range(nd - 1, -1, -1):
            carry = step(qi * nd + d, carry, d)
        n_before = qi * nd
        acc, run = lax.fori_loop(0, n_before, lambda t, c: step(n_before - 1 - t, c, None), carry)
        o_ref[...] = acc.astype(o_ref.dtype)
        c_ref[...] = jnp.broadcast_to(run, (tq, dh))
        if comm is not None:
            @pl.when(last)
            def _():
                comm["finish"](*cargs)

    qblk = pl.BlockSpec((tq, dh), lambda h, i: (i, h))
    full = pl.BlockSpec((T, dh), lambda h, i: (0, h))
    (o, c), got = _host_call(body, name, grid, [qblk, full, full], [qblk, qblk],
                             [jax.ShapeDtypeStruct((T, W), bf16), jax.ShapeDtypeStruct((T, W), f32)], [], [qn, kn, vb], comm)
    return o, c, got


def _sb_bwd(qn, kn, vb, do, ctot, do_off, name, comm=None, tq=2048, tk=256):
    T, W = qn.shape
    tq = min(tq, T)
    tk = min(tk, tq)
    nh, nq, dh, nd = W // SB_HEAD_DIM, T // tq, SB_HEAD_DIM, tq // tk
    scale = dh ** -0.5
    ob = do_off // dh
    grid = (nh, nq)

    def body(*refs):
        (q_ref, k_ref, v_ref, do_ref, c_ref), (dq_ref, dk_ref, dv_ref), _, cargs = _hosted(comm, 5, 3, refs)
        first, last = _first_last(grid)
        if comm is not None:
            @pl.when(first)
            def _():
                comm["start"](*cargs)

        qi = pl.program_id(1)

        @pl.when(qi == 0)
        def _():
            dk_ref[...] = jnp.zeros_like(dk_ref)
            dv_ref[...] = jnp.zeros_like(dv_ref)

        q = q_ref[...]
        dob = do_ref[...].astype(bf16)
        total = c_ref[:, 0:1]
        r2 = lax.broadcasted_iota(jnp.int32, (tk, tk), 0)
        c2 = lax.broadcasted_iota(jnp.int32, (tk, tk), 1)
        upto = (r2 <= c2).astype(bf16)
        before = (r2 < c2).astype(bf16)

        def step(j, carry, d):
            dq, pre, gpre = carry
            r0 = 0 if d is None else d * tk
            ks = pl.multiple_of(j * tk, tk)
            kb, vv = k_ref[pl.ds(ks, tk), :], v_ref[pl.ds(ks, tk), :]
            qs, dos = q[r0:], dob[r0:]
            _, lb, lk = _sb_logits(qs, kb, scale)
            if d is not None:
                mask = lax.broadcasted_iota(jnp.int32, lk.shape, 1) < lax.broadcasted_iota(jnp.int32, lk.shape, 0)
                lk = jnp.where(mask, lk, 0.0)
            pin = _dot_mask(lk, upto, 2)
            w = jnp.exp(lb + (total[r0:] - pre[r0:] - pin))
            if d is not None:
                w = jnp.where(mask, w, 0.0)
            dw = _dot(dos, vv, "nt")
            dv_ref[pl.ds(ks, tk), :] += _dot(w.astype(bf16), dos, "tn")
            gg = dw * w
            gex = _dot(gg.astype(bf16), before)
            beta = jnp.exp(lb)
            dz = (gg * (1.0 - beta) - (gpre[r0:] + gex) * beta) * scale
            if d is not None:
                dz = jnp.where(mask, dz, 0.0)
            dzb = dz.astype(bf16)
            dk_ref[pl.ds(ks, tk), :] += _dot(dzb, qs, "tn")
            return (_tail_update(dq, r0, dq[r0:] + _dot(dzb, kb)),
                    _tail_update(pre, r0, pre[r0:] + pin[:, tk - 1:tk]),
                    _tail_update(gpre, r0, gpre[r0:] + gex[:, tk - 1:tk] + gg[:, tk - 1:tk]))

        init = (jnp.zeros((tq, dh), f32), jnp.zeros((tq, 1), f32), jnp.zeros((tq, 1), f32))
        carry = lax.fori_loop(0, qi * nd, lambda t, c: step(t, c, None), init)
        for d in range(nd):
            carry = step(qi * nd + d, carry, d)
        dq_ref[...] = carry[0]
        if comm is not None:
            @pl.when(last)
            def _():
                comm["finish"](*cargs)

    qblk = pl.BlockSpec((tq, dh), lambda h, i: (i, h))
    full = pl.BlockSpec((T, dh), lambda h, i: (0, h))
    (dq, dk, dv), got = _host_call(
        body, name, grid, [qblk, full, full, pl.BlockSpec((tq, dh), lambda h, i: (i, h + ob)), qblk], [qblk, full, full],
        [jax.ShapeDtypeStruct((T, W), f32)] * 3, [], [qn, kn, vb, do, ctot], comm)
    return dq, dk, dv, got


def _pool_select(sums, g):
    return jnp.where(g == 0, sums[0], jnp.where(g == 1, sums[1], jnp.where(g == 2, sums[2], sums[3])))


def _pool_count(g, r, rc, n, cols, off=0):
    t = (r * rc + off + lax.broadcasted_iota(jnp.int32, (n, cols), 0)).astype(f32)
    win = jnp.left_shift(2, g).astype(f32)
    return jnp.minimum(t + 1.0, win)


def _pool_fwd(hp, xres, w, b, scale, name, rc=512, comm=None):
    T, D = hp.shape
    rc = min(rc, T)
    pg = D // len(POOL_WINDOWS)
    grid = (len(POOL_WINDOWS), T // rc)

    def body(*refs):
        (h_ref, p_ref, x_ref, w_ref, b_ref, s_ref), (o_ref, yp_ref, d_ref), _, cargs = _hosted(comm, 6, 3, refs)
        if comm is not None:
            first, last = _first_last(grid)

            @pl.when(first)
            def _():
                comm["start"](*cargs)

            @pl.when(last)
            def _():
                comm["finish"](*cargs)

        g, r = pl.program_id(0), pl.program_id(1)
        cur = h_ref[...]
        halo = jnp.where(r > 0, p_ref[...], 0.0)
        ext = jnp.concatenate([halo, cur], axis=0)
        sums, s = [], ext
        for sh in (1, 2, 4, 8):
            s = s + pltpu.roll(s, sh, 0)
            sums.append(s)
        d = _pool_select(sums, g)[POOL_HALO:] / _pool_count(g, r, rc, rc, pg) - cur
        yp = _dot(d.astype(bf16), w_ref[0]) + b_ref[...]
        yp_ref[...] = yp
        d_ref[...] = d.astype(d_ref.dtype)
        o_ref[...] = x_ref[...] + yp * s_ref[...]

    cur = pl.BlockSpec((rc, pg), lambda g, r: (r, g))
    prev = pl.BlockSpec((POOL_HALO, pg), lambda g, r: (jnp.maximum(r * (rc // POOL_HALO) - 1, 0), g))
    vec = pl.BlockSpec((1, pg), lambda g, r: (0, g))
    (o, yp, d), got = _host_call(
        body, name, grid, [cur, prev, cur, pl.BlockSpec((1, pg, pg), lambda g, r: (g, 0, 0)), vec, vec], [cur, cur, cur],
        [jax.ShapeDtypeStruct((T, D), f32), jax.ShapeDtypeStruct((T, D), f32), jax.ShapeDtypeStruct((T, D), bf16)],
        [], [hp, hp, xres, w, b, scale], comm)
    return o, yp, d, got


def _pool_bwd(dx, yp, d, w, scale, name, rc=512):
    T, D = dx.shape
    rc = min(rc, T)
    pg = D // len(POOL_WINDOWS)
    nr = T // rc

    def body(dx_ref, dn_ref, yp_ref, d_ref, w_ref, s_ref, dh_ref, dw_ref, db_ref, dsc_ref):
        g, r = pl.program_id(0), pl.program_id(1)

        @pl.when(r == 0)
        def _():
            dw_ref[...] = jnp.zeros_like(dw_ref)
            db_ref[...] = jnp.zeros_like(db_ref)
            dsc_ref[...] = jnp.zeros_like(dsc_ref)

        dxv = dx_ref[...]
        dyp = dxv * s_ref[...]
        dsc_ref[...] += jnp.sum(dxv * yp_ref[...], axis=0, keepdims=True)
        db_ref[...] += jnp.sum(dyp, axis=0, keepdims=True)
        dypb = dyp.astype(bf16)
        dw_ref[0] += _dot(d_ref[...], dypb, "tn")
        dd = _dot(dypb, w_ref[0], "nt")
        ddn = _dot((dn_ref[...] * s_ref[...]).astype(bf16), w_ref[0], "nt")
        e = dd / _pool_count(g, r, rc, rc, pg)
        en = jnp.where(r < nr - 1, ddn / _pool_count(g, r, rc, POOL_HALO, pg, off=rc), 0.0)
        ext = jnp.concatenate([e, en], axis=0)
        sums, s = [], ext
        for sh in (1, 2, 4, 8):
            s = s + pltpu.roll(s, rc + POOL_HALO - sh, 0)
            sums.append(s)
        dh_ref[...] = _pool_select(sums, g)[:rc] - dd

    cur = pl.BlockSpec((rc, pg), lambda g, r: (r, g))
    nxt = pl.BlockSpec((POOL_HALO, pg), lambda g, r: (jnp.minimum((r + 1) * (rc // POOL_HALO), T // POOL_HALO - 1), g))
    vec = pl.BlockSpec((1, pg), lambda g, r: (0, g))
    wsp = pl.BlockSpec((1, pg, pg), lambda g, r: (g, 0, 0))
    return pl.pallas_call(
        body, name=name, grid=(len(POOL_WINDOWS), nr), in_specs=[cur, nxt, cur, cur, wsp, vec],
        out_specs=[cur, wsp, vec, vec],
        out_shape=[jax.ShapeDtypeStruct((T, D), f32), jax.ShapeDtypeStruct(w.shape, f32),
                   jax.ShapeDtypeStruct((1, D), f32), jax.ShapeDtypeStruct((1, D), f32)],
        compiler_params=_cp("parallel", "arbitrary"),
    )(dx, dx, yp, d, w, scale)


def _adamw(w, g, m, v, name, tr=256, comm=None):
    R, C = w.shape
    tr = min(tr, R)
    lanes = -(-C // LANES) * LANES
    while tr > 8 and 2 * 8 * tr * lanes * 4 > MM_TILE_BUDGET:
        tr //= 2
    assert R % tr == 0
    grid = (R // tr,)

    def body(*refs):
        (w_ref, g_ref, m_ref, v_ref), (d_ref, mo_ref, vo_ref, go_ref), _, cargs = _hosted(comm, 4, 4, refs)
        if comm is not None:
            first, last = _first_last(grid)

            @pl.when(first)
            def _():
                comm["start"](*cargs)

            @pl.when(last)
            def _():
                comm["finish"](*cargs)

        gv = g_ref[...]
        mn = ADAM_B1 * m_ref[...] + (1.0 - ADAM_B1) * gv
        vn = ADAM_B2 * v_ref[...] + (1.0 - ADAM_B2) * (gv * gv)
        m_hat = mn / (1.0 - ADAM_B1 ** ADAM_STEP)
        v_hat = vn / (1.0 - ADAM_B2 ** ADAM_STEP)
        d_ref[...] = -ADAM_LR * (m_hat / (jnp.sqrt(v_hat) + ADAM_EPS) + ADAM_WD * w_ref[...])
        mo_ref[...] = mn
        vo_ref[...] = vn
        go_ref[...] = gv

    blk = pl.BlockSpec((tr, C), lambda r: (r, 0))
    outs, got = _host_call(body, name, grid, [blk] * 4, [blk] * 4, [jax.ShapeDtypeStruct((R, C), f32)] * 4, [], [w, g, m, v], comm)
    return (*outs, got)


def _pair_sum(g4, recv, name, br=256):
    _, R, C = g4.shape
    hr = R // 2
    br = min(br, hr)
    nb = hr // br

    def body(a_ref, b_ref, o_ref):
        o_ref[...] = (a_ref[...] + b_ref[...]).astype(o_ref.dtype)

    out = pl.BlockSpec((1, br, C), lambda s, i: (s, i, 0))
    return pl.pallas_call(
        body, name=name, grid=(N_CHIPS, nb),
        in_specs=[pl.BlockSpec((1, br, C), lambda s, i: (s, lax.axis_index("c") * nb + i, 0)), out], out_specs=out,
        out_shape=jax.ShapeDtypeStruct((N_CHIPS, hr, C), bf16), compiler_params=_cp("parallel", "parallel"),
    )(g4, recv)


def _chip_sum(g4, recv, pieces, name, br=256):
    _, hr, C = recv.shape
    br = min(br, hr)
    nb = hr // br
    chip = lambda: 2 * lax.axis_index("x") + lax.axis_index("y")

    def body(a_ref, r_ref, b1_ref, b2_ref, b3_ref, o_ref):
        o_ref[...] = (((a_ref[0] + r_ref[0]) + b1_ref[0].astype(f32)) + b2_ref[0].astype(f32)) + b3_ref[0].astype(f32)

    other = lambda k: pl.BlockSpec((1, br, C), lambda i: ((chip() + k) % N_CHIPS, i, 0))
    return pl.pallas_call(
        body, name=name, grid=(nb,),
        in_specs=[pl.BlockSpec((1, br, C), lambda i: (chip(), lax.axis_index("c") * nb + i, 0)),
                  pl.BlockSpec((1, br, C), lambda i: (chip(), i, 0)), other(1), other(2), other(3)],
        out_specs=pl.BlockSpec((br, C), lambda i: (lax.axis_index("c") * nb + i, 0)),
        out_shape=jax.ShapeDtypeStruct((2 * hr, C), f32), compiler_params=_cp("parallel"),
    )(g4, recv, pieces, pieces, pieces)


ANY = pl.BlockSpec(memory_space=pl.ANY)


def _mesh_pos():
    x, y, c = lax.axis_index("x"), lax.axis_index("y"), lax.axis_index("c")
    others = [(1 - x, y), (x, 1 - y), (1 - x, 1 - y)]
    return x, y, c, 2 * x + y, others


def _gather_small(blk, name):
    m, n = blk.shape

    def body(x_ref, out_ref, sum_ref, send_sems, recv_sems, local_sem):
        x, y, c, _, others = _mesh_pos()
        me, sibling = (x, y, c), (x, y, 1 - c)

        def rows(px, py, pc):
            return out_ref.at[pl.ds((4 * px + 2 * py + pc) * m, m), :]

        def copy(k, block, to, src=None):
            return pltpu.make_async_remote_copy(
                src_ref=rows(*block) if src is None else src, dst_ref=rows(*block),
                send_sem=send_sems.at[k], recv_sem=recv_sems.at[k], device_id=to, device_id_type=MESH)

        mine = pltpu.make_async_copy(x_ref, rows(*me), local_sem)
        mine.start()
        first = [copy(0, me, sibling, src=x_ref)]
        first += [copy(1 + j, me, (*chip, c), src=x_ref) for j, chip in enumerate(others)]
        for cp in first:
            cp.start()
        passed = [copy(4 + j, (*chip, c), sibling) for j, chip in enumerate(others)]
        for j, chip in enumerate(others):
            copy(1 + j, (*chip, c), me).wait_recv()
            passed[j].start()
        copy(0, sibling, me).wait_recv()
        for j, chip in enumerate(others):
            copy(4 + j, (*chip, 1 - c), me).wait_recv()
        for cp in first + passed:
            cp.wait_send()
        mine.wait()
        acc = out_ref[0:m, :]
        for d in range(1, 8):
            acc = acc + out_ref[d * m:(d + 1) * m, :]
        sum_ref[...] = acc

    vm = pl.BlockSpec(memory_space=pltpu.VMEM)
    return pl.pallas_call(
        body, name=name, in_specs=[vm], out_specs=[vm, vm],
        out_shape=[jax.ShapeDtypeStruct((8 * m, n), f32), jax.ShapeDtypeStruct((m, n), f32)],
        scratch_shapes=[pltpu.SemaphoreType.DMA((7,)), pltpu.SemaphoreType.DMA((7,)), pltpu.SemaphoreType.DMA],
    )(blk)


def _copy(src, dst, sems, idx, to):
    return pltpu.make_async_remote_copy(src_ref=src, dst_ref=dst, send_sem=sems[0].at[idx], recv_sem=sems[1].at[idx],
                                        device_id=to, device_id_type=MESH)


def _gather_ici(shards):
    nt = len(shards)

    def copies(ins, outs, sems):
        x, y, c, chip, others = _mesh_pos()
        send, land = [], []
        for t in range(nt):
            hr = ins[t].shape[0] // 2
            for j, (px, py) in enumerate(others):
                send.append((ins[t].at[pl.ds(c * hr, hr)], outs[t].at[chip, pl.ds(c * hr, hr)], sems, (t, j), (px, py, c)))
                piece = outs[t].at[2 * px + py, pl.ds(c * hr, hr)]
                land.append((piece, piece, sems, (t, j), (px, py, c)))
        return send, land

    def start(ins, outs, sems):
        for args in copies(ins, outs, sems)[0]:
            _copy(*args).start()

    def finish(ins, outs, sems):
        send, land = copies(ins, outs, sems)
        for args in land:
            _copy(*args).wait_recv()
        for args in send:
            _copy(*args).wait_send()

    return dict(ins=list(shards), outs=[jax.ShapeDtypeStruct((N_CHIPS,) + s.shape, s.dtype) for s in shards],
                sems=[pltpu.SemaphoreType.DMA((nt, 3)), pltpu.SemaphoreType.DMA((nt, 3))], start=start, finish=finish)


def _gather_d2d(stacks):
    nt = len(stacks)

    def copies(ins, outs, sems):
        x, y, c, _, others = _mesh_pos()
        send, land = [], []
        for t in range(nt):
            hr = outs[t].shape[1] // 2
            for j, (px, py) in enumerate(others):
                mine = outs[t].at[2 * px + py, pl.ds(c * hr, hr)]
                theirs = outs[t].at[2 * px + py, pl.ds((1 - c) * hr, hr)]
                send.append((mine, mine, sems, (t, j), (x, y, 1 - c)))
                land.append((theirs, theirs, sems, (t, j), (x, y, 1 - c)))
        return send, land

    def start(ins, outs, sems):
        for args in copies(ins, outs, sems)[0]:
            _copy(*args).start()

    def finish(ins, outs, sems):
        send, land = copies(ins, outs, sems)
        for args in land:
            _copy(*args).wait_recv()
        for args in send:
            _copy(*args).wait_send()

    return dict(ins=list(stacks), outs=[jax.ShapeDtypeStruct(s.shape, s.dtype) for s in stacks],
                sems=[pltpu.SemaphoreType.DMA((nt, 3)), pltpu.SemaphoreType.DMA((nt, 3))], start=start, finish=finish,
                aliases={t: t for t in range(nt)})


def _run_exchange(comm, name):
    ni, no = len(comm["ins"]), len(comm["outs"])

    def body(*refs):
        args = (refs[:ni], refs[ni:ni + no], refs[ni + no:])
        comm["start"](*args)
        comm["finish"](*args)

    return pl.pallas_call(
        body, name=name, in_specs=[ANY] * ni, out_specs=[ANY] * no, out_shape=comm["outs"], scratch_shapes=comm["sems"],
        input_output_aliases=dict(comm.get("aliases", {})))(*comm["ins"])


def _swap_halves(g4s):
    nt = len(g4s)

    def copies(ins, outs, sems):
        x, y, c, _, _ = _mesh_pos()
        both = []
        for t in range(nt):
            hr = ins[t].shape[1] // 2
            both.append((ins[t].at[:, pl.ds((1 - c) * hr, hr)], outs[t], sems, t, (x, y, 1 - c)))
        return both, both

    def start(ins, outs, sems):
        for args in copies(ins, outs, sems)[0]:
            _copy(*args).start()

    def finish(ins, outs, sems):
        send, land = copies(ins, outs, sems)
        for args in land:
            _copy(*args).wait_recv()
        for args in send:
            _copy(*args).wait_send()

    return dict(ins=list(g4s), outs=[jax.ShapeDtypeStruct((N_CHIPS, g.shape[1] // 2, g.shape[2]), g.dtype) for g in g4s],
                sems=[pltpu.SemaphoreType.DMA((nt,)), pltpu.SemaphoreType.DMA((nt,))], start=start, finish=finish)


def _exchange_chips(h4s):
    nt = len(h4s)

    def copies(ins, outs, sems):
        x, y, c, chip, others = _mesh_pos()
        send, land = [], []
        for t in range(nt):
            for j, (px, py) in enumerate(others):
                send.append((ins[t].at[2 * px + py], outs[t].at[chip], sems, (t, j), (px, py, c)))
                landed = outs[t].at[2 * px + py]
                land.append((landed, landed, sems, (t, j), (px, py, c)))
        return send, land

    def start(ins, outs, sems):
        for args in copies(ins, outs, sems)[0]:
            _copy(*args).start()

    def finish(ins, outs, sems):
        send, land = copies(ins, outs, sems)
        for args in land:
            _copy(*args).wait_recv()
        for args in send:
            _copy(*args).wait_send()

    return dict(ins=list(h4s), outs=[jax.ShapeDtypeStruct(h.shape, h.dtype) for h in h4s],
                sems=[pltpu.SemaphoreType.DMA((nt, 3)), pltpu.SemaphoreType.DMA((nt, 3))], start=start, finish=finish)


def _join_halves(fs):
    nt = len(fs)

    def copies(ins, outs, sems):
        x, y, c, _, _ = _mesh_pos()
        send, land = [], []
        for t in range(nt):
            hr = outs[t].shape[0] // 2
            mine, theirs = outs[t].at[pl.ds(c * hr, hr)], outs[t].at[pl.ds((1 - c) * hr, hr)]
            send.append((mine, mine, sems, t, (x, y, 1 - c)))
            land.append((theirs, theirs, sems, t, (x, y, 1 - c)))
        return send, land

    def start(ins, outs, sems):
        for args in copies(ins, outs, sems)[0]:
            _copy(*args).start()

    def finish(ins, outs, sems):
        send, land = copies(ins, outs, sems)
        for args in land:
            _copy(*args).wait_recv()
        for args in send:
            _copy(*args).wait_send()

    return dict(ins=list(fs), outs=[jax.ShapeDtypeStruct(f.shape, f.dtype) for f in fs],
                sems=[pltpu.SemaphoreType.DMA((nt,)), pltpu.SemaphoreType.DMA((nt,))], start=start, finish=finish,
                aliases={t: t for t in range(nt)})


def _pad_lanes(v, n=LANES):
    return jnp.pad(v, ((0, 0), (0, n - v.shape[-1])))


def _mlp_fwd(xin, norm_g, w_up, w_down, F, tag, comms=(None, None)):
    T, D = xin.shape
    h = _rms_fwd(xin, norm_g, bf16, f"{tag}_norm")

    def relu_sq(acc):
        r = jnp.maximum(acc, 0.0)
        return r, r * r

    got = [None, None]
    ua = _mm(h, w_up[0], "nn", T, F, D, (bf16, bf16), f"{tag}_up", epilogue=relu_sq, b_view=w_up[1], comm=comms[0])
    (u, a), got[0] = ua if comms[0] is not None else (ua, None)
    out = _mm(a, w_down[0], "nn", T, D, F, (f32,), f"{tag}_down", epilogue=lambda acc, res: (res + acc,),
              extras=((xin, "tile"),), b_view=w_down[1], comm=comms[1])
    (out,), got[1] = out if comms[1] is not None else ((out,), None)
    return out, (xin, h, u, a), got


def _mlp_bwd(dy, saved, norm_g, w_up, w_down, F, tag, up_to=None, down_to=None, host=None):
    xin, h, u, a = saved
    T, D = xin.shape
    to = lambda t: {} if t is None else dict(out_view=t[0], out_stack=t[1], alias=t[2])
    du = _mm(dy, w_down[0], "nt", T, F, D, (bf16,), f"{tag}_dact", epilogue=lambda acc, uu: (acc * (2.0 * uu.astype(f32)),),
             extras=((u, "tile"),), b_view=w_down[1])
    dw_down = _mm(a, dy, "tn", F, D, T, (f32,), f"{tag}_dwdown", **to(down_to))
    dw_up = _mm(h, du, "tn", D, F, T, (f32,), f"{tag}_dwup", **to(up_to))
    dh = _mm(du, w_up[0], "nt", T, D, F, (f32,), f"{tag}_dh", b_view=w_up[1], comm=host(dw_up, dw_down) if host else None)
    (dh,), got = dh if host else ((dh,), None)
    dx, dg = _rms_bwd(xin, norm_g, dh, dy, f"{tag}_dnorm")
    return dx, dg, dw_up, dw_down, got


def _local_step(xc, tgt, W, HS, SBW, net=None):
    T, D = xc.shape
    SW = HS * SSD_HEAD_DIM
    CD = W["conv_b"].shape[-1]
    mlp_norm = W["mlp_norm"]
    add = lambda acc, prev: (prev + acc,)

    h0 = _rms_fwd(xc, W["hyb_norm"], bf16, "hyb_norm")
    z = _mm(h0, W["w_z"], "nn", T, SW, D, (f32,), "proj_z")
    xraw = _mm(h0, W["w_xbc"], "nn", T, CD, D, (f32,), "proj_xbc")
    dtraw = _mm(h0, W["w_dt"], "nn", T, LANES, D, (f32,), "proj_dt")
    qkv = _mm(h0, W["w_qkv"], "nn", T, 3 * SBW, D, (f32,), "proj_qkv")
    qn, kn, vb = _qk_norm_fwd(qkv, W["q_norm"], W["k_norm"], SBW, "qk_norm")
    y_sb, ctot, got = _sb_fwd(qn, kn, vb, "sb_attn", comm=net.rest_ici() if net else None)
    xbc = _conv_fwd(xraw, W["conv_w"], W["conv_b"], "conv")
    y_ssd, yn_ssd, sprev, got = _ssd_fwd(xbc, dtraw, z, W["dt_bias"], W["a_log"], W["d_skip"], W["out_norm"], HS, "ssd",
                                         comm=net.rest_d2d(got) if net else None)
    if net:
        W = {**W, **net.rest_weights(got)}
    w_up, w_down, F = W["w_up"], W["w_down"], W["F"]
    mix = _mm(yn_ssd, W["w_out"], "nn", T, D, SW, (f32,), "out_ssd", epilogue=add, extras=((xc, "tile"),))
    x1 = _mm(y_sb, W["w_out"], "nn", T, D, SBW, (f32,), "out_sb", epilogue=add, extras=((mix, "tile"),), b_off=(SW, 0))
    x2, mlp0, got = _mlp_fwd(x1, mlp_norm[0:1], w_up[0], w_down[0], F, "mlp0",
                             comms=(net.last_ici(0), net.last_ici(1)) if net else (None, None))
    hp = _rms_fwd(x2, W["pool_norm"], f32, "pool_norm")
    x3, yp, dpool, got = _pool_fwd(hp, x2, W["w_pool"], W["pool_b"], W["pool_scale"], "pool",
                                   comm=net.last_d2d(got) if net else None)
    if net:
        w_up, w_down = net.last_weights(got, w_up, w_down)
    x4, mlp1, _ = _mlp_fwd(x3, mlp_norm[1:2], w_up[1], w_down[1], F, "mlp1")

    dy, sq = _loss_grad(x4, tgt, "loss")

    up_to, down_to = (net.mlp_to("up", 1, None), net.mlp_to("down", 1, None)) if net else (None, None)
    dx3, dg_mlp1, dw_up1, dw_down1, _ = _mlp_bwd(dy, mlp1, mlp_norm[1:2], w_up[1], w_down[1], F, "mlp1", up_to, down_to)
    dhp, dw_pool, db_pool, dsc_pool = _pool_bwd(dx3, yp, dpool, W["w_pool"], W["pool_scale"], "pool_bwd")
    dx2, dg_pool = _rms_bwd(x2, W["pool_norm"], dhp, dx3, "pool_dnorm")
    up_to, down_to = (net.mlp_to("up", 0, dw_up1), net.mlp_to("down", 0, dw_down1)) if net else (None, None)
    dx1, dg_mlp0, dw_up0, dw_down0, got_mlp = _mlp_bwd(dx2, mlp0, mlp_norm[0:1], w_up[0], w_down[0], F, "mlp0", up_to, down_to,
                                                       host=net.swap_mlp if net else None)

    dw_out = jnp.concatenate([_mm(yn_ssd, dx1, "tn", SW, D, T, (f32,), "dwout_ssd"),
                              _mm(y_sb, dx1, "tn", SBW, D, T, (f32,), "dwout_sb")], axis=0)
    if net:
        (dmerged,), got = _mm(dx1, W["w_out"], "nt", T, SW + SBW, D, (f32,), "dmerged",
                              comm=net.swap_rest(dw_out, dw_pool))
        dqn, dkn, dvv, got = _sb_bwd(qn, kn, vb, dmerged, ctot, SW, "sb_attn_bwd", comm=net.reduce_early(list(got) + list(got_mlp)))
        net.reduce_early_done(got)
    else:
        dmerged = _mm(dx1, W["w_out"], "nt", T, SW + SBW, D, (f32,), "dmerged")
        dqn, dkn, dvv, _ = _sb_bwd(qn, kn, vb, dmerged, ctot, SW, "sb_attn_bwd")
    dqkv, dg_q, dg_k = _qk_norm_bwd(qkv, dqn, dkn, dvv, W["q_norm"], W["k_norm"], SBW, "qk_norm_bwd")
    dy_ssd, dz, dg_on = _gate_bwd(y_ssd, z, dmerged, W["out_norm"], "gate_bwd")
    dxbc, ddtraw, dalog, dbias, ddskip = _ssd_bwd(xbc, dtraw, sprev, dy_ssd, W["dt_bias"], W["a_log"], W["d_skip"], HS, "ssd_bwd")
    dpre, dconv_w, dconv_b = _conv_bwd_pre(xraw, dxbc, W["conv_w"], W["conv_b"], "conv_bwd_pre")
    dxraw = _conv_bwd_in(dpre, W["conv_w"], "conv_bwd_in")
    dw_in = [_mm(h0, dz, "tn", D, SW, T, (f32,), "dwin_z"), _mm(h0, dxraw, "tn", D, CD, T, (f32,), "dwin_xbc"),
             _mm(h0, ddtraw, "tn", D, LANES, T, (f32,), "dwin_dt")[:, :HS], _mm(h0, dqkv, "tn", D, 3 * SBW, T, (f32,), "dwin_qkv")]
    dh0 = _mm(dz, W["w_z"], "nt", T, D, SW, (f32,), "dh0_z")
    if net:
        (dh0,), got = _mm(dxraw, W["w_xbc"], "nt", T, D, CD, (f32,), "dh0_xbc", epilogue=add, extras=((dh0, "tile"),),
                          comm=net.late_swap(dw_in))
        dh0 = _mm(ddtraw, W["w_dt"], "nt", T, D, LANES, (f32,), "dh0_dt", epilogue=add, extras=((dh0, "tile"),))
        (dh0,), got = _mm(dqkv, W["w_qkv"], "nt", T, D, 3 * SBW, (f32,), "dh0_qkv", epilogue=add, extras=((dh0, "tile"),),
                          comm=net.reduce_late(got))
        net.late_part_done(0, got)
    else:
        dh0 = _mm(dxraw, W["w_xbc"], "nt", T, D, CD, (f32,), "dh0_xbc", epilogue=add, extras=((dh0, "tile"),))
        dh0 = _mm(ddtraw, W["w_dt"], "nt", T, D, LANES, (f32,), "dh0_dt", epilogue=add, extras=((dh0, "tile"),))
        dh0 = _mm(dqkv, W["w_qkv"], "nt", T, D, 3 * SBW, (f32,), "dh0_qkv", epilogue=add, extras=((dh0, "tile"),))
    grad_x, dg_hyb = _rms_bwd(xc, W["hyb_norm"], dh0, dx1, "hyb_dnorm")
    grads = dict(w_in=dw_in, w_out=dw_out, w_pool=dw_pool, w_up=(dw_up0, dw_up1), w_down=(dw_down0, dw_down1),
                 hyb_norm=dg_hyb, conv_w=dconv_w, conv_b=dconv_b, dt_bias=dbias, a_log=dalog, d_skip=ddskip, out_norm=dg_on,
                 q_norm=dg_q, k_norm=dg_k, mlp_norm=(dg_mlp0, dg_mlp1), pool_norm=dg_pool, pool_b=db_pool, pool_scale=dsc_pool)
    return sq, grad_x, grads


def _stack_columns(pieces, n):
    cs = sum(p.shape[1] for p in pieces) // n
    slots = []
    for j in range(n):
        parts, off = [], 0
        for p in pieces:
            lo, hi = max(j * cs, off), min((j + 1) * cs, off + p.shape[1])
            if lo < hi:
                parts.append(p[:, lo - off:hi - off])
            off += p.shape[1]
        slots.append(parts[0] if len(parts) == 1 else jnp.concatenate(parts, axis=1))
    return jnp.stack(slots)


class _Net:
    def __init__(self, own_first, own_last, chip, dims):
        self.own, self.own_last, self.chip, self.dims = own_first, own_last, chip, dims

    def _place_own(self, stacks, own):
        return [lax.dynamic_update_index_in_dim(g, o, self.chip, 0) for g, o in zip(stacks, own)]

    def rest_ici(self):
        return _gather_ici(self.own)

    def rest_d2d(self, got):
        return _gather_d2d(list(got))

    def rest_weights(self, got):
        d, nw = self.dims, len(POOL_WINDOWS)
        D, F, PG = d["D"], d["F"], d["PG"]
        fs = F // N_CHIPS
        g_out, g_pool, g_up, g_down = self._place_own(got, self.own)
        w_pool = g_pool.reshape(N_CHIPS, nw, PG // N_CHIPS, PG).transpose(1, 0, 2, 3).reshape(nw, PG, PG)
        return dict(w_out=g_out.reshape(d["MIX"], D), w_pool=w_pool, F=F,
                    w_up=[(g_up, ("cols", fs, 0, D))], w_down=[(g_down, ("rows", fs, 0, None))])

    def last_ici(self, which):
        return _gather_ici([self.own_last[which]])

    def last_d2d(self, got):
        return _gather_d2d([got[0][0], got[1][0]])

    def last_weights(self, stacks, w_up, w_down):
        d = self.dims
        fs = d["F"] // N_CHIPS
        g_up, g_down = self._place_own(stacks, self.own_last)
        return w_up + [(g_up, ("cols", fs, 0, d["D"]))], w_down + [(g_down, ("rows", fs, 0, None))]

    def mlp_to(self, which, layer, earlier):
        d = self.dims
        fs = d["F"] // N_CHIPS
        if which == "up":
            return ("cols", fs, layer, d["D"]), (N_CHIPS, d["NL"] * d["D"], fs), earlier
        return ("rows", fs, layer, None), (N_CHIPS, d["NL"] * fs, d["D"]), earlier

    def swap_mlp(self, g_up, g_down):
        self.g_mlp = [g_up, g_down]
        return _swap_halves(self.g_mlp)

    def swap_rest(self, dw_out, dw_pool):
        d, nw = self.dims, len(POOL_WINDOWS)
        PG = d["PG"]
        self.early_g4 = [dw_out.reshape(N_CHIPS, d["MIX"] // N_CHIPS, d["D"]),
                         dw_pool.reshape(nw, N_CHIPS, PG // N_CHIPS, PG).transpose(1, 0, 2, 3).reshape(N_CHIPS, PG, PG)] + self.g_mlp
        return _swap_halves(self.early_g4[:2])

    def reduce_early(self, recv):
        self.early_recv = list(recv)
        sent = [_pair_sum(g, r, f"grads_early_pair_sum{i}") for i, (g, r) in enumerate(zip(self.early_g4, self.early_recv))]
        return _exchange_chips(sent)

    def reduce_early_done(self, got):
        self.early_got = list(got)

    def late_swap(self, dw_in):
        self.late_g4 = [_stack_columns(dw_in, N_CHIPS)]
        return _swap_halves(self.late_g4)

    def reduce_late(self, recv):
        self.late_recv = list(recv)
        sent = _pair_sum(self.late_g4[0], self.late_recv[0], "grads_late_pair_sum")
        hr = sent.shape[1]
        cuts = [0, hr // 2, 3 * hr // 4, hr]
        self.late_parts = [sent[:, cuts[i]:cuts[i + 1]] for i in range(3)]
        self.late_got = [None] * 3
        return self.late_part(0)

    def late_part(self, i):
        return _exchange_chips([self.late_parts[i]])

    def late_part_done(self, i, got):
        self.late_got[i] = got[0]

    def _finish(self, g4s, recvs, pieces, tag):
        halves = [_chip_sum(g, r, p, f"grads_{tag}_chip_sum{i}") for i, (g, r, p) in enumerate(zip(g4s, recvs, pieces))]
        return _run_exchange(_join_halves(halves), f"grads_{tag}_join")

    def reduced_early(self):
        return self._finish(self.early_g4, self.early_recv, self.early_got, "early")

    def reduced_late(self):
        return self._finish(self.late_g4, self.late_recv, [jnp.concatenate(self.late_got, axis=1)], "late")[0]


def kernel(x, hyb_norm, hyb_w_in, ssd_conv_w, ssd_conv_b, ssd_dt_bias, ssd_a_log, ssd_d, ssd_out_norm, sb_q_norm, sb_k_norm, hyb_w_out, pool_norm, pool_w, pool_b, pool_scale, mlp_norm, mlp_w_up, mlp_w_down, loss_target, m_hyb_norm, m_hyb_w_in, m_ssd_conv_w, m_ssd_conv_b, m_ssd_dt_bias, m_ssd_a_log, m_ssd_d, m_ssd_out_norm, m_sb_q_norm, m_sb_k_norm, m_hyb_w_out, m_pool_norm, m_pool_w, m_pool_b, m_pool_scale, m_mlp_norm, m_mlp_w_up, m_mlp_w_down, v_hyb_norm, v_hyb_w_in, v_ssd_conv_w, v_ssd_conv_b, v_ssd_dt_bias, v_ssd_a_log, v_ssd_d, v_ssd_out_norm, v_sb_q_norm, v_sb_k_norm, v_hyb_w_out, v_pool_norm, v_pool_w, v_pool_b, v_pool_scale, v_mlp_norm, v_mlp_w_up, v_mlp_w_down):
    T, D = x.shape[1], x.shape[2]
    HS = ssd_dt_bias.shape[-1]
    SW = HS * SSD_HEAD_DIM
    CD = ssd_conv_b.shape[-1]
    IN = N_CHIPS * hyb_w_in.shape[-1]
    SBW = (IN - SW - CD - HS) // 3
    F = N_CHIPS * mlp_w_up.shape[-1]
    NL = mlp_norm.shape[0]
    PG = D // len(POOL_WINDOWS)
    xc, tgt = x[0], loss_target[0]
    ix, iy, ic = lax.axis_index("x"), lax.axis_index("y"), lax.axis_index("c")
    chip = (2 * ix + iy).astype(jnp.int32)

    small = jnp.concatenate([ssd_conv_w.reshape(-1), pool_norm.reshape(-1), pool_b.reshape(-1), pool_scale.reshape(-1)])
    ns = small.shape[0]
    ns8 = -(-ns // (8 * LANES)) * LANES
    gathered, _ = _gather_small(jnp.pad(small, (0, 8 * ns8 - ns)).reshape(8, ns8), "gather_small")
    per_chip = gathered.reshape(N_CHIPS, 2, 8 * ns8)[:, 0, :ns]
    cw = CD // N_CHIPS
    conv_w = per_chip[:, :4 * cw].reshape(N_CHIPS, 4, cw).transpose(1, 0, 2).reshape(4, CD)
    pvec = per_chip[:, 4 * cw:].reshape(N_CHIPS, 3, PG)
    pool_norm_f, pool_b_f, pool_scale_f = (pvec[:, i].reshape(1, D) for i in range(3))

    fs = F // N_CHIPS
    own_in = hyb_w_in[0].astype(bf16)
    g_in = _run_exchange(_gather_d2d(_run_exchange(_gather_ici([own_in]), "gather_in_ici")), "gather_in_d2d")[0]
    w_in = lax.dynamic_update_index_in_dim(g_in, own_in, chip, 0).transpose(1, 0, 2).reshape(D, IN)
    c1, c2, c3 = SW, SW + CD, SW + CD + HS
    w_z, w_xbc, w_dt, w_qkv = w_in[:, :c1], w_in[:, c1:c2], _pad_lanes(w_in[:, c2:c3]), w_in[:, c3:]
    dt_bias_p, a_log_p, d_skip_p = _pad_lanes(ssd_dt_bias), _pad_lanes(ssd_a_log), jnp.repeat(ssd_d, SSD_HEAD_DIM, axis=-1)

    assert NL == 2
    own_first = [hyb_w_out[0].astype(bf16), pool_w[0].reshape(-1, PG).astype(bf16),
                 mlp_w_up[0].astype(bf16), mlp_w_down[0].astype(bf16)]
    own_last = [mlp_w_up[1].astype(bf16), mlp_w_down[1].astype(bf16)]
    net = _Net(own_first, own_last, chip, dict(D=D, F=F, NL=NL, PG=PG, IN=IN, MIX=SW + SBW))
    first = dict(hyb_norm=hyb_norm, w_z=w_z, w_xbc=w_xbc, w_dt=w_dt, w_qkv=w_qkv, conv_w=conv_w, conv_b=ssd_conv_b,
                 dt_bias=dt_bias_p, a_log=a_log_p, d_skip=d_skip_p, out_norm=ssd_out_norm, q_norm=sb_q_norm, k_norm=sb_k_norm,
                 pool_norm=pool_norm_f, pool_b=pool_b_f, pool_scale=pool_scale_f, mlp_norm=mlp_norm)
    sq, grad_x, gr = _local_step(xc, tgt, first, HS, SBW, net)
    loss = lax.psum(sq[0, 0] * (0.5 / D), ("x", "y", "c"))
    dg_hyb, dconv_b, dbias, dalog, ddskip, dg_on, dg_q, dg_k = (gr[k] for k in (
        "hyb_norm", "conv_b", "dt_bias", "a_log", "d_skip", "out_norm", "q_norm", "k_norm"))
    (dg_mlp0, dg_mlp1), dconv_w, dg_pool, db_pool, dsc_pool = gr["mlp_norm"], gr["conv_w"], gr["pool_norm"], gr["pool_b"], gr["pool_scale"]
    gb_out, gb_pool, gb_up, gb_down = net.reduced_early()

    full_small = [dg_hyb, dconv_b, dbias[:, :HS], dalog[:, :HS], ddskip[:, :HS], dg_on, dg_q, dg_k,
                  jnp.concatenate([dg_mlp0, dg_mlp1], axis=0).reshape(1, -1),
                  dconv_w.reshape(1, -1), dg_pool, db_pool, dsc_pool]
    sizes = [v.shape[-1] for v in full_small]
    packed = jnp.concatenate([v.reshape(-1) for v in full_small])
    npk = packed.shape[0]
    npk8 = -(-npk // (8 * LANES)) * LANES
    _, summed = _gather_small(jnp.pad(packed, (0, 8 * npk8 - npk)).reshape(8, npk8), "grads_small")
    summed = summed.reshape(-1)[:npk]
    offs = [0]
    for s in sizes:
        offs.append(offs[-1] + s)
    (g_hyb_norm, g_conv_b, g_dt_bias, g_a_log, g_d, g_out_norm, g_q_norm, g_k_norm, g_mlp_norm, g_conv_w_full,
     g_pool_norm_full, g_pool_b_full, g_pool_scale_full) = (summed[offs[i]:offs[i + 1]] for i in range(len(sizes)))
    take = lambda full, n: lax.dynamic_slice_in_dim(full.reshape(-1, N_CHIPS, n), chip, 1, axis=1)
    small_grads = {
        "hyb_norm": g_hyb_norm.reshape(hyb_norm.shape), "ssd_conv_w": take(g_conv_w_full, cw).reshape(ssd_conv_w.shape),
        "ssd_conv_b": g_conv_b.reshape(ssd_conv_b.shape), "ssd_dt_bias": g_dt_bias.reshape(ssd_dt_bias.shape),
        "ssd_a_log": g_a_log.reshape(ssd_a_log.shape), "ssd_d": g_d.reshape(ssd_d.shape),
        "ssd_out_norm": g_out_norm.reshape(ssd_out_norm.shape), "sb_q_norm": g_q_norm.reshape(sb_q_norm.shape),
        "sb_k_norm": g_k_norm.reshape(sb_k_norm.shape), "pool_norm": take(g_pool_norm_full, PG).reshape(pool_norm.shape),
        "pool_b": take(g_pool_b_full, PG).reshape(pool_b.shape), "pool_scale": take(g_pool_scale_full, PG).reshape(pool_scale.shape),
        "mlp_norm": g_mlp_norm.reshape(mlp_norm.shape),
    }

    weights = dict(hyb_norm=hyb_norm, hyb_w_in=hyb_w_in, ssd_conv_w=ssd_conv_w, ssd_conv_b=ssd_conv_b, ssd_dt_bias=ssd_dt_bias,
                   ssd_a_log=ssd_a_log, ssd_d=ssd_d, ssd_out_norm=ssd_out_norm, sb_q_norm=sb_q_norm, sb_k_norm=sb_k_norm,
                   hyb_w_out=hyb_w_out, pool_norm=pool_norm, pool_w=pool_w, pool_b=pool_b, pool_scale=pool_scale,
                   mlp_norm=mlp_norm, mlp_w_up=mlp_w_up, mlp_w_down=mlp_w_down)
    moms = dict(hyb_norm=m_hyb_norm, hyb_w_in=m_hyb_w_in, ssd_conv_w=m_ssd_conv_w, ssd_conv_b=m_ssd_conv_b, ssd_dt_bias=m_ssd_dt_bias,
                ssd_a_log=m_ssd_a_log, ssd_d=m_ssd_d, ssd_out_norm=m_ssd_out_norm, sb_q_norm=m_sb_q_norm, sb_k_norm=m_sb_k_norm,
                hyb_w_out=m_hyb_w_out, pool_norm=m_pool_norm, pool_w=m_pool_w, pool_b=m_pool_b, pool_scale=m_pool_scale,
                mlp_norm=m_mlp_norm, mlp_w_up=m_mlp_w_up, mlp_w_down=m_mlp_w_down)
    vels = dict(hyb_norm=v_hyb_norm, hyb_w_in=v_hyb_w_in, ssd_conv_w=v_ssd_conv_w, ssd_conv_b=v_ssd_conv_b, ssd_dt_bias=v_ssd_dt_bias,
                ssd_a_log=v_ssd_a_log, ssd_d=v_ssd_d, ssd_out_norm=v_ssd_out_norm, sb_q_norm=v_sb_q_norm, sb_k_norm=v_sb_k_norm,
                hyb_w_out=v_hyb_w_out, pool_norm=v_pool_norm, pool_w=v_pool_w, pool_b=v_pool_b, pool_scale=v_pool_scale,
                mlp_norm=v_mlp_norm, mlp_w_up=v_mlp_w_up, mlp_w_down=v_mlp_w_down)
    order = list(weights)
    grads, delta, new_m, new_v = {}, {}, {}, {}
    for name, g2, part in (("hyb_w_out", gb_out, None), ("pool_w", gb_pool, None), ("mlp_w_up", gb_up, 1),
                           ("mlp_w_down", gb_down, 2), ("hyb_w_in", None, None)):
        if g2 is None:
            g2 = net.reduced_late()
        shp = weights[name].shape
        d_, m_, v_, g_, got = _adamw(weights[name].reshape(g2.shape), g2, moms[name].reshape(g2.shape), vels[name].reshape(g2.shape),
                                     f"adamw_{name}", comm=net.late_part(part) if part else None)
        if part:
            net.late_part_done(part, got)
        grads[name], delta[name], new_m[name], new_v[name] = (t.reshape(shp) for t in (g_, d_, m_, v_))
    snames = list(small_grads)
    pack = lambda d: jnp.concatenate([d[n].reshape(-1) for n in snames])
    nsm = sum(small_grads[n].size for n in snames)
    cols = -(-nsm // (8 * LANES)) * LANES
    as_blk = lambda v: jnp.pad(v, (0, 8 * cols - nsm)).reshape(8, cols)
    padded_v = jnp.pad(pack(vels), (0, 8 * cols - nsm), constant_values=1.0).reshape(8, cols)
    d_, m_, v_, _, _ = _adamw(as_blk(pack(weights)), as_blk(pack(small_grads)), as_blk(pack(moms)), padded_v, "adamw_small")
    off = 0
    for n in snames:
        sz, shp = small_grads[n].size, weights[n].shape
        grads[n] = small_grads[n]
        delta[n], new_m[n], new_v[n] = (t.reshape(-1)[off:off + sz].reshape(shp) for t in (d_, m_, v_))
        off += sz

    return (loss, grad_x.reshape(x.shape), *[grads[n] for n in order], *[delta[n] for n in order],
            *[new_m[n] for n in order], *[new_v[n] for n in order])
```

```python
import functools
import math

import jax
import jax.numpy as jnp
from jax import lax
from jax.experimental import pallas as pl
from jax.experimental.pallas import tpu as pltpu

f32 = jnp.float32
bf16 = jnp.bfloat16

EPS = 1e-6
SSD_HEAD_DIM = 64
SSD_STATE = 128
SSD_GROUPS = 4
SSD_CHUNK = 128
LANES = 128
SB_HEAD_DIM = 128
POOL_WINDOWS = (2, 4, 8, 16)
POOL_HALO = 16
CONV_HALO = 8
ADAM_LR, ADAM_B1, ADAM_B2, ADAM_EPS, ADAM_WD, ADAM_STEP = 0.001, 0.9, 0.999, 1e-08, 0.01, 10
VMEM_LIMIT = 56 * 1024 * 1024
MM_TILE_BUDGET = 40 * 1024 * 1024
N_CHIPS = 4
MESH = pl.DeviceIdType.MESH

_DIMS = {"nn": (((1,), (0,)), ((), ())), "nt": (((1,), (1,)), ((), ())), "tn": (((0,), (0,)), ((), ()))}


def _fit(n, t):
    if n <= t:
        return n
    return max(d for d in range(LANES, t + 1, LANES) if n % d == 0)


def _cp(*sem):
    return pltpu.CompilerParams(dimension_semantics=sem, vmem_limit_bytes=VMEM_LIMIT)


def _sigmoid(v):
    return 1.0 / (1.0 + jnp.exp(-v))


def _softplus(v):
    return jnp.maximum(v, 0.0) + jnp.log(1.0 + jnp.exp(-jnp.abs(v)))


def _split(v, parts):
    out, rem = [], v
    for _ in range(parts):
        p = rem.astype(bf16)
        out.append(p)
        rem = rem - p.astype(f32)
    return out


def _dot(a, b, mode="nn"):
    return lax.dot_general(a, b, _DIMS[mode], preferred_element_type=f32)


def _mask_dot(mask_b, v, parts):
    return _dot(jnp.concatenate([mask_b] * parts, axis=1), jnp.concatenate(_split(v, parts), axis=0))


def _dot_mask(v, mask_b, parts):
    return _dot(jnp.concatenate(_split(v, parts), axis=1), jnp.concatenate([mask_b] * parts, axis=0))


def _stacked(view, br, bc, rmap, cmap):
    kind, per, layer, rows_per_layer = view
    if kind == "cols":
        npc = per // bc
        return pl.BlockSpec((None, br, bc), lambda i, j, k: (cmap(i, j, k) // npc, layer * (rows_per_layer // br) + rmap(i, j, k),
                                                              cmap(i, j, k) % npc))
    npc = per // br
    return pl.BlockSpec((None, br, bc), lambda i, j, k: (rmap(i, j, k) // npc, layer * npc + rmap(i, j, k) % npc, cmap(i, j, k)))


def _pick_tiles(M, N, K, caps, a_bytes, b_bytes, io_bytes):
    def cands(n, cap, sizes):
        got = [s for s in sizes if s <= min(n, cap) and n % s == 0]
        return got or [_fit(n, min(n, cap))]

    best = None
    for tk in cands(K, caps[2], (8192, 4096, 2048, 1024, 512, 256, 128)):
        for tm in cands(M, caps[0], (1024, 512, 256, 128)):
            for tn in cands(N, caps[1], (1024, 512, 256, 128)):
                need = 2 * (tm * tk * a_bytes + tk * tn * b_bytes) + tm * tn * (2 * io_bytes + (4 if tk < K else 0))
                key = (need <= MM_TILE_BUDGET, tk, tm * tn, tm)
                if best is None or key > best[0]:
                    best = (key, (tm, tn, tk))
    return best[1]


def _mm(a, b, mode, M, N, K, outs, name, epilogue=None, extras=(), a_off=(0, 0), b_off=(0, 0),
        b_view=None, out_view=None, out_stack=None, alias=None, comm=None):
    caps = [M, N, K]
    if b_view is not None:
        caps[1 if (b_view[0] == "cols") == (mode != "nt") else 2] = b_view[1]
    if out_view is not None:
        d = 1 if out_view[0] == "cols" else 0
        caps[d] = min(caps[d], out_view[1])
    for off, dims in ((a_off, (2, 0) if mode == "tn" else (0, 2)), (b_off, (1, 2) if mode == "nt" else (2, 1))):
        for o, d in zip(off, dims):
            if o:
                caps[d] = min(caps[d], math.gcd(o, caps[d]))
    io_bytes = sum(jnp.dtype(dt).itemsize for dt in outs) + sum(e[0].dtype.itemsize for e in extras if e[1] == "tile")
    tm, tn, tk = _pick_tiles(M, N, K, caps, a.dtype.itemsize, b.dtype.itemsize, io_bytes)
    nk = K // tk
    if mode == "tn":
        a_blk, ad = (tk, tm), (tk, tm)
    else:
        a_blk, ad = (tm, tk), (tm, tk)
    b_blk = (tn, tk) if mode == "nt" else (tk, tn)
    assert a_off[0] % ad[0] == 0 and a_off[1] % ad[1] == 0 and b_off[0] % b_blk[0] == 0 and b_off[1] % b_blk[1] == 0
    ao = (a_off[0] // ad[0], a_off[1] // ad[1])
    bo = (b_off[0] // b_blk[0], b_off[1] // b_blk[1])
    if mode == "tn":
        a_map = lambda i, j, k: (k + ao[0], i + ao[1])
    else:
        a_map = lambda i, j, k: (i + ao[0], k + ao[1])
    if mode == "nt":
        b_map = lambda i, j, k: (j + bo[0], k + bo[1])
    else:
        b_map = lambda i, j, k: (k + bo[0], j + bo[1])
    if b_view is not None:
        if mode == "nt":
            b_spec = _stacked(b_view, tn, tk, lambda i, j, k: j, lambda i, j, k: k)
        else:
            b_spec = _stacked(b_view, tk, tn, lambda i, j, k: k, lambda i, j, k: j)
    else:
        b_spec = pl.BlockSpec(b_blk, b_map)
    in_specs = [pl.BlockSpec(a_blk, a_map), b_spec]
    for arr, kind in extras:
        if kind == "tile":
            in_specs.append(pl.BlockSpec((tm, tn), lambda i, j, k: (i, j)))
        else:
            in_specs.append(pl.BlockSpec((1, tn), lambda i, j, k: (0, j)))
    ne, no = len(extras), len(outs)
    if epilogue is None:
        epilogue = lambda acc: (acc,)
    operands = [a, b, *[e[0] for e in extras]]
    aliases = {}
    if alias is not None:
        in_specs.append(ANY)
        aliases[len(operands)] = 0
        operands.append(alias)
    n_in = len(operands)
    if out_view is not None:
        out_specs = [_stacked(out_view, tm, tn, lambda i, j, k: i, lambda i, j, k: j)]
        out_shape = [jax.ShapeDtypeStruct(out_stack, outs[0])]
    else:
        out_specs = [pl.BlockSpec((tm, tn), lambda i, j, k: (i, j)) for _ in outs]
        out_shape = [jax.ShapeDtypeStruct((M, N), dt) for dt in outs]
    scratch = [pltpu.VMEM((tm, tn), f32)] if nk > 1 else []
    grid = (M // tm, N // tn, nk)
    if comm is not None:
        in_specs += [ANY] * len(comm["ins"])
        operands += comm["ins"]
        out_specs += [ANY] * len(comm["outs"])
        out_shape += comm["outs"]
        scratch += comm["sems"]
    nci, nco, ncs = (len(comm["ins"]), len(comm["outs"]), len(comm["sems"])) if comm is not None else (0, 0, 0)

    def body(*refs):
        a_ref, b_ref = refs[0], refs[1]
        ex, out_refs = refs[2:2 + ne], refs[n_in + nci:n_in + nci + no]
        rest = refs[n_in + nci + no + nco:]
        if comm is not None:
            cargs = (refs[n_in:n_in + nci], refs[n_in + nci + no:n_in + nci + no + nco], refs[len(refs) - ncs:])
            pid = [pl.program_id(d) for d in range(3)]

            @pl.when((pid[0] == 0) & (pid[1] == 0) & (pid[2] == 0))
            def _():
                comm["start"](*cargs)

        def finish(acc):
            res = epilogue(acc, *[e[...] for e in ex])
            for o, r in zip(out_refs, res):
                o[...] = r.astype(o.dtype)

        prod = lax.dot_general(a_ref[...].astype(bf16), b_ref[...].astype(bf16), _DIMS[mode],
                               preferred_element_type=f32)
        if nk == 1:
            finish(prod)
        else:
            acc_ref = rest[0]
            k = pl.program_id(2)

            @pl.when(k == 0)
            def _():
                acc_ref[...] = prod

            @pl.when(k > 0)
            def _():
                acc_ref[...] += prod

            @pl.when(k == nk - 1)
            def _():
                finish(acc_ref[...])

        if comm is not None:
            @pl.when((pid[0] == grid[0] - 1) & (pid[1] == grid[1] - 1) & (pid[2] == grid[2] - 1))
            def _():
                comm["finish"](*cargs)

    sem = ("arbitrary",) * 3 if comm is not None else ("parallel", "parallel", "arbitrary")
    res = pl.pallas_call(
        body, name=name, grid=grid, in_specs=in_specs, out_specs=out_specs, out_shape=out_shape,
        scratch_shapes=scratch, input_output_aliases=aliases, compiler_params=_cp(*sem),
    )(*operands)
    if comm is not None:
        return res[:no], res[no:]
    return res[0] if no == 1 else res


def _rms_fwd(x, g, out_dtype, name, tr=256):
    T, D = x.shape
    tr = min(tr, T)

    def body(x_ref, g_ref, o_ref):
        xv = x_ref[...]
        r = lax.rsqrt(jnp.mean(xv * xv, axis=-1, keepdims=True) + EPS)
        o_ref[...] = (xv * r * g_ref[...]).astype(o_ref.dtype)

    return pl.pallas_call(
        body, name=name, grid=(T // tr,),
        in_specs=[pl.BlockSpec((tr, D), lambda r: (r, 0)), pl.BlockSpec((1, D), lambda r: (0, 0))],
        out_specs=pl.BlockSpec((tr, D), lambda r: (r, 0)),
        out_shape=jax.ShapeDtypeStruct((T, D), out_dtype), compiler_params=_cp("parallel"),
    )(x, g)


def _rms_bwd(x, g, dh, dres, name, tr=256, comm=None):
    T, D = x.shape
    tr = min(tr, T)
    grid = (T // tr,)

    def body(*refs):
        (x_ref, g_ref, dh_ref, dres_ref), (dx_ref, dg_ref), _, cargs = _hosted(comm, 4, 2, refs)
        if comm is not None:
            first, last = _first_last(grid)

            @pl.when(first)
            def _():
                comm["start"](*cargs)

            @pl.when(last)
            def _():
                comm["finish"](*cargs)

        xv = x_ref[...]
        r = lax.rsqrt(jnp.mean(xv * xv, axis=-1, keepdims=True) + EPS)
        xh = xv * r
        dhv = dh_ref[...]
        dhg = dhv * g_ref[...]
        dx_ref[...] = dres_ref[...] + r * (dhg - xh * jnp.mean(dhg * xh, axis=-1, keepdims=True))

        @pl.when(pl.program_id(0) == 0)
        def _():
            dg_ref[...] = jnp.zeros_like(dg_ref)

        dg_ref[...] += jnp.sum(dhv * xh, axis=0, keepdims=True)

    row = pl.BlockSpec((tr, D), lambda r: (r, 0))
    vec = pl.BlockSpec((1, D), lambda r: (0, 0))
    (dx, dg), got = _host_call(body, name, grid, [row, vec, row, row], [row, vec],
                               [jax.ShapeDtypeStruct((T, D), f32), jax.ShapeDtypeStruct((1, D), f32)], [], [x, g, dh, dres], comm)
    return (dx, dg) if comm is None else (dx, dg, got)


def _loss_grad(y, tgt, name, tr=256):
    T, D = y.shape
    tr = min(tr, T)

    def body(y_ref, t_ref, dy_ref, s_ref):
        e = y_ref[...] - t_ref[...]
        dy_ref[...] = e * (1.0 / D)

        @pl.when(pl.program_id(0) == 0)
        def _():
            s_ref[...] = jnp.zeros_like(s_ref)

        s_ref[...] += jnp.sum(e * e)

    row = pl.BlockSpec((tr, D), lambda r: (r, 0))
    return pl.pallas_call(
        body, name=name, grid=(T // tr,), in_specs=[row, row],
        out_specs=[row, pl.BlockSpec((8, LANES), lambda r: (0, 0))],
        out_shape=[jax.ShapeDtypeStruct((T, D), f32), jax.ShapeDtypeStruct((8, LANES), f32)],
        compiler_params=_cp("arbitrary"),
    )(y, tgt)


def _shift_down(cur, prev, s):
    rolled = pltpu.roll(cur, s, 0)
    top = pltpu.roll(prev, s, 0)
    row = lax.broadcasted_iota(jnp.int32, top.shape, 0)
    head = jnp.where(row < s, top, rolled[0:CONV_HALO])
    return jnp.concatenate([head, rolled[CONV_HALO:]], axis=0)


def _shift_up(cur, nxt, s):
    n = cur.shape[0]
    rolled = pltpu.roll(cur, n - s, 0)
    bot = pltpu.roll(nxt, CONV_HALO - s, 0)
    row = lax.broadcasted_iota(jnp.int32, bot.shape, 0)
    tail = jnp.where(row >= CONV_HALO - s, bot, rolled[n - CONV_HALO:])
    return jnp.concatenate([rolled[:n - CONV_HALO], tail], axis=0)


def _conv_pre(cur, prev, w_ref, b_ref):
    taps = [cur] + [_shift_down(cur, prev, s) for s in (1, 2, 3)]
    pre = b_ref[...] + w_ref[3:4, :] * taps[0]
    for s in (1, 2, 3):
        pre = pre + w_ref[3 - s:4 - s, :] * taps[s]
    return pre, taps


def _conv_specs(T, C, rc, cb):
    cur = pl.BlockSpec((rc, cb), lambda j, r: (r, j))
    prev = pl.BlockSpec((CONV_HALO, cb), lambda j, r: (jnp.maximum(r * (rc // CONV_HALO) - 1, 0), j))
    nxt = pl.BlockSpec((CONV_HALO, cb), lambda j, r: (jnp.minimum((r + 1) * (rc // CONV_HALO), T // CONV_HALO - 1), j))
    w = pl.BlockSpec((4, cb), lambda j, r: (0, j))
    b = pl.BlockSpec((1, cb), lambda j, r: (0, j))
    return cur, prev, nxt, w, b


def _conv_fwd(xraw, w, b, name):
    T, C = xraw.shape
    rc, cb = min(512, T), min(512, C)
    cur, prev, _, ws, bs = _conv_specs(T, C, rc, cb)

    def body(x_ref, p_ref, w_ref, b_ref, o_ref):
        pv = jnp.where(pl.program_id(1) > 0, p_ref[...], 0.0)
        pre, _ = _conv_pre(x_ref[...], pv, w_ref, b_ref)
        o_ref[...] = pre * _sigmoid(pre)

    return pl.pallas_call(
        body, name=name, grid=(C // cb, T // rc), in_specs=[cur, prev, ws, bs], out_specs=cur,
        out_shape=jax.ShapeDtypeStruct((T, C), f32), compiler_params=_cp("parallel", "parallel"),
    )(xraw, xraw, w, b)


def _conv_bwd_pre(xraw, dxbc, w, b, name):
    T, C = xraw.shape
    rc, cb = min(512, T), min(512, C)
    cur, prev, _, ws, bs = _conv_specs(T, C, rc, cb)

    def body(x_ref, p_ref, d_ref, w_ref, b_ref, dpre_ref, dw_ref, db_ref):
        pv = jnp.where(pl.program_id(1) > 0, p_ref[...], 0.0)
        pre, taps = _conv_pre(x_ref[...], pv, w_ref, b_ref)
        sg = _sigmoid(pre)
        dpre = d_ref[...] * (sg * (1.0 + pre * (1.0 - sg)))
        dpre_ref[...] = dpre

        @pl.when(pl.program_id(1) == 0)
        def _():
            dw_ref[...] = jnp.zeros_like(dw_ref)
            db_ref[...] = jnp.zeros_like(db_ref)

        row = lax.broadcasted_iota(jnp.int32, dw_ref.shape, 0)
        upd = jnp.zeros(dw_ref.shape, f32)
        for s in range(4):
            upd = upd + jnp.where(row == 3 - s, jnp.sum(dpre * taps[s], axis=0, keepdims=True), 0.0)
        dw_ref[...] += upd
        db_ref[...] += jnp.sum(dpre, axis=0, keepdims=True)

    return pl.pallas_call(
        body, name=name, grid=(C // cb, T // rc), in_specs=[cur, prev, cur, ws, bs], out_specs=[cur, ws, bs],
        out_shape=[jax.ShapeDtypeStruct((T, C), f32), jax.ShapeDtypeStruct((4, C), f32), jax.ShapeDtypeStruct((1, C), f32)],
        compiler_params=_cp("parallel", "arbitrary"),
    )(xraw, xraw, dxbc, w, b)


def _conv_bwd_in(dpre, w, name):
    T, C = dpre.shape
    rc, cb = min(512, T), min(512, C)
    cur, _, nxt, ws, _ = _conv_specs(T, C, rc, cb)
    nr = T // rc

    def body(d_ref, n_ref, w_ref, o_ref):
        nv = jnp.where(pl.program_id(1) < nr - 1, n_ref[...], 0.0)
        cv = d_ref[...]
        out = w_ref[3:4, :] * cv
        for s in (1, 2, 3):
            out = out + w_ref[3 - s:4 - s, :] * _shift_up(cv, nv, s)
        o_ref[...] = out.astype(o_ref.dtype)

    return pl.pallas_call(
        body, name=name, grid=(C // cb, nr), in_specs=[cur, nxt, ws], out_specs=cur,
        out_shape=jax.ShapeDtypeStruct((T, C), bf16), compiler_params=_cp("parallel", "parallel"),
    )(dpre, dpre, w)


HEAD_SHIFT = SSD_HEAD_DIM.bit_length() - 1


def _ssd_prep(dtr_ref, bias_ref, alog_ref, SW):
    L = SSD_CHUNK
    xs = dtr_ref[...] + bias_ref[...]
    dt = _softplus(xs)
    a = -jnp.exp(alog_ref[...])
    causal = lax.broadcasted_iota(jnp.int32, (L, L), 0) >= lax.broadcasted_iota(jnp.int32, (L, L), 1)
    cs = _mask_dot(causal.astype(bf16), dt * a, 3)
    spread = (lax.broadcasted_iota(jnp.int32, (LANES, SW), 0)
              == lax.shift_right_logical(lax.broadcasted_iota(jnp.int32, (LANES, SW), 1), HEAD_SHIFT)).astype(bf16)
    dt_x = _dot_mask(dt, spread, 3)
    cs_x = _dot_mask(cs, spread, 3)
    last_x = cs_x[L - 1:L, :]
    return xs, dt, a, causal, cs, cs.T, dt_x, jnp.exp(cs_x), jnp.exp(last_x - cs_x), jnp.exp(last_x)


def _head_decay(cs, csT, causal, h):
    seg = cs[:, h:h + 1] - csT[h:h + 1, :]
    return jnp.where(causal, jnp.exp(jnp.minimum(seg, 0.0)), 0.0)


def _ssd_fwd(xbc, dtraw, z, dt_bias, a_log, d_skip_x, out_norm, HS, name, comm=None):
    T = xbc.shape[0]
    L, P, NS, G = SSD_CHUNK, SSD_HEAD_DIM, SSD_STATE, SSD_GROUPS
    SW, HPG, nc = HS * P, HS // SSD_GROUPS, T // SSD_CHUNK
    gsz = SW // G
    gw = HPG * P
    assert HPG % 2 == 0 and 2 * P == LANES

    def body(*refs):
        (xbc_ref, dtr_ref, z_ref, bias_ref, alog_ref, dsk_ref, on_ref), (y_ref, yn_ref, sp_ref), (st_ref,), cargs = _hosted(comm, 7, 3, refs)
        first, last = _first_last((nc,))
        if comm is not None:
            @pl.when(first)
            def _():
                comm["start"](*cargs)

            @pl.when(last)
            def _():
                comm["finish"](*cargs)

        @pl.when(pl.program_id(0) == 0)
        def _():
            st_ref[...] = jnp.zeros_like(st_ref)

        sp_ref[0] = st_ref[...]
        _, _, _, causal, cs, csT, dt_x, ecs_x, dte_x, cdec_x = _ssd_prep(dtr_ref, bias_ref, alog_ref, SW)
        X = xbc_ref[:, 0:SW]
        Xd = X * dt_x
        Xdb = Xd.astype(bf16)
        XEb = (Xd * dte_x).astype(bf16)
        left = lax.broadcasted_iota(jnp.int32, (L, LANES), 1) < P
        for g in range(G):
            gs = slice(g * gw, (g + 1) * gw)
            Bb = xbc_ref[:, SW + g * NS:SW + (g + 1) * NS].astype(bf16)
            Cb = xbc_ref[:, SW + (G + g) * NS:SW + (G + g + 1) * NS].astype(bf16)
            Gm = _dot(Cb, Bb, "nt")
            Sp = st_ref[:, gs]
            yo = _dot(Cb, Sp.astype(bf16)) * ecs_x[:, gs]
            st_ref[:, gs] = cdec_x[:, gs] * Sp + _dot(Bb, XEb[:, gs], "tn")
            for pr in range(HPG // 2):
                h0 = g * HPG + 2 * pr
                ps = slice(h0 * P, (h0 + 2) * P)
                xp = Xdb[:, ps]
                yd = jnp.where(left, _dot((_head_decay(cs, csT, causal, h0) * Gm).astype(bf16), xp),
                               _dot((_head_decay(cs, csT, causal, h0 + 1) * Gm).astype(bf16), xp))
                y_ref[:, ps] = yd + yo[:, pr * LANES:(pr + 1) * LANES] + dsk_ref[:, ps] * X[:, ps]
        zz = z_ref[...]
        gated = y_ref[...] * (zz * _sigmoid(zz))
        for g in range(G):
            gs = slice(g * gsz, (g + 1) * gsz)
            sg = gated[:, gs]
            rr = lax.rsqrt(jnp.mean(sg * sg, axis=-1, keepdims=True) + EPS)
            yn_ref[:, gs] = (sg * rr * on_ref[:, gs]).astype(yn_ref.dtype)

    vec = pl.BlockSpec((1, LANES), lambda c: (0, 0))
    wide = pl.BlockSpec((1, SW), lambda c: (0, 0))
    (y, yn, sp), got = _host_call(
        body, name, (nc,),
        [pl.BlockSpec((L, xbc.shape[1]), lambda c: (c, 0)), pl.BlockSpec((L, LANES), lambda c: (c, 0)),
         pl.BlockSpec((L, SW), lambda c: (c, 0)), vec, vec, wide, wide],
        [pl.BlockSpec((L, SW), lambda c: (c, 0)), pl.BlockSpec((L, SW), lambda c: (c, 0)),
         pl.BlockSpec((1, NS, SW), lambda c: (c, 0, 0))],
        [jax.ShapeDtypeStruct((T, SW), f32), jax.ShapeDtypeStruct((T, SW), bf16), jax.ShapeDtypeStruct((nc, NS, SW), f32)],
        [pltpu.VMEM((NS, SW), f32)], [xbc, dtraw, z, dt_bias, a_log, d_skip_x, out_norm], comm)
    return y, yn, sp, got


def _ssd_bwd(xbc, dtraw, sprev, dy, dt_bias, a_log, d_skip_x, HS, name, comm=None):
    T = xbc.shape[0]
    L, P, NS, G = SSD_CHUNK, SSD_HEAD_DIM, SSD_STATE, SSD_GROUPS
    SW, HPG, nc = HS * P, HS // SSD_GROUPS, T // SSD_CHUNK
    gw = HPG * P

    def body(*refs):
        ((xbc_ref, dtr_ref, sp_ref, dy_ref, bias_ref, alog_ref, dsk_ref),
         (dxbc_ref, ddtr_ref, dalog_ref, dbias_ref, dd_ref), (ds_ref,), cargs) = _hosted(comm, 7, 5, refs)
        if comm is not None:
            first, last = _first_last((nc,))

            @pl.when(first)
            def _():
                comm["start"](*cargs)

            @pl.when(last)
            def _():
                comm["finish"](*cargs)

        @pl.when(pl.program_id(0) == 0)
        def _():
            ds_ref[...] = jnp.zeros_like(ds_ref)
            dalog_ref[...] = jnp.zeros_like(dalog_ref)
            dbias_ref[...] = jnp.zeros_like(dbias_ref)
            dd_ref[...] = jnp.zeros_like(dd_ref)

        xs, dt, a, causal, cs, csT, dt_x, ecs_x, dte_x, cdec_x = _ssd_prep(dtr_ref, bias_ref, alog_ref, SW)
        lane = lax.broadcasted_iota(jnp.int32, (L, LANES), 1)
        sub = lax.broadcasted_iota(jnp.int32, (LANES, L), 0)
        left = lane < P
        dcs = jnp.zeros((L, LANES), f32)
        dcs_t = jnp.zeros((LANES, L), f32)
        xds = jnp.zeros((L, LANES), f32)
        dlast = jnp.zeros((1, LANES), f32)
        dD = jnp.zeros((1, LANES), f32)
        for g in range(G):
            gs = slice(g * gw, (g + 1) * gw)
            bsl = slice(SW + g * NS, SW + (g + 1) * NS)
            csl = slice(SW + (G + g) * NS, SW + (G + g + 1) * NS)
            Bb = xbc_ref[:, bsl].astype(bf16)
            Cb = xbc_ref[:, csl].astype(bf16)
            Gm = _dot(Cb, Bb, "nt")
            X = xbc_ref[:, gs]
            Xd = X * dt_x[:, gs]
            Xdb = Xd.astype(bf16)
            XE = Xd * dte_x[:, gs]
            dY = dy_ref[:, gs]
            dYb = dY.astype(bf16)
            Wb = (dY * ecs_x[:, gs]).astype(bf16)
            Sp = sp_ref[0, :, gs]
            Spb = Sp.astype(bf16)
            dS = ds_ref[:, gs]
            dSb = dS.astype(bf16)
            CS = _dot(Cb, Spb)
            Zb = _dot(Bb, dSb)
            dC = _dot(Wb, Spb, "nt")
            dB = _dot(XE.astype(bf16), dSb, "nt")
            ds_ref[:, gs] = cdec_x[:, gs] * dS + _dot(Cb, Wb, "tn")
            R1 = dY * CS * ecs_x[:, gs]
            R2 = XE * Zb
            to_head = (lax.shift_right_logical(lax.broadcasted_iota(jnp.int32, (gw, LANES), 0), HEAD_SHIFT) + g * HPG
                       == lax.broadcasted_iota(jnp.int32, (gw, LANES), 1)).astype(bf16)
            dcs = dcs + _dot_mask(R1 - R2, to_head, 3)
            dlast = (dlast + jnp.sum(_dot_mask(R2, to_head, 3), axis=0, keepdims=True)
                     + jnp.sum(_dot_mask(Sp * dS * cdec_x[:, gs], to_head, 3), axis=0, keepdims=True))
            dG = jnp.zeros((L, L), f32)
            pieces = []
            for pr in range(HPG // 2):
                h0 = g * HPG + 2 * pr
                pw = slice(pr * LANES, (pr + 1) * LANES)
                xp, dyp = Xdb[:, pw], dYb[:, pw]
                halves = []
                for k, h in enumerate((h0, h0 + 1)):
                    Lm = _head_decay(cs, csT, causal, h)
                    Mf = Lm * Gm
                    keep = left if k == 0 else jnp.logical_not(left)
                    dM = _dot(jnp.where(keep, dyp, jnp.zeros_like(dyp)), xp, "nt")
                    Q = dM * Mf
                    dcs = dcs + jnp.where(lane == h, jnp.sum(Q, axis=1, keepdims=True), 0.0)
                    dcs_t = dcs_t - jnp.where(sub == h, jnp.sum(Q, axis=0, keepdims=True), 0.0)
                    dG = dG + dM * Lm
                    halves.append(_dot(Mf.astype(bf16), dyp, "tn"))
                pieces.append(jnp.where(left, halves[0], halves[1]))
            dXd = jnp.concatenate(pieces, axis=1) + dte_x[:, gs] * Zb
            dxbc_ref[:, gs] = dXd * dt_x[:, gs] + dsk_ref[:, gs] * dY
            xds = xds + _dot_mask(dXd * X, to_head, 3)
            dD = dD + jnp.sum(_dot_mask(dY * X, to_head, 3), axis=0, keepdims=True)
            dGb = dG.astype(bf16)
            dxbc_ref[:, bsl] = dB + _dot(dGb, Cb, "tn")
            dxbc_ref[:, csl] = dC + _dot(dGb, Bb)
        rowi = lax.broadcasted_iota(jnp.int32, (L, LANES), 0)
        dcs = dcs + dcs_t.T + jnp.where(rowi == L - 1, dlast, 0.0)
        anti = (lax.broadcasted_iota(jnp.int32, (L, L), 1) >= lax.broadcasted_iota(jnp.int32, (L, L), 0)).astype(bf16)
        dda = _mask_dot(anti, dcs, 3)
        ddt = dda * a + xds
        dalog_ref[...] += jnp.sum(dda * dt, axis=0, keepdims=True) * a
        ddtr = ddt * _sigmoid(xs)
        ddtr_ref[...] = ddtr
        dbias_ref[...] += jnp.sum(ddtr, axis=0, keepdims=True)
        dd_ref[...] += dD

    rev = lambda c: (nc - 1 - c, 0)
    vec = pl.BlockSpec((1, LANES), lambda c: (0, 0))
    outs, got = _host_call(
        body, name, (nc,),
        [pl.BlockSpec((L, xbc.shape[1]), rev), pl.BlockSpec((L, LANES), rev),
         pl.BlockSpec((1, NS, SW), lambda c: (nc - 1 - c, 0, 0)), pl.BlockSpec((L, SW), rev), vec, vec,
         pl.BlockSpec((1, SW), lambda c: (0, 0))],
        [pl.BlockSpec((L, xbc.shape[1]), rev), pl.BlockSpec((L, LANES), rev), vec, vec, vec],
        [jax.ShapeDtypeStruct(xbc.shape, f32), jax.ShapeDtypeStruct((T, LANES), f32)] + [jax.ShapeDtypeStruct((1, LANES), f32)] * 3,
        [pltpu.VMEM((NS, SW), f32)], [xbc, dtraw, sprev, dy, dt_bias, a_log, d_skip_x], comm)
    return (*outs, got)


def _gate_bwd(y, z, dyn, out_norm, name, tr=256):
    T, SW = y.shape
    tr = min(tr, T)
    gsz = SW // SSD_GROUPS

    def body(y_ref, z_ref, d_ref, on_ref, dy_ref, dz_ref, don_ref):
        @pl.when(pl.program_id(0) == 0)
        def _():
            don_ref[...] = jnp.zeros_like(don_ref)

        for g in range(SSD_GROUPS):
            gs = slice(g * gsz, (g + 1) * gsz)
            yv, zv, dv = y_ref[:, gs], z_ref[:, gs], d_ref[:, gs]
            sg = _sigmoid(zv)
            sl = zv * sg
            gated = yv * sl
            rr = lax.rsqrt(jnp.mean(gated * gated, axis=-1, keepdims=True) + EPS)
            gh = gated * rr
            dgn = dv * on_ref[:, gs]
            dgated = rr * (dgn - gh * jnp.mean(dgn * gh, axis=-1, keepdims=True))
            dy_ref[:, gs] = dgated * sl
            dz_ref[:, gs] = (dgated * yv * (sg * (1.0 + zv * (1.0 - sg)))).astype(dz_ref.dtype)
            don_ref[:, gs] += jnp.sum(dv * gh, axis=0, keepdims=True)

    row = pl.BlockSpec((tr, SW), lambda r: (r, 0))
    vec = pl.BlockSpec((1, SW), lambda r: (0, 0))
    return pl.pallas_call(
        body, name=name, grid=(T // tr,), in_specs=[row, row, row, vec], out_specs=[row, row, vec],
        out_shape=[jax.ShapeDtypeStruct((T, SW), f32), jax.ShapeDtypeStruct((T, SW), bf16),
                   jax.ShapeDtypeStruct((1, SW), f32)],
        compiler_params=_cp("arbitrary"),
    )(y, z, dyn, out_norm)


def _qk_norm_fwd(qkv, qn_w, kn_w, SBW, name, tr=256):
    T = qkv.shape[0]
    tr = min(tr, T)
    nh = SBW // SB_HEAD_DIM

    def body(q_ref, k_ref, v_ref, qw_ref, kw_ref, qo_ref, ko_ref, vo_ref):
        for src, w_ref, dst in ((q_ref, qw_ref, qo_ref), (k_ref, kw_ref, ko_ref)):
            for h in range(nh):
                hs = slice(h * SB_HEAD_DIM, (h + 1) * SB_HEAD_DIM)
                sv = src[:, hs]
                rr = lax.rsqrt(jnp.mean(sv * sv, axis=-1, keepdims=True) + EPS)
                dst[:, hs] = (sv * rr * w_ref[...]).astype(dst.dtype)
        vo_ref[...] = v_ref[...].astype(vo_ref.dtype)

    blk = lambda j: pl.BlockSpec((tr, SBW), lambda r: (r, j))
    vec = pl.BlockSpec((1, SB_HEAD_DIM), lambda r: (0, 0))
    out = pl.BlockSpec((tr, SBW), lambda r: (r, 0))
    return pl.pallas_call(
        body, name=name, grid=(T // tr,), in_specs=[blk(0), blk(1), blk(2), vec, vec], out_specs=[out, out, out],
        out_shape=[jax.ShapeDtypeStruct((T, SBW), bf16)] * 3, compiler_params=_cp("parallel"),
    )(qkv, qkv, qkv, qn_w, kn_w)


def _qk_norm_bwd(qkv, dqn, dkn, dv, qn_w, kn_w, SBW, name, tr=256):
    T = qkv.shape[0]
    tr = min(tr, T)
    nh = SBW // SB_HEAD_DIM

    def body(q_ref, k_ref, dq_ref, dk_ref, dv_ref, qw_ref, kw_ref, o_ref, dqw_ref, dkw_ref):
        @pl.when(pl.program_id(0) == 0)
        def _():
            dqw_ref[...] = jnp.zeros_like(dqw_ref)
            dkw_ref[...] = jnp.zeros_like(dkw_ref)

        for part, (src, d_ref, w_ref, dw_ref) in enumerate(((q_ref, dq_ref, qw_ref, dqw_ref), (k_ref, dk_ref, kw_ref, dkw_ref))):
            dw = jnp.zeros((1, SB_HEAD_DIM), f32)
            for h in range(nh):
                hs = slice(h * SB_HEAD_DIM, (h + 1) * SB_HEAD_DIM)
                os_ = slice(part * SBW + h * SB_HEAD_DIM, part * SBW + (h + 1) * SB_HEAD_DIM)
                sv, dn = src[:, hs], d_ref[:, hs]
                rr = lax.rsqrt(jnp.mean(sv * sv, axis=-1, keepdims=True) + EPS)
                xh = sv * rr
                dg = dn * w_ref[...]
                o_ref[:, os_] = (rr * (dg - xh * jnp.mean(dg * xh, axis=-1, keepdims=True))).astype(o_ref.dtype)
                dw = dw + jnp.sum(dn * xh, axis=0, keepdims=True)
            dw_ref[...] += dw
        o_ref[:, 2 * SBW:] = dv_ref[...].astype(o_ref.dtype)

    blk = lambda j: pl.BlockSpec((tr, SBW), lambda r: (r, j))
    vec = pl.BlockSpec((1, SB_HEAD_DIM), lambda r: (0, 0))
    row = pl.BlockSpec((tr, SBW), lambda r: (r, 0))
    return pl.pallas_call(
        body, name=name, grid=(T // tr,), in_specs=[blk(0), blk(1), row, row, row, vec, vec],
        out_specs=[pl.BlockSpec((tr, 3 * SBW), lambda r: (r, 0)), vec, vec],
        out_shape=[jax.ShapeDtypeStruct((T, 3 * SBW), bf16)] + [jax.ShapeDtypeStruct((1, SB_HEAD_DIM), f32)] * 2,
        compiler_params=_cp("arbitrary"),
    )(qkv, qkv, dqn, dkn, dv, qn_w, kn_w)


def _sb_logits(q, kb, scale):
    zl = _dot(q, kb, "nt") * scale
    lb = jnp.minimum(zl, 0.0) - jnp.log(1.0 + jnp.exp(-jnp.abs(zl)))
    return zl, lb, lb - zl


def _tail_update(old, r0, new_tail):
    return new_tail if r0 == 0 else jnp.concatenate([old[:r0], new_tail], axis=0)


def _hosted(comm, n_in, n_out, refs):
    nci, nco, ncs = (len(comm["ins"]), len(comm["outs"]), len(comm["sems"])) if comm is not None else (0, 0, 0)
    ins, outs = refs[:n_in], refs[n_in + nci:n_in + nci + n_out]
    scratch = refs[n_in + nci + n_out + nco:len(refs) - ncs]
    cargs = (refs[n_in:n_in + nci], refs[n_in + nci + n_out:n_in + nci + n_out + nco], refs[len(refs) - ncs:])
    return ins, outs, scratch, cargs


def _host_call(body, name, grid, in_specs, out_specs, out_shape, scratch, operands, comm):
    n_out = len(out_shape)
    in_specs, out_specs, out_shape, scratch, operands = list(in_specs), list(out_specs), list(out_shape), list(scratch), list(operands)
    io = {}
    if comm is not None:
        for src, dst in comm.get("aliases", {}).items():
            io[len(operands) + src] = n_out + dst
        in_specs += [ANY] * len(comm["ins"])
        operands += comm["ins"]
        out_specs += [ANY] * len(comm["outs"])
        out_shape += comm["outs"]
        scratch += comm["sems"]
    res = pl.pallas_call(body, name=name, grid=grid, in_specs=in_specs, out_specs=out_specs, out_shape=out_shape,
                         scratch_shapes=scratch, input_output_aliases=io,
                         compiler_params=_cp(*(("arbitrary",) * len(grid))))(*operands)
    return res[:n_out], res[n_out:]


def _first_last(grid):
    pid = [pl.program_id(d) for d in range(len(grid))]
    first, last = pid[0] == 0, pid[0] == grid[0] - 1
    for d in range(1, len(grid)):
        first, last = first & (pid[d] == 0), last & (pid[d] == grid[d] - 1)
    return first, last


def _sb_fwd(qn, kn, vb, name, comm=None, tq=2048, tk=256):
    T, W = qn.shape
    tq = min(tq, T)
    tk = min(tk, tq)
    nh, nq, dh, nd = W // SB_HEAD_DIM, T // tq, SB_HEAD_DIM, tq // tk
    scale = dh ** -0.5
    grid = (nh, nq)

    def body(*refs):
        (q_ref, k_ref, v_ref), (o_ref, c_ref), _, cargs = _hosted(comm, 3, 2, refs)
        first, last = _first_last(grid)
        if comm is not None:
            @pl.when(first)
            def _():
                comm["start"](*cargs)

        qi = pl.program_id(1)
        q = q_ref[...]
        later = (lax.broadcasted_iota(jnp.int32, (tk, tk), 0) > lax.broadcasted_iota(jnp.int32, (tk, tk), 1)).astype(bf16)

        def step(j, carry, d):
            acc, run = carry
            r0 = 0 if d is None else d * tk
            ks = pl.multiple_of(j * tk, tk)
            kb, vv = k_ref[pl.ds(ks, tk), :], v_ref[pl.ds(ks, tk), :]
            _, lb, lk = _sb_logits(q[r0:], kb, scale)
            if d is not None:
                mask = lax.broadcasted_iota(jnp.int32, lk.shape, 1) < lax.broadcasted_iota(jnp.int32, lk.shape, 0)
                lk = jnp.where(mask, lk, 0.0)
            between = _dot_mask(lk, later, 2)
            w = jnp.exp(lb + between + run[r0:])
            if d is not None:
                w = jnp.where(mask, w, 0.0)
            return (_tail_update(acc, r0, acc[r0:] + _dot(w.astype(bf16), vv)),
                    _tail_update(run, r0, run[r0:] + between[:, 0:1] + lk[:, 0:1]))

        carry = (jnp.zeros((tq, dh), f32), jnp.zeros((tq, 1), f32))
        for d in range(nd - 1, -1, -1):
            carry = step(qi * nd + d, carry, d)
        n_before = qi * nd
        acc, run = lax.fori_loop(0, n_before, lambda t, c: step(n_before - 1 - t, c, None), carry)
        o_ref[...] = acc.astype(o_ref.dtype)
        c_ref[...] = jnp.broadcast_to(run, (tq, dh))
        if comm is not None:
            @pl.when(last)
            def _():
                comm["finish"](*cargs)

    qblk = pl.BlockSpec((tq, dh), lambda h, i: (i, h))
    full = pl.BlockSpec((T, dh), lambda h, i: (0, h))
    (o, c), got = _host_call(body, name, grid, [qblk, full, full], [qblk, qblk],
                             [jax.ShapeDtypeStruct((T, W), bf16), jax.ShapeDtypeStruct((T, W), f32)], [], [qn, kn, vb], comm)
    return o, c, got


def _sb_bwd(qn, kn, vb, do, ctot, do_off, name, comm=None, tq=2048, tk=256):
    T, W = qn.shape
    tq = min(tq, T)
    tk = min(tk, tq)
    nh, nq, dh, nd = W // SB_HEAD_DIM, T // tq, SB_HEAD_DIM, tq // tk
    scale = dh ** -0.5
    ob = do_off // dh
    grid = (nh, nq)

    def body(*refs):
        (q_ref, k_ref, v_ref, do_ref, c_ref), (dq_ref, dk_ref, dv_ref), _, cargs = _hosted(comm, 5, 3, refs)
        first, last = _first_last(grid)
        if comm is not None:
            @pl.when(first)
            def _():
                comm["start"](*cargs)

        qi = pl.program_id(1)

        @pl.when(qi == 0)
        def _():
            dk_ref[...] = jnp.zeros_like(dk_ref)
            dv_ref[...] = jnp.zeros_like(dv_ref)

        q = q_ref[...]
        dob = do_ref[...].astype(bf16)
        total = c_ref[:, 0:1]
        r2 = lax.broadcasted_iota(jnp.int32, (tk, tk), 0)
        c2 = lax.broadcasted_iota(jnp.int32, (tk, tk), 1)
        upto = (r2 <= c2).astype(bf16)
        before = (r2 < c2).astype(bf16)

        def step(j, carry, d):
            dq, pre, gpre = carry
            r0 = 0 if d is None else d * tk
            ks = pl.multiple_of(j * tk, tk)
            kb, vv = k_ref[pl.ds(ks, tk), :], v_ref[pl.ds(ks, tk), :]
            qs, dos = q[r0:], dob[r0:]
            _, lb, lk = _sb_logits(qs, kb, scale)
            if d is not None:
                mask = lax.broadcasted_iota(jnp.int32, lk.shape, 1) < lax.broadcasted_iota(jnp.int32, lk.shape, 0)
                lk = jnp.where(mask, lk, 0.0)
            pin = _dot_mask(lk, upto, 2)
            w = jnp.exp(lb + (total[r0:] - pre[r0:] - pin))
            if d is not None:
                w = jnp.where(mask, w, 0.0)
            dw = _dot(dos, vv, "nt")
            dv_ref[pl.ds(ks, tk), :] += _dot(w.astype(bf16), dos, "tn")
            gg = dw * w
            gex = _dot(gg.astype(bf16), before)
            beta = jnp.exp(lb)
            dz = (gg * (1.0 - beta) - (gpre[r0:] + gex) * beta) * scale
            if d is not None:
                dz = jnp.where(mask, dz, 0.0)
            dzb = dz.astype(bf16)
            dk_ref[pl.ds(ks, tk), :] += _dot(dzb, qs, "tn")
            return (_tail_update(dq, r0, dq[r0:] + _dot(dzb, kb)),
                    _tail_update(pre, r0, pre[r0:] + pin[:, tk - 1:tk]),
                    _tail_update(gpre, r0, gpre[r0:] + gex[:, tk - 1:tk] + gg[:, tk - 1:tk]))

        init = (jnp.zeros((tq, dh), f32), jnp.zeros((tq, 1), f32), jnp.zeros((tq, 1), f32))
        carry = lax.fori_loop(0, qi * nd, lambda t, c: step(t, c, None), init)
        for d in range(nd):
            carry = step(qi * nd + d, carry, d)
        dq_ref[...] = carry[0]
        if comm is not None:
            @pl.when(last)
            def _():
                comm["finish"](*cargs)

    qblk = pl.BlockSpec((tq, dh), lambda h, i: (i, h))
    full = pl.BlockSpec((T, dh), lambda h, i: (0, h))
    (dq, dk, dv), got = _host_call(
        body, name, grid, [qblk, full, full, pl.BlockSpec((tq, dh), lambda h, i: (i, h + ob)), qblk], [qblk, full, full],
        [jax.ShapeDtypeStruct((T, W), f32)] * 3, [], [qn, kn, vb, do, ctot], comm)
    return dq, dk, dv, got


def _pool_select(sums, g):
    return jnp.where(g == 0, sums[0], jnp.where(g == 1, sums[1], jnp.where(g == 2, sums[2], sums[3])))


def _pool_count(g, r, rc, n, cols, off=0):
    t = (r * rc + off + lax.broadcasted_iota(jnp.int32, (n, cols), 0)).astype(f32)
    win = jnp.left_shift(2, g).astype(f32)
    return jnp.minimum(t + 1.0, win)


def _pool_fwd(hp, xres, w, b, scale, name, rc=512, comm=None):
    T, D = hp.shape
    rc = min(rc, T)
    pg = D // len(POOL_WINDOWS)
    grid = (len(POOL_WINDOWS), T // rc)

    def body(*refs):
        (h_ref, p_ref, x_ref, w_ref, b_ref, s_ref), (o_ref, yp_ref, d_ref), _, cargs = _hosted(comm, 6, 3, refs)
        if comm is not None:
            first, last = _first_last(grid)

            @pl.when(first)
            def _():
                comm["start"](*cargs)

            @pl.when(last)
            def _():
                comm["finish"](*cargs)

        g, r = pl.program_id(0), pl.program_id(1)
        cur = h_ref[...]
        halo = jnp.where(r > 0, p_ref[...], 0.0)
        ext = jnp.concatenate([halo, cur], axis=0)
        sums, s = [], ext
        for sh in (1, 2, 4, 8):
            s = s + pltpu.roll(s, sh, 0)
            sums.append(s)
        d = _pool_select(sums, g)[POOL_HALO:] / _pool_count(g, r, rc, rc, pg) - cur
        yp = _dot(d.astype(bf16), w_ref[0]) + b_ref[...]
        yp_ref[...] = yp
        d_ref[...] = d.astype(d_ref.dtype)
        o_ref[...] = x_ref[...] + yp * s_ref[...]

    cur = pl.BlockSpec((rc, pg), lambda g, r: (r, g))
    prev = pl.BlockSpec((POOL_HALO, pg), lambda g, r: (jnp.maximum(r * (rc // POOL_HALO) - 1, 0), g))
    vec = pl.BlockSpec((1, pg), lambda g, r: (0, g))
    (o, yp, d), got = _host_call(
        body, name, grid, [cur, prev, cur, pl.BlockSpec((1, pg, pg), lambda g, r: (g, 0, 0)), vec, vec], [cur, cur, cur],
        [jax.ShapeDtypeStruct((T, D), f32), jax.ShapeDtypeStruct((T, D), f32), jax.ShapeDtypeStruct((T, D), bf16)],
        [], [hp, hp, xres, w, b, scale], comm)
    return o, yp, d, got


def _pool_bwd(dx, yp, d, w, scale, name, rc=512):
    T, D = dx.shape
    rc = min(rc, T)
    pg = D // len(POOL_WINDOWS)
    nr = T // rc

    def body(dx_ref, dn_ref, yp_ref, d_ref, w_ref, s_ref, dh_ref, dw_ref, db_ref, dsc_ref):
        g, r = pl.program_id(0), pl.program_id(1)

        @pl.when(r == 0)
        def _():
            dw_ref[...] = jnp.zeros_like(dw_ref)
            db_ref[...] = jnp.zeros_like(db_ref)
            dsc_ref[...] = jnp.zeros_like(dsc_ref)

        dxv = dx_ref[...]
        dyp = dxv * s_ref[...]
        dsc_ref[...] += jnp.sum(dxv * yp_ref[...], axis=0, keepdims=True)
        db_ref[...] += jnp.sum(dyp, axis=0, keepdims=True)
        dypb = dyp.astype(bf16)
        dw_ref[0] += _dot(d_ref[...], dypb, "tn")
        dd = _dot(dypb, w_ref[0], "nt")
        ddn = _dot((dn_ref[...] * s_ref[...]).astype(bf16), w_ref[0], "nt")
        e = dd / _pool_count(g, r, rc, rc, pg)
        en = jnp.where(r < nr - 1, ddn / _pool_count(g, r, rc, POOL_HALO, pg, off=rc), 0.0)
        ext = jnp.concatenate([e, en], axis=0)
        sums, s = [], ext
        for sh in (1, 2, 4, 8):
            s = s + pltpu.roll(s, rc + POOL_HALO - sh, 0)
            sums.append(s)
        dh_ref[...] = _pool_select(sums, g)[:rc] - dd

    cur = pl.BlockSpec((rc, pg), lambda g, r: (r, g))
    nxt = pl.BlockSpec((POOL_HALO, pg), lambda g, r: (jnp.minimum((r + 1) * (rc // POOL_HALO), T // POOL_HALO - 1), g))
    vec = pl.BlockSpec((1, pg), lambda g, r: (0, g))
    wsp = pl.BlockSpec((1, pg, pg), lambda g, r: (g, 0, 0))
    return pl.pallas_call(
        body, name=name, grid=(len(POOL_WINDOWS), nr), in_specs=[cur, nxt, cur, cur, wsp, vec],
        out_specs=[cur, wsp, vec, vec],
        out_shape=[jax.ShapeDtypeStruct((T, D), f32), jax.ShapeDtypeStruct(w.shape, f32),
                   jax.ShapeDtypeStruct((1, D), f32), jax.ShapeDtypeStruct((1, D), f32)],
        compiler_params=_cp("parallel", "arbitrary"),
    )(dx, dx, yp, d, w, scale)


def _adamw(w, g, m, v, name, tr=256, comm=None):
    R, C = w.shape
    tr = min(tr, R)
    lanes = -(-C // LANES) * LANES
    while tr > 8 and 2 * 8 * tr * lanes * 4 > MM_TILE_BUDGET:
        tr //= 2
    assert R % tr == 0
    grid = (R // tr,)

    def body(*refs):
        (w_ref, g_ref, m_ref, v_ref), (d_ref, mo_ref, vo_ref, go_ref), _, cargs = _hosted(comm, 4, 4, refs)
        if comm is not None:
            first, last = _first_last(grid)

            @pl.when(first)
            def _():
                comm["start"](*cargs)

            @pl.when(last)
            def _():
                comm["finish"](*cargs)

        gv = g_ref[...]
        mn = ADAM_B1 * m_ref[...] + (1.0 - ADAM_B1) * gv
        vn = ADAM_B2 * v_ref[...] + (1.0 - ADAM_B2) * (gv * gv)
        m_hat = mn / (1.0 - ADAM_B1 ** ADAM_STEP)
        v_hat = vn / (1.0 - ADAM_B2 ** ADAM_STEP)
        d_ref[...] = -ADAM_LR * (m_hat / (jnp.sqrt(v_hat) + ADAM_EPS) + ADAM_WD * w_ref[...])
        mo_ref[...] = mn
        vo_ref[...] = vn
        go_ref[...] = gv

    blk = pl.BlockSpec((tr, C), lambda r: (r, 0))
    outs, got = _host_call(body, name, grid, [blk] * 4, [blk] * 4, [jax.ShapeDtypeStruct((R, C), f32)] * 4, [], [w, g, m, v], comm)
    return (*outs, got)


def _pair_sum(g4, recv, name, br=256):
    _, R, C = g4.shape
    hr = R // 2
    br = min(br, hr)
    nb = hr // br

    def body(a_ref, b_ref, o_ref):
        o_ref[...] = (a_ref[...] + b_ref[...]).astype(o_ref.dtype)

    out = pl.BlockSpec((1, br, C), lambda s, i: (s, i, 0))
    return pl.pallas_call(
        body, name=name, grid=(N_CHIPS, nb),
        in_specs=[pl.BlockSpec((1, br, C), lambda s, i: (s, lax.axis_index("c") * nb + i, 0)), out], out_specs=out,
        out_shape=jax.ShapeDtypeStruct((N_CHIPS, hr, C), bf16), compiler_params=_cp("parallel", "parallel"),
    )(g4, recv)


def _chip_sum(g4, recv, pieces, name, br=256):
    _, hr, C = recv.shape
    br = min(br, hr)
    nb = hr // br
    chip = lambda: 2 * lax.axis_index("x") + lax.axis_index("y")

    def body(a_ref, r_ref, b1_ref, b2_ref, b3_ref, o_ref):
        o_ref[...] = (((a_ref[0] + r_ref[0]) + b1_ref[0].astype(f32)) + b2_ref[0].astype(f32)) + b3_ref[0].astype(f32)

    other = lambda k: pl.BlockSpec((1, br, C), lambda i: ((chip() + k) % N_CHIPS, i, 0))
    return pl.pallas_call(
        body, name=name, grid=(nb,),
        in_specs=[pl.BlockSpec((1, br, C), lambda i: (chip(), lax.axis_index("c") * nb + i, 0)),
                  pl.BlockSpec((1, br, C), lambda i: (chip(), i, 0)), other(1), other(2), other(3)],
        out_specs=pl.BlockSpec((br, C), lambda i: (lax.axis_index("c") * nb + i, 0)),
        out_shape=jax.ShapeDtypeStruct((2 * hr, C), f32), compiler_params=_cp("parallel"),
    )(g4, recv, pieces, pieces, pieces)


ANY = pl.BlockSpec(memory_space=pl.ANY)


def _mesh_pos():
    x, y, c = lax.axis_index("x"), lax.axis_index("y"), lax.axis_index("c")
    others = [(1 - x, y), (x, 1 - y), (1 - x, 1 - y)]
    return x, y, c, 2 * x + y, others


def _gather_small(blk, name):
    m, n = blk.shape

    def body(x_ref, out_ref, sum_ref, send_sems, recv_sems, local_sem):
        x, y, c, _, others = _mesh_pos()
        me, sibling = (x, y, c), (x, y, 1 - c)

        def rows(px, py, pc):
            return out_ref.at[pl.ds((4 * px + 2 * py + pc) * m, m), :]

        def copy(k, block, to, src=None):
            return pltpu.make_async_remote_copy(
                src_ref=rows(*block) if src is None else src, dst_ref=rows(*block),
                send_sem=send_sems.at[k], recv_sem=recv_sems.at[k], device_id=to, device_id_type=MESH)

        mine = pltpu.make_async_copy(x_ref, rows(*me), local_sem)
        mine.start()
        first = [copy(0, me, sibling, src=x_ref)]
        first += [copy(1 + j, me, (*chip, c), src=x_ref) for j, chip in enumerate(others)]
        for cp in first:
            cp.start()
        passed = [copy(4 + j, (*chip, c), sibling) for j, chip in enumerate(others)]
        for j, chip in enumerate(others):
            copy(1 + j, (*chip, c), me).wait_recv()
            passed[j].start()
        copy(0, sibling, me).wait_recv()
        for j, chip in enumerate(others):
            copy(4 + j, (*chip, 1 - c), me).wait_recv()
        for cp in first + passed:
            cp.wait_send()
        mine.wait()
        acc = out_ref[0:m, :]
        for d in range(1, 8):
            acc = acc + out_ref[d * m:(d + 1) * m, :]
        sum_ref[...] = acc

    vm = pl.BlockSpec(memory_space=pltpu.VMEM)
    return pl.pallas_call(
        body, name=name, in_specs=[vm], out_specs=[vm, vm],
        out_shape=[jax.ShapeDtypeStruct((8 * m, n), f32), jax.ShapeDtypeStruct((m, n), f32)],
        scratch_shapes=[pltpu.SemaphoreType.DMA((7,)), pltpu.SemaphoreType.DMA((7,)), pltpu.SemaphoreType.DMA],
    )(blk)


def _copy(src, dst, sems, idx, to):
    return pltpu.make_async_remote_copy(src_ref=src, dst_ref=dst, send_sem=sems[0].at[idx], recv_sem=sems[1].at[idx],
                                        device_id=to, device_id_type=MESH)


def _gather_ici(shards):
    nt = len(shards)

    def copies(ins, outs, sems):
        x, y, c, chip, others = _mesh_pos()
        send, land = [], []
        for t in range(nt):
            hr = ins[t].shape[0] // 2
            for j, (px, py) in enumerate(others):
                send.append((ins[t].at[pl.ds(c * hr, hr)], outs[t].at[chip, pl.ds(c * hr, hr)], sems, (t, j), (px, py, c)))
                piece = outs[t].at[2 * px + py, pl.ds(c * hr, hr)]
                land.append((piece, piece, sems, (t, j), (px, py, c)))
        return send, land

    def start(ins, outs, sems):
        for args in copies(ins, outs, sems)[0]:
            _copy(*args).start()

    def finish(ins, outs, sems):
        send, land = copies(ins, outs, sems)
        for args in land:
            _copy(*args).wait_recv()
        for args in send:
            _copy(*args).wait_send()

    return dict(ins=list(shards), outs=[jax.ShapeDtypeStruct((N_CHIPS,) + s.shape, s.dtype) for s in shards],
                sems=[pltpu.SemaphoreType.DMA((nt, 3)), pltpu.SemaphoreType.DMA((nt, 3))], start=start, finish=finish)


def _gather_d2d(stacks):
    nt = len(stacks)

    def copies(ins, outs, sems):
        x, y, c, _, others = _mesh_pos()
        send, land = [], []
        for t in range(nt):
            hr = outs[t].shape[1] // 2
            for j, (px, py) in enumerate(others):
                mine = outs[t].at[2 * px + py, pl.ds(c * hr, hr)]
                theirs = outs[t].at[2 * px + py, pl.ds((1 - c) * hr, hr)]
                send.append((mine, mine, sems, (t, j), (x, y, 1 - c)))
                land.append((theirs, theirs, sems, (t, j), (x, y, 1 - c)))
        return send, land

    def start(ins, outs, sems):
        for args in copies(ins, outs, sems)[0]:
            _copy(*args).start()

    def finish(ins, outs, sems):
        send, land = copies(ins, outs, sems)
        for args in land:
            _copy(*args).wait_recv()
        for args in send:
            _copy(*args).wait_send()

    return dict(ins=list(stacks), outs=[jax.ShapeDtypeStruct(s.shape, s.dtype) for s in stacks],
                sems=[pltpu.SemaphoreType.DMA((nt, 3)), pltpu.SemaphoreType.DMA((nt, 3))], start=start, finish=finish,
                aliases={t: t for t in range(nt)})


def _run_exchange(comm, name):
    ni, no = len(comm["ins"]), len(comm["outs"])

    def body(*refs):
        args = (refs[:ni], refs[ni:ni + no], refs[ni + no:])
        comm["start"](*args)
        comm["finish"](*args)

    return pl.pallas_call(
        body, name=name, in_specs=[ANY] * ni, out_specs=[ANY] * no, out_shape=comm["outs"], scratch_shapes=comm["sems"],
        input_output_aliases=dict(comm.get("aliases", {})))(*comm["ins"])


def _swap_halves(g4s):
    nt = len(g4s)

    def copies(ins, outs, sems):
        x, y, c, _, _ = _mesh_pos()
        both = []
        for t in range(nt):
            hr = ins[t].shape[1] // 2
            both.append((ins[t].at[:, pl.ds((1 - c) * hr, hr)], outs[t], sems, t, (x, y, 1 - c)))
        return both, both

    def start(ins, outs, sems):
        for args in copies(ins, outs, sems)[0]:
            _copy(*args).start()

    def finish(ins, outs, sems):
        send, land = copies(ins, outs, sems)
        for args in land:
            _copy(*args).wait_recv()
        for args in send:
            _copy(*args).wait_send()

    return dict(ins=list(g4s), outs=[jax.ShapeDtypeStruct((N_CHIPS, g.shape[1] // 2, g.shape[2]), g.dtype) for g in g4s],
                sems=[pltpu.SemaphoreType.DMA((nt,)), pltpu.SemaphoreType.DMA((nt,))], start=start, finish=finish)


def _exchange_chips(h4s):
    nt = len(h4s)

    def copies(ins, outs, sems):
        x, y, c, chip, others = _mesh_pos()
        send, land = [], []
        for t in range(nt):
            for j, (px, py) in enumerate(others):
                send.append((ins[t].at[2 * px + py], outs[t].at[chip], sems, (t, j), (px, py, c)))
                landed = outs[t].at[2 * px + py]
                land.append((landed, landed, sems, (t, j), (px, py, c)))
        return send, land

    def start(ins, outs, sems):
        for args in copies(ins, outs, sems)[0]:
            _copy(*args).start()

    def finish(ins, outs, sems):
        send, land = copies(ins, outs, sems)
        for args in land:
            _copy(*args).wait_recv()
        for args in send:
            _copy(*args).wait_send()

    return dict(ins=list(h4s), outs=[jax.ShapeDtypeStruct(h.shape, h.dtype) for h in h4s],
                sems=[pltpu.SemaphoreType.DMA((nt, 3)), pltpu.SemaphoreType.DMA((nt, 3))], start=start, finish=finish)


def _join_halves(fs):
    nt = len(fs)

    def copies(ins, outs, sems):
        x, y, c, _, _ = _mesh_pos()
        send, land = [], []
        for t in range(nt):
            hr = outs[t].shape[0] // 2
            mine, theirs = outs[t].at[pl.ds(c * hr, hr)], outs[t].at[pl.ds((1 - c) * hr, hr)]
            send.append((mine, mine, sems, t, (x, y, 1 - c)))
            land.append((theirs, theirs, sems, t, (x, y, 1 - c)))
        return send, land

    def start(ins, outs, sems):
        for args in copies(ins, outs, sems)[0]:
            _copy(*args).start()

    def finish(ins, outs, sems):
        send, land = copies(ins, outs, sems)
        for args in land:
            _copy(*args).wait_recv()
        for args in send:
            _copy(*args).wait_send()

    return dict(ins=list(fs), outs=[jax.ShapeDtypeStruct(f.shape, f.dtype) for f in fs],
                sems=[pltpu.SemaphoreType.DMA((nt,)), pltpu.SemaphoreType.DMA((nt,))], start=start, finish=finish,
                aliases={t: t for t in range(nt)})


def _pad_lanes(v, n=LANES):
    return jnp.pad(v, ((0, 0), (0, n - v.shape[-1])))


def _mlp_fwd(xin, norm_g, w_up, w_down, F, tag, comms=(None, None)):
    T, D = xin.shape
    h = _rms_fwd(xin, norm_g, bf16, f"{tag}_norm")

    def relu_sq(acc):
        r = jnp.maximum(acc, 0.0)
        return r, r * r

    got = [None, None]
    ua = _mm(h, w_up[0], "nn", T, F, D, (bf16, bf16), f"{tag}_up", epilogue=relu_sq, b_view=w_up[1], comm=comms[0])
    (u, a), got[0] = ua if comms[0] is not None else (ua, None)
    out = _mm(a, w_down[0], "nn", T, D, F, (f32,), f"{tag}_down", epilogue=lambda acc, res: (res + acc,),
              extras=((xin, "tile"),), b_view=w_down[1], comm=comms[1])
    (out,), got[1] = out if comms[1] is not None else ((out,), None)
    return out, (xin, h, u, a), got


def _mlp_bwd(dy, saved, norm_g, w_up, w_down, F, tag, up_to=None, down_to=None, host=None):
    xin, h, u, a = saved
    T, D = xin.shape
    to = lambda t: {} if t is None else dict(out_view=t[0], out_stack=t[1], alias=t[2])
    du = _mm(dy, w_down[0], "nt", T, F, D, (bf16,), f"{tag}_dact", epilogue=lambda acc, uu: (acc * (2.0 * uu.astype(f32)),),
             extras=((u, "tile"),), b_view=w_down[1])
    dw_down = _mm(a, dy, "tn", F, D, T, (f32,), f"{tag}_dwdown", **to(down_to))
    dw_up = _mm(h, du, "tn", D, F, T, (f32,), f"{tag}_dwup", **to(up_to))
    dh = _mm(du, w_up[0], "nt", T, D, F, (f32,), f"{tag}_dh", b_view=w_up[1], comm=host(dw_up, dw_down) if host else None)
    (dh,), got = dh if host else ((dh,), None)
    dx, dg = _rms_bwd(xin, norm_g, dh, dy, f"{tag}_dnorm")
    return dx, dg, dw_up, dw_down, got


def _local_step(xc, tgt, W, HS, SBW, net=None):
    T, D = xc.shape
    SW = HS * SSD_HEAD_DIM
    CD = W["conv_b"].shape[-1]
    mlp_norm = W["mlp_norm"]
    add = lambda acc, prev: (prev + acc,)

    h0 = _rms_fwd(xc, W["hyb_norm"], bf16, "hyb_norm")
    z = _mm(h0, W["w_z"], "nn", T, SW, D, (f32,), "proj_z")
    xraw = _mm(h0, W["w_xbc"], "nn", T, CD, D, (f32,), "proj_xbc")
    dtraw = _mm(h0, W["w_dt"], "nn", T, LANES, D, (f32,), "proj_dt")
    qkv = _mm(h0, W["w_qkv"], "nn", T, 3 * SBW, D, (f32,), "proj_qkv")
    qn, kn, vb = _qk_norm_fwd(qkv, W["q_norm"], W["k_norm"], SBW, "qk_norm")
    y_sb, ctot, got = _sb_fwd(qn, kn, vb, "sb_attn", comm=net.rest_ici() if net else None)
    xbc = _conv_fwd(xraw, W["conv_w"], W["conv_b"], "conv")
    y_ssd, yn_ssd, sprev, got = _ssd_fwd(xbc, dtraw, z, W["dt_bias"], W["a_log"], W["d_skip"], W["out_norm"], HS, "ssd",
                                         comm=net.rest_d2d(got) if net else None)
    if net:
        W = {**W, **net.rest_weights(got)}
    w_up, w_down, F = W["w_up"], W["w_down"], W["F"]
    mix = _mm(yn_ssd, W["w_out"], "nn", T, D, SW, (f32,), "out_ssd", epilogue=add, extras=((xc, "tile"),))
    x1 = _mm(y_sb, W["w_out"], "nn", T, D, SBW, (f32,), "out_sb", epilogue=add, extras=((mix, "tile"),), b_off=(SW, 0))
    x2, mlp0, got = _mlp_fwd(x1, mlp_norm[0:1], w_up[0], w_down[0], F, "mlp0",
                             comms=(net.last_ici(0), net.last_ici(1)) if net else (None, None))
    hp = _rms_fwd(x2, W["pool_norm"], f32, "pool_norm")
    x3, yp, dpool, got = _pool_fwd(hp, x2, W["w_pool"], W["pool_b"], W["pool_scale"], "pool",
                                   comm=net.last_d2d(got) if net else None)
    if net:
        w_up, w_down = net.last_weights(got, w_up, w_down)
    x4, mlp1, _ = _mlp_fwd(x3, mlp_norm[1:2], w_up[1], w_down[1], F, "mlp1")

    dy, sq = _loss_grad(x4, tgt, "loss")

    up_to, down_to = (net.mlp_to("up", 1, None), net.mlp_to("down", 1, None)) if net else (None, None)
    dx3, dg_mlp1, dw_up1, dw_down1, _ = _mlp_bwd(dy, mlp1, mlp_norm[1:2], w_up[1], w_down[1], F, "mlp1", up_to, down_to)
    dhp, dw_pool, db_pool, dsc_pool = _pool_bwd(dx3, yp, dpool, W["w_pool"], W["pool_scale"], "pool_bwd")
    dx2, dg_pool = _rms_bwd(x2, W["pool_norm"], dhp, dx3, "pool_dnorm")
    up_to, down_to = (net.mlp_to("up", 0, dw_up1), net.mlp_to("down", 0, dw_down1)) if net else (None, None)
    dx1, dg_mlp0, dw_up0, dw_down0, got_mlp = _mlp_bwd(dx2, mlp0, mlp_norm[0:1], w_up[0], w_down[0], F, "mlp0", up_to, down_to,
                                                       host=net.swap_mlp if net else None)

    dw_out = jnp.concatenate([_mm(yn_ssd, dx1, "tn", SW, D, T, (f32,), "dwout_ssd"),
                              _mm(y_sb, dx1, "tn", SBW, D, T, (f32,), "dwout_sb")], axis=0)
    if net:
        (dmerged,), got = _mm(dx1, W["w_out"], "nt", T, SW + SBW, D, (f32,), "dmerged",
                              comm=net.swap_rest(dw_out, dw_pool))
        dqn, dkn, dvv, got = _sb_bwd(qn, kn, vb, dmerged, ctot, SW, "sb_attn_bwd", comm=net.reduce_early(list(got) + list(got_mlp)))
        net.reduce_early_done(got)
    else:
        dmerged = _mm(dx1, W["w_out"], "nt", T, SW + SBW, D, (f32,), "dmerged")
        dqn, dkn, dvv, _ = _sb_bwd(qn, kn, vb, dmerged, ctot, SW, "sb_attn_bwd")
    dqkv, dg_q, dg_k = _qk_norm_bwd(qkv, dqn, dkn, dvv, W["q_norm"], W["k_norm"], SBW, "qk_norm_bwd")
    dy_ssd, dz, dg_on = _gate_bwd(y_ssd, z, dmerged, W["out_norm"], "gate_bwd")
    dxbc, ddtraw, dalog, dbias, ddskip, got = _ssd_bwd(xbc, dtraw, sprev, dy_ssd, W["dt_bias"], W["a_log"], W["d_skip"], HS, "ssd_bwd",
                                                       comm=net.early_join() if net else None)
    if net:
        net.early_joined(got)
    dpre, dconv_w, dconv_b = _conv_bwd_pre(xraw, dxbc, W["conv_w"], W["conv_b"], "conv_bwd_pre")
    dxraw = _conv_bwd_in(dpre, W["conv_w"], "conv_bwd_in")
    dw_in = [_mm(h0, dz, "tn", D, SW, T, (f32,), "dwin_z"), _mm(h0, dxraw, "tn", D, CD, T, (f32,), "dwin_xbc"),
             _mm(h0, ddtraw, "tn", D, LANES, T, (f32,), "dwin_dt")[:, :HS], _mm(h0, dqkv, "tn", D, 3 * SBW, T, (f32,), "dwin_qkv")]
    dh0 = _mm(dz, W["w_z"], "nt", T, D, SW, (f32,), "dh0_z")
    if net:
        (dh0,), got = _mm(dxraw, W["w_xbc"], "nt", T, D, CD, (f32,), "dh0_xbc", epilogue=add, extras=((dh0, "tile"),),
                          comm=net.late_swap(dw_in))
        dh0 = _mm(ddtraw, W["w_dt"], "nt", T, D, LANES, (f32,), "dh0_dt", epilogue=add, extras=((dh0, "tile"),))
        (dh0,), got = _mm(dqkv, W["w_qkv"], "nt", T, D, 3 * SBW, (f32,), "dh0_qkv", epilogue=add, extras=((dh0, "tile"),),
                          comm=net.reduce_late(got))
        net.late_part_done(0, got)
        grad_x, dg_hyb, got = _rms_bwd(xc, W["hyb_norm"], dh0, dx1, "hyb_dnorm", comm=net.late_part(1))
        net.late_part_done(1, got)
    else:
        dh0 = _mm(dxraw, W["w_xbc"], "nt", T, D, CD, (f32,), "dh0_xbc", epilogue=add, extras=((dh0, "tile"),))
        dh0 = _mm(ddtraw, W["w_dt"], "nt", T, D, LANES, (f32,), "dh0_dt", epilogue=add, extras=((dh0, "tile"),))
        dh0 = _mm(dqkv, W["w_qkv"], "nt", T, D, 3 * SBW, (f32,), "dh0_qkv", epilogue=add, extras=((dh0, "tile"),))
        grad_x, dg_hyb = _rms_bwd(xc, W["hyb_norm"], dh0, dx1, "hyb_dnorm")
    grads = dict(w_in=dw_in, w_out=dw_out, w_pool=dw_pool, w_up=(dw_up0, dw_up1), w_down=(dw_down0, dw_down1),
                 hyb_norm=dg_hyb, conv_w=dconv_w, conv_b=dconv_b, dt_bias=dbias, a_log=dalog, d_skip=ddskip, out_norm=dg_on,
                 q_norm=dg_q, k_norm=dg_k, mlp_norm=(dg_mlp0, dg_mlp1), pool_norm=dg_pool, pool_b=db_pool, pool_scale=dsc_pool)
    return sq, grad_x, grads


def _stack_columns(pieces, n):
    cs = sum(p.shape[1] for p in pieces) // n
    slots = []
    for j in range(n):
        parts, off = [], 0
        for p in pieces:
            lo, hi = max(j * cs, off), min((j + 1) * cs, off + p.shape[1])
            if lo < hi:
                parts.append(p[:, lo - off:hi - off])
            off += p.shape[1]
        slots.append(parts[0] if len(parts) == 1 else jnp.concatenate(parts, axis=1))
    return jnp.stack(slots)


class _Net:
    def __init__(self, own_first, own_last, chip, dims):
        self.own, self.own_last, self.chip, self.dims = own_first, own_last, chip, dims

    def _place_own(self, stacks, own):
        return [lax.dynamic_update_index_in_dim(g, o, self.chip, 0) for g, o in zip(stacks, own)]

    def rest_ici(self):
        return _gather_ici(self.own)

    def rest_d2d(self, got):
        return _gather_d2d(list(got))

    def rest_weights(self, got):
        d, nw = self.dims, len(POOL_WINDOWS)
        D, F, PG = d["D"], d["F"], d["PG"]
        fs = F // N_CHIPS
        g_out, g_pool, g_up, g_down = self._place_own(got, self.own)
        w_pool = g_pool.reshape(N_CHIPS, nw, PG // N_CHIPS, PG).transpose(1, 0, 2, 3).reshape(nw, PG, PG)
        return dict(w_out=g_out.reshape(d["MIX"], D), w_pool=w_pool, F=F,
                    w_up=[(g_up, ("cols", fs, 0, D))], w_down=[(g_down, ("rows", fs, 0, None))])

    def last_ici(self, which):
        return _gather_ici([self.own_last[which]])

    def last_d2d(self, got):
        return _gather_d2d([got[0][0], got[1][0]])

    def last_weights(self, stacks, w_up, w_down):
        d = self.dims
        fs = d["F"] // N_CHIPS
        g_up, g_down = self._place_own(stacks, self.own_last)
        return w_up + [(g_up, ("cols", fs, 0, d["D"]))], w_down + [(g_down, ("rows", fs, 0, None))]

    def mlp_to(self, which, layer, earlier):
        d = self.dims
        fs = d["F"] // N_CHIPS
        if which == "up":
            return ("cols", fs, layer, d["D"]), (N_CHIPS, d["NL"] * d["D"], fs), earlier
        return ("rows", fs, layer, None), (N_CHIPS, d["NL"] * fs, d["D"]), earlier

    def swap_mlp(self, g_up, g_down):
        self.g_mlp = [g_up, g_down]
        return _swap_halves(self.g_mlp)

    def swap_rest(self, dw_out, dw_pool):
        d, nw = self.dims, len(POOL_WINDOWS)
        PG = d["PG"]
        self.early_g4 = [dw_out.reshape(N_CHIPS, d["MIX"] // N_CHIPS, d["D"]),
                         dw_pool.reshape(nw, N_CHIPS, PG // N_CHIPS, PG).transpose(1, 0, 2, 3).reshape(N_CHIPS, PG, PG)] + self.g_mlp
        return _swap_halves(self.early_g4[:2])

    def reduce_early(self, recv):
        self.early_recv = list(recv)
        sent = [_pair_sum(g, r, f"grads_early_pair_sum{i}") for i, (g, r) in enumerate(zip(self.early_g4, self.early_recv))]
        return _exchange_chips(sent)

    def reduce_early_done(self, got):
        self.early_got = list(got)

    def late_swap(self, dw_in):
        self.late_g4 = [_stack_columns(dw_in, N_CHIPS)]
        return _swap_halves(self.late_g4)

    def reduce_late(self, recv):
        self.late_recv = list(recv)
        sent = _pair_sum(self.late_g4[0], self.late_recv[0], "grads_late_pair_sum")
        cut = 5 * sent.shape[1] // 8
        self.late_parts = [sent[:, :cut], sent[:, cut:]]
        self.late_got = [None] * 2
        return self.late_part(0)

    def late_part(self, i):
        return _exchange_chips([self.late_parts[i]])

    def late_part_done(self, i, got):
        self.late_got[i] = got[0]

    def _halves(self, g4s, recvs, pieces, tag):
        return [_chip_sum(g, r, p, f"grads_{tag}_chip_sum{i}") for i, (g, r, p) in enumerate(zip(g4s, recvs, pieces))]

    def early_join(self):
        return _join_halves(self._halves(self.early_g4, self.early_recv, self.early_got, "early"))

    def early_joined(self, got):
        self.early_done = list(got)

    def reduced_early(self):
        return self.early_done

    def reduced_late(self):
        halves = self._halves(self.late_g4, self.late_recv, [jnp.concatenate(self.late_got, axis=1)], "late")
        return _run_exchange(_join_halves(halves), "grads_late_join")[0]


def kernel(x, hyb_norm, hyb_w_in, ssd_conv_w, ssd_conv_b, ssd_dt_bias, ssd_a_log, ssd_d, ssd_out_norm, sb_q_norm, sb_k_norm, hyb_w_out, pool_norm, pool_w, pool_b, pool_scale, mlp_norm, mlp_w_up, mlp_w_down, loss_target, m_hyb_norm, m_hyb_w_in, m_ssd_conv_w, m_ssd_conv_b, m_ssd_dt_bias, m_ssd_a_log, m_ssd_d, m_ssd_out_norm, m_sb_q_norm, m_sb_k_norm, m_hyb_w_out, m_pool_norm, m_pool_w, m_pool_b, m_pool_scale, m_mlp_norm, m_mlp_w_up, m_mlp_w_down, v_hyb_norm, v_hyb_w_in, v_ssd_conv_w, v_ssd_conv_b, v_ssd_dt_bias, v_ssd_a_log, v_ssd_d, v_ssd_out_norm, v_sb_q_norm, v_sb_k_norm, v_hyb_w_out, v_pool_norm, v_pool_w, v_pool_b, v_pool_scale, v_mlp_norm, v_mlp_w_up, v_mlp_w_down):
    T, D = x.shape[1], x.shape[2]
    HS = ssd_dt_bias.shape[-1]
    SW = HS * SSD_HEAD_DIM
    CD = ssd_conv_b.shape[-1]
    IN = N_CHIPS * hyb_w_in.shape[-1]
    SBW = (IN - SW - CD - HS) // 3
    F = N_CHIPS * mlp_w_up.shape[-1]
    NL = mlp_norm.shape[0]
    PG = D // len(POOL_WINDOWS)
    xc, tgt = x[0], loss_target[0]
    ix, iy, ic = lax.axis_index("x"), lax.axis_index("y"), lax.axis_index("c")
    chip = (2 * ix + iy).astype(jnp.int32)

    small = jnp.concatenate([ssd_conv_w.reshape(-1), pool_norm.reshape(-1), pool_b.reshape(-1), pool_scale.reshape(-1)])
    ns = small.shape[0]
    ns8 = -(-ns // (8 * LANES)) * LANES
    gathered, _ = _gather_small(jnp.pad(small, (0, 8 * ns8 - ns)).reshape(8, ns8), "gather_small")
    per_chip = gathered.reshape(N_CHIPS, 2, 8 * ns8)[:, 0, :ns]
    cw = CD // N_CHIPS
    conv_w = per_chip[:, :4 * cw].reshape(N_CHIPS, 4, cw).transpose(1, 0, 2).reshape(4, CD)
    pvec = per_chip[:, 4 * cw:].reshape(N_CHIPS, 3, PG)
    pool_norm_f, pool_b_f, pool_scale_f = (pvec[:, i].reshape(1, D) for i in range(3))

    fs = F // N_CHIPS
    own_in = hyb_w_in[0].astype(bf16)
    g_in = _run_exchange(_gather_d2d(_run_exchange(_gather_ici([own_in]), "gather_in_ici")), "gather_in_d2d")[0]
    w_in = lax.dynamic_update_index_in_dim(g_in, own_in, chip, 0).transpose(1, 0, 2).reshape(D, IN)
    c1, c2, c3 = SW, SW + CD, SW + CD + HS
    w_z, w_xbc, w_dt, w_qkv = w_in[:, :c1], w_in[:, c1:c2], _pad_lanes(w_in[:, c2:c3]), w_in[:, c3:]
    dt_bias_p, a_log_p, d_skip_p = _pad_lanes(ssd_dt_bias), _pad_lanes(ssd_a_log), jnp.repeat(ssd_d, SSD_HEAD_DIM, axis=-1)

    assert NL == 2
    own_first = [hyb_w_out[0].astype(bf16), pool_w[0].reshape(-1, PG).astype(bf16),
                 mlp_w_up[0].astype(bf16), mlp_w_down[0].astype(bf16)]
    own_last = [mlp_w_up[1].astype(bf16), mlp_w_down[1].astype(bf16)]
    net = _Net(own_first, own_last, chip, dict(D=D, F=F, NL=NL, PG=PG, IN=IN, MIX=SW + SBW))
    first = dict(hyb_norm=hyb_norm, w_z=w_z, w_xbc=w_xbc, w_dt=w_dt, w_qkv=w_qkv, conv_w=conv_w, conv_b=ssd_conv_b,
                 dt_bias=dt_bias_p, a_log=a_log_p, d_skip=d_skip_p, out_norm=ssd_out_norm, q_norm=sb_q_norm, k_norm=sb_k_norm,
                 pool_norm=pool_norm_f, pool_b=pool_b_f, pool_scale=pool_scale_f, mlp_norm=mlp_norm)
    sq, grad_x, gr = _local_step(xc, tgt, first, HS, SBW, net)
    loss = lax.psum(sq[0, 0] * (0.5 / D), ("x", "y", "c"))
    dg_hyb, dconv_b, dbias, dalog, ddskip, dg_on, dg_q, dg_k = (gr[k] for k in (
        "hyb_norm", "conv_b", "dt_bias", "a_log", "d_skip", "out_norm", "q_norm", "k_norm"))
    (dg_mlp0, dg_mlp1), dconv_w, dg_pool, db_pool, dsc_pool = gr["mlp_norm"], gr["conv_w"], gr["pool_norm"], gr["pool_b"], gr["pool_scale"]
    gb_out, gb_pool, gb_up, gb_down = net.reduced_early()

    full_small = [dg_hyb, dconv_b, dbias[:, :HS], dalog[:, :HS], ddskip[:, :HS], dg_on, dg_q, dg_k,
                  jnp.concatenate([dg_mlp0, dg_mlp1], axis=0).reshape(1, -1),
                  dconv_w.reshape(1, -1), dg_pool, db_pool, dsc_pool]
    sizes = [v.shape[-1] for v in full_small]
    packed = jnp.concatenate([v.reshape(-1) for v in full_small])
    npk = packed.shape[0]
    npk8 = -(-npk // (8 * LANES)) * LANES
    _, summed = _gather_small(jnp.pad(packed, (0, 8 * npk8 - npk)).reshape(8, npk8), "grads_small")
    summed = summed.reshape(-1)[:npk]
    offs = [0]
    for s in sizes:
        offs.append(offs[-1] + s)
    (g_hyb_norm, g_conv_b, g_dt_bias, g_a_log, g_d, g_out_norm, g_q_norm, g_k_norm, g_mlp_norm, g_conv_w_full,
     g_pool_norm_full, g_pool_b_full, g_pool_scale_full) = (summed[offs[i]:offs[i + 1]] for i in range(len(sizes)))
    take = lambda full, n: lax.dynamic_slice_in_dim(full.reshape(-1, N_CHIPS, n), chip, 1, axis=1)
    small_grads = {
        "hyb_norm": g_hyb_norm.reshape(hyb_norm.shape), "ssd_conv_w": take(g_conv_w_full, cw).reshape(ssd_conv_w.shape),
        "ssd_conv_b": g_conv_b.reshape(ssd_conv_b.shape), "ssd_dt_bias": g_dt_bias.reshape(ssd_dt_bias.shape),
        "ssd_a_log": g_a_log.reshape(ssd_a_log.shape), "ssd_d": g_d.reshape(ssd_d.shape),
        "ssd_out_norm": g_out_norm.reshape(ssd_out_norm.shape), "sb_q_norm": g_q_norm.reshape(sb_q_norm.shape),
        "sb_k_norm": g_k_norm.reshape(sb_k_norm.shape), "pool_norm": take(g_pool_norm_full, PG).reshape(pool_norm.shape),
        "pool_b": take(g_pool_b_full, PG).reshape(pool_b.shape), "pool_scale": take(g_pool_scale_full, PG).reshape(pool_scale.shape),
        "mlp_norm": g_mlp_norm.reshape(mlp_norm.shape),
    }

    weights = dict(hyb_norm=hyb_norm, hyb_w_in=hyb_w_in, ssd_conv_w=ssd_conv_w, ssd_conv_b=ssd_conv_b, ssd_dt_bias=ssd_dt_bias,
                   ssd_a_log=ssd_a_log, ssd_d=ssd_d, ssd_out_norm=ssd_out_norm, sb_q_norm=sb_q_norm, sb_k_norm=sb_k_norm,
                   hyb_w_out=hyb_w_out, pool_norm=pool_norm, pool_w=pool_w, pool_b=pool_b, pool_scale=pool_scale,
                   mlp_norm=mlp_norm, mlp_w_up=mlp_w_up, mlp_w_down=mlp_w_down)
    moms = dict(hyb_norm=m_hyb_norm, hyb_w_in=m_hyb_w_in, ssd_conv_w=m_ssd_conv_w, ssd_conv_b=m_ssd_conv_b, ssd_dt_bias=m_ssd_dt_bias,
                ssd_a_log=m_ssd_a_log, ssd_d=m_ssd_d, ssd_out_norm=m_ssd_out_norm, sb_q_norm=m_sb_q_norm, sb_k_norm=m_sb_k_norm,
                hyb_w_out=m_hyb_w_out, pool_norm=m_pool_norm, pool_w=m_pool_w, pool_b=m_pool_b, pool_scale=m_pool_scale,
                mlp_norm=m_mlp_norm, mlp_w_up=m_mlp_w_up, mlp_w_down=m_mlp_w_down)
    vels = dict(hyb_norm=v_hyb_norm, hyb_w_in=v_hyb_w_in, ssd_conv_w=v_ssd_conv_w, ssd_conv_b=v_ssd_conv_b, ssd_dt_bias=v_ssd_dt_bias,
                ssd_a_log=v_ssd_a_log, ssd_d=v_ssd_d, ssd_out_norm=v_ssd_out_norm, sb_q_norm=v_sb_q_norm, sb_k_norm=v_sb_k_norm,
                hyb_w_out=v_hyb_w_out, pool_norm=v_pool_norm, pool_w=v_pool_w, pool_b=v_pool_b, pool_scale=v_pool_scale,
                mlp_norm=v_mlp_norm, mlp_w_up=v_mlp_w_up, mlp_w_down=v_mlp_w_down)
    order = list(weights)
    grads, delta, new_m, new_v = {}, {}, {}, {}
    for name, g2 in (("hyb_w_out", gb_out), ("pool_w", gb_pool), ("mlp_w_up", gb_up), ("mlp_w_down", gb_down),
                     ("hyb_w_in", net.reduced_late())):
        shp = weights[name].shape
        d_, m_, v_, g_, _ = _adamw(weights[name].reshape(g2.shape), g2, moms[name].reshape(g2.shape), vels[name].reshape(g2.shape),
                                   f"adamw_{name}")
        grads[name], delta[name], new_m[name], new_v[name] = (t.reshape(shp) for t in (g_, d_, m_, v_))
    snames = list(small_grads)
    pack = lambda d: jnp.concatenate([d[n].reshape(-1) for n in snames])
    nsm = sum(small_grads[n].size for n in snames)
    cols = -(-nsm // (8 * LANES)) * LANES
    as_blk = lambda v: jnp.pad(v, (0, 8 * cols - nsm)).reshape(8, cols)
    padded_v = jnp.pad(pack(vels), (0, 8 * cols - nsm), constant_values=1.0).reshape(8, cols)
    d_, m_, v_, _, _ = _adamw(as_blk(pack(weights)), as_blk(pack(small_grads)), as_blk(pack(moms)), padded_v, "adamw_small")
    off = 0
    for n in snames:
        sz, shp = small_grads[n].size, weights[n].shape
        grads[n] = small_grads[n]
        delta[n], new_m[n], new_v[n] = (t.reshape(-1)[off:off + sz].reshape(shp) for t in (d_, m_, v_))
        off += sz

    return (loss, grad_x.reshape(x.shape), *[grads[n] for n in order], *[delta[n] for n in order],
            *[new_m[n] for n in order], *[new_v[n] for n in order])
```

```python
import functools
import math

import jax
import jax.numpy as jnp
from jax import lax
from jax.experimental import pallas as pl
from jax.experimental.pallas import tpu as pltpu

f32 = jnp.float32
bf16 = jnp.bfloat16

EPS = 1e-6
SSD_HEAD_DIM = 64
SSD_STATE = 128
SSD_GROUPS = 4
SSD_CHUNK = 128
LANES = 128
SB_HEAD_DIM = 128
POOL_WINDOWS = (2, 4, 8, 16)
POOL_HALO = 16
CONV_HALO = 8
ADAM_LR, ADAM_B1, ADAM_B2, ADAM_EPS, ADAM_WD, ADAM_STEP = 0.001, 0.9, 0.999, 1e-08, 0.01, 10
VMEM_LIMIT = 56 * 1024 * 1024
MM_TILE_BUDGET = 40 * 1024 * 1024
N_CHIPS = 4
MESH = pl.DeviceIdType.MESH

_DIMS = {"nn": (((1,), (0,)), ((), ())), "nt": (((1,), (1,)), ((), ())), "tn": (((0,), (0,)), ((), ()))}


def _fit(n, t):
    if n <= t:
        return n
    return max(d for d in range(LANES, t + 1, LANES) if n % d == 0)


def _cp(*sem):
    return pltpu.CompilerParams(dimension_semantics=sem, vmem_limit_bytes=VMEM_LIMIT)


def _sigmoid(v):
    return 1.0 / (1.0 + jnp.exp(-v))


def _softplus(v):
    return jnp.maximum(v, 0.0) + jnp.log(1.0 + jnp.exp(-jnp.abs(v)))


def _split(v, parts):
    out, rem = [], v
    for _ in range(parts):
        p = rem.astype(bf16)
        out.append(p)
        rem = rem - p.astype(f32)
    return out


def _dot(a, b, mode="nn"):
    return lax.dot_general(a, b, _DIMS[mode], preferred_element_type=f32)


def _mask_dot(mask_b, v, parts):
    return _dot(jnp.concatenate([mask_b] * parts, axis=1), jnp.concatenate(_split(v, parts), axis=0))


def _dot_mask(v, mask_b, parts):
    return _dot(jnp.concatenate(_split(v, parts), axis=1), jnp.concatenate([mask_b] * parts, axis=0))


def _stacked(view, br, bc, rmap, cmap):
    kind, per, layer, rows_per_layer = view
    if kind == "cols":
        npc = per // bc
        return pl.BlockSpec((None, br, bc), lambda i, j, k: (cmap(i, j, k) // npc, layer * (rows_per_layer // br) + rmap(i, j, k),
                                                              cmap(i, j, k) % npc))
    npc = per // br
    return pl.BlockSpec((None, br, bc), lambda i, j, k: (rmap(i, j, k) // npc, layer * npc + rmap(i, j, k) % npc, cmap(i, j, k)))


def _pick_tiles(M, N, K, caps, a_bytes, b_bytes, io_bytes):
    def cands(n, cap, sizes):
        got = [s for s in sizes if s <= min(n, cap) and n % s == 0]
        return got or [_fit(n, min(n, cap))]

    best = None
    for tk in cands(K, caps[2], (8192, 4096, 2048, 1024, 512, 256, 128)):
        for tm in cands(M, caps[0], (1024, 512, 256, 128)):
            for tn in cands(N, caps[1], (1024, 512, 256, 128)):
                need = 2 * (tm * tk * a_bytes + tk * tn * b_bytes) + tm * tn * (2 * io_bytes + (4 if tk < K else 0))
                key = (need <= MM_TILE_BUDGET, tk, tm * tn, tm)
                if best is None or key > best[0]:
                    best = (key, (tm, tn, tk))
    return best[1]


def _mm(a, b, mode, M, N, K, outs, name, epilogue=None, extras=(), a_off=(0, 0), b_off=(0, 0),
        b_view=None, out_view=None, out_stack=None, alias=None, comm=None):
    caps = [M, N, K]
    if b_view is not None:
        caps[1 if (b_view[0] == "cols") == (mode != "nt") else 2] = b_view[1]
    if out_view is not None:
        d = 1 if out_view[0] == "cols" else 0
        caps[d] = min(caps[d], out_view[1])
    for off, dims in ((a_off, (2, 0) if mode == "tn" else (0, 2)), (b_off, (1, 2) if mode == "nt" else (2, 1))):
        for o, d in zip(off, dims):
            if o:
                caps[d] = min(caps[d], math.gcd(o, caps[d]))
    io_bytes = sum(jnp.dtype(dt).itemsize for dt in outs) + sum(e[0].dtype.itemsize for e in extras if e[1] == "tile")
    tm, tn, tk = _pick_tiles(M, N, K, caps, a.dtype.itemsize, b.dtype.itemsize, io_bytes)
    nk = K // tk
    if mode == "tn":
        a_blk, ad = (tk, tm), (tk, tm)
    else:
        a_blk, ad = (tm, tk), (tm, tk)
    b_blk = (tn, tk) if mode == "nt" else (tk, tn)
    assert a_off[0] % ad[0] == 0 and a_off[1] % ad[1] == 0 and b_off[0] % b_blk[0] == 0 and b_off[1] % b_blk[1] == 0
    ao = (a_off[0] // ad[0], a_off[1] // ad[1])
    bo = (b_off[0] // b_blk[0], b_off[1] // b_blk[1])
    if mode == "tn":
        a_map = lambda i, j, k: (k + ao[0], i + ao[1])
    else:
        a_map = lambda i, j, k: (i + ao[0], k + ao[1])
    if mode == "nt":
        b_map = lambda i, j, k: (j + bo[0], k + bo[1])
    else:
        b_map = lambda i, j, k: (k + bo[0], j + bo[1])
    if b_view is not None:
        if mode == "nt":
            b_spec = _stacked(b_view, tn, tk, lambda i, j, k: j, lambda i, j, k: k)
        else:
            b_spec = _stacked(b_view, tk, tn, lambda i, j, k: k, lambda i, j, k: j)
    else:
        b_spec = pl.BlockSpec(b_blk, b_map)
    in_specs = [pl.BlockSpec(a_blk, a_map), b_spec]
    for arr, kind in extras:
        if kind == "tile":
            in_specs.append(pl.BlockSpec((tm, tn), lambda i, j, k: (i, j)))
        else:
            in_specs.append(pl.BlockSpec((1, tn), lambda i, j, k: (0, j)))
    ne, no = len(extras), len(outs)
    if epilogue is None:
        epilogue = lambda acc: (acc,)
    operands = [a, b, *[e[0] for e in extras]]
    aliases = {}
    if alias is not None:
        in_specs.append(ANY)
        aliases[len(operands)] = 0
        operands.append(alias)
    n_in = len(operands)
    if out_view is not None:
        out_specs = [_stacked(out_view, tm, tn, lambda i, j, k: i, lambda i, j, k: j)]
        out_shape = [jax.ShapeDtypeStruct(out_stack, outs[0])]
    else:
        out_specs = [pl.BlockSpec((tm, tn), lambda i, j, k: (i, j)) for _ in outs]
        out_shape = [jax.ShapeDtypeStruct((M, N), dt) for dt in outs]
    scratch = [pltpu.VMEM((tm, tn), f32)] if nk > 1 else []
    grid = (M // tm, N // tn, nk)
    if comm is not None:
        in_specs += [ANY] * len(comm["ins"])
        operands += comm["ins"]
        out_specs += [ANY] * len(comm["outs"])
        out_shape += comm["outs"]
        scratch += comm["sems"]
    nci, nco, ncs = (len(comm["ins"]), len(comm["outs"]), len(comm["sems"])) if comm is not None else (0, 0, 0)

    def body(*refs):
        a_ref, b_ref = refs[0], refs[1]
        ex, out_refs = refs[2:2 + ne], refs[n_in + nci:n_in + nci + no]
        rest = refs[n_in + nci + no + nco:]
        if comm is not None:
            cargs = (refs[n_in:n_in + nci], refs[n_in + nci + no:n_in + nci + no + nco], refs[len(refs) - ncs:])
            pid = [pl.program_id(d) for d in range(3)]

            @pl.when((pid[0] == 0) & (pid[1] == 0) & (pid[2] == 0))
            def _():
                comm["start"](*cargs)

        def finish(acc):
            res = epilogue(acc, *[e[...] for e in ex])
            for o, r in zip(out_refs, res):
                o[...] = r.astype(o.dtype)

        prod = lax.dot_general(a_ref[...].astype(bf16), b_ref[...].astype(bf16), _DIMS[mode],
                               preferred_element_type=f32)
        if nk == 1:
            finish(prod)
        else:
            acc_ref = rest[0]
            k = pl.program_id(2)

            @pl.when(k == 0)
            def _():
                acc_ref[...] = prod

            @pl.when(k > 0)
            def _():
                acc_ref[...] += prod

            @pl.when(k == nk - 1)
            def _():
                finish(acc_ref[...])

        if comm is not None:
            @pl.when((pid[0] == grid[0] - 1) & (pid[1] == grid[1] - 1) & (pid[2] == grid[2] - 1))
            def _():
                comm["finish"](*cargs)

    sem = ("arbitrary",) * 3 if comm is not None else ("parallel", "parallel", "arbitrary")
    res = pl.pallas_call(
        body, name=name, grid=grid, in_specs=in_specs, out_specs=out_specs, out_shape=out_shape,
        scratch_shapes=scratch, input_output_aliases=aliases, compiler_params=_cp(*sem),
    )(*operands)
    if comm is not None:
        return res[:no], res[no:]
    return res[0] if no == 1 else res


def _rms_fwd(x, g, out_dtype, name, tr=256):
    T, D = x.shape
    tr = min(tr, T)

    def body(x_ref, g_ref, o_ref):
        xv = x_ref[...]
        r = lax.rsqrt(jnp.mean(xv * xv, axis=-1, keepdims=True) + EPS)
        o_ref[...] = (xv * r * g_ref[...]).astype(o_ref.dtype)

    return pl.pallas_call(
        body, name=name, grid=(T // tr,),
        in_specs=[pl.BlockSpec((tr, D), lambda r: (r, 0)), pl.BlockSpec((1, D), lambda r: (0, 0))],
        out_specs=pl.BlockSpec((tr, D), lambda r: (r, 0)),
        out_shape=jax.ShapeDtypeStruct((T, D), out_dtype), compiler_params=_cp("parallel"),
    )(x, g)


def _rms_bwd(x, g, dh, dres, name, tr=256, comm=None):
    T, D = x.shape
    tr = min(tr, T)
    grid = (T // tr,)

    def body(*refs):
        (x_ref, g_ref, dh_ref, dres_ref), (dx_ref, dg_ref), _, cargs = _hosted(comm, 4, 2, refs)
        if comm is not None:
            first, last = _first_last(grid)

            @pl.when(first)
            def _():
                comm["start"](*cargs)

            @pl.when(last)
            def _():
                comm["finish"](*cargs)

        xv = x_ref[...]
        r = lax.rsqrt(jnp.mean(xv * xv, axis=-1, keepdims=True) + EPS)
        xh = xv * r
        dhv = dh_ref[...]
        dhg = dhv * g_ref[...]
        dx_ref[...] = dres_ref[...] + r * (dhg - xh * jnp.mean(dhg * xh, axis=-1, keepdims=True))

        @pl.when(pl.program_id(0) == 0)
        def _():
            dg_ref[...] = jnp.zeros_like(dg_ref)

        dg_ref[...] += jnp.sum(dhv * xh, axis=0, keepdims=True)

    row = pl.BlockSpec((tr, D), lambda r: (r, 0))
    vec = pl.BlockSpec((1, D), lambda r: (0, 0))
    (dx, dg), got = _host_call(body, name, grid, [row, vec, row, row], [row, vec],
                               [jax.ShapeDtypeStruct((T, D), f32), jax.ShapeDtypeStruct((1, D), f32)], [], [x, g, dh, dres], comm)
    return (dx, dg) if comm is None else (dx, dg, got)


def _loss_grad(y, tgt, name, tr=256):
    T, D = y.shape
    tr = min(tr, T)

    def body(y_ref, t_ref, dy_ref, s_ref):
        e = y_ref[...] - t_ref[...]
        dy_ref[...] = e * (1.0 / D)

        @pl.when(pl.program_id(0) == 0)
        def _():
            s_ref[...] = jnp.zeros_like(s_ref)

        s_ref[...] += jnp.sum(e * e)

    row = pl.BlockSpec((tr, D), lambda r: (r, 0))
    return pl.pallas_call(
        body, name=name, grid=(T // tr,), in_specs=[row, row],
        out_specs=[row, pl.BlockSpec((8, LANES), lambda r: (0, 0))],
        out_shape=[jax.ShapeDtypeStruct((T, D), f32), jax.ShapeDtypeStruct((8, LANES), f32)],
        compiler_params=_cp("arbitrary"),
    )(y, tgt)


def _shift_down(cur, prev, s):
    rolled = pltpu.roll(cur, s, 0)
    top = pltpu.roll(prev, s, 0)
    row = lax.broadcasted_iota(jnp.int32, top.shape, 0)
    head = jnp.where(row < s, top, rolled[0:CONV_HALO])
    return jnp.concatenate([head, rolled[CONV_HALO:]], axis=0)


def _shift_up(cur, nxt, s):
    n = cur.shape[0]
    rolled = pltpu.roll(cur, n - s, 0)
    bot = pltpu.roll(nxt, CONV_HALO - s, 0)
    row = lax.broadcasted_iota(jnp.int32, bot.shape, 0)
    tail = jnp.where(row >= CONV_HALO - s, bot, rolled[n - CONV_HALO:])
    return jnp.concatenate([rolled[:n - CONV_HALO], tail], axis=0)


def _conv_pre(cur, prev, w_ref, b_ref):
    taps = [cur] + [_shift_down(cur, prev, s) for s in (1, 2, 3)]
    pre = b_ref[...] + w_ref[3:4, :] * taps[0]
    for s in (1, 2, 3):
        pre = pre + w_ref[3 - s:4 - s, :] * taps[s]
    return pre, taps


def _conv_specs(T, C, rc, cb):
    cur = pl.BlockSpec((rc, cb), lambda j, r: (r, j))
    prev = pl.BlockSpec((CONV_HALO, cb), lambda j, r: (jnp.maximum(r * (rc // CONV_HALO) - 1, 0), j))
    nxt = pl.BlockSpec((CONV_HALO, cb), lambda j, r: (jnp.minimum((r + 1) * (rc // CONV_HALO), T // CONV_HALO - 1), j))
    w = pl.BlockSpec((4, cb), lambda j, r: (0, j))
    b = pl.BlockSpec((1, cb), lambda j, r: (0, j))
    return cur, prev, nxt, w, b


def _conv_fwd(xraw, w, b, name):
    T, C = xraw.shape
    rc, cb = min(512, T), min(512, C)
    cur, prev, _, ws, bs = _conv_specs(T, C, rc, cb)

    def body(x_ref, p_ref, w_ref, b_ref, o_ref):
        pv = jnp.where(pl.program_id(1) > 0, p_ref[...], 0.0)
        pre, _ = _conv_pre(x_ref[...], pv, w_ref, b_ref)
        o_ref[...] = pre * _sigmoid(pre)

    return pl.pallas_call(
        body, name=name, grid=(C // cb, T // rc), in_specs=[cur, prev, ws, bs], out_specs=cur,
        out_shape=jax.ShapeDtypeStruct((T, C), f32), compiler_params=_cp("parallel", "parallel"),
    )(xraw, xraw, w, b)


def _conv_bwd_pre(xraw, dxbc, w, b, name):
    T, C = xraw.shape
    rc, cb = min(512, T), min(512, C)
    cur, prev, _, ws, bs = _conv_specs(T, C, rc, cb)

    def body(x_ref, p_ref, d_ref, w_ref, b_ref, dpre_ref, dw_ref, db_ref):
        pv = jnp.where(pl.program_id(1) > 0, p_ref[...], 0.0)
        pre, taps = _conv_pre(x_ref[...], pv, w_ref, b_ref)
        sg = _sigmoid(pre)
        dpre = d_ref[...] * (sg * (1.0 + pre * (1.0 - sg)))
        dpre_ref[...] = dpre

        @pl.when(pl.program_id(1) == 0)
        def _():
            dw_ref[...] = jnp.zeros_like(dw_ref)
            db_ref[...] = jnp.zeros_like(db_ref)

        row = lax.broadcasted_iota(jnp.int32, dw_ref.shape, 0)
        upd = jnp.zeros(dw_ref.shape, f32)
        for s in range(4):
            upd = upd + jnp.where(row == 3 - s, jnp.sum(dpre * taps[s], axis=0, keepdims=True), 0.0)
        dw_ref[...] += upd
        db_ref[...] += jnp.sum(dpre, axis=0, keepdims=True)

    return pl.pallas_call(
        body, name=name, grid=(C // cb, T // rc), in_specs=[cur, prev, cur, ws, bs], out_specs=[cur, ws, bs],
        out_shape=[jax.ShapeDtypeStruct((T, C), f32), jax.ShapeDtypeStruct((4, C), f32), jax.ShapeDtypeStruct((1, C), f32)],
        compiler_params=_cp("parallel", "arbitrary"),
    )(xraw, xraw, dxbc, w, b)


def _conv_bwd_in(dpre, w, name):
    T, C = dpre.shape
    rc, cb = min(512, T), min(512, C)
    cur, _, nxt, ws, _ = _conv_specs(T, C, rc, cb)
    nr = T // rc

    def body(d_ref, n_ref, w_ref, o_ref):
        nv = jnp.where(pl.program_id(1) < nr - 1, n_ref[...], 0.0)
        cv = d_ref[...]
        out = w_ref[3:4, :] * cv
        for s in (1, 2, 3):
            out = out + w_ref[3 - s:4 - s, :] * _shift_up(cv, nv, s)
        o_ref[...] = out.astype(o_ref.dtype)

    return pl.pallas_call(
        body, name=name, grid=(C // cb, nr), in_specs=[cur, nxt, ws], out_specs=cur,
        out_shape=jax.ShapeDtypeStruct((T, C), bf16), compiler_params=_cp("parallel", "parallel"),
    )(dpre, dpre, w)


HEAD_SHIFT = SSD_HEAD_DIM.bit_length() - 1


def _ssd_prep(dtr_ref, bias_ref, alog_ref, SW):
    L = SSD_CHUNK
    xs = dtr_ref[...] + bias_ref[...]
    dt = _softplus(xs)
    a = -jnp.exp(alog_ref[...])
    causal = lax.broadcasted_iota(jnp.int32, (L, L), 0) >= lax.broadcasted_iota(jnp.int32, (L, L), 1)
    cs = _mask_dot(causal.astype(bf16), dt * a, 3)
    spread = (lax.broadcasted_iota(jnp.int32, (LANES, SW), 0)
              == lax.shift_right_logical(lax.broadcasted_iota(jnp.int32, (LANES, SW), 1), HEAD_SHIFT)).astype(bf16)
    dt_x = _dot_mask(dt, spread, 3)
    cs_x = _dot_mask(cs, spread, 3)
    last_x = cs_x[L - 1:L, :]
    return xs, dt, a, causal, cs, cs.T, dt_x, jnp.exp(cs_x), jnp.exp(last_x - cs_x), jnp.exp(last_x)


def _head_decay(cs, csT, causal, h):
    seg = cs[:, h:h + 1] - csT[h:h + 1, :]
    return jnp.where(causal, jnp.exp(jnp.minimum(seg, 0.0)), 0.0)


def _ssd_fwd(xbc, dtraw, z, dt_bias, a_log, d_skip_x, out_norm, HS, name, comm=None):
    T = xbc.shape[0]
    L, P, NS, G = SSD_CHUNK, SSD_HEAD_DIM, SSD_STATE, SSD_GROUPS
    SW, HPG, nc = HS * P, HS // SSD_GROUPS, T // SSD_CHUNK
    gsz = SW // G
    gw = HPG * P
    assert HPG % 2 == 0 and 2 * P == LANES

    def body(*refs):
        (xbc_ref, dtr_ref, z_ref, bias_ref, alog_ref, dsk_ref, on_ref), (y_ref, yn_ref, sp_ref), (st_ref,), cargs = _hosted(comm, 7, 3, refs)
        first, last = _first_last((nc,))
        if comm is not None:
            @pl.when(first)
            def _():
                comm["start"](*cargs)

            @pl.when(last)
            def _():
                comm["finish"](*cargs)

        @pl.when(pl.program_id(0) == 0)
        def _():
            st_ref[...] = jnp.zeros_like(st_ref)

        sp_ref[0] = st_ref[...]
        _, _, _, causal, cs, csT, dt_x, ecs_x, dte_x, cdec_x = _ssd_prep(dtr_ref, bias_ref, alog_ref, SW)
        X = xbc_ref[:, 0:SW]
        Xd = X * dt_x
        Xdb = Xd.astype(bf16)
        XEb = (Xd * dte_x).astype(bf16)
        left = lax.broadcasted_iota(jnp.int32, (L, LANES), 1) < P
        for g in range(G):
            gs = slice(g * gw, (g + 1) * gw)
            Bb = xbc_ref[:, SW + g * NS:SW + (g + 1) * NS].astype(bf16)
            Cb = xbc_ref[:, SW + (G + g) * NS:SW + (G + g + 1) * NS].astype(bf16)
            Gm = _dot(Cb, Bb, "nt")
            Sp = st_ref[:, gs]
            yo = _dot(Cb, Sp.astype(bf16)) * ecs_x[:, gs]
            st_ref[:, gs] = cdec_x[:, gs] * Sp + _dot(Bb, XEb[:, gs], "tn")
            for pr in range(HPG // 2):
                h0 = g * HPG + 2 * pr
                ps = slice(h0 * P, (h0 + 2) * P)
                xp = Xdb[:, ps]
                yd = jnp.where(left, _dot((_head_decay(cs, csT, causal, h0) * Gm).astype(bf16), xp),
                               _dot((_head_decay(cs, csT, causal, h0 + 1) * Gm).astype(bf16), xp))
                y_ref[:, ps] = yd + yo[:, pr * LANES:(pr + 1) * LANES] + dsk_ref[:, ps] * X[:, ps]
        zz = z_ref[...]
        gated = y_ref[...] * (zz * _sigmoid(zz))
        for g in range(G):
            gs = slice(g * gsz, (g + 1) * gsz)
            sg = gated[:, gs]
            rr = lax.rsqrt(jnp.mean(sg * sg, axis=-1, keepdims=True) + EPS)
            yn_ref[:, gs] = (sg * rr * on_ref[:, gs]).astype(yn_ref.dtype)

    vec = pl.BlockSpec((1, LANES), lambda c: (0, 0))
    wide = pl.BlockSpec((1, SW), lambda c: (0, 0))
    (y, yn, sp), got = _host_call(
        body, name, (nc,),
        [pl.BlockSpec((L, xbc.shape[1]), lambda c: (c, 0)), pl.BlockSpec((L, LANES), lambda c: (c, 0)),
         pl.BlockSpec((L, SW), lambda c: (c, 0)), vec, vec, wide, wide],
        [pl.BlockSpec((L, SW), lambda c: (c, 0)), pl.BlockSpec((L, SW), lambda c: (c, 0)),
         pl.BlockSpec((1, NS, SW), lambda c: (c, 0, 0))],
        [jax.ShapeDtypeStruct((T, SW), f32), jax.ShapeDtypeStruct((T, SW), bf16), jax.ShapeDtypeStruct((nc, NS, SW), f32)],
        [pltpu.VMEM((NS, SW), f32)], [xbc, dtraw, z, dt_bias, a_log, d_skip_x, out_norm], comm)
    return y, yn, sp, got


def _ssd_bwd(xbc, dtraw, sprev, dy, dt_bias, a_log, d_skip_x, HS, name, comm=None):
    T = xbc.shape[0]
    L, P, NS, G = SSD_CHUNK, SSD_HEAD_DIM, SSD_STATE, SSD_GROUPS
    SW, HPG, nc = HS * P, HS // SSD_GROUPS, T // SSD_CHUNK
    gw = HPG * P

    def body(*refs):
        ((xbc_ref, dtr_ref, sp_ref, dy_ref, bias_ref, alog_ref, dsk_ref),
         (dxbc_ref, ddtr_ref, dalog_ref, dbias_ref, dd_ref), (ds_ref,), cargs) = _hosted(comm, 7, 5, refs)
        if comm is not None:
            first, last = _first_last((nc,))

            @pl.when(first)
            def _():
                comm["start"](*cargs)

            @pl.when(last)
            def _():
                comm["finish"](*cargs)

        @pl.when(pl.program_id(0) == 0)
        def _():
            ds_ref[...] = jnp.zeros_like(ds_ref)
            dalog_ref[...] = jnp.zeros_like(dalog_ref)
            dbias_ref[...] = jnp.zeros_like(dbias_ref)
            dd_ref[...] = jnp.zeros_like(dd_ref)

        xs, dt, a, causal, cs, csT, dt_x, ecs_x, dte_x, cdec_x = _ssd_prep(dtr_ref, bias_ref, alog_ref, SW)
        lane = lax.broadcasted_iota(jnp.int32, (L, LANES), 1)
        sub = lax.broadcasted_iota(jnp.int32, (LANES, L), 0)
        left = lane < P
        dcs = jnp.zeros((L, LANES), f32)
        dcs_t = jnp.zeros((LANES, L), f32)
        xds = jnp.zeros((L, LANES), f32)
        dlast = jnp.zeros((1, LANES), f32)
        dD = jnp.zeros((1, LANES), f32)
        for g in range(G):
            gs = slice(g * gw, (g + 1) * gw)
            bsl = slice(SW + g * NS, SW + (g + 1) * NS)
            csl = slice(SW + (G + g) * NS, SW + (G + g + 1) * NS)
            Bb = xbc_ref[:, bsl].astype(bf16)
            Cb = xbc_ref[:, csl].astype(bf16)
            Gm = _dot(Cb, Bb, "nt")
            X = xbc_ref[:, gs]
            Xd = X * dt_x[:, gs]
            Xdb = Xd.astype(bf16)
            XE = Xd * dte_x[:, gs]
            dY = dy_ref[:, gs]
            dYb = dY.astype(bf16)
            Wb = (dY * ecs_x[:, gs]).astype(bf16)
            Sp = sp_ref[0, :, gs]
            Spb = Sp.astype(bf16)
            dS = ds_ref[:, gs]
            dSb = dS.astype(bf16)
            CS = _dot(Cb, Spb)
            Zb = _dot(Bb, dSb)
            dC = _dot(Wb, Spb, "nt")
            dB = _dot(XE.astype(bf16), dSb, "nt")
            ds_ref[:, gs] = cdec_x[:, gs] * dS + _dot(Cb, Wb, "tn")
            R1 = dY * CS * ecs_x[:, gs]
            R2 = XE * Zb
            to_head = (lax.shift_right_logical(lax.broadcasted_iota(jnp.int32, (gw, LANES), 0), HEAD_SHIFT) + g * HPG
                       == lax.broadcasted_iota(jnp.int32, (gw, LANES), 1)).astype(bf16)
            dcs = dcs + _dot_mask(R1 - R2, to_head, 3)
            dlast = (dlast + jnp.sum(_dot_mask(R2, to_head, 3), axis=0, keepdims=True)
                     + jnp.sum(_dot_mask(Sp * dS * cdec_x[:, gs], to_head, 3), axis=0, keepdims=True))
            dG = jnp.zeros((L, L), f32)
            pieces = []
            for pr in range(HPG // 2):
                h0 = g * HPG + 2 * pr
                pw = slice(pr * LANES, (pr + 1) * LANES)
                xp, dyp = Xdb[:, pw], dYb[:, pw]
                halves = []
                for k, h in enumerate((h0, h0 + 1)):
                    Lm = _head_decay(cs, csT, causal, h)
                    Mf = Lm * Gm
                    keep = left if k == 0 else jnp.logical_not(left)
                    dM = _dot(jnp.where(keep, dyp, jnp.zeros_like(dyp)), xp, "nt")
                    Q = dM * Mf
                    dcs = dcs + jnp.where(lane == h, jnp.sum(Q, axis=1, keepdims=True), 0.0)
                    dcs_t = dcs_t - jnp.where(sub == h, jnp.sum(Q, axis=0, keepdims=True), 0.0)
                    dG = dG + dM * Lm
                    halves.append(_dot(Mf.astype(bf16), dyp, "tn"))
                pieces.append(jnp.where(left, halves[0], halves[1]))
            dXd = jnp.concatenate(pieces, axis=1) + dte_x[:, gs] * Zb
            dxbc_ref[:, gs] = dXd * dt_x[:, gs] + dsk_ref[:, gs] * dY
            xds = xds + _dot_mask(dXd * X, to_head, 3)
            dD = dD + jnp.sum(_dot_mask(dY * X, to_head, 3), axis=0, keepdims=True)
            dGb = dG.astype(bf16)
            dxbc_ref[:, bsl] = dB + _dot(dGb, Cb, "tn")
            dxbc_ref[:, csl] = dC + _dot(dGb, Bb)
        rowi = lax.broadcasted_iota(jnp.int32, (L, LANES), 0)
        dcs = dcs + dcs_t.T + jnp.where(rowi == L - 1, dlast, 0.0)
        anti = (lax.broadcasted_iota(jnp.int32, (L, L), 1) >= lax.broadcasted_iota(jnp.int32, (L, L), 0)).astype(bf16)
        dda = _mask_dot(anti, dcs, 3)
        ddt = dda * a + xds
        dalog_ref[...] += jnp.sum(dda * dt, axis=0, keepdims=True) * a
        ddtr = ddt * _sigmoid(xs)
        ddtr_ref[...] = ddtr
        dbias_ref[...] += jnp.sum(ddtr, axis=0, keepdims=True)
        dd_ref[...] += dD

    rev = lambda c: (nc - 1 - c, 0)
    vec = pl.BlockSpec((1, LANES), lambda c: (0, 0))
    outs, got = _host_call(
        body, name, (nc,),
        [pl.BlockSpec((L, xbc.shape[1]), rev), pl.BlockSpec((L, LANES), rev),
         pl.BlockSpec((1, NS, SW), lambda c: (nc - 1 - c, 0, 0)), pl.BlockSpec((L, SW), rev), vec, vec,
         pl.BlockSpec((1, SW), lambda c: (0, 0))],
        [pl.BlockSpec((L, xbc.shape[1]), rev), pl.BlockSpec((L, LANES), rev), vec, vec, vec],
        [jax.ShapeDtypeStruct(xbc.shape, f32), jax.ShapeDtypeStruct((T, LANES), f32)] + [jax.ShapeDtypeStruct((1, LANES), f32)] * 3,
        [pltpu.VMEM((NS, SW), f32)], [xbc, dtraw, sprev, dy, dt_bias, a_log, d_skip_x], comm)
    return (*outs, got)


def _gate_bwd(y, z, dyn, out_norm, name, tr=256):
    T, SW = y.shape
    tr = min(tr, T)
    gsz = SW // SSD_GROUPS

    def body(y_ref, z_ref, d_ref, on_ref, dy_ref, dz_ref, don_ref):
        @pl.when(pl.program_id(0) == 0)
        def _():
            don_ref[...] = jnp.zeros_like(don_ref)

        for g in range(SSD_GROUPS):
            gs = slice(g * gsz, (g + 1) * gsz)
            yv, zv, dv = y_ref[:, gs], z_ref[:, gs], d_ref[:, gs]
            sg = _sigmoid(zv)
            sl = zv * sg
            gated = yv * sl
            rr = lax.rsqrt(jnp.mean(gated * gated, axis=-1, keepdims=True) + EPS)
            gh = gated * rr
            dgn = dv * on_ref[:, gs]
            dgated = rr * (dgn - gh * jnp.mean(dgn * gh, axis=-1, keepdims=True))
            dy_ref[:, gs] = dgated * sl
            dz_ref[:, gs] = (dgated * yv * (sg * (1.0 + zv * (1.0 - sg)))).astype(dz_ref.dtype)
            don_ref[:, gs] += jnp.sum(dv * gh, axis=0, keepdims=True)

    row = pl.BlockSpec((tr, SW), lambda r: (r, 0))
    vec = pl.BlockSpec((1, SW), lambda r: (0, 0))
    return pl.pallas_call(
        body, name=name, grid=(T // tr,), in_specs=[row, row, row, vec], out_specs=[row, row, vec],
        out_shape=[jax.ShapeDtypeStruct((T, SW), f32), jax.ShapeDtypeStruct((T, SW), bf16),
                   jax.ShapeDtypeStruct((1, SW), f32)],
        compiler_params=_cp("arbitrary"),
    )(y, z, dyn, out_norm)


def _qk_norm_fwd(qkv, qn_w, kn_w, SBW, name, tr=256):
    T = qkv.shape[0]
    tr = min(tr, T)
    nh = SBW // SB_HEAD_DIM

    def body(q_ref, k_ref, v_ref, qw_ref, kw_ref, qo_ref, ko_ref, vo_ref):
        for src, w_ref, dst in ((q_ref, qw_ref, qo_ref), (k_ref, kw_ref, ko_ref)):
            for h in range(nh):
                hs = slice(h * SB_HEAD_DIM, (h + 1) * SB_HEAD_DIM)
                sv = src[:, hs]
                rr = lax.rsqrt(jnp.mean(sv * sv, axis=-1, keepdims=True) + EPS)
                dst[:, hs] = (sv * rr * w_ref[...]).astype(dst.dtype)
        vo_ref[...] = v_ref[...].astype(vo_ref.dtype)

    blk = lambda j: pl.BlockSpec((tr, SBW), lambda r: (r, j))
    vec = pl.BlockSpec((1, SB_HEAD_DIM), lambda r: (0, 0))
    out = pl.BlockSpec((tr, SBW), lambda r: (r, 0))
    return pl.pallas_call(
        body, name=name, grid=(T // tr,), in_specs=[blk(0), blk(1), blk(2), vec, vec], out_specs=[out, out, out],
        out_shape=[jax.ShapeDtypeStruct((T, SBW), bf16)] * 3, compiler_params=_cp("parallel"),
    )(qkv, qkv, qkv, qn_w, kn_w)


def _qk_norm_bwd(qkv, dqn, dkn, dv, qn_w, kn_w, SBW, name, tr=256):
    T = qkv.shape[0]
    tr = min(tr, T)
    nh = SBW // SB_HEAD_DIM

    def body(q_ref, k_ref, dq_ref, dk_ref, dv_ref, qw_ref, kw_ref, o_ref, dqw_ref, dkw_ref):
        @pl.when(pl.program_id(0) == 0)
        def _():
            dqw_ref[...] = jnp.zeros_like(dqw_ref)
            dkw_ref[...] = jnp.zeros_like(dkw_ref)

        for part, (src, d_ref, w_ref, dw_ref) in enumerate(((q_ref, dq_ref, qw_ref, dqw_ref), (k_ref, dk_ref, kw_ref, dkw_ref))):
            dw = jnp.zeros((1, SB_HEAD_DIM), f32)
            for h in range(nh):
                hs = slice(h * SB_HEAD_DIM, (h + 1) * SB_HEAD_DIM)
                os_ = slice(part * SBW + h * SB_HEAD_DIM, part * SBW + (h + 1) * SB_HEAD_DIM)
                sv, dn = src[:, hs], d_ref[:, hs]
                rr = lax.rsqrt(jnp.mean(sv * sv, axis=-1, keepdims=True) + EPS)
                xh = sv * rr
                dg = dn * w_ref[...]
                o_ref[:, os_] = (rr * (dg - xh * jnp.mean(dg * xh, axis=-1, keepdims=True))).astype(o_ref.dtype)
                dw = dw + jnp.sum(dn * xh, axis=0, keepdims=True)
            dw_ref[...] += dw
        o_ref[:, 2 * SBW:] = dv_ref[...].astype(o_ref.dtype)

    blk = lambda j: pl.BlockSpec((tr, SBW), lambda r: (r, j))
    vec = pl.BlockSpec((1, SB_HEAD_DIM), lambda r: (0, 0))
    row = pl.BlockSpec((tr, SBW), lambda r: (r, 0))
    return pl.pallas_call(
        body, name=name, grid=(T // tr,), in_specs=[blk(0), blk(1), row, row, row, vec, vec],
        out_specs=[pl.BlockSpec((tr, 3 * SBW), lambda r: (r, 0)), vec, vec],
        out_shape=[jax.ShapeDtypeStruct((T, 3 * SBW), bf16)] + [jax.ShapeDtypeStruct((1, SB_HEAD_DIM), f32)] * 2,
        compiler_params=_cp("arbitrary"),
    )(qkv, qkv, dqn, dkn, dv, qn_w, kn_w)


def _sb_logits(q, kb, scale):
    zl = _dot(q, kb, "nt") * scale
    lb = jnp.minimum(zl, 0.0) - jnp.log(1.0 + jnp.exp(-jnp.abs(zl)))
    return zl, lb, lb - zl


def _tail_update(old, r0, new_tail):
    return new_tail if r0 == 0 else jnp.concatenate([old[:r0], new_tail], axis=0)


def _hosted(comm, n_in, n_out, refs):
    nci, nco, ncs = (len(comm["ins"]), len(comm["outs"]), len(comm["sems"])) if comm is not None else (0, 0, 0)
    ins, outs = refs[:n_in], refs[n_in + nci:n_in + nci + n_out]
    scratch = refs[n_in + nci + n_out + nco:len(refs) - ncs]
    cargs = (refs[n_in:n_in + nci], refs[n_in + nci + n_out:n_in + nci + n_out + nco], refs[len(refs) - ncs:])
    return ins, outs, scratch, cargs


def _host_call(body, name, grid, in_specs, out_specs, out_shape, scratch, operands, comm):
    n_out = len(out_shape)
    in_specs, out_specs, out_shape, scratch, operands = list(in_specs), list(out_specs), list(out_shape), list(scratch), list(operands)
    io = {}
    if comm is not None:
        for src, dst in comm.get("aliases", {}).items():
            io[len(operands) + src] = n_out + dst
        in_specs += [ANY] * len(comm["ins"])
        operands += comm["ins"]
        out_specs += [ANY] * len(comm["outs"])
        out_shape += comm["outs"]
        scratch += comm["sems"]
    res = pl.pallas_call(body, name=name, grid=grid, in_specs=in_specs, out_specs=out_specs, out_shape=out_shape,
                         scratch_shapes=scratch, input_output_aliases=io,
                         compiler_params=_cp(*(("arbitrary",) * len(grid))))(*operands)
    return res[:n_out], res[n_out:]


def _first_last(grid):
    pid = [pl.program_id(d) for d in range(len(grid))]
    first, last = pid[0] == 0, pid[0] == grid[0] - 1
    for d in range(1, len(grid)):
        first, last = first & (pid[d] == 0), last & (pid[d] == grid[d] - 1)
    return first, last


def _sb_fwd(qn, kn, vb, name, comm=None, tq=4096, tk=256):
    T, W = qn.shape
    tq = min(tq, T)
    tk = min(tk, tq)
    nh, nq, dh, nd = W // SB_HEAD_DIM, T // tq, SB_HEAD_DIM, tq // tk
    scale = dh ** -0.5
    grid = (nh, nq)

    def body(*refs):
        (q_ref, k_ref, v_ref), (o_ref, c_ref), _, cargs = _hosted(comm, 3, 2, refs)
        first, last = _first_last(grid)
        if comm is not None:
            @pl.when(first)
            def _():
                comm["start"](*cargs)

        qi = pl.program_id(1)
        q = q_ref[...]
        later = (lax.broadcasted_iota(jnp.int32, (tk, tk), 0) > lax.broadcasted_iota(jnp.int32, (tk, tk), 1)).astype(bf16)

        def step(j, carry, d):
            acc, run = carry
            r0 = 0 if d is None else d * tk
            ks = pl.multiple_of(j * tk, tk)
            kb, vv = k_ref[pl.ds(ks, tk), :], v_ref[pl.ds(ks, tk), :]
            _, lb, lk = _sb_logits(q[r0:], kb, scale)
            if d is not None:
                mask = lax.broadcasted_iota(jnp.int32, lk.shape, 1) < lax.broadcasted_iota(jnp.int32, lk.shape, 0)
                lk = jnp.where(mask, lk, 0.0)
            between = _dot_mask(lk, later, 2)
            w = jnp.exp(lb + between + run[r0:])
            if d is not None:
                w = jnp.where(mask, w, 0.0)
            return (_tail_update(acc, r0, acc[r0:] + _dot(w.astype(bf16), vv)),
                    _tail_update(run, r0, run[r0:] + between[:, 0:1] + lk[:, 0:1]))

        carry = (jnp.zeros((tq, dh), f32), jnp.zeros((tq, 1), f32))
        for d in range(nd - 1, -1, -1):
            carry = step(qi * nd + d, carry, d)
        n_before = qi * nd
        acc, run = lax.fori_loop(0, n_before, lambda t, c: step(n_before - 1 - t, c, None), carry)
        o_ref[...] = acc.astype(o_ref.dtype)
        c_ref[...] = jnp.broadcast_to(run, (tq, dh))
        if comm is not None:
            @pl.when(last)
            def _():
                comm["finish"](*cargs)

    qblk = pl.BlockSpec((tq, dh), lambda h, i: (i, h))
    full = pl.BlockSpec((T, dh), lambda h, i: (0, h))
    (o, c), got = _host_call(body, name, grid, [qblk, full, full], [qblk, qblk],
                             [jax.ShapeDtypeStruct((T, W), bf16), jax.ShapeDtypeStruct((T, W), f32)], [], [qn, kn, vb], comm)
    return o, c, got


def _sb_bwd(qn, kn, vb, do, ctot, do_off, name, comm=None, tq=4096, tk=256):
    T, W = qn.shape
    tq = min(tq, T)
    tk = min(tk, tq)
    nh, nq, dh, nd = W // SB_HEAD_DIM, T // tq, SB_HEAD_DIM, tq // tk
    scale = dh ** -0.5
    ob = do_off // dh
    grid = (nh, nq)

    def body(*refs):
        (q_ref, k_ref, v_ref, do_ref, c_ref), (dq_ref, dk_ref, dv_ref), _, cargs = _hosted(comm, 5, 3, refs)
        first, last = _first_last(grid)
        if comm is not None:
            @pl.when(first)
            def _():
                comm["start"](*cargs)

        qi = pl.program_id(1)

        @pl.when(qi == 0)
        def _():
            dk_ref[...] = jnp.zeros_like(dk_ref)
            dv_ref[...] = jnp.zeros_like(dv_ref)

        q = q_ref[...]
        dob = do_ref[...].astype(bf16)
        total = c_ref[:, 0:1]
        r2 = lax.broadcasted_iota(jnp.int32, (tk, tk), 0)
        c2 = lax.broadcasted_iota(jnp.int32, (tk, tk), 1)
        upto = (r2 <= c2).astype(bf16)
        before = (r2 < c2).astype(bf16)

        def step(j, carry, d):
            dq, pre, gpre = carry
            r0 = 0 if d is None else d * tk
            ks = pl.multiple_of(j * tk, tk)
            kb, vv = k_ref[pl.ds(ks, tk), :], v_ref[pl.ds(ks, tk), :]
            qs, dos = q[r0:], dob[r0:]
            _, lb, lk = _sb_logits(qs, kb, scale)
            if d is not None:
                mask = lax.broadcasted_iota(jnp.int32, lk.shape, 1) < lax.broadcasted_iota(jnp.int32, lk.shape, 0)
                lk = jnp.where(mask, lk, 0.0)
            pin = _dot_mask(lk, upto, 2)
            w = jnp.exp(lb + (total[r0:] - pre[r0:] - pin))
            if d is not None:
                w = jnp.where(mask, w, 0.0)
            dw = _dot(dos, vv, "nt")
            dv_ref[pl.ds(ks, tk), :] += _dot(w.astype(bf16), dos, "tn")
            gg = dw * w
            gex = _dot(gg.astype(bf16), before)
            beta = jnp.exp(lb)
            dz = (gg * (1.0 - beta) - (gpre[r0:] + gex) * beta) * scale
            if d is not None:
                dz = jnp.where(mask, dz, 0.0)
            dzb = dz.astype(bf16)
            dk_ref[pl.ds(ks, tk), :] += _dot(dzb, qs, "tn")
            return (_tail_update(dq, r0, dq[r0:] + _dot(dzb, kb)),
                    _tail_update(pre, r0, pre[r0:] + pin[:, tk - 1:tk]),
                    _tail_update(gpre, r0, gpre[r0:] + gex[:, tk - 1:tk] + gg[:, tk - 1:tk]))

        init = (jnp.zeros((tq, dh), f32), jnp.zeros((tq, 1), f32), jnp.zeros((tq, 1), f32))
        carry = lax.fori_loop(0, qi * nd, lambda t, c: step(t, c, None), init)
        for d in range(nd):
            carry = step(qi * nd + d, carry, d)
        dq_ref[...] = carry[0]
        if comm is not None:
            @pl.when(last)
            def _():
                comm["finish"](*cargs)

    qblk = pl.BlockSpec((tq, dh), lambda h, i: (i, h))
    full = pl.BlockSpec((T, dh), lambda h, i: (0, h))
    (dq, dk, dv), got = _host_call(
        body, name, grid, [qblk, full, full, pl.BlockSpec((tq, dh), lambda h, i: (i, h + ob)), qblk], [qblk, full, full],
        [jax.ShapeDtypeStruct((T, W), f32)] * 3, [], [qn, kn, vb, do, ctot], comm)
    return dq, dk, dv, got


def _pool_select(sums, g):
    return jnp.where(g == 0, sums[0], jnp.where(g == 1, sums[1], jnp.where(g == 2, sums[2], sums[3])))


def _pool_count(g, r, rc, n, cols, off=0):
    t = (r * rc + off + lax.broadcasted_iota(jnp.int32, (n, cols), 0)).astype(f32)
    win = jnp.left_shift(2, g).astype(f32)
    return jnp.minimum(t + 1.0, win)


def _pool_fwd(hp, xres, w, b, scale, name, rc=512, comm=None):
    T, D = hp.shape
    rc = min(rc, T)
    pg = D // len(POOL_WINDOWS)
    grid = (len(POOL_WINDOWS), T // rc)

    def body(*refs):
        (h_ref, p_ref, x_ref, w_ref, b_ref, s_ref), (o_ref, yp_ref, d_ref), _, cargs = _hosted(comm, 6, 3, refs)
        if comm is not None:
            first, last = _first_last(grid)

            @pl.when(first)
            def _():
                comm["start"](*cargs)

            @pl.when(last)
            def _():
                comm["finish"](*cargs)

        g, r = pl.program_id(0), pl.program_id(1)
        cur = h_ref[...]
        halo = jnp.where(r > 0, p_ref[...], 0.0)
        ext = jnp.concatenate([halo, cur], axis=0)
        sums, s = [], ext
        for sh in (1, 2, 4, 8):
            s = s + pltpu.roll(s, sh, 0)
            sums.append(s)
        d = _pool_select(sums, g)[POOL_HALO:] / _pool_count(g, r, rc, rc, pg) - cur
        yp = _dot(d.astype(bf16), w_ref[0]) + b_ref[...]
        yp_ref[...] = yp
        d_ref[...] = d.astype(d_ref.dtype)
        o_ref[...] = x_ref[...] + yp * s_ref[...]

    cur = pl.BlockSpec((rc, pg), lambda g, r: (r, g))
    prev = pl.BlockSpec((POOL_HALO, pg), lambda g, r: (jnp.maximum(r * (rc // POOL_HALO) - 1, 0), g))
    vec = pl.BlockSpec((1, pg), lambda g, r: (0, g))
    (o, yp, d), got = _host_call(
        body, name, grid, [cur, prev, cur, pl.BlockSpec((1, pg, pg), lambda g, r: (g, 0, 0)), vec, vec], [cur, cur, cur],
        [jax.ShapeDtypeStruct((T, D), f32), jax.ShapeDtypeStruct((T, D), f32), jax.ShapeDtypeStruct((T, D), bf16)],
        [], [hp, hp, xres, w, b, scale], comm)
    return o, yp, d, got


def _pool_bwd(dx, yp, d, w, scale, name, rc=512):
    T, D = dx.shape
    rc = min(rc, T)
    pg = D // len(POOL_WINDOWS)
    nr = T // rc

    def body(dx_ref, dn_ref, yp_ref, d_ref, w_ref, s_ref, dh_ref, dw_ref, db_ref, dsc_ref):
        g, r = pl.program_id(0), pl.program_id(1)

        @pl.when(r == 0)
        def _():
            dw_ref[...] = jnp.zeros_like(dw_ref)
            db_ref[...] = jnp.zeros_like(db_ref)
            dsc_ref[...] = jnp.zeros_like(dsc_ref)

        dxv = dx_ref[...]
        dyp = dxv * s_ref[...]
        dsc_ref[...] += jnp.sum(dxv * yp_ref[...], axis=0, keepdims=True)
        db_ref[...] += jnp.sum(dyp, axis=0, keepdims=True)
        dypb = dyp.astype(bf16)
        dw_ref[0] += _dot(d_ref[...], dypb, "tn")
        dd = _dot(dypb, w_ref[0], "nt")
        ddn = _dot((dn_ref[...] * s_ref[...]).astype(bf16), w_ref[0], "nt")
        e = dd / _pool_count(g, r, rc, rc, pg)
        en = jnp.where(r < nr - 1, ddn / _pool_count(g, r, rc, POOL_HALO, pg, off=rc), 0.0)
        ext = jnp.concatenate([e, en], axis=0)
        sums, s = [], ext
        for sh in (1, 2, 4, 8):
            s = s + pltpu.roll(s, rc + POOL_HALO - sh, 0)
            sums.append(s)
        dh_ref[...] = _pool_select(sums, g)[:rc] - dd

    cur = pl.BlockSpec((rc, pg), lambda g, r: (r, g))
    nxt = pl.BlockSpec((POOL_HALO, pg), lambda g, r: (jnp.minimum((r + 1) * (rc // POOL_HALO), T // POOL_HALO - 1), g))
    vec = pl.BlockSpec((1, pg), lambda g, r: (0, g))
    wsp = pl.BlockSpec((1, pg, pg), lambda g, r: (g, 0, 0))
    return pl.pallas_call(
        body, name=name, grid=(len(POOL_WINDOWS), nr), in_specs=[cur, nxt, cur, cur, wsp, vec],
        out_specs=[cur, wsp, vec, vec],
        out_shape=[jax.ShapeDtypeStruct((T, D), f32), jax.ShapeDtypeStruct(w.shape, f32),
                   jax.ShapeDtypeStruct((1, D), f32), jax.ShapeDtypeStruct((1, D), f32)],
        compiler_params=_cp("parallel", "arbitrary"),
    )(dx, dx, yp, d, w, scale)


def _adamw(w, g, m, v, name, tr=256, comm=None):
    R, C = w.shape
    tr = min(tr, R)
    lanes = -(-C // LANES) * LANES
    while tr > 8 and 2 * 8 * tr * lanes * 4 > MM_TILE_BUDGET:
        tr //= 2
    assert R % tr == 0
    grid = (R // tr,)

    def body(*refs):
        (w_ref, g_ref, m_ref, v_ref), (d_ref, mo_ref, vo_ref, go_ref), _, cargs = _hosted(comm, 4, 4, refs)
        if comm is not None:
            first, last = _first_last(grid)

            @pl.when(first)
            def _():
                comm["start"](*cargs)

            @pl.when(last)
            def _():
                comm["finish"](*cargs)

        gv = g_ref[...]
        mn = ADAM_B1 * m_ref[...] + (1.0 - ADAM_B1) * gv
        vn = ADAM_B2 * v_ref[...] + (1.0 - ADAM_B2) * (gv * gv)
        m_hat = mn / (1.0 - ADAM_B1 ** ADAM_STEP)
        v_hat = vn / (1.0 - ADAM_B2 ** ADAM_STEP)
        d_ref[...] = -ADAM_LR * (m_hat / (jnp.sqrt(v_hat) + ADAM_EPS) + ADAM_WD * w_ref[...])
        mo_ref[...] = mn
        vo_ref[...] = vn
        go_ref[...] = gv

    blk = pl.BlockSpec((tr, C), lambda r: (r, 0))
    outs, got = _host_call(body, name, grid, [blk] * 4, [blk] * 4, [jax.ShapeDtypeStruct((R, C), f32)] * 4, [], [w, g, m, v], comm)
    return (*outs, got)


def _pair_sum(g4, recv, name, br=256):
    _, R, C = g4.shape
    hr = R // 2
    br = min(br, hr)
    nb = hr // br

    def body(a_ref, b_ref, o_ref):
        o_ref[...] = (a_ref[...] + b_ref[...]).astype(o_ref.dtype)

    out = pl.BlockSpec((1, br, C), lambda s, i: (s, i, 0))
    return pl.pallas_call(
        body, name=name, grid=(N_CHIPS, nb),
        in_specs=[pl.BlockSpec((1, br, C), lambda s, i: (s, lax.axis_index("c") * nb + i, 0)), out], out_specs=out,
        out_shape=jax.ShapeDtypeStruct((N_CHIPS, hr, C), bf16), compiler_params=_cp("parallel", "parallel"),
    )(g4, recv)


def _chip_sum(g4, recv, pieces, name, br=256):
    _, hr, C = recv.shape
    br = min(br, hr)
    nb = hr // br
    chip = lambda: 2 * lax.axis_index("x") + lax.axis_index("y")

    def body(a_ref, r_ref, b1_ref, b2_ref, b3_ref, o_ref):
        o_ref[...] = (((a_ref[0] + r_ref[0]) + b1_ref[0].astype(f32)) + b2_ref[0].astype(f32)) + b3_ref[0].astype(f32)

    other = lambda k: pl.BlockSpec((1, br, C), lambda i: ((chip() + k) % N_CHIPS, i, 0))
    return pl.pallas_call(
        body, name=name, grid=(nb,),
        in_specs=[pl.BlockSpec((1, br, C), lambda i: (chip(), lax.axis_index("c") * nb + i, 0)),
                  pl.BlockSpec((1, br, C), lambda i: (chip(), i, 0)), other(1), other(2), other(3)],
        out_specs=pl.BlockSpec((br, C), lambda i: (lax.axis_index("c") * nb + i, 0)),
        out_shape=jax.ShapeDtypeStruct((2 * hr, C), f32), compiler_params=_cp("parallel"),
    )(g4, recv, pieces, pieces, pieces)


ANY = pl.BlockSpec(memory_space=pl.ANY)


def _mesh_pos():
    x, y, c = lax.axis_index("x"), lax.axis_index("y"), lax.axis_index("c")
    others = [(1 - x, y), (x, 1 - y), (1 - x, 1 - y)]
    return x, y, c, 2 * x + y, others


def _gather_small(blk, name):
    m, n = blk.shape

    def body(x_ref, out_ref, sum_ref, send_sems, recv_sems, local_sem):
        x, y, c, _, others = _mesh_pos()
        me, sibling = (x, y, c), (x, y, 1 - c)

        def rows(px, py, pc):
            return out_ref.at[pl.ds((4 * px + 2 * py + pc) * m, m), :]

        def copy(k, block, to, src=None):
            return pltpu.make_async_remote_copy(
                src_ref=rows(*block) if src is None else src, dst_ref=rows(*block),
                send_sem=send_sems.at[k], recv_sem=recv_sems.at[k], device_id=to, device_id_type=MESH)

        mine = pltpu.make_async_copy(x_ref, rows(*me), local_sem)
        mine.start()
        first = [copy(0, me, sibling, src=x_ref)]
        first += [copy(1 + j, me, (*chip, c), src=x_ref) for j, chip in enumerate(others)]
        for cp in first:
            cp.start()
        passed = [copy(4 + j, (*chip, c), sibling) for j, chip in enumerate(others)]
        for j, chip in enumerate(others):
            copy(1 + j, (*chip, c), me).wait_recv()
            passed[j].start()
        copy(0, sibling, me).wait_recv()
        for j, chip in enumerate(others):
            copy(4 + j, (*chip, 1 - c), me).wait_recv()
        for cp in first + passed:
            cp.wait_send()
        mine.wait()
        acc = out_ref[0:m, :]
        for d in range(1, 8):
            acc = acc + out_ref[d * m:(d + 1) * m, :]
        sum_ref[...] = acc

    vm = pl.BlockSpec(memory_space=pltpu.VMEM)
    return pl.pallas_call(
        body, name=name, in_specs=[vm], out_specs=[vm, vm],
        out_shape=[jax.ShapeDtypeStruct((8 * m, n), f32), jax.ShapeDtypeStruct((m, n), f32)],
        scratch_shapes=[pltpu.SemaphoreType.DMA((7,)), pltpu.SemaphoreType.DMA((7,)), pltpu.SemaphoreType.DMA],
    )(blk)


def _copy(src, dst, sems, idx, to):
    return pltpu.make_async_remote_copy(src_ref=src, dst_ref=dst, send_sem=sems[0].at[idx], recv_sem=sems[1].at[idx],
                                        device_id=to, device_id_type=MESH)


def _gather_ici(shards):
    nt = len(shards)

    def copies(ins, outs, sems):
        x, y, c, chip, others = _mesh_pos()
        send, land = [], []
        for t in range(nt):
            hr = ins[t].shape[0] // 2
            for j, (px, py) in enumerate(others):
                send.append((ins[t].at[pl.ds(c * hr, hr)], outs[t].at[chip, pl.ds(c * hr, hr)], sems, (t, j), (px, py, c)))
                piece = outs[t].at[2 * px + py, pl.ds(c * hr, hr)]
                land.append((piece, piece, sems, (t, j), (px, py, c)))
        return send, land

    def start(ins, outs, sems):
        for args in copies(ins, outs, sems)[0]:
            _copy(*args).start()

    def finish(ins, outs, sems):
        send, land = copies(ins, outs, sems)
        for args in land:
            _copy(*args).wait_recv()
        for args in send:
            _copy(*args).wait_send()

    return dict(ins=list(shards), outs=[jax.ShapeDtypeStruct((N_CHIPS,) + s.shape, s.dtype) for s in shards],
                sems=[pltpu.SemaphoreType.DMA((nt, 3)), pltpu.SemaphoreType.DMA((nt, 3))], start=start, finish=finish)


def _gather_d2d(stacks):
    nt = len(stacks)

    def copies(ins, outs, sems):
        x, y, c, _, others = _mesh_pos()
        send, land = [], []
        for t in range(nt):
            hr = outs[t].shape[1] // 2
            for j, (px, py) in enumerate(others):
                mine = outs[t].at[2 * px + py, pl.ds(c * hr, hr)]
                theirs = outs[t].at[2 * px + py, pl.ds((1 - c) * hr, hr)]
                send.append((mine, mine, sems, (t, j), (x, y, 1 - c)))
                land.append((theirs, theirs, sems, (t, j), (x, y, 1 - c)))
        return send, land

    def start(ins, outs, sems):
        for args in copies(ins, outs, sems)[0]:
            _copy(*args).start()

    def finish(ins, outs, sems):
        send, land = copies(ins, outs, sems)
        for args in land:
            _copy(*args).wait_recv()
        for args in send:
            _copy(*args).wait_send()

    return dict(ins=list(stacks), outs=[jax.ShapeDtypeStruct(s.shape, s.dtype) for s in stacks],
                sems=[pltpu.SemaphoreType.DMA((nt, 3)), pltpu.SemaphoreType.DMA((nt, 3))], start=start, finish=finish,
                aliases={t: t for t in range(nt)})


def _run_exchange(comm, name):
    ni, no = len(comm["ins"]), len(comm["outs"])

    def body(*refs):
        args = (refs[:ni], refs[ni:ni + no], refs[ni + no:])
        comm["start"](*args)
        comm["finish"](*args)

    return pl.pallas_call(
        body, name=name, in_specs=[ANY] * ni, out_specs=[ANY] * no, out_shape=comm["outs"], scratch_shapes=comm["sems"],
        input_output_aliases=dict(comm.get("aliases", {})))(*comm["ins"])


def _swap_halves(g4s):
    nt = len(g4s)

    def copies(ins, outs, sems):
        x, y, c, _, _ = _mesh_pos()
        both = []
        for t in range(nt):
            hr = ins[t].shape[1] // 2
            both.append((ins[t].at[:, pl.ds((1 - c) * hr, hr)], outs[t], sems, t, (x, y, 1 - c)))
        return both, both

    def start(ins, outs, sems):
        for args in copies(ins, outs, sems)[0]:
            _copy(*args).start()

    def finish(ins, outs, sems):
        send, land = copies(ins, outs, sems)
        for args in land:
            _copy(*args).wait_recv()
        for args in send:
            _copy(*args).wait_send()

    return dict(ins=list(g4s), outs=[jax.ShapeDtypeStruct((N_CHIPS, g.shape[1] // 2, g.shape[2]), g.dtype) for g in g4s],
                sems=[pltpu.SemaphoreType.DMA((nt,)), pltpu.SemaphoreType.DMA((nt,))], start=start, finish=finish)


def _exchange_chips(h4s):
    nt = len(h4s)

    def copies(ins, outs, sems):
        x, y, c, chip, others = _mesh_pos()
        send, land = [], []
        for t in range(nt):
            for j, (px, py) in enumerate(others):
                send.append((ins[t].at[2 * px + py], outs[t].at[chip], sems, (t, j), (px, py, c)))
                landed = outs[t].at[2 * px + py]
                land.append((landed, landed, sems, (t, j), (px, py, c)))
        return send, land

    def start(ins, outs, sems):
        for args in copies(ins, outs, sems)[0]:
            _copy(*args).start()

    def finish(ins, outs, sems):
        send, land = copies(ins, outs, sems)
        for args in land:
            _copy(*args).wait_recv()
        for args in send:
            _copy(*args).wait_send()

    return dict(ins=list(h4s), outs=[jax.ShapeDtypeStruct(h.shape, h.dtype) for h in h4s],
                sems=[pltpu.SemaphoreType.DMA((nt, 3)), pltpu.SemaphoreType.DMA((nt, 3))], start=start, finish=finish)


def _join_halves(fs):
    nt = len(fs)

    def copies(ins, outs, sems):
        x, y, c, _, _ = _mesh_pos()
        send, land = [], []
        for t in range(nt):
            hr = outs[t].shape[0] // 2
            mine, theirs = outs[t].at[pl.ds(c * hr, hr)], outs[t].at[pl.ds((1 - c) * hr, hr)]
            send.append((mine, mine, sems, t, (x, y, 1 - c)))
            land.append((theirs, theirs, sems, t, (x, y, 1 - c)))
        return send, land

    def start(ins, outs, sems):
        for args in copies(ins, outs, sems)[0]:
            _copy(*args).start()

    def finish(ins, outs, sems):
        send, land = copies(ins, outs, sems)
        for args in land:
            _copy(*args).wait_recv()
        for args in send:
            _copy(*args).wait_send()

    return dict(ins=list(fs), outs=[jax.ShapeDtypeStruct(f.shape, f.dtype) for f in fs],
                sems=[pltpu.SemaphoreType.DMA((nt,)), pltpu.SemaphoreType.DMA((nt,))], start=start, finish=finish,
                aliases={t: t for t in range(nt)})


def _pad_lanes(v, n=LANES):
    return jnp.pad(v, ((0, 0), (0, n - v.shape[-1])))


def _mlp_fwd(xin, norm_g, w_up, w_down, F, tag, comms=(None, None)):
    T, D = xin.shape
    h = _rms_fwd(xin, norm_g, bf16, f"{tag}_norm")

    def relu_sq(acc):
        r = jnp.maximum(acc, 0.0)
        return r, r * r

    got = [None, None]
    ua = _mm(h, w_up[0], "nn", T, F, D, (bf16, bf16), f"{tag}_up", epilogue=relu_sq, b_view=w_up[1], comm=comms[0])
    (u, a), got[0] = ua if comms[0] is not None else (ua, None)
    out = _mm(a, w_down[0], "nn", T, D, F, (f32,), f"{tag}_down", epilogue=lambda acc, res: (res + acc,),
              extras=((xin, "tile"),), b_view=w_down[1], comm=comms[1])
    (out,), got[1] = out if comms[1] is not None else ((out,), None)
    return out, (xin, h, u, a), got


def _mlp_bwd(dy, saved, norm_g, w_up, w_down, F, tag, up_to=None, down_to=None, host=None):
    xin, h, u, a = saved
    T, D = xin.shape
    to = lambda t: {} if t is None else dict(out_view=t[0], out_stack=t[1], alias=t[2])
    du = _mm(dy, w_down[0], "nt", T, F, D, (bf16,), f"{tag}_dact", epilogue=lambda acc, uu: (acc * (2.0 * uu.astype(f32)),),
             extras=((u, "tile"),), b_view=w_down[1])
    dw_down = _mm(a, dy, "tn", F, D, T, (f32,), f"{tag}_dwdown", **to(down_to))
    dw_up = _mm(h, du, "tn", D, F, T, (f32,), f"{tag}_dwup", **to(up_to))
    dh = _mm(du, w_up[0], "nt", T, D, F, (f32,), f"{tag}_dh", b_view=w_up[1], comm=host(dw_up, dw_down) if host else None)
    (dh,), got = dh if host else ((dh,), None)
    dx, dg = _rms_bwd(xin, norm_g, dh, dy, f"{tag}_dnorm")
    return dx, dg, dw_up, dw_down, got


def _local_step(xc, tgt, W, HS, SBW, net=None):
    T, D = xc.shape
    SW = HS * SSD_HEAD_DIM
    CD = W["conv_b"].shape[-1]
    mlp_norm = W["mlp_norm"]
    add = lambda acc, prev: (prev + acc,)

    h0 = _rms_fwd(xc, W["hyb_norm"], bf16, "hyb_norm")
    z = _mm(h0, W["w_z"], "nn", T, SW, D, (f32,), "proj_z")
    xraw = _mm(h0, W["w_xbc"], "nn", T, CD, D, (f32,), "proj_xbc")
    dtraw = _mm(h0, W["w_dt"], "nn", T, LANES, D, (f32,), "proj_dt")
    qkv = _mm(h0, W["w_qkv"], "nn", T, 3 * SBW, D, (f32,), "proj_qkv")
    qn, kn, vb = _qk_norm_fwd(qkv, W["q_norm"], W["k_norm"], SBW, "qk_norm")
    y_sb, ctot, got = _sb_fwd(qn, kn, vb, "sb_attn", comm=net.rest_ici() if net else None)
    xbc = _conv_fwd(xraw, W["conv_w"], W["conv_b"], "conv")
    y_ssd, yn_ssd, sprev, got = _ssd_fwd(xbc, dtraw, z, W["dt_bias"], W["a_log"], W["d_skip"], W["out_norm"], HS, "ssd",
                                         comm=net.rest_d2d(got) if net else None)
    if net:
        W = {**W, **net.rest_weights(got)}
    w_up, w_down, F = W["w_up"], W["w_down"], W["F"]
    mix = _mm(yn_ssd, W["w_out"], "nn", T, D, SW, (f32,), "out_ssd", epilogue=add, extras=((xc, "tile"),))
    x1 = _mm(y_sb, W["w_out"], "nn", T, D, SBW, (f32,), "out_sb", epilogue=add, extras=((mix, "tile"),), b_off=(SW, 0))
    x2, mlp0, got = _mlp_fwd(x1, mlp_norm[0:1], w_up[0], w_down[0], F, "mlp0",
                             comms=(net.last_ici(0), net.last_ici(1)) if net else (None, None))
    hp = _rms_fwd(x2, W["pool_norm"], f32, "pool_norm")
    x3, yp, dpool, got = _pool_fwd(hp, x2, W["w_pool"], W["pool_b"], W["pool_scale"], "pool",
                                   comm=net.last_d2d(got) if net else None)
    if net:
        w_up, w_down = net.last_weights(got, w_up, w_down)
    x4, mlp1, _ = _mlp_fwd(x3, mlp_norm[1:2], w_up[1], w_down[1], F, "mlp1")

    dy, sq = _loss_grad(x4, tgt, "loss")

    up_to, down_to = (net.mlp_to("up", 1, None), net.mlp_to("down", 1, None)) if net else (None, None)
    dx3, dg_mlp1, dw_up1, dw_down1, _ = _mlp_bwd(dy, mlp1, mlp_norm[1:2], w_up[1], w_down[1], F, "mlp1", up_to, down_to)
    dhp, dw_pool, db_pool, dsc_pool = _pool_bwd(dx3, yp, dpool, W["w_pool"], W["pool_scale"], "pool_bwd")
    dx2, dg_pool = _rms_bwd(x2, W["pool_norm"], dhp, dx3, "pool_dnorm")
    up_to, down_to = (net.mlp_to("up", 0, dw_up1), net.mlp_to("down", 0, dw_down1)) if net else (None, None)
    dx1, dg_mlp0, dw_up0, dw_down0, got_mlp = _mlp_bwd(dx2, mlp0, mlp_norm[0:1], w_up[0], w_down[0], F, "mlp0", up_to, down_to,
                                                       host=net.swap_mlp if net else None)

    dw_out = jnp.concatenate([_mm(yn_ssd, dx1, "tn", SW, D, T, (f32,), "dwout_ssd"),
                              _mm(y_sb, dx1, "tn", SBW, D, T, (f32,), "dwout_sb")], axis=0)
    if net:
        (dmerged,), got = _mm(dx1, W["w_out"], "nt", T, SW + SBW, D, (f32,), "dmerged",
                              comm=net.swap_rest(dw_out, dw_pool))
        dqn, dkn, dvv, got = _sb_bwd(qn, kn, vb, dmerged, ctot, SW, "sb_attn_bwd", comm=net.reduce_early(list(got) + list(got_mlp)))
        net.reduce_early_done(got)
    else:
        dmerged = _mm(dx1, W["w_out"], "nt", T, SW + SBW, D, (f32,), "dmerged")
        dqn, dkn, dvv, _ = _sb_bwd(qn, kn, vb, dmerged, ctot, SW, "sb_attn_bwd")
    dqkv, dg_q, dg_k = _qk_norm_bwd(qkv, dqn, dkn, dvv, W["q_norm"], W["k_norm"], SBW, "qk_norm_bwd")
    dy_ssd, dz, dg_on = _gate_bwd(y_ssd, z, dmerged, W["out_norm"], "gate_bwd")
    dxbc, ddtraw, dalog, dbias, ddskip, got = _ssd_bwd(xbc, dtraw, sprev, dy_ssd, W["dt_bias"], W["a_log"], W["d_skip"], HS, "ssd_bwd",
                                                       comm=net.early_join() if net else None)
    if net:
        net.early_joined(got)
    dpre, dconv_w, dconv_b = _conv_bwd_pre(xraw, dxbc, W["conv_w"], W["conv_b"], "conv_bwd_pre")
    dxraw = _conv_bwd_in(dpre, W["conv_w"], "conv_bwd_in")
    dw_in = [_mm(h0, dz, "tn", D, SW, T, (f32,), "dwin_z"), _mm(h0, dxraw, "tn", D, CD, T, (f32,), "dwin_xbc"),
             _mm(h0, ddtraw, "tn", D, LANES, T, (f32,), "dwin_dt")[:, :HS], _mm(h0, dqkv, "tn", D, 3 * SBW, T, (f32,), "dwin_qkv")]
    dh0 = _mm(dz, W["w_z"], "nt", T, D, SW, (f32,), "dh0_z")
    if net:
        (dh0,), got = _mm(dxraw, W["w_xbc"], "nt", T, D, CD, (f32,), "dh0_xbc", epilogue=add, extras=((dh0, "tile"),),
                          comm=net.late_swap(dw_in))
        dh0 = _mm(ddtraw, W["w_dt"], "nt", T, D, LANES, (f32,), "dh0_dt", epilogue=add, extras=((dh0, "tile"),))
        (dh0,), got = _mm(dqkv, W["w_qkv"], "nt", T, D, 3 * SBW, (f32,), "dh0_qkv", epilogue=add, extras=((dh0, "tile"),),
                          comm=net.reduce_late(got))
        net.late_part_done(0, got)
        grad_x, dg_hyb, got = _rms_bwd(xc, W["hyb_norm"], dh0, dx1, "hyb_dnorm", comm=net.late_part(1))
        net.late_part_done(1, got)
    else:
        dh0 = _mm(dxraw, W["w_xbc"], "nt", T, D, CD, (f32,), "dh0_xbc", epilogue=add, extras=((dh0, "tile"),))
        dh0 = _mm(ddtraw, W["w_dt"], "nt", T, D, LANES, (f32,), "dh0_dt", epilogue=add, extras=((dh0, "tile"),))
        dh0 = _mm(dqkv, W["w_qkv"], "nt", T, D, 3 * SBW, (f32,), "dh0_qkv", epilogue=add, extras=((dh0, "tile"),))
        grad_x, dg_hyb = _rms_bwd(xc, W["hyb_norm"], dh0, dx1, "hyb_dnorm")
    grads = dict(w_in=dw_in, w_out=dw_out, w_pool=dw_pool, w_up=(dw_up0, dw_up1), w_down=(dw_down0, dw_down1),
                 hyb_norm=dg_hyb, conv_w=dconv_w, conv_b=dconv_b, dt_bias=dbias, a_log=dalog, d_skip=ddskip, out_norm=dg_on,
                 q_norm=dg_q, k_norm=dg_k, mlp_norm=(dg_mlp0, dg_mlp1), pool_norm=dg_pool, pool_b=db_pool, pool_scale=dsc_pool)
    return sq, grad_x, grads


def _stack_columns(pieces, n):
    cs = sum(p.shape[1] for p in pieces) // n
    slots = []
    for j in range(n):
        parts, off = [], 0
        for p in pieces:
            lo, hi = max(j * cs, off), min((j + 1) * cs, off + p.shape[1])
            if lo < hi:
                parts.append(p[:, lo - off:hi - off])
            off += p.shape[1]
        slots.append(parts[0] if len(parts) == 1 else jnp.concatenate(parts, axis=1))
    return jnp.stack(slots)


class _Net:
    def __init__(self, own_first, own_last, chip, dims):
        self.own, self.own_last, self.chip, self.dims = own_first, own_last, chip, dims

    def _place_own(self, stacks, own):
        return [lax.dynamic_update_index_in_dim(g, o, self.chip, 0) for g, o in zip(stacks, own)]

    def rest_ici(self):
        return _gather_ici(self.own)

    def rest_d2d(self, got):
        return _gather_d2d(list(got))

    def rest_weights(self, got):
        d, nw = self.dims, len(POOL_WINDOWS)
        D, F, PG = d["D"], d["F"], d["PG"]
        fs = F // N_CHIPS
        g_out, g_pool, g_up, g_down = self._place_own(got, self.own)
        w_pool = g_pool.reshape(N_CHIPS, nw, PG // N_CHIPS, PG).transpose(1, 0, 2, 3).reshape(nw, PG, PG)
        return dict(w_out=g_out.reshape(d["MIX"], D), w_pool=w_pool, F=F,
                    w_up=[(g_up, ("cols", fs, 0, D))], w_down=[(g_down, ("rows", fs, 0, None))])

    def last_ici(self, which):
        return _gather_ici([self.own_last[which]])

    def last_d2d(self, got):
        return _gather_d2d([got[0][0], got[1][0]])

    def last_weights(self, stacks, w_up, w_down):
        d = self.dims
        fs = d["F"] // N_CHIPS
        g_up, g_down = self._place_own(stacks, self.own_last)
        return w_up + [(g_up, ("cols", fs, 0, d["D"]))], w_down + [(g_down, ("rows", fs, 0, None))]

    def mlp_to(self, which, layer, earlier):
        d = self.dims
        fs = d["F"] // N_CHIPS
        if which == "up":
            return ("cols", fs, layer, d["D"]), (N_CHIPS, d["NL"] * d["D"], fs), earlier
        return ("rows", fs, layer, None), (N_CHIPS, d["NL"] * fs, d["D"]), earlier

    def swap_mlp(self, g_up, g_down):
        self.g_mlp = [g_up, g_down]
        return _swap_halves(self.g_mlp)

    def swap_rest(self, dw_out, dw_pool):
        d, nw = self.dims, len(POOL_WINDOWS)
        PG = d["PG"]
        self.early_g4 = [dw_out.reshape(N_CHIPS, d["MIX"] // N_CHIPS, d["D"]),
                         dw_pool.reshape(nw, N_CHIPS, PG // N_CHIPS, PG).transpose(1, 0, 2, 3).reshape(N_CHIPS, PG, PG)] + self.g_mlp
        return _swap_halves(self.early_g4[:2])

    def reduce_early(self, recv):
        self.early_recv = list(recv)
        sent = [_pair_sum(g, r, f"grads_early_pair_sum{i}") for i, (g, r) in enumerate(zip(self.early_g4, self.early_recv))]
        return _exchange_chips(sent)

    def reduce_early_done(self, got):
        self.early_got = list(got)

    def late_swap(self, dw_in):
        self.late_g4 = [_stack_columns(dw_in, N_CHIPS)]
        return _swap_halves(self.late_g4)

    def reduce_late(self, recv):
        self.late_recv = list(recv)
        sent = _pair_sum(self.late_g4[0], self.late_recv[0], "grads_late_pair_sum")
        cut = 5 * sent.shape[1] // 8
        self.late_parts = [sent[:, :cut], sent[:, cut:]]
        self.late_got = [None] * 2
        return self.late_part(0)

    def late_part(self, i):
        return _exchange_chips([self.late_parts[i]])

    def late_part_done(self, i, got):
        self.late_got[i] = got[0]

    def _halves(self, g4s, recvs, pieces, tag):
        return [_chip_sum(g, r, p, f"grads_{tag}_chip_sum{i}") for i, (g, r, p) in enumerate(zip(g4s, recvs, pieces))]

    def early_join(self):
        return _join_halves(self._halves(self.early_g4, self.early_recv, self.early_got, "early"))

    def early_joined(self, got):
        self.early_done = list(got)

    def reduced_early(self):
        return self.early_done

    def reduced_late(self):
        halves = self._halves(self.late_g4, self.late_recv, [jnp.concatenate(self.late_got, axis=1)], "late")
        return _run_exchange(_join_halves(halves), "grads_late_join")[0]


def kernel(x, hyb_norm, hyb_w_in, ssd_conv_w, ssd_conv_b, ssd_dt_bias, ssd_a_log, ssd_d, ssd_out_norm, sb_q_norm, sb_k_norm, hyb_w_out, pool_norm, pool_w, pool_b, pool_scale, mlp_norm, mlp_w_up, mlp_w_down, loss_target, m_hyb_norm, m_hyb_w_in, m_ssd_conv_w, m_ssd_conv_b, m_ssd_dt_bias, m_ssd_a_log, m_ssd_d, m_ssd_out_norm, m_sb_q_norm, m_sb_k_norm, m_hyb_w_out, m_pool_norm, m_pool_w, m_pool_b, m_pool_scale, m_mlp_norm, m_mlp_w_up, m_mlp_w_down, v_hyb_norm, v_hyb_w_in, v_ssd_conv_w, v_ssd_conv_b, v_ssd_dt_bias, v_ssd_a_log, v_ssd_d, v_ssd_out_norm, v_sb_q_norm, v_sb_k_norm, v_hyb_w_out, v_pool_norm, v_pool_w, v_pool_b, v_pool_scale, v_mlp_norm, v_mlp_w_up, v_mlp_w_down):
    T, D = x.shape[1], x.shape[2]
    HS = ssd_dt_bias.shape[-1]
    SW = HS * SSD_HEAD_DIM
    CD = ssd_conv_b.shape[-1]
    IN = N_CHIPS * hyb_w_in.shape[-1]
    SBW = (IN - SW - CD - HS) // 3
    F = N_CHIPS * mlp_w_up.shape[-1]
    NL = mlp_norm.shape[0]
    PG = D // len(POOL_WINDOWS)
    xc, tgt = x[0], loss_target[0]
    ix, iy, ic = lax.axis_index("x"), lax.axis_index("y"), lax.axis_index("c")
    chip = (2 * ix + iy).astype(jnp.int32)

    small = jnp.concatenate([ssd_conv_w.reshape(-1), pool_norm.reshape(-1), pool_b.reshape(-1), pool_scale.reshape(-1)])
    ns = small.shape[0]
    ns8 = -(-ns // (8 * LANES)) * LANES
    gathered, _ = _gather_small(jnp.pad(small, (0, 8 * ns8 - ns)).reshape(8, ns8), "gather_small")
    per_chip = gathered.reshape(N_CHIPS, 2, 8 * ns8)[:, 0, :ns]
    cw = CD // N_CHIPS
    conv_w = per_chip[:, :4 * cw].reshape(N_CHIPS, 4, cw).transpose(1, 0, 2).reshape(4, CD)
    pvec = per_chip[:, 4 * cw:].reshape(N_CHIPS, 3, PG)
    pool_norm_f, pool_b_f, pool_scale_f = (pvec[:, i].reshape(1, D) for i in range(3))

    fs = F // N_CHIPS
    own_in = hyb_w_in[0].astype(bf16)
    g_in = _run_exchange(_gather_d2d(_run_exchange(_gather_ici([own_in]), "gather_in_ici")), "gather_in_d2d")[0]
    w_in = lax.dynamic_update_index_in_dim(g_in, own_in, chip, 0).transpose(1, 0, 2).reshape(D, IN)
    c1, c2, c3 = SW, SW + CD, SW + CD + HS
    w_z, w_xbc, w_dt, w_qkv = w_in[:, :c1], w_in[:, c1:c2], _pad_lanes(w_in[:, c2:c3]), w_in[:, c3:]
    dt_bias_p, a_log_p, d_skip_p = _pad_lanes(ssd_dt_bias), _pad_lanes(ssd_a_log), jnp.repeat(ssd_d, SSD_HEAD_DIM, axis=-1)

    assert NL == 2
    own_first = [hyb_w_out[0].astype(bf16), pool_w[0].reshape(-1, PG).astype(bf16),
                 mlp_w_up[0].astype(bf16), mlp_w_down[0].astype(bf16)]
    own_last = [mlp_w_up[1].astype(bf16), mlp_w_down[1].astype(bf16)]
    net = _Net(own_first, own_last, chip, dict(D=D, F=F, NL=NL, PG=PG, IN=IN, MIX=SW + SBW))
    first = dict(hyb_norm=hyb_norm, w_z=w_z, w_xbc=w_xbc, w_dt=w_dt, w_qkv=w_qkv, conv_w=conv_w, conv_b=ssd_conv_b,
                 dt_bias=dt_bias_p, a_log=a_log_p, d_skip=d_skip_p, out_norm=ssd_out_norm, q_norm=sb_q_norm, k_norm=sb_k_norm,
                 pool_norm=pool_norm_f, pool_b=pool_b_f, pool_scale=pool_scale_f, mlp_norm=mlp_norm)
    sq, grad_x, gr = _local_step(xc, tgt, first, HS, SBW, net)
    loss = lax.psum(sq[0, 0] * (0.5 / D), ("x", "y", "c"))
    dg_hyb, dconv_b, dbias, dalog, ddskip, dg_on, dg_q, dg_k = (gr[k] for k in (
        "hyb_norm", "conv_b", "dt_bias", "a_log", "d_skip", "out_norm", "q_norm", "k_norm"))
    (dg_mlp0, dg_mlp1), dconv_w, dg_pool, db_pool, dsc_pool = gr["mlp_norm"], gr["conv_w"], gr["pool_norm"], gr["pool_b"], gr["pool_scale"]
    gb_out, gb_pool, gb_up, gb_down = net.reduced_early()

    full_small = [dg_hyb, dconv_b, dbias[:, :HS], dalog[:, :HS], ddskip[:, :HS], dg_on, dg_q, dg_k,
                  jnp.concatenate([dg_mlp0, dg_mlp1], axis=0).reshape(1, -1),
                  dconv_w.reshape(1, -1), dg_pool, db_pool, dsc_pool]
    sizes = [v.shape[-1] for v in full_small]
    packed = jnp.concatenate([v.reshape(-1) for v in full_small])
    npk = packed.shape[0]
    npk8 = -(-npk // (8 * LANES)) * LANES
    _, summed = _gather_small(jnp.pad(packed, (0, 8 * npk8 - npk)).reshape(8, npk8), "grads_small")
    summed = summed.reshape(-1)[:npk]
    offs = [0]
    for s in sizes:
        offs.append(offs[-1] + s)
    (g_hyb_norm, g_conv_b, g_dt_bias, g_a_log, g_d, g_out_norm, g_q_norm, g_k_norm, g_mlp_norm, g_conv_w_full,
     g_pool_norm_full, g_pool_b_full, g_pool_scale_full) = (summed[offs[i]:offs[i + 1]] for i in range(len(sizes)))
    take = lambda full, n: lax.dynamic_slice_in_dim(full.reshape(-1, N_CHIPS, n), chip, 1, axis=1)
    small_grads = {
        "hyb_norm": g_hyb_norm.reshape(hyb_norm.shape), "ssd_conv_w": take(g_conv_w_full, cw).reshape(ssd_conv_w.shape),
        "ssd_conv_b": g_conv_b.reshape(ssd_conv_b.shape), "ssd_dt_bias": g_dt_bias.reshape(ssd_dt_bias.shape),
        "ssd_a_log": g_a_log.reshape(ssd_a_log.shape), "ssd_d": g_d.reshape(ssd_d.shape),
        "ssd_out_norm": g_out_norm.reshape(ssd_out_norm.shape), "sb_q_norm": g_q_norm.reshape(sb_q_norm.shape),
        "sb_k_norm": g_k_norm.reshape(sb_k_norm.shape), "pool_norm": take(g_pool_norm_full, PG).reshape(pool_norm.shape),
        "pool_b": take(g_pool_b_full, PG).reshape(pool_b.shape), "pool_scale": take(g_pool_scale_full, PG).reshape(pool_scale.shape),
        "mlp_norm": g_mlp_norm.reshape(mlp_norm.shape),
    }

    weights = dict(hyb_norm=hyb_norm, hyb_w_in=hyb_w_in, ssd_conv_w=ssd_conv_w, ssd_conv_b=ssd_conv_b, ssd_dt_bias=ssd_dt_bias,
                   ssd_a_log=ssd_a_log, ssd_d=ssd_d, ssd_out_norm=ssd_out_norm, sb_q_norm=sb_q_norm, sb_k_norm=sb_k_norm,
                   hyb_w_out=hyb_w_out, pool_norm=pool_norm, pool_w=pool_w, pool_b=pool_b, pool_scale=pool_scale,
                   mlp_norm=mlp_norm, mlp_w_up=mlp_w_up, mlp_w_down=mlp_w_down)
    moms = dict(hyb_norm=m_hyb_norm, hyb_w_in=m_hyb_w_in, ssd_conv_w=m_ssd_conv_w, ssd_conv_b=m_ssd_conv_b, ssd_dt_bias=m_ssd_dt_bias,
                ssd_a_log=m_ssd_a_log, ssd_d=m_ssd_d, ssd_out_norm=m_ssd_out_norm, sb_q_norm=m_sb_q_norm, sb_k_norm=m_sb_k_norm,
                hyb_w_out=m_hyb_w_out, pool_norm=m_pool_norm, pool_w=m_pool_w, pool_b=m_pool_b, pool_scale=m_pool_scale,
                mlp_norm=m_mlp_norm, mlp_w_up=m_mlp_w_up, mlp_w_down=m_mlp_w_down)
    vels = dict(hyb_norm=v_hyb_norm, hyb_w_in=v_hyb_w_in, ssd_conv_w=v_ssd_conv_w, ssd_conv_b=v_ssd_conv_b, ssd_dt_bias=v_ssd_dt_bias,
                ssd_a_log=v_ssd_a_log, ssd_d=v_ssd_d, ssd_out_norm=v_ssd_out_norm, sb_q_norm=v_sb_q_norm, sb_k_norm=v_sb_k_norm,
                hyb_w_out=v_hyb_w_out, pool_norm=v_pool_norm, pool_w=v_pool_w, pool_b=v_pool_b, pool_scale=v_pool_scale,
                mlp_norm=v_mlp_norm, mlp_w_up=v_mlp_w_up, mlp_w_down=v_mlp_w_down)
    order = list(weights)
    grads, delta, new_m, new_v = {}, {}, {}, {}
    for name, g2 in (("hyb_w_out", gb_out), ("pool_w", gb_pool), ("mlp_w_up", gb_up), ("mlp_w_down", gb_down),
                     ("hyb_w_in", net.reduced_late())):
        shp = weights[name].shape
        d_, m_, v_, g_, _ = _adamw(weights[name].reshape(g2.shape), g2, moms[name].reshape(g2.shape), vels[name].reshape(g2.shape),
                                   f"adamw_{name}")
        grads[name], delta[name], new_m[name], new_v[name] = (t.reshape(shp) for t in (g_, d_, m_, v_))
    snames = list(small_grads)
    pack = lambda d: jnp.concatenate([d[n].reshape(-1) for n in snames])
    nsm = sum(small_grads[n].size for n in snames)
    cols = -(-nsm // (8 * LANES)) * LANES
    as_blk = lambda v: jnp.pad(v, (0, 8 * cols - nsm)).reshape(8, cols)
    padded_v = jnp.pad(pack(vels), (0, 8 * cols - nsm), constant_values=1.0).reshape(8, cols)
    d_, m_, v_, _, _ = _adamw(as_blk(pack(weights)), as_blk(pack(small_grads)), as_blk(pack(moms)), padded_v, "adamw_small")
    off = 0
    for n in snames:
        sz, shp = small_grads[n].size, weights[n].shape
        grads[n] = small_grads[n]
        delta[n], new_m[n], new_v[n] = (t.reshape(-1)[off:off + sz].reshape(shp) for t in (d_, m_, v_))
        off += sz

    return (loss, grad_x.reshape(x.shape), *[grads[n] for n in order], *[delta[n] for n in order],
            *[new_m[n] for n in order], *[new_v[n] for n in order])
```

```python
import functools
import math

import jax
import jax.numpy as jnp
from jax import lax
from jax.experimental import pallas as pl
from jax.experimental.pallas import tpu as pltpu

f32 = jnp.float32
bf16 = jnp.bfloat16

EPS = 1e-6
SSD_HEAD_DIM = 64
SSD_STATE = 128
SSD_GROUPS = 4
SSD_CHUNK = 128
LANES = 128
SB_HEAD_DIM = 128
POOL_WINDOWS = (2, 4, 8, 16)
POOL_HALO = 16
CONV_HALO = 8
ADAM_LR, ADAM_B1, ADAM_B2, ADAM_EPS, ADAM_WD, ADAM_STEP = 0.001, 0.9, 0.999, 1e-08, 0.01, 10
VMEM_LIMIT = 56 * 1024 * 1024
MM_TILE_BUDGET = 40 * 1024 * 1024
N_CHIPS = 4
MESH = pl.DeviceIdType.MESH

_DIMS = {"nn": (((1,), (0,)), ((), ())), "nt": (((1,), (1,)), ((), ())), "tn": (((0,), (0,)), ((), ()))}


def _fit(n, t):
    if n <= t:
        return n
    return max(d for d in range(LANES, t + 1, LANES) if n % d == 0)


def _cp(*sem):
    return pltpu.CompilerParams(dimension_semantics=sem, vmem_limit_bytes=VMEM_LIMIT)


def _sigmoid(v):
    return 1.0 / (1.0 + jnp.exp(-v))


def _softplus(v):
    return jnp.maximum(v, 0.0) + jnp.log(1.0 + jnp.exp(-jnp.abs(v)))


def _split(v, parts):
    out, rem = [], v
    for _ in range(parts):
        p = rem.astype(bf16)
        out.append(p)
        rem = rem - p.astype(f32)
    return out


def _dot(a, b, mode="nn"):
    return lax.dot_general(a, b, _DIMS[mode], preferred_element_type=f32)


def _mask_dot(mask_b, v, parts):
    return _dot(jnp.concatenate([mask_b] * parts, axis=1), jnp.concatenate(_split(v, parts), axis=0))


def _dot_mask(v, mask_b, parts):
    return _dot(jnp.concatenate(_split(v, parts), axis=1), jnp.concatenate([mask_b] * parts, axis=0))


def _stacked(view, br, bc, rmap, cmap):
    kind, per, layer, rows_per_layer = view
    if kind == "cols":
        npc = per // bc
        return pl.BlockSpec((None, br, bc), lambda i, j, k: (cmap(i, j, k) // npc, layer * (rows_per_layer // br) + rmap(i, j, k),
                                                              cmap(i, j, k) % npc))
    npc = per // br
    return pl.BlockSpec((None, br, bc), lambda i, j, k: (rmap(i, j, k) // npc, layer * npc + rmap(i, j, k) % npc, cmap(i, j, k)))


def _pick_tiles(M, N, K, caps, a_bytes, b_bytes, io_bytes):
    def cands(n, cap, sizes):
        got = [s for s in sizes if s <= min(n, cap) and n % s == 0]
        return got or [_fit(n, min(n, cap))]

    best = None
    for tk in cands(K, caps[2], (8192, 4096, 2048, 1024, 512, 256, 128)):
        for tm in cands(M, caps[0], (1024, 512, 256, 128)):
            for tn in cands(N, caps[1], (1024, 512, 256, 128)):
                need = 2 * (tm * tk * a_bytes + tk * tn * b_bytes) + tm * tn * (2 * io_bytes + (4 if tk < K else 0))
                key = (need <= MM_TILE_BUDGET, tk, tm * tn, tm)
                if best is None or key > best[0]:
                    best = (key, (tm, tn, tk))
    return best[1]


def _mm(a, b, mode, M, N, K, outs, name, epilogue=None, extras=(), a_off=(0, 0), b_off=(0, 0),
        b_view=None, out_view=None, out_stack=None, alias=None, comm=None):
    caps = [M, N, K]
    if b_view is not None:
        caps[1 if (b_view[0] == "cols") == (mode != "nt") else 2] = b_view[1]
    if out_view is not None:
        d = 1 if out_view[0] == "cols" else 0
        caps[d] = min(caps[d], out_view[1])
    for off, dims in ((a_off, (2, 0) if mode == "tn" else (0, 2)), (b_off, (1, 2) if mode == "nt" else (2, 1))):
        for o, d in zip(off, dims):
            if o:
                caps[d] = min(caps[d], math.gcd(o, caps[d]))
    io_bytes = sum(jnp.dtype(dt).itemsize for dt in outs) + sum(e[0].dtype.itemsize for e in extras if e[1] == "tile")
    tm, tn, tk = _pick_tiles(M, N, K, caps, a.dtype.itemsize, b.dtype.itemsize, io_bytes)
    nk = K // tk
    if mode == "tn":
        a_blk, ad = (tk, tm), (tk, tm)
    else:
        a_blk, ad = (tm, tk), (tm, tk)
    b_blk = (tn, tk) if mode == "nt" else (tk, tn)
    assert a_off[0] % ad[0] == 0 and a_off[1] % ad[1] == 0 and b_off[0] % b_blk[0] == 0 and b_off[1] % b_blk[1] == 0
    ao = (a_off[0] // ad[0], a_off[1] // ad[1])
    bo = (b_off[0] // b_blk[0], b_off[1] // b_blk[1])
    if mode == "tn":
        a_map = lambda i, j, k: (k + ao[0], i + ao[1])
    else:
        a_map = lambda i, j, k: (i + ao[0], k + ao[1])
    if mode == "nt":
        b_map = lambda i, j, k: (j + bo[0], k + bo[1])
    else:
        b_map = lambda i, j, k: (k + bo[0], j + bo[1])
    if b_view is not None:
        if mode == "nt":
            b_spec = _stacked(b_view, tn, tk, lambda i, j, k: j, lambda i, j, k: k)
        else:
            b_spec = _stacked(b_view, tk, tn, lambda i, j, k: k, lambda i, j, k: j)
    else:
        b_spec = pl.BlockSpec(b_blk, b_map)
    in_specs = [pl.BlockSpec(a_blk, a_map), b_spec]
    for arr, kind in extras:
        if kind == "tile":
            in_specs.append(pl.BlockSpec((tm, tn), lambda i, j, k: (i, j)))
        else:
            in_specs.append(pl.BlockSpec((1, tn), lambda i, j, k: (0, j)))
    ne, no = len(extras), len(outs)
    if epilogue is None:
        epilogue = lambda acc: (acc,)
    operands = [a, b, *[e[0] for e in extras]]
    aliases = {}
    if alias is not None:
        in_specs.append(ANY)
        aliases[len(operands)] = 0
        operands.append(alias)
    n_in = len(operands)
    if out_view is not None:
        out_specs = [_stacked(out_view, tm, tn, lambda i, j, k: i, lambda i, j, k: j)]
        out_shape = [jax.ShapeDtypeStruct(out_stack, outs[0])]
    else:
        out_specs = [pl.BlockSpec((tm, tn), lambda i, j, k: (i, j)) for _ in outs]
        out_shape = [jax.ShapeDtypeStruct((M, N), dt) for dt in outs]
    scratch = [pltpu.VMEM((tm, tn), f32)] if nk > 1 else []
    grid = (M // tm, N // tn, nk)
    if comm is not None:
        in_specs += [ANY] * len(comm["ins"])
        operands += comm["ins"]
        out_specs += [ANY] * len(comm["outs"])
        out_shape += comm["outs"]
        scratch += comm["sems"]
    nci, nco, ncs = (len(comm["ins"]), len(comm["outs"]), len(comm["sems"])) if comm is not None else (0, 0, 0)

    def body(*refs):
        a_ref, b_ref = refs[0], refs[1]
        ex, out_refs = refs[2:2 + ne], refs[n_in + nci:n_in + nci + no]
        rest = refs[n_in + nci + no + nco:]
        if comm is not None:
            cargs = (refs[n_in:n_in + nci], refs[n_in + nci + no:n_in + nci + no + nco], refs[len(refs) - ncs:])
            pid = [pl.program_id(d) for d in range(3)]

            @pl.when((pid[0] == 0) & (pid[1] == 0) & (pid[2] == 0))
            def _():
                comm["start"](*cargs)

        def finish(acc):
            res = epilogue(acc, *[e[...] for e in ex])
            for o, r in zip(out_refs, res):
                o[...] = r.astype(o.dtype)

        prod = lax.dot_general(a_ref[...].astype(bf16), b_ref[...].astype(bf16), _DIMS[mode],
                               preferred_element_type=f32)
        if nk == 1:
            finish(prod)
        else:
            acc_ref = rest[0]
            k = pl.program_id(2)

            @pl.when(k == 0)
            def _():
                acc_ref[...] = prod

            @pl.when(k > 0)
            def _():
                acc_ref[...] += prod

            @pl.when(k == nk - 1)
            def _():
                finish(acc_ref[...])

        if comm is not None:
            @pl.when((pid[0] == grid[0] - 1) & (pid[1] == grid[1] - 1) & (pid[2] == grid[2] - 1))
            def _():
                comm["finish"](*cargs)

    sem = ("arbitrary",) * 3 if comm is not None else ("parallel", "parallel", "arbitrary")
    res = pl.pallas_call(
        body, name=name, grid=grid, in_specs=in_specs, out_specs=out_specs, out_shape=out_shape,
        scratch_shapes=scratch, input_output_aliases=aliases, compiler_params=_cp(*sem),
    )(*operands)
    if comm is not None:
        return res[:no], res[no:]
    return res[0] if no == 1 else res


def _rms_fwd(x, g, out_dtype, name, tr=256, comm=None):
    T, D = x.shape
    tr = min(tr, T)
    grid = (T // tr,)

    def body(*refs):
        (x_ref, g_ref), (o_ref,), _, cargs = _hosted(comm, 2, 1, refs)
        if comm is not None:
            first, last = _first_last(grid)

            @pl.when(first)
            def _():
                comm["start"](*cargs)

            @pl.when(last)
            def _():
                comm["finish"](*cargs)

        xv = x_ref[...]
        r = lax.rsqrt(jnp.mean(xv * xv, axis=-1, keepdims=True) + EPS)
        o_ref[...] = (xv * r * g_ref[...]).astype(o_ref.dtype)

    (o,), got = _host_call(body, name, grid, [pl.BlockSpec((tr, D), lambda r: (r, 0)), pl.BlockSpec((1, D), lambda r: (0, 0))],
                           [pl.BlockSpec((tr, D), lambda r: (r, 0))], [jax.ShapeDtypeStruct((T, D), out_dtype)], [], [x, g], comm)
    return o if comm is None else (o, got)


def _rms_bwd(x, g, dh, dres, name, tr=256, comm=None):
    T, D = x.shape
    tr = min(tr, T)
    grid = (T // tr,)

    def body(*refs):
        (x_ref, g_ref, dh_ref, dres_ref), (dx_ref, dg_ref), _, cargs = _hosted(comm, 4, 2, refs)
        if comm is not None:
            first, last = _first_last(grid)

            @pl.when(first)
            def _():
                comm["start"](*cargs)

            @pl.when(last)
            def _():
                comm["finish"](*cargs)

        xv = x_ref[...]
        r = lax.rsqrt(jnp.mean(xv * xv, axis=-1, keepdims=True) + EPS)
        xh = xv * r
        dhv = dh_ref[...]
        dhg = dhv * g_ref[...]
        dx_ref[...] = dres_ref[...] + r * (dhg - xh * jnp.mean(dhg * xh, axis=-1, keepdims=True))

        @pl.when(pl.program_id(0) == 0)
        def _():
            dg_ref[...] = jnp.zeros_like(dg_ref)

        dg_ref[...] += jnp.sum(dhv * xh, axis=0, keepdims=True)

    row = pl.BlockSpec((tr, D), lambda r: (r, 0))
    vec = pl.BlockSpec((1, D), lambda r: (0, 0))
    (dx, dg), got = _host_call(body, name, grid, [row, vec, row, row], [row, vec],
                               [jax.ShapeDtypeStruct((T, D), f32), jax.ShapeDtypeStruct((1, D), f32)], [], [x, g, dh, dres], comm)
    return (dx, dg) if comm is None else (dx, dg, got)


def _loss_grad(y, tgt, name, tr=256):
    T, D = y.shape
    tr = min(tr, T)

    def body(y_ref, t_ref, dy_ref, s_ref):
        e = y_ref[...] - t_ref[...]
        dy_ref[...] = e * (1.0 / D)

        @pl.when(pl.program_id(0) == 0)
        def _():
            s_ref[...] = jnp.zeros_like(s_ref)

        s_ref[...] += jnp.sum(e * e)

    row = pl.BlockSpec((tr, D), lambda r: (r, 0))
    return pl.pallas_call(
        body, name=name, grid=(T // tr,), in_specs=[row, row],
        out_specs=[row, pl.BlockSpec((8, LANES), lambda r: (0, 0))],
        out_shape=[jax.ShapeDtypeStruct((T, D), f32), jax.ShapeDtypeStruct((8, LANES), f32)],
        compiler_params=_cp("arbitrary"),
    )(y, tgt)


def _shift_down(cur, prev, s):
    rolled = pltpu.roll(cur, s, 0)
    top = pltpu.roll(prev, s, 0)
    row = lax.broadcasted_iota(jnp.int32, top.shape, 0)
    head = jnp.where(row < s, top, rolled[0:CONV_HALO])
    return jnp.concatenate([head, rolled[CONV_HALO:]], axis=0)


def _shift_up(cur, nxt, s):
    n = cur.shape[0]
    rolled = pltpu.roll(cur, n - s, 0)
    bot = pltpu.roll(nxt, CONV_HALO - s, 0)
    row = lax.broadcasted_iota(jnp.int32, bot.shape, 0)
    tail = jnp.where(row >= CONV_HALO - s, bot, rolled[n - CONV_HALO:])
    return jnp.concatenate([rolled[:n - CONV_HALO], tail], axis=0)


def _conv_pre(cur, prev, w_ref, b_ref):
    taps = [cur] + [_shift_down(cur, prev, s) for s in (1, 2, 3)]
    pre = b_ref[...] + w_ref[3:4, :] * taps[0]
    for s in (1, 2, 3):
        pre = pre + w_ref[3 - s:4 - s, :] * taps[s]
    return pre, taps


def _conv_specs(T, C, rc, cb):
    cur = pl.BlockSpec((rc, cb), lambda j, r: (r, j))
    prev = pl.BlockSpec((CONV_HALO, cb), lambda j, r: (jnp.maximum(r * (rc // CONV_HALO) - 1, 0), j))
    nxt = pl.BlockSpec((CONV_HALO, cb), lambda j, r: (jnp.minimum((r + 1) * (rc // CONV_HALO), T // CONV_HALO - 1), j))
    w = pl.BlockSpec((4, cb), lambda j, r: (0, j))
    b = pl.BlockSpec((1, cb), lambda j, r: (0, j))
    return cur, prev, nxt, w, b


def _conv_fwd(xraw, w, b, name):
    T, C = xraw.shape
    rc, cb = min(512, T), min(512, C)
    cur, prev, _, ws, bs = _conv_specs(T, C, rc, cb)

    def body(x_ref, p_ref, w_ref, b_ref, o_ref):
        pv = jnp.where(pl.program_id(1) > 0, p_ref[...], 0.0)
        pre, _ = _conv_pre(x_ref[...], pv, w_ref, b_ref)
        o_ref[...] = pre * _sigmoid(pre)

    return pl.pallas_call(
        body, name=name, grid=(C // cb, T // rc), in_specs=[cur, prev, ws, bs], out_specs=cur,
        out_shape=jax.ShapeDtypeStruct((T, C), f32), compiler_params=_cp("parallel", "parallel"),
    )(xraw, xraw, w, b)


def _conv_bwd_pre(xraw, dxbc, w, b, name):
    T, C = xraw.shape
    rc, cb = min(512, T), min(512, C)
    cur, prev, _, ws, bs = _conv_specs(T, C, rc, cb)

    def body(x_ref, p_ref, d_ref, w_ref, b_ref, dpre_ref, dw_ref, db_ref):
        pv = jnp.where(pl.program_id(1) > 0, p_ref[...], 0.0)
        pre, taps = _conv_pre(x_ref[...], pv, w_ref, b_ref)
        sg = _sigmoid(pre)
        dpre = d_ref[...] * (sg * (1.0 + pre * (1.0 - sg)))
        dpre_ref[...] = dpre

        @pl.when(pl.program_id(1) == 0)
        def _():
            dw_ref[...] = jnp.zeros_like(dw_ref)
            db_ref[...] = jnp.zeros_like(db_ref)

        row = lax.broadcasted_iota(jnp.int32, dw_ref.shape, 0)
        upd = jnp.zeros(dw_ref.shape, f32)
        for s in range(4):
            upd = upd + jnp.where(row == 3 - s, jnp.sum(dpre * taps[s], axis=0, keepdims=True), 0.0)
        dw_ref[...] += upd
        db_ref[...] += jnp.sum(dpre, axis=0, keepdims=True)

    return pl.pallas_call(
        body, name=name, grid=(C // cb, T // rc), in_specs=[cur, prev, cur, ws, bs], out_specs=[cur, ws, bs],
        out_shape=[jax.ShapeDtypeStruct((T, C), f32), jax.ShapeDtypeStruct((4, C), f32), jax.ShapeDtypeStruct((1, C), f32)],
        compiler_params=_cp("parallel", "arbitrary"),
    )(xraw, xraw, dxbc, w, b)


def _conv_bwd_in(dpre, w, name):
    T, C = dpre.shape
    rc, cb = min(512, T), min(512, C)
    cur, _, nxt, ws, _ = _conv_specs(T, C, rc, cb)
    nr = T // rc

    def body(d_ref, n_ref, w_ref, o_ref):
        nv = jnp.where(pl.program_id(1) < nr - 1, n_ref[...], 0.0)
        cv = d_ref[...]
        out = w_ref[3:4, :] * cv
        for s in (1, 2, 3):
            out = out + w_ref[3 - s:4 - s, :] * _shift_up(cv, nv, s)
        o_ref[...] = out.astype(o_ref.dtype)

    return pl.pallas_call(
        body, name=name, grid=(C // cb, nr), in_specs=[cur, nxt, ws], out_specs=cur,
        out_shape=jax.ShapeDtypeStruct((T, C), bf16), compiler_params=_cp("parallel", "parallel"),
    )(dpre, dpre, w)


HEAD_SHIFT = SSD_HEAD_DIM.bit_length() - 1


def _ssd_prep(dtr_ref, bias_ref, alog_ref, SW):
    L = SSD_CHUNK
    xs = dtr_ref[...] + bias_ref[...]
    dt = _softplus(xs)
    a = -jnp.exp(alog_ref[...])
    causal = lax.broadcasted_iota(jnp.int32, (L, L), 0) >= lax.broadcasted_iota(jnp.int32, (L, L), 1)
    cs = _mask_dot(causal.astype(bf16), dt * a, 3)
    spread = (lax.broadcasted_iota(jnp.int32, (LANES, SW), 0)
              == lax.shift_right_logical(lax.broadcasted_iota(jnp.int32, (LANES, SW), 1), HEAD_SHIFT)).astype(bf16)
    dt_x = _dot_mask(dt, spread, 3)
    cs_x = _dot_mask(cs, spread, 3)
    last_x = cs_x[L - 1:L, :]
    return xs, dt, a, causal, cs, cs.T, dt_x, jnp.exp(cs_x), jnp.exp(last_x - cs_x), jnp.exp(last_x)


def _head_decay(cs, csT, causal, h):
    seg = cs[:, h:h + 1] - csT[h:h + 1, :]
    return jnp.where(causal, jnp.exp(jnp.minimum(seg, 0.0)), 0.0)


def _ssd_fwd(xbc, dtraw, z, dt_bias, a_log, d_skip_x, out_norm, HS, name, comm=None):
    T = xbc.shape[0]
    L, P, NS, G = SSD_CHUNK, SSD_HEAD_DIM, SSD_STATE, SSD_GROUPS
    SW, HPG, nc = HS * P, HS // SSD_GROUPS, T // SSD_CHUNK
    gsz = SW // G
    gw = HPG * P
    assert HPG % 2 == 0 and 2 * P == LANES

    def body(*refs):
        (xbc_ref, dtr_ref, z_ref, bias_ref, alog_ref, dsk_ref, on_ref), (y_ref, yn_ref, sp_ref), (st_ref,), cargs = _hosted(comm, 7, 3, refs)
        first, last = _first_last((nc,))
        if comm is not None:
            @pl.when(first)
            def _():
                comm["start"](*cargs)

            @pl.when(last)
            def _():
                comm["finish"](*cargs)

        @pl.when(pl.program_id(0) == 0)
        def _():
            st_ref[...] = jnp.zeros_like(st_ref)

        sp_ref[0] = st_ref[...]
        _, _, _, causal, cs, csT, dt_x, ecs_x, dte_x, cdec_x = _ssd_prep(dtr_ref, bias_ref, alog_ref, SW)
        X = xbc_ref[:, 0:SW]
        Xd = X * dt_x
        Xdb = Xd.astype(bf16)
        XEb = (Xd * dte_x).astype(bf16)
        left = lax.broadcasted_iota(jnp.int32, (L, LANES), 1) < P
        for g in range(G):
            gs = slice(g * gw, (g + 1) * gw)
            Bb = xbc_ref[:, SW + g * NS:SW + (g + 1) * NS].astype(bf16)
            Cb = xbc_ref[:, SW + (G + g) * NS:SW + (G + g + 1) * NS].astype(bf16)
            Gm = _dot(Cb, Bb, "nt")
            Sp = st_ref[:, gs]
            yo = _dot(Cb, Sp.astype(bf16)) * ecs_x[:, gs]
            st_ref[:, gs] = cdec_x[:, gs] * Sp + _dot(Bb, XEb[:, gs], "tn")
            for pr in range(HPG // 2):
                h0 = g * HPG + 2 * pr
                ps = slice(h0 * P, (h0 + 2) * P)
                xp = Xdb[:, ps]
                yd = jnp.where(left, _dot((_head_decay(cs, csT, causal, h0) * Gm).astype(bf16), xp),
                               _dot((_head_decay(cs, csT, causal, h0 + 1) * Gm).astype(bf16), xp))
                y_ref[:, ps] = yd + yo[:, pr * LANES:(pr + 1) * LANES] + dsk_ref[:, ps] * X[:, ps]
        zz = z_ref[...]
        gated = y_ref[...] * (zz * _sigmoid(zz))
        for g in range(G):
            gs = slice(g * gsz, (g + 1) * gsz)
            sg = gated[:, gs]
            rr = lax.rsqrt(jnp.mean(sg * sg, axis=-1, keepdims=True) + EPS)
            yn_ref[:, gs] = (sg * rr * on_ref[:, gs]).astype(yn_ref.dtype)

    vec = pl.BlockSpec((1, LANES), lambda c: (0, 0))
    wide = pl.BlockSpec((1, SW), lambda c: (0, 0))
    (y, yn, sp), got = _host_call(
        body, name, (nc,),
        [pl.BlockSpec((L, xbc.shape[1]), lambda c: (c, 0)), pl.BlockSpec((L, LANES), lambda c: (c, 0)),
         pl.BlockSpec((L, SW), lambda c: (c, 0)), vec, vec, wide, wide],
        [pl.BlockSpec((L, SW), lambda c: (c, 0)), pl.BlockSpec((L, SW), lambda c: (c, 0)),
         pl.BlockSpec((1, NS, SW), lambda c: (c, 0, 0))],
        [jax.ShapeDtypeStruct((T, SW), f32), jax.ShapeDtypeStruct((T, SW), bf16), jax.ShapeDtypeStruct((nc, NS, SW), f32)],
        [pltpu.VMEM((NS, SW), f32)], [xbc, dtraw, z, dt_bias, a_log, d_skip_x, out_norm], comm)
    return y, yn, sp, got


def _ssd_bwd(xbc, dtraw, sprev, dy, dt_bias, a_log, d_skip_x, HS, name, comm=None):
    T = xbc.shape[0]
    L, P, NS, G = SSD_CHUNK, SSD_HEAD_DIM, SSD_STATE, SSD_GROUPS
    SW, HPG, nc = HS * P, HS // SSD_GROUPS, T // SSD_CHUNK
    gw = HPG * P

    def body(*refs):
        ((xbc_ref, dtr_ref, sp_ref, dy_ref, bias_ref, alog_ref, dsk_ref),
         (dxbc_ref, ddtr_ref, dalog_ref, dbias_ref, dd_ref), (ds_ref,), cargs) = _hosted(comm, 7, 5, refs)
        if comm is not None:
            first, last = _first_last((nc,))

            @pl.when(first)
            def _():
                comm["start"](*cargs)

            @pl.when(last)
            def _():
                comm["finish"](*cargs)

        @pl.when(pl.program_id(0) == 0)
        def _():
            ds_ref[...] = jnp.zeros_like(ds_ref)
            dalog_ref[...] = jnp.zeros_like(dalog_ref)
            dbias_ref[...] = jnp.zeros_like(dbias_ref)
            dd_ref[...] = jnp.zeros_like(dd_ref)

        xs, dt, a, causal, cs, csT, dt_x, ecs_x, dte_x, cdec_x = _ssd_prep(dtr_ref, bias_ref, alog_ref, SW)
        lane = lax.broadcasted_iota(jnp.int32, (L, LANES), 1)
        sub = lax.broadcasted_iota(jnp.int32, (LANES, L), 0)
        left = lane < P
        dcs = jnp.zeros((L, LANES), f32)
        dcs_t = jnp.zeros((LANES, L), f32)
        xds = jnp.zeros((L, LANES), f32)
        dlast = jnp.zeros((1, LANES), f32)
        dD = jnp.zeros((1, LANES), f32)
        for g in range(G):
            gs = slice(g * gw, (g + 1) * gw)
            bsl = slice(SW + g * NS, SW + (g + 1) * NS)
            csl = slice(SW + (G + g) * NS, SW + (G + g + 1) * NS)
            Bb = xbc_ref[:, bsl].astype(bf16)
            Cb = xbc_ref[:, csl].astype(bf16)
            Gm = _dot(Cb, Bb, "nt")
            X = xbc_ref[:, gs]
            Xd = X * dt_x[:, gs]
            Xdb = Xd.astype(bf16)
            XE = Xd * dte_x[:, gs]
            dY = dy_ref[:, gs]
            dYb = dY.astype(bf16)
            Wb = (dY * ecs_x[:, gs]).astype(bf16)
            Sp = sp_ref[0, :, gs]
            Spb = Sp.astype(bf16)
            dS = ds_ref[:, gs]
            dSb = dS.astype(bf16)
            CS = _dot(Cb, Spb)
            Zb = _dot(Bb, dSb)
            dC = _dot(Wb, Spb, "nt")
            dB = _dot(XE.astype(bf16), dSb, "nt")
            ds_ref[:, gs] = cdec_x[:, gs] * dS + _dot(Cb, Wb, "tn")
            R1 = dY * CS * ecs_x[:, gs]
            R2 = XE * Zb
            to_head = (lax.shift_right_logical(lax.broadcasted_iota(jnp.int32, (gw, LANES), 0), HEAD_SHIFT) + g * HPG
                       == lax.broadcasted_iota(jnp.int32, (gw, LANES), 1)).astype(bf16)
            dcs = dcs + _dot_mask(R1 - R2, to_head, 3)
            dlast = (dlast + jnp.sum(_dot_mask(R2, to_head, 3), axis=0, keepdims=True)
                     + jnp.sum(_dot_mask(Sp * dS * cdec_x[:, gs], to_head, 3), axis=0, keepdims=True))
            dG = jnp.zeros((L, L), f32)
            pieces = []
            for pr in range(HPG // 2):
                h0 = g * HPG + 2 * pr
                pw = slice(pr * LANES, (pr + 1) * LANES)
                xp, dyp = Xdb[:, pw], dYb[:, pw]
                halves = []
                for k, h in enumerate((h0, h0 + 1)):
                    Lm = _head_decay(cs, csT, causal, h)
                    Mf = Lm * Gm
                    keep = left if k == 0 else jnp.logical_not(left)
                    dM = _dot(jnp.where(keep, dyp, jnp.zeros_like(dyp)), xp, "nt")
                    Q = dM * Mf
                    dcs = dcs + jnp.where(lane == h, jnp.sum(Q, axis=1, keepdims=True), 0.0)
                    dcs_t = dcs_t - jnp.where(sub == h, jnp.sum(Q, axis=0, keepdims=True), 0.0)
                    dG = dG + dM * Lm
                    halves.append(_dot(Mf.astype(bf16), dyp, "tn"))
                pieces.append(jnp.where(left, halves[0], halves[1]))
            dXd = jnp.concatenate(pieces, axis=1) + dte_x[:, gs] * Zb
            dxbc_ref[:, gs] = dXd * dt_x[:, gs] + dsk_ref[:, gs] * dY
            xds = xds + _dot_mask(dXd * X, to_head, 3)
            dD = dD + jnp.sum(_dot_mask(dY * X, to_head, 3), axis=0, keepdims=True)
            dGb = dG.astype(bf16)
            dxbc_ref[:, bsl] = dB + _dot(dGb, Cb, "tn")
            dxbc_ref[:, csl] = dC + _dot(dGb, Bb)
        rowi = lax.broadcasted_iota(jnp.int32, (L, LANES), 0)
        dcs = dcs + dcs_t.T + jnp.where(rowi == L - 1, dlast, 0.0)
        anti = (lax.broadcasted_iota(jnp.int32, (L, L), 1) >= lax.broadcasted_iota(jnp.int32, (L, L), 0)).astype(bf16)
        dda = _mask_dot(anti, dcs, 3)
        ddt = dda * a + xds
        dalog_ref[...] += jnp.sum(dda * dt, axis=0, keepdims=True) * a
        ddtr = ddt * _sigmoid(xs)
        ddtr_ref[...] = ddtr
        dbias_ref[...] += jnp.sum(ddtr, axis=0, keepdims=True)
        dd_ref[...] += dD

    rev = lambda c: (nc - 1 - c, 0)
    vec = pl.BlockSpec((1, LANES), lambda c: (0, 0))
    outs, got = _host_call(
        body, name, (nc,),
        [pl.BlockSpec((L, xbc.shape[1]), rev), pl.BlockSpec((L, LANES), rev),
         pl.BlockSpec((1, NS, SW), lambda c: (nc - 1 - c, 0, 0)), pl.BlockSpec((L, SW), rev), vec, vec,
         pl.BlockSpec((1, SW), lambda c: (0, 0))],
        [pl.BlockSpec((L, xbc.shape[1]), rev), pl.BlockSpec((L, LANES), rev), vec, vec, vec],
        [jax.ShapeDtypeStruct(xbc.shape, f32), jax.ShapeDtypeStruct((T, LANES), f32)] + [jax.ShapeDtypeStruct((1, LANES), f32)] * 3,
        [pltpu.VMEM((NS, SW), f32)], [xbc, dtraw, sprev, dy, dt_bias, a_log, d_skip_x], comm)
    return (*outs, got)


def _gate_bwd(y, z, dyn, out_norm, name, tr=256):
    T, SW = y.shape
    tr = min(tr, T)
    gsz = SW // SSD_GROUPS

    def body(y_ref, z_ref, d_ref, on_ref, dy_ref, dz_ref, don_ref):
        @pl.when(pl.program_id(0) == 0)
        def _():
            don_ref[...] = jnp.zeros_like(don_ref)

        for g in range(SSD_GROUPS):
            gs = slice(g * gsz, (g + 1) * gsz)
            yv, zv, dv = y_ref[:, gs], z_ref[:, gs], d_ref[:, gs]
            sg = _sigmoid(zv)
            sl = zv * sg
            gated = yv * sl
            rr = lax.rsqrt(jnp.mean(gated * gated, axis=-1, keepdims=True) + EPS)
            gh = gated * rr
            dgn = dv * on_ref[:, gs]
            dgated = rr * (dgn - gh * jnp.mean(dgn * gh, axis=-1, keepdims=True))
            dy_ref[:, gs] = dgated * sl
            dz_ref[:, gs] = (dgated * yv * (sg * (1.0 + zv * (1.0 - sg)))).astype(dz_ref.dtype)
            don_ref[:, gs] += jnp.sum(dv * gh, axis=0, keepdims=True)

    row = pl.BlockSpec((tr, SW), lambda r: (r, 0))
    vec = pl.BlockSpec((1, SW), lambda r: (0, 0))
    return pl.pallas_call(
        body, name=name, grid=(T // tr,), in_specs=[row, row, row, vec], out_specs=[row, row, vec],
        out_shape=[jax.ShapeDtypeStruct((T, SW), f32), jax.ShapeDtypeStruct((T, SW), bf16),
                   jax.ShapeDtypeStruct((1, SW), f32)],
        compiler_params=_cp("arbitrary"),
    )(y, z, dyn, out_norm)


def _qk_norm_fwd(qkv, qn_w, kn_w, SBW, name, tr=256):
    T = qkv.shape[0]
    tr = min(tr, T)
    nh = SBW // SB_HEAD_DIM

    def body(q_ref, k_ref, v_ref, qw_ref, kw_ref, qo_ref, ko_ref, vo_ref):
        for src, w_ref, dst in ((q_ref, qw_ref, qo_ref), (k_ref, kw_ref, ko_ref)):
            for h in range(nh):
                hs = slice(h * SB_HEAD_DIM, (h + 1) * SB_HEAD_DIM)
                sv = src[:, hs]
                rr = lax.rsqrt(jnp.mean(sv * sv, axis=-1, keepdims=True) + EPS)
                dst[:, hs] = (sv * rr * w_ref[...]).astype(dst.dtype)
        vo_ref[...] = v_ref[...].astype(vo_ref.dtype)

    blk = lambda j: pl.BlockSpec((tr, SBW), lambda r: (r, j))
    vec = pl.BlockSpec((1, SB_HEAD_DIM), lambda r: (0, 0))
    out = pl.BlockSpec((tr, SBW), lambda r: (r, 0))
    return pl.pallas_call(
        body, name=name, grid=(T // tr,), in_specs=[blk(0), blk(1), blk(2), vec, vec], out_specs=[out, out, out],
        out_shape=[jax.ShapeDtypeStruct((T, SBW), bf16)] * 3, compiler_params=_cp("parallel"),
    )(qkv, qkv, qkv, qn_w, kn_w)


def _qk_norm_bwd(qkv, dqn, dkn, dv, qn_w, kn_w, SBW, name, tr=256):
    T = qkv.shape[0]
    tr = min(tr, T)
    nh = SBW // SB_HEAD_DIM

    def body(q_ref, k_ref, dq_ref, dk_ref, dv_ref, qw_ref, kw_ref, o_ref, dqw_ref, dkw_ref):
        @pl.when(pl.program_id(0) == 0)
        def _():
            dqw_ref[...] = jnp.zeros_like(dqw_ref)
            dkw_ref[...] = jnp.zeros_like(dkw_ref)

        for part, (src, d_ref, w_ref, dw_ref) in enumerate(((q_ref, dq_ref, qw_ref, dqw_ref), (k_ref, dk_ref, kw_ref, dkw_ref))):
            dw = jnp.zeros((1, SB_HEAD_DIM), f32)
            for h in range(nh):
                hs = slice(h * SB_HEAD_DIM, (h + 1) * SB_HEAD_DIM)
                os_ = slice(part * SBW + h * SB_HEAD_DIM, part * SBW + (h + 1) * SB_HEAD_DIM)
                sv, dn = src[:, hs], d_ref[:, hs]
                rr = lax.rsqrt(jnp.mean(sv * sv, axis=-1, keepdims=True) + EPS)
                xh = sv * rr
                dg = dn * w_ref[...]
                o_ref[:, os_] = (rr * (dg - xh * jnp.mean(dg * xh, axis=-1, keepdims=True))).astype(o_ref.dtype)
                dw = dw + jnp.sum(dn * xh, axis=0, keepdims=True)
            dw_ref[...] += dw
        o_ref[:, 2 * SBW:] = dv_ref[...].astype(o_ref.dtype)

    blk = lambda j: pl.BlockSpec((tr, SBW), lambda r: (r, j))
    vec = pl.BlockSpec((1, SB_HEAD_DIM), lambda r: (0, 0))
    row = pl.BlockSpec((tr, SBW), lambda r: (r, 0))
    return pl.pallas_call(
        body, name=name, grid=(T // tr,), in_specs=[blk(0), blk(1), row, row, row, vec, vec],
        out_specs=[pl.BlockSpec((tr, 3 * SBW), lambda r: (r, 0)), vec, vec],
        out_shape=[jax.ShapeDtypeStruct((T, 3 * SBW), bf16)] + [jax.ShapeDtypeStruct((1, SB_HEAD_DIM), f32)] * 2,
        compiler_params=_cp("arbitrary"),
    )(qkv, qkv, dqn, dkn, dv, qn_w, kn_w)


def _sb_logits(q, kb, scale):
    zl = _dot(q, kb, "nt") * scale
    lb = jnp.minimum(zl, 0.0) - jnp.log(1.0 + jnp.exp(-jnp.abs(zl)))
    return zl, lb, lb - zl


def _tail_update(old, r0, new_tail):
    return new_tail if r0 == 0 else jnp.concatenate([old[:r0], new_tail], axis=0)


def _hosted(comm, n_in, n_out, refs):
    nci, nco, ncs = (len(comm["ins"]), len(comm["outs"]), len(comm["sems"])) if comm is not None else (0, 0, 0)
    ins, outs = refs[:n_in], refs[n_in + nci:n_in + nci + n_out]
    scratch = refs[n_in + nci + n_out + nco:len(refs) - ncs]
    cargs = (refs[n_in:n_in + nci], refs[n_in + nci + n_out:n_in + nci + n_out + nco], refs[len(refs) - ncs:])
    return ins, outs, scratch, cargs


def _host_call(body, name, grid, in_specs, out_specs, out_shape, scratch, operands, comm):
    n_out = len(out_shape)
    in_specs, out_specs, out_shape, scratch, operands = list(in_specs), list(out_specs), list(out_shape), list(scratch), list(operands)
    io = {}
    if comm is not None:
        for src, dst in comm.get("aliases", {}).items():
            io[len(operands) + src] = n_out + dst
        in_specs += [ANY] * len(comm["ins"])
        operands += comm["ins"]
        out_specs += [ANY] * len(comm["outs"])
        out_shape += comm["outs"]
        scratch += comm["sems"]
    res = pl.pallas_call(body, name=name, grid=grid, in_specs=in_specs, out_specs=out_specs, out_shape=out_shape,
                         scratch_shapes=scratch, input_output_aliases=io,
                         compiler_params=_cp(*(("arbitrary",) * len(grid))))(*operands)
    return res[:n_out], res[n_out:]


def _first_last(grid):
    pid = [pl.program_id(d) for d in range(len(grid))]
    first, last = pid[0] == 0, pid[0] == grid[0] - 1
    for d in range(1, len(grid)):
        first, last = first & (pid[d] == 0), last & (pid[d] == grid[d] - 1)
    return first, last


def _sb_fwd(qn, kn, vb, name, comm=None, tq=4096, tk=256):
    T, W = qn.shape
    tq = min(tq, T)
    tk = min(tk, tq)
    nh, nq, dh, nd = W // SB_HEAD_DIM, T // tq, SB_HEAD_DIM, tq // tk
    scale = dh ** -0.5
    grid = (nh, nq)

    def body(*refs):
        (q_ref, k_ref, v_ref), (o_ref, c_ref), _, cargs = _hosted(comm, 3, 2, refs)
        first, last = _first_last(grid)
        if comm is not None:
            @pl.when(first)
            def _():
                comm["start"](*cargs)

        qi = pl.program_id(1)
        q = q_ref[...]
        later = (lax.broadcasted_iota(jnp.int32, (tk, tk), 0) > lax.broadcasted_iota(jnp.int32, (tk, tk), 1)).astype(bf16)

        def step(j, carry, d):
            acc, run = carry
            r0 = 0 if d is None else d * tk
            ks = pl.multiple_of(j * tk, tk)
            kb, vv = k_ref[pl.ds(ks, tk), :], v_ref[pl.ds(ks, tk), :]
            _, lb, lk = _sb_logits(q[r0:], kb, scale)
            if d is not None:
                mask = lax.broadcasted_iota(jnp.int32, lk.shape, 1) < lax.broadcasted_iota(jnp.int32, lk.shape, 0)
                lk = jnp.where(mask, lk, 0.0)
            between = _dot_mask(lk, later, 2)
            w = jnp.exp(lb + between + run[r0:])
            if d is not None:
                w = jnp.where(mask, w, 0.0)
            return (_tail_update(acc, r0, acc[r0:] + _dot(w.astype(bf16), vv)),
                    _tail_update(run, r0, run[r0:] + between[:, 0:1] + lk[:, 0:1]))

        carry = (jnp.zeros((tq, dh), f32), jnp.zeros((tq, 1), f32))
        for d in range(nd - 1, -1, -1):
            carry = step(qi * nd + d, carry, d)
        n_before = qi * nd
        acc, run = lax.fori_loop(0, n_before, lambda t, c: step(n_before - 1 - t, c, None), carry)
        o_ref[...] = acc.astype(o_ref.dtype)
        c_ref[...] = jnp.broadcast_to(run, (tq, dh))
        if comm is not None:
            @pl.when(last)
            def _():
                comm["finish"](*cargs)

    qblk = pl.BlockSpec((tq, dh), lambda h, i: (i, h))
    full = pl.BlockSpec((T, dh), lambda h, i: (0, h))
    (o, c), got = _host_call(body, name, grid, [qblk, full, full], [qblk, qblk],
                             [jax.ShapeDtypeStruct((T, W), bf16), jax.ShapeDtypeStruct((T, W), f32)], [], [qn, kn, vb], comm)
    return o, c, got


def _sb_bwd(qn, kn, vb, do, ctot, do_off, name, comm=None, tq=4096, tk=256):
    T, W = qn.shape
    tq = min(tq, T)
    tk = min(tk, tq)
    nh, nq, dh, nd = W // SB_HEAD_DIM, T // tq, SB_HEAD_DIM, tq // tk
    scale = dh ** -0.5
    ob = do_off // dh
    grid = (nh, nq)

    def body(*refs):
        (q_ref, k_ref, v_ref, do_ref, c_ref), (dq_ref, dk_ref, dv_ref), _, cargs = _hosted(comm, 5, 3, refs)
        first, last = _first_last(grid)
        if comm is not None:
            @pl.when(first)
            def _():
                comm["start"](*cargs)

        qi = pl.program_id(1)

        @pl.when(qi == 0)
        def _():
            dk_ref[...] = jnp.zeros_like(dk_ref)
            dv_ref[...] = jnp.zeros_like(dv_ref)

        q = q_ref[...]
        dob = do_ref[...].astype(bf16)
        total = c_ref[:, 0:1]
        r2 = lax.broadcasted_iota(jnp.int32, (tk, tk), 0)
        c2 = lax.broadcasted_iota(jnp.int32, (tk, tk), 1)
        upto = (r2 <= c2).astype(bf16)
        before = (r2 < c2).astype(bf16)

        def step(j, carry, d):
            dq, pre, gpre = carry
            r0 = 0 if d is None else d * tk
            ks = pl.multiple_of(j * tk, tk)
            kb, vv = k_ref[pl.ds(ks, tk), :], v_ref[pl.ds(ks, tk), :]
            qs, dos = q[r0:], dob[r0:]
            _, lb, lk = _sb_logits(qs, kb, scale)
            if d is not None:
                mask = lax.broadcasted_iota(jnp.int32, lk.shape, 1) < lax.broadcasted_iota(jnp.int32, lk.shape, 0)
                lk = jnp.where(mask, lk, 0.0)
            pin = _dot_mask(lk, upto, 2)
            w = jnp.exp(lb + (total[r0:] - pre[r0:] - pin))
            if d is not None:
                w = jnp.where(mask, w, 0.0)
            dw = _dot(dos, vv, "nt")
            dv_ref[pl.ds(ks, tk), :] += _dot(w.astype(bf16), dos, "tn")
            gg = dw * w
            gex = _dot(gg.astype(bf16), before)
            beta = jnp.exp(lb)
            dz = (gg * (1.0 - beta) - (gpre[r0:] + gex) * beta) * scale
            if d is not None:
                dz = jnp.where(mask, dz, 0.0)
            dzb = dz.astype(bf16)
            dk_ref[pl.ds(ks, tk), :] += _dot(dzb, qs, "tn")
            return (_tail_update(dq, r0, dq[r0:] + _dot(dzb, kb)),
                    _tail_update(pre, r0, pre[r0:] + pin[:, tk - 1:tk]),
                    _tail_update(gpre, r0, gpre[r0:] + gex[:, tk - 1:tk] + gg[:, tk - 1:tk]))

        init = (jnp.zeros((tq, dh), f32), jnp.zeros((tq, 1), f32), jnp.zeros((tq, 1), f32))
        carry = lax.fori_loop(0, qi * nd, lambda t, c: step(t, c, None), init)
        for d in range(nd):
            carry = step(qi * nd + d, carry, d)
        dq_ref[...] = carry[0]
        if comm is not None:
            @pl.when(last)
            def _():
                comm["finish"](*cargs)

    qblk = pl.BlockSpec((tq, dh), lambda h, i: (i, h))
    full = pl.BlockSpec((T, dh), lambda h, i: (0, h))
    (dq, dk, dv), got = _host_call(
        body, name, grid, [qblk, full, full, pl.BlockSpec((tq, dh), lambda h, i: (i, h + ob)), qblk], [qblk, full, full],
        [jax.ShapeDtypeStruct((T, W), f32)] * 3, [], [qn, kn, vb, do, ctot], comm)
    return dq, dk, dv, got


def _pool_select(sums, g):
    return jnp.where(g == 0, sums[0], jnp.where(g == 1, sums[1], jnp.where(g == 2, sums[2], sums[3])))


def _pool_count(g, r, rc, n, cols, off=0):
    t = (r * rc + off + lax.broadcasted_iota(jnp.int32, (n, cols), 0)).astype(f32)
    win = jnp.left_shift(2, g).astype(f32)
    return jnp.minimum(t + 1.0, win)


def _pool_fwd(hp, xres, w, b, scale, name, rc=512, comm=None):
    T, D = hp.shape
    rc = min(rc, T)
    pg = D // len(POOL_WINDOWS)
    grid = (len(POOL_WINDOWS), T // rc)

    def body(*refs):
        (h_ref, p_ref, x_ref, w_ref, b_ref, s_ref), (o_ref, yp_ref, d_ref), _, cargs = _hosted(comm, 6, 3, refs)
        if comm is not None:
            first, last = _first_last(grid)

            @pl.when(first)
            def _():
                comm["start"](*cargs)

            @pl.when(last)
            def _():
                comm["finish"](*cargs)

        g, r = pl.program_id(0), pl.program_id(1)
        cur = h_ref[...]
        halo = jnp.where(r > 0, p_ref[...], 0.0)
        ext = jnp.concatenate([halo, cur], axis=0)
        sums, s = [], ext
        for sh in (1, 2, 4, 8):
            s = s + pltpu.roll(s, sh, 0)
            sums.append(s)
        d = _pool_select(sums, g)[POOL_HALO:] / _pool_count(g, r, rc, rc, pg) - cur
        yp = _dot(d.astype(bf16), w_ref[0]) + b_ref[...]
        yp_ref[...] = yp
        d_ref[...] = d.astype(d_ref.dtype)
        o_ref[...] = x_ref[...] + yp * s_ref[...]

    cur = pl.BlockSpec((rc, pg), lambda g, r: (r, g))
    prev = pl.BlockSpec((POOL_HALO, pg), lambda g, r: (jnp.maximum(r * (rc // POOL_HALO) - 1, 0), g))
    vec = pl.BlockSpec((1, pg), lambda g, r: (0, g))
    (o, yp, d), got = _host_call(
        body, name, grid, [cur, prev, cur, pl.BlockSpec((1, pg, pg), lambda g, r: (g, 0, 0)), vec, vec], [cur, cur, cur],
        [jax.ShapeDtypeStruct((T, D), f32), jax.ShapeDtypeStruct((T, D), f32), jax.ShapeDtypeStruct((T, D), bf16)],
        [], [hp, hp, xres, w, b, scale], comm)
    return o, yp, d, got


def _pool_bwd(dx, yp, d, w, scale, name, rc=512):
    T, D = dx.shape
    rc = min(rc, T)
    pg = D // len(POOL_WINDOWS)
    nr = T // rc

    def body(dx_ref, dn_ref, yp_ref, d_ref, w_ref, s_ref, dh_ref, dw_ref, db_ref, dsc_ref):
        g, r = pl.program_id(0), pl.program_id(1)

        @pl.when(r == 0)
        def _():
            dw_ref[...] = jnp.zeros_like(dw_ref)
            db_ref[...] = jnp.zeros_like(db_ref)
            dsc_ref[...] = jnp.zeros_like(dsc_ref)

        dxv = dx_ref[...]
        dyp = dxv * s_ref[...]
        dsc_ref[...] += jnp.sum(dxv * yp_ref[...], axis=0, keepdims=True)
        db_ref[...] += jnp.sum(dyp, axis=0, keepdims=True)
        dypb = dyp.astype(bf16)
        dw_ref[0] += _dot(d_ref[...], dypb, "tn")
        dd = _dot(dypb, w_ref[0], "nt")
        ddn = _dot((dn_ref[...] * s_ref[...]).astype(bf16), w_ref[0], "nt")
        e = dd / _pool_count(g, r, rc, rc, pg)
        en = jnp.where(r < nr - 1, ddn / _pool_count(g, r, rc, POOL_HALO, pg, off=rc), 0.0)
        ext = jnp.concatenate([e, en], axis=0)
        sums, s = [], ext
        for sh in (1, 2, 4, 8):
            s = s + pltpu.roll(s, rc + POOL_HALO - sh, 0)
            sums.append(s)
        dh_ref[...] = _pool_select(sums, g)[:rc] - dd

    cur = pl.BlockSpec((rc, pg), lambda g, r: (r, g))
    nxt = pl.BlockSpec((POOL_HALO, pg), lambda g, r: (jnp.minimum((r + 1) * (rc // POOL_HALO), T // POOL_HALO - 1), g))
    vec = pl.BlockSpec((1, pg), lambda g, r: (0, g))
    wsp = pl.BlockSpec((1, pg, pg), lambda g, r: (g, 0, 0))
    return pl.pallas_call(
        body, name=name, grid=(len(POOL_WINDOWS), nr), in_specs=[cur, nxt, cur, cur, wsp, vec],
        out_specs=[cur, wsp, vec, vec],
        out_shape=[jax.ShapeDtypeStruct((T, D), f32), jax.ShapeDtypeStruct(w.shape, f32),
                   jax.ShapeDtypeStruct((1, D), f32), jax.ShapeDtypeStruct((1, D), f32)],
        compiler_params=_cp("parallel", "arbitrary"),
    )(dx, dx, yp, d, w, scale)


def _adamw(w, g, m, v, name, tr=256, comm=None):
    R, C = w.shape
    tr = min(tr, R)
    lanes = -(-C // LANES) * LANES
    while tr > 8 and 2 * 8 * tr * lanes * 4 > MM_TILE_BUDGET:
        tr //= 2
    assert R % tr == 0
    grid = (R // tr,)

    def body(*refs):
        (w_ref, g_ref, m_ref, v_ref), (d_ref, mo_ref, vo_ref, go_ref), _, cargs = _hosted(comm, 4, 4, refs)
        if comm is not None:
            first, last = _first_last(grid)

            @pl.when(first)
            def _():
                comm["start"](*cargs)

            @pl.when(last)
            def _():
                comm["finish"](*cargs)

        gv = g_ref[...]
        mn = ADAM_B1 * m_ref[...] + (1.0 - ADAM_B1) * gv
        vn = ADAM_B2 * v_ref[...] + (1.0 - ADAM_B2) * (gv * gv)
        m_hat = mn / (1.0 - ADAM_B1 ** ADAM_STEP)
        v_hat = vn / (1.0 - ADAM_B2 ** ADAM_STEP)
        d_ref[...] = -ADAM_LR * (m_hat / (jnp.sqrt(v_hat) + ADAM_EPS) + ADAM_WD * w_ref[...])
        mo_ref[...] = mn
        vo_ref[...] = vn
        go_ref[...] = gv

    blk = pl.BlockSpec((tr, C), lambda r: (r, 0))
    outs, got = _host_call(body, name, grid, [blk] * 4, [blk] * 4, [jax.ShapeDtypeStruct((R, C), f32)] * 4, [], [w, g, m, v], comm)
    return (*outs, got)


def _pair_sum(g4, recv, name, br=256):
    _, R, C = g4.shape
    hr = R // 2
    br = min(br, hr)
    nb = hr // br

    def body(a_ref, b_ref, o_ref):
        o_ref[...] = (a_ref[...] + b_ref[...]).astype(o_ref.dtype)

    out = pl.BlockSpec((1, br, C), lambda s, i: (s, i, 0))
    return pl.pallas_call(
        body, name=name, grid=(N_CHIPS, nb),
        in_specs=[pl.BlockSpec((1, br, C), lambda s, i: (s, lax.axis_index("c") * nb + i, 0)), out], out_specs=out,
        out_shape=jax.ShapeDtypeStruct((N_CHIPS, hr, C), bf16), compiler_params=_cp("parallel", "parallel"),
    )(g4, recv)


def _chip_sum(g4, recv, pieces, name, br=256):
    _, hr, C = recv.shape
    br = min(br, hr)
    nb = hr // br
    chip = lambda: 2 * lax.axis_index("x") + lax.axis_index("y")

    def body(a_ref, r_ref, b1_ref, b2_ref, b3_ref, o_ref):
        o_ref[...] = (((a_ref[0] + r_ref[0]) + b1_ref[0].astype(f32)) + b2_ref[0].astype(f32)) + b3_ref[0].astype(f32)

    other = lambda k: pl.BlockSpec((1, br, C), lambda i: ((chip() + k) % N_CHIPS, i, 0))
    return pl.pallas_call(
        body, name=name, grid=(nb,),
        in_specs=[pl.BlockSpec((1, br, C), lambda i: (chip(), lax.axis_index("c") * nb + i, 0)),
                  pl.BlockSpec((1, br, C), lambda i: (chip(), i, 0)), other(1), other(2), other(3)],
        out_specs=pl.BlockSpec((br, C), lambda i: (lax.axis_index("c") * nb + i, 0)),
        out_shape=jax.ShapeDtypeStruct((2 * hr, C), f32), compiler_params=_cp("parallel"),
    )(g4, recv, pieces, pieces, pieces)


ANY = pl.BlockSpec(memory_space=pl.ANY)


def _mesh_pos():
    x, y, c = lax.axis_index("x"), lax.axis_index("y"), lax.axis_index("c")
    others = [(1 - x, y), (x, 1 - y), (1 - x, 1 - y)]
    return x, y, c, 2 * x + y, others


def _gather_small(blk, name):
    m, n = blk.shape

    def body(x_ref, out_ref, sum_ref, send_sems, recv_sems, local_sem):
        x, y, c, _, others = _mesh_pos()
        me, sibling = (x, y, c), (x, y, 1 - c)

        def rows(px, py, pc):
            return out_ref.at[pl.ds((4 * px + 2 * py + pc) * m, m), :]

        def copy(k, block, to, src=None):
            return pltpu.make_async_remote_copy(
                src_ref=rows(*block) if src is None else src, dst_ref=rows(*block),
                send_sem=send_sems.at[k], recv_sem=recv_sems.at[k], device_id=to, device_id_type=MESH)

        mine = pltpu.make_async_copy(x_ref, rows(*me), local_sem)
        mine.start()
        first = [copy(0, me, sibling, src=x_ref)]
        first += [copy(1 + j, me, (*chip, c), src=x_ref) for j, chip in enumerate(others)]
        for cp in first:
            cp.start()
        passed = [copy(4 + j, (*chip, c), sibling) for j, chip in enumerate(others)]
        for j, chip in enumerate(others):
            copy(1 + j, (*chip, c), me).wait_recv()
            passed[j].start()
        copy(0, sibling, me).wait_recv()
        for j, chip in enumerate(others):
            copy(4 + j, (*chip, 1 - c), me).wait_recv()
        for cp in first + passed:
            cp.wait_send()
        mine.wait()
        acc = out_ref[0:m, :]
        for d in range(1, 8):
            acc = acc + out_ref[d * m:(d + 1) * m, :]
        sum_ref[...] = acc

    vm = pl.BlockSpec(memory_space=pltpu.VMEM)
    return pl.pallas_call(
        body, name=name, in_specs=[vm], out_specs=[vm, vm],
        out_shape=[jax.ShapeDtypeStruct((8 * m, n), f32), jax.ShapeDtypeStruct((m, n), f32)],
        scratch_shapes=[pltpu.SemaphoreType.DMA((7,)), pltpu.SemaphoreType.DMA((7,)), pltpu.SemaphoreType.DMA],
    )(blk)


def _copy(src, dst, sems, idx, to):
    return pltpu.make_async_remote_copy(src_ref=src, dst_ref=dst, send_sem=sems[0].at[idx], recv_sem=sems[1].at[idx],
                                        device_id=to, device_id_type=MESH)


def _gather_ici(shards):
    nt = len(shards)

    def copies(ins, outs, sems):
        x, y, c, chip, others = _mesh_pos()
        send, land = [], []
        for t in range(nt):
            hr = ins[t].shape[0] // 2
            for j, (px, py) in enumerate(others):
                send.append((ins[t].at[pl.ds(c * hr, hr)], outs[t].at[chip, pl.ds(c * hr, hr)], sems, (t, j), (px, py, c)))
                piece = outs[t].at[2 * px + py, pl.ds(c * hr, hr)]
                land.append((piece, piece, sems, (t, j), (px, py, c)))
        return send, land

    def start(ins, outs, sems):
        for args in copies(ins, outs, sems)[0]:
            _copy(*args).start()

    def finish(ins, outs, sems):
        send, land = copies(ins, outs, sems)
        for args in land:
            _copy(*args).wait_recv()
        for args in send:
            _copy(*args).wait_send()

    return dict(ins=list(shards), outs=[jax.ShapeDtypeStruct((N_CHIPS,) + s.shape, s.dtype) for s in shards],
                sems=[pltpu.SemaphoreType.DMA((nt, 3)), pltpu.SemaphoreType.DMA((nt, 3))], start=start, finish=finish)


def _gather_d2d(stacks):
    nt = len(stacks)

    def copies(ins, outs, sems):
        x, y, c, _, others = _mesh_pos()
        send, land = [], []
        for t in range(nt):
            hr = outs[t].shape[1] // 2
            for j, (px, py) in enumerate(others):
                mine = outs[t].at[2 * px + py, pl.ds(c * hr, hr)]
                theirs = outs[t].at[2 * px + py, pl.ds((1 - c) * hr, hr)]
                send.append((mine, mine, sems, (t, j), (x, y, 1 - c)))
                land.append((theirs, theirs, sems, (t, j), (x, y, 1 - c)))
        return send, land

    def start(ins, outs, sems):
        for args in copies(ins, outs, sems)[0]:
            _copy(*args).start()

    def finish(ins, outs, sems):
        send, land = copies(ins, outs, sems)
        for args in land:
            _copy(*args).wait_recv()
        for args in send:
            _copy(*args).wait_send()

    return dict(ins=list(stacks), outs=[jax.ShapeDtypeStruct(s.shape, s.dtype) for s in stacks],
                sems=[pltpu.SemaphoreType.DMA((nt, 3)), pltpu.SemaphoreType.DMA((nt, 3))], start=start, finish=finish,
                aliases={t: t for t in range(nt)})


def _run_exchange(comm, name):
    ni, no = len(comm["ins"]), len(comm["outs"])

    def body(*refs):
        args = (refs[:ni], refs[ni:ni + no], refs[ni + no:])
        comm["start"](*args)
        comm["finish"](*args)

    return pl.pallas_call(
        body, name=name, in_specs=[ANY] * ni, out_specs=[ANY] * no, out_shape=comm["outs"], scratch_shapes=comm["sems"],
        input_output_aliases=dict(comm.get("aliases", {})))(*comm["ins"])


def _swap_halves(g4s):
    nt = len(g4s)

    def copies(ins, outs, sems):
        x, y, c, _, _ = _mesh_pos()
        both = []
        for t in range(nt):
            hr = ins[t].shape[1] // 2
            both.append((ins[t].at[:, pl.ds((1 - c) * hr, hr)], outs[t], sems, t, (x, y, 1 - c)))
        return both, both

    def start(ins, outs, sems):
        for args in copies(ins, outs, sems)[0]:
            _copy(*args).start()

    def finish(ins, outs, sems):
        send, land = copies(ins, outs, sems)
        for args in land:
            _copy(*args).wait_recv()
        for args in send:
            _copy(*args).wait_send()

    return dict(ins=list(g4s), outs=[jax.ShapeDtypeStruct((N_CHIPS, g.shape[1] // 2, g.shape[2]), g.dtype) for g in g4s],
                sems=[pltpu.SemaphoreType.DMA((nt,)), pltpu.SemaphoreType.DMA((nt,))], start=start, finish=finish)


def _exchange_chips(h4s):
    nt = len(h4s)

    def copies(ins, outs, sems):
        x, y, c, chip, others = _mesh_pos()
        send, land = [], []
        for t in range(nt):
            for j, (px, py) in enumerate(others):
                send.append((ins[t].at[2 * px + py], outs[t].at[chip], sems, (t, j), (px, py, c)))
                landed = outs[t].at[2 * px + py]
                land.append((landed, landed, sems, (t, j), (px, py, c)))
        return send, land

    def start(ins, outs, sems):
        for args in copies(ins, outs, sems)[0]:
            _copy(*args).start()

    def finish(ins, outs, sems):
        send, land = copies(ins, outs, sems)
        for args in land:
            _copy(*args).wait_recv()
        for args in send:
            _copy(*args).wait_send()

    return dict(ins=list(h4s), outs=[jax.ShapeDtypeStruct(h.shape, h.dtype) for h in h4s],
                sems=[pltpu.SemaphoreType.DMA((nt, 3)), pltpu.SemaphoreType.DMA((nt, 3))], start=start, finish=finish)


def _join_halves(fs):
    nt = len(fs)

    def copies(ins, outs, sems):
        x, y, c, _, _ = _mesh_pos()
        send, land = [], []
        for t in range(nt):
            hr = outs[t].shape[0] // 2
            mine, theirs = outs[t].at[pl.ds(c * hr, hr)], outs[t].at[pl.ds((1 - c) * hr, hr)]
            send.append((mine, mine, sems, t, (x, y, 1 - c)))
            land.append((theirs, theirs, sems, t, (x, y, 1 - c)))
        return send, land

    def start(ins, outs, sems):
        for args in copies(ins, outs, sems)[0]:
            _copy(*args).start()

    def finish(ins, outs, sems):
        send, land = copies(ins, outs, sems)
        for args in land:
            _copy(*args).wait_recv()
        for args in send:
            _copy(*args).wait_send()

    return dict(ins=list(fs), outs=[jax.ShapeDtypeStruct(f.shape, f.dtype) for f in fs],
                sems=[pltpu.SemaphoreType.DMA((nt,)), pltpu.SemaphoreType.DMA((nt,))], start=start, finish=finish,
                aliases={t: t for t in range(nt)})


def _pad_lanes(v, n=LANES):
    return jnp.pad(v, ((0, 0), (0, n - v.shape[-1])))


def _mlp_fwd(xin, norm_g, w_up, w_down, F, tag, comms=(None, None)):
    T, D = xin.shape
    h = _rms_fwd(xin, norm_g, bf16, f"{tag}_norm")

    def relu_sq(acc):
        r = jnp.maximum(acc, 0.0)
        return r, r * r

    got = [None, None]
    ua = _mm(h, w_up[0], "nn", T, F, D, (bf16, bf16), f"{tag}_up", epilogue=relu_sq, b_view=w_up[1], comm=comms[0])
    (u, a), got[0] = ua if comms[0] is not None else (ua, None)
    out = _mm(a, w_down[0], "nn", T, D, F, (f32,), f"{tag}_down", epilogue=lambda acc, res: (res + acc,),
              extras=((xin, "tile"),), b_view=w_down[1], comm=comms[1])
    (out,), got[1] = out if comms[1] is not None else ((out,), None)
    return out, (xin, h, u, a), got


def _mlp_bwd(dy, saved, norm_g, w_up, w_down, F, tag, up_to=None, down_to=None, host=None):
    xin, h, u, a = saved
    T, D = xin.shape
    to = lambda t: {} if t is None else dict(out_view=t[0], out_stack=t[1], alias=t[2])
    du = _mm(dy, w_down[0], "nt", T, F, D, (bf16,), f"{tag}_dact", epilogue=lambda acc, uu: (acc * (2.0 * uu.astype(f32)),),
             extras=((u, "tile"),), b_view=w_down[1])
    dw_down = _mm(a, dy, "tn", F, D, T, (f32,), f"{tag}_dwdown", **to(down_to))
    dw_up = _mm(h, du, "tn", D, F, T, (f32,), f"{tag}_dwup", **to(up_to))
    dh = _mm(du, w_up[0], "nt", T, D, F, (f32,), f"{tag}_dh", b_view=w_up[1], comm=host(dw_up, dw_down) if host else None)
    (dh,), got = dh if host else ((dh,), None)
    dx, dg = _rms_bwd(xin, norm_g, dh, dy, f"{tag}_dnorm")
    return dx, dg, dw_up, dw_down, got


def _local_step(xc, tgt, W, HS, SBW, net=None):
    T, D = xc.shape
    SW = HS * SSD_HEAD_DIM
    CD = W["conv_b"].shape[-1]
    mlp_norm = W["mlp_norm"]
    add = lambda acc, prev: (prev + acc,)

    h0 = W["h0"] if "h0" in W else _rms_fwd(xc, W["hyb_norm"], bf16, "hyb_norm")
    z = _mm(h0, W["w_z"], "nn", T, SW, D, (f32,), "proj_z")
    xraw = _mm(h0, W["w_xbc"], "nn", T, CD, D, (f32,), "proj_xbc")
    dtraw = _mm(h0, W["w_dt"], "nn", T, LANES, D, (f32,), "proj_dt")
    qkv = _mm(h0, W["w_qkv"], "nn", T, 3 * SBW, D, (f32,), "proj_qkv")
    qn, kn, vb = _qk_norm_fwd(qkv, W["q_norm"], W["k_norm"], SBW, "qk_norm")
    y_sb, ctot, got = _sb_fwd(qn, kn, vb, "sb_attn", comm=net.rest_ici() if net else None)
    xbc = _conv_fwd(xraw, W["conv_w"], W["conv_b"], "conv")
    y_ssd, yn_ssd, sprev, got = _ssd_fwd(xbc, dtraw, z, W["dt_bias"], W["a_log"], W["d_skip"], W["out_norm"], HS, "ssd",
                                         comm=net.rest_d2d(got) if net else None)
    if net:
        W = {**W, **net.rest_weights(got)}
    w_up, w_down, F = W["w_up"], W["w_down"], W["F"]
    mix = _mm(yn_ssd, W["w_out"], "nn", T, D, SW, (f32,), "out_ssd", epilogue=add, extras=((xc, "tile"),))
    x1 = _mm(y_sb, W["w_out"], "nn", T, D, SBW, (f32,), "out_sb", epilogue=add, extras=((mix, "tile"),), b_off=(SW, 0))
    x2, mlp0, got = _mlp_fwd(x1, mlp_norm[0:1], w_up[0], w_down[0], F, "mlp0",
                             comms=(net.last_ici(0), net.last_ici(1)) if net else (None, None))
    hp = _rms_fwd(x2, W["pool_norm"], f32, "pool_norm")
    x3, yp, dpool, got = _pool_fwd(hp, x2, W["w_pool"], W["pool_b"], W["pool_scale"], "pool",
                                   comm=net.last_d2d(got) if net else None)
    if net:
        w_up, w_down = net.last_weights(got, w_up, w_down)
    x4, mlp1, _ = _mlp_fwd(x3, mlp_norm[1:2], w_up[1], w_down[1], F, "mlp1")

    dy, sq = _loss_grad(x4, tgt, "loss")

    up_to, down_to = (net.mlp_to("up", 1, None), net.mlp_to("down", 1, None)) if net else (None, None)
    dx3, dg_mlp1, dw_up1, dw_down1, _ = _mlp_bwd(dy, mlp1, mlp_norm[1:2], w_up[1], w_down[1], F, "mlp1", up_to, down_to)
    dhp, dw_pool, db_pool, dsc_pool = _pool_bwd(dx3, yp, dpool, W["w_pool"], W["pool_scale"], "pool_bwd")
    dx2, dg_pool = _rms_bwd(x2, W["pool_norm"], dhp, dx3, "pool_dnorm")
    up_to, down_to = (net.mlp_to("up", 0, dw_up1), net.mlp_to("down", 0, dw_down1)) if net else (None, None)
    dx1, dg_mlp0, dw_up0, dw_down0, got_mlp = _mlp_bwd(dx2, mlp0, mlp_norm[0:1], w_up[0], w_down[0], F, "mlp0", up_to, down_to,
                                                       host=net.swap_mlp if net else None)

    dw_out = jnp.concatenate([_mm(yn_ssd, dx1, "tn", SW, D, T, (f32,), "dwout_ssd"),
                              _mm(y_sb, dx1, "tn", SBW, D, T, (f32,), "dwout_sb")], axis=0)
    if net:
        (dmerged,), got = _mm(dx1, W["w_out"], "nt", T, SW + SBW, D, (f32,), "dmerged",
                              comm=net.swap_rest(dw_out, dw_pool))
        dqn, dkn, dvv, got = _sb_bwd(qn, kn, vb, dmerged, ctot, SW, "sb_attn_bwd", comm=net.reduce_early(list(got) + list(got_mlp)))
        net.reduce_early_done(got)
    else:
        dmerged = _mm(dx1, W["w_out"], "nt", T, SW + SBW, D, (f32,), "dmerged")
        dqn, dkn, dvv, _ = _sb_bwd(qn, kn, vb, dmerged, ctot, SW, "sb_attn_bwd")
    dqkv, dg_q, dg_k = _qk_norm_bwd(qkv, dqn, dkn, dvv, W["q_norm"], W["k_norm"], SBW, "qk_norm_bwd")
    dy_ssd, dz, dg_on = _gate_bwd(y_ssd, z, dmerged, W["out_norm"], "gate_bwd")
    dxbc, ddtraw, dalog, dbias, ddskip, got = _ssd_bwd(xbc, dtraw, sprev, dy_ssd, W["dt_bias"], W["a_log"], W["d_skip"], HS, "ssd_bwd",
                                                       comm=net.early_join() if net else None)
    if net:
        net.early_joined(got)
    dpre, dconv_w, dconv_b = _conv_bwd_pre(xraw, dxbc, W["conv_w"], W["conv_b"], "conv_bwd_pre")
    dxraw = _conv_bwd_in(dpre, W["conv_w"], "conv_bwd_in")
    dw_in = [_mm(h0, dz, "tn", D, SW, T, (f32,), "dwin_z"), _mm(h0, dxraw, "tn", D, CD, T, (f32,), "dwin_xbc"),
             _mm(h0, ddtraw, "tn", D, LANES, T, (f32,), "dwin_dt")[:, :HS], _mm(h0, dqkv, "tn", D, 3 * SBW, T, (f32,), "dwin_qkv")]
    dh0 = _mm(dz, W["w_z"], "nt", T, D, SW, (f32,), "dh0_z")
    if net:
        (dh0,), got = _mm(dxraw, W["w_xbc"], "nt", T, D, CD, (f32,), "dh0_xbc", epilogue=add, extras=((dh0, "tile"),),
                          comm=net.late_swap(dw_in))
        dh0 = _mm(ddtraw, W["w_dt"], "nt", T, D, LANES, (f32,), "dh0_dt", epilogue=add, extras=((dh0, "tile"),))
        (dh0,), got = _mm(dqkv, W["w_qkv"], "nt", T, D, 3 * SBW, (f32,), "dh0_qkv", epilogue=add, extras=((dh0, "tile"),),
                          comm=net.reduce_late(got))
        net.late_part_done(0, got)
        grad_x, dg_hyb, got = _rms_bwd(xc, W["hyb_norm"], dh0, dx1, "hyb_dnorm", comm=net.late_part(1))
        net.late_part_done(1, got)
    else:
        dh0 = _mm(dxraw, W["w_xbc"], "nt", T, D, CD, (f32,), "dh0_xbc", epilogue=add, extras=((dh0, "tile"),))
        dh0 = _mm(ddtraw, W["w_dt"], "nt", T, D, LANES, (f32,), "dh0_dt", epilogue=add, extras=((dh0, "tile"),))
        dh0 = _mm(dqkv, W["w_qkv"], "nt", T, D, 3 * SBW, (f32,), "dh0_qkv", epilogue=add, extras=((dh0, "tile"),))
        grad_x, dg_hyb = _rms_bwd(xc, W["hyb_norm"], dh0, dx1, "hyb_dnorm")
    grads = dict(w_in=dw_in, w_out=dw_out, w_pool=dw_pool, w_up=(dw_up0, dw_up1), w_down=(dw_down0, dw_down1),
                 hyb_norm=dg_hyb, conv_w=dconv_w, conv_b=dconv_b, dt_bias=dbias, a_log=dalog, d_skip=ddskip, out_norm=dg_on,
                 q_norm=dg_q, k_norm=dg_k, mlp_norm=(dg_mlp0, dg_mlp1), pool_norm=dg_pool, pool_b=db_pool, pool_scale=dsc_pool)
    return sq, grad_x, grads


def _stack_columns(pieces, n):
    cs = sum(p.shape[1] for p in pieces) // n
    slots = []
    for j in range(n):
        parts, off = [], 0
        for p in pieces:
            lo, hi = max(j * cs, off), min((j + 1) * cs, off + p.shape[1])
            if lo < hi:
                parts.append(p[:, lo - off:hi - off])
            off += p.shape[1]
        slots.append(parts[0] if len(parts) == 1 else jnp.concatenate(parts, axis=1))
    return jnp.stack(slots)


class _Net:
    def __init__(self, own_first, own_last, chip, dims):
        self.own, self.own_last, self.chip, self.dims = own_first, own_last, chip, dims

    def _place_own(self, stacks, own):
        return [lax.dynamic_update_index_in_dim(g, o, self.chip, 0) for g, o in zip(stacks, own)]

    def rest_ici(self):
        return _gather_ici(self.own)

    def rest_d2d(self, got):
        return _gather_d2d(list(got))

    def rest_weights(self, got):
        d, nw = self.dims, len(POOL_WINDOWS)
        D, F, PG = d["D"], d["F"], d["PG"]
        fs = F // N_CHIPS
        g_out, g_pool, g_up, g_down = self._place_own(got, self.own)
        w_pool = g_pool.reshape(N_CHIPS, nw, PG // N_CHIPS, PG).transpose(1, 0, 2, 3).reshape(nw, PG, PG)
        return dict(w_out=g_out.reshape(d["MIX"], D), w_pool=w_pool, F=F,
                    w_up=[(g_up, ("cols", fs, 0, D))], w_down=[(g_down, ("rows", fs, 0, None))])

    def last_ici(self, which):
        return _gather_ici([self.own_last[which]])

    def last_d2d(self, got):
        return _gather_d2d([got[0][0], got[1][0]])

    def last_weights(self, stacks, w_up, w_down):
        d = self.dims
        fs = d["F"] // N_CHIPS
        g_up, g_down = self._place_own(stacks, self.own_last)
        return w_up + [(g_up, ("cols", fs, 0, d["D"]))], w_down + [(g_down, ("rows", fs, 0, None))]

    def mlp_to(self, which, layer, earlier):
        d = self.dims
        fs = d["F"] // N_CHIPS
        if which == "up":
            return ("cols", fs, layer, d["D"]), (N_CHIPS, d["NL"] * d["D"], fs), earlier
        return ("rows", fs, layer, None), (N_CHIPS, d["NL"] * fs, d["D"]), earlier

    def swap_mlp(self, g_up, g_down):
        self.g_mlp = [g_up, g_down]
        return _swap_halves(self.g_mlp)

    def swap_rest(self, dw_out, dw_pool):
        d, nw = self.dims, len(POOL_WINDOWS)
        PG = d["PG"]
        self.early_g4 = [dw_out.reshape(N_CHIPS, d["MIX"] // N_CHIPS, d["D"]),
                         dw_pool.reshape(nw, N_CHIPS, PG // N_CHIPS, PG).transpose(1, 0, 2, 3).reshape(N_CHIPS, PG, PG)] + self.g_mlp
        return _swap_halves(self.early_g4[:2])

    def reduce_early(self, recv):
        self.early_recv = list(recv)
        sent = [_pair_sum(g, r, f"grads_early_pair_sum{i}") for i, (g, r) in enumerate(zip(self.early_g4, self.early_recv))]
        return _exchange_chips(sent)

    def reduce_early_done(self, got):
        self.early_got = list(got)

    def late_swap(self, dw_in):
        self.late_g4 = [_stack_columns(dw_in, N_CHIPS)]
        return _swap_halves(self.late_g4)

    def reduce_late(self, recv):
        self.late_recv = list(recv)
        sent = _pair_sum(self.late_g4[0], self.late_recv[0], "grads_late_pair_sum")
        cut = 5 * sent.shape[1] // 8
        self.late_parts = [sent[:, :cut], sent[:, cut:]]
        self.late_got = [None] * 2
        return self.late_part(0)

    def late_part(self, i):
        return _exchange_chips([self.late_parts[i]])

    def late_part_done(self, i, got):
        self.late_got[i] = got[0]

    def _halves(self, g4s, recvs, pieces, tag):
        return [_chip_sum(g, r, p, f"grads_{tag}_chip_sum{i}") for i, (g, r, p) in enumerate(zip(g4s, recvs, pieces))]

    def early_join(self):
        return _join_halves(self._halves(self.early_g4, self.early_recv, self.early_got, "early"))

    def early_joined(self, got):
        self.early_done = list(got)

    def reduced_early(self):
        return self.early_done

    def reduced_late(self):
        halves = self._halves(self.late_g4, self.late_recv, [jnp.concatenate(self.late_got, axis=1)], "late")
        return _run_exchange(_join_halves(halves), "grads_late_join")[0]


def kernel(x, hyb_norm, hyb_w_in, ssd_conv_w, ssd_conv_b, ssd_dt_bias, ssd_a_log, ssd_d, ssd_out_norm, sb_q_norm, sb_k_norm, hyb_w_out, pool_norm, pool_w, pool_b, pool_scale, mlp_norm, mlp_w_up, mlp_w_down, loss_target, m_hyb_norm, m_hyb_w_in, m_ssd_conv_w, m_ssd_conv_b, m_ssd_dt_bias, m_ssd_a_log, m_ssd_d, m_ssd_out_norm, m_sb_q_norm, m_sb_k_norm, m_hyb_w_out, m_pool_norm, m_pool_w, m_pool_b, m_pool_scale, m_mlp_norm, m_mlp_w_up, m_mlp_w_down, v_hyb_norm, v_hyb_w_in, v_ssd_conv_w, v_ssd_conv_b, v_ssd_dt_bias, v_ssd_a_log, v_ssd_d, v_ssd_out_norm, v_sb_q_norm, v_sb_k_norm, v_hyb_w_out, v_pool_norm, v_pool_w, v_pool_b, v_pool_scale, v_mlp_norm, v_mlp_w_up, v_mlp_w_down):
    T, D = x.shape[1], x.shape[2]
    HS = ssd_dt_bias.shape[-1]
    SW = HS * SSD_HEAD_DIM
    CD = ssd_conv_b.shape[-1]
    IN = N_CHIPS * hyb_w_in.shape[-1]
    SBW = (IN - SW - CD - HS) // 3
    F = N_CHIPS * mlp_w_up.shape[-1]
    NL = mlp_norm.shape[0]
    PG = D // len(POOL_WINDOWS)
    xc, tgt = x[0], loss_target[0]
    ix, iy, ic = lax.axis_index("x"), lax.axis_index("y"), lax.axis_index("c")
    chip = (2 * ix + iy).astype(jnp.int32)

    small = jnp.concatenate([ssd_conv_w.reshape(-1), pool_norm.reshape(-1), pool_b.reshape(-1), pool_scale.reshape(-1)])
    ns = small.shape[0]
    ns8 = -(-ns // (8 * LANES)) * LANES
    gathered, _ = _gather_small(jnp.pad(small, (0, 8 * ns8 - ns)).reshape(8, ns8), "gather_small")
    per_chip = gathered.reshape(N_CHIPS, 2, 8 * ns8)[:, 0, :ns]
    cw = CD // N_CHIPS
    conv_w = per_chip[:, :4 * cw].reshape(N_CHIPS, 4, cw).transpose(1, 0, 2).reshape(4, CD)
    pvec = per_chip[:, 4 * cw:].reshape(N_CHIPS, 3, PG)
    pool_norm_f, pool_b_f, pool_scale_f = (pvec[:, i].reshape(1, D) for i in range(3))

    fs = F // N_CHIPS
    own_in = hyb_w_in[0].astype(bf16)
    arrived = _run_exchange(_gather_ici([own_in]), "gather_in_ici")
    h0, (g_in,) = _rms_fwd(xc, hyb_norm, bf16, "hyb_norm", comm=_gather_d2d(arrived))
    w_in = lax.dynamic_update_index_in_dim(g_in, own_in, chip, 0).transpose(1, 0, 2).reshape(D, IN)
    c1, c2, c3 = SW, SW + CD, SW + CD + HS
    w_z, w_xbc, w_dt, w_qkv = w_in[:, :c1], w_in[:, c1:c2], _pad_lanes(w_in[:, c2:c3]), w_in[:, c3:]
    dt_bias_p, a_log_p, d_skip_p = _pad_lanes(ssd_dt_bias), _pad_lanes(ssd_a_log), jnp.repeat(ssd_d, SSD_HEAD_DIM, axis=-1)

    assert NL == 2
    own_first = [hyb_w_out[0].astype(bf16), pool_w[0].reshape(-1, PG).astype(bf16),
                 mlp_w_up[0].astype(bf16), mlp_w_down[0].astype(bf16)]
    own_last = [mlp_w_up[1].astype(bf16), mlp_w_down[1].astype(bf16)]
    net = _Net(own_first, own_last, chip, dict(D=D, F=F, NL=NL, PG=PG, IN=IN, MIX=SW + SBW))
    first = dict(h0=h0, hyb_norm=hyb_norm, w_z=w_z, w_xbc=w_xbc, w_dt=w_dt, w_qkv=w_qkv, conv_w=conv_w, conv_b=ssd_conv_b,
                 dt_bias=dt_bias_p, a_log=a_log_p, d_skip=d_skip_p, out_norm=ssd_out_norm, q_norm=sb_q_norm, k_norm=sb_k_norm,
                 pool_norm=pool_norm_f, pool_b=pool_b_f, pool_scale=pool_scale_f, mlp_norm=mlp_norm)
    sq, grad_x, gr = _local_step(xc, tgt, first, HS, SBW, net)
    loss = lax.psum(sq[0, 0] * (0.5 / D), ("x", "y", "c"))
    dg_hyb, dconv_b, dbias, dalog, ddskip, dg_on, dg_q, dg_k = (gr[k] for k in (
        "hyb_norm", "conv_b", "dt_bias", "a_log", "d_skip", "out_norm", "q_norm", "k_norm"))
    (dg_mlp0, dg_mlp1), dconv_w, dg_pool, db_pool, dsc_pool = gr["mlp_norm"], gr["conv_w"], gr["pool_norm"], gr["pool_b"], gr["pool_scale"]
    gb_out, gb_pool, gb_up, gb_down = net.reduced_early()

    full_small = [dg_hyb, dconv_b, dbias[:, :HS], dalog[:, :HS], ddskip[:, :HS], dg_on, dg_q, dg_k,
                  jnp.concatenate([dg_mlp0, dg_mlp1], axis=0).reshape(1, -1),
                  dconv_w.reshape(1, -1), dg_pool, db_pool, dsc_pool]
    sizes = [v.shape[-1] for v in full_small]
    packed = jnp.concatenate([v.reshape(-1) for v in full_small])
    npk = packed.shape[0]
    npk8 = -(-npk // (8 * LANES)) * LANES
    _, summed = _gather_small(jnp.pad(packed, (0, 8 * npk8 - npk)).reshape(8, npk8), "grads_small")
    summed = summed.reshape(-1)[:npk]
    offs = [0]
    for s in sizes:
        offs.append(offs[-1] + s)
    (g_hyb_norm, g_conv_b, g_dt_bias, g_a_log, g_d, g_out_norm, g_q_norm, g_k_norm, g_mlp_norm, g_conv_w_full,
     g_pool_norm_full, g_pool_b_full, g_pool_scale_full) = (summed[offs[i]:offs[i + 1]] for i in range(len(sizes)))
    take = lambda full, n: lax.dynamic_slice_in_dim(full.reshape(-1, N_CHIPS, n), chip, 1, axis=1)
    small_grads = {
        "hyb_norm": g_hyb_norm.reshape(hyb_norm.shape), "ssd_conv_w": take(g_conv_w_full, cw).reshape(ssd_conv_w.shape),
        "ssd_conv_b": g_conv_b.reshape(ssd_conv_b.shape), "ssd_dt_bias": g_dt_bias.reshape(ssd_dt_bias.shape),
        "ssd_a_log": g_a_log.reshape(ssd_a_log.shape), "ssd_d": g_d.reshape(ssd_d.shape),
        "ssd_out_norm": g_out_norm.reshape(ssd_out_norm.shape), "sb_q_norm": g_q_norm.reshape(sb_q_norm.shape),
        "sb_k_norm": g_k_norm.reshape(sb_k_norm.shape), "pool_norm": take(g_pool_norm_full, PG).reshape(pool_norm.shape),
        "pool_b": take(g_pool_b_full, PG).reshape(pool_b.shape), "pool_scale": take(g_pool_scale_full, PG).reshape(pool_scale.shape),
        "mlp_norm": g_mlp_norm.reshape(mlp_norm.shape),
    }

    weights = dict(hyb_norm=hyb_norm, hyb_w_in=hyb_w_in, ssd_conv_w=ssd_conv_w, ssd_conv_b=ssd_conv_b, ssd_dt_bias=ssd_dt_bias,
                   ssd_a_log=ssd_a_log, ssd_d=ssd_d, ssd_out_norm=ssd_out_norm, sb_q_norm=sb_q_norm, sb_k_norm=sb_k_norm,
                   hyb_w_out=hyb_w_out, pool_norm=pool_norm, pool_w=pool_w, pool_b=pool_b, pool_scale=pool_scale,
                   mlp_norm=mlp_norm, mlp_w_up=mlp_w_up, mlp_w_down=mlp_w_down)
    moms = dict(hyb_norm=m_hyb_norm, hyb_w_in=m_hyb_w_in, ssd_conv_w=m_ssd_conv_w, ssd_conv_b=m_ssd_conv_b, ssd_dt_bias=m_ssd_dt_bias,
                ssd_a_log=m_ssd_a_log, ssd_d=m_ssd_d, ssd_out_norm=m_ssd_out_norm, sb_q_norm=m_sb_q_norm, sb_k_norm=m_sb_k_norm,
                hyb_w_out=m_hyb_w_out, pool_norm=m_pool_norm, pool_w=m_pool_w, pool_b=m_pool_b, pool_scale=m_pool_scale,
                mlp_norm=m_mlp_norm, mlp_w_up=m_mlp_w_up, mlp_w_down=m_mlp_w_down)
    vels = dict(hyb_norm=v_hyb_norm, hyb_w_in=v_hyb_w_in, ssd_conv_w=v_ssd_conv_w, ssd_conv_b=v_ssd_conv_b, ssd_dt_bias=v_ssd_dt_bias,
                ssd_a_log=v_ssd_a_log, ssd_d=v_ssd_d, ssd_out_norm=v_ssd_out_norm, sb_q_norm=v_sb_q_norm, sb_k_norm=v_sb_k_norm,
                hyb_w_out=v_hyb_w_out, pool_norm=v_pool_norm, pool_w=v_pool_w, pool_b=v_pool_b, pool_scale=v_pool_scale,
                mlp_norm=v_mlp_norm, mlp_w_up=v_mlp_w_up, mlp_w_down=v_mlp_w_down)
    order = list(weights)
    grads, delta, new_m, new_v = {}, {}, {}, {}
    for name, g2 in (("hyb_w_out", gb_out), ("pool_w", gb_pool), ("mlp_w_up", gb_up), ("mlp_w_down", gb_down),
                     ("hyb_w_in", net.reduced_late())):
        shp = weights[name].shape
        d_, m_, v_, g_, _ = _adamw(weights[name].reshape(g2.shape), g2, moms[name].reshape(g2.shape), vels[name].reshape(g2.shape),
                                   f"adamw_{name}")
        grads[name], delta[name], new_m[name], new_v[name] = (t.reshape(shp) for t in (g_, d_, m_, v_))
    snames = list(small_grads)
    pack = lambda d: jnp.concatenate([d[n].reshape(-1) for n in snames])
    nsm = sum(small_grads[n].size for n in snames)
    cols = -(-nsm // (8 * LANES)) * LANES
    as_blk = lambda v: jnp.pad(v, (0, 8 * cols - nsm)).reshape(8, cols)
    padded_v = jnp.pad(pack(vels), (0, 8 * cols - nsm), constant_values=1.0).reshape(8, cols)
    d_, m_, v_, _, _ = _adamw(as_blk(pack(weights)), as_blk(pack(small_grads)), as_blk(pack(moms)), padded_v, "adamw_small")
    off = 0
    for n in snames:
        sz, shp = small_grads[n].size, weights[n].shape
        grads[n] = small_grads[n]
        delta[n], new_m[n], new_v[n] = (t.reshape(-1)[off:off + sz].reshape(shp) for t in (d_, m_, v_))
        off += sz

    return (loss, grad_x.reshape(x.shape), *[grads[n] for n in order], *[delta[n] for n in order],
            *[new_m[n] for n in order], *[new_v[n] for n in order])
```

```python
import functools
import math

import jax
import jax.numpy as jnp
from jax import lax
from jax.experimental import pallas as pl
from jax.experimental.pallas import tpu as pltpu

f32 = jnp.float32
bf16 = jnp.bfloat16

EPS = 1e-6
SSD_HEAD_DIM = 64
SSD_STATE = 128
SSD_GROUPS = 4
SSD_CHUNK = 128
LANES = 128
SB_HEAD_DIM = 128
POOL_WINDOWS = (2, 4, 8, 16)
POOL_HALO = 16
CONV_HALO = 8
ADAM_LR, ADAM_B1, ADAM_B2, ADAM_EPS, ADAM_WD, ADAM_STEP = 0.001, 0.9, 0.999, 1e-08, 0.01, 10
VMEM_LIMIT = 56 * 1024 * 1024
MM_TILE_BUDGET = 40 * 1024 * 1024
N_CHIPS = 4
MESH = pl.DeviceIdType.MESH

_DIMS = {"nn": (((1,), (0,)), ((), ())), "nt": (((1,), (1,)), ((), ())), "tn": (((0,), (0,)), ((), ()))}


def _fit(n, t):
    if n <= t:
        return n
    return max(d for d in range(LANES, t + 1, LANES) if n % d == 0)


def _cp(*sem):
    return pltpu.CompilerParams(dimension_semantics=sem, vmem_limit_bytes=VMEM_LIMIT)


def _sigmoid(v):
    return 1.0 / (1.0 + jnp.exp(-v))


def _softplus(v):
    return jnp.maximum(v, 0.0) + jnp.log(1.0 + jnp.exp(-jnp.abs(v)))


def _split(v, parts):
    out, rem = [], v
    for _ in range(parts):
        p = rem.astype(bf16)
        out.append(p)
        rem = rem - p.astype(f32)
    return out


def _dot(a, b, mode="nn"):
    return lax.dot_general(a, b, _DIMS[mode], preferred_element_type=f32)


def _mask_dot(mask_b, v, parts):
    return _dot(jnp.concatenate([mask_b] * parts, axis=1), jnp.concatenate(_split(v, parts), axis=0))


def _dot_mask(v, mask_b, parts):
    return _dot(jnp.concatenate(_split(v, parts), axis=1), jnp.concatenate([mask_b] * parts, axis=0))


def _stacked(view, br, bc, rmap, cmap):
    kind, per, layer, rows_per_layer = view
    if kind == "cols":
        npc = per // bc
        return pl.BlockSpec((None, br, bc), lambda i, j, k: (cmap(i, j, k) // npc, layer * (rows_per_layer // br) + rmap(i, j, k),
                                                              cmap(i, j, k) % npc))
    npc = per // br
    return pl.BlockSpec((None, br, bc), lambda i, j, k: (rmap(i, j, k) // npc, layer * npc + rmap(i, j, k) % npc, cmap(i, j, k)))


def _pick_tiles(M, N, K, caps, a_bytes, b_bytes, io_bytes):
    def cands(n, cap, sizes):
        got = [s for s in sizes if s <= min(n, cap) and n % s == 0]
        return got or [_fit(n, min(n, cap))]

    best = None
    for tk in cands(K, caps[2], (8192, 4096, 2048, 1024, 512, 256, 128)):
        for tm in cands(M, caps[0], (1024, 512, 256, 128)):
            for tn in cands(N, caps[1], (1024, 512, 256, 128)):
                need = 2 * (tm * tk * a_bytes + tk * tn * b_bytes) + tm * tn * (2 * io_bytes + (4 if tk < K else 0))
                key = (need <= MM_TILE_BUDGET, tk, tm * tn, tm)
                if best is None or key > best[0]:
                    best = (key, (tm, tn, tk))
    return best[1]


def _mm(a, b, mode, M, N, K, outs, name, epilogue=None, extras=(), a_off=(0, 0), b_off=(0, 0),
        b_view=None, out_view=None, out_stack=None, alias=None, comm=None):
    caps = [M, N, K]
    if b_view is not None:
        caps[1 if (b_view[0] == "cols") == (mode != "nt") else 2] = b_view[1]
    if out_view is not None:
        d = 1 if out_view[0] == "cols" else 0
        caps[d] = min(caps[d], out_view[1])
    for off, dims in ((a_off, (2, 0) if mode == "tn" else (0, 2)), (b_off, (1, 2) if mode == "nt" else (2, 1))):
        for o, d in zip(off, dims):
            if o:
                caps[d] = min(caps[d], math.gcd(o, caps[d]))
    io_bytes = sum(jnp.dtype(dt).itemsize for dt in outs) + sum(e[0].dtype.itemsize for e in extras if e[1] == "tile")
    tm, tn, tk = _pick_tiles(M, N, K, caps, a.dtype.itemsize, b.dtype.itemsize, io_bytes)
    nk = K // tk
    if mode == "tn":
        a_blk, ad = (tk, tm), (tk, tm)
    else:
        a_blk, ad = (tm, tk), (tm, tk)
    b_blk = (tn, tk) if mode == "nt" else (tk, tn)
    assert a_off[0] % ad[0] == 0 and a_off[1] % ad[1] == 0 and b_off[0] % b_blk[0] == 0 and b_off[1] % b_blk[1] == 0
    ao = (a_off[0] // ad[0], a_off[1] // ad[1])
    bo = (b_off[0] // b_blk[0], b_off[1] // b_blk[1])
    if mode == "tn":
        a_map = lambda i, j, k: (k + ao[0], i + ao[1])
    else:
        a_map = lambda i, j, k: (i + ao[0], k + ao[1])
    if mode == "nt":
        b_map = lambda i, j, k: (j + bo[0], k + bo[1])
    else:
        b_map = lambda i, j, k: (k + bo[0], j + bo[1])
    if b_view is not None:
        if mode == "nt":
            b_spec = _stacked(b_view, tn, tk, lambda i, j, k: j, lambda i, j, k: k)
        else:
            b_spec = _stacked(b_view, tk, tn, lambda i, j, k: k, lambda i, j, k: j)
    else:
        b_spec = pl.BlockSpec(b_blk, b_map)
    in_specs = [pl.BlockSpec(a_blk, a_map), b_spec]
    for arr, kind in extras:
        if kind == "tile":
            in_specs.append(pl.BlockSpec((tm, tn), lambda i, j, k: (i, j)))
        else:
            in_specs.append(pl.BlockSpec((1, tn), lambda i, j, k: (0, j)))
    ne, no = len(extras), len(outs)
    if epilogue is None:
        epilogue = lambda acc: (acc,)
    operands = [a, b, *[e[0] for e in extras]]
    aliases = {}
    if alias is not None:
        in_specs.append(ANY)
        aliases[len(operands)] = 0
        operands.append(alias)
    n_in = len(operands)
    if out_view is not None:
        out_specs = [_stacked(out_view, tm, tn, lambda i, j, k: i, lambda i, j, k: j)]
        out_shape = [jax.ShapeDtypeStruct(out_stack, outs[0])]
    else:
        out_specs = [pl.BlockSpec((tm, tn), lambda i, j, k: (i, j)) for _ in outs]
        out_shape = [jax.ShapeDtypeStruct((M, N), dt) for dt in outs]
    scratch = [pltpu.VMEM((tm, tn), f32)] if nk > 1 else []
    grid = (M // tm, N // tn, nk)
    if comm is not None:
        in_specs += [ANY] * len(comm["ins"])
        operands += comm["ins"]
        out_specs += [ANY] * len(comm["outs"])
        out_shape += comm["outs"]
        scratch += comm["sems"]
    nci, nco, ncs = (len(comm["ins"]), len(comm["outs"]), len(comm["sems"])) if comm is not None else (0, 0, 0)

    def body(*refs):
        a_ref, b_ref = refs[0], refs[1]
        ex, out_refs = refs[2:2 + ne], refs[n_in + nci:n_in + nci + no]
        rest = refs[n_in + nci + no + nco:]
        if comm is not None:
            cargs = (refs[n_in:n_in + nci], refs[n_in + nci + no:n_in + nci + no + nco], refs[len(refs) - ncs:])
            pid = [pl.program_id(d) for d in range(3)]

            @pl.when((pid[0] == 0) & (pid[1] == 0) & (pid[2] == 0))
            def _():
                comm["start"](*cargs)

        def finish(acc):
            res = epilogue(acc, *[e[...] for e in ex])
            for o, r in zip(out_refs, res):
                o[...] = r.astype(o.dtype)

        prod = lax.dot_general(a_ref[...].astype(bf16), b_ref[...].astype(bf16), _DIMS[mode],
                               preferred_element_type=f32)
        if nk == 1:
            finish(prod)
        else:
            acc_ref = rest[0]
            k = pl.program_id(2)

            @pl.when(k == 0)
            def _():
                acc_ref[...] = prod

            @pl.when(k > 0)
            def _():
                acc_ref[...] += prod

            @pl.when(k == nk - 1)
            def _():
                finish(acc_ref[...])

        if comm is not None:
            @pl.when((pid[0] == grid[0] - 1) & (pid[1] == grid[1] - 1) & (pid[2] == grid[2] - 1))
            def _():
                comm["finish"](*cargs)

    sem = ("arbitrary",) * 3 if comm is not None else ("parallel", "parallel", "arbitrary")
    res = pl.pallas_call(
        body, name=name, grid=grid, in_specs=in_specs, out_specs=out_specs, out_shape=out_shape,
        scratch_shapes=scratch, input_output_aliases=aliases, compiler_params=_cp(*sem),
    )(*operands)
    if comm is not None:
        return res[:no], res[no:]
    return res[0] if no == 1 else res


def _rms_fwd(x, g, out_dtype, name, tr=256, comm=None):
    T, D = x.shape
    tr = min(tr, T)
    grid = (T // tr,)

    def body(*refs):
        (x_ref, g_ref), (o_ref,), _, cargs = _hosted(comm, 2, 1, refs)
        if comm is not None:
            first, last = _first_last(grid)

            @pl.when(first)
            def _():
                comm["start"](*cargs)

            @pl.when(last)
            def _():
                comm["finish"](*cargs)

        xv = x_ref[...]
        r = lax.rsqrt(jnp.mean(xv * xv, axis=-1, keepdims=True) + EPS)
        o_ref[...] = (xv * r * g_ref[...]).astype(o_ref.dtype)

    (o,), got = _host_call(body, name, grid, [pl.BlockSpec((tr, D), lambda r: (r, 0)), pl.BlockSpec((1, D), lambda r: (0, 0))],
                           [pl.BlockSpec((tr, D), lambda r: (r, 0))], [jax.ShapeDtypeStruct((T, D), out_dtype)], [], [x, g], comm)
    return o if comm is None else (o, got)


def _rms_bwd(x, g, dh, dres, name, tr=256, comm=None, with_bf16=False):
    T, D = x.shape
    tr = min(tr, T)
    grid = (T // tr,)

    def body(*refs):
        (x_ref, g_ref, dh_ref, dres_ref), outs, _, cargs = _hosted(comm, 4, 3 if with_bf16 else 2, refs)
        dx_ref, dg_ref = outs[0], outs[1]
        if comm is not None:
            first, last = _first_last(grid)

            @pl.when(first)
            def _():
                comm["start"](*cargs)

            @pl.when(last)
            def _():
                comm["finish"](*cargs)

        xv = x_ref[...]
        r = lax.rsqrt(jnp.mean(xv * xv, axis=-1, keepdims=True) + EPS)
        xh = xv * r
        dhv = dh_ref[...]
        dhg = dhv * g_ref[...]
        dxv = dres_ref[...] + r * (dhg - xh * jnp.mean(dhg * xh, axis=-1, keepdims=True))
        dx_ref[...] = dxv
        if with_bf16:
            outs[2][...] = dxv.astype(bf16)

        @pl.when(pl.program_id(0) == 0)
        def _():
            dg_ref[...] = jnp.zeros_like(dg_ref)

        dg_ref[...] += jnp.sum(dhv * xh, axis=0, keepdims=True)

    row = pl.BlockSpec((tr, D), lambda r: (r, 0))
    vec = pl.BlockSpec((1, D), lambda r: (0, 0))
    extra = ([row], [jax.ShapeDtypeStruct((T, D), bf16)]) if with_bf16 else ([], [])
    res, got = _host_call(body, name, grid, [row, vec, row, row], [row, vec] + extra[0],
                          [jax.ShapeDtypeStruct((T, D), f32), jax.ShapeDtypeStruct((1, D), f32)] + extra[1], [], [x, g, dh, dres], comm)
    return tuple(res) if comm is None else (*res, got)


def _loss_grad(y, tgt, name, tr=256):
    T, D = y.shape
    tr = min(tr, T)

    def body(y_ref, t_ref, dy_ref, dyb_ref, s_ref):
        e = y_ref[...] - t_ref[...]
        dy_ref[...] = e * (1.0 / D)
        dyb_ref[...] = (e * (1.0 / D)).astype(bf16)

        @pl.when(pl.program_id(0) == 0)
        def _():
            s_ref[...] = jnp.zeros_like(s_ref)

        s_ref[...] += jnp.sum(e * e)

    row = pl.BlockSpec((tr, D), lambda r: (r, 0))
    return pl.pallas_call(
        body, name=name, grid=(T // tr,), in_specs=[row, row],
        out_specs=[row, row, pl.BlockSpec((8, LANES), lambda r: (0, 0))],
        out_shape=[jax.ShapeDtypeStruct((T, D), f32), jax.ShapeDtypeStruct((T, D), bf16), jax.ShapeDtypeStruct((8, LANES), f32)],
        compiler_params=_cp("arbitrary"),
    )(y, tgt)


def _shift_down(cur, prev, s):
    rolled = pltpu.roll(cur, s, 0)
    top = pltpu.roll(prev, s, 0)
    row = lax.broadcasted_iota(jnp.int32, top.shape, 0)
    head = jnp.where(row < s, top, rolled[0:CONV_HALO])
    return jnp.concatenate([head, rolled[CONV_HALO:]], axis=0)


def _shift_up(cur, nxt, s):
    n = cur.shape[0]
    rolled = pltpu.roll(cur, n - s, 0)
    bot = pltpu.roll(nxt, CONV_HALO - s, 0)
    row = lax.broadcasted_iota(jnp.int32, bot.shape, 0)
    tail = jnp.where(row >= CONV_HALO - s, bot, rolled[n - CONV_HALO:])
    return jnp.concatenate([rolled[:n - CONV_HALO], tail], axis=0)


def _conv_pre(cur, prev, w_ref, b_ref):
    taps = [cur] + [_shift_down(cur, prev, s) for s in (1, 2, 3)]
    pre = b_ref[...] + w_ref[3:4, :] * taps[0]
    for s in (1, 2, 3):
        pre = pre + w_ref[3 - s:4 - s, :] * taps[s]
    return pre, taps


def _conv_specs(T, C, rc, cb):
    cur = pl.BlockSpec((rc, cb), lambda j, r: (r, j))
    prev = pl.BlockSpec((CONV_HALO, cb), lambda j, r: (jnp.maximum(r * (rc // CONV_HALO) - 1, 0), j))
    nxt = pl.BlockSpec((CONV_HALO, cb), lambda j, r: (jnp.minimum((r + 1) * (rc // CONV_HALO), T // CONV_HALO - 1), j))
    w = pl.BlockSpec((4, cb), lambda j, r: (0, j))
    b = pl.BlockSpec((1, cb), lambda j, r: (0, j))
    return cur, prev, nxt, w, b


def _conv_fwd(xraw, w, b, name):
    T, C = xraw.shape
    rc, cb = min(512, T), min(512, C)
    cur, prev, _, ws, bs = _conv_specs(T, C, rc, cb)

    def body(x_ref, p_ref, w_ref, b_ref, o_ref):
        pv = jnp.where(pl.program_id(1) > 0, p_ref[...], 0.0)
        pre, _ = _conv_pre(x_ref[...], pv, w_ref, b_ref)
        o_ref[...] = pre * _sigmoid(pre)

    return pl.pallas_call(
        body, name=name, grid=(C // cb, T // rc), in_specs=[cur, prev, ws, bs], out_specs=cur,
        out_shape=jax.ShapeDtypeStruct((T, C), f32), compiler_params=_cp("parallel", "parallel"),
    )(xraw, xraw, w, b)


def _conv_bwd_pre(xraw, dxbc, w, b, name):
    T, C = xraw.shape
    rc, cb = min(512, T), min(512, C)
    cur, prev, _, ws, bs = _conv_specs(T, C, rc, cb)

    def body(x_ref, p_ref, d_ref, w_ref, b_ref, dpre_ref, dw_ref, db_ref):
        pv = jnp.where(pl.program_id(1) > 0, p_ref[...], 0.0)
        pre, taps = _conv_pre(x_ref[...], pv, w_ref, b_ref)
        sg = _sigmoid(pre)
        dpre = d_ref[...] * (sg * (1.0 + pre * (1.0 - sg)))
        dpre_ref[...] = dpre

        @pl.when(pl.program_id(1) == 0)
        def _():
            dw_ref[...] = jnp.zeros_like(dw_ref)
            db_ref[...] = jnp.zeros_like(db_ref)

        row = lax.broadcasted_iota(jnp.int32, dw_ref.shape, 0)
        upd = jnp.zeros(dw_ref.shape, f32)
        for s in range(4):
            upd = upd + jnp.where(row == 3 - s, jnp.sum(dpre * taps[s], axis=0, keepdims=True), 0.0)
        dw_ref[...] += upd
        db_ref[...] += jnp.sum(dpre, axis=0, keepdims=True)

    return pl.pallas_call(
        body, name=name, grid=(C // cb, T // rc), in_specs=[cur, prev, cur, ws, bs], out_specs=[cur, ws, bs],
        out_shape=[jax.ShapeDtypeStruct((T, C), f32), jax.ShapeDtypeStruct((4, C), f32), jax.ShapeDtypeStruct((1, C), f32)],
        compiler_params=_cp("parallel", "arbitrary"),
    )(xraw, xraw, dxbc, w, b)


def _conv_bwd_in(dpre, w, name):
    T, C = dpre.shape
    rc, cb = min(512, T), min(512, C)
    cur, _, nxt, ws, _ = _conv_specs(T, C, rc, cb)
    nr = T // rc

    def body(d_ref, n_ref, w_ref, o_ref):
        nv = jnp.where(pl.program_id(1) < nr - 1, n_ref[...], 0.0)
        cv = d_ref[...]
        out = w_ref[3:4, :] * cv
        for s in (1, 2, 3):
            out = out + w_ref[3 - s:4 - s, :] * _shift_up(cv, nv, s)
        o_ref[...] = out.astype(o_ref.dtype)

    return pl.pallas_call(
        body, name=name, grid=(C // cb, nr), in_specs=[cur, nxt, ws], out_specs=cur,
        out_shape=jax.ShapeDtypeStruct((T, C), bf16), compiler_params=_cp("parallel", "parallel"),
    )(dpre, dpre, w)


HEAD_SHIFT = SSD_HEAD_DIM.bit_length() - 1


def _ssd_prep(dtr_ref, bias_ref, alog_ref, SW):
    L = SSD_CHUNK
    xs = dtr_ref[...] + bias_ref[...]
    dt = _softplus(xs)
    a = -jnp.exp(alog_ref[...])
    causal = lax.broadcasted_iota(jnp.int32, (L, L), 0) >= lax.broadcasted_iota(jnp.int32, (L, L), 1)
    cs = _mask_dot(causal.astype(bf16), dt * a, 3)
    spread = (lax.broadcasted_iota(jnp.int32, (LANES, SW), 0)
              == lax.shift_right_logical(lax.broadcasted_iota(jnp.int32, (LANES, SW), 1), HEAD_SHIFT)).astype(bf16)
    dt_x = _dot_mask(dt, spread, 3)
    cs_x = _dot_mask(cs, spread, 3)
    last_x = cs_x[L - 1:L, :]
    return xs, dt, a, causal, cs, cs.T, dt_x, jnp.exp(cs_x), jnp.exp(last_x - cs_x), jnp.exp(last_x)


def _head_decay(cs, csT, causal, h):
    seg = cs[:, h:h + 1] - csT[h:h + 1, :]
    return jnp.where(causal, jnp.exp(jnp.minimum(seg, 0.0)), 0.0)


def _ssd_fwd(xbc, dtraw, z, dt_bias, a_log, d_skip_x, out_norm, HS, name, comm=None):
    T = xbc.shape[0]
    L, P, NS, G = SSD_CHUNK, SSD_HEAD_DIM, SSD_STATE, SSD_GROUPS
    SW, HPG, nc = HS * P, HS // SSD_GROUPS, T // SSD_CHUNK
    gsz = SW // G
    gw = HPG * P
    assert HPG % 2 == 0 and 2 * P == LANES

    def body(*refs):
        (xbc_ref, dtr_ref, z_ref, bias_ref, alog_ref, dsk_ref, on_ref), (y_ref, yn_ref, sp_ref), (st_ref,), cargs = _hosted(comm, 7, 3, refs)
        first, last = _first_last((nc,))
        if comm is not None:
            @pl.when(first)
            def _():
                comm["start"](*cargs)

            @pl.when(last)
            def _():
                comm["finish"](*cargs)

        @pl.when(pl.program_id(0) == 0)
        def _():
            st_ref[...] = jnp.zeros_like(st_ref)

        sp_ref[0] = st_ref[...]
        _, _, _, causal, cs, csT, dt_x, ecs_x, dte_x, cdec_x = _ssd_prep(dtr_ref, bias_ref, alog_ref, SW)
        X = xbc_ref[:, 0:SW]
        Xd = X * dt_x
        Xdb = Xd.astype(bf16)
        XEb = (Xd * dte_x).astype(bf16)
        left = lax.broadcasted_iota(jnp.int32, (L, LANES), 1) < P
        for g in range(G):
            gs = slice(g * gw, (g + 1) * gw)
            Bb = xbc_ref[:, SW + g * NS:SW + (g + 1) * NS].astype(bf16)
            Cb = xbc_ref[:, SW + (G + g) * NS:SW + (G + g + 1) * NS].astype(bf16)
            Gm = _dot(Cb, Bb, "nt")
            Sp = st_ref[:, gs]
            yo = _dot(Cb, Sp.astype(bf16)) * ecs_x[:, gs]
            st_ref[:, gs] = cdec_x[:, gs] * Sp + _dot(Bb, XEb[:, gs], "tn")
            for pr in range(HPG // 2):
                h0 = g * HPG + 2 * pr
                ps = slice(h0 * P, (h0 + 2) * P)
                xp = Xdb[:, ps]
                yd = jnp.where(left, _dot((_head_decay(cs, csT, causal, h0) * Gm).astype(bf16), xp),
                               _dot((_head_decay(cs, csT, causal, h0 + 1) * Gm).astype(bf16), xp))
                y_ref[:, ps] = yd + yo[:, pr * LANES:(pr + 1) * LANES] + dsk_ref[:, ps] * X[:, ps]
        zz = z_ref[...]
        gated = y_ref[...] * (zz * _sigmoid(zz))
        for g in range(G):
            gs = slice(g * gsz, (g + 1) * gsz)
            sg = gated[:, gs]
            rr = lax.rsqrt(jnp.mean(sg * sg, axis=-1, keepdims=True) + EPS)
            yn_ref[:, gs] = (sg * rr * on_ref[:, gs]).astype(yn_ref.dtype)

    vec = pl.BlockSpec((1, LANES), lambda c: (0, 0))
    wide = pl.BlockSpec((1, SW), lambda c: (0, 0))
    (y, yn, sp), got = _host_call(
        body, name, (nc,),
        [pl.BlockSpec((L, xbc.shape[1]), lambda c: (c, 0)), pl.BlockSpec((L, LANES), lambda c: (c, 0)),
         pl.BlockSpec((L, SW), lambda c: (c, 0)), vec, vec, wide, wide],
        [pl.BlockSpec((L, SW), lambda c: (c, 0)), pl.BlockSpec((L, SW), lambda c: (c, 0)),
         pl.BlockSpec((1, NS, SW), lambda c: (c, 0, 0))],
        [jax.ShapeDtypeStruct((T, SW), f32), jax.ShapeDtypeStruct((T, SW), bf16), jax.ShapeDtypeStruct((nc, NS, SW), f32)],
        [pltpu.VMEM((NS, SW), f32)], [xbc, dtraw, z, dt_bias, a_log, d_skip_x, out_norm], comm)
    return y, yn, sp, got


def _ssd_bwd(xbc, dtraw, sprev, dy, dt_bias, a_log, d_skip_x, HS, name, comm=None):
    T = xbc.shape[0]
    L, P, NS, G = SSD_CHUNK, SSD_HEAD_DIM, SSD_STATE, SSD_GROUPS
    SW, HPG, nc = HS * P, HS // SSD_GROUPS, T // SSD_CHUNK
    gw = HPG * P

    def body(*refs):
        ((xbc_ref, dtr_ref, sp_ref, dy_ref, bias_ref, alog_ref, dsk_ref),
         (dxbc_ref, ddtr_ref, dalog_ref, dbias_ref, dd_ref), (ds_ref,), cargs) = _hosted(comm, 7, 5, refs)
        if comm is not None:
            first, last = _first_last((nc,))

            @pl.when(first)
            def _():
                comm["start"](*cargs)

            @pl.when(last)
            def _():
                comm["finish"](*cargs)

        @pl.when(pl.program_id(0) == 0)
        def _():
            ds_ref[...] = jnp.zeros_like(ds_ref)
            dalog_ref[...] = jnp.zeros_like(dalog_ref)
            dbias_ref[...] = jnp.zeros_like(dbias_ref)
            dd_ref[...] = jnp.zeros_like(dd_ref)

        xs, dt, a, causal, cs, csT, dt_x, ecs_x, dte_x, cdec_x = _ssd_prep(dtr_ref, bias_ref, alog_ref, SW)
        lane = lax.broadcasted_iota(jnp.int32, (L, LANES), 1)
        sub = lax.broadcasted_iota(jnp.int32, (LANES, L), 0)
        left = lane < P
        dcs = jnp.zeros((L, LANES), f32)
        dcs_t = jnp.zeros((LANES, L), f32)
        xds = jnp.zeros((L, LANES), f32)
        dlast = jnp.zeros((1, LANES), f32)
        dD = jnp.zeros((1, LANES), f32)
        for g in range(G):
            gs = slice(g * gw, (g + 1) * gw)
            bsl = slice(SW + g * NS, SW + (g + 1) * NS)
            csl = slice(SW + (G + g) * NS, SW + (G + g + 1) * NS)
            Bb = xbc_ref[:, bsl].astype(bf16)
            Cb = xbc_ref[:, csl].astype(bf16)
            Gm = _dot(Cb, Bb, "nt")
            X = xbc_ref[:, gs]
            Xd = X * dt_x[:, gs]
            Xdb = Xd.astype(bf16)
            XE = Xd * dte_x[:, gs]
            dY = dy_ref[:, gs]
            dYb = dY.astype(bf16)
            Wb = (dY * ecs_x[:, gs]).astype(bf16)
            Sp = sp_ref[0, :, gs]
            Spb = Sp.astype(bf16)
            dS = ds_ref[:, gs]
            dSb = dS.astype(bf16)
            CS = _dot(Cb, Spb)
            Zb = _dot(Bb, dSb)
            dC = _dot(Wb, Spb, "nt")
            dB = _dot(XE.astype(bf16), dSb, "nt")
            ds_ref[:, gs] = cdec_x[:, gs] * dS + _dot(Cb, Wb, "tn")
            R1 = dY * CS * ecs_x[:, gs]
            R2 = XE * Zb
            to_head = (lax.shift_right_logical(lax.broadcasted_iota(jnp.int32, (gw, LANES), 0), HEAD_SHIFT) + g * HPG
                       == lax.broadcasted_iota(jnp.int32, (gw, LANES), 1)).astype(bf16)
            dcs = dcs + _dot_mask(R1 - R2, to_head, 3)
            dlast = (dlast + jnp.sum(_dot_mask(R2, to_head, 3), axis=0, keepdims=True)
                     + jnp.sum(_dot_mask(Sp * dS * cdec_x[:, gs], to_head, 3), axis=0, keepdims=True))
            dG = jnp.zeros((L, L), f32)
            pieces = []
            for pr in range(HPG // 2):
                h0 = g * HPG + 2 * pr
                pw = slice(pr * LANES, (pr + 1) * LANES)
                xp, dyp = Xdb[:, pw], dYb[:, pw]
                halves = []
                for k, h in enumerate((h0, h0 + 1)):
                    Lm = _head_decay(cs, csT, causal, h)
                    Mf = Lm * Gm
                    keep = left if k == 0 else jnp.logical_not(left)
                    dM = _dot(jnp.where(keep, dyp, jnp.zeros_like(dyp)), xp, "nt")
                    Q = dM * Mf
                    dcs = dcs + jnp.where(lane == h, jnp.sum(Q, axis=1, keepdims=True), 0.0)
                    dcs_t = dcs_t - jnp.where(sub == h, jnp.sum(Q, axis=0, keepdims=True), 0.0)
                    dG = dG + dM * Lm
                    halves.append(_dot(Mf.astype(bf16), dyp, "tn"))
                pieces.append(jnp.where(left, halves[0], halves[1]))
            dXd = jnp.concatenate(pieces, axis=1) + dte_x[:, gs] * Zb
            dxbc_ref[:, gs] = dXd * dt_x[:, gs] + dsk_ref[:, gs] * dY
            xds = xds + _dot_mask(dXd * X, to_head, 3)
            dD = dD + jnp.sum(_dot_mask(dY * X, to_head, 3), axis=0, keepdims=True)
            dGb = dG.astype(bf16)
            dxbc_ref[:, bsl] = dB + _dot(dGb, Cb, "tn")
            dxbc_ref[:, csl] = dC + _dot(dGb, Bb)
        rowi = lax.broadcasted_iota(jnp.int32, (L, LANES), 0)
        dcs = dcs + dcs_t.T + jnp.where(rowi == L - 1, dlast, 0.0)
        anti = (lax.broadcasted_iota(jnp.int32, (L, L), 1) >= lax.broadcasted_iota(jnp.int32, (L, L), 0)).astype(bf16)
        dda = _mask_dot(anti, dcs, 3)
        ddt = dda * a + xds
        dalog_ref[...] += jnp.sum(dda * dt, axis=0, keepdims=True) * a
        ddtr = ddt * _sigmoid(xs)
        ddtr_ref[...] = ddtr
        dbias_ref[...] += jnp.sum(ddtr, axis=0, keepdims=True)
        dd_ref[...] += dD

    rev = lambda c: (nc - 1 - c, 0)
    vec = pl.BlockSpec((1, LANES), lambda c: (0, 0))
    outs, got = _host_call(
        body, name, (nc,),
        [pl.BlockSpec((L, xbc.shape[1]), rev), pl.BlockSpec((L, LANES), rev),
         pl.BlockSpec((1, NS, SW), lambda c: (nc - 1 - c, 0, 0)), pl.BlockSpec((L, SW), rev), vec, vec,
         pl.BlockSpec((1, SW), lambda c: (0, 0))],
        [pl.BlockSpec((L, xbc.shape[1]), rev), pl.BlockSpec((L, LANES), rev), vec, vec, vec],
        [jax.ShapeDtypeStruct(xbc.shape, f32), jax.ShapeDtypeStruct((T, LANES), f32)] + [jax.ShapeDtypeStruct((1, LANES), f32)] * 3,
        [pltpu.VMEM((NS, SW), f32)], [xbc, dtraw, sprev, dy, dt_bias, a_log, d_skip_x], comm)
    return (*outs, got)


def _gate_bwd(y, z, dyn, out_norm, name, tr=256):
    T, SW = y.shape
    tr = min(tr, T)
    gsz = SW // SSD_GROUPS

    def body(y_ref, z_ref, d_ref, on_ref, dy_ref, dz_ref, don_ref):
        @pl.when(pl.program_id(0) == 0)
        def _():
            don_ref[...] = jnp.zeros_like(don_ref)

        for g in range(SSD_GROUPS):
            gs = slice(g * gsz, (g + 1) * gsz)
            yv, zv, dv = y_ref[:, gs], z_ref[:, gs], d_ref[:, gs]
            sg = _sigmoid(zv)
            sl = zv * sg
            gated = yv * sl
            rr = lax.rsqrt(jnp.mean(gated * gated, axis=-1, keepdims=True) + EPS)
            gh = gated * rr
            dgn = dv * on_ref[:, gs]
            dgated = rr * (dgn - gh * jnp.mean(dgn * gh, axis=-1, keepdims=True))
            dy_ref[:, gs] = dgated * sl
            dz_ref[:, gs] = (dgated * yv * (sg * (1.0 + zv * (1.0 - sg)))).astype(dz_ref.dtype)
            don_ref[:, gs] += jnp.sum(dv * gh, axis=0, keepdims=True)

    row = pl.BlockSpec((tr, SW), lambda r: (r, 0))
    vec = pl.BlockSpec((1, SW), lambda r: (0, 0))
    return pl.pallas_call(
        body, name=name, grid=(T // tr,), in_specs=[row, row, row, vec], out_specs=[row, row, vec],
        out_shape=[jax.ShapeDtypeStruct((T, SW), f32), jax.ShapeDtypeStruct((T, SW), bf16),
                   jax.ShapeDtypeStruct((1, SW), f32)],
        compiler_params=_cp("arbitrary"),
    )(y, z, dyn, out_norm)


def _qk_norm_fwd(qkv, qn_w, kn_w, SBW, name, tr=256):
    T = qkv.shape[0]
    tr = min(tr, T)
    nh = SBW // SB_HEAD_DIM

    def body(q_ref, k_ref, v_ref, qw_ref, kw_ref, qo_ref, ko_ref, vo_ref):
        for src, w_ref, dst in ((q_ref, qw_ref, qo_ref), (k_ref, kw_ref, ko_ref)):
            for h in range(nh):
                hs = slice(h * SB_HEAD_DIM, (h + 1) * SB_HEAD_DIM)
                sv = src[:, hs]
                rr = lax.rsqrt(jnp.mean(sv * sv, axis=-1, keepdims=True) + EPS)
                dst[:, hs] = (sv * rr * w_ref[...]).astype(dst.dtype)
        vo_ref[...] = v_ref[...].astype(vo_ref.dtype)

    blk = lambda j: pl.BlockSpec((tr, SBW), lambda r: (r, j))
    vec = pl.BlockSpec((1, SB_HEAD_DIM), lambda r: (0, 0))
    out = pl.BlockSpec((tr, SBW), lambda r: (r, 0))
    return pl.pallas_call(
        body, name=name, grid=(T // tr,), in_specs=[blk(0), blk(1), blk(2), vec, vec], out_specs=[out, out, out],
        out_shape=[jax.ShapeDtypeStruct((T, SBW), bf16)] * 3, compiler_params=_cp("parallel"),
    )(qkv, qkv, qkv, qn_w, kn_w)


def _qk_norm_bwd(qkv, dqn, dkn, dv, qn_w, kn_w, SBW, name, tr=256):
    T = qkv.shape[0]
    tr = min(tr, T)
    nh = SBW // SB_HEAD_DIM

    def body(q_ref, k_ref, dq_ref, dk_ref, dv_ref, qw_ref, kw_ref, o_ref, dqw_ref, dkw_ref):
        @pl.when(pl.program_id(0) == 0)
        def _():
            dqw_ref[...] = jnp.zeros_like(dqw_ref)
            dkw_ref[...] = jnp.zeros_like(dkw_ref)

        for part, (src, d_ref, w_ref, dw_ref) in enumerate(((q_ref, dq_ref, qw_ref, dqw_ref), (k_ref, dk_ref, kw_ref, dkw_ref))):
            dw = jnp.zeros((1, SB_HEAD_DIM), f32)
            for h in range(nh):
                hs = slice(h * SB_HEAD_DIM, (h + 1) * SB_HEAD_DIM)
                os_ = slice(part * SBW + h * SB_HEAD_DIM, part * SBW + (h + 1) * SB_HEAD_DIM)
                sv, dn = src[:, hs], d_ref[:, hs]
                rr = lax.rsqrt(jnp.mean(sv * sv, axis=-1, keepdims=True) + EPS)
                xh = sv * rr
                dg = dn * w_ref[...]
                o_ref[:, os_] = (rr * (dg - xh * jnp.mean(dg * xh, axis=-1, keepdims=True))).astype(o_ref.dtype)
                dw = dw + jnp.sum(dn * xh, axis=0, keepdims=True)
            dw_ref[...] += dw
        o_ref[:, 2 * SBW:] = dv_ref[...].astype(o_ref.dtype)

    blk = lambda j: pl.BlockSpec((tr, SBW), lambda r: (r, j))
    vec = pl.BlockSpec((1, SB_HEAD_DIM), lambda r: (0, 0))
    row = pl.BlockSpec((tr, SBW), lambda r: (r, 0))
    return pl.pallas_call(
        body, name=name, grid=(T // tr,), in_specs=[blk(0), blk(1), row, row, row, vec, vec],
        out_specs=[pl.BlockSpec((tr, 3 * SBW), lambda r: (r, 0)), vec, vec],
        out_shape=[jax.ShapeDtypeStruct((T, 3 * SBW), bf16)] + [jax.ShapeDtypeStruct((1, SB_HEAD_DIM), f32)] * 2,
        compiler_params=_cp("arbitrary"),
    )(qkv, qkv, dqn, dkn, dv, qn_w, kn_w)


def _sb_logits(q, kb, scale):
    zl = _dot(q, kb, "nt") * scale
    lb = jnp.minimum(zl, 0.0) - jnp.log(1.0 + jnp.exp(-jnp.abs(zl)))
    return zl, lb, lb - zl


def _tail_update(old, r0, new_tail):
    return new_tail if r0 == 0 else jnp.concatenate([old[:r0], new_tail], axis=0)


def _hosted(comm, n_in, n_out, refs):
    nci, nco, ncs = (len(comm["ins"]), len(comm["outs"]), len(comm["sems"])) if comm is not None else (0, 0, 0)
    ins, outs = refs[:n_in], refs[n_in + nci:n_in + nci + n_out]
    scratch = refs[n_in + nci + n_out + nco:len(refs) - ncs]
    cargs = (refs[n_in:n_in + nci], refs[n_in + nci + n_out:n_in + nci + n_out + nco], refs[len(refs) - ncs:])
    return ins, outs, scratch, cargs


def _host_call(body, name, grid, in_specs, out_specs, out_shape, scratch, operands, comm):
    n_out = len(out_shape)
    in_specs, out_specs, out_shape, scratch, operands = list(in_specs), list(out_specs), list(out_shape), list(scratch), list(operands)
    io = {}
    if comm is not None:
        for src, dst in comm.get("aliases", {}).items():
            io[len(operands) + src] = n_out + dst
        in_specs += [ANY] * len(comm["ins"])
        operands += comm["ins"]
        out_specs += [ANY] * len(comm["outs"])
        out_shape += comm["outs"]
        scratch += comm["sems"]
    res = pl.pallas_call(body, name=name, grid=grid, in_specs=in_specs, out_specs=out_specs, out_shape=out_shape,
                         scratch_shapes=scratch, input_output_aliases=io,
                         compiler_params=_cp(*(("arbitrary",) * len(grid))))(*operands)
    return res[:n_out], res[n_out:]


def _first_last(grid):
    pid = [pl.program_id(d) for d in range(len(grid))]
    first, last = pid[0] == 0, pid[0] == grid[0] - 1
    for d in range(1, len(grid)):
        first, last = first & (pid[d] == 0), last & (pid[d] == grid[d] - 1)
    return first, last


def _sb_fwd(qn, kn, vb, name, comm=None, tq=4096, tk=256):
    T, W = qn.shape
    tq = min(tq, T)
    tk = min(tk, tq)
    nh, nq, dh, nd = W // SB_HEAD_DIM, T // tq, SB_HEAD_DIM, tq // tk
    scale = dh ** -0.5
    grid = (nh, nq)

    def body(*refs):
        (q_ref, k_ref, v_ref), (o_ref, c_ref), _, cargs = _hosted(comm, 3, 2, refs)
        first, last = _first_last(grid)
        if comm is not None:
            @pl.when(first)
            def _():
                comm["start"](*cargs)

        qi = pl.program_id(1)
        q = q_ref[...]
        later = (lax.broadcasted_iota(jnp.int32, (tk, tk), 0) > lax.broadcasted_iota(jnp.int32, (tk, tk), 1)).astype(bf16)

        def step(j, carry, d):
            acc, run = carry
            r0 = 0 if d is None else d * tk
            ks = pl.multiple_of(j * tk, tk)
            kb, vv = k_ref[pl.ds(ks, tk), :], v_ref[pl.ds(ks, tk), :]
            _, lb, lk = _sb_logits(q[r0:], kb, scale)
            if d is not None:
                mask = lax.broadcasted_iota(jnp.int32, lk.shape, 1) < lax.broadcasted_iota(jnp.int32, lk.shape, 0)
                lk = jnp.where(mask, lk, 0.0)
            between = _dot_mask(lk, later, 2)
            w = jnp.exp(lb + between + run[r0:])
            if d is not None:
                w = jnp.where(mask, w, 0.0)
            return (_tail_update(acc, r0, acc[r0:] + _dot(w.astype(bf16), vv)),
                    _tail_update(run, r0, run[r0:] + between[:, 0:1] + lk[:, 0:1]))

        carry = (jnp.zeros((tq, dh), f32), jnp.zeros((tq, 1), f32))
        for d in range(nd - 1, -1, -1):
            carry = step(qi * nd + d, carry, d)
        n_before = qi * nd
        acc, run = lax.fori_loop(0, n_before, lambda t, c: step(n_before - 1 - t, c, None), carry)
        o_ref[...] = acc.astype(o_ref.dtype)
        c_ref[...] = jnp.broadcast_to(run, (tq, dh))
        if comm is not None:
            @pl.when(last)
            def _():
                comm["finish"](*cargs)

    qblk = pl.BlockSpec((tq, dh), lambda h, i: (i, h))
    full = pl.BlockSpec((T, dh), lambda h, i: (0, h))
    (o, c), got = _host_call(body, name, grid, [qblk, full, full], [qblk, qblk],
                             [jax.ShapeDtypeStruct((T, W), bf16), jax.ShapeDtypeStruct((T, W), f32)], [], [qn, kn, vb], comm)
    return o, c, got


def _sb_bwd(qn, kn, vb, do, ctot, do_off, name, comm=None, tq=4096, tk=256):
    T, W = qn.shape
    tq = min(tq, T)
    tk = min(tk, tq)
    nh, nq, dh, nd = W // SB_HEAD_DIM, T // tq, SB_HEAD_DIM, tq // tk
    scale = dh ** -0.5
    ob = do_off // dh
    grid = (nh, nq)

    def body(*refs):
        (q_ref, k_ref, v_ref, do_ref, c_ref), (dq_ref, dk_ref, dv_ref), _, cargs = _hosted(comm, 5, 3, refs)
        first, last = _first_last(grid)
        if comm is not None:
            @pl.when(first)
            def _():
                comm["start"](*cargs)

        qi = pl.program_id(1)

        @pl.when(qi == 0)
        def _():
            dk_ref[...] = jnp.zeros_like(dk_ref)
            dv_ref[...] = jnp.zeros_like(dv_ref)

        q = q_ref[...]
        dob = do_ref[...].astype(bf16)
        total = c_ref[:, 0:1]
        r2 = lax.broadcasted_iota(jnp.int32, (tk, tk), 0)
        c2 = lax.broadcasted_iota(jnp.int32, (tk, tk), 1)
        upto = (r2 <= c2).astype(bf16)
        before = (r2 < c2).astype(bf16)

        def step(j, carry, d):
            dq, pre, gpre = carry
            r0 = 0 if d is None else d * tk
            ks = pl.multiple_of(j * tk, tk)
            kb, vv = k_ref[pl.ds(ks, tk), :], v_ref[pl.ds(ks, tk), :]
            qs, dos = q[r0:], dob[r0:]
            _, lb, lk = _sb_logits(qs, kb, scale)
            if d is not None:
                mask = lax.broadcasted_iota(jnp.int32, lk.shape, 1) < lax.broadcasted_iota(jnp.int32, lk.shape, 0)
                lk = jnp.where(mask, lk, 0.0)
            pin = _dot_mask(lk, upto, 2)
            w = jnp.exp(lb + (total[r0:] - pre[r0:] - pin))
            if d is not None:
                w = jnp.where(mask, w, 0.0)
            dw = _dot(dos, vv, "nt")
            dv_ref[pl.ds(ks, tk), :] += _dot(w.astype(bf16), dos, "tn")
            gg = dw * w
            gex = _dot(gg.astype(bf16), before)
            beta = jnp.exp(lb)
            dz = (gg * (1.0 - beta) - (gpre[r0:] + gex) * beta) * scale
            if d is not None:
                dz = jnp.where(mask, dz, 0.0)
            dzb = dz.astype(bf16)
            dk_ref[pl.ds(ks, tk), :] += _dot(dzb, qs, "tn")
            return (_tail_update(dq, r0, dq[r0:] + _dot(dzb, kb)),
                    _tail_update(pre, r0, pre[r0:] + pin[:, tk - 1:tk]),
                    _tail_update(gpre, r0, gpre[r0:] + gex[:, tk - 1:tk] + gg[:, tk - 1:tk]))

        init = (jnp.zeros((tq, dh), f32), jnp.zeros((tq, 1), f32), jnp.zeros((tq, 1), f32))
        carry = lax.fori_loop(0, qi * nd, lambda t, c: step(t, c, None), init)
        for d in range(nd):
            carry = step(qi * nd + d, carry, d)
        dq_ref[...] = carry[0]
        if comm is not None:
            @pl.when(last)
            def _():
                comm["finish"](*cargs)

    qblk = pl.BlockSpec((tq, dh), lambda h, i: (i, h))
    full = pl.BlockSpec((T, dh), lambda h, i: (0, h))
    (dq, dk, dv), got = _host_call(
        body, name, grid, [qblk, full, full, pl.BlockSpec((tq, dh), lambda h, i: (i, h + ob)), qblk], [qblk, full, full],
        [jax.ShapeDtypeStruct((T, W), f32)] * 3, [], [qn, kn, vb, do, ctot], comm)
    return dq, dk, dv, got


def _pool_select(sums, g):
    return jnp.where(g == 0, sums[0], jnp.where(g == 1, sums[1], jnp.where(g == 2, sums[2], sums[3])))


def _pool_count(g, r, rc, n, cols, off=0):
    t = (r * rc + off + lax.broadcasted_iota(jnp.int32, (n, cols), 0)).astype(f32)
    win = jnp.left_shift(2, g).astype(f32)
    return jnp.minimum(t + 1.0, win)


def _pool_fwd(hp, xres, w, b, scale, name, rc=512, comm=None):
    T, D = hp.shape
    rc = min(rc, T)
    pg = D // len(POOL_WINDOWS)
    grid = (len(POOL_WINDOWS), T // rc)

    def body(*refs):
        (h_ref, p_ref, x_ref, w_ref, b_ref, s_ref), (o_ref, yp_ref, d_ref), _, cargs = _hosted(comm, 6, 3, refs)
        if comm is not None:
            first, last = _first_last(grid)

            @pl.when(first)
            def _():
                comm["start"](*cargs)

            @pl.when(last)
            def _():
                comm["finish"](*cargs)

        g, r = pl.program_id(0), pl.program_id(1)
        cur = h_ref[...]
        halo = jnp.where(r > 0, p_ref[...], 0.0)
        ext = jnp.concatenate([halo, cur], axis=0)
        sums, s = [], ext
        for sh in (1, 2, 4, 8):
            s = s + pltpu.roll(s, sh, 0)
            sums.append(s)
        d = _pool_select(sums, g)[POOL_HALO:] / _pool_count(g, r, rc, rc, pg) - cur
        yp = _dot(d.astype(bf16), w_ref[0]) + b_ref[...]
        yp_ref[...] = yp
        d_ref[...] = d.astype(d_ref.dtype)
        o_ref[...] = x_ref[...] + yp * s_ref[...]

    cur = pl.BlockSpec((rc, pg), lambda g, r: (r, g))
    prev = pl.BlockSpec((POOL_HALO, pg), lambda g, r: (jnp.maximum(r * (rc // POOL_HALO) - 1, 0), g))
    vec = pl.BlockSpec((1, pg), lambda g, r: (0, g))
    (o, yp, d), got = _host_call(
        body, name, grid, [cur, prev, cur, pl.BlockSpec((1, pg, pg), lambda g, r: (g, 0, 0)), vec, vec], [cur, cur, cur],
        [jax.ShapeDtypeStruct((T, D), f32), jax.ShapeDtypeStruct((T, D), f32), jax.ShapeDtypeStruct((T, D), bf16)],
        [], [hp, hp, xres, w, b, scale], comm)
    return o, yp, d, got


def _pool_bwd(dx, yp, d, w, scale, name, rc=512):
    T, D = dx.shape
    rc = min(rc, T)
    pg = D // len(POOL_WINDOWS)
    nr = T // rc

    def body(dx_ref, dn_ref, yp_ref, d_ref, w_ref, s_ref, dh_ref, dw_ref, db_ref, dsc_ref):
        g, r = pl.program_id(0), pl.program_id(1)

        @pl.when(r == 0)
        def _():
            dw_ref[...] = jnp.zeros_like(dw_ref)
            db_ref[...] = jnp.zeros_like(db_ref)
            dsc_ref[...] = jnp.zeros_like(dsc_ref)

        dxv = dx_ref[...]
        dyp = dxv * s_ref[...]
        dsc_ref[...] += jnp.sum(dxv * yp_ref[...], axis=0, keepdims=True)
        db_ref[...] += jnp.sum(dyp, axis=0, keepdims=True)
        dypb = dyp.astype(bf16)
        dw_ref[0] += _dot(d_ref[...], dypb, "tn")
        dd = _dot(dypb, w_ref[0], "nt")
        ddn = _dot((dn_ref[...] * s_ref[...]).astype(bf16), w_ref[0], "nt")
        e = dd / _pool_count(g, r, rc, rc, pg)
        en = jnp.where(r < nr - 1, ddn / _pool_count(g, r, rc, POOL_HALO, pg, off=rc), 0.0)
        ext = jnp.concatenate([e, en], axis=0)
        sums, s = [], ext
        for sh in (1, 2, 4, 8):
            s = s + pltpu.roll(s, rc + POOL_HALO - sh, 0)
            sums.append(s)
        dh_ref[...] = _pool_select(sums, g)[:rc] - dd

    cur = pl.BlockSpec((rc, pg), lambda g, r: (r, g))
    nxt = pl.BlockSpec((POOL_HALO, pg), lambda g, r: (jnp.minimum((r + 1) * (rc // POOL_HALO), T // POOL_HALO - 1), g))
    vec = pl.BlockSpec((1, pg), lambda g, r: (0, g))
    wsp = pl.BlockSpec((1, pg, pg), lambda g, r: (g, 0, 0))
    return pl.pallas_call(
        body, name=name, grid=(len(POOL_WINDOWS), nr), in_specs=[cur, nxt, cur, cur, wsp, vec],
        out_specs=[cur, wsp, vec, vec],
        out_shape=[jax.ShapeDtypeStruct((T, D), f32), jax.ShapeDtypeStruct(w.shape, f32),
                   jax.ShapeDtypeStruct((1, D), f32), jax.ShapeDtypeStruct((1, D), f32)],
        compiler_params=_cp("parallel", "arbitrary"),
    )(dx, dx, yp, d, w, scale)


def _adamw(w, g, m, v, name, tr=256, comm=None):
    R, C = w.shape
    tr = min(tr, R)
    lanes = -(-C // LANES) * LANES
    while tr > 8 and 2 * 8 * tr * lanes * 4 > MM_TILE_BUDGET:
        tr //= 2
    assert R % tr == 0
    grid = (R // tr,)

    def body(*refs):
        (w_ref, g_ref, m_ref, v_ref), (d_ref, mo_ref, vo_ref, go_ref), _, cargs = _hosted(comm, 4, 4, refs)
        if comm is not None:
            first, last = _first_last(grid)

            @pl.when(first)
            def _():
                comm["start"](*cargs)

            @pl.when(last)
            def _():
                comm["finish"](*cargs)

        gv = g_ref[...]
        mn = ADAM_B1 * m_ref[...] + (1.0 - ADAM_B1) * gv
        vn = ADAM_B2 * v_ref[...] + (1.0 - ADAM_B2) * (gv * gv)
        m_hat = mn / (1.0 - ADAM_B1 ** ADAM_STEP)
        v_hat = vn / (1.0 - ADAM_B2 ** ADAM_STEP)
        d_ref[...] = -ADAM_LR * (m_hat / (jnp.sqrt(v_hat) + ADAM_EPS) + ADAM_WD * w_ref[...])
        mo_ref[...] = mn
        vo_ref[...] = vn
        go_ref[...] = gv

    blk = pl.BlockSpec((tr, C), lambda r: (r, 0))
    outs, got = _host_call(body, name, grid, [blk] * 4, [blk] * 4, [jax.ShapeDtypeStruct((R, C), f32)] * 4, [], [w, g, m, v], comm)
    return (*outs, got)


def _pair_sum(g4, recv, name, br=256):
    _, R, C = g4.shape
    hr = R // 2
    br = min(br, hr)
    nb = hr // br

    def body(a_ref, b_ref, o_ref):
        o_ref[...] = (a_ref[...] + b_ref[...]).astype(o_ref.dtype)

    out = pl.BlockSpec((1, br, C), lambda s, i: (s, i, 0))
    return pl.pallas_call(
        body, name=name, grid=(N_CHIPS, nb),
        in_specs=[pl.BlockSpec((1, br, C), lambda s, i: (s, lax.axis_index("c") * nb + i, 0)), out], out_specs=out,
        out_shape=jax.ShapeDtypeStruct((N_CHIPS, hr, C), bf16), compiler_params=_cp("parallel", "parallel"),
    )(g4, recv)


def _chip_sum(g4, recv, pieces, name, br=256):
    _, hr, C = recv.shape
    br = min(br, hr)
    nb = hr // br
    chip = lambda: 2 * lax.axis_index("x") + lax.axis_index("y")

    def body(a_ref, r_ref, b1_ref, b2_ref, b3_ref, o_ref):
        o_ref[...] = (((a_ref[0] + r_ref[0]) + b1_ref[0].astype(f32)) + b2_ref[0].astype(f32)) + b3_ref[0].astype(f32)

    other = lambda k: pl.BlockSpec((1, br, C), lambda i: ((chip() + k) % N_CHIPS, i, 0))
    return pl.pallas_call(
        body, name=name, grid=(nb,),
        in_specs=[pl.BlockSpec((1, br, C), lambda i: (chip(), lax.axis_index("c") * nb + i, 0)),
                  pl.BlockSpec((1, br, C), lambda i: (chip(), i, 0)), other(1), other(2), other(3)],
        out_specs=pl.BlockSpec((br, C), lambda i: (lax.axis_index("c") * nb + i, 0)),
        out_shape=jax.ShapeDtypeStruct((2 * hr, C), f32), compiler_params=_cp("parallel"),
    )(g4, recv, pieces, pieces, pieces)


ANY = pl.BlockSpec(memory_space=pl.ANY)


def _mesh_pos():
    x, y, c = lax.axis_index("x"), lax.axis_index("y"), lax.axis_index("c")
    others = [(1 - x, y), (x, 1 - y), (1 - x, 1 - y)]
    return x, y, c, 2 * x + y, others


def _gather_small(blk, name):
    m, n = blk.shape

    def body(x_ref, out_ref, sum_ref, send_sems, recv_sems, local_sem):
        x, y, c, _, others = _mesh_pos()
        me, sibling = (x, y, c), (x, y, 1 - c)

        def rows(px, py, pc):
            return out_ref.at[pl.ds((4 * px + 2 * py + pc) * m, m), :]

        def copy(k, block, to, src=None):
            return pltpu.make_async_remote_copy(
                src_ref=rows(*block) if src is None else src, dst_ref=rows(*block),
                send_sem=send_sems.at[k], recv_sem=recv_sems.at[k], device_id=to, device_id_type=MESH)

        mine = pltpu.make_async_copy(x_ref, rows(*me), local_sem)
        mine.start()
        first = [copy(0, me, sibling, src=x_ref)]
        first += [copy(1 + j, me, (*chip, c), src=x_ref) for j, chip in enumerate(others)]
        for cp in first:
            cp.start()
        passed = [copy(4 + j, (*chip, c), sibling) for j, chip in enumerate(others)]
        for j, chip in enumerate(others):
            copy(1 + j, (*chip, c), me).wait_recv()
            passed[j].start()
        copy(0, sibling, me).wait_recv()
        for j, chip in enumerate(others):
            copy(4 + j, (*chip, 1 - c), me).wait_recv()
        for cp in first + passed:
            cp.wait_send()
        mine.wait()
        acc = out_ref[0:m, :]
        for d in range(1, 8):
            acc = acc + out_ref[d * m:(d + 1) * m, :]
        sum_ref[...] = acc

    vm = pl.BlockSpec(memory_space=pltpu.VMEM)
    return pl.pallas_call(
        body, name=name, in_specs=[vm], out_specs=[vm, vm],
        out_shape=[jax.ShapeDtypeStruct((8 * m, n), f32), jax.ShapeDtypeStruct((m, n), f32)],
        scratch_shapes=[pltpu.SemaphoreType.DMA((7,)), pltpu.SemaphoreType.DMA((7,)), pltpu.SemaphoreType.DMA],
    )(blk)


def _copy(src, dst, sems, idx, to):
    return pltpu.make_async_remote_copy(src_ref=src, dst_ref=dst, send_sem=sems[0].at[idx], recv_sem=sems[1].at[idx],
                                        device_id=to, device_id_type=MESH)


def _gather_ici(shards):
    nt = len(shards)

    def copies(ins, outs, sems):
        x, y, c, chip, others = _mesh_pos()
        send, land = [], []
        for t in range(nt):
            hr = ins[t].shape[0] // 2
            for j, (px, py) in enumerate(others):
                send.append((ins[t].at[pl.ds(c * hr, hr)], outs[t].at[chip, pl.ds(c * hr, hr)], sems, (t, j), (px, py, c)))
                piece = outs[t].at[2 * px + py, pl.ds(c * hr, hr)]
                land.append((piece, piece, sems, (t, j), (px, py, c)))
        return send, land

    def start(ins, outs, sems):
        for args in copies(ins, outs, sems)[0]:
            _copy(*args).start()

    def finish(ins, outs, sems):
        send, land = copies(ins, outs, sems)
        for args in land:
            _copy(*args).wait_recv()
        for args in send:
            _copy(*args).wait_send()

    return dict(ins=list(shards), outs=[jax.ShapeDtypeStruct((N_CHIPS,) + s.shape, s.dtype) for s in shards],
                sems=[pltpu.SemaphoreType.DMA((nt, 3)), pltpu.SemaphoreType.DMA((nt, 3))], start=start, finish=finish)


def _gather_d2d(stacks):
    nt = len(stacks)

    def copies(ins, outs, sems):
        x, y, c, _, others = _mesh_pos()
        send, land = [], []
        for t in range(nt):
            hr = outs[t].shape[1] // 2
            for j, (px, py) in enumerate(others):
                mine = outs[t].at[2 * px + py, pl.ds(c * hr, hr)]
                theirs = outs[t].at[2 * px + py, pl.ds((1 - c) * hr, hr)]
                send.append((mine, mine, sems, (t, j), (x, y, 1 - c)))
                land.append((theirs, theirs, sems, (t, j), (x, y, 1 - c)))
        return send, land

    def start(ins, outs, sems):
        for args in copies(ins, outs, sems)[0]:
            _copy(*args).start()

    def finish(ins, outs, sems):
        send, land = copies(ins, outs, sems)
        for args in land:
            _copy(*args).wait_recv()
        for args in send:
            _copy(*args).wait_send()

    return dict(ins=list(stacks), outs=[jax.ShapeDtypeStruct(s.shape, s.dtype) for s in stacks],
                sems=[pltpu.SemaphoreType.DMA((nt, 3)), pltpu.SemaphoreType.DMA((nt, 3))], start=start, finish=finish,
                aliases={t: t for t in range(nt)})


def _run_exchange(comm, name):
    ni, no = len(comm["ins"]), len(comm["outs"])

    def body(*refs):
        args = (refs[:ni], refs[ni:ni + no], refs[ni + no:])
        comm["start"](*args)
        comm["finish"](*args)

    return pl.pallas_call(
        body, name=name, in_specs=[ANY] * ni, out_specs=[ANY] * no, out_shape=comm["outs"], scratch_shapes=comm["sems"],
        input_output_aliases=dict(comm.get("aliases", {})))(*comm["ins"])


def _swap_halves(g4s):
    nt = len(g4s)

    def copies(ins, outs, sems):
        x, y, c, _, _ = _mesh_pos()
        both = []
        for t in range(nt):
            hr = ins[t].shape[1] // 2
            both.append((ins[t].at[:, pl.ds((1 - c) * hr, hr)], outs[t], sems, t, (x, y, 1 - c)))
        return both, both

    def start(ins, outs, sems):
        for args in copies(ins, outs, sems)[0]:
            _copy(*args).start()

    def finish(ins, outs, sems):
        send, land = copies(ins, outs, sems)
        for args in land:
            _copy(*args).wait_recv()
        for args in send:
            _copy(*args).wait_send()

    return dict(ins=list(g4s), outs=[jax.ShapeDtypeStruct((N_CHIPS, g.shape[1] // 2, g.shape[2]), g.dtype) for g in g4s],
                sems=[pltpu.SemaphoreType.DMA((nt,)), pltpu.SemaphoreType.DMA((nt,))], start=start, finish=finish)


def _exchange_chips(h4s):
    nt = len(h4s)

    def copies(ins, outs, sems):
        x, y, c, chip, others = _mesh_pos()
        send, land = [], []
        for t in range(nt):
            for j, (px, py) in enumerate(others):
                send.append((ins[t].at[2 * px + py], outs[t].at[chip], sems, (t, j), (px, py, c)))
                landed = outs[t].at[2 * px + py]
                land.append((landed, landed, sems, (t, j), (px, py, c)))
        return send, land

    def start(ins, outs, sems):
        for args in copies(ins, outs, sems)[0]:
            _copy(*args).start()

    def finish(ins, outs, sems):
        send, land = copies(ins, outs, sems)
        for args in land:
            _copy(*args).wait_recv()
        for args in send:
            _copy(*args).wait_send()

    return dict(ins=list(h4s), outs=[jax.ShapeDtypeStruct(h.shape, h.dtype) for h in h4s],
                sems=[pltpu.SemaphoreType.DMA((nt, 3)), pltpu.SemaphoreType.DMA((nt, 3))], start=start, finish=finish)


def _join_halves(fs):
    nt = len(fs)

    def copies(ins, outs, sems):
        x, y, c, _, _ = _mesh_pos()
        send, land = [], []
        for t in range(nt):
            hr = outs[t].shape[0] // 2
            mine, theirs = outs[t].at[pl.ds(c * hr, hr)], outs[t].at[pl.ds((1 - c) * hr, hr)]
            send.append((mine, mine, sems, t, (x, y, 1 - c)))
            land.append((theirs, theirs, sems, t, (x, y, 1 - c)))
        return send, land

    def start(ins, outs, sems):
        for args in copies(ins, outs, sems)[0]:
            _copy(*args).start()

    def finish(ins, outs, sems):
        send, land = copies(ins, outs, sems)
        for args in land:
            _copy(*args).wait_recv()
        for args in send:
            _copy(*args).wait_send()

    return dict(ins=list(fs), outs=[jax.ShapeDtypeStruct(f.shape, f.dtype) for f in fs],
                sems=[pltpu.SemaphoreType.DMA((nt,)), pltpu.SemaphoreType.DMA((nt,))], start=start, finish=finish,
                aliases={t: t for t in range(nt)})


def _pad_lanes(v, n=LANES):
    return jnp.pad(v, ((0, 0), (0, n - v.shape[-1])))


def _mlp_fwd(xin, norm_g, w_up, w_down, F, tag, comms=(None, None)):
    T, D = xin.shape
    h = _rms_fwd(xin, norm_g, bf16, f"{tag}_norm")

    def relu_sq(acc):
        r = jnp.maximum(acc, 0.0)
        return r, r * r

    got = [None, None]
    ua = _mm(h, w_up[0], "nn", T, F, D, (bf16, bf16), f"{tag}_up", epilogue=relu_sq, b_view=w_up[1], comm=comms[0])
    (u, a), got[0] = ua if comms[0] is not None else (ua, None)
    out = _mm(a, w_down[0], "nn", T, D, F, (f32,), f"{tag}_down", epilogue=lambda acc, res: (res + acc,),
              extras=((xin, "tile"),), b_view=w_down[1], comm=comms[1])
    (out,), got[1] = out if comms[1] is not None else ((out,), None)
    return out, (xin, h, u, a), got


def _mlp_bwd(dy, saved, norm_g, w_up, w_down, F, tag, up_to=None, down_to=None, host=None, dy_b=None, with_bf16=False):
    xin, h, u, a = saved
    T, D = xin.shape
    to = lambda t: {} if t is None else dict(out_view=t[0], out_stack=t[1], alias=t[2])
    dyo = dy if dy_b is None else dy_b
    du = _mm(dyo, w_down[0], "nt", T, F, D, (bf16,), f"{tag}_dact", epilogue=lambda acc, uu: (acc * (2.0 * uu.astype(f32)),),
             extras=((u, "tile"),), b_view=w_down[1])
    dw_down = _mm(a, dyo, "tn", F, D, T, (f32,), f"{tag}_dwdown", **to(down_to))
    dw_up = _mm(h, du, "tn", D, F, T, (f32,), f"{tag}_dwup", **to(up_to))
    dh = _mm(du, w_up[0], "nt", T, D, F, (f32,), f"{tag}_dh", b_view=w_up[1], comm=host(dw_up, dw_down) if host else None)
    (dh,), got = dh if host else ((dh,), None)
    dx, dg, *dxb = _rms_bwd(xin, norm_g, dh, dy, f"{tag}_dnorm", with_bf16=with_bf16)
    return dx, dg, dw_up, dw_down, got, (dxb[0] if with_bf16 else None)


def _local_step(xc, tgt, W, HS, SBW, net=None):
    T, D = xc.shape
    SW = HS * SSD_HEAD_DIM
    CD = W["conv_b"].shape[-1]
    mlp_norm = W["mlp_norm"]
    add = lambda acc, prev: (prev + acc,)

    h0 = W["h0"] if "h0" in W else _rms_fwd(xc, W["hyb_norm"], bf16, "hyb_norm")
    z = _mm(h0, W["w_z"], "nn", T, SW, D, (f32,), "proj_z")
    xraw = _mm(h0, W["w_xbc"], "nn", T, CD, D, (f32,), "proj_xbc")
    dtraw = _mm(h0, W["w_dt"], "nn", T, LANES, D, (f32,), "proj_dt")
    qkv = _mm(h0, W["w_qkv"], "nn", T, 3 * SBW, D, (f32,), "proj_qkv")
    qn, kn, vb = _qk_norm_fwd(qkv, W["q_norm"], W["k_norm"], SBW, "qk_norm")
    y_sb, ctot, got = _sb_fwd(qn, kn, vb, "sb_attn", comm=net.rest_ici() if net else None)
    xbc = _conv_fwd(xraw, W["conv_w"], W["conv_b"], "conv")
    y_ssd, yn_ssd, sprev, got = _ssd_fwd(xbc, dtraw, z, W["dt_bias"], W["a_log"], W["d_skip"], W["out_norm"], HS, "ssd",
                                         comm=net.rest_d2d(got) if net else None)
    if net:
        W = {**W, **net.rest_weights(got)}
    w_up, w_down, F = W["w_up"], W["w_down"], W["F"]
    mix = _mm(yn_ssd, W["w_out"], "nn", T, D, SW, (f32,), "out_ssd", epilogue=add, extras=((xc, "tile"),))
    x1 = _mm(y_sb, W["w_out"], "nn", T, D, SBW, (f32,), "out_sb", epilogue=add, extras=((mix, "tile"),), b_off=(SW, 0))
    x2, mlp0, got = _mlp_fwd(x1, mlp_norm[0:1], w_up[0], w_down[0], F, "mlp0",
                             comms=(net.last_ici(0), net.last_ici(1)) if net else (None, None))
    hp = _rms_fwd(x2, W["pool_norm"], f32, "pool_norm")
    x3, yp, dpool, got = _pool_fwd(hp, x2, W["w_pool"], W["pool_b"], W["pool_scale"], "pool",
                                   comm=net.last_d2d(got) if net else None)
    if net:
        w_up, w_down = net.last_weights(got, w_up, w_down)
    x4, mlp1, _ = _mlp_fwd(x3, mlp_norm[1:2], w_up[1], w_down[1], F, "mlp1")

    dy, dyb, sq = _loss_grad(x4, tgt, "loss")

    up_to, down_to = (net.mlp_to("up", 1, None), net.mlp_to("down", 1, None)) if net else (None, None)
    dx3, dg_mlp1, dw_up1, dw_down1, _, _ = _mlp_bwd(dy, mlp1, mlp_norm[1:2], w_up[1], w_down[1], F, "mlp1", up_to, down_to, dy_b=dyb)
    dhp, dw_pool, db_pool, dsc_pool = _pool_bwd(dx3, yp, dpool, W["w_pool"], W["pool_scale"], "pool_bwd")
    dx2, dg_pool, dx2b = _rms_bwd(x2, W["pool_norm"], dhp, dx3, "pool_dnorm", with_bf16=True)
    up_to, down_to = (net.mlp_to("up", 0, dw_up1), net.mlp_to("down", 0, dw_down1)) if net else (None, None)
    dx1, dg_mlp0, dw_up0, dw_down0, got_mlp, dx1b = _mlp_bwd(dx2, mlp0, mlp_norm[0:1], w_up[0], w_down[0], F, "mlp0", up_to, down_to,
                                                             host=net.swap_mlp if net else None, dy_b=dx2b, with_bf16=True)

    dw_out = jnp.concatenate([_mm(yn_ssd, dx1b, "tn", SW, D, T, (f32,), "dwout_ssd"),
                              _mm(y_sb, dx1b, "tn", SBW, D, T, (f32,), "dwout_sb")], axis=0)
    if net:
        (dmerged,), got = _mm(dx1b, W["w_out"], "nt", T, SW + SBW, D, (f32,), "dmerged",
                              comm=net.swap_rest(dw_out, dw_pool))
        dqn, dkn, dvv, got = _sb_bwd(qn, kn, vb, dmerged, ctot, SW, "sb_attn_bwd", comm=net.reduce_early(list(got) + list(got_mlp)))
        net.reduce_early_done(got)
    else:
        dmerged = _mm(dx1b, W["w_out"], "nt", T, SW + SBW, D, (f32,), "dmerged")
        dqn, dkn, dvv, _ = _sb_bwd(qn, kn, vb, dmerged, ctot, SW, "sb_attn_bwd")
    dqkv, dg_q, dg_k = _qk_norm_bwd(qkv, dqn, dkn, dvv, W["q_norm"], W["k_norm"], SBW, "qk_norm_bwd")
    dy_ssd, dz, dg_on = _gate_bwd(y_ssd, z, dmerged, W["out_norm"], "gate_bwd")
    dxbc, ddtraw, dalog, dbias, ddskip, got = _ssd_bwd(xbc, dtraw, sprev, dy_ssd, W["dt_bias"], W["a_log"], W["d_skip"], HS, "ssd_bwd",
                                                       comm=net.early_join() if net else None)
    if net:
        net.early_joined(got)
    dpre, dconv_w, dconv_b = _conv_bwd_pre(xraw, dxbc, W["conv_w"], W["conv_b"], "conv_bwd_pre")
    dxraw = _conv_bwd_in(dpre, W["conv_w"], "conv_bwd_in")
    dw_in = [_mm(h0, dz, "tn", D, SW, T, (f32,), "dwin_z"), _mm(h0, dxraw, "tn", D, CD, T, (f32,), "dwin_xbc"),
             _mm(h0, ddtraw, "tn", D, LANES, T, (f32,), "dwin_dt")[:, :HS], _mm(h0, dqkv, "tn", D, 3 * SBW, T, (f32,), "dwin_qkv")]
    dh0 = _mm(dz, W["w_z"], "nt", T, D, SW, (f32,), "dh0_z")
    if net:
        (dh0,), got = _mm(dxraw, W["w_xbc"], "nt", T, D, CD, (f32,), "dh0_xbc", epilogue=add, extras=((dh0, "tile"),),
                          comm=net.late_swap(dw_in))
        dh0 = _mm(ddtraw, W["w_dt"], "nt", T, D, LANES, (f32,), "dh0_dt", epilogue=add, extras=((dh0, "tile"),))
        (dh0,), got = _mm(dqkv, W["w_qkv"], "nt", T, D, 3 * SBW, (f32,), "dh0_qkv", epilogue=add, extras=((dh0, "tile"),),
                          comm=net.reduce_late(got))
        net.late_part_done(0, got)
        grad_x, dg_hyb, got = _rms_bwd(xc, W["hyb_norm"], dh0, dx1, "hyb_dnorm", comm=net.late_part(1))
        net.late_part_done(1, got)
    else:
        dh0 = _mm(dxraw, W["w_xbc"], "nt", T, D, CD, (f32,), "dh0_xbc", epilogue=add, extras=((dh0, "tile"),))
        dh0 = _mm(ddtraw, W["w_dt"], "nt", T, D, LANES, (f32,), "dh0_dt", epilogue=add, extras=((dh0, "tile"),))
        dh0 = _mm(dqkv, W["w_qkv"], "nt", T, D, 3 * SBW, (f32,), "dh0_qkv", epilogue=add, extras=((dh0, "tile"),))
        grad_x, dg_hyb = _rms_bwd(xc, W["hyb_norm"], dh0, dx1, "hyb_dnorm")
    grads = dict(w_in=dw_in, w_out=dw_out, w_pool=dw_pool, w_up=(dw_up0, dw_up1), w_down=(dw_down0, dw_down1),
                 hyb_norm=dg_hyb, conv_w=dconv_w, conv_b=dconv_b, dt_bias=dbias, a_log=dalog, d_skip=ddskip, out_norm=dg_on,
                 q_norm=dg_q, k_norm=dg_k, mlp_norm=(dg_mlp0, dg_mlp1), pool_norm=dg_pool, pool_b=db_pool, pool_scale=dsc_pool)
    return sq, grad_x, grads


def _stack_columns(pieces, n):
    cs = sum(p.shape[1] for p in pieces) // n
    slots = []
    for j in range(n):
        parts, off = [], 0
        for p in pieces:
            lo, hi = max(j * cs, off), min((j + 1) * cs, off + p.shape[1])
            if lo < hi:
                parts.append(p[:, lo - off:hi - off])
            off += p.shape[1]
        slots.append(parts[0] if len(parts) == 1 else jnp.concatenate(parts, axis=1))
    return jnp.stack(slots)


class _Net:
    def __init__(self, own_first, own_last, chip, dims):
        self.own, self.own_last, self.chip, self.dims = own_first, own_last, chip, dims

    def _place_own(self, stacks, own):
        return [lax.dynamic_update_index_in_dim(g, o, self.chip, 0) for g, o in zip(stacks, own)]

    def rest_ici(self):
        return _gather_ici(self.own)

    def rest_d2d(self, got):
        return _gather_d2d(list(got))

    def rest_weights(self, got):
        d, nw = self.dims, len(POOL_WINDOWS)
        D, F, PG = d["D"], d["F"], d["PG"]
        fs = F // N_CHIPS
        g_out, g_pool, g_up, g_down = self._place_own(got, self.own)
        w_pool = g_pool.reshape(N_CHIPS, nw, PG // N_CHIPS, PG).transpose(1, 0, 2, 3).reshape(nw, PG, PG)
        return dict(w_out=g_out.reshape(d["MIX"], D), w_pool=w_pool, F=F,
                    w_up=[(g_up, ("cols", fs, 0, D))], w_down=[(g_down, ("rows", fs, 0, None))])

    def last_ici(self, which):
        return _gather_ici([self.own_last[which]])

    def last_d2d(self, got):
        return _gather_d2d([got[0][0], got[1][0]])

    def last_weights(self, stacks, w_up, w_down):
        d = self.dims
        fs = d["F"] // N_CHIPS
        g_up, g_down = self._place_own(stacks, self.own_last)
        return w_up + [(g_up, ("cols", fs, 0, d["D"]))], w_down + [(g_down, ("rows", fs, 0, None))]

    def mlp_to(self, which, layer, earlier):
        d = self.dims
        fs = d["F"] // N_CHIPS
        if which == "up":
            return ("cols", fs, layer, d["D"]), (N_CHIPS, d["NL"] * d["D"], fs), earlier
        return ("rows", fs, layer, None), (N_CHIPS, d["NL"] * fs, d["D"]), earlier

    def swap_mlp(self, g_up, g_down):
        self.g_mlp = [g_up, g_down]
        return _swap_halves(self.g_mlp)

    def swap_rest(self, dw_out, dw_pool):
        d, nw = self.dims, len(POOL_WINDOWS)
        PG = d["PG"]
        self.early_g4 = [dw_out.reshape(N_CHIPS, d["MIX"] // N_CHIPS, d["D"]),
                         dw_pool.reshape(nw, N_CHIPS, PG // N_CHIPS, PG).transpose(1, 0, 2, 3).reshape(N_CHIPS, PG, PG)] + self.g_mlp
        return _swap_halves(self.early_g4[:2])

    def reduce_early(self, recv):
        self.early_recv = list(recv)
        sent = [_pair_sum(g, r, f"grads_early_pair_sum{i}") for i, (g, r) in enumerate(zip(self.early_g4, self.early_recv))]
        return _exchange_chips(sent)

    def reduce_early_done(self, got):
        self.early_got = list(got)

    def late_swap(self, dw_in):
        self.late_g4 = [_stack_columns(dw_in, N_CHIPS)]
        return _swap_halves(self.late_g4)

    def reduce_late(self, recv):
        self.late_recv = list(recv)
        sent = _pair_sum(self.late_g4[0], self.late_recv[0], "grads_late_pair_sum")
        cut = 5 * sent.shape[1] // 8
        self.late_parts = [sent[:, :cut], sent[:, cut:]]
        self.late_got = [None] * 2
        return self.late_part(0)

    def late_part(self, i):
        return _exchange_chips([self.late_parts[i]])

    def late_part_done(self, i, got):
        self.late_got[i] = got[0]

    def _halves(self, g4s, recvs, pieces, tag):
        return [_chip_sum(g, r, p, f"grads_{tag}_chip_sum{i}") for i, (g, r, p) in enumerate(zip(g4s, recvs, pieces))]

    def early_join(self):
        return _join_halves(self._halves(self.early_g4, self.early_recv, self.early_got, "early"))

    def early_joined(self, got):
        self.early_done = list(got)

    def reduced_early(self):
        return self.early_done

    def reduced_late(self):
        halves = self._halves(self.late_g4, self.late_recv, [jnp.concatenate(self.late_got, axis=1)], "late")
        return _run_exchange(_join_halves(halves), "grads_late_join")[0]


def kernel(x, hyb_norm, hyb_w_in, ssd_conv_w, ssd_conv_b, ssd_dt_bias, ssd_a_log, ssd_d, ssd_out_norm, sb_q_norm, sb_k_norm, hyb_w_out, pool_norm, pool_w, pool_b, pool_scale, mlp_norm, mlp_w_up, mlp_w_down, loss_target, m_hyb_norm, m_hyb_w_in, m_ssd_conv_w, m_ssd_conv_b, m_ssd_dt_bias, m_ssd_a_log, m_ssd_d, m_ssd_out_norm, m_sb_q_norm, m_sb_k_norm, m_hyb_w_out, m_pool_norm, m_pool_w, m_pool_b, m_pool_scale, m_mlp_norm, m_mlp_w_up, m_mlp_w_down, v_hyb_norm, v_hyb_w_in, v_ssd_conv_w, v_ssd_conv_b, v_ssd_dt_bias, v_ssd_a_log, v_ssd_d, v_ssd_out_norm, v_sb_q_norm, v_sb_k_norm, v_hyb_w_out, v_pool_norm, v_pool_w, v_pool_b, v_pool_scale, v_mlp_norm, v_mlp_w_up, v_mlp_w_down):
    T, D = x.shape[1], x.shape[2]
    HS = ssd_dt_bias.shape[-1]
    SW = HS * SSD_HEAD_DIM
    CD = ssd_conv_b.shape[-1]
    IN = N_CHIPS * hyb_w_in.shape[-1]
    SBW = (IN - SW - CD - HS) // 3
    F = N_CHIPS * mlp_w_up.shape[-1]
    NL = mlp_norm.shape[0]
    PG = D // len(POOL_WINDOWS)
    xc, tgt = x[0], loss_target[0]
    ix, iy, ic = lax.axis_index("x"), lax.axis_index("y"), lax.axis_index("c")
    chip = (2 * ix + iy).astype(jnp.int32)

    small = jnp.concatenate([ssd_conv_w.reshape(-1), pool_norm.reshape(-1), pool_b.reshape(-1), pool_scale.reshape(-1)])
    ns = small.shape[0]
    ns8 = -(-ns // (8 * LANES)) * LANES
    gathered, _ = _gather_small(jnp.pad(small, (0, 8 * ns8 - ns)).reshape(8, ns8), "gather_small")
    per_chip = gathered.reshape(N_CHIPS, 2, 8 * ns8)[:, 0, :ns]
    cw = CD // N_CHIPS
    conv_w = per_chip[:, :4 * cw].reshape(N_CHIPS, 4, cw).transpose(1, 0, 2).reshape(4, CD)
    pvec = per_chip[:, 4 * cw:].reshape(N_CHIPS, 3, PG)
    pool_norm_f, pool_b_f, pool_scale_f = (pvec[:, i].reshape(1, D) for i in range(3))

    fs = F // N_CHIPS
    own_in = hyb_w_in[0].astype(bf16)
    arrived = _run_exchange(_gather_ici([own_in]), "gather_in_ici")
    h0, (g_in,) = _rms_fwd(xc, hyb_norm, bf16, "hyb_norm", comm=_gather_d2d(arrived))
    w_in = lax.dynamic_update_index_in_dim(g_in, own_in, chip, 0).transpose(1, 0, 2).reshape(D, IN)
    c1, c2, c3 = SW, SW + CD, SW + CD + HS
    w_z, w_xbc, w_dt, w_qkv = w_in[:, :c1], w_in[:, c1:c2], _pad_lanes(w_in[:, c2:c3]), w_in[:, c3:]
    dt_bias_p, a_log_p, d_skip_p = _pad_lanes(ssd_dt_bias), _pad_lanes(ssd_a_log), jnp.repeat(ssd_d, SSD_HEAD_DIM, axis=-1)

    assert NL == 2
    own_first = [hyb_w_out[0].astype(bf16), pool_w[0].reshape(-1, PG).astype(bf16),
                 mlp_w_up[0].astype(bf16), mlp_w_down[0].astype(bf16)]
    own_last = [mlp_w_up[1].astype(bf16), mlp_w_down[1].astype(bf16)]
    net = _Net(own_first, own_last, chip, dict(D=D, F=F, NL=NL, PG=PG, IN=IN, MIX=SW + SBW))
    first = dict(h0=h0, hyb_norm=hyb_norm, w_z=w_z, w_xbc=w_xbc, w_dt=w_dt, w_qkv=w_qkv, conv_w=conv_w, conv_b=ssd_conv_b,
                 dt_bias=dt_bias_p, a_log=a_log_p, d_skip=d_skip_p, out_norm=ssd_out_norm, q_norm=sb_q_norm, k_norm=sb_k_norm,
                 pool_norm=pool_norm_f, pool_b=pool_b_f, pool_scale=pool_scale_f, mlp_norm=mlp_norm)
    sq, grad_x, gr = _local_step(xc, tgt, first, HS, SBW, net)
    loss = lax.psum(sq[0, 0] * (0.5 / D), ("x", "y", "c"))
    dg_hyb, dconv_b, dbias, dalog, ddskip, dg_on, dg_q, dg_k = (gr[k] for k in (
        "hyb_norm", "conv_b", "dt_bias", "a_log", "d_skip", "out_norm", "q_norm", "k_norm"))
    (dg_mlp0, dg_mlp1), dconv_w, dg_pool, db_pool, dsc_pool = gr["mlp_norm"], gr["conv_w"], gr["pool_norm"], gr["pool_b"], gr["pool_scale"]
    gb_out, gb_pool, gb_up, gb_down = net.reduced_early()

    full_small = [dg_hyb, dconv_b, dbias[:, :HS], dalog[:, :HS], ddskip[:, :HS], dg_on, dg_q, dg_k,
                  jnp.concatenate([dg_mlp0, dg_mlp1], axis=0).reshape(1, -1),
                  dconv_w.reshape(1, -1), dg_pool, db_pool, dsc_pool]
    sizes = [v.shape[-1] for v in full_small]
    packed = jnp.concatenate([v.reshape(-1) for v in full_small])
    npk = packed.shape[0]
    npk8 = -(-npk // (8 * LANES)) * LANES
    _, summed = _gather_small(jnp.pad(packed, (0, 8 * npk8 - npk)).reshape(8, npk8), "grads_small")
    summed = summed.reshape(-1)[:npk]
    offs = [0]
    for s in sizes:
        offs.append(offs[-1] + s)
    (g_hyb_norm, g_conv_b, g_dt_bias, g_a_log, g_d, g_out_norm, g_q_norm, g_k_norm, g_mlp_norm, g_conv_w_full,
     g_pool_norm_full, g_pool_b_full, g_pool_scale_full) = (summed[offs[i]:offs[i + 1]] for i in range(len(sizes)))
    take = lambda full, n: lax.dynamic_slice_in_dim(full.reshape(-1, N_CHIPS, n), chip, 1, axis=1)
    small_grads = {
        "hyb_norm": g_hyb_norm.reshape(hyb_norm.shape), "ssd_conv_w": take(g_conv_w_full, cw).reshape(ssd_conv_w.shape),
        "ssd_conv_b": g_conv_b.reshape(ssd_conv_b.shape), "ssd_dt_bias": g_dt_bias.reshape(ssd_dt_bias.shape),
        "ssd_a_log": g_a_log.reshape(ssd_a_log.shape), "ssd_d": g_d.reshape(ssd_d.shape),
        "ssd_out_norm": g_out_norm.reshape(ssd_out_norm.shape), "sb_q_norm": g_q_norm.reshape(sb_q_norm.shape),
        "sb_k_norm": g_k_norm.reshape(sb_k_norm.shape), "pool_norm": take(g_pool_norm_full, PG).reshape(pool_norm.shape),
        "pool_b": take(g_pool_b_full, PG).reshape(pool_b.shape), "pool_scale": take(g_pool_scale_full, PG).reshape(pool_scale.shape),
        "mlp_norm": g_mlp_norm.reshape(mlp_norm.shape),
    }

    weights = dict(hyb_norm=hyb_norm, hyb_w_in=hyb_w_in, ssd_conv_w=ssd_conv_w, ssd_conv_b=ssd_conv_b, ssd_dt_bias=ssd_dt_bias,
                   ssd_a_log=ssd_a_log, ssd_d=ssd_d, ssd_out_norm=ssd_out_norm, sb_q_norm=sb_q_norm, sb_k_norm=sb_k_norm,
                   hyb_w_out=hyb_w_out, pool_norm=pool_norm, pool_w=pool_w, pool_b=pool_b, pool_scale=pool_scale,
                   mlp_norm=mlp_norm, mlp_w_up=mlp_w_up, mlp_w_down=mlp_w_down)
    moms = dict(hyb_norm=m_hyb_norm, hyb_w_in=m_hyb_w_in, ssd_conv_w=m_ssd_conv_w, ssd_conv_b=m_ssd_conv_b, ssd_dt_bias=m_ssd_dt_bias,
                ssd_a_log=m_ssd_a_log, ssd_d=m_ssd_d, ssd_out_norm=m_ssd_out_norm, sb_q_norm=m_sb_q_norm, sb_k_norm=m_sb_k_norm,
                hyb_w_out=m_hyb_w_out, pool_norm=m_pool_norm, pool_w=m_pool_w, pool_b=m_pool_b, pool_scale=m_pool_scale,
                mlp_norm=m_mlp_norm, mlp_w_up=m_mlp_w_up, mlp_w_down=m_mlp_w_down)
    vels = dict(hyb_norm=v_hyb_norm, hyb_w_in=v_hyb_w_in, ssd_conv_w=v_ssd_conv_w, ssd_conv_b=v_ssd_conv_b, ssd_dt_bias=v_ssd_dt_bias,
                ssd_a_log=v_ssd_a_log, ssd_d=v_ssd_d, ssd_out_norm=v_ssd_out_norm, sb_q_norm=v_sb_q_norm, sb_k_norm=v_sb_k_norm,
                hyb_w_out=v_hyb_w_out, pool_norm=v_pool_norm, pool_w=v_pool_w, pool_b=v_pool_b, pool_scale=v_pool_scale,
                mlp_norm=v_mlp_norm, mlp_w_up=v_mlp_w_up, mlp_w_down=v_mlp_w_down)
    order = list(weights)
    grads, delta, new_m, new_v = {}, {}, {}, {}
    for name, g2 in (("hyb_w_out", gb_out), ("pool_w", gb_pool), ("mlp_w_up", gb_up), ("mlp_w_down", gb_down),
                     ("hyb_w_in", net.reduced_late())):
        shp = weights[name].shape
        d_, m_, v_, g_, _ = _adamw(weights[name].reshape(g2.shape), g2, moms[name].reshape(g2.shape), vels[name].reshape(g2.shape),
                                   f"adamw_{name}")
        grads[name], delta[name], new_m[name], new_v[name] = (t.reshape(shp) for t in (g_, d_, m_, v_))
    snames = list(small_grads)
    pack = lambda d: jnp.concatenate([d[n].reshape(-1) for n in snames])
    nsm = sum(small_grads[n].size for n in snames)
    cols = -(-nsm // (8 * LANES)) * LANES
    as_blk = lambda v: jnp.pad(v, (0, 8 * cols - nsm)).reshape(8, cols)
    padded_v = jnp.pad(pack(vels), (0, 8 * cols - nsm), constant_values=1.0).reshape(8, cols)
    d_, m_, v_, _, _ = _adamw(as_blk(pack(weights)), as_blk(pack(small_grads)), as_blk(pack(moms)), padded_v, "adamw_small")
    off = 0
    for n in snames:
        sz, shp = small_grads[n].size, weights[n].shape
        grads[n] = small_grads[n]
        delta[n], new_m[n], new_v[n] = (t.reshape(-1)[off:off + sz].reshape(shp) for t in (d_, m_, v_))
        off += sz

    return (loss, grad_x.reshape(x.shape), *[grads[n] for n in order], *[delta[n] for n in order],
            *[new_m[n] for n in order], *[new_v[n] for n in order])
```
